```python
import math
import jax, jax.numpy as jnp
from jax import lax
import numpy as np

D_MODEL = 1024
BATCH = 8
SEQ = 8192
DEPTH = 1

CHUNK = 64
MIX_WIDTH = D_MODEL
POOL_WIDTH = MIX_WIDTH // 2
ATTN_WIDTH = MIX_WIDTH - POOL_WIDTH
POOL_WINDOWS = (2, 4, 8, 16)
N_POOL_GROUPS = len(POOL_WINDOWS)
POOL_GROUP_DIM = POOL_WIDTH // N_POOL_GROUPS
HEAD_DIM = 64
N_HEADS = ATTN_WIDTH // HEAD_DIM
LEFT_CHUNKS = 8
BAND = (LEFT_CHUNKS + 1) * CHUNK
MAX_REL = 64
N_REL = 2 * MAX_REL + 1
IN_PROJ_WIDTH = 2 * POOL_WIDTH + 4 * ATTN_WIDTH
EPS = 1e-6
MASK_VALUE = -1e30

kernel_name = "hybrid_pool_chunkattn_block"


def rms_norm(x, g):
    xf = x.astype(jnp.float32)
    y = xf * lax.rsqrt(jnp.mean(xf * xf, axis=-1, keepdims=True) + EPS)
    return (y * g.astype(jnp.float32)).astype(x.dtype)


def multiscale_pool(v, pool_w, pool_scale):
    S = v.shape[1]
    vf = v.astype(jnp.float32)
    cs = jnp.pad(jnp.cumsum(vf, axis=1), ((0, 0), (1, 0), (0, 0)))
    t = jnp.arange(S)
    diffs = []
    for gi, w in enumerate(POOL_WINDOWS):
        sl = slice(gi * POOL_GROUP_DIM, (gi + 1) * POOL_GROUP_DIM)
        cs_g = cs[..., sl]
        lo = jnp.maximum(t + 1 - w, 0)
        win_sum = cs_g[:, 1:] - jnp.take(cs_g, lo, axis=1)
        count = (t + 1 - lo).astype(jnp.float32)[None, :, None]
        diffs.append(win_sum / count - vf[..., sl])
    d = jnp.stack(diffs, axis=2)
    y = jnp.einsum('bsgc,gcd->bsgd', d, pool_w.astype(jnp.float32))
    y = y.reshape(y.shape[0], S, POOL_WIDTH) * pool_scale.astype(jnp.float32)
    return y.astype(v.dtype)


def chunked_rel_attention(q, k, v, rel_bias):
    B, S, H, Dh = q.shape
    n_chunks = S // CHUNK
    pad = LEFT_CHUNKS * CHUNK
    kp = jnp.pad(k, ((0, 0), (pad, 0), (0, 0), (0, 0)))
    vp = jnp.pad(v, ((0, 0), (pad, 0), (0, 0), (0, 0)))
    qc = jnp.moveaxis(q.reshape(B, n_chunks, CHUNK, H, Dh), 1, 0)
    i = jnp.arange(CHUNK)
    j = jnp.arange(BAND)
    rel = j[None, :] - pad - i[:, None]
    rel_idx = jnp.clip(rel, -MAX_REL, MAX_REL) + MAX_REL
    bias = rel_bias.astype(jnp.float32)[:, rel_idx]
    scale = 1.0 / math.sqrt(Dh)

    def one_chunk(args):
        c, qb = args
        start = c * CHUNK
        kb = lax.dynamic_slice_in_dim(kp, start, BAND, axis=1)
        vb = lax.dynamic_slice_in_dim(vp, start, BAND, axis=1)
        s = jnp.einsum('bqhd,bkhd->bhqk', qb, kb).astype(jnp.float32) * scale + bias[None]
        valid = (start + j - pad) >= 0
        s = jnp.where(valid[None, None, None, :], s, MASK_VALUE)
        p = jax.nn.softmax(s, axis=-1)
        return jnp.einsum('bhqk,bkhd->bqhd', p.astype(vb.dtype), vb)

    out = lax.map(one_chunk, (jnp.arange(n_chunks), qc))
    return jnp.moveaxis(out, 0, 1).reshape(B, S, H * Dh)


def _fwd_setup_inputs(seed: int = 0) -> dict:
    key = jax.random.key(seed)
    ks = jax.random.split(key, 9)
    x = jax.random.normal(ks[0], (BATCH, SEQ, D_MODEL), jnp.float32)
    norm_gain = 1.0 + 0.02 * jax.random.normal(ks[1], (DEPTH, D_MODEL), jnp.float32)
    w_in = jax.random.normal(ks[2], (DEPTH, D_MODEL, IN_PROJ_WIDTH), jnp.float32) * D_MODEL ** -0.5
    pool_w = jax.random.normal(ks[3], (DEPTH, N_POOL_GROUPS, POOL_GROUP_DIM, POOL_GROUP_DIM), jnp.float32) * POOL_GROUP_DIM ** -0.5
    pool_scale = 1.0 + 0.02 * jax.random.normal(ks[4], (DEPTH, POOL_WIDTH), jnp.float32)
    rel_bias = 0.5 * jax.random.normal(ks[5], (DEPTH, N_HEADS, N_REL), jnp.float32)
    w_out = jax.random.normal(ks[6], (DEPTH, MIX_WIDTH, D_MODEL), jnp.float32) * MIX_WIDTH ** -0.5
    final_norm_gain = 1.0 + 0.02 * jax.random.normal(ks[7], (D_MODEL,), jnp.float32)
    return {"x": x, "norm_gain": norm_gain, "w_in": w_in, "pool_w": pool_w,
            "pool_scale": pool_scale, "rel_bias": rel_bias, "w_out": w_out,
            "final_norm_gain": final_norm_gain}


def _fwd_reference(x, norm_gain, w_in, pool_w, pool_scale, rel_bias, w_out, final_norm_gain):
    B, S, _ = x.shape
    for layer in range(DEPTH):
        h = rms_norm(x, norm_gain[layer])
        proj = jnp.einsum('bsd,de->bse', h, w_in[layer])
        o = 0
        pool_v = proj[..., o:o + POOL_WIDTH]; o += POOL_WIDTH
        pool_g = proj[..., o:o + POOL_WIDTH]; o += POOL_WIDTH
        q = proj[..., o:o + ATTN_WIDTH]; o += ATTN_WIDTH
        k = proj[..., o:o + ATTN_WIDTH]; o += ATTN_WIDTH
        v = proj[..., o:o + ATTN_WIDTH]; o += ATTN_WIDTH
        attn_g = proj[..., o:o + ATTN_WIDTH]
        y_pool = multiscale_pool(pool_v, pool_w[layer], pool_scale[layer]) * jax.nn.silu(pool_g)
        qh = q.reshape(B, S, N_HEADS, HEAD_DIM)
        kh = k.reshape(B, S, N_HEADS, HEAD_DIM)
        vh = v.reshape(B, S, N_HEADS, HEAD_DIM)
        y_attn = chunked_rel_attention(qh, kh, vh, rel_bias[layer]) * jax.nn.silu(attn_g)
        y = jnp.concatenate([y_pool, y_attn], axis=-1)
        x = x + jnp.einsum('bse,ed->bsd', y, w_out[layer])
    return rms_norm(x, final_norm_gain)


import jax as _jax
import jax.numpy as _jnp

TWIN_FORMAT = 'train_step'
FWD_PARAMS = ['x', 'norm_gain', 'w_in', 'pool_w', 'pool_scale', 'rel_bias', 'w_out', 'final_norm_gain']
TWIN_WEIGHTS = ['norm_gain', 'w_in', 'pool_w', 'pool_scale', 'rel_bias', 'w_out', 'final_norm_gain']
TWIN_DIFF_INPUT = 'x'
TWIN_INPUTS = ['x', 'norm_gain', 'w_in', 'pool_w', 'pool_scale', 'rel_bias', 'w_out', 'final_norm_gain', 'loss_target', 'm_norm_gain', 'm_w_in', 'm_pool_w', 'm_pool_scale', 'm_rel_bias', 'm_w_out', 'm_final_norm_gain', 'v_norm_gain', 'v_w_in', 'v_pool_w', 'v_pool_scale', 'v_rel_bias', 'v_w_out', 'v_final_norm_gain']
TWIN_OUTPUTS = ['loss', 'grad_x', 'grad_norm_gain', 'grad_w_in', 'grad_pool_w', 'grad_pool_scale', 'grad_rel_bias', 'grad_w_out', 'grad_final_norm_gain', 'delta_norm_gain', 'delta_w_in', 'delta_pool_w', 'delta_pool_scale', 'delta_rel_bias', 'delta_w_out', 'delta_final_norm_gain', 'new_m_norm_gain', 'new_m_w_in', 'new_m_pool_w', 'new_m_pool_scale', 'new_m_rel_bias', 'new_m_w_out', 'new_m_final_norm_gain', 'new_v_norm_gain', 'new_v_w_in', 'new_v_pool_w', 'new_v_pool_scale', 'new_v_rel_bias', 'new_v_w_out', 'new_v_final_norm_gain']
TWIN_LEAF_KINDS = {'loss': 'loss', 'grad_x': 'grad_x', 'grad_norm_gain': 'grad_w', 'grad_w_in': 'grad_w', 'grad_pool_w': 'grad_w', 'grad_pool_scale': 'grad_w', 'grad_rel_bias': 'grad_w', 'grad_w_out': 'grad_w', 'grad_final_norm_gain': 'grad_w', 'delta_norm_gain': 'delta_w', 'delta_w_in': 'delta_w', 'delta_pool_w': 'delta_w', 'delta_pool_scale': 'delta_w', 'delta_rel_bias': 'delta_w', 'delta_w_out': 'delta_w', 'delta_final_norm_gain': 'delta_w', 'new_m_norm_gain': 'new_m', 'new_m_w_in': 'new_m', 'new_m_pool_w': 'new_m', 'new_m_pool_scale': 'new_m', 'new_m_rel_bias': 'new_m', 'new_m_w_out': 'new_m', 'new_m_final_norm_gain': 'new_m', 'new_v_norm_gain': 'new_v', 'new_v_w_in': 'new_v', 'new_v_pool_w': 'new_v', 'new_v_pool_scale': 'new_v', 'new_v_rel_bias': 'new_v', 'new_v_w_out': 'new_v', 'new_v_final_norm_gain': 'new_v'}


def _forward(args):
    return _fwd_reference(*[args[k] for k in FWD_PARAMS])


def _output_shape():
    out = _jax.eval_shape(lambda: _forward(_fwd_setup_inputs(0)))
    return out.shape, out.dtype

N_MICROBATCH = 1
ADAM_LR = 0.001
ADAM_B1 = 0.9
ADAM_B2 = 0.999
ADAM_EPS = 1e-08
ADAM_WD = 0.01
ADAM_STEP = 10
PER_EXAMPLE_BATCH_AXIS = {'x': 0, 'loss_target': 0}
SHARED_INPUTS = []
_WEIGHT_DTYPES = {'norm_gain': _jnp.float32, 'w_in': _jnp.float32, 'pool_w': _jnp.float32, 'pool_scale': _jnp.float32, 'rel_bias': _jnp.float32, 'w_out': _jnp.float32, 'final_norm_gain': _jnp.float32}
MOMENT_SCALE = {'norm_gain': 1.290056e-01, 'w_in': 7.335656e-02, 'pool_w': 1.230765e-01, 'pool_scale': 1.225298e-01, 'rel_bias': 1.439884e-02, 'w_out': 8.709044e-02, 'final_norm_gain': 6.388162e+01}


def _to_microbatches(a, axis):
    t = _jnp.moveaxis(a, axis, 0)
    t = t.reshape((N_MICROBATCH, t.shape[0] // N_MICROBATCH) + t.shape[1:])
    return _jnp.moveaxis(t, 1, axis + 1)


def setup_inputs(seed: int = 0) -> dict:
    inp = _fwd_setup_inputs(seed)
    key = _jax.random.fold_in(_jax.random.key(seed), 7919)
    shape, _ = _output_shape()
    out = dict(inp)
    out["loss_target"] = _jax.random.normal(_jax.random.fold_in(key, 0), shape, _jnp.float32)
    for i, name in enumerate(TWIN_WEIGHTS):
        w = inp[name].astype(_jnp.float32)
        if MOMENT_SCALE is None:
            s = _jnp.sqrt(_jnp.mean(_jnp.square(w)) + 1e-30)
        else:
            s = MOMENT_SCALE[name]
        km, kv = _jax.random.split(_jax.random.fold_in(key, i + 1))
        out[name] = w
        out["m_" + name] = s * _jax.random.normal(km, w.shape, _jnp.float32)
        out["v_" + name] = (s * s) * _jax.random.uniform(kv, w.shape, _jnp.float32, 0.5, 1.5)
    if N_MICROBATCH > 1:
        for name, axis in PER_EXAMPLE_BATCH_AXIS.items():
            out[name] = _to_microbatches(out[name], axis)
    return {'x': out['x'], 'norm_gain': out['norm_gain'], 'w_in': out['w_in'], 'pool_w': out['pool_w'], 'pool_scale': out['pool_scale'], 'rel_bias': out['rel_bias'], 'w_out': out['w_out'], 'final_norm_gain': out['final_norm_gain'], 'loss_target': out['loss_target'], 'm_norm_gain': out['m_norm_gain'], 'm_w_in': out['m_w_in'], 'm_pool_w': out['m_pool_w'], 'm_pool_scale': out['m_pool_scale'], 'm_rel_bias': out['m_rel_bias'], 'm_w_out': out['m_w_out'], 'm_final_norm_gain': out['m_final_norm_gain'], 'v_norm_gain': out['v_norm_gain'], 'v_w_in': out['v_w_in'], 'v_pool_w': out['v_pool_w'], 'v_pool_scale': out['v_pool_scale'], 'v_rel_bias': out['v_rel_bias'], 'v_w_out': out['v_w_out'], 'v_final_norm_gain': out['v_final_norm_gain']}


def _loss(weights, diff, rest, loss_target):
    with _jax.named_scope("forward"):
        args = {**rest, TWIN_DIFF_INPUT: diff, **{k: w.astype(_WEIGHT_DTYPES[k]) for k, w in weights.items()}}
        y = _forward(args)
    with _jax.named_scope("loss_head"):
        err = _jnp.square(y.astype(_jnp.float32) - loss_target)
        return 0.5 * _jnp.sum(_jnp.mean(err, axis=-1)) if err.ndim else 0.5 * err


def _adamw(w, g, m, v):
    m = ADAM_B1 * m + (1.0 - ADAM_B1) * g
    v = ADAM_B2 * v + (1.0 - ADAM_B2) * _jnp.square(g)
    m_hat = m / (1.0 - ADAM_B1 ** ADAM_STEP)
    v_hat = v / (1.0 - ADAM_B2 ** ADAM_STEP)
    delta = -ADAM_LR * (m_hat / (_jnp.sqrt(v_hat) + ADAM_EPS) + ADAM_WD * w)
    return delta, m, v


def reference(x, norm_gain, w_in, pool_w, pool_scale, rel_bias, w_out, final_norm_gain, loss_target, m_norm_gain, m_w_in, m_pool_w, m_pool_scale, m_rel_bias, m_w_out, m_final_norm_gain, v_norm_gain, v_w_in, v_pool_w, v_pool_scale, v_rel_bias, v_w_out, v_final_norm_gain):
    given = dict(x=x, norm_gain=norm_gain, w_in=w_in, pool_w=pool_w, pool_scale=pool_scale, rel_bias=rel_bias, w_out=w_out, final_norm_gain=final_norm_gain, loss_target=loss_target, m_norm_gain=m_norm_gain, m_w_in=m_w_in, m_pool_w=m_pool_w, m_pool_scale=m_pool_scale, m_rel_bias=m_rel_bias, m_w_out=m_w_out, m_final_norm_gain=m_final_norm_gain, v_norm_gain=v_norm_gain, v_w_in=v_w_in, v_pool_w=v_pool_w, v_pool_scale=v_pool_scale, v_rel_bias=v_rel_bias, v_w_out=v_w_out, v_final_norm_gain=v_final_norm_gain)
    weights = {n: given[n] for n in TWIN_WEIGHTS}
    shared = {n: given[n] for n in SHARED_INPUTS}
    per_example = {n: given[n] for n in ['x']}
    grad_fn = _jax.value_and_grad(_loss, argnums=(0, 1))

    def one_microbatch(ex, loss_target):
        ex = dict(ex)
        diff = ex.pop(TWIN_DIFF_INPUT)
        return grad_fn(weights, diff, {**shared, **ex}, loss_target)

    if N_MICROBATCH == 1:
        loss, (grad_w, grad_x) = one_microbatch(per_example, given["loss_target"])
    else:
        def body(carry, xs):
            loss_sum, grad_sum = carry
            l_k, (gw_k, gx_k) = one_microbatch(xs[0], xs[1])
            with _jax.named_scope("update"):
                return (loss_sum + l_k, _jax.tree.map(_jnp.add, grad_sum, gw_k)), gx_k

        init = (_jnp.zeros((), _jnp.float32), _jax.tree.map(_jnp.zeros_like, weights))
        (loss, grad_w), grad_x = _jax.lax.scan(body, init, (per_example, given["loss_target"]))
    with _jax.named_scope("update"):
        delta_w, new_m, new_v = {}, {}, {}
        for n in TWIN_WEIGHTS:
            delta_w[n], new_m[n], new_v[n] = _adamw(weights[n], grad_w[n], given["m_" + n], given["v_" + n])
    return (loss, grad_x, *[grad_w[n] for n in TWIN_WEIGHTS], *[delta_w[n] for n in TWIN_WEIGHTS],
            *[new_m[n] for n in TWIN_WEIGHTS], *[new_v[n] for n in TWIN_WEIGHTS])
```

```python
import numpy as np
import jax
import jax.numpy as jnp
from jax import lax
from jax.experimental import pallas as pl
from jax.experimental.pallas import tpu as pltpu

F32 = jnp.float32
BF16 = jnp.bfloat16

D_MODEL = 1024
POOL_WIDTH = 512
ATTN_WIDTH = 512
POOL_WINDOWS = (2, 4, 8, 16)
GROUP = 128
CHUNK = 64
LEFT_CHUNKS = 8
BAND = (LEFT_CHUNKS + 1) * CHUNK
HEAD_DIM = 64
N_PAIR = 4
MAX_REL = 64
N_REL = 2 * MAX_REL + 1
EPS = 1e-6
MASK_VALUE = -1e30
SCALE = 0.125

ADAM_LR = 0.001
ADAM_B1 = 0.9
ADAM_B2 = 0.999
ADAM_EPS = 1e-08
ADAM_WD = 0.01
ADAM_STEP = 10

TS = LEFT_CHUNKS * CHUNK
CHUNKS_PER_TILE = TS // CHUNK
HALO = 16
N_CHIP = 4
SHARD_IN = 768
SHARD_OUT = 256
PIECE = 256
PIECES_PER_SHARD = SHARD_IN // PIECE
HALF_IN = D_MODEL // 2
HALF_OUT = SHARD_OUT // 2
SMALL_ROWS = 560
LOSS_ROW = 552
VMEM_LIMIT = 56 * 1024 * 1024

MESH = pl.DeviceIdType.MESH
ANY = pl.BlockSpec(memory_space=pl.ANY)
VMEM = pl.BlockSpec(memory_space=pltpu.VMEM)


def _sigmoid(x):
    return 1.0 / (1.0 + jnp.exp(-x))


def _dot(a, b):
    return jnp.dot(a, b, preferred_element_type=F32)


def _dot_t(a, b):
    return lax.dot_general(a, b, (((1,), (1,)), ((), ())), preferred_element_type=F32)


def _tdot(a, b):
    return lax.dot_general(a, b, (((0,), (0,)), ((), ())), preferred_element_type=F32)


def _my_place():
    return lax.axis_index("x"), lax.axis_index("y"), lax.axis_index("c")


def _other_chips(x, y):
    places = [(1 - x, y), (x, 1 - y), (1 - x, 1 - y)]
    return [(p, 2 * p[0] + p[1]) for p in places]


def _gather_weights(win_sh, wout_sh):
    def body(win_ref, wout_ref, win_full, wout_full, send_sems, recv_sems, local_sems):
        x, y, c = _my_place()
        b = 2 * x + y
        sibling = (x, y, 1 - c)

        def halves(chip, core):
            return (
                win_full.at[chip, pl.ds(core * HALF_IN, HALF_IN)],
                wout_full.at[chip, pl.ds(core * HALF_OUT, HALF_OUT)],
            )

        def copy(k, src, dst, to):
            return pltpu.make_async_remote_copy(
                src_ref=src, dst_ref=dst, send_sem=send_sems.at[k], recv_sem=recv_sems.at[k],
                device_id=to, device_id_type=MESH)

        own = [
            pltpu.make_async_copy(win_ref, win_full.at[b], local_sems.at[0]),
            pltpu.make_async_copy(wout_ref, wout_full.at[b], local_sems.at[1]),
        ]
        for cp in own:
            cp.start()
        mine_src = (win_ref.at[pl.ds(c * HALF_IN, HALF_IN)], wout_ref.at[pl.ds(c * HALF_OUT, HALF_OUT)])
        sends = []
        for j, (place, _) in enumerate(_other_chips(x, y)):
            for t, dst in enumerate(halves(b, c)):
                sends.append(copy(2 * j + t, mine_src[t], dst, (*place, c)))
        for cp in sends:
            cp.start()
        passed = []
        for j, (place, chip) in enumerate(_other_chips(x, y)):
            for t, landed in enumerate(halves(chip, c)):
                copy(2 * j + t, landed, landed, (*place, c)).wait_recv()
                fwd = copy(6 + 2 * j + t, landed, landed, sibling)
                fwd.start()
                passed.append(fwd)
        for j, (place, chip) in enumerate(_other_chips(x, y)):
            for t, landed in enumerate(halves(chip, 1 - c)):
                copy(6 + 2 * j + t, landed, landed, sibling).wait_recv()
        for cp in sends + passed:
            cp.wait_send()
        for cp in own:
            cp.wait()

    return pl.pallas_call(
        body,
        name="gather_weights",
        out_shape=(
            jax.ShapeDtypeStruct((N_CHIP, D_MODEL, SHARD_IN), BF16),
            jax.ShapeDtypeStruct((N_CHIP, SHARD_OUT, D_MODEL), BF16),
        ),
        in_specs=[ANY, ANY],
        out_specs=(ANY, ANY),
        scratch_shapes=[pltpu.SemaphoreType.DMA((12,)), pltpu.SemaphoreType.DMA((12,)), pltpu.SemaphoreType.DMA((2,))],
    )(win_sh, wout_sh)


def _window_sums(ext, forward):
    n = ext.shape[0]
    sums = []
    acc = ext
    for step in (1, 2, 4, 8):
        acc = acc + pltpu.roll(acc, (n - step) if forward else step, 0)
        sums.append(acc)
    return sums


def _row_counts(tile, rows):
    t = tile * TS + lax.broadcasted_iota(jnp.int32, (rows, GROUP), 0)
    return [jnp.minimum(t + 1, w).astype(F32) for w in POOL_WINDOWS]


def _pool_diffs(pv, prev_rows, tile):
    ext = jnp.concatenate([prev_rows, pv], axis=0)
    sums = _window_sums(ext, forward=False)
    counts = _row_counts(tile, TS)
    out = []
    for g in range(len(POOL_WINDOWS)):
        cols = slice(g * GROUP, (g + 1) * GROUP)
        out.append(sums[g][HALO:, cols] / counts[g] - pv[:, cols])
    return out


def _proj_piece(h, w_ref, n):
    chip, sub = divmod(n, PIECES_PER_SHARD)
    return _dot(h, w_ref[chip, :, sub * PIECE:(sub + 1) * PIECE])


def _fwd_inproj(x, g1, win_full, pw_bf, ps):
    seq = x.shape[0]
    n_tiles = seq // TS

    def body(x_ref, g1_ref, w_ref, pw_ref, ps_ref, pv_ref, pg_ref, ag_ref, q_ref, k_ref, v_ref, yp_ref, halo_ref):
        i = pl.program_id(0)

        @pl.when(i == 0)
        def _():
            halo_ref[...] = jnp.zeros_like(halo_ref)

        xt = x_ref[...]
        r = lax.rsqrt(jnp.mean(xt * xt, axis=-1, keepdims=True) + EPS)
        h = ((xt * r) * g1_ref[...]).astype(BF16)

        def wide(n):
            return jnp.concatenate([_proj_piece(h, w_ref, n), _proj_piece(h, w_ref, n + 1)], axis=1)

        pv = wide(0)
        pg = wide(2)
        pv_ref[...] = pv
        pg_ref[...] = pg
        q_ref[...] = (wide(4) * SCALE).astype(BF16)
        k_ref[...] = wide(6).astype(BF16)
        v_ref[...] = wide(8).astype(BF16)
        ag_ref[...] = wide(10)

        diffs = _pool_diffs(pv, halo_ref[...], i)
        halo_ref[...] = pv[TS - HALO:, :]
        mixed = jnp.concatenate(
            [_dot(diffs[g].astype(BF16), pw_ref[g]) for g in range(len(POOL_WINDOWS))], axis=1)
        yp_ref[...] = ((mixed * ps_ref[...]) * (pg * _sigmoid(pg))).astype(BF16)

    def rows(width):
        return pl.BlockSpec((TS, width), lambda i: (i, 0))

    def out(dtype):
        return jax.ShapeDtypeStruct((seq, POOL_WIDTH), dtype)

    return pl.pallas_call(
        body,
        name="fwd_inproj",
        grid=(n_tiles,),
        in_specs=[rows(D_MODEL), VMEM, VMEM, VMEM, VMEM],
        out_specs=[rows(POOL_WIDTH)] * 7,
        out_shape=[out(F32), out(F32), out(F32), out(BF16), out(BF16), out(BF16), out(BF16)],
        scratch_shapes=[pltpu.VMEM((HALO, POOL_WIDTH), F32)],
        compiler_params=pltpu.CompilerParams(dimension_semantics=("arbitrary",), vmem_limit_bytes=VMEM_LIMIT),
    )(x, g1, win_full, pw_bf, ps)


def _bias_tile(rel_bias):
    flat = jnp.concatenate(
        [jnp.broadcast_to(rel_bias[:, :1], (rel_bias.shape[0], BAND - CHUNK - 1)), rel_bias[:, :2 * MAX_REL]], axis=1)
    rows = [flat[:, CHUNK - 1 - i:CHUNK - 1 - i + BAND] for i in range(CHUNK)]
    bias = jnp.stack(rows, axis=1)
    return bias.reshape(N_PAIR, 2 * CHUNK, BAND).transpose(0, 2, 1)


def _rel_index_tile():
    j = np.arange(BAND)[:, None]
    i = np.arange(2 * CHUNK)[None, :] % CHUNK
    return (np.clip(j - LEFT_CHUNKS * CHUNK - i, -MAX_REL, MAX_REL) + MAX_REL).astype(np.int32)


def _by_head(block):
    low = lax.broadcasted_iota(jnp.int32, block.shape, 1) < HEAD_DIM
    zero = jnp.zeros_like(block)
    return jnp.concatenate([jnp.where(low, block, zero), jnp.where(low, zero, block)], axis=0)


def _own_head(square):
    top = square[:CHUNK]
    low = lax.broadcasted_iota(jnp.int32, top.shape, 1) < HEAD_DIM
    return jnp.where(low, top, square[CHUNK:])


def _band_probs(kb, q_rows, bias, first_valid):
    s = _dot_t(kb, q_rows) + bias
    key = lax.broadcasted_iota(jnp.int32, s.shape, 0)
    s = jnp.where(key >= first_valid, s, MASK_VALUE)
    e = jnp.exp(s - jnp.max(s, axis=0, keepdims=True))
    return e * (1.0 / jnp.sum(e, axis=0, keepdims=True))


def _first_valid_key(tile, chunk):
    return jnp.maximum((LEFT_CHUNKS - (tile * CHUNKS_PER_TILE + chunk)) * CHUNK, 0)


def _shift_band(i, band_ref, new_ref):
    @pl.when(i == 0)
    def _():
        band_ref[:TS] = jnp.zeros((TS, GROUP), band_ref.dtype)

    @pl.when(i > 0)
    def _():
        band_ref[:TS] = band_ref[TS:]

    band_ref[TS:] = new_ref[...]


def _attn_fwd(qs, k, v, ag, bias_t):
    seq = qs.shape[0]
    n_tiles = seq // TS

    def body(q_ref, k_ref, v_ref, ag_ref, bias_ref, a_ref, ya_ref, kband, vband):
        i = pl.program_id(1)
        _shift_band(i, kband, k_ref)
        _shift_band(i, vband, v_ref)
        bias = bias_ref[0]
        for cl in range(CHUNKS_PER_TILE):
            rows = slice(cl * CHUNK, (cl + 1) * CHUNK)
            band = slice(cl * CHUNK, cl * CHUNK + BAND)
            p = _band_probs(kband[band], _by_head(q_ref[rows]), bias, _first_valid_key(i, cl))
            a = _own_head(_tdot(p.astype(BF16), vband[band]))
            a_ref[rows] = a
            g = ag_ref[rows]
            ya_ref[rows] = (a * (g * _sigmoid(g))).astype(BF16)

    blk = pl.BlockSpec((TS, GROUP), lambda p, i: (i, p))
    return pl.pallas_call(
        body,
        name="attn_fwd",
        grid=(N_PAIR, n_tiles),
        in_specs=[blk, blk, blk, blk, pl.BlockSpec((1, BAND, GROUP), lambda p, i: (p, 0, 0))],
        out_specs=[blk, blk],
        out_shape=[jax.ShapeDtypeStruct((seq, ATTN_WIDTH), F32), jax.ShapeDtypeStruct((seq, ATTN_WIDTH), BF16)],
        scratch_shapes=[pltpu.VMEM((2 * TS, GROUP), BF16), pltpu.VMEM((2 * TS, GROUP), BF16)],
        compiler_params=pltpu.CompilerParams(
            dimension_semantics=("arbitrary", "arbitrary"), vmem_limit_bytes=VMEM_LIMIT),
    )(qs, k, v, ag, bias_t)


def _attn_bwd(qs, k, v, a, ag, dy, bias_t):
    seq = qs.shape[0]
    n_tiles = seq // TS

    def body(q_ref, k_ref, v_ref, a_ref, ag_ref, dy_ref, bias_ref,
             dq_ref, dk_ref, dv_ref, dag_ref, db_ref, kband, vband, dkacc, dvacc):
        i = pl.program_id(1)

        @pl.when(i == 0)
        def _():
            dkacc[...] = jnp.zeros_like(dkacc)
            dvacc[...] = jnp.zeros_like(dvacc)
            db_ref[...] = jnp.zeros_like(db_ref)

        @pl.when(i < n_tiles)
        def _():
            _shift_band(i, kband, k_ref)
            _shift_band(i, vband, v_ref)
            bias = bias_ref[0]
            for cl in range(CHUNKS_PER_TILE):
                rows = slice(cl * CHUNK, (cl + 1) * CHUNK)
                band = slice(cl * CHUNK, cl * CHUNK + BAND)
                q_rows = _by_head(q_ref[rows])
                kb = kband[band]
                vb = vband[band]
                p = _band_probs(kb, q_rows, bias, _first_valid_key(i, cl))
                g = ag_ref[rows]
                sg = _sigmoid(g)
                dyc = dy_ref[rows]
                dag_ref[rows] = dyc * a_ref[rows] * (sg * (1.0 + g * (1.0 - sg)))
                da_rows = _by_head((dyc * (g * sg)).astype(BF16))
                dp = _dot_t(vb, da_rows)
                ds = p * (dp - jnp.sum(p * dp, axis=0, keepdims=True))
                db_ref[0] += ds
                ds_bf = ds.astype(BF16)
                dq_ref[rows] = (_own_head(_tdot(ds_bf, kb)) * SCALE).astype(BF16)
                dkacc[band] += _dot(ds_bf, q_rows)
                dvacc[band] += _dot(p.astype(BF16), da_rows)

        dk_ref[...] = dkacc[:TS].astype(BF16)
        dv_ref[...] = dvacc[:TS].astype(BF16)
        dkacc[:TS] = dkacc[TS:]
        dvacc[:TS] = dvacc[TS:]
        dkacc[TS:] = jnp.zeros((TS, GROUP), F32)
        dvacc[TS:] = jnp.zeros((TS, GROUP), F32)

    last = n_tiles - 1
    cur = pl.BlockSpec((TS, GROUP), lambda p, i: (jnp.minimum(i, last), p))
    older = pl.BlockSpec((TS, GROUP), lambda p, i: (jnp.maximum(i - 1, 0), p))
    dy_blk = pl.BlockSpec((TS, GROUP), lambda p, i: (jnp.minimum(i, last), N_PAIR + p))
    per_pair = pl.BlockSpec((1, BAND, GROUP), lambda p, i: (p, 0, 0))

    def out(dtype):
        return jax.ShapeDtypeStruct((seq, ATTN_WIDTH), dtype)

    return pl.pallas_call(
        body,
        name="attn_bwd",
        grid=(N_PAIR, n_tiles + 1),
        in_specs=[cur, cur, cur, cur, cur, dy_blk, per_pair],
        out_specs=[cur, older, older, cur, per_pair],
        out_shape=[out(BF16), out(BF16), out(BF16), out(F32), jax.ShapeDtypeStruct((N_PAIR, BAND, GROUP), F32)],
        scratch_shapes=[
            pltpu.VMEM((2 * TS, GROUP), BF16), pltpu.VMEM((2 * TS, GROUP), BF16),
            pltpu.VMEM((2 * TS, GROUP), F32), pltpu.VMEM((2 * TS, GROUP), F32)],
        compiler_params=pltpu.CompilerParams(
            dimension_semantics=("arbitrary", "arbitrary"), vmem_limit_bytes=VMEM_LIMIT),
    )(qs, k, v, a, ag, dy, bias_t)


BIN_ROWS = 136


def _bias_bins(db_t):
    idx_t = jnp.asarray(_rel_index_tile())

    def body(db_ref, idx_ref, out_ref):
        lane = lax.broadcasted_iota(jnp.int32, (1, GROUP), 1)
        row = lax.broadcasted_iota(jnp.int32, (BIN_ROWS, GROUP), 0)
        out = jnp.zeros((BIN_ROWS, GROUP), F32)
        for r in range(N_REL - 1):
            lo = 0 if r == 0 else ((BAND - 2 * CHUNK + r) // 8) * 8
            hi = BAND if r == 0 else min(BAND, lo + CHUNK + 8)
            hit = jnp.where(idx_ref[lo:hi] == r, db_ref[0, lo:hi], 0.0)
            col = jnp.sum(hit, axis=0, keepdims=True)
            s0 = jnp.sum(jnp.where(lane < HEAD_DIM, col, 0.0), axis=1, keepdims=True)
            s1 = jnp.sum(jnp.where(lane < HEAD_DIM, 0.0, col), axis=1, keepdims=True)
            val = jnp.where(lane == 0, s0, jnp.where(lane == 1, s1, 0.0))
            out = jnp.where(row == r, val, out)
        out_ref[0] = out

    return pl.pallas_call(
        body,
        name="bias_bins",
        grid=(N_PAIR,),
        in_specs=[pl.BlockSpec((1, BAND, GROUP), lambda p: (p, 0, 0)), VMEM],
        out_specs=pl.BlockSpec((1, BIN_ROWS, GROUP), lambda p: (p, 0, 0)),
        out_shape=jax.ShapeDtypeStruct((N_PAIR, BIN_ROWS, GROUP), F32),
        compiler_params=pltpu.CompilerParams(dimension_semantics=("arbitrary",)),
    )(db_t, idx_t)


def _out_loss(x, tgt, yp, ya, wout_full, g2):
    seq = x.shape[0]
    n_tiles = seq // TS

    def body(x_ref, t_ref, yp_ref, ya_ref, w_ref, g2_ref, dx2_ref, dy_ref, gw_ref, gg_ref, loss_ref, sq_ref):
        i = pl.program_id(0)

        @pl.when(i == 0)
        def _():
            gw_ref[...] = jnp.zeros_like(gw_ref)
            gg_ref[...] = jnp.zeros_like(gg_ref)
            sq_ref[...] = jnp.zeros_like(sq_ref)

        ys = [yp_ref[:, :SHARD_OUT], yp_ref[:, SHARD_OUT:], ya_ref[:, :SHARD_OUT], ya_ref[:, SHARD_OUT:]]
        x2 = x_ref[...]
        for b in range(N_CHIP):
            x2 = x2 + _dot(ys[b], w_ref[b])
        r = lax.rsqrt(jnp.mean(x2 * x2, axis=-1, keepdims=True) + EPS)
        xh = x2 * r
        g2v = g2_ref[...]
        diff = xh * g2v - t_ref[...]
        sq_ref[...] += jnp.sum(diff * diff, axis=0, keepdims=True)
        dfin = diff * (1.0 / D_MODEL)
        gg_ref[...] += jnp.sum(dfin * xh, axis=0, keepdims=True)
        dxh = dfin * g2v
        dx2 = r * (dxh - xh * jnp.mean(dxh * xh, axis=-1, keepdims=True))
        dx2_ref[...] = dx2
        dx2_bf = dx2.astype(BF16)
        for b in range(N_CHIP):
            gw_ref[b] += _tdot(ys[b], dx2_bf)
            dy_ref[:, b * SHARD_OUT:(b + 1) * SHARD_OUT] = _dot_t(dx2_bf, w_ref[b])

        @pl.when(i == n_tiles - 1)
        def _():
            total = jnp.sum(sq_ref[...], axis=1, keepdims=True) * (0.5 / D_MODEL)
            loss_ref[...] = jnp.broadcast_to(total, loss_ref.shape)

    def rows(width):
        return pl.BlockSpec((TS, width), lambda i: (i, 0))

    return pl.pallas_call(
        body,
        name="out_loss",
        grid=(n_tiles,),
        in_specs=[rows(D_MODEL), rows(D_MODEL), rows(POOL_WIDTH), rows(ATTN_WIDTH), VMEM, VMEM],
        out_specs=[rows(D_MODEL), rows(D_MODEL), VMEM, VMEM, VMEM],
        out_shape=[
            jax.ShapeDtypeStruct((seq, D_MODEL), F32), jax.ShapeDtypeStruct((seq, D_MODEL), F32),
            jax.ShapeDtypeStruct((N_CHIP, SHARD_OUT, D_MODEL), F32), jax.ShapeDtypeStruct((1, D_MODEL), F32),
            jax.ShapeDtypeStruct((8, GROUP), F32)],
        scratch_shapes=[pltpu.VMEM((1, D_MODEL), F32)],
        compiler_params=pltpu.CompilerParams(dimension_semantics=("arbitrary",), vmem_limit_bytes=VMEM_LIMIT),
    )(x, tgt, yp, ya, wout_full, g2)


def _inproj_bwd(x, dx2, dy, pv, pg, dq, dk, dv, dag, win_full, pw_bf, g1, ps):
    seq = x.shape[0]
    n_tiles = seq // TS

    def body(x_ref, dx2_ref, dyp_ref, pv_ref, pvprev_ref, pg_ref, dq_ref, dk_ref, dv_ref, dag_ref,
             w_ref, pw_ref, g1_ref, ps_ref, gx_ref, gw_ref, gg_ref, gps_ref, gpw_ref, halo_ref):
        i = pl.program_id(0)
        tile = n_tiles - 1 - i

        @pl.when(i == 0)
        def _():
            gw_ref[...] = jnp.zeros_like(gw_ref)
            gg_ref[...] = jnp.zeros_like(gg_ref)
            gps_ref[...] = jnp.zeros_like(gps_ref)
            gpw_ref[...] = jnp.zeros_like(gpw_ref)
            halo_ref[...] = jnp.zeros_like(halo_ref)

        pv_t = pv_ref[...]
        pg_t = pg_ref[...]
        prev_rows = jnp.where(tile > 0, pvprev_ref[...], 0.0)
        diffs = [d.astype(BF16) for d in _pool_diffs(pv_t, prev_rows, tile)]
        mixed = jnp.concatenate([_dot(diffs[g], pw_ref[g]) for g in range(len(POOL_WINDOWS))], axis=1)
        sg = _sigmoid(pg_t)
        silu = pg_t * sg
        dyp = dyp_ref[...]
        psv = ps_ref[...]
        gps_ref[...] += jnp.sum(dyp * mixed * silu, axis=0, keepdims=True)
        dmixed = (dyp * psv * silu).astype(BF16)
        dpg = dyp * (mixed * psv) * (sg * (1.0 + pg_t * (1.0 - sg)))
        counts = _row_counts(tile, TS)
        dds = []
        for g in range(len(POOL_WINDOWS)):
            dm_g = dmixed[:, g * GROUP:(g + 1) * GROUP]
            gpw_ref[g] += _tdot(diffs[g], dm_g)
            dds.append(_dot_t(dm_g, pw_ref[g]))
        dd = jnp.concatenate(dds, axis=1)
        spread = jnp.concatenate([dds[g] / counts[g] for g in range(len(POOL_WINDOWS))], axis=1)
        sums = _window_sums(jnp.concatenate([spread, halo_ref[...]], axis=0), forward=True)
        halo_ref[...] = spread[:HALO]
        dpv = jnp.concatenate(
            [sums[g][:TS, g * GROUP:(g + 1) * GROUP] for g in range(len(POOL_WINDOWS))], axis=1) - dd

        xt = x_ref[...]
        r = lax.rsqrt(jnp.mean(xt * xt, axis=-1, keepdims=True) + EPS)
        xh = xt * r
        g1v = g1_ref[...]
        h = (xh * g1v).astype(BF16)
        wides = [dpv.astype(BF16), dpg.astype(BF16), dq_ref[...], dk_ref[...], dv_ref[...], dag_ref[...].astype(BF16)]
        dh = jnp.zeros((TS, D_MODEL), F32)
        for n in range(N_CHIP * PIECES_PER_SHARD):
            chip, sub = divmod(n, PIECES_PER_SHARD)
            cols = slice(sub * PIECE, (sub + 1) * PIECE)
            piece = wides[n // 2][:, (n % 2) * PIECE:(n % 2 + 1) * PIECE]
            dh = dh + _dot_t(piece, w_ref[chip, :, cols])
            gw_ref[chip, :, cols] += _tdot(h, piece)

        gg_ref[...] += jnp.sum(dh * xh, axis=0, keepdims=True)
        dxh = dh * g1v
        gx_ref[...] = dx2_ref[...] + r * (dxh - xh * jnp.mean(dxh * xh, axis=-1, keepdims=True))

    def rows(width):
        return pl.BlockSpec((TS, width), lambda i: (n_tiles - 1 - i, 0))

    prev = pl.BlockSpec((HALO, POOL_WIDTH), lambda i: (jnp.maximum((n_tiles - 1 - i) * (TS // HALO) - 1, 0), 0))
    return pl.pallas_call(
        body,
        name="inproj_bwd",
        grid=(n_tiles,),
        in_specs=[rows(D_MODEL), rows(D_MODEL), rows(POOL_WIDTH), rows(POOL_WIDTH), prev, rows(POOL_WIDTH),
                  rows(ATTN_WIDTH), rows(ATTN_WIDTH), rows(ATTN_WIDTH), rows(ATTN_WIDTH), VMEM, VMEM, VMEM, VMEM],
        out_specs=[rows(D_MODEL), VMEM, VMEM, VMEM, VMEM],
        out_shape=[
            jax.ShapeDtypeStruct((seq, D_MODEL), F32),
            jax.ShapeDtypeStruct((N_CHIP, D_MODEL, SHARD_IN), F32),
            jax.ShapeDtypeStruct((1, D_MODEL), F32),
            jax.ShapeDtypeStruct((1, POOL_WIDTH), F32),
            jax.ShapeDtypeStruct((len(POOL_WINDOWS), GROUP, GROUP), F32)],
        scratch_shapes=[pltpu.VMEM((HALO, POOL_WIDTH), F32)],
        compiler_params=pltpu.CompilerParams(dimension_semantics=("arbitrary",), vmem_limit_bytes=VMEM_LIMIT),
    )(x, dx2, dy, pv, pv, pg, dq, dk, dv, dag, win_full, pw_bf, g1, ps)


def _reduce_grads(gwin, gwout, small):
    half_small = SMALL_ROWS // 2

    def body(gwin_ref, gwout_ref, small_ref, gin_out, gout_out, small_out,
             pair_in, pair_out, pair_small, tx_in, tx_out, rx_in, rx_out, rx_small, send_sems, recv_sems):
        x, y, c = _my_place()
        b = 2 * x + y
        sibling = (x, y, 1 - c)
        mine_in = pl.ds(pl.multiple_of(c * HALF_IN, HALF_IN), HALF_IN)
        mine_out = pl.ds(pl.multiple_of(c * HALF_OUT, HALF_OUT), HALF_OUT)
        mine_small = pl.ds(pl.multiple_of(c * half_small, 8), half_small)
        theirs_in = pl.ds(pl.multiple_of((1 - c) * HALF_IN, HALF_IN), HALF_IN)
        theirs_out = pl.ds(pl.multiple_of((1 - c) * HALF_OUT, HALF_OUT), HALF_OUT)

        def copy(k, src, dst, to):
            return pltpu.make_async_remote_copy(
                src_ref=src, dst_ref=dst, send_sem=send_sems.at[k], recv_sem=recv_sems.at[k],
                device_id=to, device_id_type=MESH)

        swaps = [
            copy(0, gwin_ref.at[:, theirs_in], pair_in, sibling),
            copy(1, gwout_ref.at[:, theirs_out], pair_out, sibling),
            copy(2, small_ref, pair_small, sibling),
        ]
        for cp in swaps:
            cp.start()
        for cp in swaps:
            cp.wait_recv()
        for chip in range(N_CHIP):
            pair_in[chip] = gwin_ref[chip, mine_in] + pair_in[chip]
            pair_out[chip] = gwout_ref[chip, mine_out] + pair_out[chip]
        pair_small[...] = small_ref[...] + pair_small[...]

        rx_small[b] = pair_small[mine_small]
        sends = []
        for j, (place, chip) in enumerate(_other_chips(x, y)):
            tx_in[j] = pair_in[chip].astype(BF16)
            tx_out[j] = pair_out[chip].astype(BF16)
            to = (*place, c)
            sends += [
                copy(3 + 3 * j, tx_in.at[j], rx_in.at[j], to),
                copy(4 + 3 * j, tx_out.at[j], rx_out.at[j], to),
                copy(5 + 3 * j, pair_small.at[mine_small], rx_small.at[b], to),
            ]
        for cp in sends:
            cp.start()
        for j, (place, chip) in enumerate(_other_chips(x, y)):
            to = (*place, c)
            copy(3 + 3 * j, tx_in.at[j], rx_in.at[j], to).wait_recv()
            copy(4 + 3 * j, tx_out.at[j], rx_out.at[j], to).wait_recv()
            copy(5 + 3 * j, pair_small.at[mine_small], rx_small.at[chip], to).wait_recv()
        total_in = pair_in[b]
        total_out = pair_out[b]
        for j in range(N_CHIP - 1):
            total_in = total_in + rx_in[j].astype(F32)
            total_out = total_out + rx_out[j].astype(F32)
        gin_out[mine_in] = total_in
        gout_out[mine_out] = total_out
        small_out[mine_small] = ((rx_small[0] + rx_small[1]) + rx_small[2]) + rx_small[3]

        shares = [
            copy(12, gin_out.at[mine_in], gin_out.at[mine_in], sibling),
            copy(13, gout_out.at[mine_out], gout_out.at[mine_out], sibling),
            copy(14, small_out.at[mine_small], small_out.at[mine_small], sibling),
        ]
        for cp in shares:
            cp.start()
        theirs_small = pl.ds(pl.multiple_of((1 - c) * half_small, 8), half_small)
        copy(12, gin_out.at[theirs_in], gin_out.at[theirs_in], sibling).wait_recv()
        copy(13, gout_out.at[theirs_out], gout_out.at[theirs_out], sibling).wait_recv()
        copy(14, small_out.at[theirs_small], small_out.at[theirs_small], sibling).wait_recv()
        for cp in swaps + sends + shares:
            cp.wait_send()

    return pl.pallas_call(
        body,
        name="reduce_grads",
        out_shape=(
            jax.ShapeDtypeStruct((D_MODEL, SHARD_IN), F32),
            jax.ShapeDtypeStruct((SHARD_OUT, D_MODEL), F32),
            jax.ShapeDtypeStruct((SMALL_ROWS, GROUP), F32),
        ),
        in_specs=[VMEM, VMEM, VMEM],
        out_specs=(VMEM, VMEM, VMEM),
        scratch_shapes=[
            pltpu.VMEM((N_CHIP, HALF_IN, SHARD_IN), F32),
            pltpu.VMEM((N_CHIP, HALF_OUT, D_MODEL), F32),
            pltpu.VMEM((SMALL_ROWS, GROUP), F32),
            pltpu.VMEM((N_CHIP - 1, HALF_IN, SHARD_IN), BF16),
            pltpu.VMEM((N_CHIP - 1, HALF_OUT, D_MODEL), BF16),
            pltpu.VMEM((N_CHIP - 1, HALF_IN, SHARD_IN), BF16),
            pltpu.VMEM((N_CHIP - 1, HALF_OUT, D_MODEL), BF16),
            pltpu.VMEM((N_CHIP, half_small, GROUP), F32),
            pltpu.SemaphoreType.DMA((15,)),
            pltpu.SemaphoreType.DMA((15,)),
        ],
        compiler_params=pltpu.CompilerParams(vmem_limit_bytes=VMEM_LIMIT),
    )(gwin, gwout, small)


def _adamw(name, w, g, m, v, block_rows):
    rows, cols = w.shape

    def body(w_ref, g_ref, m_ref, v_ref, d_ref, m_out, v_out):
        grad = g_ref[...]
        m_new = ADAM_B1 * m_ref[...] + (1.0 - ADAM_B1) * grad
        v_new = ADAM_B2 * v_ref[...] + (1.0 - ADAM_B2) * (grad * grad)
        m_hat = m_new / (1.0 - ADAM_B1 ** ADAM_STEP)
        v_hat = v_new / (1.0 - ADAM_B2 ** ADAM_STEP)
        d_ref[...] = -ADAM_LR * (m_hat / (jnp.sqrt(v_hat) + ADAM_EPS) + ADAM_WD * w_ref[...])
        m_out[...] = m_new
        v_out[...] = v_new

    blk = pl.BlockSpec((block_rows, cols), lambda i: (i, 0))
    shape = jax.ShapeDtypeStruct((rows, cols), F32)
    return pl.pallas_call(
        body,
        name=name,
        grid=(rows // block_rows,),
        in_specs=[blk] * 4,
        out_specs=[blk] * 3,
        out_shape=[shape] * 3,
        compiler_params=pltpu.CompilerParams(dimension_semantics=("arbitrary",)),
    )(w, g, m, v)


def _pack_small(norm_gain, pool_w, pool_scale, rel_bias, final_gain, loss_rows):
    parts = [
        norm_gain.reshape(8, GROUP),
        pool_w.reshape(len(POOL_WINDOWS) * GROUP, GROUP),
        jnp.pad(pool_scale.reshape(4, GROUP), ((0, 4), (0, 0))),
        jnp.pad(rel_bias.reshape(8, N_REL), ((0, 0), (0, 2 * GROUP - N_REL))).reshape(16, GROUP),
        final_gain.reshape(8, GROUP),
        loss_rows,
    ]
    return jnp.concatenate(parts, axis=0)


def _unpack_small(block):
    norm_gain = block[0:8].reshape(1, D_MODEL)
    pool_w = block[8:520].reshape(1, len(POOL_WINDOWS), GROUP, GROUP)
    pool_scale = block[520:524].reshape(1, POOL_WIDTH)
    rel_bias = block[528:544].reshape(8, 2 * GROUP)[:, :N_REL].reshape(1, 8, N_REL)
    final_gain = block[544:552].reshape(D_MODEL)
    return norm_gain, pool_w, pool_scale, rel_bias, final_gain


def kernel(x, norm_gain, w_in, pool_w, pool_scale, rel_bias, w_out, final_norm_gain, loss_target, m_norm_gain, m_w_in, m_pool_w, m_pool_scale, m_rel_bias, m_w_out, m_final_norm_gain, v_norm_gain, v_w_in, v_pool_w, v_pool_scale, v_rel_bias, v_w_out, v_final_norm_gain):
    assert x.shape[1] % TS == 0 and x.shape[2] == D_MODEL
    xs = x[0]
    tgt = loss_target[0]
    g1 = norm_gain.reshape(1, D_MODEL)
    g2 = final_norm_gain.reshape(1, D_MODEL)
    ps = pool_scale.reshape(1, POOL_WIDTH)
    pw_bf = pool_w[0].astype(BF16)

    win_full, wout_full = _gather_weights(w_in[0].astype(BF16), w_out[0].astype(BF16))
    bias_t = _bias_tile(rel_bias[0])

    pv, pg, ag, qs, k, v, yp = _fwd_inproj(xs, g1, win_full, pw_bf, ps)
    a, ya = _attn_fwd(qs, k, v, ag, bias_t)
    dx2, dy, gwout, gg2, loss_rows = _out_loss(xs, tgt, yp, ya, wout_full, g2)
    dq, dk, dv, dag, db_t = _attn_bwd(qs, k, v, a, ag, dy, bias_t)
    bins = _bias_bins(db_t)
    gx, gwin, gg1, gps, gpw = _inproj_bwd(xs, dx2, dy, pv, pg, dq, dk, dv, dag, win_full, pw_bf, g1, ps)

    g_bias = bins[:, :N_REL, :2].transpose(0, 2, 1).reshape(8, N_REL)
    small = _pack_small(gg1, gpw, gps, g_bias, gg2, loss_rows)
    g_win, g_wout, g_small = _reduce_grads(gwin, gwout, small)
    loss = g_small[LOSS_ROW, 0]

    zeros8 = jnp.zeros((8, GROUP), F32)
    w_small = _pack_small(norm_gain, pool_w, pool_scale, rel_bias, final_norm_gain, zeros8)
    m_small = _pack_small(m_norm_gain, m_pool_w, m_pool_scale, m_rel_bias, m_final_norm_gain, zeros8)
    v_small = _pack_small(v_norm_gain, v_pool_w, v_pool_scale, v_rel_bias, v_final_norm_gain, zeros8)

    d_win, m_win, v_win = _adamw("adamw_w_in", w_in[0], g_win, m_w_in[0], v_w_in[0], 256)
    d_wout, m_wout, v_wout = _adamw("adamw_w_out", w_out[0], g_wout, m_w_out[0], v_w_out[0], 128)
    d_small, m_new_small, v_new_small = _adamw("adamw_small", w_small, g_small, m_small, v_small, SMALL_ROWS // 2)

    def full(win_part, wout_part, block):
        ng, pw, psc, rb, fg = _unpack_small(block)
        return [ng, win_part[None], pw, psc, rb, wout_part[None], fg]

    grads = full(g_win, g_wout, g_small)
    deltas = full(d_win, d_wout, d_small)
    new_m = full(m_win, m_wout, m_new_small)
    new_v = full(v_win, v_wout, v_new_small)
    return (loss, gx[None], *grads, *deltas, *new_m, *new_v)
```

```python
import numpy as np
import jax
import jax.numpy as jnp
from jax import lax
from jax.experimental import pallas as pl
from jax.experimental.pallas import tpu as pltpu

F32 = jnp.float32
BF16 = jnp.bfloat16

D_MODEL = 1024
POOL_WIDTH = 512
ATTN_WIDTH = 512
POOL_WINDOWS = (2, 4, 8, 16)
GROUP = 128
CHUNK = 64
LEFT_CHUNKS = 8
BAND = (LEFT_CHUNKS + 1) * CHUNK
HEAD_DIM = 64
N_PAIR = 4
MAX_REL = 64
N_REL = 2 * MAX_REL + 1
EPS = 1e-6
MASK_VALUE = -1e30
SCALE = 0.125

ADAM_LR = 0.001
ADAM_B1 = 0.9
ADAM_B2 = 0.999
ADAM_EPS = 1e-08
ADAM_WD = 0.01
ADAM_STEP = 10

TS = LEFT_CHUNKS * CHUNK
SUPER = 2 * CHUNK
WINDOW = BAND + CHUNK
SUPERS_PER_TILE = TS // SUPER
PAIR_LANES = 2 * SUPER
HALO = 16
N_CHIP = 4
SHARD_IN = 768
SHARD_OUT = 256
PIECE = 256
PIECES_PER_SHARD = SHARD_IN // PIECE
HALF_IN = D_MODEL // 2
HALF_OUT = SHARD_OUT // 2
SMALL_ROWS = 560
LOSS_ROW = 552
VMEM_LIMIT = 56 * 1024 * 1024

MESH = pl.DeviceIdType.MESH
ANY = pl.BlockSpec(memory_space=pl.ANY)
VMEM = pl.BlockSpec(memory_space=pltpu.VMEM)


def _sigmoid(x):
    return 1.0 / (1.0 + jnp.exp(-x))


def _dot(a, b):
    return jnp.dot(a, b, preferred_element_type=F32)


def _dot_t(a, b):
    return lax.dot_general(a, b, (((1,), (1,)), ((), ())), preferred_element_type=F32)


def _tdot(a, b):
    return lax.dot_general(a, b, (((0,), (0,)), ((), ())), preferred_element_type=F32)


def _my_place():
    return lax.axis_index("x"), lax.axis_index("y"), lax.axis_index("c")


def _other_chips(x, y):
    places = [(1 - x, y), (x, 1 - y), (1 - x, 1 - y)]
    return [(p, 2 * p[0] + p[1]) for p in places]


def _gather_weights(win_sh, wout_sh):
    def body(win_ref, wout_ref, win_full, wout_full, send_sems, recv_sems, local_sems):
        x, y, c = _my_place()
        b = 2 * x + y
        sibling = (x, y, 1 - c)

        def halves(chip, core):
            return (
                win_full.at[chip, pl.ds(core * HALF_IN, HALF_IN)],
                wout_full.at[chip, pl.ds(core * HALF_OUT, HALF_OUT)],
            )

        def copy(k, src, dst, to):
            return pltpu.make_async_remote_copy(
                src_ref=src, dst_ref=dst, send_sem=send_sems.at[k], recv_sem=recv_sems.at[k],
                device_id=to, device_id_type=MESH)

        own = [
            pltpu.make_async_copy(win_ref, win_full.at[b], local_sems.at[0]),
            pltpu.make_async_copy(wout_ref, wout_full.at[b], local_sems.at[1]),
        ]
        for cp in own:
            cp.start()
        mine_src = (win_ref.at[pl.ds(c * HALF_IN, HALF_IN)], wout_ref.at[pl.ds(c * HALF_OUT, HALF_OUT)])
        sends = []
        for j, (place, _) in enumerate(_other_chips(x, y)):
            for t, dst in enumerate(halves(b, c)):
                sends.append(copy(2 * j + t, mine_src[t], dst, (*place, c)))
        for cp in sends:
            cp.start()
        passed = []
        for j, (place, chip) in enumerate(_other_chips(x, y)):
            for t, landed in enumerate(halves(chip, c)):
                copy(2 * j + t, landed, landed, (*place, c)).wait_recv()
                fwd = copy(6 + 2 * j + t, landed, landed, sibling)
                fwd.start()
                passed.append(fwd)
        for j, (place, chip) in enumerate(_other_chips(x, y)):
            for t, landed in enumerate(halves(chip, 1 - c)):
                copy(6 + 2 * j + t, landed, landed, sibling).wait_recv()
        for cp in sends + passed:
            cp.wait_send()
        for cp in own:
            cp.wait()

    return pl.pallas_call(
        body,
        name="gather_weights",
        out_shape=(
            jax.ShapeDtypeStruct((N_CHIP, D_MODEL, SHARD_IN), BF16),
            jax.ShapeDtypeStruct((N_CHIP, SHARD_OUT, D_MODEL), BF16),
        ),
        in_specs=[ANY, ANY],
        out_specs=(ANY, ANY),
        scratch_shapes=[pltpu.SemaphoreType.DMA((12,)), pltpu.SemaphoreType.DMA((12,)), pltpu.SemaphoreType.DMA((2,))],
    )(win_sh, wout_sh)


def _window_sums(ext, forward):
    n = ext.shape[0]
    sums = []
    acc = ext
    for step in (1, 2, 4, 8):
        acc = acc + pltpu.roll(acc, (n - step) if forward else step, 0)
        sums.append(acc)
    return sums


def _row_counts(tile, rows):
    t = tile * TS + lax.broadcasted_iota(jnp.int32, (rows, GROUP), 0)
    return [jnp.minimum(t + 1, w).astype(F32) for w in POOL_WINDOWS]


def _pool_diffs(pv, prev_rows, tile):
    ext = jnp.concatenate([prev_rows, pv], axis=0)
    sums = _window_sums(ext, forward=False)
    counts = _row_counts(tile, TS)
    out = []
    for g in range(len(POOL_WINDOWS)):
        cols = slice(g * GROUP, (g + 1) * GROUP)
        out.append(sums[g][HALO:, cols] / counts[g] - pv[:, cols])
    return out


def _proj_piece(h, w_ref, n):
    chip, sub = divmod(n, PIECES_PER_SHARD)
    return _dot(h, w_ref[chip, :, sub * PIECE:(sub + 1) * PIECE])


def _fwd_inproj(x, g1, win_full, pw_bf, ps):
    seq = x.shape[0]
    n_tiles = seq // TS

    def body(x_ref, g1_ref, w_ref, pw_ref, ps_ref, pv_ref, pg_ref, ag_ref, q_ref, k_ref, v_ref, yp_ref, halo_ref):
        i = pl.program_id(0)

        @pl.when(i == 0)
        def _():
            halo_ref[...] = jnp.zeros_like(halo_ref)

        xt = x_ref[...]
        r = lax.rsqrt(jnp.mean(xt * xt, axis=-1, keepdims=True) + EPS)
        h = ((xt * r) * g1_ref[...]).astype(BF16)

        def wide(n):
            return jnp.concatenate([_proj_piece(h, w_ref, n), _proj_piece(h, w_ref, n + 1)], axis=1)

        pv = wide(0)
        pg = wide(2)
        pv_ref[...] = pv
        pg_ref[...] = pg
        q_ref[...] = (wide(4) * SCALE).astype(BF16)
        k_ref[...] = wide(6).astype(BF16)
        v_ref[...] = wide(8).astype(BF16)
        ag_ref[...] = wide(10)

        diffs = _pool_diffs(pv, halo_ref[...], i)
        halo_ref[...] = pv[TS - HALO:, :]
        mixed = jnp.concatenate(
            [_dot(diffs[g].astype(BF16), pw_ref[g]) for g in range(len(POOL_WINDOWS))], axis=1)
        yp_ref[...] = ((mixed * ps_ref[...]) * (pg * _sigmoid(pg))).astype(BF16)

    def rows(width):
        return pl.BlockSpec((TS, width), lambda i: (i, 0))

    def out(dtype):
        return jax.ShapeDtypeStruct((seq, POOL_WIDTH), dtype)

    return pl.pallas_call(
        body,
        name="fwd_inproj",
        grid=(n_tiles,),
        in_specs=[rows(D_MODEL), VMEM, VMEM, VMEM, VMEM],
        out_specs=[rows(POOL_WIDTH)] * 7,
        out_shape=[out(F32), out(F32), out(F32), out(BF16), out(BF16), out(BF16), out(BF16)],
        scratch_shapes=[pltpu.VMEM((HALO, POOL_WIDTH), F32)],
        compiler_params=pltpu.CompilerParams(dimension_semantics=("arbitrary",), vmem_limit_bytes=VMEM_LIMIT),
    )(x, g1, win_full, pw_bf, ps)


def _bias_tile(rel_bias):
    flat = jnp.concatenate(
        [jnp.broadcast_to(rel_bias[:, :1], (rel_bias.shape[0], BAND - CHUNK - 1)), rel_bias[:, :2 * MAX_REL]], axis=1)
    rows = [flat[:, CHUNK - 1 - i:CHUNK - 1 - i + BAND] for i in range(CHUNK)]
    bias = jnp.stack(rows, axis=1)
    first = jnp.pad(bias, ((0, 0), (0, 0), (0, CHUNK)), constant_values=MASK_VALUE)
    second = jnp.pad(bias, ((0, 0), (0, 0), (CHUNK, 0)), constant_values=MASK_VALUE)
    both = jnp.concatenate([first, second], axis=1)
    return both.reshape(N_PAIR, PAIR_LANES, WINDOW).transpose(0, 2, 1)


def _rel_index_tile():
    j = np.arange(WINDOW)[:, None]
    q = np.arange(PAIR_LANES)[None, :] % SUPER
    band_key = j - CHUNK * (q // CHUNK)
    idx = np.clip(band_key - LEFT_CHUNKS * CHUNK - q % CHUNK, -MAX_REL, MAX_REL) + MAX_REL
    return np.where((band_key >= 0) & (band_key < BAND), idx, -1).astype(np.int32)


def _by_head(block):
    low = lax.broadcasted_iota(jnp.int32, block.shape, 1) < HEAD_DIM
    zero = jnp.zeros_like(block)
    return jnp.concatenate([jnp.where(low, block, zero), jnp.where(low, zero, block)], axis=0)


def _own_head_rows(cross):
    head_of_row = lax.broadcasted_iota(jnp.int32, cross.shape, 0) >= HEAD_DIM
    head_of_lane = lax.broadcasted_iota(jnp.int32, cross.shape, 1) >= SUPER
    both = jnp.where(jnp.logical_xor(head_of_row, head_of_lane), 0.0, cross).T
    return both[:SUPER] + both[SUPER:]


def _with_mask_lane(q_rows):
    lane = lax.broadcasted_iota(jnp.int32, q_rows.shape, 1)
    return jnp.concatenate([q_rows, jnp.where(lane == 0, MASK_VALUE, 0.0).astype(q_rows.dtype)], axis=1)


def _band_exp(kb, q_rows, bias):
    s = _dot_t(kb, _with_mask_lane(q_rows)) + bias
    e = jnp.exp(s - jnp.max(s, axis=0, keepdims=True))
    return e, jnp.sum(e, axis=0, keepdims=True)


def _shift_band(i, band_ref, new_ref):
    @pl.when(i == 0)
    def _():
        band_ref[:TS] = jnp.zeros((TS, GROUP), band_ref.dtype)

    @pl.when(i > 0)
    def _():
        band_ref[:TS] = band_ref[TS:]

    band_ref[TS:] = new_ref[...]


def _shift_key_band(i, band_ref, new_ref):
    @pl.when(i == 0)
    def _():
        lane = lax.broadcasted_iota(jnp.int32, (TS, 2 * GROUP), 1)
        band_ref[:TS] = jnp.where(lane == GROUP, 1.0, 0.0).astype(band_ref.dtype)
        band_ref[TS:, GROUP:] = jnp.zeros((TS, GROUP), band_ref.dtype)

    @pl.when(i > 0)
    def _():
        band_ref[:TS] = band_ref[TS:]

    band_ref[TS:, :GROUP] = new_ref[...]


def _shift_band_t(i, band_ref, new_ref):
    @pl.when(i == 0)
    def _():
        band_ref[:, :TS] = jnp.zeros((GROUP, TS), band_ref.dtype)

    @pl.when(i > 0)
    def _():
        band_ref[:, :TS] = band_ref[:, TS:]

    band_ref[:, TS:] = new_ref[...].T


def _attn_fwd(qs, k, v, ag, bias_t):
    seq = qs.shape[0]
    n_tiles = seq // TS

    def body(q_ref, k_ref, v_ref, ag_ref, bias_ref, a_ref, ya_ref, kband, vband_t):
        i = pl.program_id(1)
        _shift_key_band(i, kband, k_ref)
        _shift_band_t(i, vband_t, v_ref)

        def weights(sc):
            rows = slice(sc * SUPER, (sc + 1) * SUPER)
            win = slice(sc * SUPER, sc * SUPER + WINDOW)
            e, total = _band_exp(kband[win], _by_head(q_ref[rows]), bias_ref[0])
            return e.astype(BF16), 1.0 / total

        nxt = weights(0)
        for sc in range(SUPERS_PER_TILE):
            rows = slice(sc * SUPER, (sc + 1) * SUPER)
            win = slice(sc * SUPER, sc * SUPER + WINDOW)
            e_bf, inv_total = nxt
            if sc + 1 < SUPERS_PER_TILE:
                nxt = weights(sc + 1)
            a = _own_head_rows(_dot(vband_t[:, win], e_bf) * inv_total)
            a_ref[rows] = a
            g = ag_ref[rows]
            ya_ref[rows] = (a * (g * _sigmoid(g))).astype(BF16)

    blk = pl.BlockSpec((TS, GROUP), lambda p, i: (i, p))
    return pl.pallas_call(
        body,
        name="attn_fwd",
        grid=(N_PAIR, n_tiles),
        in_specs=[blk, blk, blk, blk, pl.BlockSpec((1, WINDOW, PAIR_LANES), lambda p, i: (p, 0, 0))],
        out_specs=[blk, blk],
        out_shape=[jax.ShapeDtypeStruct((seq, ATTN_WIDTH), F32), jax.ShapeDtypeStruct((seq, ATTN_WIDTH), BF16)],
        scratch_shapes=[pltpu.VMEM((2 * TS, 2 * GROUP), BF16), pltpu.VMEM((GROUP, 2 * TS), BF16)],
        compiler_params=pltpu.CompilerParams(
            dimension_semantics=("arbitrary", "arbitrary"), vmem_limit_bytes=VMEM_LIMIT),
    )(qs, k, v, ag, bias_t)


def _attn_bwd(qs, k, v, a, ag, dy, bias_t):
    seq = qs.shape[0]
    n_tiles = seq // TS

    def body(q_ref, k_ref, v_ref, a_ref, ag_ref, dy_ref, bias_ref,
             dq_ref, dk_ref, dv_ref, dag_ref, db_ref, kband, vband, kband_t, dkacc, dvacc):
        i = pl.program_id(1)

        @pl.when(i == 0)
        def _():
            dkacc[...] = jnp.zeros_like(dkacc)
            dvacc[...] = jnp.zeros_like(dvacc)
            db_ref[...] = jnp.zeros_like(db_ref)

        @pl.when(i < n_tiles)
        def _():
            _shift_key_band(i, kband, k_ref)
            _shift_band(i, vband, v_ref)
            _shift_band_t(i, kband_t, k_ref)
            def score_grads(sc):
                rows = slice(sc * SUPER, (sc + 1) * SUPER)
                win = slice(sc * SUPER, sc * SUPER + WINDOW)
                q_rows = _by_head(q_ref[rows])
                g = ag_ref[rows]
                sg = _sigmoid(g)
                dyc = dy_ref[rows]
                dag_ref[rows] = dyc * a_ref[rows] * (sg * (1.0 + g * (1.0 - sg)))
                da_rows = _by_head((dyc * (g * sg)).astype(BF16))
                e, total = _band_exp(kband[win], q_rows, bias_ref[0])
                p = e * (1.0 / total)
                dp = _dot_t(vband[win], da_rows)
                ds = p * (dp - jnp.sum(p * dp, axis=0, keepdims=True))
                db_ref[0] += ds
                return q_rows, da_rows, p.astype(BF16), ds.astype(BF16)

            nxt = score_grads(0)
            for sc in range(SUPERS_PER_TILE):
                rows = slice(sc * SUPER, (sc + 1) * SUPER)
                win = slice(sc * SUPER, sc * SUPER + WINDOW)
                q_rows, da_rows, p_bf, ds_bf = nxt
                if sc + 1 < SUPERS_PER_TILE:
                    nxt = score_grads(sc + 1)
                dq_ref[rows] = (_own_head_rows(_dot(kband_t[:, win], ds_bf)) * SCALE).astype(BF16)
                dkacc[win] += _dot(ds_bf, q_rows)
                dvacc[win] += _dot(p_bf, da_rows)

        dk_ref[...] = dkacc[:TS].astype(BF16)
        dv_ref[...] = dvacc[:TS].astype(BF16)
        dkacc[:TS] = dkacc[TS:]
        dvacc[:TS] = dvacc[TS:]
        dkacc[TS:] = jnp.zeros((TS, GROUP), F32)
        dvacc[TS:] = jnp.zeros((TS, GROUP), F32)

    last = n_tiles - 1
    cur = pl.BlockSpec((TS, GROUP), lambda p, i: (jnp.minimum(i, last), p))
    older = pl.BlockSpec((TS, GROUP), lambda p, i: (jnp.maximum(i - 1, 0), p))
    dy_blk = pl.BlockSpec((TS, GROUP), lambda p, i: (jnp.minimum(i, last), N_PAIR + p))
    per_pair = pl.BlockSpec((1, WINDOW, PAIR_LANES), lambda p, i: (p, 0, 0))

    def out(dtype):
        return jax.ShapeDtypeStruct((seq, ATTN_WIDTH), dtype)

    return pl.pallas_call(
        body,
        name="attn_bwd",
        grid=(N_PAIR, n_tiles + 1),
        in_specs=[cur, cur, cur, cur, cur, dy_blk, per_pair],
        out_specs=[cur, older, older, cur, per_pair],
        out_shape=[out(BF16), out(BF16), out(BF16), out(F32),
                   jax.ShapeDtypeStruct((N_PAIR, WINDOW, PAIR_LANES), F32)],
        scratch_shapes=[
            pltpu.VMEM((2 * TS, 2 * GROUP), BF16), pltpu.VMEM((2 * TS, GROUP), BF16), pltpu.VMEM((GROUP, 2 * TS), BF16),
            pltpu.VMEM((2 * TS, GROUP), F32), pltpu.VMEM((2 * TS, GROUP), F32)],
        compiler_params=pltpu.CompilerParams(
            dimension_semantics=("arbitrary", "arbitrary"), vmem_limit_bytes=VMEM_LIMIT),
    )(qs, k, v, a, ag, dy, bias_t)


BIN_ROWS = 136


def _bias_bins(db_t):
    idx_t = jnp.asarray(_rel_index_tile())

    def body(db_ref, idx_ref, out_ref):
        lane = lax.broadcasted_iota(jnp.int32, (1, GROUP), 1)
        row = lax.broadcasted_iota(jnp.int32, (BIN_ROWS, GROUP), 0)
        out = jnp.zeros((BIN_ROWS, GROUP), F32)
        for r in range(N_REL - 1):
            lo = 0 if r == 0 else ((BAND - 2 * CHUNK + r) // 8) * 8
            hi = WINDOW if r == 0 else min(WINDOW, lo + SUPER + 8)
            hit = jnp.where(idx_ref[lo:hi] == r, db_ref[0, lo:hi], 0.0)
            col = jnp.sum(hit, axis=0, keepdims=True)
            s0 = jnp.sum(col[:, :SUPER], axis=1, keepdims=True)
            s1 = jnp.sum(col[:, SUPER:], axis=1, keepdims=True)
            val = jnp.where(lane == 0, s0, jnp.where(lane == 1, s1, 0.0))
            out = jnp.where(row == r, val, out)
        out_ref[0] = out

    return pl.pallas_call(
        body,
        name="bias_bins",
        grid=(N_PAIR,),
        in_specs=[pl.BlockSpec((1, WINDOW, PAIR_LANES), lambda p: (p, 0, 0)), VMEM],
        out_specs=pl.BlockSpec((1, BIN_ROWS, GROUP), lambda p: (p, 0, 0)),
        out_shape=jax.ShapeDtypeStruct((N_PAIR, BIN_ROWS, GROUP), F32),
        compiler_params=pltpu.CompilerParams(dimension_semantics=("arbitrary",)),
    )(db_t, idx_t)


def _out_loss(x, tgt, yp, ya, wout_full, g2):
    seq = x.shape[0]
    n_tiles = seq // TS

    def body(x_ref, t_ref, yp_ref, ya_ref, w_ref, g2_ref, dx2_ref, dy_ref, gw_ref, gg_ref, loss_ref, sq_ref):
        i = pl.program_id(0)

        @pl.when(i == 0)
        def _():
            gw_ref[...] = jnp.zeros_like(gw_ref)
            gg_ref[...] = jnp.zeros_like(gg_ref)
            sq_ref[...] = jnp.zeros_like(sq_ref)

        ys = [yp_ref[:, :SHARD_OUT], yp_ref[:, SHARD_OUT:], ya_ref[:, :SHARD_OUT], ya_ref[:, SHARD_OUT:]]
        x2 = x_ref[...]
        for b in range(N_CHIP):
            x2 = x2 + _dot(ys[b], w_ref[b])
        r = lax.rsqrt(jnp.mean(x2 * x2, axis=-1, keepdims=True) + EPS)
        xh = x2 * r
        g2v = g2_ref[...]
        diff = xh * g2v - t_ref[...]
        sq_ref[...] += jnp.sum(diff * diff, axis=0, keepdims=True)
        dfin = diff * (1.0 / D_MODEL)
        gg_ref[...] += jnp.sum(dfin * xh, axis=0, keepdims=True)
        dxh = dfin * g2v
        dx2 = r * (dxh - xh * jnp.mean(dxh * xh, axis=-1, keepdims=True))
        dx2_ref[...] = dx2
        dx2_bf = dx2.astype(BF16)
        for b in range(N_CHIP):
            gw_ref[b] += _tdot(ys[b], dx2_bf)
            dy_ref[:, b * SHARD_OUT:(b + 1) * SHARD_OUT] = _dot_t(dx2_bf, w_ref[b])

        @pl.when(i == n_tiles - 1)
        def _():
            total = jnp.sum(sq_ref[...], axis=1, keepdims=True) * (0.5 / D_MODEL)
            loss_ref[...] = jnp.broadcast_to(total, loss_ref.shape)

    def rows(width):
        return pl.BlockSpec((TS, width), lambda i: (i, 0))

    return pl.pallas_call(
        body,
        name="out_loss",
        grid=(n_tiles,),
        in_specs=[rows(D_MODEL), rows(D_MODEL), rows(POOL_WIDTH), rows(ATTN_WIDTH), VMEM, VMEM],
        out_specs=[rows(D_MODEL), rows(D_MODEL), VMEM, VMEM, VMEM],
        out_shape=[
            jax.ShapeDtypeStruct((seq, D_MODEL), F32), jax.ShapeDtypeStruct((seq, D_MODEL), F32),
            jax.ShapeDtypeStruct((N_CHIP, SHARD_OUT, D_MODEL), F32), jax.ShapeDtypeStruct((1, D_MODEL), F32),
            jax.ShapeDtypeStruct((8, GROUP), F32)],
        scratch_shapes=[pltpu.VMEM((1, D_MODEL), F32)],
        compiler_params=pltpu.CompilerParams(dimension_semantics=("arbitrary",), vmem_limit_bytes=VMEM_LIMIT),
    )(x, tgt, yp, ya, wout_full, g2)


def _inproj_bwd(x, dx2, dy, pv, pg, dq, dk, dv, dag, win_full, pw_bf, g1, ps):
    seq = x.shape[0]
    n_tiles = seq // TS

    def body(x_ref, dx2_ref, dyp_ref, pv_ref, pvprev_ref, pg_ref, dq_ref, dk_ref, dv_ref, dag_ref,
             w_ref, pw_ref, g1_ref, ps_ref, gx_ref, gw_ref, gg_ref, gps_ref, gpw_ref, halo_ref):
        i = pl.program_id(0)
        tile = n_tiles - 1 - i

        @pl.when(i == 0)
        def _():
            gw_ref[...] = jnp.zeros_like(gw_ref)
            gg_ref[...] = jnp.zeros_like(gg_ref)
            gps_ref[...] = jnp.zeros_like(gps_ref)
            gpw_ref[...] = jnp.zeros_like(gpw_ref)
            halo_ref[...] = jnp.zeros_like(halo_ref)

        pv_t = pv_ref[...]
        pg_t = pg_ref[...]
        prev_rows = jnp.where(tile > 0, pvprev_ref[...], 0.0)
        diffs = [d.astype(BF16) for d in _pool_diffs(pv_t, prev_rows, tile)]
        mixed = jnp.concatenate([_dot(diffs[g], pw_ref[g]) for g in range(len(POOL_WINDOWS))], axis=1)
        sg = _sigmoid(pg_t)
        silu = pg_t * sg
        dyp = dyp_ref[...]
        psv = ps_ref[...]
        gps_ref[...] += jnp.sum(dyp * mixed * silu, axis=0, keepdims=True)
        dmixed = (dyp * psv * silu).astype(BF16)
        dpg = dyp * (mixed * psv) * (sg * (1.0 + pg_t * (1.0 - sg)))
        counts = _row_counts(tile, TS)
        dds = []
        for g in range(len(POOL_WINDOWS)):
            dm_g = dmixed[:, g * GROUP:(g + 1) * GROUP]
            gpw_ref[g] += _tdot(diffs[g], dm_g)
            dds.append(_dot_t(dm_g, pw_ref[g]))
        dd = jnp.concatenate(dds, axis=1)
        spread = jnp.concatenate([dds[g] / counts[g] for g in range(len(POOL_WINDOWS))], axis=1)
        sums = _window_sums(jnp.concatenate([spread, halo_ref[...]], axis=0), forward=True)
        halo_ref[...] = spread[:HALO]
        dpv = jnp.concatenate(
            [sums[g][:TS, g * GROUP:(g + 1) * GROUP] for g in range(len(POOL_WINDOWS))], axis=1) - dd

        xt = x_ref[...]
        r = lax.rsqrt(jnp.mean(xt * xt, axis=-1, keepdims=True) + EPS)
        xh = xt * r
        g1v = g1_ref[...]
        h = (xh * g1v).astype(BF16)
        wides = [dpv.astype(BF16), dpg.astype(BF16), dq_ref[...], dk_ref[...], dv_ref[...], dag_ref[...].astype(BF16)]
        dh = jnp.zeros((TS, D_MODEL), F32)
        for n in range(N_CHIP * PIECES_PER_SHARD):
            chip, sub = divmod(n, PIECES_PER_SHARD)
            cols = slice(sub * PIECE, (sub + 1) * PIECE)
            piece = wides[n // 2][:, (n % 2) * PIECE:(n % 2 + 1) * PIECE]
            dh = dh + _dot_t(piece, w_ref[chip, :, cols])
            gw_ref[chip, :, cols] += _tdot(h, piece)

        gg_ref[...] += jnp.sum(dh * xh, axis=0, keepdims=True)
        dxh = dh * g1v
        gx_ref[...] = dx2_ref[...] + r * (dxh - xh * jnp.mean(dxh * xh, axis=-1, keepdims=True))

    def rows(width):
        return pl.BlockSpec((TS, width), lambda i: (n_tiles - 1 - i, 0))

    prev = pl.BlockSpec((HALO, POOL_WIDTH), lambda i: (jnp.maximum((n_tiles - 1 - i) * (TS // HALO) - 1, 0), 0))
    return pl.pallas_call(
        body,
        name="inproj_bwd",
        grid=(n_tiles,),
        in_specs=[rows(D_MODEL), rows(D_MODEL), rows(POOL_WIDTH), rows(POOL_WIDTH), prev, rows(POOL_WIDTH),
                  rows(ATTN_WIDTH), rows(ATTN_WIDTH), rows(ATTN_WIDTH), rows(ATTN_WIDTH), VMEM, VMEM, VMEM, VMEM],
        out_specs=[rows(D_MODEL), VMEM, VMEM, VMEM, VMEM],
        out_shape=[
            jax.ShapeDtypeStruct((seq, D_MODEL), F32),
            jax.ShapeDtypeStruct((N_CHIP, D_MODEL, SHARD_IN), F32),
            jax.ShapeDtypeStruct((1, D_MODEL), F32),
            jax.ShapeDtypeStruct((1, POOL_WIDTH), F32),
            jax.ShapeDtypeStruct((len(POOL_WINDOWS), GROUP, GROUP), F32)],
        scratch_shapes=[pltpu.VMEM((HALO, POOL_WIDTH), F32)],
        compiler_params=pltpu.CompilerParams(dimension_semantics=("arbitrary",), vmem_limit_bytes=VMEM_LIMIT),
    )(x, dx2, dy, pv, pv, pg, dq, dk, dv, dag, win_full, pw_bf, g1, ps)


def _reduce_grads(gwin, gwout, small):
    half_small = SMALL_ROWS // 2

    def body(gwin_ref, gwout_ref, small_ref, gin_out, gout_out, small_out,
             pair_in, pair_out, pair_small, tx_in, tx_out, rx_in, rx_out, rx_small, send_sems, recv_sems):
        x, y, c = _my_place()
        b = 2 * x + y
        sibling = (x, y, 1 - c)
        mine_in = pl.ds(pl.multiple_of(c * HALF_IN, HALF_IN), HALF_IN)
        mine_out = pl.ds(pl.multiple_of(c * HALF_OUT, HALF_OUT), HALF_OUT)
        mine_small = pl.ds(pl.multiple_of(c * half_small, 8), half_small)
        theirs_in = pl.ds(pl.multiple_of((1 - c) * HALF_IN, HALF_IN), HALF_IN)
        theirs_out = pl.ds(pl.multiple_of((1 - c) * HALF_OUT, HALF_OUT), HALF_OUT)

        def copy(k, src, dst, to):
            return pltpu.make_async_remote_copy(
                src_ref=src, dst_ref=dst, send_sem=send_sems.at[k], recv_sem=recv_sems.at[k],
                device_id=to, device_id_type=MESH)

        swaps = [
            copy(0, gwin_ref.at[:, theirs_in], pair_in, sibling),
            copy(1, gwout_ref.at[:, theirs_out], pair_out, sibling),
            copy(2, small_ref, pair_small, sibling),
        ]
        for cp in swaps:
            cp.start()
        for cp in swaps:
            cp.wait_recv()
        for chip in range(N_CHIP):
            pair_in[chip] = gwin_ref[chip, mine_in] + pair_in[chip]
            pair_out[chip] = gwout_ref[chip, mine_out] + pair_out[chip]
        pair_small[...] = small_ref[...] + pair_small[...]

        rx_small[b] = pair_small[mine_small]
        sends = []
        for j, (place, chip) in enumerate(_other_chips(x, y)):
            tx_in[j] = pair_in[chip].astype(BF16)
            tx_out[j] = pair_out[chip].astype(BF16)
            to = (*place, c)
            sends += [
                copy(3 + 3 * j, tx_in.at[j], rx_in.at[j], to),
                copy(4 + 3 * j, tx_out.at[j], rx_out.at[j], to),
                copy(5 + 3 * j, pair_small.at[mine_small], rx_small.at[b], to),
            ]
        for cp in sends:
            cp.start()
        for j, (place, chip) in enumerate(_other_chips(x, y)):
            to = (*place, c)
            copy(3 + 3 * j, tx_in.at[j], rx_in.at[j], to).wait_recv()
            copy(4 + 3 * j, tx_out.at[j], rx_out.at[j], to).wait_recv()
            copy(5 + 3 * j, pair_small.at[mine_small], rx_small.at[chip], to).wait_recv()
        total_in = pair_in[b]
        total_out = pair_out[b]
        for j in range(N_CHIP - 1):
            total_in = total_in + rx_in[j].astype(F32)
            total_out = total_out + rx_out[j].astype(F32)
        gin_out[mine_in] = total_in
        gout_out[mine_out] = total_out
        small_out[mine_small] = ((rx_small[0] + rx_small[1]) + rx_small[2]) + rx_small[3]

        shares = [
            copy(12, gin_out.at[mine_in], gin_out.at[mine_in], sibling),
            copy(13, gout_out.at[mine_out], gout_out.at[mine_out], sibling),
            copy(14, small_out.at[mine_small], small_out.at[mine_small], sibling),
        ]
        for cp in shares:
            cp.start()
        theirs_small = pl.ds(pl.multiple_of((1 - c) * half_small, 8), half_small)
        copy(12, gin_out.at[theirs_in], gin_out.at[theirs_in], sibling).wait_recv()
        copy(13, gout_out.at[theirs_out], gout_out.at[theirs_out], sibling).wait_recv()
        copy(14, small_out.at[theirs_small], small_out.at[theirs_small], sibling).wait_recv()
        for cp in swaps + sends + shares:
            cp.wait_send()

    return pl.pallas_call(
        body,
        name="reduce_grads",
        out_shape=(
            jax.ShapeDtypeStruct((D_MODEL, SHARD_IN), F32),
            jax.ShapeDtypeStruct((SHARD_OUT, D_MODEL), F32),
            jax.ShapeDtypeStruct((SMALL_ROWS, GROUP), F32),
        ),
        in_specs=[VMEM, VMEM, VMEM],
        out_specs=(VMEM, VMEM, VMEM),
        scratch_shapes=[
            pltpu.VMEM((N_CHIP, HALF_IN, SHARD_IN), F32),
            pltpu.VMEM((N_CHIP, HALF_OUT, D_MODEL), F32),
            pltpu.VMEM((SMALL_ROWS, GROUP), F32),
            pltpu.VMEM((N_CHIP - 1, HALF_IN, SHARD_IN), BF16),
            pltpu.VMEM((N_CHIP - 1, HALF_OUT, D_MODEL), BF16),
            pltpu.VMEM((N_CHIP - 1, HALF_IN, SHARD_IN), BF16),
            pltpu.VMEM((N_CHIP - 1, HALF_OUT, D_MODEL), BF16),
            pltpu.VMEM((N_CHIP, half_small, GROUP), F32),
            pltpu.SemaphoreType.DMA((15,)),
            pltpu.SemaphoreType.DMA((15,)),
        ],
        compiler_params=pltpu.CompilerParams(vmem_limit_bytes=VMEM_LIMIT),
    )(gwin, gwout, small)


def _adamw(name, w, g, m, v, block_rows):
    rows, cols = w.shape

    def body(w_ref, g_ref, m_ref, v_ref, d_ref, m_out, v_out):
        grad = g_ref[...]
        m_new = ADAM_B1 * m_ref[...] + (1.0 - ADAM_B1) * grad
        v_new = ADAM_B2 * v_ref[...] + (1.0 - ADAM_B2) * (grad * grad)
        m_hat = m_new / (1.0 - ADAM_B1 ** ADAM_STEP)
        v_hat = v_new / (1.0 - ADAM_B2 ** ADAM_STEP)
        d_ref[...] = -ADAM_LR * (m_hat / (jnp.sqrt(v_hat) + ADAM_EPS) + ADAM_WD * w_ref[...])
        m_out[...] = m_new
        v_out[...] = v_new

    blk = pl.BlockSpec((block_rows, cols), lambda i: (i, 0))
    shape = jax.ShapeDtypeStruct((rows, cols), F32)
    return pl.pallas_call(
        body,
        name=name,
        grid=(rows // block_rows,),
        in_specs=[blk] * 4,
        out_specs=[blk] * 3,
        out_shape=[shape] * 3,
        compiler_params=pltpu.CompilerParams(dimension_semantics=("arbitrary",)),
    )(w, g, m, v)


def _pack_small(norm_gain, pool_w, pool_scale, rel_bias, final_gain, loss_rows):
    parts = [
        norm_gain.reshape(8, GROUP),
        pool_w.reshape(len(POOL_WINDOWS) * GROUP, GROUP),
        jnp.pad(pool_scale.reshape(4, GROUP), ((0, 4), (0, 0))),
        jnp.pad(rel_bias.reshape(8, N_REL), ((0, 0), (0, 2 * GROUP - N_REL))).reshape(16, GROUP),
        final_gain.reshape(8, GROUP),
        loss_rows,
    ]
    return jnp.concatenate(parts, axis=0)


def _unpack_small(block):
    norm_gain = block[0:8].reshape(1, D_MODEL)
    pool_w = block[8:520].reshape(1, len(POOL_WINDOWS), GROUP, GROUP)
    pool_scale = block[520:524].reshape(1, POOL_WIDTH)
    rel_bias = block[528:544].reshape(8, 2 * GROUP)[:, :N_REL].reshape(1, 8, N_REL)
    final_gain = block[544:552].reshape(D_MODEL)
    return norm_gain, pool_w, pool_scale, rel_bias, final_gain


def kernel(x, norm_gain, w_in, pool_w, pool_scale, rel_bias, w_out, final_norm_gain, loss_target, m_norm_gain, m_w_in, m_pool_w, m_pool_scale, m_rel_bias, m_w_out, m_final_norm_gain, v_norm_gain, v_w_in, v_pool_w, v_pool_scale, v_rel_bias, v_w_out, v_final_norm_gain):
    assert x.shape[1] % TS == 0 and x.shape[2] == D_MODEL
    xs = x[0]
    tgt = loss_target[0]
    g1 = norm_gain.reshape(1, D_MODEL)
    g2 = final_norm_gain.reshape(1, D_MODEL)
    ps = pool_scale.reshape(1, POOL_WIDTH)
    pw_bf = pool_w[0].astype(BF16)

    win_full, wout_full = _gather_weights(w_in[0].astype(BF16), w_out[0].astype(BF16))
    bias_t = _bias_tile(rel_bias[0])

    pv, pg, ag, qs, k, v, yp = _fwd_inproj(xs, g1, win_full, pw_bf, ps)
    a, ya = _attn_fwd(qs, k, v, ag, bias_t)
    dx2, dy, gwout, gg2, loss_rows = _out_loss(xs, tgt, yp, ya, wout_full, g2)
    dq, dk, dv, dag, db_t = _attn_bwd(qs, k, v, a, ag, dy, bias_t)
    bins = _bias_bins(db_t)
    gx, gwin, gg1, gps, gpw = _inproj_bwd(xs, dx2, dy, pv, pg, dq, dk, dv, dag, win_full, pw_bf, g1, ps)

    g_bias = bins[:, :N_REL, :2].transpose(0, 2, 1).reshape(8, N_REL)
    small = _pack_small(gg1, gpw, gps, g_bias, gg2, loss_rows)
    g_win, g_wout, g_small = _reduce_grads(gwin, gwout, small)
    loss = g_small[LOSS_ROW, 0]

    zeros8 = jnp.zeros((8, GROUP), F32)
    w_small = _pack_small(norm_gain, pool_w, pool_scale, rel_bias, final_norm_gain, zeros8)
    m_small = _pack_small(m_norm_gain, m_pool_w, m_pool_scale, m_rel_bias, m_final_norm_gain, zeros8)
    v_small = _pack_small(v_norm_gain, v_pool_w, v_pool_scale, v_rel_bias, v_final_norm_gain, zeros8)

    d_win, m_win, v_win = _adamw("adamw_w_in", w_in[0], g_win, m_w_in[0], v_w_in[0], 256)
    d_wout, m_wout, v_wout = _adamw("adamw_w_out", w_out[0], g_wout, m_w_out[0], v_w_out[0], 128)
    d_small, m_new_small, v_new_small = _adamw("adamw_small", w_small, g_small, m_small, v_small, SMALL_ROWS // 2)

    def full(win_part, wout_part, block):
        ng, pw, psc, rb, fg = _unpack_small(block)
        return [ng, win_part[None], pw, psc, rb, wout_part[None], fg]

    grads = full(g_win, g_wout, g_small)
    deltas = full(d_win, d_wout, d_small)
    new_m = full(m_win, m_wout, m_new_small)
    new_v = full(v_win, v_wout, v_new_small)
    return (loss, gx[None], *grads, *deltas, *new_m, *new_v)
```

```python
import numpy as np
import jax
import jax.numpy as jnp
from jax import lax
from jax.experimental import pallas as pl
from jax.experimental.pallas import tpu as pltpu

F32 = jnp.float32
BF16 = jnp.bfloat16

D_MODEL = 1024
POOL_WIDTH = 512
ATTN_WIDTH = 512
POOL_WINDOWS = (2, 4, 8, 16)
GROUP = 128
CHUNK = 64
LEFT_CHUNKS = 8
BAND = (LEFT_CHUNKS + 1) * CHUNK
HEAD_DIM = 64
N_PAIR = 4
MAX_REL = 64
N_REL = 2 * MAX_REL + 1
EPS = 1e-6
MASK_VALUE = -1e30
SCALE = 0.125

ADAM_LR = 0.001
ADAM_B1 = 0.9
ADAM_B2 = 0.999
ADAM_EPS = 1e-08
ADAM_WD = 0.01
ADAM_STEP = 10

TS = LEFT_CHUNKS * CHUNK
SUPER = 2 * CHUNK
WINDOW = BAND + CHUNK
SUPERS_PER_TILE = TS // SUPER
PAIR_LANES = 2 * SUPER
HALO = 16
N_CHIP = 4
SHARD_IN = 768
SHARD_OUT = 256
PIECE = 256
PIECES_PER_SHARD = SHARD_IN // PIECE
HALF_IN = D_MODEL // 2
HALF_OUT = SHARD_OUT // 2
SMALL_ROWS = 560
LOSS_ROW = 552
VMEM_LIMIT = 56 * 1024 * 1024

MESH = pl.DeviceIdType.MESH
ANY = pl.BlockSpec(memory_space=pl.ANY)
VMEM = pl.BlockSpec(memory_space=pltpu.VMEM)


def _sigmoid(x):
    return 1.0 / (1.0 + jnp.exp(-x))


def _dot(a, b):
    return jnp.dot(a, b, preferred_element_type=F32)


def _dot_t(a, b):
    return lax.dot_general(a, b, (((1,), (1,)), ((), ())), preferred_element_type=F32)


def _tdot(a, b):
    return lax.dot_general(a, b, (((0,), (0,)), ((), ())), preferred_element_type=F32)


def _my_place():
    return lax.axis_index("x"), lax.axis_index("y"), lax.axis_index("c")


def _other_chips(x, y):
    places = [(1 - x, y), (x, 1 - y), (1 - x, 1 - y)]
    return [(p, 2 * p[0] + p[1]) for p in places]


def _gather_inproj(x, g1, win_sh, wout_sh):
    seq = x.shape[0]
    n_tiles = seq // TS
    cx, cy = lax.axis_index("x"), lax.axis_index("y")
    order = jnp.stack([2 * cx + cy] + [chip for _, chip in _other_chips(cx, cy)]).astype(jnp.int32)

    def body(order_ref, x_ref, g1_ref, win_ref, wout_ref, proj_ref, win_full, wout_full,
             hbuf, wbuf, send_sems, recv_sems, local_sems):
        j = pl.program_id(0)
        i = pl.program_id(1)
        x_, y_, c = _my_place()
        b = 2 * x_ + y_
        sibling = (x_, y_, 1 - c)
        others = _other_chips(x_, y_)

        def halves(chip, core):
            return (
                win_full.at[chip, pl.ds(core * HALF_IN, HALF_IN)],
                wout_full.at[chip, pl.ds(core * HALF_OUT, HALF_OUT)],
            )

        def copy(k, src, dst, to):
            return pltpu.make_async_remote_copy(
                src_ref=src, dst_ref=dst, send_sem=send_sems.at[k], recv_sem=recv_sems.at[k],
                device_id=to, device_id_type=MESH)

        own = [
            pltpu.make_async_copy(win_ref, win_full.at[b], local_sems.at[0]),
            pltpu.make_async_copy(wout_ref, wout_full.at[b], local_sems.at[1]),
        ]
        mine_src = (win_ref.at[pl.ds(c * HALF_IN, HALF_IN)], wout_ref.at[pl.ds(c * HALF_OUT, HALF_OUT)])

        def direct(n, t):
            return copy(2 * n + t, mine_src[t], halves(b, c)[t], (*others[n][0], c))

        def arrival(n, t):
            landed = halves(others[n][1], c)[t]
            return copy(2 * n + t, landed, landed, (*others[n][0], c))

        def passing(n, t):
            landed = halves(others[n][1], c)[t]
            return copy(6 + 2 * n + t, landed, landed, sibling)

        def from_sibling(n, t):
            landed = halves(others[n][1], 1 - c)[t]
            return copy(6 + 2 * n + t, landed, landed, sibling)

        @pl.when((j == 0) & (i == 0))
        def _():
            first = pltpu.make_async_copy(win_ref, wbuf, local_sems.at[2])
            first.start()
            for cp in own:
                cp.start()
            for t in range(2):
                for n in range(N_CHIP - 1):
                    direct(n, t).start()
            first.wait()

        for n in range(N_CHIP - 1):
            @pl.when((j == n + 1) & (i == 0))
            def _(n=n):
                arrival(n, 0).wait_recv()
                passing(n, 0).start()
                from_sibling(n, 0).wait_recv()
                load = pltpu.make_async_copy(win_full.at[others[n][1]], wbuf, local_sems.at[2])
                load.start()
                load.wait()

        @pl.when((j == N_CHIP - 1) & (i == 0))
        def _():
            for n in range(N_CHIP - 1):
                arrival(n, 1).wait_recv()
                passing(n, 1).start()

        rows = pl.ds(pl.multiple_of(i * TS, TS), TS)

        @pl.when(j == 0)
        def _():
            xt = x_ref[...]
            r = lax.rsqrt(jnp.mean(xt * xt, axis=-1, keepdims=True) + EPS)
            hbuf[rows] = ((xt * r) * g1_ref[...]).astype(BF16)

        proj_ref[...] = _dot(hbuf[rows], wbuf[...])

        @pl.when((j == N_CHIP - 1) & (i == n_tiles - 1))
        def _():
            for n in range(N_CHIP - 1):
                from_sibling(n, 1).wait_recv()
            for n in range(N_CHIP - 1):
                for t in range(2):
                    direct(n, t).wait_send()
                    passing(n, t).wait_send()
            for cp in own:
                cp.wait()

    last = n_tiles - 1
    grid_spec = pltpu.PrefetchScalarGridSpec(
        num_scalar_prefetch=1,
        grid=(N_CHIP, n_tiles),
        in_specs=[
            pl.BlockSpec((TS, D_MODEL), lambda j, i, o: (jnp.where(j == 0, i, last), 0)),
            VMEM, ANY, ANY],
        out_specs=[pl.BlockSpec((TS, SHARD_IN), lambda j, i, o: (i, o[j])), ANY, ANY],
        scratch_shapes=[
            pltpu.VMEM((seq, D_MODEL), BF16), pltpu.VMEM((D_MODEL, SHARD_IN), BF16),
            pltpu.SemaphoreType.DMA((12,)), pltpu.SemaphoreType.DMA((12,)), pltpu.SemaphoreType.DMA((3,))],
    )
    return pl.pallas_call(
        body,
        name="gather_inproj",
        grid_spec=grid_spec,
        out_shape=(
            jax.ShapeDtypeStruct((seq, N_CHIP * SHARD_IN), F32),
            jax.ShapeDtypeStruct((N_CHIP, D_MODEL, SHARD_IN), BF16),
            jax.ShapeDtypeStruct((N_CHIP, SHARD_OUT, D_MODEL), BF16),
        ),
        compiler_params=pltpu.CompilerParams(
            dimension_semantics=("arbitrary", "arbitrary"), vmem_limit_bytes=VMEM_LIMIT),
    )(order, x, g1, win_sh, wout_sh)


def _window_sums(ext, forward):
    n = ext.shape[0]
    sums = []
    acc = ext
    for step in (1, 2, 4, 8):
        acc = acc + pltpu.roll(acc, (n - step) if forward else step, 0)
        sums.append(acc)
    return sums


def _row_counts(tile, rows):
    t = tile * TS + lax.broadcasted_iota(jnp.int32, (rows, GROUP), 0)
    return [jnp.minimum(t + 1, w).astype(F32) for w in POOL_WINDOWS]


def _pool_diffs(pv, prev_rows, tile):
    ext = jnp.concatenate([prev_rows, pv], axis=0)
    sums = _window_sums(ext, forward=False)
    counts = _row_counts(tile, TS)
    out = []
    for g in range(len(POOL_WINDOWS)):
        cols = slice(g * GROUP, (g + 1) * GROUP)
        out.append(sums[g][HALO:, cols] / counts[g] - pv[:, cols])
    return out


def _pool_mix(diffs, pw_ref):
    return jnp.concatenate([_dot(diffs[g], pw_ref[g]) for g in range(len(POOL_WINDOWS))], axis=1)


Q_BLOCK, K_BLOCK, V_BLOCK, AG_BLOCK = 8, 12, 16, 20


def _bias_tile(rel_bias):
    flat = jnp.concatenate(
        [jnp.broadcast_to(rel_bias[:, :1], (rel_bias.shape[0], BAND - CHUNK - 1)), rel_bias[:, :2 * MAX_REL]], axis=1)
    rows = [flat[:, CHUNK - 1 - i:CHUNK - 1 - i + BAND] for i in range(CHUNK)]
    bias = jnp.stack(rows, axis=1)
    first = jnp.pad(bias, ((0, 0), (0, 0), (0, CHUNK)), constant_values=MASK_VALUE)
    second = jnp.pad(bias, ((0, 0), (0, 0), (CHUNK, 0)), constant_values=MASK_VALUE)
    both = jnp.concatenate([first, second], axis=1)
    return both.reshape(N_PAIR, PAIR_LANES, WINDOW).transpose(0, 2, 1)


def _rel_index_tile():
    j = np.arange(WINDOW)[:, None]
    q = np.arange(PAIR_LANES)[None, :] % SUPER
    band_key = j - CHUNK * (q // CHUNK)
    idx = np.clip(band_key - LEFT_CHUNKS * CHUNK - q % CHUNK, -MAX_REL, MAX_REL) + MAX_REL
    return np.where((band_key >= 0) & (band_key < BAND), idx, -1).astype(np.int32)


def _by_head(block):
    low = lax.broadcasted_iota(jnp.int32, block.shape, 1) < HEAD_DIM
    zero = jnp.zeros_like(block)
    return jnp.concatenate([jnp.where(low, block, zero), jnp.where(low, zero, block)], axis=0)


def _own_head_rows(cross):
    head_of_row = lax.broadcasted_iota(jnp.int32, cross.shape, 0) >= HEAD_DIM
    head_of_lane = lax.broadcasted_iota(jnp.int32, cross.shape, 1) >= SUPER
    both = jnp.where(jnp.logical_xor(head_of_row, head_of_lane), 0.0, cross).T
    return both[:SUPER] + both[SUPER:]


def _with_mask_lane(q_rows):
    lane = lax.broadcasted_iota(jnp.int32, q_rows.shape, 1)
    return jnp.concatenate([q_rows, jnp.where(lane == 0, MASK_VALUE, 0.0).astype(q_rows.dtype)], axis=1)


def _band_exp(kb, q_rows, bias):
    s = _dot_t(kb, _with_mask_lane(q_rows)) + bias
    e = jnp.exp(s - jnp.max(s, axis=0, keepdims=True))
    return e, jnp.sum(e, axis=0, keepdims=True)


def _shift_band(i, band_ref, new_ref):
    @pl.when(i == 0)
    def _():
        band_ref[:TS] = jnp.zeros((TS, GROUP), band_ref.dtype)

    @pl.when(i > 0)
    def _():
        band_ref[:TS] = band_ref[TS:]

    band_ref[TS:] = new_ref[...].astype(band_ref.dtype)


def _shift_key_band(i, band_ref, new_ref):
    @pl.when(i == 0)
    def _():
        lane = lax.broadcasted_iota(jnp.int32, (TS, 2 * GROUP), 1)
        band_ref[:TS] = jnp.where(lane == GROUP, 1.0, 0.0).astype(band_ref.dtype)
        band_ref[TS:, GROUP:] = jnp.zeros((TS, GROUP), band_ref.dtype)

    @pl.when(i > 0)
    def _():
        band_ref[:TS] = band_ref[TS:]

    band_ref[TS:, :GROUP] = new_ref[...].astype(band_ref.dtype)


def _shift_band_t(i, band_ref, new_ref):
    @pl.when(i == 0)
    def _():
        band_ref[:, :TS] = jnp.zeros((GROUP, TS), band_ref.dtype)

    @pl.when(i > 0)
    def _():
        band_ref[:, :TS] = band_ref[:, TS:]

    band_ref[:, TS:] = new_ref[...].astype(band_ref.dtype).T


def _scaled_queries(q_ref, rows):
    return _by_head((q_ref[rows] * SCALE).astype(BF16))


def _attn_fwd(proj, bias_t):
    seq = proj.shape[0]
    n_tiles = seq // TS

    def body(q_ref, k_ref, v_ref, ag_ref, bias_ref, a_ref, ya_ref, kband, vband_t):
        i = pl.program_id(1)
        _shift_key_band(i, kband, k_ref)
        _shift_band_t(i, vband_t, v_ref)

        def weights(sc):
            rows = slice(sc * SUPER, (sc + 1) * SUPER)
            win = slice(sc * SUPER, sc * SUPER + WINDOW)
            e, total = _band_exp(kband[win], _scaled_queries(q_ref, rows), bias_ref[0])
            return e.astype(BF16), 1.0 / total

        nxt = weights(0)
        for sc in range(SUPERS_PER_TILE):
            rows = slice(sc * SUPER, (sc + 1) * SUPER)
            win = slice(sc * SUPER, sc * SUPER + WINDOW)
            e_bf, inv_total = nxt
            if sc + 1 < SUPERS_PER_TILE:
                nxt = weights(sc + 1)
            a = _own_head_rows(_dot(vband_t[:, win], e_bf) * inv_total)
            a_ref[rows] = a
            g = ag_ref[rows]
            ya_ref[rows] = (a * (g * _sigmoid(g))).astype(BF16)

    blk = pl.BlockSpec((TS, GROUP), lambda p, i: (i, p))

    def cols(first):
        return pl.BlockSpec((TS, GROUP), lambda p, i: (i, first + p))

    return pl.pallas_call(
        body,
        name="attn_fwd",
        grid=(N_PAIR, n_tiles),
        in_specs=[cols(Q_BLOCK), cols(K_BLOCK), cols(V_BLOCK), cols(AG_BLOCK),
                  pl.BlockSpec((1, WINDOW, PAIR_LANES), lambda p, i: (p, 0, 0))],
        out_specs=[blk, blk],
        out_shape=[jax.ShapeDtypeStruct((seq, ATTN_WIDTH), F32), jax.ShapeDtypeStruct((seq, ATTN_WIDTH), BF16)],
        scratch_shapes=[pltpu.VMEM((2 * TS, 2 * GROUP), BF16), pltpu.VMEM((GROUP, 2 * TS), BF16)],
        compiler_params=pltpu.CompilerParams(
            dimension_semantics=("arbitrary", "arbitrary"), vmem_limit_bytes=VMEM_LIMIT),
    )(proj, proj, proj, proj, bias_t)


def _attn_bwd(proj, a, dy, bias_t):
    seq = proj.shape[0]
    n_tiles = seq // TS

    def body(q_ref, k_ref, v_ref, a_ref, ag_ref, dy_ref, bias_ref,
             dq_ref, dk_ref, dv_ref, dag_ref, db_ref, kband, vband, kband_t, dkacc, dvacc):
        i = pl.program_id(1)

        @pl.when(i == 0)
        def _():
            dkacc[...] = jnp.zeros_like(dkacc)
            dvacc[...] = jnp.zeros_like(dvacc)
            db_ref[...] = jnp.zeros_like(db_ref)

        @pl.when(i < n_tiles)
        def _():
            _shift_key_band(i, kband, k_ref)
            _shift_band(i, vband, v_ref)
            _shift_band_t(i, kband_t, k_ref)
            def score_grads(sc):
                rows = slice(sc * SUPER, (sc + 1) * SUPER)
                win = slice(sc * SUPER, sc * SUPER + WINDOW)
                q_rows = _scaled_queries(q_ref, rows)
                g = ag_ref[rows]
                sg = _sigmoid(g)
                dyc = dy_ref[rows]
                dag_ref[rows] = dyc * a_ref[rows] * (sg * (1.0 + g * (1.0 - sg)))
                da_rows = _by_head((dyc * (g * sg)).astype(BF16))
                e, total = _band_exp(kband[win], q_rows, bias_ref[0])
                p = e * (1.0 / total)
                dp = _dot_t(vband[win], da_rows)
                ds = p * (dp - jnp.sum(p * dp, axis=0, keepdims=True))
                db_ref[0] += ds
                return q_rows, da_rows, p.astype(BF16), ds.astype(BF16)

            nxt = score_grads(0)
            for sc in range(SUPERS_PER_TILE):
                rows = slice(sc * SUPER, (sc + 1) * SUPER)
                win = slice(sc * SUPER, sc * SUPER + WINDOW)
                q_rows, da_rows, p_bf, ds_bf = nxt
                if sc + 1 < SUPERS_PER_TILE:
                    nxt = score_grads(sc + 1)
                dq_ref[rows] = (_own_head_rows(_dot(kband_t[:, win], ds_bf)) * SCALE).astype(BF16)
                dkacc[win] += _dot(ds_bf, q_rows)
                dvacc[win] += _dot(p_bf, da_rows)

        dk_ref[...] = dkacc[:TS].astype(BF16)
        dv_ref[...] = dvacc[:TS].astype(BF16)
        dkacc[:TS] = dkacc[TS:]
        dvacc[:TS] = dvacc[TS:]
        dkacc[TS:] = jnp.zeros((TS, GROUP), F32)
        dvacc[TS:] = jnp.zeros((TS, GROUP), F32)

    last = n_tiles - 1
    cur = pl.BlockSpec((TS, GROUP), lambda p, i: (jnp.minimum(i, last), p))
    older = pl.BlockSpec((TS, GROUP), lambda p, i: (jnp.maximum(i - 1, 0), p))
    dy_blk = pl.BlockSpec((TS, GROUP), lambda p, i: (jnp.minimum(i, last), N_PAIR + p))
    per_pair = pl.BlockSpec((1, WINDOW, PAIR_LANES), lambda p, i: (p, 0, 0))

    def cols(first):
        return pl.BlockSpec((TS, GROUP), lambda p, i: (jnp.minimum(i, last), first + p))

    def out(dtype):
        return jax.ShapeDtypeStruct((seq, ATTN_WIDTH), dtype)

    return pl.pallas_call(
        body,
        name="attn_bwd",
        grid=(N_PAIR, n_tiles + 1),
        in_specs=[cols(Q_BLOCK), cols(K_BLOCK), cols(V_BLOCK), cur, cols(AG_BLOCK), dy_blk, per_pair],
        out_specs=[cur, older, older, cur, per_pair],
        out_shape=[out(BF16), out(BF16), out(BF16), out(F32),
                   jax.ShapeDtypeStruct((N_PAIR, WINDOW, PAIR_LANES), F32)],
        scratch_shapes=[
            pltpu.VMEM((2 * TS, 2 * GROUP), BF16), pltpu.VMEM((2 * TS, GROUP), BF16), pltpu.VMEM((GROUP, 2 * TS), BF16),
            pltpu.VMEM((2 * TS, GROUP), F32), pltpu.VMEM((2 * TS, GROUP), F32)],
        compiler_params=pltpu.CompilerParams(
            dimension_semantics=("arbitrary", "arbitrary"), vmem_limit_bytes=VMEM_LIMIT),
    )(proj, proj, proj, a, proj, dy, bias_t)


BIN_ROWS = 136


def _bias_bins(db_t):
    idx_t = jnp.asarray(_rel_index_tile())

    def body(db_ref, idx_ref, out_ref):
        lane = lax.broadcasted_iota(jnp.int32, (1, GROUP), 1)
        row = lax.broadcasted_iota(jnp.int32, (BIN_ROWS, GROUP), 0)
        out = jnp.zeros((BIN_ROWS, GROUP), F32)
        for r in range(N_REL - 1):
            lo = 0 if r == 0 else ((BAND - 2 * CHUNK + r) // 8) * 8
            hi = WINDOW if r == 0 else min(WINDOW, lo + SUPER + 8)
            hit = jnp.where(idx_ref[lo:hi] == r, db_ref[0, lo:hi], 0.0)
            col = jnp.sum(hit, axis=0, keepdims=True)
            s0 = jnp.sum(col[:, :SUPER], axis=1, keepdims=True)
            s1 = jnp.sum(col[:, SUPER:], axis=1, keepdims=True)
            val = jnp.where(lane == 0, s0, jnp.where(lane == 1, s1, 0.0))
            out = jnp.where(row == r, val, out)
        out_ref[0] = out

    return pl.pallas_call(
        body,
        name="bias_bins",
        grid=(N_PAIR,),
        in_specs=[pl.BlockSpec((1, WINDOW, PAIR_LANES), lambda p: (p, 0, 0)), VMEM],
        out_specs=pl.BlockSpec((1, BIN_ROWS, GROUP), lambda p: (p, 0, 0)),
        out_shape=jax.ShapeDtypeStruct((N_PAIR, BIN_ROWS, GROUP), F32),
        compiler_params=pltpu.CompilerParams(dimension_semantics=("arbitrary",)),
    )(db_t, idx_t)


def _out_loss(x, tgt, proj, ya, wout_full, g2, pw_bf, ps):
    seq = x.shape[0]
    n_tiles = seq // TS
    half = TS // 2

    def body(x_ref, t_ref, pv_ref, pg_ref, ya_ref, w_ref, g2_ref, pw_ref, ps_ref,
             dx2_ref, dy_ref, gw_ref, gg_ref, loss_ref, sq_ref, halo_ref):
        i = pl.program_id(0)

        @pl.when(i == 0)
        def _():
            gw_ref[...] = jnp.zeros_like(gw_ref)
            gg_ref[...] = jnp.zeros_like(gg_ref)
            sq_ref[...] = jnp.zeros_like(sq_ref)
            halo_ref[...] = jnp.zeros_like(halo_ref)

        pv = pv_ref[...]
        pg = pg_ref[...]
        diffs = [d.astype(BF16) for d in _pool_diffs(pv, halo_ref[...], i)]
        halo_ref[...] = pv[TS - HALO:, :]
        yp = ((_pool_mix(diffs, pw_ref) * ps_ref[...]) * (pg * _sigmoid(pg))).astype(BF16)
        g2v = g2_ref[...]

        def parts(rows):
            return [yp[rows, :SHARD_OUT], yp[rows, SHARD_OUT:], ya_ref[rows, :SHARD_OUT], ya_ref[rows, SHARD_OUT:]]

        def project(rows, ys):
            x2 = x_ref[rows]
            for b in range(N_CHIP):
                x2 = x2 + _dot(ys[b], w_ref[b])
            return x2

        def norm_loss(rows, x2):
            r = lax.rsqrt(jnp.mean(x2 * x2, axis=-1, keepdims=True) + EPS)
            xh = x2 * r
            diff = xh * g2v - t_ref[rows]
            sq_ref[...] += jnp.sum(diff * diff, axis=0, keepdims=True)
            dfin = diff * (1.0 / D_MODEL)
            gg_ref[...] += jnp.sum(dfin * xh, axis=0, keepdims=True)
            dxh = dfin * g2v
            dx2 = r * (dxh - xh * jnp.mean(dxh * xh, axis=-1, keepdims=True))
            dx2_ref[rows] = dx2
            return dx2.astype(BF16)

        def back(rows, ys, dx2_bf):
            for b in range(N_CHIP):
                gw_ref[b] += _tdot(ys[b], dx2_bf)
                dy_ref[rows, b * SHARD_OUT:(b + 1) * SHARD_OUT] = _dot_t(dx2_bf, w_ref[b])

        top, bottom = slice(0, half), slice(half, TS)
        ys_top, ys_bottom = parts(top), parts(bottom)
        x2_top = project(top, ys_top)
        x2_bottom = project(bottom, ys_bottom)
        d_top = norm_loss(top, x2_top)
        back(top, ys_top, d_top)
        d_bottom = norm_loss(bottom, x2_bottom)
        back(bottom, ys_bottom, d_bottom)

        @pl.when(i == n_tiles - 1)
        def _():
            total = jnp.sum(sq_ref[...], axis=1, keepdims=True) * (0.5 / D_MODEL)
            loss_ref[...] = jnp.broadcast_to(total, loss_ref.shape)

    def rows(width, col=0):
        return pl.BlockSpec((TS, width), lambda i: (i, col))

    return pl.pallas_call(
        body,
        name="out_loss",
        grid=(n_tiles,),
        in_specs=[rows(D_MODEL), rows(D_MODEL), rows(POOL_WIDTH, 0), rows(POOL_WIDTH, 1), rows(ATTN_WIDTH),
                  VMEM, VMEM, VMEM, VMEM],
        out_specs=[rows(D_MODEL), rows(D_MODEL), VMEM, VMEM, VMEM],
        out_shape=[
            jax.ShapeDtypeStruct((seq, D_MODEL), F32), jax.ShapeDtypeStruct((seq, D_MODEL), F32),
            jax.ShapeDtypeStruct((N_CHIP, SHARD_OUT, D_MODEL), F32), jax.ShapeDtypeStruct((1, D_MODEL), F32),
            jax.ShapeDtypeStruct((8, GROUP), F32)],
        scratch_shapes=[pltpu.VMEM((1, D_MODEL), F32), pltpu.VMEM((HALO, POOL_WIDTH), F32)],
        compiler_params=pltpu.CompilerParams(dimension_semantics=("arbitrary",), vmem_limit_bytes=VMEM_LIMIT),
    )(x, tgt, proj, proj, ya, wout_full, g2, pw_bf, ps)


def _inproj_bwd(x, dx2, dy, proj, dq, dk, dv, dag, win_full, pw_bf, g1, ps):
    seq = x.shape[0]
    n_tiles = seq // TS

    def body(x_ref, dx2_ref, dyp_ref, pv_ref, pvprev_ref, pg_ref, dq_ref, dk_ref, dv_ref, dag_ref,
             w_ref, pw_ref, g1_ref, ps_ref, gx_ref, gw_ref, gg_ref, gps_ref, gpw_ref, halo_ref):
        i = pl.program_id(0)
        tile = n_tiles - 1 - i

        @pl.when(i == 0)
        def _():
            gw_ref[...] = jnp.zeros_like(gw_ref)
            gg_ref[...] = jnp.zeros_like(gg_ref)
            gps_ref[...] = jnp.zeros_like(gps_ref)
            gpw_ref[...] = jnp.zeros_like(gpw_ref)
            halo_ref[...] = jnp.zeros_like(halo_ref)

        pv_t = pv_ref[...]
        pg_t = pg_ref[...]
        prev_rows = jnp.where(tile > 0, pvprev_ref[...], 0.0)
        diffs = [d.astype(BF16) for d in _pool_diffs(pv_t, prev_rows, tile)]
        mixed = _pool_mix(diffs, pw_ref)
        sg = _sigmoid(pg_t)
        silu = pg_t * sg
        dyp = dyp_ref[...]
        psv = ps_ref[...]
        gps_ref[...] += jnp.sum(dyp * mixed * silu, axis=0, keepdims=True)
        dmixed = (dyp * psv * silu).astype(BF16)
        dpg = dyp * (mixed * psv) * (sg * (1.0 + pg_t * (1.0 - sg)))
        counts = _row_counts(tile, TS)
        dds = []
        for g in range(len(POOL_WINDOWS)):
            dm_g = dmixed[:, g * GROUP:(g + 1) * GROUP]
            gpw_ref[g] += _tdot(diffs[g], dm_g)
            dds.append(_dot_t(dm_g, pw_ref[g]))
        dd = jnp.concatenate(dds, axis=1)
        spread = jnp.concatenate([dds[g] / counts[g] for g in range(len(POOL_WINDOWS))], axis=1)
        sums = _window_sums(jnp.concatenate([spread, halo_ref[...]], axis=0), forward=True)
        halo_ref[...] = spread[:HALO]
        dpv = jnp.concatenate(
            [sums[g][:TS, g * GROUP:(g + 1) * GROUP] for g in range(len(POOL_WINDOWS))], axis=1) - dd

        xt = x_ref[...]
        r = lax.rsqrt(jnp.mean(xt * xt, axis=-1, keepdims=True) + EPS)
        xh = xt * r
        g1v = g1_ref[...]
        h = (xh * g1v).astype(BF16)
        wides = [dpv.astype(BF16), dpg.astype(BF16), dq_ref[...], dk_ref[...], dv_ref[...], dag_ref[...].astype(BF16)]
        dh = jnp.zeros((TS, D_MODEL), F32)
        for n in range(N_CHIP * PIECES_PER_SHARD):
            chip, sub = divmod(n, PIECES_PER_SHARD)
            cols = slice(sub * PIECE, (sub + 1) * PIECE)
            piece = wides[n // 2][:, (n % 2) * PIECE:(n % 2 + 1) * PIECE]
            dh = dh + _dot_t(piece, w_ref[chip, :, cols])
            gw_ref[chip, :, cols] += _tdot(h, piece)

        gg_ref[...] += jnp.sum(dh * xh, axis=0, keepdims=True)
        dxh = dh * g1v
        gx_ref[...] = dx2_ref[...] + r * (dxh - xh * jnp.mean(dxh * xh, axis=-1, keepdims=True))

    def rows(width, col=0):
        return pl.BlockSpec((TS, width), lambda i: (n_tiles - 1 - i, col))

    prev = pl.BlockSpec((HALO, POOL_WIDTH), lambda i: (jnp.maximum((n_tiles - 1 - i) * (TS // HALO) - 1, 0), 0))
    return pl.pallas_call(
        body,
        name="inproj_bwd",
        grid=(n_tiles,),
        in_specs=[rows(D_MODEL), rows(D_MODEL), rows(POOL_WIDTH), rows(POOL_WIDTH), prev, rows(POOL_WIDTH, 1),
                  rows(ATTN_WIDTH), rows(ATTN_WIDTH), rows(ATTN_WIDTH), rows(ATTN_WIDTH), VMEM, VMEM, VMEM, VMEM],
        out_specs=[rows(D_MODEL), VMEM, VMEM, VMEM, VMEM],
        out_shape=[
            jax.ShapeDtypeStruct((seq, D_MODEL), F32),
            jax.ShapeDtypeStruct((N_CHIP, D_MODEL, SHARD_IN), F32),
            jax.ShapeDtypeStruct((1, D_MODEL), F32),
            jax.ShapeDtypeStruct((1, POOL_WIDTH), F32),
            jax.ShapeDtypeStruct((len(POOL_WINDOWS), GROUP, GROUP), F32)],
        scratch_shapes=[pltpu.VMEM((HALO, POOL_WIDTH), F32)],
        compiler_params=pltpu.CompilerParams(dimension_semantics=("arbitrary",), vmem_limit_bytes=VMEM_LIMIT),
    )(x, dx2, dy, proj, proj, proj, dq, dk, dv, dag, win_full, pw_bf, g1, ps)


def _reduce_grads(gwin, gwout, small):
    half_small = SMALL_ROWS // 2

    def body(gwin_ref, gwout_ref, small_ref, gin_out, gout_out, small_out,
             pair_in, pair_out, pair_small, tx_in, tx_out, rx_in, rx_out, rx_small, send_sems, recv_sems):
        x, y, c = _my_place()
        b = 2 * x + y
        sibling = (x, y, 1 - c)
        mine_in = pl.ds(pl.multiple_of(c * HALF_IN, HALF_IN), HALF_IN)
        mine_out = pl.ds(pl.multiple_of(c * HALF_OUT, HALF_OUT), HALF_OUT)
        mine_small = pl.ds(pl.multiple_of(c * half_small, 8), half_small)
        theirs_in = pl.ds(pl.multiple_of((1 - c) * HALF_IN, HALF_IN), HALF_IN)
        theirs_out = pl.ds(pl.multiple_of((1 - c) * HALF_OUT, HALF_OUT), HALF_OUT)

        def copy(k, src, dst, to):
            return pltpu.make_async_remote_copy(
                src_ref=src, dst_ref=dst, send_sem=send_sems.at[k], recv_sem=recv_sems.at[k],
                device_id=to, device_id_type=MESH)

        swaps = [
            copy(0, gwin_ref.at[:, theirs_in], pair_in, sibling),
            copy(1, gwout_ref.at[:, theirs_out], pair_out, sibling),
            copy(2, small_ref, pair_small, sibling),
        ]
        for cp in swaps:
            cp.start()
        for cp in swaps:
            cp.wait_recv()
        for chip in range(N_CHIP):
            pair_in[chip] = gwin_ref[chip, mine_in] + pair_in[chip]
            pair_out[chip] = gwout_ref[chip, mine_out] + pair_out[chip]
        pair_small[...] = small_ref[...] + pair_small[...]

        rx_small[b] = pair_small[mine_small]
        sends = []
        for j, (place, chip) in enumerate(_other_chips(x, y)):
            tx_in[j] = pair_in[chip].astype(BF16)
            tx_out[j] = pair_out[chip].astype(BF16)
            to = (*place, c)
            sends += [
                copy(3 + 3 * j, tx_in.at[j], rx_in.at[j], to),
                copy(4 + 3 * j, tx_out.at[j], rx_out.at[j], to),
                copy(5 + 3 * j, pair_small.at[mine_small], rx_small.at[b], to),
            ]
        for cp in sends:
            cp.start()
        for j, (place, chip) in enumerate(_other_chips(x, y)):
            to = (*place, c)
            copy(3 + 3 * j, tx_in.at[j], rx_in.at[j], to).wait_recv()
            copy(4 + 3 * j, tx_out.at[j], rx_out.at[j], to).wait_recv()
            copy(5 + 3 * j, pair_small.at[mine_small], rx_small.at[chip], to).wait_recv()
        total_in = pair_in[b]
        total_out = pair_out[b]
        for j in range(N_CHIP - 1):
            total_in = total_in + rx_in[j].astype(F32)
            total_out = total_out + rx_out[j].astype(F32)
        gin_out[mine_in] = total_in
        gout_out[mine_out] = total_out
        small_out[mine_small] = ((rx_small[0] + rx_small[1]) + rx_small[2]) + rx_small[3]

        shares = [
            copy(12, gin_out.at[mine_in], gin_out.at[mine_in], sibling),
            copy(13, gout_out.at[mine_out], gout_out.at[mine_out], sibling),
            copy(14, small_out.at[mine_small], small_out.at[mine_small], sibling),
        ]
        for cp in shares:
            cp.start()
        theirs_small = pl.ds(pl.multiple_of((1 - c) * half_small, 8), half_small)
        copy(12, gin_out.at[theirs_in], gin_out.at[theirs_in], sibling).wait_recv()
        copy(13, gout_out.at[theirs_out], gout_out.at[theirs_out], sibling).wait_recv()
        copy(14, small_out.at[theirs_small], small_out.at[theirs_small], sibling).wait_recv()
        for cp in swaps + sends + shares:
            cp.wait_send()

    return pl.pallas_call(
        body,
        name="reduce_grads",
        out_shape=(
            jax.ShapeDtypeStruct((D_MODEL, SHARD_IN), F32),
            jax.ShapeDtypeStruct((SHARD_OUT, D_MODEL), F32),
            jax.ShapeDtypeStruct((SMALL_ROWS, GROUP), F32),
        ),
        in_specs=[VMEM, VMEM, VMEM],
        out_specs=(VMEM, VMEM, VMEM),
        scratch_shapes=[
            pltpu.VMEM((N_CHIP, HALF_IN, SHARD_IN), F32),
            pltpu.VMEM((N_CHIP, HALF_OUT, D_MODEL), F32),
            pltpu.VMEM((SMALL_ROWS, GROUP), F32),
            pltpu.VMEM((N_CHIP - 1, HALF_IN, SHARD_IN), BF16),
            pltpu.VMEM((N_CHIP - 1, HALF_OUT, D_MODEL), BF16),
            pltpu.VMEM((N_CHIP - 1, HALF_IN, SHARD_IN), BF16),
            pltpu.VMEM((N_CHIP - 1, HALF_OUT, D_MODEL), BF16),
            pltpu.VMEM((N_CHIP, half_small, GROUP), F32),
            pltpu.SemaphoreType.DMA((15,)),
            pltpu.SemaphoreType.DMA((15,)),
        ],
        compiler_params=pltpu.CompilerParams(vmem_limit_bytes=VMEM_LIMIT),
    )(gwin, gwout, small)


def _adamw(name, w, g, m, v, block_rows):
    rows, cols = w.shape

    def body(w_ref, g_ref, m_ref, v_ref, d_ref, m_out, v_out):
        grad = g_ref[...]
        m_new = ADAM_B1 * m_ref[...] + (1.0 - ADAM_B1) * grad
        v_new = ADAM_B2 * v_ref[...] + (1.0 - ADAM_B2) * (grad * grad)
        m_hat = m_new / (1.0 - ADAM_B1 ** ADAM_STEP)
        v_hat = v_new / (1.0 - ADAM_B2 ** ADAM_STEP)
        d_ref[...] = -ADAM_LR * (m_hat / (jnp.sqrt(v_hat) + ADAM_EPS) + ADAM_WD * w_ref[...])
        m_out[...] = m_new
        v_out[...] = v_new

    blk = pl.BlockSpec((block_rows, cols), lambda i: (i, 0))
    shape = jax.ShapeDtypeStruct((rows, cols), F32)
    return pl.pallas_call(
        body,
        name=name,
        grid=(rows // block_rows,),
        in_specs=[blk] * 4,
        out_specs=[blk] * 3,
        out_shape=[shape] * 3,
        compiler_params=pltpu.CompilerParams(dimension_semantics=("arbitrary",)),
    )(w, g, m, v)


def _pack_small(norm_gain, pool_w, pool_scale, rel_bias, final_gain, loss_rows):
    parts = [
        norm_gain.reshape(8, GROUP),
        pool_w.reshape(len(POOL_WINDOWS) * GROUP, GROUP),
        jnp.pad(pool_scale.reshape(4, GROUP), ((0, 4), (0, 0))),
        jnp.pad(rel_bias.reshape(8, N_REL), ((0, 0), (0, 2 * GROUP - N_REL))).reshape(16, GROUP),
        final_gain.reshape(8, GROUP),
        loss_rows,
    ]
    return jnp.concatenate(parts, axis=0)


def _unpack_small(block):
    norm_gain = block[0:8].reshape(1, D_MODEL)
    pool_w = block[8:520].reshape(1, len(POOL_WINDOWS), GROUP, GROUP)
    pool_scale = block[520:524].reshape(1, POOL_WIDTH)
    rel_bias = block[528:544].reshape(8, 2 * GROUP)[:, :N_REL].reshape(1, 8, N_REL)
    final_gain = block[544:552].reshape(D_MODEL)
    return norm_gain, pool_w, pool_scale, rel_bias, final_gain


def kernel(x, norm_gain, w_in, pool_w, pool_scale, rel_bias, w_out, final_norm_gain, loss_target, m_norm_gain, m_w_in, m_pool_w, m_pool_scale, m_rel_bias, m_w_out, m_final_norm_gain, v_norm_gain, v_w_in, v_pool_w, v_pool_scale, v_rel_bias, v_w_out, v_final_norm_gain):
    assert x.shape[1] % TS == 0 and x.shape[2] == D_MODEL
    xs = x[0]
    tgt = loss_target[0]
    g1 = norm_gain.reshape(1, D_MODEL)
    g2 = final_norm_gain.reshape(1, D_MODEL)
    ps = pool_scale.reshape(1, POOL_WIDTH)
    pw_bf = pool_w[0].astype(BF16)

    proj, win_full, wout_full = _gather_inproj(xs, g1, w_in[0].astype(BF16), w_out[0].astype(BF16))
    bias_t = _bias_tile(rel_bias[0])

    a, ya = _attn_fwd(proj, bias_t)
    dx2, dy, gwout, gg2, loss_rows = _out_loss(xs, tgt, proj, ya, wout_full, g2, pw_bf, ps)
    dq, dk, dv, dag, db_t = _attn_bwd(proj, a, dy, bias_t)
    bins = _bias_bins(db_t)
    gx, gwin, gg1, gps, gpw = _inproj_bwd(xs, dx2, dy, proj, dq, dk, dv, dag, win_full, pw_bf, g1, ps)

    g_bias = bins[:, :N_REL, :2].transpose(0, 2, 1).reshape(8, N_REL)
    small = _pack_small(gg1, gpw, gps, g_bias, gg2, loss_rows)
    g_win, g_wout, g_small = _reduce_grads(gwin, gwout, small)
    loss = g_small[LOSS_ROW, 0]

    zeros8 = jnp.zeros((8, GROUP), F32)
    w_small = _pack_small(norm_gain, pool_w, pool_scale, rel_bias, final_norm_gain, zeros8)
    m_small = _pack_small(m_norm_gain, m_pool_w, m_pool_scale, m_rel_bias, m_final_norm_gain, zeros8)
    v_small = _pack_small(v_norm_gain, v_pool_w, v_pool_scale, v_rel_bias, v_final_norm_gain, zeros8)

    d_win, m_win, v_win = _adamw("adamw_w_in", w_in[0], g_win, m_w_in[0], v_w_in[0], 256)
    d_wout, m_wout, v_wout = _adamw("adamw_w_out", w_out[0], g_wout, m_w_out[0], v_w_out[0], 128)
    d_small, m_new_small, v_new_small = _adamw("adamw_small", w_small, g_small, m_small, v_small, SMALL_ROWS // 2)

    def full(win_part, wout_part, block):
        ng, pw, psc, rb, fg = _unpack_small(block)
        return [ng, win_part[None], pw, psc, rb, wout_part[None], fg]

    grads = full(g_win, g_wout, g_small)
    deltas = full(d_win, d_wout, d_small)
    new_m = full(m_win, m_wout, m_new_small)
    new_v = full(v_win, v_wout, v_new_small)
    return (loss, gx[None], *grads, *deltas, *new_m, *new_v)
```

```python
import numpy as np
import jax
import jax.numpy as jnp
from jax import lax
from jax.experimental import pallas as pl
from jax.experimental.pallas import tpu as pltpu

F32 = jnp.float32
BF16 = jnp.bfloat16

D_MODEL = 1024
POOL_WIDTH = 512
ATTN_WIDTH = 512
POOL_WINDOWS = (2, 4, 8, 16)
GROUP = 128
CHUNK = 64
LEFT_CHUNKS = 8
BAND = (LEFT_CHUNKS + 1) * CHUNK
HEAD_DIM = 64
N_PAIR = 4
MAX_REL = 64
N_REL = 2 * MAX_REL + 1
EPS = 1e-6
MASK_VALUE = -1e30
SCALE = 0.125

ADAM_LR = 0.001
ADAM_B1 = 0.9
ADAM_B2 = 0.999
ADAM_EPS = 1e-08
ADAM_WD = 0.01
ADAM_STEP = 10

TS = LEFT_CHUNKS * CHUNK
SUPER = 2 * CHUNK
WINDOW = BAND + CHUNK
SUPERS_PER_TILE = TS // SUPER
PAIR_LANES = 2 * SUPER
HALO = 16
N_CHIP = 4
SHARD_IN = 768
SHARD_OUT = 256
PIECE = 256
PIECES_PER_SHARD = SHARD_IN // PIECE
HALF_IN = D_MODEL // 2
HALF_OUT = SHARD_OUT // 2
SMALL_ROWS = 560
LOSS_ROW = 552
VMEM_LIMIT = 56 * 1024 * 1024

MESH = pl.DeviceIdType.MESH
ANY = pl.BlockSpec(memory_space=pl.ANY)
VMEM = pl.BlockSpec(memory_space=pltpu.VMEM)


def _sigmoid(x):
    return 1.0 / (1.0 + jnp.exp(-x))


def _dot(a, b):
    return jnp.dot(a, b, preferred_element_type=F32)


def _dot_t(a, b):
    return lax.dot_general(a, b, (((1,), (1,)), ((), ())), preferred_element_type=F32)


def _tdot(a, b):
    return lax.dot_general(a, b, (((0,), (0,)), ((), ())), preferred_element_type=F32)


def _my_place():
    return lax.axis_index("x"), lax.axis_index("y"), lax.axis_index("c")


def _other_chips(x, y):
    places = [(1 - x, y), (x, 1 - y), (1 - x, 1 - y)]
    return [(p, 2 * p[0] + p[1]) for p in places]


def _gather_inproj(x, g1, win_sh, wout_sh):
    seq = x.shape[0]
    tm = min(seq, 2 * TS)
    n_tiles = seq // tm
    cx, cy = lax.axis_index("x"), lax.axis_index("y")
    order = jnp.stack([2 * cx + cy] + [chip for _, chip in _other_chips(cx, cy)]).astype(jnp.int32)

    def body(order_ref, x_ref, g1_ref, win_ref, wout_ref, proj_ref, proj_bf_ref, win_full, wout_full,
             hbuf, wbuf, send_sems, recv_sems, local_sems):
        j = pl.program_id(0)
        i = pl.program_id(1)
        x_, y_, c = _my_place()
        b = 2 * x_ + y_
        sibling = (x_, y_, 1 - c)
        others = _other_chips(x_, y_)

        def halves(chip, core):
            return (
                win_full.at[chip, pl.ds(core * HALF_IN, HALF_IN)],
                wout_full.at[chip, pl.ds(core * HALF_OUT, HALF_OUT)],
            )

        def copy(k, src, dst, to):
            return pltpu.make_async_remote_copy(
                src_ref=src, dst_ref=dst, send_sem=send_sems.at[k], recv_sem=recv_sems.at[k],
                device_id=to, device_id_type=MESH)

        own = [
            pltpu.make_async_copy(win_ref, win_full.at[b], local_sems.at[0]),
            pltpu.make_async_copy(wout_ref, wout_full.at[b], local_sems.at[1]),
        ]
        mine_src = (win_ref.at[pl.ds(c * HALF_IN, HALF_IN)], wout_ref.at[pl.ds(c * HALF_OUT, HALF_OUT)])

        def direct(n, t):
            return copy(2 * n + t, mine_src[t], halves(b, c)[t], (*others[n][0], c))

        def arrival(n, t):
            landed = halves(others[n][1], c)[t]
            return copy(2 * n + t, landed, landed, (*others[n][0], c))

        def passing(n, t):
            landed = halves(others[n][1], c)[t]
            return copy(6 + 2 * n + t, landed, landed, sibling)

        def from_sibling(n, t):
            landed = halves(others[n][1], 1 - c)[t]
            return copy(6 + 2 * n + t, landed, landed, sibling)

        @pl.when((j == 0) & (i == 0))
        def _():
            first = pltpu.make_async_copy(win_ref, wbuf, local_sems.at[2])
            first.start()
            for cp in own:
                cp.start()
            for t in range(2):
                for n in range(N_CHIP - 1):
                    direct(n, t).start()
            first.wait()

        for n in range(N_CHIP - 1):
            @pl.when((j == n + 1) & (i == 0))
            def _(n=n):
                arrival(n, 0).wait_recv()
                passing(n, 0).start()
                from_sibling(n, 0).wait_recv()
                load = pltpu.make_async_copy(win_full.at[others[n][1]], wbuf, local_sems.at[2])
                load.start()
                load.wait()

        @pl.when((j == N_CHIP - 1) & (i == 0))
        def _():
            for n in range(N_CHIP - 1):
                arrival(n, 1).wait_recv()
                passing(n, 1).start()

        rows = pl.ds(pl.multiple_of(i * tm, tm), tm)

        @pl.when(j == 0)
        def _():
            xt = x_ref[...]
            r = lax.rsqrt(jnp.mean(xt * xt, axis=-1, keepdims=True) + EPS)
            hbuf[rows] = ((xt * r) * g1_ref[...]).astype(BF16)

        out = _dot(hbuf[rows], wbuf[...])
        proj_ref[...] = out
        proj_bf_ref[...] = out.astype(BF16)

        @pl.when((j == N_CHIP - 1) & (i == n_tiles - 1))
        def _():
            for n in range(N_CHIP - 1):
                from_sibling(n, 1).wait_recv()
            for n in range(N_CHIP - 1):
                for t in range(2):
                    direct(n, t).wait_send()
                    passing(n, t).wait_send()
            for cp in own:
                cp.wait()

    last = n_tiles - 1
    grid_spec = pltpu.PrefetchScalarGridSpec(
        num_scalar_prefetch=1,
        grid=(N_CHIP, n_tiles),
        in_specs=[
            pl.BlockSpec((tm, D_MODEL), lambda j, i, o: (jnp.where(j == 0, i, last), 0)),
            VMEM, ANY, ANY],
        out_specs=[pl.BlockSpec((tm, SHARD_IN), lambda j, i, o: (i, o[j])),
                   pl.BlockSpec((tm, SHARD_IN), lambda j, i, o: (i, o[j])), ANY, ANY],
        scratch_shapes=[
            pltpu.VMEM((seq, D_MODEL), BF16), pltpu.VMEM((D_MODEL, SHARD_IN), BF16),
            pltpu.SemaphoreType.DMA((12,)), pltpu.SemaphoreType.DMA((12,)), pltpu.SemaphoreType.DMA((3,))],
    )
    return pl.pallas_call(
        body,
        name="gather_inproj",
        grid_spec=grid_spec,
        out_shape=(
            jax.ShapeDtypeStruct((seq, N_CHIP * SHARD_IN), F32),
            jax.ShapeDtypeStruct((seq, N_CHIP * SHARD_IN), BF16),
            jax.ShapeDtypeStruct((N_CHIP, D_MODEL, SHARD_IN), BF16),
            jax.ShapeDtypeStruct((N_CHIP, SHARD_OUT, D_MODEL), BF16),
        ),
        compiler_params=pltpu.CompilerParams(
            dimension_semantics=("arbitrary", "arbitrary"), vmem_limit_bytes=VMEM_LIMIT),
    )(order, x, g1, win_sh, wout_sh)


def _window_sums(ext, forward):
    n = ext.shape[0]
    sums = []
    acc = ext
    for step in (1, 2, 4, 8):
        acc = acc + pltpu.roll(acc, (n - step) if forward else step, 0)
        sums.append(acc)
    return sums


def _row_counts(tile, rows):
    t = tile * TS + lax.broadcasted_iota(jnp.int32, (rows, GROUP), 0)
    return [jnp.minimum(t + 1, w).astype(F32) for w in POOL_WINDOWS]


def _pool_diffs(pv, prev_rows, tile):
    ext = jnp.concatenate([prev_rows, pv], axis=0)
    sums = _window_sums(ext, forward=False)
    counts = _row_counts(tile, TS)
    out = []
    for g in range(len(POOL_WINDOWS)):
        cols = slice(g * GROUP, (g + 1) * GROUP)
        out.append(sums[g][HALO:, cols] / counts[g] - pv[:, cols])
    return out


def _pool_mix(diffs, pw_ref):
    return jnp.concatenate([_dot(diffs[g], pw_ref[g]) for g in range(len(POOL_WINDOWS))], axis=1)


Q_BLOCK, K_BLOCK, V_BLOCK, AG_BLOCK = 8, 12, 16, 20


def _bias_tile(rel_bias):
    flat = jnp.concatenate(
        [jnp.broadcast_to(rel_bias[:, :1], (rel_bias.shape[0], BAND - CHUNK - 1)), rel_bias[:, :2 * MAX_REL]], axis=1)
    rows = [flat[:, CHUNK - 1 - i:CHUNK - 1 - i + BAND] for i in range(CHUNK)]
    bias = jnp.stack(rows, axis=1)
    first = jnp.pad(bias, ((0, 0), (0, 0), (0, CHUNK)), constant_values=MASK_VALUE)
    second = jnp.pad(bias, ((0, 0), (0, 0), (CHUNK, 0)), constant_values=MASK_VALUE)
    both = jnp.concatenate([first, second], axis=1)
    return both.reshape(N_PAIR, PAIR_LANES, WINDOW).transpose(0, 2, 1)


def _rel_index_tile():
    j = np.arange(WINDOW)[:, None]
    q = np.arange(PAIR_LANES)[None, :] % SUPER
    band_key = j - CHUNK * (q // CHUNK)
    idx = np.clip(band_key - LEFT_CHUNKS * CHUNK - q % CHUNK, -MAX_REL, MAX_REL) + MAX_REL
    return np.where((band_key >= 0) & (band_key < BAND), idx, -1).astype(np.int32)


def _by_head(block):
    low = lax.broadcasted_iota(jnp.int32, block.shape, 1) < HEAD_DIM
    zero = jnp.zeros_like(block)
    return jnp.concatenate([jnp.where(low, block, zero), jnp.where(low, zero, block)], axis=0)


def _own_head_rows(cross):
    head_of_row = lax.broadcasted_iota(jnp.int32, cross.shape, 0) >= HEAD_DIM
    head_of_lane = lax.broadcasted_iota(jnp.int32, cross.shape, 1) >= SUPER
    both = jnp.where(jnp.logical_xor(head_of_row, head_of_lane), 0.0, cross).T
    return both[:SUPER] + both[SUPER:]


def _with_mask_lane(q_rows):
    lane = lax.broadcasted_iota(jnp.int32, q_rows.shape, 1)
    return jnp.concatenate([q_rows, jnp.where(lane == 0, MASK_VALUE, 0.0).astype(q_rows.dtype)], axis=1)


def _band_exp(kb, q_rows, bias):
    s = _dot_t(kb, _with_mask_lane(q_rows)) + bias
    e = jnp.exp(s - jnp.max(s, axis=0, keepdims=True))
    return e, jnp.sum(e, axis=0, keepdims=True)


def _shift_band(i, band_ref, new_ref):
    @pl.when(i == 0)
    def _():
        band_ref[:TS] = jnp.zeros((TS, GROUP), band_ref.dtype)

    @pl.when(i > 0)
    def _():
        band_ref[:TS] = band_ref[TS:]

    band_ref[TS:] = new_ref[...].astype(band_ref.dtype)


def _shift_key_band(i, band_ref, new_ref):
    @pl.when(i == 0)
    def _():
        lane = lax.broadcasted_iota(jnp.int32, (TS, 2 * GROUP), 1)
        band_ref[:TS] = jnp.where(lane == GROUP, 1.0, 0.0).astype(band_ref.dtype)
        band_ref[TS:, GROUP:] = jnp.zeros((TS, GROUP), band_ref.dtype)

    @pl.when(i > 0)
    def _():
        band_ref[:TS] = band_ref[TS:]

    band_ref[TS:, :GROUP] = new_ref[...].astype(band_ref.dtype)


def _shift_band_t(i, band_ref, new_ref):
    @pl.when(i == 0)
    def _():
        band_ref[:, :TS] = jnp.zeros((GROUP, TS), band_ref.dtype)

    @pl.when(i > 0)
    def _():
        band_ref[:, :TS] = band_ref[:, TS:]

    band_ref[:, TS:] = new_ref[...].astype(band_ref.dtype).T


def _scaled_queries(q_ref, rows):
    return _by_head((q_ref[rows] * SCALE).astype(BF16))


def _attn_fwd(proj, proj_bf, bias_t):
    seq = proj.shape[0]
    n_tiles = seq // TS

    def body(q_ref, k_ref, v_ref, ag_ref, bias_ref, a_ref, ya_ref, kband, vband_t):
        i = pl.program_id(1)
        _shift_key_band(i, kband, k_ref)
        _shift_band_t(i, vband_t, v_ref)

        def weights(sc):
            rows = slice(sc * SUPER, (sc + 1) * SUPER)
            win = slice(sc * SUPER, sc * SUPER + WINDOW)
            e, total = _band_exp(kband[win], _scaled_queries(q_ref, rows), bias_ref[0])
            return e.astype(BF16), 1.0 / total

        nxt = weights(0)
        for sc in range(SUPERS_PER_TILE):
            rows = slice(sc * SUPER, (sc + 1) * SUPER)
            win = slice(sc * SUPER, sc * SUPER + WINDOW)
            e_bf, inv_total = nxt
            if sc + 1 < SUPERS_PER_TILE:
                nxt = weights(sc + 1)
            a = _own_head_rows(_dot(vband_t[:, win], e_bf) * inv_total)
            a_ref[rows] = a
            g = ag_ref[rows]
            ya_ref[rows] = (a * (g * _sigmoid(g))).astype(BF16)

    blk = pl.BlockSpec((TS, GROUP), lambda p, i: (i, p))

    def cols(first):
        return pl.BlockSpec((TS, GROUP), lambda p, i: (i, first + p))

    return pl.pallas_call(
        body,
        name="attn_fwd",
        grid=(N_PAIR, n_tiles),
        in_specs=[cols(Q_BLOCK), cols(K_BLOCK), cols(V_BLOCK), cols(AG_BLOCK),
                  pl.BlockSpec((1, WINDOW, PAIR_LANES), lambda p, i: (p, 0, 0))],
        out_specs=[blk, blk],
        out_shape=[jax.ShapeDtypeStruct((seq, ATTN_WIDTH), F32), jax.ShapeDtypeStruct((seq, ATTN_WIDTH), BF16)],
        scratch_shapes=[pltpu.VMEM((2 * TS, 2 * GROUP), BF16), pltpu.VMEM((GROUP, 2 * TS), BF16)],
        compiler_params=pltpu.CompilerParams(
            dimension_semantics=("arbitrary", "arbitrary"), vmem_limit_bytes=VMEM_LIMIT),
    )(proj_bf, proj_bf, proj_bf, proj, bias_t)


def _attn_bwd(proj, proj_bf, a, dy, bias_t):
    seq = proj.shape[0]
    n_tiles = seq // TS

    def body(q_ref, k_ref, v_ref, a_ref, ag_ref, dy_ref, bias_ref,
             dq_ref, dk_ref, dv_ref, dag_ref, db_ref, kband, vband, kband_t, dkacc, dvacc):
        i = pl.program_id(1)

        @pl.when(i == 0)
        def _():
            dkacc[...] = jnp.zeros_like(dkacc)
            dvacc[...] = jnp.zeros_like(dvacc)
            db_ref[...] = jnp.zeros_like(db_ref)

        @pl.when(i < n_tiles)
        def _():
            _shift_key_band(i, kband, k_ref)
            _shift_band(i, vband, v_ref)
            _shift_band_t(i, kband_t, k_ref)
            def score_grads(sc):
                rows = slice(sc * SUPER, (sc + 1) * SUPER)
                win = slice(sc * SUPER, sc * SUPER + WINDOW)
                q_rows = _scaled_queries(q_ref, rows)
                g = ag_ref[rows]
                sg = _sigmoid(g)
                dyc = dy_ref[rows]
                dag_ref[rows] = dyc * a_ref[rows] * (sg * (1.0 + g * (1.0 - sg)))
                da_rows = _by_head((dyc * (g * sg)).astype(BF16))
                e, total = _band_exp(kband[win], q_rows, bias_ref[0])
                p = e * (1.0 / total)
                dp = _dot_t(vband[win], da_rows)
                ds = p * (dp - jnp.sum(p * dp, axis=0, keepdims=True))
                db_ref[0] += ds
                return q_rows, da_rows, p.astype(BF16), ds.astype(BF16)

            nxt = score_grads(0)
            for sc in range(SUPERS_PER_TILE):
                rows = slice(sc * SUPER, (sc + 1) * SUPER)
                win = slice(sc * SUPER, sc * SUPER + WINDOW)
                q_rows, da_rows, p_bf, ds_bf = nxt
                if sc + 1 < SUPERS_PER_TILE:
                    nxt = score_grads(sc + 1)
                dq_ref[rows] = (_own_head_rows(_dot(kband_t[:, win], ds_bf)) * SCALE).astype(BF16)
                dkacc[win] += _dot(ds_bf, q_rows)
                dvacc[win] += _dot(p_bf, da_rows)

        dk_ref[...] = dkacc[:TS].astype(BF16)
        dv_ref[...] = dvacc[:TS].astype(BF16)
        dkacc[:TS] = dkacc[TS:]
        dvacc[:TS] = dvacc[TS:]
        dkacc[TS:] = jnp.zeros((TS, GROUP), F32)
        dvacc[TS:] = jnp.zeros((TS, GROUP), F32)

    last = n_tiles - 1
    cur = pl.BlockSpec((TS, GROUP), lambda p, i: (jnp.minimum(i, last), p))
    older = pl.BlockSpec((TS, GROUP), lambda p, i: (jnp.maximum(i - 1, 0), p))
    dy_blk = pl.BlockSpec((TS, GROUP), lambda p, i: (jnp.minimum(i, last), N_PAIR + p))
    per_pair = pl.BlockSpec((1, WINDOW, PAIR_LANES), lambda p, i: (p, 0, 0))

    def cols(first):
        return pl.BlockSpec((TS, GROUP), lambda p, i: (jnp.minimum(i, last), first + p))

    def out(dtype):
        return jax.ShapeDtypeStruct((seq, ATTN_WIDTH), dtype)

    return pl.pallas_call(
        body,
        name="attn_bwd",
        grid=(N_PAIR, n_tiles + 1),
        in_specs=[cols(Q_BLOCK), cols(K_BLOCK), cols(V_BLOCK), cur, cols(AG_BLOCK), dy_blk, per_pair],
        out_specs=[cur, older, older, cur, per_pair],
        out_shape=[out(BF16), out(BF16), out(BF16), out(F32),
                   jax.ShapeDtypeStruct((N_PAIR, WINDOW, PAIR_LANES), F32)],
        scratch_shapes=[
            pltpu.VMEM((2 * TS, 2 * GROUP), BF16), pltpu.VMEM((2 * TS, GROUP), BF16), pltpu.VMEM((GROUP, 2 * TS), BF16),
            pltpu.VMEM((2 * TS, GROUP), F32), pltpu.VMEM((2 * TS, GROUP), F32)],
        compiler_params=pltpu.CompilerParams(
            dimension_semantics=("arbitrary", "arbitrary"), vmem_limit_bytes=VMEM_LIMIT),
    )(proj_bf, proj_bf, proj_bf, a, proj, dy, bias_t)


BIN_ROWS = 136


def _bias_bins(db_t):
    idx_t = jnp.asarray(_rel_index_tile())

    def body(db_ref, idx_ref, out_ref):
        lane = lax.broadcasted_iota(jnp.int32, (1, GROUP), 1)
        row = lax.broadcasted_iota(jnp.int32, (BIN_ROWS, GROUP), 0)
        out = jnp.zeros((BIN_ROWS, GROUP), F32)
        for r in range(N_REL - 1):
            lo = 0 if r == 0 else ((BAND - 2 * CHUNK + r) // 8) * 8
            hi = WINDOW if r == 0 else min(WINDOW, lo + SUPER + 8)
            hit = jnp.where(idx_ref[lo:hi] == r, db_ref[0, lo:hi], 0.0)
            col = jnp.sum(hit, axis=0, keepdims=True)
            s0 = jnp.sum(col[:, :SUPER], axis=1, keepdims=True)
            s1 = jnp.sum(col[:, SUPER:], axis=1, keepdims=True)
            val = jnp.where(lane == 0, s0, jnp.where(lane == 1, s1, 0.0))
            out = jnp.where(row == r, val, out)
        out_ref[0] = out

    return pl.pallas_call(
        body,
        name="bias_bins",
        grid=(N_PAIR,),
        in_specs=[pl.BlockSpec((1, WINDOW, PAIR_LANES), lambda p: (p, 0, 0)), VMEM],
        out_specs=pl.BlockSpec((1, BIN_ROWS, GROUP), lambda p: (p, 0, 0)),
        out_shape=jax.ShapeDtypeStruct((N_PAIR, BIN_ROWS, GROUP), F32),
        compiler_params=pltpu.CompilerParams(dimension_semantics=("arbitrary",)),
    )(db_t, idx_t)


def _out_loss(x, tgt, proj, ya, wout_full, g2, pw_bf, ps):
    seq = x.shape[0]
    n_tiles = seq // TS
    half = TS // 2

    def body(x_ref, t_ref, pv_ref, pg_ref, ya_ref, w_ref, g2_ref, pw_ref, ps_ref,
             dx2_ref, dy_ref, gw_ref, gg_ref, loss_ref, sq_ref, halo_ref):
        i = pl.program_id(0)

        @pl.when(i == 0)
        def _():
            gw_ref[...] = jnp.zeros_like(gw_ref)
            gg_ref[...] = jnp.zeros_like(gg_ref)
            sq_ref[...] = jnp.zeros_like(sq_ref)
            halo_ref[...] = jnp.zeros_like(halo_ref)

        pv = pv_ref[...]
        pg = pg_ref[...]
        diffs = [d.astype(BF16) for d in _pool_diffs(pv, halo_ref[...], i)]
        halo_ref[...] = pv[TS - HALO:, :]
        yp = ((_pool_mix(diffs, pw_ref) * ps_ref[...]) * (pg * _sigmoid(pg))).astype(BF16)
        g2v = g2_ref[...]

        def parts(rows):
            return [yp[rows, :SHARD_OUT], yp[rows, SHARD_OUT:], ya_ref[rows, :SHARD_OUT], ya_ref[rows, SHARD_OUT:]]

        def project(rows, ys):
            x2 = x_ref[rows]
            for b in range(N_CHIP):
                x2 = x2 + _dot(ys[b], w_ref[b])
            return x2

        def norm_loss(rows, x2):
            r = lax.rsqrt(jnp.mean(x2 * x2, axis=-1, keepdims=True) + EPS)
            xh = x2 * r
            diff = xh * g2v - t_ref[rows]
            sq_ref[...] += jnp.sum(diff * diff, axis=0, keepdims=True)
            dfin = diff * (1.0 / D_MODEL)
            gg_ref[...] += jnp.sum(dfin * xh, axis=0, keepdims=True)
            dxh = dfin * g2v
            dx2 = r * (dxh - xh * jnp.mean(dxh * xh, axis=-1, keepdims=True))
            dx2_ref[rows] = dx2
            return dx2.astype(BF16)

        def back(rows, ys, dx2_bf):
            for b in range(N_CHIP):
                gw_ref[b] += _tdot(ys[b], dx2_bf)
                dy_ref[rows, b * SHARD_OUT:(b + 1) * SHARD_OUT] = _dot_t(dx2_bf, w_ref[b])

        top, bottom = slice(0, half), slice(half, TS)
        ys_top, ys_bottom = parts(top), parts(bottom)
        x2_top = project(top, ys_top)
        x2_bottom = project(bottom, ys_bottom)
        d_top = norm_loss(top, x2_top)
        back(top, ys_top, d_top)
        d_bottom = norm_loss(bottom, x2_bottom)
        back(bottom, ys_bottom, d_bottom)

        @pl.when(i == n_tiles - 1)
        def _():
            total = jnp.sum(sq_ref[...], axis=1, keepdims=True) * (0.5 / D_MODEL)
            loss_ref[...] = jnp.broadcast_to(total, loss_ref.shape)

    def rows(width, col=0):
        return pl.BlockSpec((TS, width), lambda i: (i, col))

    return pl.pallas_call(
        body,
        name="out_loss",
        grid=(n_tiles,),
        in_specs=[rows(D_MODEL), rows(D_MODEL), rows(POOL_WIDTH, 0), rows(POOL_WIDTH, 1), rows(ATTN_WIDTH),
                  VMEM, VMEM, VMEM, VMEM],
        out_specs=[rows(D_MODEL), rows(D_MODEL), VMEM, VMEM, VMEM],
        out_shape=[
            jax.ShapeDtypeStruct((seq, D_MODEL), F32), jax.ShapeDtypeStruct((seq, D_MODEL), F32),
            jax.ShapeDtypeStruct((N_CHIP, SHARD_OUT, D_MODEL), F32), jax.ShapeDtypeStruct((1, D_MODEL), F32),
            jax.ShapeDtypeStruct((8, GROUP), F32)],
        scratch_shapes=[pltpu.VMEM((1, D_MODEL), F32), pltpu.VMEM((HALO, POOL_WIDTH), F32)],
        compiler_params=pltpu.CompilerParams(dimension_semantics=("arbitrary",), vmem_limit_bytes=VMEM_LIMIT),
    )(x, tgt, proj, proj, ya, wout_full, g2, pw_bf, ps)


def _inproj_bwd(x, dx2, dy, proj, dq, dk, dv, dag, win_full, pw_bf, g1, ps):
    seq = x.shape[0]
    n_tiles = seq // TS

    def body(x_ref, dx2_ref, dyp_ref, pv_ref, pvprev_ref, pg_ref, dq_ref, dk_ref, dv_ref, dag_ref,
             w_ref, pw_ref, g1_ref, ps_ref, gx_ref, dproj_ref, ht_ref, gg_ref, gps_ref, gpw_ref, halo_ref):
        i = pl.program_id(0)
        tile = n_tiles - 1 - i

        @pl.when(i == 0)
        def _():
            gg_ref[...] = jnp.zeros_like(gg_ref)
            gps_ref[...] = jnp.zeros_like(gps_ref)
            gpw_ref[...] = jnp.zeros_like(gpw_ref)
            halo_ref[...] = jnp.zeros_like(halo_ref)

        pv_t = pv_ref[...]
        pg_t = pg_ref[...]
        prev_rows = jnp.where(tile > 0, pvprev_ref[...], 0.0)
        diffs = [d.astype(BF16) for d in _pool_diffs(pv_t, prev_rows, tile)]
        mixed = _pool_mix(diffs, pw_ref)
        sg = _sigmoid(pg_t)
        silu = pg_t * sg
        dyp = dyp_ref[...]
        psv = ps_ref[...]
        gps_ref[...] += jnp.sum(dyp * mixed * silu, axis=0, keepdims=True)
        dmixed = (dyp * psv * silu).astype(BF16)
        dpg = dyp * (mixed * psv) * (sg * (1.0 + pg_t * (1.0 - sg)))
        counts = _row_counts(tile, TS)
        dds = []
        for g in range(len(POOL_WINDOWS)):
            dm_g = dmixed[:, g * GROUP:(g + 1) * GROUP]
            gpw_ref[g] += _tdot(diffs[g], dm_g)
            dds.append(_dot_t(dm_g, pw_ref[g]))
        dd = jnp.concatenate(dds, axis=1)
        spread = jnp.concatenate([dds[g] / counts[g] for g in range(len(POOL_WINDOWS))], axis=1)
        sums = _window_sums(jnp.concatenate([spread, halo_ref[...]], axis=0), forward=True)
        halo_ref[...] = spread[:HALO]
        dpv = jnp.concatenate(
            [sums[g][:TS, g * GROUP:(g + 1) * GROUP] for g in range(len(POOL_WINDOWS))], axis=1) - dd

        xt = x_ref[...]
        r = lax.rsqrt(jnp.mean(xt * xt, axis=-1, keepdims=True) + EPS)
        xh = xt * r
        g1v = g1_ref[...]
        ht_ref[...] = (xh * g1v).astype(BF16).T
        dproj = jnp.concatenate(
            [dpv.astype(BF16), dpg.astype(BF16), dq_ref[...], dk_ref[...], dv_ref[...], dag_ref[...].astype(BF16)],
            axis=1)
        dproj_ref[...] = dproj
        dh = _dot_t(dproj[:, :SHARD_IN], w_ref[0])
        for chip in range(1, N_CHIP):
            dh = dh + _dot_t(dproj[:, chip * SHARD_IN:(chip + 1) * SHARD_IN], w_ref[chip])

        gg_ref[...] += jnp.sum(dh * xh, axis=0, keepdims=True)
        dxh = dh * g1v
        gx_ref[...] = dx2_ref[...] + r * (dxh - xh * jnp.mean(dxh * xh, axis=-1, keepdims=True))

    def rows(width, col=0):
        return pl.BlockSpec((TS, width), lambda i: (n_tiles - 1 - i, col))

    prev = pl.BlockSpec((HALO, POOL_WIDTH), lambda i: (jnp.maximum((n_tiles - 1 - i) * (TS // HALO) - 1, 0), 0))
    return pl.pallas_call(
        body,
        name="inproj_bwd",
        grid=(n_tiles,),
        in_specs=[rows(D_MODEL), rows(D_MODEL), rows(POOL_WIDTH), rows(POOL_WIDTH), prev, rows(POOL_WIDTH, 1),
                  rows(ATTN_WIDTH), rows(ATTN_WIDTH), rows(ATTN_WIDTH), rows(ATTN_WIDTH), VMEM, VMEM, VMEM, VMEM],
        out_specs=[rows(D_MODEL), rows(N_CHIP * SHARD_IN),
                   pl.BlockSpec((D_MODEL, TS), lambda i: (0, n_tiles - 1 - i)), VMEM, VMEM, VMEM],
        out_shape=[
            jax.ShapeDtypeStruct((seq, D_MODEL), F32),
            jax.ShapeDtypeStruct((seq, N_CHIP * SHARD_IN), BF16),
            jax.ShapeDtypeStruct((D_MODEL, seq), BF16),
            jax.ShapeDtypeStruct((1, D_MODEL), F32),
            jax.ShapeDtypeStruct((1, POOL_WIDTH), F32),
            jax.ShapeDtypeStruct((len(POOL_WINDOWS), GROUP, GROUP), F32)],
        scratch_shapes=[pltpu.VMEM((HALO, POOL_WIDTH), F32)],
        compiler_params=pltpu.CompilerParams(dimension_semantics=("arbitrary",), vmem_limit_bytes=VMEM_LIMIT),
    )(x, dx2, dy, proj, proj, proj, dq, dk, dv, dag, win_full, pw_bf, g1, ps)


PASS_PEER = (2, 0, 1)


def _gw_reduce(ht, dproj, gwout, small):
    seq = ht.shape[1]
    tm = min(2 * TS, seq // 8)
    n_tiles = seq // tm
    half_small = SMALL_ROWS // 2
    cx, cy = lax.axis_index("x"), lax.axis_index("y")
    outer = _other_chips(cx, cy)
    order = jnp.stack([outer[n][1] for n in PASS_PEER] + [2 * cx + cy]).astype(jnp.int32)

    def body(order_ref, ht_ref, dp_ref, gwout_ref, small_ref, gin_final, gout_final, small_final,
             acc, pair_in, pair_out, pair_small, tx_in, tx_out, rx_in, rx_out, rx_small,
             gin_out, gout_out, small_out, send_sems, recv_sems):
        j = pl.program_id(0)
        i = pl.program_id(1)
        x, y, c = _my_place()
        b = 2 * x + y
        sibling = (x, y, 1 - c)
        others = _other_chips(x, y)
        mine_in = pl.ds(pl.multiple_of(c * HALF_IN, HALF_IN), HALF_IN)
        mine_out = pl.ds(pl.multiple_of(c * HALF_OUT, HALF_OUT), HALF_OUT)
        mine_small = pl.ds(pl.multiple_of(c * half_small, 8), half_small)
        theirs_in = pl.ds(pl.multiple_of((1 - c) * HALF_IN, HALF_IN), HALF_IN)
        theirs_out = pl.ds(pl.multiple_of((1 - c) * HALF_OUT, HALF_OUT), HALF_OUT)
        theirs_small = pl.ds(pl.multiple_of((1 - c) * half_small, 8), half_small)

        def copy(k, src, dst, to):
            return pltpu.make_async_remote_copy(
                src_ref=src, dst_ref=dst, send_sem=send_sems.at[k], recv_sem=recv_sems.at[k],
                device_id=to, device_id_type=MESH)

        swap_out = copy(0, gwout_ref.at[:, theirs_out], pair_out, sibling)
        swap_small = copy(1, small_ref, pair_small, sibling)

        def swap_in(p):
            return copy(2 + p, acc.at[p % 2, theirs_in], pair_in.at[p % 2], sibling)

        def to_chip(n, t):
            to = (*others[n][0], c)
            if t == 0:
                return copy(6 + 3 * n, tx_in.at[n], rx_in.at[n], to)
            if t == 1:
                return copy(7 + 3 * n, tx_out.at[n], rx_out.at[n], to)
            return copy(8 + 3 * n, pair_small.at[mine_small], rx_small.at[b], to)

        share_in = copy(15, gin_out.at[mine_in], gin_out.at[mine_in], sibling)
        share_out = copy(16, gout_out.at[mine_out], gout_out.at[mine_out], sibling)
        share_small = copy(17, small_out.at[mine_small], small_out.at[mine_small], sibling)

        def at(jj, ii):
            return (j == jj) & (i == ii)

        par = j % 2

        @pl.when(i == 0)
        def _():
            acc[par] = jnp.zeros((D_MODEL, SHARD_IN), F32)

        acc[par] += _dot(ht_ref[...], dp_ref[...])

        @pl.when(at(0, 0))
        def _():
            swap_out.start()
            swap_small.start()

        @pl.when(at(0, 2))
        def _():
            swap_out.wait_recv()
            swap_small.wait_recv()
            for chip in range(N_CHIP):
                pair_out[chip] = gwout_ref[chip, mine_out] + pair_out[chip]
            pair_small[...] = small_ref[...] + pair_small[...]
            rx_small[b] = pair_small[mine_small]
            for n in range(N_CHIP - 1):
                tx_out[n] = pair_out[others[n][1]].astype(BF16)
                to_chip(n, 1).start()
                to_chip(n, 2).start()

        @pl.when(at(1, 4))
        def _():
            total_out = pair_out[b]
            for n in range(N_CHIP - 1):
                to_chip(n, 1).wait_recv()
                copy(8 + 3 * n, pair_small.at[mine_small], rx_small.at[others[n][1]], (*others[n][0], c)).wait_recv()
                total_out = total_out + rx_out[n].astype(F32)
            gout_out[mine_out] = total_out
            small_out[mine_small] = ((rx_small[0] + rx_small[1]) + rx_small[2]) + rx_small[3]
            share_out.start()
            share_small.start()

        for p in range(N_CHIP - 1):
            n = PASS_PEER[p]

            @pl.when(at(p + 1, 0))
            def _(p=p):
                swap_in(p).start()

            @pl.when(at(p + 1, 2))
            def _(p=p, n=n):
                swap_in(p).wait_recv()
                swap_in(p).wait_send()
                tx_in[n] = (acc[p % 2, mine_in] + pair_in[p % 2]).astype(BF16)
                to_chip(n, 0).start()

        @pl.when(at(N_CHIP - 1, n_tiles - 1))
        def _():
            last = N_CHIP - 1
            swap_in(last).start()
            swap_in(last).wait_recv()
            total_in = acc[last % 2, mine_in] + pair_in[last % 2]
            for n in range(N_CHIP - 1):
                to_chip(n, 0).wait_recv()
                total_in = total_in + rx_in[n].astype(F32)
            gin_out[mine_in] = total_in
            share_in.start()
            copy(15, gin_out.at[theirs_in], gin_out.at[theirs_in], sibling).wait_recv()
            copy(16, gout_out.at[theirs_out], gout_out.at[theirs_out], sibling).wait_recv()
            copy(17, small_out.at[theirs_small], small_out.at[theirs_small], sibling).wait_recv()
            swap_out.wait_send()
            swap_small.wait_send()
            swap_in(last).wait_send()
            for n in range(N_CHIP - 1):
                for t in range(3):
                    to_chip(n, t).wait_send()
            share_in.wait_send()
            share_out.wait_send()
            share_small.wait_send()
            gin_final[...] = gin_out[...]
            gout_final[...] = gout_out[...]
            small_final[...] = small_out[...]

    assert n_tiles >= 5, "the reduction's steps are spread over the first five token steps of a pass"
    grid_spec = pltpu.PrefetchScalarGridSpec(
        num_scalar_prefetch=1,
        grid=(N_CHIP, n_tiles),
        in_specs=[
            pl.BlockSpec((D_MODEL, tm), lambda j, i, o: (0, i)),
            pl.BlockSpec((tm, SHARD_IN), lambda j, i, o: (i, o[j])),
            VMEM, VMEM],
        out_specs=[VMEM, VMEM, VMEM],
        scratch_shapes=[
            pltpu.VMEM((2, D_MODEL, SHARD_IN), F32),
            pltpu.VMEM((2, HALF_IN, SHARD_IN), F32),
            pltpu.VMEM((N_CHIP, HALF_OUT, D_MODEL), F32),
            pltpu.VMEM((SMALL_ROWS, GROUP), F32),
            pltpu.VMEM((N_CHIP - 1, HALF_IN, SHARD_IN), BF16),
            pltpu.VMEM((N_CHIP - 1, HALF_OUT, D_MODEL), BF16),
            pltpu.VMEM((N_CHIP - 1, HALF_IN, SHARD_IN), BF16),
            pltpu.VMEM((N_CHIP - 1, HALF_OUT, D_MODEL), BF16),
            pltpu.VMEM((N_CHIP, half_small, GROUP), F32),
            pltpu.VMEM((D_MODEL, SHARD_IN), F32),
            pltpu.VMEM((SHARD_OUT, D_MODEL), F32),
            pltpu.VMEM((SMALL_ROWS, GROUP), F32),
            pltpu.SemaphoreType.DMA((18,)),
            pltpu.SemaphoreType.DMA((18,)),
        ],
    )
    return pl.pallas_call(
        body,
        name="gw_reduce",
        grid_spec=grid_spec,
        out_shape=(
            jax.ShapeDtypeStruct((D_MODEL, SHARD_IN), F32),
            jax.ShapeDtypeStruct((SHARD_OUT, D_MODEL), F32),
            jax.ShapeDtypeStruct((SMALL_ROWS, GROUP), F32),
        ),
        compiler_params=pltpu.CompilerParams(
            dimension_semantics=("arbitrary", "arbitrary"), vmem_limit_bytes=VMEM_LIMIT),
    )(order, ht, dproj, gwout, small)


def _adamw(name, w, g, m, v, block_rows):
    rows, cols = w.shape

    def body(w_ref, g_ref, m_ref, v_ref, d_ref, m_out, v_out):
        grad = g_ref[...]
        m_new = ADAM_B1 * m_ref[...] + (1.0 - ADAM_B1) * grad
        v_new = ADAM_B2 * v_ref[...] + (1.0 - ADAM_B2) * (grad * grad)
        m_hat = m_new / (1.0 - ADAM_B1 ** ADAM_STEP)
        v_hat = v_new / (1.0 - ADAM_B2 ** ADAM_STEP)
        d_ref[...] = -ADAM_LR * (m_hat / (jnp.sqrt(v_hat) + ADAM_EPS) + ADAM_WD * w_ref[...])
        m_out[...] = m_new
        v_out[...] = v_new

    blk = pl.BlockSpec((block_rows, cols), lambda i: (i, 0))
    shape = jax.ShapeDtypeStruct((rows, cols), F32)
    return pl.pallas_call(
        body,
        name=name,
        grid=(rows // block_rows,),
        in_specs=[blk] * 4,
        out_specs=[blk] * 3,
        out_shape=[shape] * 3,
        compiler_params=pltpu.CompilerParams(dimension_semantics=("arbitrary",)),
    )(w, g, m, v)


def _pack_small(norm_gain, pool_w, pool_scale, rel_bias, final_gain, loss_rows):
    parts = [
        norm_gain.reshape(8, GROUP),
        pool_w.reshape(len(POOL_WINDOWS) * GROUP, GROUP),
        jnp.pad(pool_scale.reshape(4, GROUP), ((0, 4), (0, 0))),
        jnp.pad(rel_bias.reshape(8, N_REL), ((0, 0), (0, 2 * GROUP - N_REL))).reshape(16, GROUP),
        final_gain.reshape(8, GROUP),
        loss_rows,
    ]
    return jnp.concatenate(parts, axis=0)


def _unpack_small(block):
    norm_gain = block[0:8].reshape(1, D_MODEL)
    pool_w = block[8:520].reshape(1, len(POOL_WINDOWS), GROUP, GROUP)
    pool_scale = block[520:524].reshape(1, POOL_WIDTH)
    rel_bias = block[528:544].reshape(8, 2 * GROUP)[:, :N_REL].reshape(1, 8, N_REL)
    final_gain = block[544:552].reshape(D_MODEL)
    return norm_gain, pool_w, pool_scale, rel_bias, final_gain


def kernel(x, norm_gain, w_in, pool_w, pool_scale, rel_bias, w_out, final_norm_gain, loss_target, m_norm_gain, m_w_in, m_pool_w, m_pool_scale, m_rel_bias, m_w_out, m_final_norm_gain, v_norm_gain, v_w_in, v_pool_w, v_pool_scale, v_rel_bias, v_w_out, v_final_norm_gain):
    assert x.shape[1] % TS == 0 and x.shape[2] == D_MODEL
    xs = x[0]
    tgt = loss_target[0]
    g1 = norm_gain.reshape(1, D_MODEL)
    g2 = final_norm_gain.reshape(1, D_MODEL)
    ps = pool_scale.reshape(1, POOL_WIDTH)
    pw_bf = pool_w[0].astype(BF16)

    proj, proj_bf, win_full, wout_full = _gather_inproj(xs, g1, w_in[0].astype(BF16), w_out[0].astype(BF16))
    bias_t = _bias_tile(rel_bias[0])

    a, ya = _attn_fwd(proj, proj_bf, bias_t)
    dx2, dy, gwout, gg2, loss_rows = _out_loss(xs, tgt, proj, ya, wout_full, g2, pw_bf, ps)
    dq, dk, dv, dag, db_t = _attn_bwd(proj, proj_bf, a, dy, bias_t)
    bins = _bias_bins(db_t)
    gx, dproj, ht, gg1, gps, gpw = _inproj_bwd(xs, dx2, dy, proj, dq, dk, dv, dag, win_full, pw_bf, g1, ps)

    g_bias = bins[:, :N_REL, :2].transpose(0, 2, 1).reshape(8, N_REL)
    small = _pack_small(gg1, gpw, gps, g_bias, gg2, loss_rows)
    g_win, g_wout, g_small = _gw_reduce(ht, dproj, gwout, small)
    loss = g_small[LOSS_ROW, 0]

    zeros8 = jnp.zeros((8, GROUP), F32)
    w_small = _pack_small(norm_gain, pool_w, pool_scale, rel_bias, final_norm_gain, zeros8)
    m_small = _pack_small(m_norm_gain, m_pool_w, m_pool_scale, m_rel_bias, m_final_norm_gain, zeros8)
    v_small = _pack_small(v_norm_gain, v_pool_w, v_pool_scale, v_rel_bias, v_final_norm_gain, zeros8)

    d_win, m_win, v_win = _adamw("adamw_w_in", w_in[0], g_win, m_w_in[0], v_w_in[0], 256)
    d_wout, m_wout, v_wout = _adamw("adamw_w_out", w_out[0], g_wout, m_w_out[0], v_w_out[0], 128)
    d_small, m_new_small, v_new_small = _adamw("adamw_small", w_small, g_small, m_small, v_small, SMALL_ROWS // 2)

    def full(win_part, wout_part, block):
        ng, pw, psc, rb, fg = _unpack_small(block)
        return [ng, win_part[None], pw, psc, rb, wout_part[None], fg]

    grads = full(g_win, g_wout, g_small)
    deltas = full(d_win, d_wout, d_small)
    new_m = full(m_win, m_wout, m_new_small)
    new_v = full(v_win, v_wout, v_new_small)
    return (loss, gx[None], *grads, *deltas, *new_m, *new_v)
```

```python
import numpy as np
import jax
import jax.numpy as jnp
from jax import lax
from jax.experimental import pallas as pl
from jax.experimental.pallas import tpu as pltpu

F32 = jnp.float32
BF16 = jnp.bfloat16

D_MODEL = 1024
POOL_WIDTH = 512
ATTN_WIDTH = 512
POOL_WINDOWS = (2, 4, 8, 16)
GROUP = 128
CHUNK = 64
LEFT_CHUNKS = 8
BAND = (LEFT_CHUNKS + 1) * CHUNK
HEAD_DIM = 64
N_PAIR = 4
MAX_REL = 64
N_REL = 2 * MAX_REL + 1
EPS = 1e-6
MASK_VALUE = -1e30
SCALE = 0.125

ADAM_LR = 0.001
ADAM_B1 = 0.9
ADAM_B2 = 0.999
ADAM_EPS = 1e-08
ADAM_WD = 0.01
ADAM_STEP = 10

TS = LEFT_CHUNKS * CHUNK
SUPER = 2 * CHUNK
WINDOW = BAND + CHUNK
SUPERS_PER_TILE = TS // SUPER
PAIR_LANES = 2 * SUPER
HALO = 16
N_CHIP = 4
SHARD_IN = 768
SHARD_OUT = 256
PIECE = 256
PIECES_PER_SHARD = SHARD_IN // PIECE
HALF_IN = D_MODEL // 2
HALF_OUT = SHARD_OUT // 2
SMALL_ROWS = 560
LOSS_ROW = 552
VMEM_LIMIT = 60 * 1024 * 1024

MESH = pl.DeviceIdType.MESH
ANY = pl.BlockSpec(memory_space=pl.ANY)
VMEM = pl.BlockSpec(memory_space=pltpu.VMEM)


def _sigmoid(x):
    return 1.0 / (1.0 + jnp.exp(-x))


def _dot(a, b):
    return jnp.dot(a, b, preferred_element_type=F32)


def _dot_t(a, b):
    return lax.dot_general(a, b, (((1,), (1,)), ((), ())), preferred_element_type=F32)


def _tdot(a, b):
    return lax.dot_general(a, b, (((0,), (0,)), ((), ())), preferred_element_type=F32)


def _my_place():
    return lax.axis_index("x"), lax.axis_index("y"), lax.axis_index("c")


def _other_chips(x, y):
    places = [(1 - x, y), (x, 1 - y), (1 - x, 1 - y)]
    return [(p, 2 * p[0] + p[1]) for p in places]


def _gather_inproj(x, g1, win_sh, wout_sh):
    seq = x.shape[0]
    tm = min(seq, 2 * TS)
    n_tiles = seq // tm
    cx, cy = lax.axis_index("x"), lax.axis_index("y")
    order = jnp.stack([2 * cx + cy] + [chip for _, chip in _other_chips(cx, cy)]).astype(jnp.int32)

    def body(order_ref, x_ref, g1_ref, win_ref, wout_ref, proj_ref, proj_bf_ref, win_full, wout_full,
             hbuf, wbuf, send_sems, recv_sems, local_sems):
        j = pl.program_id(0)
        i = pl.program_id(1)
        x_, y_, c = _my_place()
        b = 2 * x_ + y_
        sibling = (x_, y_, 1 - c)
        others = _other_chips(x_, y_)

        def halves(chip, core):
            return (
                win_full.at[chip, pl.ds(core * HALF_IN, HALF_IN)],
                wout_full.at[chip, pl.ds(core * HALF_OUT, HALF_OUT)],
            )

        def copy(k, src, dst, to):
            return pltpu.make_async_remote_copy(
                src_ref=src, dst_ref=dst, send_sem=send_sems.at[k], recv_sem=recv_sems.at[k],
                device_id=to, device_id_type=MESH)

        own = [
            pltpu.make_async_copy(win_ref, win_full.at[b], local_sems.at[0]),
            pltpu.make_async_copy(wout_ref, wout_full.at[b], local_sems.at[1]),
        ]
        mine_src = (win_ref.at[pl.ds(c * HALF_IN, HALF_IN)], wout_ref.at[pl.ds(c * HALF_OUT, HALF_OUT)])

        def direct(n, t):
            return copy(2 * n + t, mine_src[t], halves(b, c)[t], (*others[n][0], c))

        def arrival(n, t):
            landed = halves(others[n][1], c)[t]
            return copy(2 * n + t, landed, landed, (*others[n][0], c))

        def passing(n, t):
            landed = halves(others[n][1], c)[t]
            return copy(6 + 2 * n + t, landed, landed, sibling)

        def from_sibling(n, t):
            landed = halves(others[n][1], 1 - c)[t]
            return copy(6 + 2 * n + t, landed, landed, sibling)

        @pl.when((j == 0) & (i == 0))
        def _():
            first = pltpu.make_async_copy(win_ref, wbuf, local_sems.at[2])
            first.start()
            for cp in own:
                cp.start()
            for t in range(2):
                for n in range(N_CHIP - 1):
                    direct(n, t).start()
            first.wait()

        for n in range(N_CHIP - 1):
            @pl.when((j == n + 1) & (i == 0))
            def _(n=n):
                arrival(n, 0).wait_recv()
                passing(n, 0).start()
                from_sibling(n, 0).wait_recv()
                load = pltpu.make_async_copy(win_full.at[others[n][1]], wbuf, local_sems.at[2])
                load.start()
                load.wait()

        @pl.when((j == N_CHIP - 1) & (i == 0))
        def _():
            for n in range(N_CHIP - 1):
                arrival(n, 1).wait_recv()
                passing(n, 1).start()

        rows = pl.ds(pl.multiple_of(i * tm, tm), tm)

        @pl.when(j == 0)
        def _():
            xt = x_ref[...]
            r = lax.rsqrt(jnp.mean(xt * xt, axis=-1, keepdims=True) + EPS)
            hbuf[rows] = ((xt * r) * g1_ref[...]).astype(BF16)

        out = _dot(hbuf[rows], wbuf[...])
        proj_ref[...] = out
        proj_bf_ref[...] = out.astype(BF16)

        @pl.when((j == N_CHIP - 1) & (i == n_tiles - 1))
        def _():
            for n in range(N_CHIP - 1):
                from_sibling(n, 1).wait_recv()
            for n in range(N_CHIP - 1):
                for t in range(2):
                    direct(n, t).wait_send()
                    passing(n, t).wait_send()
            for cp in own:
                cp.wait()

    last = n_tiles - 1
    grid_spec = pltpu.PrefetchScalarGridSpec(
        num_scalar_prefetch=1,
        grid=(N_CHIP, n_tiles),
        in_specs=[
            pl.BlockSpec((tm, D_MODEL), lambda j, i, o: (jnp.where(j == 0, i, last), 0)),
            VMEM, ANY, ANY],
        out_specs=[pl.BlockSpec((tm, SHARD_IN), lambda j, i, o: (i, o[j])),
                   pl.BlockSpec((tm, SHARD_IN), lambda j, i, o: (i, o[j])), ANY, ANY],
        scratch_shapes=[
            pltpu.VMEM((seq, D_MODEL), BF16), pltpu.VMEM((D_MODEL, SHARD_IN), BF16),
            pltpu.SemaphoreType.DMA((12,)), pltpu.SemaphoreType.DMA((12,)), pltpu.SemaphoreType.DMA((3,))],
    )
    return pl.pallas_call(
        body,
        name="gather_inproj",
        grid_spec=grid_spec,
        out_shape=(
            jax.ShapeDtypeStruct((seq, N_CHIP * SHARD_IN), F32),
            jax.ShapeDtypeStruct((seq, N_CHIP * SHARD_IN), BF16),
            jax.ShapeDtypeStruct((N_CHIP, D_MODEL, SHARD_IN), BF16),
            jax.ShapeDtypeStruct((N_CHIP, SHARD_OUT, D_MODEL), BF16),
        ),
        compiler_params=pltpu.CompilerParams(
            dimension_semantics=("arbitrary", "arbitrary"), vmem_limit_bytes=VMEM_LIMIT),
    )(order, x, g1, win_sh, wout_sh)


def _window_sums(ext, forward):
    n = ext.shape[0]
    sums = []
    acc = ext
    for step in (1, 2, 4, 8):
        acc = acc + pltpu.roll(acc, (n - step) if forward else step, 0)
        sums.append(acc)
    return sums


def _row_counts(tile, rows):
    t = tile * TS + lax.broadcasted_iota(jnp.int32, (rows, GROUP), 0)
    return [jnp.minimum(t + 1, w).astype(F32) for w in POOL_WINDOWS]


def _pool_diffs(pv, prev_rows, tile):
    ext = jnp.concatenate([prev_rows, pv], axis=0)
    sums = _window_sums(ext, forward=False)
    counts = _row_counts(tile, TS)
    out = []
    for g in range(len(POOL_WINDOWS)):
        cols = slice(g * GROUP, (g + 1) * GROUP)
        out.append(sums[g][HALO:, cols] / counts[g] - pv[:, cols])
    return out


def _pool_mix(diffs, pw_ref):
    return jnp.concatenate([_dot(diffs[g], pw_ref[g]) for g in range(len(POOL_WINDOWS))], axis=1)


Q_BLOCK, K_BLOCK, V_BLOCK, AG_BLOCK = 8, 12, 16, 20


def _bias_tile(rel_bias):
    flat = jnp.concatenate(
        [jnp.broadcast_to(rel_bias[:, :1], (rel_bias.shape[0], BAND - CHUNK - 1)), rel_bias[:, :2 * MAX_REL]], axis=1)
    rows = [flat[:, CHUNK - 1 - i:CHUNK - 1 - i + BAND] for i in range(CHUNK)]
    bias = jnp.stack(rows, axis=1)
    first = jnp.pad(bias, ((0, 0), (0, 0), (0, CHUNK)), constant_values=MASK_VALUE)
    second = jnp.pad(bias, ((0, 0), (0, 0), (CHUNK, 0)), constant_values=MASK_VALUE)
    both = jnp.concatenate([first, second], axis=1)
    return both.reshape(N_PAIR, PAIR_LANES, WINDOW).transpose(0, 2, 1)


def _rel_index_tile():
    j = np.arange(WINDOW)[:, None]
    q = np.arange(PAIR_LANES)[None, :] % SUPER
    band_key = j - CHUNK * (q // CHUNK)
    idx = np.clip(band_key - LEFT_CHUNKS * CHUNK - q % CHUNK, -MAX_REL, MAX_REL) + MAX_REL
    return np.where((band_key >= 0) & (band_key < BAND), idx, -1).astype(np.int32)


def _by_head(block):
    low = lax.broadcasted_iota(jnp.int32, block.shape, 1) < HEAD_DIM
    zero = jnp.zeros_like(block)
    return jnp.concatenate([jnp.where(low, block, zero), jnp.where(low, zero, block)], axis=0)


def _own_head_rows(cross):
    head_of_row = lax.broadcasted_iota(jnp.int32, cross.shape, 0) >= HEAD_DIM
    head_of_lane = lax.broadcasted_iota(jnp.int32, cross.shape, 1) >= SUPER
    both = jnp.where(jnp.logical_xor(head_of_row, head_of_lane), 0.0, cross).T
    return both[:SUPER] + both[SUPER:]


def _with_mask_lane(q_rows):
    lane = lax.broadcasted_iota(jnp.int32, q_rows.shape, 1)
    return jnp.concatenate([q_rows, jnp.where(lane == 0, MASK_VALUE, 0.0).astype(q_rows.dtype)], axis=1)


def _band_exp(kb, q_rows, bias):
    s = _dot_t(kb, _with_mask_lane(q_rows)) + bias
    e = jnp.exp(s - jnp.max(s, axis=0, keepdims=True))
    return e, jnp.sum(e, axis=0, keepdims=True)


def _shift_band(i, band_ref, new_ref):
    @pl.when(i == 0)
    def _():
        band_ref[:TS] = jnp.zeros((TS, GROUP), band_ref.dtype)

    @pl.when(i > 0)
    def _():
        band_ref[:TS] = band_ref[TS:]

    band_ref[TS:] = new_ref[...].astype(band_ref.dtype)


def _shift_key_band(i, band_ref, new_ref):
    @pl.when(i == 0)
    def _():
        lane = lax.broadcasted_iota(jnp.int32, (TS, 2 * GROUP), 1)
        band_ref[:TS] = jnp.where(lane == GROUP, 1.0, 0.0).astype(band_ref.dtype)
        band_ref[TS:, GROUP:] = jnp.zeros((TS, GROUP), band_ref.dtype)

    @pl.when(i > 0)
    def _():
        band_ref[:TS] = band_ref[TS:]

    band_ref[TS:, :GROUP] = new_ref[...].astype(band_ref.dtype)


def _shift_band_t(i, band_ref, new_ref):
    @pl.when(i == 0)
    def _():
        band_ref[:, :TS] = jnp.zeros((GROUP, TS), band_ref.dtype)

    @pl.when(i > 0)
    def _():
        band_ref[:, :TS] = band_ref[:, TS:]

    band_ref[:, TS:] = new_ref[...].astype(band_ref.dtype).T


def _scaled_queries(q_ref, rows):
    return _by_head((q_ref[rows] * SCALE).astype(BF16))


def _attn_fwd(proj, proj_bf, bias_t):
    seq = proj.shape[0]
    n_tiles = seq // TS

    def body(q_ref, k_ref, v_ref, ag_ref, bias_ref, a_ref, ya_ref, e_ref, inv_ref, kband, vband_t):
        i = pl.program_id(1)
        _shift_key_band(i, kband, k_ref)
        _shift_band_t(i, vband_t, v_ref)

        def weights(sc):
            rows = slice(sc * SUPER, (sc + 1) * SUPER)
            win = slice(sc * SUPER, sc * SUPER + WINDOW)
            e, total = _band_exp(kband[win], _scaled_queries(q_ref, rows), bias_ref[0])
            e_ref[0, sc] = e.astype(BF16)
            inv_total = 1.0 / total
            inv_ref[0, sc] = jnp.broadcast_to(inv_total, (8, PAIR_LANES))
            return inv_total

        nxt = weights(0)
        for sc in range(SUPERS_PER_TILE):
            rows = slice(sc * SUPER, (sc + 1) * SUPER)
            win = slice(sc * SUPER, sc * SUPER + WINDOW)
            inv_total = nxt
            if sc + 1 < SUPERS_PER_TILE:
                nxt = weights(sc + 1)
            a = _own_head_rows(_dot(vband_t[:, win], e_ref[0, sc]) * inv_total)
            a_ref[rows] = a
            g = ag_ref[rows]
            ya_ref[rows] = (a * (g * _sigmoid(g))).astype(BF16)

    blk = pl.BlockSpec((TS, GROUP), lambda p, i: (i, p))

    def cols(first):
        return pl.BlockSpec((TS, GROUP), lambda p, i: (i, first + p))

    return pl.pallas_call(
        body,
        name="attn_fwd",
        grid=(N_PAIR, n_tiles),
        in_specs=[cols(Q_BLOCK), cols(K_BLOCK), cols(V_BLOCK), cols(AG_BLOCK),
                  pl.BlockSpec((1, WINDOW, PAIR_LANES), lambda p, i: (p, 0, 0))],
        out_specs=[blk, blk,
                   pl.BlockSpec((1, SUPERS_PER_TILE, WINDOW, PAIR_LANES), lambda p, i: (p, i, 0, 0)),
                   pl.BlockSpec((1, SUPERS_PER_TILE, 8, PAIR_LANES), lambda p, i: (p, i, 0, 0))],
        out_shape=[jax.ShapeDtypeStruct((seq, ATTN_WIDTH), F32), jax.ShapeDtypeStruct((seq, ATTN_WIDTH), BF16),
                   jax.ShapeDtypeStruct((N_PAIR, seq // SUPER, WINDOW, PAIR_LANES), BF16),
                   jax.ShapeDtypeStruct((N_PAIR, seq // SUPER, 8, PAIR_LANES), F32)],
        scratch_shapes=[pltpu.VMEM((2 * TS, 2 * GROUP), BF16), pltpu.VMEM((GROUP, 2 * TS), BF16)],
        compiler_params=pltpu.CompilerParams(
            dimension_semantics=("arbitrary", "arbitrary"), vmem_limit_bytes=VMEM_LIMIT),
    )(proj_bf, proj_bf, proj_bf, proj, bias_t)


def _attn_bwd(proj, proj_bf, a, dy, e_all, inv_all):
    seq = proj.shape[0]
    n_tiles = seq // TS

    def body(q_ref, k_ref, v_ref, a_ref, ag_ref, dy_ref, e_ref, inv_ref,
             dq_ref, dk_ref, dv_ref, dag_ref, db_ref, vband, kband_t, dkacc, dvacc):
        i = pl.program_id(1)

        @pl.when(i == 0)
        def _():
            dkacc[...] = jnp.zeros_like(dkacc)
            dvacc[...] = jnp.zeros_like(dvacc)
            db_ref[...] = jnp.zeros_like(db_ref)

        @pl.when(i < n_tiles)
        def _():
            _shift_band(i, vband, v_ref)
            _shift_band_t(i, kband_t, k_ref)

            def score_grads(sc):
                rows = slice(sc * SUPER, (sc + 1) * SUPER)
                win = slice(sc * SUPER, sc * SUPER + WINDOW)
                q_rows = _scaled_queries(q_ref, rows)
                g = ag_ref[rows]
                sg = _sigmoid(g)
                dyc = dy_ref[rows]
                dag_ref[rows] = dyc * a_ref[rows] * (sg * (1.0 + g * (1.0 - sg)))
                da_rows = _by_head((dyc * (g * sg)).astype(BF16))
                p = e_ref[0, sc].astype(F32) * inv_ref[0, sc, :1]
                dp = _dot_t(vband[win], da_rows)
                ds = p * (dp - jnp.sum(p * dp, axis=0, keepdims=True))
                db_ref[0] += ds
                return q_rows, da_rows, p.astype(BF16), ds.astype(BF16)

            nxt = score_grads(0)
            for sc in range(SUPERS_PER_TILE):
                rows = slice(sc * SUPER, (sc + 1) * SUPER)
                win = slice(sc * SUPER, sc * SUPER + WINDOW)
                q_rows, da_rows, p_bf, ds_bf = nxt
                if sc + 1 < SUPERS_PER_TILE:
                    nxt = score_grads(sc + 1)
                dq_ref[rows] = (_own_head_rows(_dot(kband_t[:, win], ds_bf)) * SCALE).astype(BF16)
                dkacc[win] += _dot(ds_bf, q_rows)
                dvacc[win] += _dot(p_bf, da_rows)

        dk_ref[...] = dkacc[:TS].astype(BF16)
        dv_ref[...] = dvacc[:TS].astype(BF16)
        dkacc[:TS] = dkacc[TS:]
        dvacc[:TS] = dvacc[TS:]
        dkacc[TS:] = jnp.zeros((TS, GROUP), F32)
        dvacc[TS:] = jnp.zeros((TS, GROUP), F32)

    last = n_tiles - 1
    cur = pl.BlockSpec((TS, GROUP), lambda p, i: (jnp.minimum(i, last), p))
    older = pl.BlockSpec((TS, GROUP), lambda p, i: (jnp.maximum(i - 1, 0), p))
    dy_blk = pl.BlockSpec((TS, GROUP), lambda p, i: (jnp.minimum(i, last), N_PAIR + p))
    per_pair = pl.BlockSpec((1, WINDOW, PAIR_LANES), lambda p, i: (p, 0, 0))

    def cols(first):
        return pl.BlockSpec((TS, GROUP), lambda p, i: (jnp.minimum(i, last), first + p))

    def kept(rows):
        return pl.BlockSpec((1, SUPERS_PER_TILE, rows, PAIR_LANES), lambda p, i: (p, jnp.minimum(i, last), 0, 0))

    def out(dtype):
        return jax.ShapeDtypeStruct((seq, ATTN_WIDTH), dtype)

    return pl.pallas_call(
        body,
        name="attn_bwd",
        grid=(N_PAIR, n_tiles + 1),
        in_specs=[cols(Q_BLOCK), cols(K_BLOCK), cols(V_BLOCK), cur, cols(AG_BLOCK), dy_blk, kept(WINDOW), kept(8)],
        out_specs=[cur, older, older, cur, per_pair],
        out_shape=[out(BF16), out(BF16), out(BF16), out(F32),
                   jax.ShapeDtypeStruct((N_PAIR, WINDOW, PAIR_LANES), F32)],
        scratch_shapes=[
            pltpu.VMEM((2 * TS, GROUP), BF16), pltpu.VMEM((GROUP, 2 * TS), BF16),
            pltpu.VMEM((2 * TS, GROUP), F32), pltpu.VMEM((2 * TS, GROUP), F32)],
        compiler_params=pltpu.CompilerParams(
            dimension_semantics=("arbitrary", "arbitrary"), vmem_limit_bytes=VMEM_LIMIT),
    )(proj_bf, proj_bf, proj_bf, a, proj, dy, e_all, inv_all)


BIN_ROWS = 136


def _bias_bins(db_t):
    idx_t = jnp.asarray(_rel_index_tile())

    def body(db_ref, idx_ref, out_ref):
        lane = lax.broadcasted_iota(jnp.int32, (1, GROUP), 1)
        row = lax.broadcasted_iota(jnp.int32, (BIN_ROWS, GROUP), 0)
        out = jnp.zeros((BIN_ROWS, GROUP), F32)
        for r in range(N_REL - 1):
            lo = 0 if r == 0 else ((BAND - 2 * CHUNK + r) // 8) * 8
            hi = WINDOW if r == 0 else min(WINDOW, lo + SUPER + 8)
            hit = jnp.where(idx_ref[lo:hi] == r, db_ref[0, lo:hi], 0.0)
            col = jnp.sum(hit, axis=0, keepdims=True)
            s0 = jnp.sum(col[:, :SUPER], axis=1, keepdims=True)
            s1 = jnp.sum(col[:, SUPER:], axis=1, keepdims=True)
            val = jnp.where(lane == 0, s0, jnp.where(lane == 1, s1, 0.0))
            out = jnp.where(row == r, val, out)
        out_ref[0] = out

    return pl.pallas_call(
        body,
        name="bias_bins",
        grid=(N_PAIR,),
        in_specs=[pl.BlockSpec((1, WINDOW, PAIR_LANES), lambda p: (p, 0, 0)), VMEM],
        out_specs=pl.BlockSpec((1, BIN_ROWS, GROUP), lambda p: (p, 0, 0)),
        out_shape=jax.ShapeDtypeStruct((N_PAIR, BIN_ROWS, GROUP), F32),
        compiler_params=pltpu.CompilerParams(dimension_semantics=("arbitrary",)),
    )(db_t, idx_t)


def _out_loss(x, tgt, proj, ya, wout_full, g2, pw_bf, ps):
    seq = x.shape[0]
    n_tiles = seq // TS
    half = TS // 2

    def body(x_ref, t_ref, pv_ref, pg_ref, ya_ref, w_ref, g2_ref, pw_ref, ps_ref,
             dx2_ref, dy_ref, gw_ref, gg_ref, loss_ref, sq_ref, halo_ref):
        i = pl.program_id(0)

        @pl.when(i == 0)
        def _():
            gw_ref[...] = jnp.zeros_like(gw_ref)
            gg_ref[...] = jnp.zeros_like(gg_ref)
            sq_ref[...] = jnp.zeros_like(sq_ref)
            halo_ref[...] = jnp.zeros_like(halo_ref)

        pv = pv_ref[...]
        pg = pg_ref[...]
        diffs = [d.astype(BF16) for d in _pool_diffs(pv, halo_ref[...], i)]
        halo_ref[...] = pv[TS - HALO:, :]
        yp = ((_pool_mix(diffs, pw_ref) * ps_ref[...]) * (pg * _sigmoid(pg))).astype(BF16)
        g2v = g2_ref[...]

        def parts(rows):
            return [yp[rows, :SHARD_OUT], yp[rows, SHARD_OUT:], ya_ref[rows, :SHARD_OUT], ya_ref[rows, SHARD_OUT:]]

        def project(rows, ys):
            x2 = x_ref[rows]
            for b in range(N_CHIP):
                x2 = x2 + _dot(ys[b], w_ref[b])
            return x2

        def norm_loss(rows, x2):
            r = lax.rsqrt(jnp.mean(x2 * x2, axis=-1, keepdims=True) + EPS)
            xh = x2 * r
            diff = xh * g2v - t_ref[rows]
            sq_ref[...] += jnp.sum(diff * diff, axis=0, keepdims=True)
            dfin = diff * (1.0 / D_MODEL)
            gg_ref[...] += jnp.sum(dfin * xh, axis=0, keepdims=True)
            dxh = dfin * g2v
            dx2 = r * (dxh - xh * jnp.mean(dxh * xh, axis=-1, keepdims=True))
            dx2_ref[rows] = dx2
            return dx2.astype(BF16)

        def back(rows, ys, dx2_bf):
            for b in range(N_CHIP):
                gw_ref[b] += _tdot(ys[b], dx2_bf)
                dy_ref[rows, b * SHARD_OUT:(b + 1) * SHARD_OUT] = _dot_t(dx2_bf, w_ref[b])

        top, bottom = slice(0, half), slice(half, TS)
        ys_top, ys_bottom = parts(top), parts(bottom)
        x2_top = project(top, ys_top)
        x2_bottom = project(bottom, ys_bottom)
        d_top = norm_loss(top, x2_top)
        back(top, ys_top, d_top)
        d_bottom = norm_loss(bottom, x2_bottom)
        back(bottom, ys_bottom, d_bottom)

        @pl.when(i == n_tiles - 1)
        def _():
            total = jnp.sum(sq_ref[...], axis=1, keepdims=True) * (0.5 / D_MODEL)
            loss_ref[...] = jnp.broadcast_to(total, loss_ref.shape)

    def rows(width, col=0):
        return pl.BlockSpec((TS, width), lambda i: (i, col))

    return pl.pallas_call(
        body,
        name="out_loss",
        grid=(n_tiles,),
        in_specs=[rows(D_MODEL), rows(D_MODEL), rows(POOL_WIDTH, 0), rows(POOL_WIDTH, 1), rows(ATTN_WIDTH),
                  VMEM, VMEM, VMEM, VMEM],
        out_specs=[rows(D_MODEL), rows(D_MODEL), VMEM, VMEM, VMEM],
        out_shape=[
            jax.ShapeDtypeStruct((seq, D_MODEL), F32), jax.ShapeDtypeStruct((seq, D_MODEL), F32),
            jax.ShapeDtypeStruct((N_CHIP, SHARD_OUT, D_MODEL), F32), jax.ShapeDtypeStruct((1, D_MODEL), F32),
            jax.ShapeDtypeStruct((8, GROUP), F32)],
        scratch_shapes=[pltpu.VMEM((1, D_MODEL), F32), pltpu.VMEM((HALO, POOL_WIDTH), F32)],
        compiler_params=pltpu.CompilerParams(dimension_semantics=("arbitrary",), vmem_limit_bytes=VMEM_LIMIT),
    )(x, tgt, proj, proj, ya, wout_full, g2, pw_bf, ps)


def _inproj_bwd(x, dx2, dy, proj, dq, dk, dv, dag, win_full, pw_bf, g1, ps):
    seq = x.shape[0]
    n_tiles = seq // TS

    def body(x_ref, dx2_ref, dyp_ref, pv_ref, pvprev_ref, pg_ref, dq_ref, dk_ref, dv_ref, dag_ref,
             w_ref, pw_ref, g1_ref, ps_ref, gx_ref, dproj_ref, ht_ref, gg_ref, gps_ref, gpw_ref, halo_ref):
        i = pl.program_id(0)
        tile = n_tiles - 1 - i

        @pl.when(i == 0)
        def _():
            gg_ref[...] = jnp.zeros_like(gg_ref)
            gps_ref[...] = jnp.zeros_like(gps_ref)
            gpw_ref[...] = jnp.zeros_like(gpw_ref)
            halo_ref[...] = jnp.zeros_like(halo_ref)

        pv_t = pv_ref[...]
        pg_t = pg_ref[...]
        prev_rows = jnp.where(tile > 0, pvprev_ref[...], 0.0)
        diffs = [d.astype(BF16) for d in _pool_diffs(pv_t, prev_rows, tile)]
        mixed = _pool_mix(diffs, pw_ref)
        sg = _sigmoid(pg_t)
        silu = pg_t * sg
        dyp = dyp_ref[...]
        psv = ps_ref[...]
        gps_ref[...] += jnp.sum(dyp * mixed * silu, axis=0, keepdims=True)
        dmixed = (dyp * psv * silu).astype(BF16)
        dpg = dyp * (mixed * psv) * (sg * (1.0 + pg_t * (1.0 - sg)))
        counts = _row_counts(tile, TS)
        dds = []
        for g in range(len(POOL_WINDOWS)):
            dm_g = dmixed[:, g * GROUP:(g + 1) * GROUP]
            gpw_ref[g] += _tdot(diffs[g], dm_g)
            dds.append(_dot_t(dm_g, pw_ref[g]))
        dd = jnp.concatenate(dds, axis=1)
        spread = jnp.concatenate([dds[g] / counts[g] for g in range(len(POOL_WINDOWS))], axis=1)
        sums = _window_sums(jnp.concatenate([spread, halo_ref[...]], axis=0), forward=True)
        halo_ref[...] = spread[:HALO]
        dpv = jnp.concatenate(
            [sums[g][:TS, g * GROUP:(g + 1) * GROUP] for g in range(len(POOL_WINDOWS))], axis=1) - dd

        xt = x_ref[...]
        r = lax.rsqrt(jnp.mean(xt * xt, axis=-1, keepdims=True) + EPS)
        xh = xt * r
        g1v = g1_ref[...]
        ht_ref[...] = (xh * g1v).astype(BF16).T
        dproj = jnp.concatenate(
            [dpv.astype(BF16), dpg.astype(BF16), dq_ref[...], dk_ref[...], dv_ref[...], dag_ref[...].astype(BF16)],
            axis=1)
        dproj_ref[...] = dproj
        dh = _dot_t(dproj[:, :SHARD_IN], w_ref[0])
        for chip in range(1, N_CHIP):
            dh = dh + _dot_t(dproj[:, chip * SHARD_IN:(chip + 1) * SHARD_IN], w_ref[chip])

        gg_ref[...] += jnp.sum(dh * xh, axis=0, keepdims=True)
        dxh = dh * g1v
        gx_ref[...] = dx2_ref[...] + r * (dxh - xh * jnp.mean(dxh * xh, axis=-1, keepdims=True))

    def rows(width, col=0):
        return pl.BlockSpec((TS, width), lambda i: (n_tiles - 1 - i, col))

    prev = pl.BlockSpec((HALO, POOL_WIDTH), lambda i: (jnp.maximum((n_tiles - 1 - i) * (TS // HALO) - 1, 0), 0))
    return pl.pallas_call(
        body,
        name="inproj_bwd",
        grid=(n_tiles,),
        in_specs=[rows(D_MODEL), rows(D_MODEL), rows(POOL_WIDTH), rows(POOL_WIDTH), prev, rows(POOL_WIDTH, 1),
                  rows(ATTN_WIDTH), rows(ATTN_WIDTH), rows(ATTN_WIDTH), rows(ATTN_WIDTH), VMEM, VMEM, VMEM, VMEM],
        out_specs=[rows(D_MODEL), rows(N_CHIP * SHARD_IN),
                   pl.BlockSpec((D_MODEL, TS), lambda i: (0, n_tiles - 1 - i)), VMEM, VMEM, VMEM],
        out_shape=[
            jax.ShapeDtypeStruct((seq, D_MODEL), F32),
            jax.ShapeDtypeStruct((seq, N_CHIP * SHARD_IN), BF16),
            jax.ShapeDtypeStruct((D_MODEL, seq), BF16),
            jax.ShapeDtypeStruct((1, D_MODEL), F32),
            jax.ShapeDtypeStruct((1, POOL_WIDTH), F32),
            jax.ShapeDtypeStruct((len(POOL_WINDOWS), GROUP, GROUP), F32)],
        scratch_shapes=[pltpu.VMEM((HALO, POOL_WIDTH), F32)],
        compiler_params=pltpu.CompilerParams(dimension_semantics=("arbitrary",), vmem_limit_bytes=VMEM_LIMIT),
    )(x, dx2, dy, proj, proj, proj, dq, dk, dv, dag, win_full, pw_bf, g1, ps)


PASS_PEER = (2, 0, 1)


def _gw_reduce(ht, dproj, gwout, small):
    seq = ht.shape[1]
    tm = min(2 * TS, seq // 8)
    n_tiles = seq // tm
    half_small = SMALL_ROWS // 2
    cx, cy = lax.axis_index("x"), lax.axis_index("y")
    outer = _other_chips(cx, cy)
    order = jnp.stack([outer[n][1] for n in PASS_PEER] + [2 * cx + cy]).astype(jnp.int32)

    def body(order_ref, ht_ref, dp_ref, gwout_ref, small_ref, gin_final, gout_final, small_final,
             hbuf, acc, pair_in, pair_out, pair_small, tx_in, tx_out, rx_in, rx_out, rx_small,
             gin_out, gout_out, small_out, send_sems, recv_sems):
        j = pl.program_id(0)
        i = pl.program_id(1)
        x, y, c = _my_place()
        b = 2 * x + y
        sibling = (x, y, 1 - c)
        others = _other_chips(x, y)
        mine_in = pl.ds(pl.multiple_of(c * HALF_IN, HALF_IN), HALF_IN)
        mine_out = pl.ds(pl.multiple_of(c * HALF_OUT, HALF_OUT), HALF_OUT)
        mine_small = pl.ds(pl.multiple_of(c * half_small, 8), half_small)
        theirs_in = pl.ds(pl.multiple_of((1 - c) * HALF_IN, HALF_IN), HALF_IN)
        theirs_out = pl.ds(pl.multiple_of((1 - c) * HALF_OUT, HALF_OUT), HALF_OUT)
        theirs_small = pl.ds(pl.multiple_of((1 - c) * half_small, 8), half_small)

        def copy(k, src, dst, to):
            return pltpu.make_async_remote_copy(
                src_ref=src, dst_ref=dst, send_sem=send_sems.at[k], recv_sem=recv_sems.at[k],
                device_id=to, device_id_type=MESH)

        swap_out = copy(0, gwout_ref.at[:, theirs_out], pair_out, sibling)
        swap_small = copy(1, small_ref, pair_small, sibling)

        def swap_in(p):
            return copy(2 + p, acc.at[p % 2, theirs_in], pair_in.at[p % 2], sibling)

        def to_chip(n, t):
            to = (*others[n][0], c)
            if t == 0:
                return copy(6 + 3 * n, tx_in.at[n], rx_in.at[n], to)
            if t == 1:
                return copy(7 + 3 * n, tx_out.at[n], rx_out.at[n], to)
            return copy(8 + 3 * n, pair_small.at[mine_small], rx_small.at[b], to)

        share_in = copy(15, gin_out.at[mine_in], gin_out.at[mine_in], sibling)
        share_out = copy(16, gout_out.at[mine_out], gout_out.at[mine_out], sibling)
        share_small = copy(17, small_out.at[mine_small], small_out.at[mine_small], sibling)

        def at(jj, ii):
            return (j == jj) & (i == ii)

        par = j % 2

        @pl.when(i == 0)
        def _():
            acc[par] = jnp.zeros((D_MODEL, SHARD_IN), F32)

        cols = pl.ds(pl.multiple_of(i * tm, tm), tm)

        @pl.when(j == 0)
        def _():
            hbuf[:, cols] = ht_ref[...]

        acc[par] += _dot(hbuf[:, cols], dp_ref[...])

        @pl.when(at(0, 0))
        def _():
            swap_out.start()
            swap_small.start()

        @pl.when(at(0, 2))
        def _():
            swap_out.wait_recv()
            swap_small.wait_recv()
            for chip in range(N_CHIP):
                pair_out[chip] = gwout_ref[chip, mine_out] + pair_out[chip]
            pair_small[...] = small_ref[...] + pair_small[...]
            rx_small[b] = pair_small[mine_small]
            for n in range(N_CHIP - 1):
                tx_out[n] = pair_out[others[n][1]].astype(BF16)
                to_chip(n, 1).start()
                to_chip(n, 2).start()

        @pl.when(at(1, 4))
        def _():
            total_out = pair_out[b]
            for n in range(N_CHIP - 1):
                to_chip(n, 1).wait_recv()
                copy(8 + 3 * n, pair_small.at[mine_small], rx_small.at[others[n][1]], (*others[n][0], c)).wait_recv()
                total_out = total_out + rx_out[n].astype(F32)
            gout_out[mine_out] = total_out
            small_out[mine_small] = ((rx_small[0] + rx_small[1]) + rx_small[2]) + rx_small[3]
            share_out.start()
            share_small.start()

        for p in range(N_CHIP - 1):
            n = PASS_PEER[p]

            @pl.when(at(p + 1, 0))
            def _(p=p):
                swap_in(p).start()

            @pl.when(at(p + 1, 2))
            def _(p=p, n=n):
                swap_in(p).wait_recv()
                swap_in(p).wait_send()
                tx_in[n] = (acc[p % 2, mine_in] + pair_in[p % 2]).astype(BF16)
                to_chip(n, 0).start()

        @pl.when(at(N_CHIP - 1, n_tiles - 1))
        def _():
            last = N_CHIP - 1
            swap_in(last).start()
            swap_in(last).wait_recv()
            total_in = acc[last % 2, mine_in] + pair_in[last % 2]
            for n in range(N_CHIP - 1):
                to_chip(n, 0).wait_recv()
                total_in = total_in + rx_in[n].astype(F32)
            gin_out[mine_in] = total_in
            share_in.start()
            copy(15, gin_out.at[theirs_in], gin_out.at[theirs_in], sibling).wait_recv()
            copy(16, gout_out.at[theirs_out], gout_out.at[theirs_out], sibling).wait_recv()
            copy(17, small_out.at[theirs_small], small_out.at[theirs_small], sibling).wait_recv()
            swap_out.wait_send()
            swap_small.wait_send()
            swap_in(last).wait_send()
            for n in range(N_CHIP - 1):
                for t in range(3):
                    to_chip(n, t).wait_send()
            share_in.wait_send()
            share_out.wait_send()
            share_small.wait_send()
            gin_final[...] = gin_out[...]
            gout_final[...] = gout_out[...]
            small_final[...] = small_out[...]

    assert n_tiles >= 5, "the reduction's steps are spread over the first five token steps of a pass"
    grid_spec = pltpu.PrefetchScalarGridSpec(
        num_scalar_prefetch=1,
        grid=(N_CHIP, n_tiles),
        in_specs=[
            pl.BlockSpec((D_MODEL, tm), lambda j, i, o: (0, jnp.where(j == 0, i, n_tiles - 1))),
            pl.BlockSpec((tm, SHARD_IN), lambda j, i, o: (i, o[j])),
            VMEM, VMEM],
        out_specs=[VMEM, VMEM, VMEM],
        scratch_shapes=[
            pltpu.VMEM((D_MODEL, seq), BF16),
            pltpu.VMEM((2, D_MODEL, SHARD_IN), F32),
            pltpu.VMEM((2, HALF_IN, SHARD_IN), F32),
            pltpu.VMEM((N_CHIP, HALF_OUT, D_MODEL), F32),
            pltpu.VMEM((SMALL_ROWS, GROUP), F32),
            pltpu.VMEM((N_CHIP - 1, HALF_IN, SHARD_IN), BF16),
            pltpu.VMEM((N_CHIP - 1, HALF_OUT, D_MODEL), BF16),
            pltpu.VMEM((N_CHIP - 1, HALF_IN, SHARD_IN), BF16),
            pltpu.VMEM((N_CHIP - 1, HALF_OUT, D_MODEL), BF16),
            pltpu.VMEM((N_CHIP, half_small, GROUP), F32),
            pltpu.VMEM((D_MODEL, SHARD_IN), F32),
            pltpu.VMEM((SHARD_OUT, D_MODEL), F32),
            pltpu.VMEM((SMALL_ROWS, GROUP), F32),
            pltpu.SemaphoreType.DMA((18,)),
            pltpu.SemaphoreType.DMA((18,)),
        ],
    )
    return pl.pallas_call(
        body,
        name="gw_reduce",
        grid_spec=grid_spec,
        out_shape=(
            jax.ShapeDtypeStruct((D_MODEL, SHARD_IN), F32),
            jax.ShapeDtypeStruct((SHARD_OUT, D_MODEL), F32),
            jax.ShapeDtypeStruct((SMALL_ROWS, GROUP), F32),
        ),
        compiler_params=pltpu.CompilerParams(
            dimension_semantics=("arbitrary", "arbitrary"), vmem_limit_bytes=VMEM_LIMIT),
    )(order, ht, dproj, gwout, small)


def _adamw(name, w, g, m, v, block_rows):
    rows, cols = w.shape

    def body(w_ref, g_ref, m_ref, v_ref, d_ref, m_out, v_out):
        grad = g_ref[...]
        m_new = ADAM_B1 * m_ref[...] + (1.0 - ADAM_B1) * grad
        v_new = ADAM_B2 * v_ref[...] + (1.0 - ADAM_B2) * (grad * grad)
        m_hat = m_new / (1.0 - ADAM_B1 ** ADAM_STEP)
        v_hat = v_new / (1.0 - ADAM_B2 ** ADAM_STEP)
        d_ref[...] = -ADAM_LR * (m_hat / (jnp.sqrt(v_hat) + ADAM_EPS) + ADAM_WD * w_ref[...])
        m_out[...] = m_new
        v_out[...] = v_new

    blk = pl.BlockSpec((block_rows, cols), lambda i: (i, 0))
    shape = jax.ShapeDtypeStruct((rows, cols), F32)
    return pl.pallas_call(
        body,
        name=name,
        grid=(rows // block_rows,),
        in_specs=[blk] * 4,
        out_specs=[blk] * 3,
        out_shape=[shape] * 3,
        compiler_params=pltpu.CompilerParams(dimension_semantics=("arbitrary",)),
    )(w, g, m, v)


def _pack_small(norm_gain, pool_w, pool_scale, rel_bias, final_gain, loss_rows):
    parts = [
        norm_gain.reshape(8, GROUP),
        pool_w.reshape(len(POOL_WINDOWS) * GROUP, GROUP),
        jnp.pad(pool_scale.reshape(4, GROUP), ((0, 4), (0, 0))),
        jnp.pad(rel_bias.reshape(8, N_REL), ((0, 0), (0, 2 * GROUP - N_REL))).reshape(16, GROUP),
        final_gain.reshape(8, GROUP),
        loss_rows,
    ]
    return jnp.concatenate(parts, axis=0)


def _unpack_small(block):
    norm_gain = block[0:8].reshape(1, D_MODEL)
    pool_w = block[8:520].reshape(1, len(POOL_WINDOWS), GROUP, GROUP)
    pool_scale = block[520:524].reshape(1, POOL_WIDTH)
    rel_bias = block[528:544].reshape(8, 2 * GROUP)[:, :N_REL].reshape(1, 8, N_REL)
    final_gain = block[544:552].reshape(D_MODEL)
    return norm_gain, pool_w, pool_scale, rel_bias, final_gain


def kernel(x, norm_gain, w_in, pool_w, pool_scale, rel_bias, w_out, final_norm_gain, loss_target, m_norm_gain, m_w_in, m_pool_w, m_pool_scale, m_rel_bias, m_w_out, m_final_norm_gain, v_norm_gain, v_w_in, v_pool_w, v_pool_scale, v_rel_bias, v_w_out, v_final_norm_gain):
    assert x.shape[1] % TS == 0 and x.shape[2] == D_MODEL
    xs = x[0]
    tgt = loss_target[0]
    g1 = norm_gain.reshape(1, D_MODEL)
    g2 = final_norm_gain.reshape(1, D_MODEL)
    ps = pool_scale.reshape(1, POOL_WIDTH)
    pw_bf = pool_w[0].astype(BF16)

    proj, proj_bf, win_full, wout_full = _gather_inproj(xs, g1, w_in[0].astype(BF16), w_out[0].astype(BF16))
    bias_t = _bias_tile(rel_bias[0])

    a, ya, e_all, inv_all = _attn_fwd(proj, proj_bf, bias_t)
    dx2, dy, gwout, gg2, loss_rows = _out_loss(xs, tgt, proj, ya, wout_full, g2, pw_bf, ps)
    dq, dk, dv, dag, db_t = _attn_bwd(proj, proj_bf, a, dy, e_all, inv_all)
    bins = _bias_bins(db_t)
    gx, dproj, ht, gg1, gps, gpw = _inproj_bwd(xs, dx2, dy, proj, dq, dk, dv, dag, win_full, pw_bf, g1, ps)

    g_bias = bins[:, :N_REL, :2].transpose(0, 2, 1).reshape(8, N_REL)
    small = _pack_small(gg1, gpw, gps, g_bias, gg2, loss_rows)
    g_win, g_wout, g_small = _gw_reduce(ht, dproj, gwout, small)
    loss = g_small[LOSS_ROW, 0]

    zeros8 = jnp.zeros((8, GROUP), F32)
    w_small = _pack_small(norm_gain, pool_w, pool_scale, rel_bias, final_norm_gain, zeros8)
    m_small = _pack_small(m_norm_gain, m_pool_w, m_pool_scale, m_rel_bias, m_final_norm_gain, zeros8)
    v_small = _pack_small(v_norm_gain, v_pool_w, v_pool_scale, v_rel_bias, v_final_norm_gain, zeros8)

    d_win, m_win, v_win = _adamw("adamw_w_in", w_in[0], g_win, m_w_in[0], v_w_in[0], 256)
    d_wout, m_wout, v_wout = _adamw("adamw_w_out", w_out[0], g_wout, m_w_out[0], v_w_out[0], 128)
    d_small, m_new_small, v_new_small = _adamw("adamw_small", w_small, g_small, m_small, v_small, SMALL_ROWS // 2)

    def full(win_part, wout_part, block):
        ng, pw, psc, rb, fg = _unpack_small(block)
        return [ng, win_part[None], pw, psc, rb, wout_part[None], fg]

    grads = full(g_win, g_wout, g_small)
    deltas = full(d_win, d_wout, d_small)
    new_m = full(m_win, m_wout, m_new_small)
    new_v = full(v_win, v_wout, v_new_small)
    return (loss, gx[None], *grads, *deltas, *new_m, *new_v)
```

```python
import numpy as np
import jax
import jax.numpy as jnp
from jax import lax
from jax.experimental import pallas as pl
from jax.experimental.pallas import tpu as pltpu

F32 = jnp.float32
BF16 = jnp.bfloat16

D_MODEL = 1024
POOL_WIDTH = 512
ATTN_WIDTH = 512
POOL_WINDOWS = (2, 4, 8, 16)
GROUP = 128
CHUNK = 64
LEFT_CHUNKS = 8
BAND = (LEFT_CHUNKS + 1) * CHUNK
HEAD_DIM = 64
N_PAIR = 4
MAX_REL = 64
N_REL = 2 * MAX_REL + 1
EPS = 1e-6
MASK_VALUE = -1e30
SCALE = 0.125

ADAM_LR = 0.001
ADAM_B1 = 0.9
ADAM_B2 = 0.999
ADAM_EPS = 1e-08
ADAM_WD = 0.01
ADAM_STEP = 10

TS = LEFT_CHUNKS * CHUNK
SUPER = 2 * CHUNK
WINDOW = BAND + CHUNK
TA = 2 * TS
WIN_BASE = TA - LEFT_CHUNKS * CHUNK
SUPERS_PER_TILE = TA // SUPER
PAIR_LANES = 2 * SUPER
HALO = 16
N_CHIP = 4
SHARD_IN = 768
SHARD_OUT = 256
PIECE = 256
PIECES_PER_SHARD = SHARD_IN // PIECE
HALF_IN = D_MODEL // 2
HALF_OUT = SHARD_OUT // 2
SMALL_ROWS = 560
LOSS_ROW = 552
VMEM_LIMIT = 60 * 1024 * 1024

MESH = pl.DeviceIdType.MESH
ANY = pl.BlockSpec(memory_space=pl.ANY)
VMEM = pl.BlockSpec(memory_space=pltpu.VMEM)


def _sigmoid(x):
    return 1.0 / (1.0 + jnp.exp(-x))


def _dot(a, b):
    return jnp.dot(a, b, preferred_element_type=F32)


def _dot_t(a, b):
    return lax.dot_general(a, b, (((1,), (1,)), ((), ())), preferred_element_type=F32)


def _tdot(a, b):
    return lax.dot_general(a, b, (((0,), (0,)), ((), ())), preferred_element_type=F32)


def _my_place():
    return lax.axis_index("x"), lax.axis_index("y"), lax.axis_index("c")


def _other_chips(x, y):
    places = [(1 - x, y), (x, 1 - y), (1 - x, 1 - y)]
    return [(p, 2 * p[0] + p[1]) for p in places]


def _gather_inproj(x, g1, win_sh, wout_sh):
    seq = x.shape[0]
    tm = min(seq, 2 * TS)
    n_tiles = seq // tm
    cx, cy = lax.axis_index("x"), lax.axis_index("y")
    order = jnp.stack([2 * cx + cy] + [chip for _, chip in _other_chips(cx, cy)]).astype(jnp.int32)

    def body(order_ref, x_ref, g1_ref, win_ref, wout_ref, proj_ref, proj_bf_ref, win_full, wout_full,
             hbuf, wbuf, send_sems, recv_sems, local_sems):
        j = pl.program_id(0)
        i = pl.program_id(1)
        x_, y_, c = _my_place()
        b = 2 * x_ + y_
        sibling = (x_, y_, 1 - c)
        others = _other_chips(x_, y_)

        def halves(chip, core):
            return (
                win_full.at[chip, pl.ds(core * HALF_IN, HALF_IN)],
                wout_full.at[chip, pl.ds(core * HALF_OUT, HALF_OUT)],
            )

        def copy(k, src, dst, to):
            return pltpu.make_async_remote_copy(
                src_ref=src, dst_ref=dst, send_sem=send_sems.at[k], recv_sem=recv_sems.at[k],
                device_id=to, device_id_type=MESH)

        own = [
            pltpu.make_async_copy(win_ref, win_full.at[b], local_sems.at[0]),
            pltpu.make_async_copy(wout_ref, wout_full.at[b], local_sems.at[1]),
        ]
        mine_src = (win_ref.at[pl.ds(c * HALF_IN, HALF_IN)], wout_ref.at[pl.ds(c * HALF_OUT, HALF_OUT)])

        def direct(n, t):
            return copy(2 * n + t, mine_src[t], halves(b, c)[t], (*others[n][0], c))

        def arrival(n, t):
            landed = halves(others[n][1], c)[t]
            return copy(2 * n + t, landed, landed, (*others[n][0], c))

        def passing(n, t):
            landed = halves(others[n][1], c)[t]
            return copy(6 + 2 * n + t, landed, landed, sibling)

        def from_sibling(n, t):
            landed = halves(others[n][1], 1 - c)[t]
            return copy(6 + 2 * n + t, landed, landed, sibling)

        @pl.when((j == 0) & (i == 0))
        def _():
            first = pltpu.make_async_copy(win_ref, wbuf, local_sems.at[2])
            first.start()
            for cp in own:
                cp.start()
            for t in range(2):
                for n in range(N_CHIP - 1):
                    direct(n, t).start()
            first.wait()

        for n in range(N_CHIP - 1):
            @pl.when((j == n + 1) & (i == 0))
            def _(n=n):
                arrival(n, 0).wait_recv()
                passing(n, 0).start()
                from_sibling(n, 0).wait_recv()
                load = pltpu.make_async_copy(win_full.at[others[n][1]], wbuf, local_sems.at[2])
                load.start()
                load.wait()

        @pl.when((j == N_CHIP - 1) & (i == 0))
        def _():
            for n in range(N_CHIP - 1):
                arrival(n, 1).wait_recv()
                passing(n, 1).start()

        rows = pl.ds(pl.multiple_of(i * tm, tm), tm)

        @pl.when(j == 0)
        def _():
            xt = x_ref[...]
            r = lax.rsqrt(jnp.mean(xt * xt, axis=-1, keepdims=True) + EPS)
            hbuf[rows] = ((xt * r) * g1_ref[...]).astype(BF16)

        out = _dot(hbuf[rows], wbuf[...])
        proj_ref[...] = out
        proj_bf_ref[...] = out.astype(BF16)

        @pl.when((j == N_CHIP - 1) & (i == n_tiles - 1))
        def _():
            for n in range(N_CHIP - 1):
                from_sibling(n, 1).wait_recv()
            for n in range(N_CHIP - 1):
                for t in range(2):
                    direct(n, t).wait_send()
                    passing(n, t).wait_send()
            for cp in own:
                cp.wait()

    last = n_tiles - 1
    grid_spec = pltpu.PrefetchScalarGridSpec(
        num_scalar_prefetch=1,
        grid=(N_CHIP, n_tiles),
        in_specs=[
            pl.BlockSpec((tm, D_MODEL), lambda j, i, o: (jnp.where(j == 0, i, last), 0)),
            VMEM, ANY, ANY],
        out_specs=[pl.BlockSpec((tm, SHARD_IN), lambda j, i, o: (i, o[j])),
                   pl.BlockSpec((tm, SHARD_IN), lambda j, i, o: (i, o[j])), ANY, ANY],
        scratch_shapes=[
            pltpu.VMEM((seq, D_MODEL), BF16), pltpu.VMEM((D_MODEL, SHARD_IN), BF16),
            pltpu.SemaphoreType.DMA((12,)), pltpu.SemaphoreType.DMA((12,)), pltpu.SemaphoreType.DMA((3,))],
    )
    return pl.pallas_call(
        body,
        name="gather_inproj",
        grid_spec=grid_spec,
        out_shape=(
            jax.ShapeDtypeStruct((seq, N_CHIP * SHARD_IN), F32),
            jax.ShapeDtypeStruct((seq, N_CHIP * SHARD_IN), BF16),
            jax.ShapeDtypeStruct((N_CHIP, D_MODEL, SHARD_IN), BF16),
            jax.ShapeDtypeStruct((N_CHIP, SHARD_OUT, D_MODEL), BF16),
        ),
        compiler_params=pltpu.CompilerParams(
            dimension_semantics=("arbitrary", "arbitrary"), vmem_limit_bytes=VMEM_LIMIT),
    )(order, x, g1, win_sh, wout_sh)


def _window_sums(ext, forward):
    n = ext.shape[0]
    sums = []
    acc = ext
    for step in (1, 2, 4, 8):
        acc = acc + pltpu.roll(acc, (n - step) if forward else step, 0)
        sums.append(acc)
    return sums


def _row_counts(tile, rows):
    t = tile * TS + lax.broadcasted_iota(jnp.int32, (rows, GROUP), 0)
    return [jnp.minimum(t + 1, w).astype(F32) for w in POOL_WINDOWS]


def _pool_diffs(pv, prev_rows, tile):
    ext = jnp.concatenate([prev_rows, pv], axis=0)
    sums = _window_sums(ext, forward=False)
    counts = _row_counts(tile, TS)
    out = []
    for g in range(len(POOL_WINDOWS)):
        cols = slice(g * GROUP, (g + 1) * GROUP)
        out.append(sums[g][HALO:, cols] / counts[g] - pv[:, cols])
    return out


def _pool_mix(diffs, pw_ref):
    return jnp.concatenate([_dot(diffs[g], pw_ref[g]) for g in range(len(POOL_WINDOWS))], axis=1)


Q_BLOCK, K_BLOCK, V_BLOCK, AG_BLOCK = 8, 12, 16, 20


def _bias_tile(rel_bias):
    flat = jnp.concatenate(
        [jnp.broadcast_to(rel_bias[:, :1], (rel_bias.shape[0], BAND - CHUNK - 1)), rel_bias[:, :2 * MAX_REL]], axis=1)
    rows = [flat[:, CHUNK - 1 - i:CHUNK - 1 - i + BAND] for i in range(CHUNK)]
    bias = jnp.stack(rows, axis=1)
    first = jnp.pad(bias, ((0, 0), (0, 0), (0, CHUNK)), constant_values=MASK_VALUE)
    second = jnp.pad(bias, ((0, 0), (0, 0), (CHUNK, 0)), constant_values=MASK_VALUE)
    both = jnp.concatenate([first, second], axis=1)
    return both.reshape(N_PAIR, PAIR_LANES, WINDOW).transpose(0, 2, 1)


def _rel_index_tile():
    j = np.arange(WINDOW)[:, None]
    q = np.arange(PAIR_LANES)[None, :] % SUPER
    band_key = j - CHUNK * (q // CHUNK)
    idx = np.clip(band_key - LEFT_CHUNKS * CHUNK - q % CHUNK, -MAX_REL, MAX_REL) + MAX_REL
    return np.where((band_key >= 0) & (band_key < BAND), idx, -1).astype(np.int32)


def _by_head(block):
    low = lax.broadcasted_iota(jnp.int32, block.shape, 1) < HEAD_DIM
    zero = jnp.zeros_like(block)
    return jnp.concatenate([jnp.where(low, block, zero), jnp.where(low, zero, block)], axis=0)


def _own_head_rows(cross):
    head_of_row = lax.broadcasted_iota(jnp.int32, cross.shape, 0) >= HEAD_DIM
    head_of_lane = lax.broadcasted_iota(jnp.int32, cross.shape, 1) >= SUPER
    both = jnp.where(jnp.logical_xor(head_of_row, head_of_lane), 0.0, cross).T
    return both[:SUPER] + both[SUPER:]


def _with_mask_lane(q_rows):
    lane = lax.broadcasted_iota(jnp.int32, q_rows.shape, 1)
    return jnp.concatenate([q_rows, jnp.where(lane == 0, MASK_VALUE, 0.0).astype(q_rows.dtype)], axis=1)


def _band_exp(kb, q_rows, bias):
    s = _dot_t(kb, _with_mask_lane(q_rows)) + bias
    e = jnp.exp(s - jnp.max(s, axis=0, keepdims=True))
    return e, jnp.sum(e, axis=0, keepdims=True)


def _window(sc):
    return slice(WIN_BASE + sc * SUPER, WIN_BASE + sc * SUPER + WINDOW)


def _shift_band(i, band_ref, new_ref):
    @pl.when(i == 0)
    def _():
        band_ref[:TA] = jnp.zeros((TA, GROUP), band_ref.dtype)

    @pl.when(i > 0)
    def _():
        band_ref[:TA] = band_ref[TA:]

    band_ref[TA:] = new_ref[...].astype(band_ref.dtype)


def _shift_key_band(i, band_ref, new_ref):
    @pl.when(i == 0)
    def _():
        lane = lax.broadcasted_iota(jnp.int32, (TA, 2 * GROUP), 1)
        band_ref[:TA] = jnp.where(lane == GROUP, 1.0, 0.0).astype(band_ref.dtype)
        band_ref[TA:, GROUP:] = jnp.zeros((TA, GROUP), band_ref.dtype)

    @pl.when(i > 0)
    def _():
        band_ref[:TA] = band_ref[TA:]

    band_ref[TA:, :GROUP] = new_ref[...].astype(band_ref.dtype)


def _shift_band_t(i, band_ref, new_ref):
    @pl.when(i == 0)
    def _():
        band_ref[:, :TA] = jnp.zeros((GROUP, TA), band_ref.dtype)

    @pl.when(i > 0)
    def _():
        band_ref[:, :TA] = band_ref[:, TA:]

    band_ref[:, TA:] = new_ref[...].astype(band_ref.dtype).T


def _scaled_queries(q_ref, rows):
    return _by_head((q_ref[rows] * SCALE).astype(BF16))


def _attn_fwd(proj, proj_bf, bias_t):
    seq = proj.shape[0]
    n_tiles = seq // TA

    def body(q_ref, k_ref, v_ref, ag_ref, bias_ref, a_ref, ya_ref, e_ref, inv_ref, kband, vband_t):
        i = pl.program_id(1)
        _shift_key_band(i, kband, k_ref)
        _shift_band_t(i, vband_t, v_ref)

        def weights(sc):
            rows = slice(sc * SUPER, (sc + 1) * SUPER)
            win = _window(sc)
            e, total = _band_exp(kband[win], _scaled_queries(q_ref, rows), bias_ref[0])
            e_ref[0, sc] = e.astype(BF16)
            inv_total = 1.0 / total
            inv_ref[0, sc] = jnp.broadcast_to(inv_total, (8, PAIR_LANES))
            return inv_total

        nxt = weights(0)
        for sc in range(SUPERS_PER_TILE):
            rows = slice(sc * SUPER, (sc + 1) * SUPER)
            win = _window(sc)
            inv_total = nxt
            if sc + 1 < SUPERS_PER_TILE:
                nxt = weights(sc + 1)
            a = _own_head_rows(_dot(vband_t[:, win], e_ref[0, sc]) * inv_total)
            a_ref[rows] = a
            g = ag_ref[rows]
            ya_ref[rows] = (a * (g * _sigmoid(g))).astype(BF16)

    blk = pl.BlockSpec((TA, GROUP), lambda p, i: (i, p))

    def cols(first):
        return pl.BlockSpec((TA, GROUP), lambda p, i: (i, first + p))

    return pl.pallas_call(
        body,
        name="attn_fwd",
        grid=(N_PAIR, n_tiles),
        in_specs=[cols(Q_BLOCK), cols(K_BLOCK), cols(V_BLOCK), cols(AG_BLOCK),
                  pl.BlockSpec((1, WINDOW, PAIR_LANES), lambda p, i: (p, 0, 0))],
        out_specs=[blk, blk,
                   pl.BlockSpec((1, SUPERS_PER_TILE, WINDOW, PAIR_LANES), lambda p, i: (p, i, 0, 0)),
                   pl.BlockSpec((1, SUPERS_PER_TILE, 8, PAIR_LANES), lambda p, i: (p, i, 0, 0))],
        out_shape=[jax.ShapeDtypeStruct((seq, ATTN_WIDTH), F32), jax.ShapeDtypeStruct((seq, ATTN_WIDTH), BF16),
                   jax.ShapeDtypeStruct((N_PAIR, seq // SUPER, WINDOW, PAIR_LANES), BF16),
                   jax.ShapeDtypeStruct((N_PAIR, seq // SUPER, 8, PAIR_LANES), F32)],
        scratch_shapes=[pltpu.VMEM((2 * TA, 2 * GROUP), BF16), pltpu.VMEM((GROUP, 2 * TA), BF16)],
        compiler_params=pltpu.CompilerParams(
            dimension_semantics=("arbitrary", "arbitrary"), vmem_limit_bytes=VMEM_LIMIT),
    )(proj_bf, proj_bf, proj_bf, proj, bias_t)


def _attn_bwd(proj, proj_bf, a, dy, e_all, inv_all):
    seq = proj.shape[0]
    n_tiles = seq // TA

    def body(q_ref, k_ref, v_ref, a_ref, ag_ref, dy_ref, e_ref, inv_ref,
             dq_ref, dk_ref, dv_ref, dag_ref, db_ref, vband, kband_t, dkacc, dvacc):
        i = pl.program_id(1)

        @pl.when(i == 0)
        def _():
            dkacc[...] = jnp.zeros_like(dkacc)
            dvacc[...] = jnp.zeros_like(dvacc)
            db_ref[...] = jnp.zeros_like(db_ref)

        @pl.when(i < n_tiles)
        def _():
            _shift_band(i, vband, v_ref)
            _shift_band_t(i, kband_t, k_ref)

            def score_grads(sc):
                rows = slice(sc * SUPER, (sc + 1) * SUPER)
                win = _window(sc)
                q_rows = _scaled_queries(q_ref, rows)
                g = ag_ref[rows]
                sg = _sigmoid(g)
                dyc = dy_ref[rows]
                dag_ref[rows] = dyc * a_ref[rows] * (sg * (1.0 + g * (1.0 - sg)))
                da_rows = _by_head((dyc * (g * sg)).astype(BF16))
                p = e_ref[0, sc].astype(F32) * inv_ref[0, sc, :1]
                dp = _dot_t(vband[win], da_rows)
                ds = p * (dp - jnp.sum(p * dp, axis=0, keepdims=True))
                db_ref[0] += ds
                return q_rows, da_rows, p.astype(BF16), ds.astype(BF16)

            nxt = score_grads(0)
            for sc in range(SUPERS_PER_TILE):
                rows = slice(sc * SUPER, (sc + 1) * SUPER)
                win = _window(sc)
                q_rows, da_rows, p_bf, ds_bf = nxt
                if sc + 1 < SUPERS_PER_TILE:
                    nxt = score_grads(sc + 1)
                dq_ref[rows] = (_own_head_rows(_dot(kband_t[:, win], ds_bf)) * SCALE).astype(BF16)
                dkacc[win] += _dot(ds_bf, q_rows)
                dvacc[win] += _dot(p_bf, da_rows)

        dk_ref[...] = dkacc[:TA].astype(BF16)
        dv_ref[...] = dvacc[:TA].astype(BF16)
        dkacc[:TA] = dkacc[TA:]
        dvacc[:TA] = dvacc[TA:]
        dkacc[TA:] = jnp.zeros((TA, GROUP), F32)
        dvacc[TA:] = jnp.zeros((TA, GROUP), F32)

    last = n_tiles - 1
    cur = pl.BlockSpec((TA, GROUP), lambda p, i: (jnp.minimum(i, last), p))
    older = pl.BlockSpec((TA, GROUP), lambda p, i: (jnp.maximum(i - 1, 0), p))
    dy_blk = pl.BlockSpec((TA, GROUP), lambda p, i: (jnp.minimum(i, last), N_PAIR + p))
    per_pair = pl.BlockSpec((1, WINDOW, PAIR_LANES), lambda p, i: (p, 0, 0))

    def cols(first):
        return pl.BlockSpec((TA, GROUP), lambda p, i: (jnp.minimum(i, last), first + p))

    def kept(rows):
        return pl.BlockSpec((1, SUPERS_PER_TILE, rows, PAIR_LANES), lambda p, i: (p, jnp.minimum(i, last), 0, 0))

    def out(dtype):
        return jax.ShapeDtypeStruct((seq, ATTN_WIDTH), dtype)

    return pl.pallas_call(
        body,
        name="attn_bwd",
        grid=(N_PAIR, n_tiles + 1),
        in_specs=[cols(Q_BLOCK), cols(K_BLOCK), cols(V_BLOCK), cur, cols(AG_BLOCK), dy_blk, kept(WINDOW), kept(8)],
        out_specs=[cur, older, older, cur, per_pair],
        out_shape=[out(BF16), out(BF16), out(BF16), out(F32),
                   jax.ShapeDtypeStruct((N_PAIR, WINDOW, PAIR_LANES), F32)],
        scratch_shapes=[
            pltpu.VMEM((2 * TA, GROUP), BF16), pltpu.VMEM((GROUP, 2 * TA), BF16),
            pltpu.VMEM((2 * TA, GROUP), F32), pltpu.VMEM((2 * TA, GROUP), F32)],
        compiler_params=pltpu.CompilerParams(
            dimension_semantics=("arbitrary", "arbitrary"), vmem_limit_bytes=VMEM_LIMIT),
    )(proj_bf, proj_bf, proj_bf, a, proj, dy, e_all, inv_all)


BIN_ROWS = 136


def _bias_bins(db_t):
    idx_t = jnp.asarray(_rel_index_tile())

    def body(db_ref, idx_ref, out_ref):
        lane = lax.broadcasted_iota(jnp.int32, (1, GROUP), 1)
        row = lax.broadcasted_iota(jnp.int32, (BIN_ROWS, GROUP), 0)
        out = jnp.zeros((BIN_ROWS, GROUP), F32)
        for r in range(N_REL - 1):
            lo = 0 if r == 0 else ((BAND - 2 * CHUNK + r) // 8) * 8
            hi = WINDOW if r == 0 else min(WINDOW, lo + SUPER + 8)
            hit = jnp.where(idx_ref[lo:hi] == r, db_ref[0, lo:hi], 0.0)
            col = jnp.sum(hit, axis=0, keepdims=True)
            s0 = jnp.sum(col[:, :SUPER], axis=1, keepdims=True)
            s1 = jnp.sum(col[:, SUPER:], axis=1, keepdims=True)
            val = jnp.where(lane == 0, s0, jnp.where(lane == 1, s1, 0.0))
            out = jnp.where(row == r, val, out)
        out_ref[0] = out

    return pl.pallas_call(
        body,
        name="bias_bins",
        grid=(N_PAIR,),
        in_specs=[pl.BlockSpec((1, WINDOW, PAIR_LANES), lambda p: (p, 0, 0)), VMEM],
        out_specs=pl.BlockSpec((1, BIN_ROWS, GROUP), lambda p: (p, 0, 0)),
        out_shape=jax.ShapeDtypeStruct((N_PAIR, BIN_ROWS, GROUP), F32),
        compiler_params=pltpu.CompilerParams(dimension_semantics=("arbitrary",)),
    )(db_t, idx_t)


def _out_loss(x, tgt, proj, ya, wout_full, g2, pw_bf, ps):
    seq = x.shape[0]
    n_tiles = seq // TS
    half = TS // 2

    def body(x_ref, t_ref, pv_ref, pg_ref, ya_ref, w_ref, g2_ref, pw_ref, ps_ref,
             dx2_ref, dy_ref, gw_ref, gg_ref, loss_ref, sq_ref, halo_ref):
        i = pl.program_id(0)

        @pl.when(i == 0)
        def _():
            gw_ref[...] = jnp.zeros_like(gw_ref)
            gg_ref[...] = jnp.zeros_like(gg_ref)
            sq_ref[...] = jnp.zeros_like(sq_ref)
            halo_ref[...] = jnp.zeros_like(halo_ref)

        pv = pv_ref[...]
        pg = pg_ref[...]
        diffs = [d.astype(BF16) for d in _pool_diffs(pv, halo_ref[...], i)]
        halo_ref[...] = pv[TS - HALO:, :]
        yp = ((_pool_mix(diffs, pw_ref) * ps_ref[...]) * (pg * _sigmoid(pg))).astype(BF16)
        g2v = g2_ref[...]

        def parts(rows):
            return [yp[rows, :SHARD_OUT], yp[rows, SHARD_OUT:], ya_ref[rows, :SHARD_OUT], ya_ref[rows, SHARD_OUT:]]

        def project(rows, ys):
            x2 = x_ref[rows]
            for b in range(N_CHIP):
                x2 = x2 + _dot(ys[b], w_ref[b])
            return x2

        def norm_loss(rows, x2):
            r = lax.rsqrt(jnp.mean(x2 * x2, axis=-1, keepdims=True) + EPS)
            xh = x2 * r
            diff = xh * g2v - t_ref[rows]
            sq_ref[...] += jnp.sum(diff * diff, axis=0, keepdims=True)
            dfin = diff * (1.0 / D_MODEL)
            gg_ref[...] += jnp.sum(dfin * xh, axis=0, keepdims=True)
            dxh = dfin * g2v
            dx2 = r * (dxh - xh * jnp.mean(dxh * xh, axis=-1, keepdims=True))
            dx2_ref[rows] = dx2
            return dx2.astype(BF16)

        def back(rows, ys, dx2_bf):
            for b in range(N_CHIP):
                gw_ref[b] += _tdot(ys[b], dx2_bf)
                dy_ref[rows, b * SHARD_OUT:(b + 1) * SHARD_OUT] = _dot_t(dx2_bf, w_ref[b])

        top, bottom = slice(0, half), slice(half, TS)
        ys_top, ys_bottom = parts(top), parts(bottom)
        x2_top = project(top, ys_top)
        x2_bottom = project(bottom, ys_bottom)
        d_top = norm_loss(top, x2_top)
        back(top, ys_top, d_top)
        d_bottom = norm_loss(bottom, x2_bottom)
        back(bottom, ys_bottom, d_bottom)

        @pl.when(i == n_tiles - 1)
        def _():
            total = jnp.sum(sq_ref[...], axis=1, keepdims=True) * (0.5 / D_MODEL)
            loss_ref[...] = jnp.broadcast_to(total, loss_ref.shape)

    def rows(width, col=0):
        return pl.BlockSpec((TS, width), lambda i: (i, col))

    return pl.pallas_call(
        body,
        name="out_loss",
        grid=(n_tiles,),
        in_specs=[rows(D_MODEL), rows(D_MODEL), rows(POOL_WIDTH, 0), rows(POOL_WIDTH, 1), rows(ATTN_WIDTH),
                  VMEM, VMEM, VMEM, VMEM],
        out_specs=[rows(D_MODEL), rows(D_MODEL), VMEM, VMEM, VMEM],
        out_shape=[
            jax.ShapeDtypeStruct((seq, D_MODEL), F32), jax.ShapeDtypeStruct((seq, D_MODEL), F32),
            jax.ShapeDtypeStruct((N_CHIP, SHARD_OUT, D_MODEL), F32), jax.ShapeDtypeStruct((1, D_MODEL), F32),
            jax.ShapeDtypeStruct((8, GROUP), F32)],
        scratch_shapes=[pltpu.VMEM((1, D_MODEL), F32), pltpu.VMEM((HALO, POOL_WIDTH), F32)],
        compiler_params=pltpu.CompilerParams(dimension_semantics=("arbitrary",), vmem_limit_bytes=VMEM_LIMIT),
    )(x, tgt, proj, proj, ya, wout_full, g2, pw_bf, ps)


def _inproj_bwd(x, dx2, dy, proj, dq, dk, dv, dag, win_full, pw_bf, g1, ps):
    seq = x.shape[0]
    n_tiles = seq // TS

    def body(x_ref, dx2_ref, dyp_ref, pv_ref, pvprev_ref, pg_ref, dq_ref, dk_ref, dv_ref, dag_ref,
             w_ref, pw_ref, g1_ref, ps_ref, gx_ref, dproj_ref, ht_ref, gg_ref, gps_ref, gpw_ref, halo_ref):
        i = pl.program_id(0)
        tile = n_tiles - 1 - i

        @pl.when(i == 0)
        def _():
            gg_ref[...] = jnp.zeros_like(gg_ref)
            gps_ref[...] = jnp.zeros_like(gps_ref)
            gpw_ref[...] = jnp.zeros_like(gpw_ref)
            halo_ref[...] = jnp.zeros_like(halo_ref)

        pv_t = pv_ref[...]
        pg_t = pg_ref[...]
        prev_rows = jnp.where(tile > 0, pvprev_ref[...], 0.0)
        diffs = [d.astype(BF16) for d in _pool_diffs(pv_t, prev_rows, tile)]
        mixed = _pool_mix(diffs, pw_ref)
        sg = _sigmoid(pg_t)
        silu = pg_t * sg
        dyp = dyp_ref[...]
        psv = ps_ref[...]
        gps_ref[...] += jnp.sum(dyp * mixed * silu, axis=0, keepdims=True)
        dmixed = (dyp * psv * silu).astype(BF16)
        dpg = dyp * (mixed * psv) * (sg * (1.0 + pg_t * (1.0 - sg)))
        counts = _row_counts(tile, TS)
        dds = []
        for g in range(len(POOL_WINDOWS)):
            dm_g = dmixed[:, g * GROUP:(g + 1) * GROUP]
            gpw_ref[g] += _tdot(diffs[g], dm_g)
            dds.append(_dot_t(dm_g, pw_ref[g]))
        dd = jnp.concatenate(dds, axis=1)
        spread = jnp.concatenate([dds[g] / counts[g] for g in range(len(POOL_WINDOWS))], axis=1)
        sums = _window_sums(jnp.concatenate([spread, halo_ref[...]], axis=0), forward=True)
        halo_ref[...] = spread[:HALO]
        dpv = jnp.concatenate(
            [sums[g][:TS, g * GROUP:(g + 1) * GROUP] for g in range(len(POOL_WINDOWS))], axis=1) - dd

        xt = x_ref[...]
        r = lax.rsqrt(jnp.mean(xt * xt, axis=-1, keepdims=True) + EPS)
        xh = xt * r
        g1v = g1_ref[...]
        ht_ref[...] = (xh * g1v).astype(BF16).T
        dproj = jnp.concatenate(
            [dpv.astype(BF16), dpg.astype(BF16), dq_ref[...], dk_ref[...], dv_ref[...], dag_ref[...].astype(BF16)],
            axis=1)
        dproj_ref[...] = dproj
        dh = _dot_t(dproj[:, :SHARD_IN], w_ref[0])
        for chip in range(1, N_CHIP):
            dh = dh + _dot_t(dproj[:, chip * SHARD_IN:(chip + 1) * SHARD_IN], w_ref[chip])

        gg_ref[...] += jnp.sum(dh * xh, axis=0, keepdims=True)
        dxh = dh * g1v
        gx_ref[...] = dx2_ref[...] + r * (dxh - xh * jnp.mean(dxh * xh, axis=-1, keepdims=True))

    def rows(width, col=0):
        return pl.BlockSpec((TS, width), lambda i: (n_tiles - 1 - i, col))

    prev = pl.BlockSpec((HALO, POOL_WIDTH), lambda i: (jnp.maximum((n_tiles - 1 - i) * (TS // HALO) - 1, 0), 0))
    return pl.pallas_call(
        body,
        name="inproj_bwd",
        grid=(n_tiles,),
        in_specs=[rows(D_MODEL), rows(D_MODEL), rows(POOL_WIDTH), rows(POOL_WIDTH), prev, rows(POOL_WIDTH, 1),
                  rows(ATTN_WIDTH), rows(ATTN_WIDTH), rows(ATTN_WIDTH), rows(ATTN_WIDTH), VMEM, VMEM, VMEM, VMEM],
        out_specs=[rows(D_MODEL), rows(N_CHIP * SHARD_IN),
                   pl.BlockSpec((D_MODEL, TS), lambda i: (0, n_tiles - 1 - i)), VMEM, VMEM, VMEM],
        out_shape=[
            jax.ShapeDtypeStruct((seq, D_MODEL), F32),
            jax.ShapeDtypeStruct((seq, N_CHIP * SHARD_IN), BF16),
            jax.ShapeDtypeStruct((D_MODEL, seq), BF16),
            jax.ShapeDtypeStruct((1, D_MODEL), F32),
            jax.ShapeDtypeStruct((1, POOL_WIDTH), F32),
            jax.ShapeDtypeStruct((len(POOL_WINDOWS), GROUP, GROUP), F32)],
        scratch_shapes=[pltpu.VMEM((HALO, POOL_WIDTH), F32)],
        compiler_params=pltpu.CompilerParams(dimension_semantics=("arbitrary",), vmem_limit_bytes=VMEM_LIMIT),
    )(x, dx2, dy, proj, proj, proj, dq, dk, dv, dag, win_full, pw_bf, g1, ps)


PASS_PEER = (2, 0, 1)


def _gw_reduce(ht, dproj, gwout, small):
    seq = ht.shape[1]
    tm = min(2 * TS, seq // 8)
    n_tiles = seq // tm
    half_small = SMALL_ROWS // 2
    cx, cy = lax.axis_index("x"), lax.axis_index("y")
    outer = _other_chips(cx, cy)
    order = jnp.stack([outer[n][1] for n in PASS_PEER] + [2 * cx + cy]).astype(jnp.int32)

    def body(order_ref, ht_ref, dp_ref, gwout_ref, small_ref, gin_final, gout_final, small_final,
             hbuf, acc, pair_in, pair_out, pair_small, tx_in, tx_out, rx_in, rx_out, rx_small,
             gin_out, gout_out, small_out, send_sems, recv_sems):
        j = pl.program_id(0)
        i = pl.program_id(1)
        x, y, c = _my_place()
        b = 2 * x + y
        sibling = (x, y, 1 - c)
        others = _other_chips(x, y)
        mine_in = pl.ds(pl.multiple_of(c * HALF_IN, HALF_IN), HALF_IN)
        mine_out = pl.ds(pl.multiple_of(c * HALF_OUT, HALF_OUT), HALF_OUT)
        mine_small = pl.ds(pl.multiple_of(c * half_small, 8), half_small)
        theirs_in = pl.ds(pl.multiple_of((1 - c) * HALF_IN, HALF_IN), HALF_IN)
        theirs_out = pl.ds(pl.multiple_of((1 - c) * HALF_OUT, HALF_OUT), HALF_OUT)
        theirs_small = pl.ds(pl.multiple_of((1 - c) * half_small, 8), half_small)

        def copy(k, src, dst, to):
            return pltpu.make_async_remote_copy(
                src_ref=src, dst_ref=dst, send_sem=send_sems.at[k], recv_sem=recv_sems.at[k],
                device_id=to, device_id_type=MESH)

        swap_out = copy(0, gwout_ref.at[:, theirs_out], pair_out, sibling)
        swap_small = copy(1, small_ref, pair_small, sibling)

        def swap_in(p):
            return copy(2 + p, acc.at[p % 2, theirs_in], pair_in.at[p % 2], sibling)

        def to_chip(n, t):
            to = (*others[n][0], c)
            if t == 0:
                return copy(6 + 3 * n, tx_in.at[n], rx_in.at[n], to)
            if t == 1:
                return copy(7 + 3 * n, tx_out.at[n], rx_out.at[n], to)
            return copy(8 + 3 * n, pair_small.at[mine_small], rx_small.at[b], to)

        share_in = copy(15, gin_out.at[mine_in], gin_out.at[mine_in], sibling)
        share_out = copy(16, gout_out.at[mine_out], gout_out.at[mine_out], sibling)
        share_small = copy(17, small_out.at[mine_small], small_out.at[mine_small], sibling)

        def at(jj, ii):
            return (j == jj) & (i == ii)

        par = j % 2

        @pl.when(i == 0)
        def _():
            acc[par] = jnp.zeros((D_MODEL, SHARD_IN), F32)

        cols = pl.ds(pl.multiple_of(i * tm, tm), tm)

        @pl.when(j == 0)
        def _():
            hbuf[:, cols] = ht_ref[...]

        acc[par] += _dot(hbuf[:, cols], dp_ref[...])

        @pl.when(at(0, 0))
        def _():
            swap_out.start()
            swap_small.start()

        @pl.when(at(0, 2))
        def _():
            swap_out.wait_recv()
            swap_small.wait_recv()
            for chip in range(N_CHIP):
                pair_out[chip] = gwout_ref[chip, mine_out] + pair_out[chip]
            pair_small[...] = small_ref[...] + pair_small[...]
            rx_small[b] = pair_small[mine_small]
            for n in range(N_CHIP - 1):
                tx_out[n] = pair_out[others[n][1]].astype(BF16)
                to_chip(n, 1).start()
                to_chip(n, 2).start()

        @pl.when(at(1, 4))
        def _():
            total_out = pair_out[b]
            for n in range(N_CHIP - 1):
                to_chip(n, 1).wait_recv()
                copy(8 + 3 * n, pair_small.at[mine_small], rx_small.at[others[n][1]], (*others[n][0], c)).wait_recv()
                total_out = total_out + rx_out[n].astype(F32)
            gout_out[mine_out] = total_out
            small_out[mine_small] = ((rx_small[0] + rx_small[1]) + rx_small[2]) + rx_small[3]
            share_out.start()
            share_small.start()

        for p in range(N_CHIP - 1):
            n = PASS_PEER[p]

            @pl.when(at(p + 1, 0))
            def _(p=p):
                swap_in(p).start()

            @pl.when(at(p + 1, 2))
            def _(p=p, n=n):
                swap_in(p).wait_recv()
                swap_in(p).wait_send()
                tx_in[n] = (acc[p % 2, mine_in] + pair_in[p % 2]).astype(BF16)
                to_chip(n, 0).start()

        @pl.when(at(N_CHIP - 1, n_tiles - 1))
        def _():
            last = N_CHIP - 1
            swap_in(last).start()
            swap_in(last).wait_recv()
            total_in = acc[last % 2, mine_in] + pair_in[last % 2]
            for n in range(N_CHIP - 1):
                to_chip(n, 0).wait_recv()
                total_in = total_in + rx_in[n].astype(F32)
            gin_out[mine_in] = total_in
            share_in.start()
            copy(15, gin_out.at[theirs_in], gin_out.at[theirs_in], sibling).wait_recv()
            copy(16, gout_out.at[theirs_out], gout_out.at[theirs_out], sibling).wait_recv()
            copy(17, small_out.at[theirs_small], small_out.at[theirs_small], sibling).wait_recv()
            swap_out.wait_send()
            swap_small.wait_send()
            swap_in(last).wait_send()
            for n in range(N_CHIP - 1):
                for t in range(3):
                    to_chip(n, t).wait_send()
            share_in.wait_send()
            share_out.wait_send()
            share_small.wait_send()
            gin_final[...] = gin_out[...]
            gout_final[...] = gout_out[...]
            small_final[...] = small_out[...]

    assert n_tiles >= 5, "the reduction's steps are spread over the first five token steps of a pass"
    grid_spec = pltpu.PrefetchScalarGridSpec(
        num_scalar_prefetch=1,
        grid=(N_CHIP, n_tiles),
        in_specs=[
            pl.BlockSpec((D_MODEL, tm), lambda j, i, o: (0, jnp.where(j == 0, i, n_tiles - 1))),
            pl.BlockSpec((tm, SHARD_IN), lambda j, i, o: (i, o[j])),
            VMEM, VMEM],
        out_specs=[VMEM, VMEM, VMEM],
        scratch_shapes=[
            pltpu.VMEM((D_MODEL, seq), BF16),
            pltpu.VMEM((2, D_MODEL, SHARD_IN), F32),
            pltpu.VMEM((2, HALF_IN, SHARD_IN), F32),
            pltpu.VMEM((N_CHIP, HALF_OUT, D_MODEL), F32),
            pltpu.VMEM((SMALL_ROWS, GROUP), F32),
            pltpu.VMEM((N_CHIP - 1, HALF_IN, SHARD_IN), BF16),
            pltpu.VMEM((N_CHIP - 1, HALF_OUT, D_MODEL), BF16),
            pltpu.VMEM((N_CHIP - 1, HALF_IN, SHARD_IN), BF16),
            pltpu.VMEM((N_CHIP - 1, HALF_OUT, D_MODEL), BF16),
            pltpu.VMEM((N_CHIP, half_small, GROUP), F32),
            pltpu.VMEM((D_MODEL, SHARD_IN), F32),
            pltpu.VMEM((SHARD_OUT, D_MODEL), F32),
            pltpu.VMEM((SMALL_ROWS, GROUP), F32),
            pltpu.SemaphoreType.DMA((18,)),
            pltpu.SemaphoreType.DMA((18,)),
        ],
    )
    return pl.pallas_call(
        body,
        name="gw_reduce",
        grid_spec=grid_spec,
        out_shape=(
            jax.ShapeDtypeStruct((D_MODEL, SHARD_IN), F32),
            jax.ShapeDtypeStruct((SHARD_OUT, D_MODEL), F32),
            jax.ShapeDtypeStruct((SMALL_ROWS, GROUP), F32),
        ),
        compiler_params=pltpu.CompilerParams(
            dimension_semantics=("arbitrary", "arbitrary"), vmem_limit_bytes=VMEM_LIMIT),
    )(order, ht, dproj, gwout, small)


def _adamw(name, w, g, m, v, block_rows):
    rows, cols = w.shape

    def body(w_ref, g_ref, m_ref, v_ref, d_ref, m_out, v_out):
        grad = g_ref[...]
        m_new = ADAM_B1 * m_ref[...] + (1.0 - ADAM_B1) * grad
        v_new = ADAM_B2 * v_ref[...] + (1.0 - ADAM_B2) * (grad * grad)
        m_hat = m_new / (1.0 - ADAM_B1 ** ADAM_STEP)
        v_hat = v_new / (1.0 - ADAM_B2 ** ADAM_STEP)
        d_ref[...] = -ADAM_LR * (m_hat / (jnp.sqrt(v_hat) + ADAM_EPS) + ADAM_WD * w_ref[...])
        m_out[...] = m_new
        v_out[...] = v_new

    blk = pl.BlockSpec((block_rows, cols), lambda i: (i, 0))
    shape = jax.ShapeDtypeStruct((rows, cols), F32)
    return pl.pallas_call(
        body,
        name=name,
        grid=(rows // block_rows,),
        in_specs=[blk] * 4,
        out_specs=[blk] * 3,
        out_shape=[shape] * 3,
        compiler_params=pltpu.CompilerParams(dimension_semantics=("arbitrary",)),
    )(w, g, m, v)


def _pack_small(norm_gain, pool_w, pool_scale, rel_bias, final_gain, loss_rows):
    parts = [
        norm_gain.reshape(8, GROUP),
        pool_w.reshape(len(POOL_WINDOWS) * GROUP, GROUP),
        jnp.pad(pool_scale.reshape(4, GROUP), ((0, 4), (0, 0))),
        jnp.pad(rel_bias.reshape(8, N_REL), ((0, 0), (0, 2 * GROUP - N_REL))).reshape(16, GROUP),
        final_gain.reshape(8, GROUP),
        loss_rows,
    ]
    return jnp.concatenate(parts, axis=0)


def _unpack_small(block):
    norm_gain = block[0:8].reshape(1, D_MODEL)
    pool_w = block[8:520].reshape(1, len(POOL_WINDOWS), GROUP, GROUP)
    pool_scale = block[520:524].reshape(1, POOL_WIDTH)
    rel_bias = block[528:544].reshape(8, 2 * GROUP)[:, :N_REL].reshape(1, 8, N_REL)
    final_gain = block[544:552].reshape(D_MODEL)
    return norm_gain, pool_w, pool_scale, rel_bias, final_gain


def kernel(x, norm_gain, w_in, pool_w, pool_scale, rel_bias, w_out, final_norm_gain, loss_target, m_norm_gain, m_w_in, m_pool_w, m_pool_scale, m_rel_bias, m_w_out, m_final_norm_gain, v_norm_gain, v_w_in, v_pool_w, v_pool_scale, v_rel_bias, v_w_out, v_final_norm_gain):
    assert x.shape[1] % TS == 0 and x.shape[2] == D_MODEL
    xs = x[0]
    tgt = loss_target[0]
    g1 = norm_gain.reshape(1, D_MODEL)
    g2 = final_norm_gain.reshape(1, D_MODEL)
    ps = pool_scale.reshape(1, POOL_WIDTH)
    pw_bf = pool_w[0].astype(BF16)

    proj, proj_bf, win_full, wout_full = _gather_inproj(xs, g1, w_in[0].astype(BF16), w_out[0].astype(BF16))
    bias_t = _bias_tile(rel_bias[0])

    a, ya, e_all, inv_all = _attn_fwd(proj, proj_bf, bias_t)
    dx2, dy, gwout, gg2, loss_rows = _out_loss(xs, tgt, proj, ya, wout_full, g2, pw_bf, ps)
    dq, dk, dv, dag, db_t = _attn_bwd(proj, proj_bf, a, dy, e_all, inv_all)
    bins = _bias_bins(db_t)
    gx, dproj, ht, gg1, gps, gpw = _inproj_bwd(xs, dx2, dy, proj, dq, dk, dv, dag, win_full, pw_bf, g1, ps)

    g_bias = bins[:, :N_REL, :2].transpose(0, 2, 1).reshape(8, N_REL)
    small = _pack_small(gg1, gpw, gps, g_bias, gg2, loss_rows)
    g_win, g_wout, g_small = _gw_reduce(ht, dproj, gwout, small)
    loss = g_small[LOSS_ROW, 0]

    zeros8 = jnp.zeros((8, GROUP), F32)
    w_small = _pack_small(norm_gain, pool_w, pool_scale, rel_bias, final_norm_gain, zeros8)
    m_small = _pack_small(m_norm_gain, m_pool_w, m_pool_scale, m_rel_bias, m_final_norm_gain, zeros8)
    v_small = _pack_small(v_norm_gain, v_pool_w, v_pool_scale, v_rel_bias, v_final_norm_gain, zeros8)

    d_win, m_win, v_win = _adamw("adamw_w_in", w_in[0], g_win, m_w_in[0], v_w_in[0], 256)
    d_wout, m_wout, v_wout = _adamw("adamw_w_out", w_out[0], g_wout, m_w_out[0], v_w_out[0], 128)
    d_small, m_new_small, v_new_small = _adamw("adamw_small", w_small, g_small, m_small, v_small, SMALL_ROWS // 2)

    def full(win_part, wout_part, block):
        ng, pw, psc, rb, fg = _unpack_small(block)
        return [ng, win_part[None], pw, psc, rb, wout_part[None], fg]

    grads = full(g_win, g_wout, g_small)
    deltas = full(d_win, d_wout, d_small)
    new_m = full(m_win, m_wout, m_new_small)
    new_v = full(v_win, v_wout, v_new_small)
    return (loss, gx[None], *grads, *deltas, *new_m, *new_v)
```

```python
import numpy as np
import jax
import jax.numpy as jnp
from jax import lax
from jax.experimental import pallas as pl
from jax.experimental.pallas import tpu as pltpu

F32 = jnp.float32
BF16 = jnp.bfloat16

D_MODEL = 1024
POOL_WIDTH = 512
ATTN_WIDTH = 512
POOL_WINDOWS = (2, 4, 8, 16)
GROUP = 128
CHUNK = 64
LEFT_CHUNKS = 8
BAND = (LEFT_CHUNKS + 1) * CHUNK
HEAD_DIM = 64
N_PAIR = 4
MAX_REL = 64
N_REL = 2 * MAX_REL + 1
EPS = 1e-6
MASK_VALUE = -1e30
SCALE = 0.125

ADAM_LR = 0.001
ADAM_B1 = 0.9
ADAM_B2 = 0.999
ADAM_EPS = 1e-08
ADAM_WD = 0.01
ADAM_STEP = 10

TS = LEFT_CHUNKS * CHUNK
SUPER = 2 * CHUNK
WINDOW = BAND + CHUNK
TA = 4 * TS
WIN_BASE = TA - LEFT_CHUNKS * CHUNK
SUPERS_PER_TILE = TA // SUPER
PAIR_LANES = 2 * SUPER
HALO = 16
N_CHIP = 4
SHARD_IN = 768
SHARD_OUT = 256
PIECE = 256
PIECES_PER_SHARD = SHARD_IN // PIECE
HALF_IN = D_MODEL // 2
HALF_OUT = SHARD_OUT // 2
SMALL_ROWS = 560
LOSS_ROW = 552
VMEM_LIMIT = 60 * 1024 * 1024

MESH = pl.DeviceIdType.MESH
ANY = pl.BlockSpec(memory_space=pl.ANY)
VMEM = pl.BlockSpec(memory_space=pltpu.VMEM)


def _sigmoid(x):
    return 1.0 / (1.0 + jnp.exp(-x))


def _dot(a, b):
    return jnp.dot(a, b, preferred_element_type=F32)


def _dot_t(a, b):
    return lax.dot_general(a, b, (((1,), (1,)), ((), ())), preferred_element_type=F32)


def _tdot(a, b):
    return lax.dot_general(a, b, (((0,), (0,)), ((), ())), preferred_element_type=F32)


def _my_place():
    return lax.axis_index("x"), lax.axis_index("y"), lax.axis_index("c")


def _other_chips(x, y):
    places = [(1 - x, y), (x, 1 - y), (1 - x, 1 - y)]
    return [(p, 2 * p[0] + p[1]) for p in places]


def _gather_inproj(x, g1, win_sh, wout_sh):
    seq = x.shape[0]
    tm = min(seq, 4 * TS)
    n_tiles = seq // tm
    cx, cy = lax.axis_index("x"), lax.axis_index("y")
    order = jnp.stack([2 * cx + cy] + [chip for _, chip in _other_chips(cx, cy)]).astype(jnp.int32)

    def body(order_ref, x_ref, g1_ref, win_ref, wout_ref, proj_ref, proj_bf_ref, win_full, wout_full,
             hbuf, wbuf, send_sems, recv_sems, local_sems):
        j = pl.program_id(0)
        i = pl.program_id(1)
        x_, y_, c = _my_place()
        b = 2 * x_ + y_
        sibling = (x_, y_, 1 - c)
        others = _other_chips(x_, y_)

        def halves(chip, core):
            return (
                win_full.at[chip, pl.ds(core * HALF_IN, HALF_IN)],
                wout_full.at[chip, pl.ds(core * HALF_OUT, HALF_OUT)],
            )

        def copy(k, src, dst, to):
            return pltpu.make_async_remote_copy(
                src_ref=src, dst_ref=dst, send_sem=send_sems.at[k], recv_sem=recv_sems.at[k],
                device_id=to, device_id_type=MESH)

        own = [
            pltpu.make_async_copy(win_ref, win_full.at[b], local_sems.at[0]),
            pltpu.make_async_copy(wout_ref, wout_full.at[b], local_sems.at[1]),
        ]
        mine_src = (win_ref.at[pl.ds(c * HALF_IN, HALF_IN)], wout_ref.at[pl.ds(c * HALF_OUT, HALF_OUT)])

        def direct(n, t):
            return copy(2 * n + t, mine_src[t], halves(b, c)[t], (*others[n][0], c))

        def arrival(n, t):
            landed = halves(others[n][1], c)[t]
            return copy(2 * n + t, landed, landed, (*others[n][0], c))

        def passing(n, t):
            landed = halves(others[n][1], c)[t]
            return copy(6 + 2 * n + t, landed, landed, sibling)

        def from_sibling(n, t):
            landed = halves(others[n][1], 1 - c)[t]
            return copy(6 + 2 * n + t, landed, landed, sibling)

        @pl.when((j == 0) & (i == 0))
        def _():
            first = pltpu.make_async_copy(win_ref, wbuf, local_sems.at[2])
            first.start()
            for cp in own:
                cp.start()
            for t in range(2):
                for n in range(N_CHIP - 1):
                    direct(n, t).start()
            first.wait()

        for n in range(N_CHIP - 1):
            @pl.when((j == n + 1) & (i == 0))
            def _(n=n):
                arrival(n, 0).wait_recv()
                passing(n, 0).start()
                from_sibling(n, 0).wait_recv()
                load = pltpu.make_async_copy(win_full.at[others[n][1]], wbuf, local_sems.at[2])
                load.start()
                load.wait()

        @pl.when((j == N_CHIP - 1) & (i == 0))
        def _():
            for n in range(N_CHIP - 1):
                arrival(n, 1).wait_recv()
                passing(n, 1).start()

        rows = pl.ds(pl.multiple_of(i * tm, tm), tm)

        @pl.when(j == 0)
        def _():
            xt = x_ref[...]
            r = lax.rsqrt(jnp.mean(xt * xt, axis=-1, keepdims=True) + EPS)
            hbuf[rows] = ((xt * r) * g1_ref[...]).astype(BF16)

        out = _dot(hbuf[rows], wbuf[...])
        proj_ref[...] = out
        proj_bf_ref[...] = out.astype(BF16)

        @pl.when((j == N_CHIP - 1) & (i == n_tiles - 1))
        def _():
            for n in range(N_CHIP - 1):
                from_sibling(n, 1).wait_recv()
            for n in range(N_CHIP - 1):
                for t in range(2):
                    direct(n, t).wait_send()
                    passing(n, t).wait_send()
            for cp in own:
                cp.wait()

    last = n_tiles - 1
    grid_spec = pltpu.PrefetchScalarGridSpec(
        num_scalar_prefetch=1,
        grid=(N_CHIP, n_tiles),
        in_specs=[
            pl.BlockSpec((tm, D_MODEL), lambda j, i, o: (jnp.where(j == 0, i, last), 0)),
            VMEM, ANY, ANY],
        out_specs=[pl.BlockSpec((tm, SHARD_IN), lambda j, i, o: (i, o[j])),
                   pl.BlockSpec((tm, SHARD_IN), lambda j, i, o: (i, o[j])), ANY, ANY],
        scratch_shapes=[
            pltpu.VMEM((seq, D_MODEL), BF16), pltpu.VMEM((D_MODEL, SHARD_IN), BF16),
            pltpu.SemaphoreType.DMA((12,)), pltpu.SemaphoreType.DMA((12,)), pltpu.SemaphoreType.DMA((3,))],
    )
    return pl.pallas_call(
        body,
        name="gather_inproj",
        grid_spec=grid_spec,
        out_shape=(
            jax.ShapeDtypeStruct((seq, N_CHIP * SHARD_IN), F32),
            jax.ShapeDtypeStruct((seq, N_CHIP * SHARD_IN), BF16),
            jax.ShapeDtypeStruct((N_CHIP, D_MODEL, SHARD_IN), BF16),
            jax.ShapeDtypeStruct((N_CHIP, SHARD_OUT, D_MODEL), BF16),
        ),
        compiler_params=pltpu.CompilerParams(
            dimension_semantics=("arbitrary", "arbitrary"), vmem_limit_bytes=VMEM_LIMIT),
    )(order, x, g1, win_sh, wout_sh)


def _window_sums(ext, forward):
    n = ext.shape[0]
    sums = []
    acc = ext
    for step in (1, 2, 4, 8):
        acc = acc + pltpu.roll(acc, (n - step) if forward else step, 0)
        sums.append(acc)
    return sums


def _row_counts(tile, rows):
    t = tile * TS + lax.broadcasted_iota(jnp.int32, (rows, GROUP), 0)
    return [jnp.minimum(t + 1, w).astype(F32) for w in POOL_WINDOWS]


def _pool_diffs(pv, prev_rows, tile):
    ext = jnp.concatenate([prev_rows, pv], axis=0)
    sums = _window_sums(ext, forward=False)
    counts = _row_counts(tile, TS)
    out = []
    for g in range(len(POOL_WINDOWS)):
        cols = slice(g * GROUP, (g + 1) * GROUP)
        out.append(sums[g][HALO:, cols] / counts[g] - pv[:, cols])
    return out


def _pool_mix(diffs, pw_ref):
    return jnp.concatenate([_dot(diffs[g], pw_ref[g]) for g in range(len(POOL_WINDOWS))], axis=1)


Q_BLOCK, K_BLOCK, V_BLOCK, AG_BLOCK = 8, 12, 16, 20


def _bias_tile(rel_bias):
    flat = jnp.concatenate(
        [jnp.broadcast_to(rel_bias[:, :1], (rel_bias.shape[0], BAND - CHUNK - 1)), rel_bias[:, :2 * MAX_REL]], axis=1)
    rows = [flat[:, CHUNK - 1 - i:CHUNK - 1 - i + BAND] for i in range(CHUNK)]
    bias = jnp.stack(rows, axis=1)
    first = jnp.pad(bias, ((0, 0), (0, 0), (0, CHUNK)), constant_values=MASK_VALUE)
    second = jnp.pad(bias, ((0, 0), (0, 0), (CHUNK, 0)), constant_values=MASK_VALUE)
    both = jnp.concatenate([first, second], axis=1)
    return both.reshape(N_PAIR, PAIR_LANES, WINDOW).transpose(0, 2, 1)


def _rel_index_tile():
    j = np.arange(WINDOW)[:, None]
    q = np.arange(PAIR_LANES)[None, :] % SUPER
    band_key = j - CHUNK * (q // CHUNK)
    idx = np.clip(band_key - LEFT_CHUNKS * CHUNK - q % CHUNK, -MAX_REL, MAX_REL) + MAX_REL
    return np.where((band_key >= 0) & (band_key < BAND), idx, -1).astype(np.int32)


def _by_head(block):
    low = lax.broadcasted_iota(jnp.int32, block.shape, 1) < HEAD_DIM
    zero = jnp.zeros_like(block)
    return jnp.concatenate([jnp.where(low, block, zero), jnp.where(low, zero, block)], axis=0)


def _own_head_rows(cross):
    head_of_row = lax.broadcasted_iota(jnp.int32, cross.shape, 0) >= HEAD_DIM
    head_of_lane = lax.broadcasted_iota(jnp.int32, cross.shape, 1) >= SUPER
    both = jnp.where(jnp.logical_xor(head_of_row, head_of_lane), 0.0, cross).T
    return both[:SUPER] + both[SUPER:]


def _with_mask_lane(q_rows):
    lane = lax.broadcasted_iota(jnp.int32, q_rows.shape, 1)
    return jnp.concatenate([q_rows, jnp.where(lane == 0, MASK_VALUE, 0.0).astype(q_rows.dtype)], axis=1)


def _band_exp(kb, q_rows, bias):
    s = _dot_t(kb, _with_mask_lane(q_rows)) + bias
    e = jnp.exp(s - jnp.max(s, axis=0, keepdims=True))
    return e, jnp.sum(e, axis=0, keepdims=True)


def _window(sc):
    return slice(WIN_BASE + sc * SUPER, WIN_BASE + sc * SUPER + WINDOW)


def _shift_band(i, band_ref, new_ref):
    @pl.when(i == 0)
    def _():
        band_ref[:TA] = jnp.zeros((TA, GROUP), band_ref.dtype)

    @pl.when(i > 0)
    def _():
        band_ref[:TA] = band_ref[TA:]

    band_ref[TA:] = new_ref[...].astype(band_ref.dtype)


def _shift_key_band(i, band_ref, new_ref):
    @pl.when(i == 0)
    def _():
        lane = lax.broadcasted_iota(jnp.int32, (TA, 2 * GROUP), 1)
        band_ref[:TA] = jnp.where(lane == GROUP, 1.0, 0.0).astype(band_ref.dtype)
        band_ref[TA:, GROUP:] = jnp.zeros((TA, GROUP), band_ref.dtype)

    @pl.when(i > 0)
    def _():
        band_ref[:TA] = band_ref[TA:]

    band_ref[TA:, :GROUP] = new_ref[...].astype(band_ref.dtype)


def _shift_band_t(i, band_ref, new_ref):
    @pl.when(i == 0)
    def _():
        band_ref[:, :TA] = jnp.zeros((GROUP, TA), band_ref.dtype)

    @pl.when(i > 0)
    def _():
        band_ref[:, :TA] = band_ref[:, TA:]

    band_ref[:, TA:] = new_ref[...].astype(band_ref.dtype).T


def _scaled_queries(q_ref, rows):
    return _by_head((q_ref[rows] * SCALE).astype(BF16))


def _attn_fwd(proj, proj_bf, bias_t):
    seq = proj.shape[0]
    n_tiles = seq // TA

    def body(q_ref, k_ref, v_ref, ag_ref, bias_ref, a_ref, ya_ref, e_ref, inv_ref, kband, vband_t):
        i = pl.program_id(1)
        _shift_key_band(i, kband, k_ref)
        _shift_band_t(i, vband_t, v_ref)

        def weights(sc):
            rows = slice(sc * SUPER, (sc + 1) * SUPER)
            win = _window(sc)
            e, total = _band_exp(kband[win], _scaled_queries(q_ref, rows), bias_ref[0])
            e_ref[0, sc] = e.astype(BF16)
            inv_total = 1.0 / total
            inv_ref[0, sc] = jnp.broadcast_to(inv_total, (8, PAIR_LANES))
            return inv_total

        nxt = weights(0)
        for sc in range(SUPERS_PER_TILE):
            rows = slice(sc * SUPER, (sc + 1) * SUPER)
            win = _window(sc)
            inv_total = nxt
            if sc + 1 < SUPERS_PER_TILE:
                nxt = weights(sc + 1)
            a = _own_head_rows(_dot(vband_t[:, win], e_ref[0, sc]) * inv_total)
            a_ref[rows] = a
            g = ag_ref[rows]
            ya_ref[rows] = (a * (g * _sigmoid(g))).astype(BF16)

    blk = pl.BlockSpec((TA, GROUP), lambda p, i: (i, p))

    def cols(first):
        return pl.BlockSpec((TA, GROUP), lambda p, i: (i, first + p))

    return pl.pallas_call(
        body,
        name="attn_fwd",
        grid=(N_PAIR, n_tiles),
        in_specs=[cols(Q_BLOCK), cols(K_BLOCK), cols(V_BLOCK), cols(AG_BLOCK),
                  pl.BlockSpec((1, WINDOW, PAIR_LANES), lambda p, i: (p, 0, 0))],
        out_specs=[blk, blk,
                   pl.BlockSpec((1, SUPERS_PER_TILE, WINDOW, PAIR_LANES), lambda p, i: (p, i, 0, 0)),
                   pl.BlockSpec((1, SUPERS_PER_TILE, 8, PAIR_LANES), lambda p, i: (p, i, 0, 0))],
        out_shape=[jax.ShapeDtypeStruct((seq, ATTN_WIDTH), F32), jax.ShapeDtypeStruct((seq, ATTN_WIDTH), BF16),
                   jax.ShapeDtypeStruct((N_PAIR, seq // SUPER, WINDOW, PAIR_LANES), BF16),
                   jax.ShapeDtypeStruct((N_PAIR, seq // SUPER, 8, PAIR_LANES), F32)],
        scratch_shapes=[pltpu.VMEM((2 * TA, 2 * GROUP), BF16), pltpu.VMEM((GROUP, 2 * TA), BF16)],
        compiler_params=pltpu.CompilerParams(
            dimension_semantics=("arbitrary", "arbitrary"), vmem_limit_bytes=VMEM_LIMIT),
    )(proj_bf, proj_bf, proj_bf, proj, bias_t)


def _attn_bwd(proj, proj_bf, a, dy, e_all, inv_all):
    seq = proj.shape[0]
    n_tiles = seq // TA

    def body(q_ref, k_ref, v_ref, a_ref, ag_ref, dy_ref, e_ref, inv_ref,
             dq_ref, dk_ref, dv_ref, dag_ref, db_ref, vband, kband_t, dkacc, dvacc):
        i = pl.program_id(1)

        @pl.when(i == 0)
        def _():
            dkacc[...] = jnp.zeros_like(dkacc)
            dvacc[...] = jnp.zeros_like(dvacc)
            db_ref[...] = jnp.zeros_like(db_ref)

        @pl.when(i < n_tiles)
        def _():
            _shift_band(i, vband, v_ref)
            _shift_band_t(i, kband_t, k_ref)

            def score_grads(sc):
                rows = slice(sc * SUPER, (sc + 1) * SUPER)
                win = _window(sc)
                q_rows = _scaled_queries(q_ref, rows)
                g = ag_ref[rows]
                sg = _sigmoid(g)
                dyc = dy_ref[rows]
                dag_ref[rows] = dyc * a_ref[rows] * (sg * (1.0 + g * (1.0 - sg)))
                da_rows = _by_head((dyc * (g * sg)).astype(BF16))
                p = e_ref[0, sc].astype(F32) * inv_ref[0, sc, :1]
                dp = _dot_t(vband[win], da_rows)
                ds = p * (dp - jnp.sum(p * dp, axis=0, keepdims=True))
                db_ref[0] += ds
                return q_rows, da_rows, p.astype(BF16), ds.astype(BF16)

            nxt = score_grads(0)
            for sc in range(SUPERS_PER_TILE):
                rows = slice(sc * SUPER, (sc + 1) * SUPER)
                win = _window(sc)
                q_rows, da_rows, p_bf, ds_bf = nxt
                if sc + 1 < SUPERS_PER_TILE:
                    nxt = score_grads(sc + 1)
                dq_ref[rows] = (_own_head_rows(_dot(kband_t[:, win], ds_bf)) * SCALE).astype(BF16)
                dkacc[win] += _dot(ds_bf, q_rows)
                dvacc[win] += _dot(p_bf, da_rows)

        dk_ref[...] = dkacc[:TA].astype(BF16)
        dv_ref[...] = dvacc[:TA].astype(BF16)
        dkacc[:TA] = dkacc[TA:]
        dvacc[:TA] = dvacc[TA:]
        dkacc[TA:] = jnp.zeros((TA, GROUP), F32)
        dvacc[TA:] = jnp.zeros((TA, GROUP), F32)

    last = n_tiles - 1
    cur = pl.BlockSpec((TA, GROUP), lambda p, i: (jnp.minimum(i, last), p))
    older = pl.BlockSpec((TA, GROUP), lambda p, i: (jnp.maximum(i - 1, 0), p))
    dy_blk = pl.BlockSpec((TA, GROUP), lambda p, i: (jnp.minimum(i, last), N_PAIR + p))
    per_pair = pl.BlockSpec((1, WINDOW, PAIR_LANES), lambda p, i: (p, 0, 0))

    def cols(first):
        return pl.BlockSpec((TA, GROUP), lambda p, i: (jnp.minimum(i, last), first + p))

    def kept(rows):
        return pl.BlockSpec((1, SUPERS_PER_TILE, rows, PAIR_LANES), lambda p, i: (p, jnp.minimum(i, last), 0, 0))

    def out(dtype):
        return jax.ShapeDtypeStruct((seq, ATTN_WIDTH), dtype)

    return pl.pallas_call(
        body,
        name="attn_bwd",
        grid=(N_PAIR, n_tiles + 1),
        in_specs=[cols(Q_BLOCK), cols(K_BLOCK), cols(V_BLOCK), cur, cols(AG_BLOCK), dy_blk, kept(WINDOW), kept(8)],
        out_specs=[cur, older, older, cur, per_pair],
        out_shape=[out(BF16), out(BF16), out(BF16), out(F32),
                   jax.ShapeDtypeStruct((N_PAIR, WINDOW, PAIR_LANES), F32)],
        scratch_shapes=[
            pltpu.VMEM((2 * TA, GROUP), BF16), pltpu.VMEM((GROUP, 2 * TA), BF16),
            pltpu.VMEM((2 * TA, GROUP), F32), pltpu.VMEM((2 * TA, GROUP), F32)],
        compiler_params=pltpu.CompilerParams(
            dimension_semantics=("arbitrary", "arbitrary"), vmem_limit_bytes=VMEM_LIMIT),
    )(proj_bf, proj_bf, proj_bf, a, proj, dy, e_all, inv_all)


BIN_ROWS = 136


def _bias_bins(db_t):
    idx_t = jnp.asarray(_rel_index_tile())

    def body(db_ref, idx_ref, out_ref):
        lane = lax.broadcasted_iota(jnp.int32, (1, GROUP), 1)
        row = lax.broadcasted_iota(jnp.int32, (BIN_ROWS, GROUP), 0)
        out = jnp.zeros((BIN_ROWS, GROUP), F32)
        for r in range(N_REL - 1):
            lo = 0 if r == 0 else ((BAND - 2 * CHUNK + r) // 8) * 8
            hi = WINDOW if r == 0 else min(WINDOW, lo + SUPER + 8)
            hit = jnp.where(idx_ref[lo:hi] == r, db_ref[0, lo:hi], 0.0)
            col = jnp.sum(hit, axis=0, keepdims=True)
            s0 = jnp.sum(col[:, :SUPER], axis=1, keepdims=True)
            s1 = jnp.sum(col[:, SUPER:], axis=1, keepdims=True)
            val = jnp.where(lane == 0, s0, jnp.where(lane == 1, s1, 0.0))
            out = jnp.where(row == r, val, out)
        out_ref[0] = out

    return pl.pallas_call(
        body,
        name="bias_bins",
        grid=(N_PAIR,),
        in_specs=[pl.BlockSpec((1, WINDOW, PAIR_LANES), lambda p: (p, 0, 0)), VMEM],
        out_specs=pl.BlockSpec((1, BIN_ROWS, GROUP), lambda p: (p, 0, 0)),
        out_shape=jax.ShapeDtypeStruct((N_PAIR, BIN_ROWS, GROUP), F32),
        compiler_params=pltpu.CompilerParams(dimension_semantics=("arbitrary",)),
    )(db_t, idx_t)


def _out_loss(x, tgt, proj, ya, wout_full, g2, pw_bf, ps):
    seq = x.shape[0]
    n_tiles = seq // TS
    half = TS // 2

    def body(x_ref, t_ref, pv_ref, pg_ref, ya_ref, w_ref, g2_ref, pw_ref, ps_ref,
             dx2_ref, dy_ref, gw_ref, gg_ref, loss_ref, sq_ref, halo_ref):
        i = pl.program_id(0)

        @pl.when(i == 0)
        def _():
            gw_ref[...] = jnp.zeros_like(gw_ref)
            gg_ref[...] = jnp.zeros_like(gg_ref)
            sq_ref[...] = jnp.zeros_like(sq_ref)
            halo_ref[...] = jnp.zeros_like(halo_ref)

        pv = pv_ref[...]
        pg = pg_ref[...]
        diffs = [d.astype(BF16) for d in _pool_diffs(pv, halo_ref[...], i)]
        halo_ref[...] = pv[TS - HALO:, :]
        yp = ((_pool_mix(diffs, pw_ref) * ps_ref[...]) * (pg * _sigmoid(pg))).astype(BF16)
        g2v = g2_ref[...]

        def parts(rows):
            return [yp[rows, :SHARD_OUT], yp[rows, SHARD_OUT:], ya_ref[rows, :SHARD_OUT], ya_ref[rows, SHARD_OUT:]]

        def project(rows, ys):
            x2 = x_ref[rows]
            for b in range(N_CHIP):
                x2 = x2 + _dot(ys[b], w_ref[b])
            return x2

        def norm_loss(rows, x2):
            r = lax.rsqrt(jnp.mean(x2 * x2, axis=-1, keepdims=True) + EPS)
            xh = x2 * r
            diff = xh * g2v - t_ref[rows]
            sq_ref[...] += jnp.sum(diff * diff, axis=0, keepdims=True)
            dfin = diff * (1.0 / D_MODEL)
            gg_ref[...] += jnp.sum(dfin * xh, axis=0, keepdims=True)
            dxh = dfin * g2v
            dx2 = r * (dxh - xh * jnp.mean(dxh * xh, axis=-1, keepdims=True))
            dx2_ref[rows] = dx2
            return dx2.astype(BF16)

        def back(rows, ys, dx2_bf):
            for b in range(N_CHIP):
                gw_ref[b] += _tdot(ys[b], dx2_bf)
                dy_ref[rows, b * SHARD_OUT:(b + 1) * SHARD_OUT] = _dot_t(dx2_bf, w_ref[b])

        top, bottom = slice(0, half), slice(half, TS)
        ys_top, ys_bottom = parts(top), parts(bottom)
        x2_top = project(top, ys_top)
        x2_bottom = project(bottom, ys_bottom)
        d_top = norm_loss(top, x2_top)
        back(top, ys_top, d_top)
        d_bottom = norm_loss(bottom, x2_bottom)
        back(bottom, ys_bottom, d_bottom)

        @pl.when(i == n_tiles - 1)
        def _():
            total = jnp.sum(sq_ref[...], axis=1, keepdims=True) * (0.5 / D_MODEL)
            loss_ref[...] = jnp.broadcast_to(total, loss_ref.shape)

    def rows(width, col=0):
        return pl.BlockSpec((TS, width), lambda i: (i, col))

    return pl.pallas_call(
        body,
        name="out_loss",
        grid=(n_tiles,),
        in_specs=[rows(D_MODEL), rows(D_MODEL), rows(POOL_WIDTH, 0), rows(POOL_WIDTH, 1), rows(ATTN_WIDTH),
                  VMEM, VMEM, VMEM, VMEM],
        out_specs=[rows(D_MODEL), rows(D_MODEL), VMEM, VMEM, VMEM],
        out_shape=[
            jax.ShapeDtypeStruct((seq, D_MODEL), F32), jax.ShapeDtypeStruct((seq, D_MODEL), F32),
            jax.ShapeDtypeStruct((N_CHIP, SHARD_OUT, D_MODEL), F32), jax.ShapeDtypeStruct((1, D_MODEL), F32),
            jax.ShapeDtypeStruct((8, GROUP), F32)],
        scratch_shapes=[pltpu.VMEM((1, D_MODEL), F32), pltpu.VMEM((HALO, POOL_WIDTH), F32)],
        compiler_params=pltpu.CompilerParams(dimension_semantics=("arbitrary",), vmem_limit_bytes=VMEM_LIMIT),
    )(x, tgt, proj, proj, ya, wout_full, g2, pw_bf, ps)


def _inproj_bwd(x, dx2, dy, proj, dq, dk, dv, dag, win_full, pw_bf, g1, ps):
    seq = x.shape[0]
    n_tiles = seq // TS

    def body(x_ref, dx2_ref, dyp_ref, pv_ref, pvprev_ref, pg_ref, dq_ref, dk_ref, dv_ref, dag_ref,
             w_ref, pw_ref, g1_ref, ps_ref, gx_ref, dproj_ref, ht_ref, gg_ref, gps_ref, gpw_ref, halo_ref):
        i = pl.program_id(0)
        tile = n_tiles - 1 - i

        @pl.when(i == 0)
        def _():
            gg_ref[...] = jnp.zeros_like(gg_ref)
            gps_ref[...] = jnp.zeros_like(gps_ref)
            gpw_ref[...] = jnp.zeros_like(gpw_ref)
            halo_ref[...] = jnp.zeros_like(halo_ref)

        pv_t = pv_ref[...]
        pg_t = pg_ref[...]
        prev_rows = jnp.where(tile > 0, pvprev_ref[...], 0.0)
        diffs = [d.astype(BF16) for d in _pool_diffs(pv_t, prev_rows, tile)]
        mixed = _pool_mix(diffs, pw_ref)
        sg = _sigmoid(pg_t)
        silu = pg_t * sg
        dyp = dyp_ref[...]
        psv = ps_ref[...]
        gps_ref[...] += jnp.sum(dyp * mixed * silu, axis=0, keepdims=True)
        dmixed = (dyp * psv * silu).astype(BF16)
        dpg = dyp * (mixed * psv) * (sg * (1.0 + pg_t * (1.0 - sg)))
        counts = _row_counts(tile, TS)
        dds = []
        for g in range(len(POOL_WINDOWS)):
            dm_g = dmixed[:, g * GROUP:(g + 1) * GROUP]
            gpw_ref[g] += _tdot(diffs[g], dm_g)
            dds.append(_dot_t(dm_g, pw_ref[g]))
        dd = jnp.concatenate(dds, axis=1)
        spread = jnp.concatenate([dds[g] / counts[g] for g in range(len(POOL_WINDOWS))], axis=1)
        sums = _window_sums(jnp.concatenate([spread, halo_ref[...]], axis=0), forward=True)
        halo_ref[...] = spread[:HALO]
        dpv = jnp.concatenate(
            [sums[g][:TS, g * GROUP:(g + 1) * GROUP] for g in range(len(POOL_WINDOWS))], axis=1) - dd

        xt = x_ref[...]
        r = lax.rsqrt(jnp.mean(xt * xt, axis=-1, keepdims=True) + EPS)
        xh = xt * r
        g1v = g1_ref[...]
        ht_ref[...] = (xh * g1v).astype(BF16).T
        dproj = jnp.concatenate(
            [dpv.astype(BF16), dpg.astype(BF16), dq_ref[...], dk_ref[...], dv_ref[...], dag_ref[...].astype(BF16)],
            axis=1)
        dproj_ref[...] = dproj
        dh = _dot_t(dproj[:, :SHARD_IN], w_ref[0])
        for chip in range(1, N_CHIP):
            dh = dh + _dot_t(dproj[:, chip * SHARD_IN:(chip + 1) * SHARD_IN], w_ref[chip])

        gg_ref[...] += jnp.sum(dh * xh, axis=0, keepdims=True)
        dxh = dh * g1v
        gx_ref[...] = dx2_ref[...] + r * (dxh - xh * jnp.mean(dxh * xh, axis=-1, keepdims=True))

    def rows(width, col=0):
        return pl.BlockSpec((TS, width), lambda i: (n_tiles - 1 - i, col))

    prev = pl.BlockSpec((HALO, POOL_WIDTH), lambda i: (jnp.maximum((n_tiles - 1 - i) * (TS // HALO) - 1, 0), 0))
    return pl.pallas_call(
        body,
        name="inproj_bwd",
        grid=(n_tiles,),
        in_specs=[rows(D_MODEL), rows(D_MODEL), rows(POOL_WIDTH), rows(POOL_WIDTH), prev, rows(POOL_WIDTH, 1),
                  rows(ATTN_WIDTH), rows(ATTN_WIDTH), rows(ATTN_WIDTH), rows(ATTN_WIDTH), VMEM, VMEM, VMEM, VMEM],
        out_specs=[rows(D_MODEL), rows(N_CHIP * SHARD_IN),
                   pl.BlockSpec((D_MODEL, TS), lambda i: (0, n_tiles - 1 - i)), VMEM, VMEM, VMEM],
        out_shape=[
            jax.ShapeDtypeStruct((seq, D_MODEL), F32),
            jax.ShapeDtypeStruct((seq, N_CHIP * SHARD_IN), BF16),
            jax.ShapeDtypeStruct((D_MODEL, seq), BF16),
            jax.ShapeDtypeStruct((1, D_MODEL), F32),
            jax.ShapeDtypeStruct((1, POOL_WIDTH), F32),
            jax.ShapeDtypeStruct((len(POOL_WINDOWS), GROUP, GROUP), F32)],
        scratch_shapes=[pltpu.VMEM((HALO, POOL_WIDTH), F32)],
        compiler_params=pltpu.CompilerParams(dimension_semantics=("arbitrary",), vmem_limit_bytes=VMEM_LIMIT),
    )(x, dx2, dy, proj, proj, proj, dq, dk, dv, dag, win_full, pw_bf, g1, ps)


PASS_PEER = (2, 0, 1)


def _gw_reduce(ht, dproj, gwout, small):
    seq = ht.shape[1]
    tm = min(2 * TS, seq // 8)
    n_tiles = seq // tm
    half_small = SMALL_ROWS // 2
    cx, cy = lax.axis_index("x"), lax.axis_index("y")
    outer = _other_chips(cx, cy)
    order = jnp.stack([outer[n][1] for n in PASS_PEER] + [2 * cx + cy]).astype(jnp.int32)

    def body(order_ref, ht_ref, dp_ref, gwout_ref, small_ref, gin_final, gout_final, small_final,
             hbuf, acc, pair_in, pair_out, pair_small, tx_in, tx_out, rx_in, rx_out, rx_small,
             gin_out, gout_out, small_out, send_sems, recv_sems):
        j = pl.program_id(0)
        i = pl.program_id(1)
        x, y, c = _my_place()
        b = 2 * x + y
        sibling = (x, y, 1 - c)
        others = _other_chips(x, y)
        mine_in = pl.ds(pl.multiple_of(c * HALF_IN, HALF_IN), HALF_IN)
        mine_out = pl.ds(pl.multiple_of(c * HALF_OUT, HALF_OUT), HALF_OUT)
        mine_small = pl.ds(pl.multiple_of(c * half_small, 8), half_small)
        theirs_in = pl.ds(pl.multiple_of((1 - c) * HALF_IN, HALF_IN), HALF_IN)
        theirs_out = pl.ds(pl.multiple_of((1 - c) * HALF_OUT, HALF_OUT), HALF_OUT)
        theirs_small = pl.ds(pl.multiple_of((1 - c) * half_small, 8), half_small)

        def copy(k, src, dst, to):
            return pltpu.make_async_remote_copy(
                src_ref=src, dst_ref=dst, send_sem=send_sems.at[k], recv_sem=recv_sems.at[k],
                device_id=to, device_id_type=MESH)

        swap_out = copy(0, gwout_ref.at[:, theirs_out], pair_out, sibling)
        swap_small = copy(1, small_ref, pair_small, sibling)

        def swap_in(p):
            return copy(2 + p, acc.at[p % 2, theirs_in], pair_in.at[p % 2], sibling)

        def to_chip(n, t):
            to = (*others[n][0], c)
            if t == 0:
                return copy(6 + 3 * n, tx_in.at[n], rx_in.at[n], to)
            if t == 1:
                return copy(7 + 3 * n, tx_out.at[n], rx_out.at[n], to)
            return copy(8 + 3 * n, pair_small.at[mine_small], rx_small.at[b], to)

        share_in = copy(15, gin_out.at[mine_in], gin_out.at[mine_in], sibling)
        share_out = copy(16, gout_out.at[mine_out], gout_out.at[mine_out], sibling)
        share_small = copy(17, small_out.at[mine_small], small_out.at[mine_small], sibling)

        def at(jj, ii):
            return (j == jj) & (i == ii)

        par = j % 2

        @pl.when(i == 0)
        def _():
            acc[par] = jnp.zeros((D_MODEL, SHARD_IN), F32)

        cols = pl.ds(pl.multiple_of(i * tm, tm), tm)

        @pl.when(j == 0)
        def _():
            hbuf[:, cols] = ht_ref[...]

        acc[par] += _dot(hbuf[:, cols], dp_ref[...])

        @pl.when(at(0, 0))
        def _():
            swap_out.start()
            swap_small.start()

        @pl.when(at(0, 2))
        def _():
            swap_out.wait_recv()
            swap_small.wait_recv()
            for chip in range(N_CHIP):
                pair_out[chip] = gwout_ref[chip, mine_out] + pair_out[chip]
            pair_small[...] = small_ref[...] + pair_small[...]
            rx_small[b] = pair_small[mine_small]
            for n in range(N_CHIP - 1):
                tx_out[n] = pair_out[others[n][1]].astype(BF16)
                to_chip(n, 1).start()
                to_chip(n, 2).start()

        @pl.when(at(1, 4))
        def _():
            total_out = pair_out[b]
            for n in range(N_CHIP - 1):
                to_chip(n, 1).wait_recv()
                copy(8 + 3 * n, pair_small.at[mine_small], rx_small.at[others[n][1]], (*others[n][0], c)).wait_recv()
                total_out = total_out + rx_out[n].astype(F32)
            gout_out[mine_out] = total_out
            small_out[mine_small] = ((rx_small[0] + rx_small[1]) + rx_small[2]) + rx_small[3]
            share_out.start()
            share_small.start()

        for p in range(N_CHIP - 1):
            n = PASS_PEER[p]

            @pl.when(at(p + 1, 0))
            def _(p=p):
                swap_in(p).start()

            @pl.when(at(p + 1, 2))
            def _(p=p, n=n):
                swap_in(p).wait_recv()
                swap_in(p).wait_send()
                tx_in[n] = (acc[p % 2, mine_in] + pair_in[p % 2]).astype(BF16)
                to_chip(n, 0).start()

        @pl.when(at(N_CHIP - 1, n_tiles - 1))
        def _():
            last = N_CHIP - 1
            swap_in(last).start()
            swap_in(last).wait_recv()
            total_in = acc[last % 2, mine_in] + pair_in[last % 2]
            for n in range(N_CHIP - 1):
                to_chip(n, 0).wait_recv()
                total_in = total_in + rx_in[n].astype(F32)
            gin_out[mine_in] = total_in
            share_in.start()
            copy(15, gin_out.at[theirs_in], gin_out.at[theirs_in], sibling).wait_recv()
            copy(16, gout_out.at[theirs_out], gout_out.at[theirs_out], sibling).wait_recv()
            copy(17, small_out.at[theirs_small], small_out.at[theirs_small], sibling).wait_recv()
            swap_out.wait_send()
            swap_small.wait_send()
            swap_in(last).wait_send()
            for n in range(N_CHIP - 1):
                for t in range(3):
                    to_chip(n, t).wait_send()
            share_in.wait_send()
            share_out.wait_send()
            share_small.wait_send()
            gin_final[...] = gin_out[...]
            gout_final[...] = gout_out[...]
            small_final[...] = small_out[...]

    assert n_tiles >= 5, "the reduction's steps are spread over the first five token steps of a pass"
    grid_spec = pltpu.PrefetchScalarGridSpec(
        num_scalar_prefetch=1,
        grid=(N_CHIP, n_tiles),
        in_specs=[
            pl.BlockSpec((D_MODEL, tm), lambda j, i, o: (0, jnp.where(j == 0, i, n_tiles - 1))),
            pl.BlockSpec((tm, SHARD_IN), lambda j, i, o: (i, o[j])),
            VMEM, VMEM],
        out_specs=[VMEM, VMEM, VMEM],
        scratch_shapes=[
            pltpu.VMEM((D_MODEL, seq), BF16),
            pltpu.VMEM((2, D_MODEL, SHARD_IN), F32),
            pltpu.VMEM((2, HALF_IN, SHARD_IN), F32),
            pltpu.VMEM((N_CHIP, HALF_OUT, D_MODEL), F32),
            pltpu.VMEM((SMALL_ROWS, GROUP), F32),
            pltpu.VMEM((N_CHIP - 1, HALF_IN, SHARD_IN), BF16),
            pltpu.VMEM((N_CHIP - 1, HALF_OUT, D_MODEL), BF16),
            pltpu.VMEM((N_CHIP - 1, HALF_IN, SHARD_IN), BF16),
            pltpu.VMEM((N_CHIP - 1, HALF_OUT, D_MODEL), BF16),
            pltpu.VMEM((N_CHIP, half_small, GROUP), F32),
            pltpu.VMEM((D_MODEL, SHARD_IN), F32),
            pltpu.VMEM((SHARD_OUT, D_MODEL), F32),
            pltpu.VMEM((SMALL_ROWS, GROUP), F32),
            pltpu.SemaphoreType.DMA((18,)),
            pltpu.SemaphoreType.DMA((18,)),
        ],
    )
    return pl.pallas_call(
        body,
        name="gw_reduce",
        grid_spec=grid_spec,
        out_shape=(
            jax.ShapeDtypeStruct((D_MODEL, SHARD_IN), F32),
            jax.ShapeDtypeStruct((SHARD_OUT, D_MODEL), F32),
            jax.ShapeDtypeStruct((SMALL_ROWS, GROUP), F32),
        ),
        compiler_params=pltpu.CompilerParams(
            dimension_semantics=("arbitrary", "arbitrary"), vmem_limit_bytes=VMEM_LIMIT),
    )(order, ht, dproj, gwout, small)


def _adamw(name, w, g, m, v, block_rows):
    rows, cols = w.shape

    def body(w_ref, g_ref, m_ref, v_ref, d_ref, m_out, v_out):
        grad = g_ref[...]
        m_new = ADAM_B1 * m_ref[...] + (1.0 - ADAM_B1) * grad
        v_new = ADAM_B2 * v_ref[...] + (1.0 - ADAM_B2) * (grad * grad)
        m_hat = m_new / (1.0 - ADAM_B1 ** ADAM_STEP)
        v_hat = v_new / (1.0 - ADAM_B2 ** ADAM_STEP)
        d_ref[...] = -ADAM_LR * (m_hat / (jnp.sqrt(v_hat) + ADAM_EPS) + ADAM_WD * w_ref[...])
        m_out[...] = m_new
        v_out[...] = v_new

    blk = pl.BlockSpec((block_rows, cols), lambda i: (i, 0))
    shape = jax.ShapeDtypeStruct((rows, cols), F32)
    return pl.pallas_call(
        body,
        name=name,
        grid=(rows // block_rows,),
        in_specs=[blk] * 4,
        out_specs=[blk] * 3,
        out_shape=[shape] * 3,
        compiler_params=pltpu.CompilerParams(dimension_semantics=("arbitrary",)),
    )(w, g, m, v)


def _pack_small(norm_gain, pool_w, pool_scale, rel_bias, final_gain, loss_rows):
    parts = [
        norm_gain.reshape(8, GROUP),
        pool_w.reshape(len(POOL_WINDOWS) * GROUP, GROUP),
        jnp.pad(pool_scale.reshape(4, GROUP), ((0, 4), (0, 0))),
        jnp.pad(rel_bias.reshape(8, N_REL), ((0, 0), (0, 2 * GROUP - N_REL))).reshape(16, GROUP),
        final_gain.reshape(8, GROUP),
        loss_rows,
    ]
    return jnp.concatenate(parts, axis=0)


def _unpack_small(block):
    norm_gain = block[0:8].reshape(1, D_MODEL)
    pool_w = block[8:520].reshape(1, len(POOL_WINDOWS), GROUP, GROUP)
    pool_scale = block[520:524].reshape(1, POOL_WIDTH)
    rel_bias = block[528:544].reshape(8, 2 * GROUP)[:, :N_REL].reshape(1, 8, N_REL)
    final_gain = block[544:552].reshape(D_MODEL)
    return norm_gain, pool_w, pool_scale, rel_bias, final_gain


def kernel(x, norm_gain, w_in, pool_w, pool_scale, rel_bias, w_out, final_norm_gain, loss_target, m_norm_gain, m_w_in, m_pool_w, m_pool_scale, m_rel_bias, m_w_out, m_final_norm_gain, v_norm_gain, v_w_in, v_pool_w, v_pool_scale, v_rel_bias, v_w_out, v_final_norm_gain):
    assert x.shape[1] % TS == 0 and x.shape[2] == D_MODEL
    xs = x[0]
    tgt = loss_target[0]
    g1 = norm_gain.reshape(1, D_MODEL)
    g2 = final_norm_gain.reshape(1, D_MODEL)
    ps = pool_scale.reshape(1, POOL_WIDTH)
    pw_bf = pool_w[0].astype(BF16)

    proj, proj_bf, win_full, wout_full = _gather_inproj(xs, g1, w_in[0].astype(BF16), w_out[0].astype(BF16))
    bias_t = _bias_tile(rel_bias[0])

    a, ya, e_all, inv_all = _attn_fwd(proj, proj_bf, bias_t)
    dx2, dy, gwout, gg2, loss_rows = _out_loss(xs, tgt, proj, ya, wout_full, g2, pw_bf, ps)
    dq, dk, dv, dag, db_t = _attn_bwd(proj, proj_bf, a, dy, e_all, inv_all)
    bins = _bias_bins(db_t)
    gx, dproj, ht, gg1, gps, gpw = _inproj_bwd(xs, dx2, dy, proj, dq, dk, dv, dag, win_full, pw_bf, g1, ps)

    g_bias = bins[:, :N_REL, :2].transpose(0, 2, 1).reshape(8, N_REL)
    small = _pack_small(gg1, gpw, gps, g_bias, gg2, loss_rows)
    g_win, g_wout, g_small = _gw_reduce(ht, dproj, gwout, small)
    loss = g_small[LOSS_ROW, 0]

    zeros8 = jnp.zeros((8, GROUP), F32)
    w_small = _pack_small(norm_gain, pool_w, pool_scale, rel_bias, final_norm_gain, zeros8)
    m_small = _pack_small(m_norm_gain, m_pool_w, m_pool_scale, m_rel_bias, m_final_norm_gain, zeros8)
    v_small = _pack_small(v_norm_gain, v_pool_w, v_pool_scale, v_rel_bias, v_final_norm_gain, zeros8)

    d_win, m_win, v_win = _adamw("adamw_w_in", w_in[0], g_win, m_w_in[0], v_w_in[0], 256)
    d_wout, m_wout, v_wout = _adamw("adamw_w_out", w_out[0], g_wout, m_w_out[0], v_w_out[0], 128)
    d_small, m_new_small, v_new_small = _adamw("adamw_small", w_small, g_small, m_small, v_small, SMALL_ROWS // 2)

    def full(win_part, wout_part, block):
        ng, pw, psc, rb, fg = _unpack_small(block)
        return [ng, win_part[None], pw, psc, rb, wout_part[None], fg]

    grads = full(g_win, g_wout, g_small)
    deltas = full(d_win, d_wout, d_small)
    new_m = full(m_win, m_wout, m_new_small)
    new_v = full(v_win, v_wout, v_new_small)
    return (loss, gx[None], *grads, *deltas, *new_m, *new_v)
```

```python
import numpy as np
import jax
import jax.numpy as jnp
from jax import lax
from jax.experimental import pallas as pl
from jax.experimental.pallas import tpu as pltpu

F32 = jnp.float32
BF16 = jnp.bfloat16

D_MODEL = 1024
POOL_WIDTH = 512
ATTN_WIDTH = 512
POOL_WINDOWS = (2, 4, 8, 16)
GROUP = 128
CHUNK = 64
LEFT_CHUNKS = 8
BAND = (LEFT_CHUNKS + 1) * CHUNK
HEAD_DIM = 64
N_PAIR = 4
MAX_REL = 64
N_REL = 2 * MAX_REL + 1
EPS = 1e-6
MASK_VALUE = -1e30
SCALE = 0.125

ADAM_LR = 0.001
ADAM_B1 = 0.9
ADAM_B2 = 0.999
ADAM_EPS = 1e-08
ADAM_WD = 0.01
ADAM_STEP = 10

TS = LEFT_CHUNKS * CHUNK
SUPER = 2 * CHUNK
WINDOW = BAND + CHUNK
TA = 4 * TS
WIN_BASE = TA - LEFT_CHUNKS * CHUNK
SUPERS_PER_TILE = TA // SUPER
PAIR_LANES = 2 * SUPER
HALO = 16
N_CHIP = 4
SHARD_IN = 768
SHARD_OUT = 256
PIECE = 256
PIECES_PER_SHARD = SHARD_IN // PIECE
HALF_IN = D_MODEL // 2
HALF_OUT = SHARD_OUT // 2
SMALL_ROWS = 560
LOSS_ROW = 552
VMEM_LIMIT = 60 * 1024 * 1024

MESH = pl.DeviceIdType.MESH
ANY = pl.BlockSpec(memory_space=pl.ANY)
VMEM = pl.BlockSpec(memory_space=pltpu.VMEM)


def _sigmoid(x):
    return 1.0 / (1.0 + jnp.exp(-x))


def _dot(a, b):
    return jnp.dot(a, b, preferred_element_type=F32)


def _dot_t(a, b):
    return lax.dot_general(a, b, (((1,), (1,)), ((), ())), preferred_element_type=F32)


def _tdot(a, b):
    return lax.dot_general(a, b, (((0,), (0,)), ((), ())), preferred_element_type=F32)


def _my_place():
    return lax.axis_index("x"), lax.axis_index("y"), lax.axis_index("c")


def _other_chips(x, y):
    places = [(1 - x, y), (x, 1 - y), (1 - x, 1 - y)]
    return [(p, 2 * p[0] + p[1]) for p in places]


def _gather_inproj(x, g1, win_sh, wout_sh):
    seq = x.shape[0]
    tm = min(seq, 4 * TS)
    n_tiles = seq // tm
    cx, cy = lax.axis_index("x"), lax.axis_index("y")
    order = jnp.stack([2 * cx + cy] + [chip for _, chip in _other_chips(cx, cy)]).astype(jnp.int32)

    def body(order_ref, x_ref, g1_ref, win_ref, wout_ref, proj_ref, proj_bf_ref, win_full, wout_full,
             hbuf, wbuf, send_sems, recv_sems, local_sems):
        j = pl.program_id(0)
        i = pl.program_id(1)
        x_, y_, c = _my_place()
        b = 2 * x_ + y_
        sibling = (x_, y_, 1 - c)
        others = _other_chips(x_, y_)

        def halves(chip, core):
            return (
                win_full.at[chip, pl.ds(core * HALF_IN, HALF_IN)],
                wout_full.at[chip, pl.ds(core * HALF_OUT, HALF_OUT)],
            )

        def copy(k, src, dst, to):
            return pltpu.make_async_remote_copy(
                src_ref=src, dst_ref=dst, send_sem=send_sems.at[k], recv_sem=recv_sems.at[k],
                device_id=to, device_id_type=MESH)

        own = [
            pltpu.make_async_copy(win_ref, win_full.at[b], local_sems.at[0]),
            pltpu.make_async_copy(wout_ref, wout_full.at[b], local_sems.at[1]),
        ]
        mine_src = (win_ref.at[pl.ds(c * HALF_IN, HALF_IN)], wout_ref.at[pl.ds(c * HALF_OUT, HALF_OUT)])

        def direct(n, t):
            return copy(2 * n + t, mine_src[t], halves(b, c)[t], (*others[n][0], c))

        def arrival(n, t):
            landed = halves(others[n][1], c)[t]
            return copy(2 * n + t, landed, landed, (*others[n][0], c))

        def passing(n, t):
            landed = halves(others[n][1], c)[t]
            return copy(6 + 2 * n + t, landed, landed, sibling)

        def from_sibling(n, t):
            landed = halves(others[n][1], 1 - c)[t]
            return copy(6 + 2 * n + t, landed, landed, sibling)

        @pl.when((j == 0) & (i == 0))
        def _():
            first = pltpu.make_async_copy(win_ref, wbuf, local_sems.at[2])
            first.start()
            for cp in own:
                cp.start()
            for t in range(2):
                for n in range(N_CHIP - 1):
                    direct(n, t).start()
            first.wait()

        for n in range(N_CHIP - 1):
            @pl.when((j == n + 1) & (i == 0))
            def _(n=n):
                arrival(n, 0).wait_recv()
                passing(n, 0).start()
                from_sibling(n, 0).wait_recv()
                load = pltpu.make_async_copy(win_full.at[others[n][1]], wbuf, local_sems.at[2])
                load.start()
                load.wait()

        rows = pl.ds(pl.multiple_of(i * tm, tm), tm)

        @pl.when(j == 0)
        def _():
            xt = x_ref[...]
            r = lax.rsqrt(jnp.mean(xt * xt, axis=-1, keepdims=True) + EPS)
            hbuf[rows] = ((xt * r) * g1_ref[...]).astype(BF16)

        out = _dot(hbuf[rows], wbuf[...])
        proj_ref[...] = out
        proj_bf_ref[...] = out.astype(BF16)

        @pl.when((j == N_CHIP - 1) & (i == n_tiles - 1))
        def _():
            for n in range(N_CHIP - 1):
                arrival(n, 1).wait_recv()
                passing(n, 1).start()
            for n in range(N_CHIP - 1):
                from_sibling(n, 1).wait_recv()
            for n in range(N_CHIP - 1):
                for t in range(2):
                    direct(n, t).wait_send()
                    passing(n, t).wait_send()
            for cp in own:
                cp.wait()

    last = n_tiles - 1
    grid_spec = pltpu.PrefetchScalarGridSpec(
        num_scalar_prefetch=1,
        grid=(N_CHIP, n_tiles),
        in_specs=[
            pl.BlockSpec((tm, D_MODEL), lambda j, i, o: (jnp.where(j == 0, i, last), 0)),
            VMEM, ANY, ANY],
        out_specs=[pl.BlockSpec((tm, SHARD_IN), lambda j, i, o: (i, o[j])),
                   pl.BlockSpec((tm, SHARD_IN), lambda j, i, o: (i, o[j])), ANY, ANY],
        scratch_shapes=[
            pltpu.VMEM((seq, D_MODEL), BF16), pltpu.VMEM((D_MODEL, SHARD_IN), BF16),
            pltpu.SemaphoreType.DMA((12,)), pltpu.SemaphoreType.DMA((12,)), pltpu.SemaphoreType.DMA((3,))],
    )
    return pl.pallas_call(
        body,
        name="gather_inproj",
        grid_spec=grid_spec,
        out_shape=(
            jax.ShapeDtypeStruct((seq, N_CHIP * SHARD_IN), F32),
            jax.ShapeDtypeStruct((seq, N_CHIP * SHARD_IN), BF16),
            jax.ShapeDtypeStruct((N_CHIP, D_MODEL, SHARD_IN), BF16),
            jax.ShapeDtypeStruct((N_CHIP, SHARD_OUT, D_MODEL), BF16),
        ),
        compiler_params=pltpu.CompilerParams(
            dimension_semantics=("arbitrary", "arbitrary"), vmem_limit_bytes=VMEM_LIMIT),
    )(order, x, g1, win_sh, wout_sh)


def _window_sums(ext, forward):
    n = ext.shape[0]
    sums = []
    acc = ext
    for step in (1, 2, 4, 8):
        acc = acc + pltpu.roll(acc, (n - step) if forward else step, 0)
        sums.append(acc)
    return sums


def _row_counts(tile, rows):
    t = tile * TS + lax.broadcasted_iota(jnp.int32, (rows, GROUP), 0)
    return [jnp.minimum(t + 1, w).astype(F32) for w in POOL_WINDOWS]


def _pool_diffs(pv, prev_rows, tile):
    ext = jnp.concatenate([prev_rows, pv], axis=0)
    sums = _window_sums(ext, forward=False)
    counts = _row_counts(tile, TS)
    out = []
    for g in range(len(POOL_WINDOWS)):
        cols = slice(g * GROUP, (g + 1) * GROUP)
        out.append(sums[g][HALO:, cols] / counts[g] - pv[:, cols])
    return out


def _pool_mix(diffs, pw_ref):
    return jnp.concatenate([_dot(diffs[g], pw_ref[g]) for g in range(len(POOL_WINDOWS))], axis=1)


Q_BLOCK, K_BLOCK, V_BLOCK, AG_BLOCK = 8, 12, 16, 20


def _bias_tile(rel_bias):
    flat = jnp.concatenate(
        [jnp.broadcast_to(rel_bias[:, :1], (rel_bias.shape[0], BAND - CHUNK - 1)), rel_bias[:, :2 * MAX_REL]], axis=1)
    rows = [flat[:, CHUNK - 1 - i:CHUNK - 1 - i + BAND] for i in range(CHUNK)]
    bias = jnp.stack(rows, axis=1)
    first = jnp.pad(bias, ((0, 0), (0, 0), (0, CHUNK)), constant_values=MASK_VALUE)
    second = jnp.pad(bias, ((0, 0), (0, 0), (CHUNK, 0)), constant_values=MASK_VALUE)
    both = jnp.concatenate([first, second], axis=1)
    return both.reshape(N_PAIR, PAIR_LANES, WINDOW).transpose(0, 2, 1)


def _rel_index_tile():
    j = np.arange(WINDOW)[:, None]
    q = np.arange(PAIR_LANES)[None, :] % SUPER
    band_key = j - CHUNK * (q // CHUNK)
    idx = np.clip(band_key - LEFT_CHUNKS * CHUNK - q % CHUNK, -MAX_REL, MAX_REL) + MAX_REL
    return np.where((band_key >= 0) & (band_key < BAND), idx, -1).astype(np.int32)


def _by_head(block):
    low = lax.broadcasted_iota(jnp.int32, block.shape, 1) < HEAD_DIM
    zero = jnp.zeros_like(block)
    return jnp.concatenate([jnp.where(low, block, zero), jnp.where(low, zero, block)], axis=0)


def _own_head_rows(cross):
    head_of_row = lax.broadcasted_iota(jnp.int32, cross.shape, 0) >= HEAD_DIM
    head_of_lane = lax.broadcasted_iota(jnp.int32, cross.shape, 1) >= SUPER
    both = jnp.where(jnp.logical_xor(head_of_row, head_of_lane), 0.0, cross).T
    return both[:SUPER] + both[SUPER:]


def _with_mask_lane(q_rows):
    lane = lax.broadcasted_iota(jnp.int32, q_rows.shape, 1)
    return jnp.concatenate([q_rows, jnp.where(lane == 0, MASK_VALUE, 0.0).astype(q_rows.dtype)], axis=1)


def _band_exp(kb, q_rows, bias):
    s = _dot_t(kb, _with_mask_lane(q_rows)) + bias
    e = jnp.exp(s - jnp.max(s, axis=0, keepdims=True))
    return e, jnp.sum(e, axis=0, keepdims=True)


def _window(sc):
    return slice(WIN_BASE + sc * SUPER, WIN_BASE + sc * SUPER + WINDOW)


def _shift_band(i, band_ref, new_ref):
    @pl.when(i == 0)
    def _():
        band_ref[:TA] = jnp.zeros((TA, GROUP), band_ref.dtype)

    @pl.when(i > 0)
    def _():
        band_ref[:TA] = band_ref[TA:]

    band_ref[TA:] = new_ref[...].astype(band_ref.dtype)


def _shift_key_band(i, band_ref, new_ref):
    @pl.when(i == 0)
    def _():
        lane = lax.broadcasted_iota(jnp.int32, (TA, 2 * GROUP), 1)
        band_ref[:TA] = jnp.where(lane == GROUP, 1.0, 0.0).astype(band_ref.dtype)
        band_ref[TA:, GROUP:] = jnp.zeros((TA, GROUP), band_ref.dtype)

    @pl.when(i > 0)
    def _():
        band_ref[:TA] = band_ref[TA:]

    band_ref[TA:, :GROUP] = new_ref[...].astype(band_ref.dtype)


def _shift_band_t(i, band_ref, new_ref):
    @pl.when(i == 0)
    def _():
        band_ref[:, :TA] = jnp.zeros((GROUP, TA), band_ref.dtype)

    @pl.when(i > 0)
    def _():
        band_ref[:, :TA] = band_ref[:, TA:]

    band_ref[:, TA:] = new_ref[...].astype(band_ref.dtype).T


def _scaled_queries(q_ref, rows):
    return _by_head((q_ref[rows] * SCALE).astype(BF16))


def _attn_fwd(proj, proj_bf, bias_t):
    seq = proj.shape[0]
    n_tiles = seq // TA

    def body(q_ref, k_ref, v_ref, ag_ref, bias_ref, a_ref, ya_ref, e_ref, inv_ref, kband, vband_t):
        i = pl.program_id(1)
        _shift_key_band(i, kband, k_ref)
        _shift_band_t(i, vband_t, v_ref)

        def weights(sc):
            rows = slice(sc * SUPER, (sc + 1) * SUPER)
            win = _window(sc)
            e, total = _band_exp(kband[win], _scaled_queries(q_ref, rows), bias_ref[0])
            e_ref[0, sc] = e.astype(BF16)
            inv_total = 1.0 / total
            inv_ref[0, sc] = jnp.broadcast_to(inv_total, (8, PAIR_LANES))
            return inv_total

        nxt = weights(0)
        for sc in range(SUPERS_PER_TILE):
            rows = slice(sc * SUPER, (sc + 1) * SUPER)
            win = _window(sc)
            inv_total = nxt
            if sc + 1 < SUPERS_PER_TILE:
                nxt = weights(sc + 1)
            a = _own_head_rows(_dot(vband_t[:, win], e_ref[0, sc]) * inv_total)
            a_ref[rows] = a
            g = ag_ref[rows]
            ya_ref[rows] = (a * (g * _sigmoid(g))).astype(BF16)

    blk = pl.BlockSpec((TA, GROUP), lambda p, i: (i, p))

    def cols(first):
        return pl.BlockSpec((TA, GROUP), lambda p, i: (i, first + p))

    return pl.pallas_call(
        body,
        name="attn_fwd",
        grid=(N_PAIR, n_tiles),
        in_specs=[cols(Q_BLOCK), cols(K_BLOCK), cols(V_BLOCK), cols(AG_BLOCK),
                  pl.BlockSpec((1, WINDOW, PAIR_LANES), lambda p, i: (p, 0, 0))],
        out_specs=[blk, blk,
                   pl.BlockSpec((1, SUPERS_PER_TILE, WINDOW, PAIR_LANES), lambda p, i: (p, i, 0, 0)),
                   pl.BlockSpec((1, SUPERS_PER_TILE, 8, PAIR_LANES), lambda p, i: (p, i, 0, 0))],
        out_shape=[jax.ShapeDtypeStruct((seq, ATTN_WIDTH), F32), jax.ShapeDtypeStruct((seq, ATTN_WIDTH), BF16),
                   jax.ShapeDtypeStruct((N_PAIR, seq // SUPER, WINDOW, PAIR_LANES), BF16),
                   jax.ShapeDtypeStruct((N_PAIR, seq // SUPER, 8, PAIR_LANES), F32)],
        scratch_shapes=[pltpu.VMEM((2 * TA, 2 * GROUP), BF16), pltpu.VMEM((GROUP, 2 * TA), BF16)],
        compiler_params=pltpu.CompilerParams(
            dimension_semantics=("arbitrary", "arbitrary"), vmem_limit_bytes=VMEM_LIMIT),
    )(proj_bf, proj_bf, proj_bf, proj, bias_t)


def _attn_bwd(proj, proj_bf, a, dy, e_all, inv_all):
    seq = proj.shape[0]
    n_tiles = seq // TA

    def body(q_ref, k_ref, v_ref, a_ref, ag_ref, dy_ref, e_ref, inv_ref,
             dq_ref, dk_ref, dv_ref, dag_ref, db_ref, vband, kband_t, dkacc, dvacc):
        i = pl.program_id(1)

        @pl.when(i == 0)
        def _():
            dkacc[...] = jnp.zeros_like(dkacc)
            dvacc[...] = jnp.zeros_like(dvacc)
            db_ref[...] = jnp.zeros_like(db_ref)

        @pl.when(i < n_tiles)
        def _():
            _shift_band(i, vband, v_ref)
            _shift_band_t(i, kband_t, k_ref)

            def score_grads(sc):
                rows = slice(sc * SUPER, (sc + 1) * SUPER)
                win = _window(sc)
                q_rows = _scaled_queries(q_ref, rows)
                g = ag_ref[rows]
                sg = _sigmoid(g)
                dyc = dy_ref[rows]
                dag_ref[rows] = dyc * a_ref[rows] * (sg * (1.0 + g * (1.0 - sg)))
                da_rows = _by_head((dyc * (g * sg)).astype(BF16))
                p = e_ref[0, sc].astype(F32) * inv_ref[0, sc, :1]
                dp = _dot_t(vband[win], da_rows)
                ds = p * (dp - jnp.sum(p * dp, axis=0, keepdims=True))
                db_ref[0] += ds
                return q_rows, da_rows, p.astype(BF16), ds.astype(BF16)

            nxt = score_grads(0)
            for sc in range(SUPERS_PER_TILE):
                rows = slice(sc * SUPER, (sc + 1) * SUPER)
                win = _window(sc)
                q_rows, da_rows, p_bf, ds_bf = nxt
                if sc + 1 < SUPERS_PER_TILE:
                    nxt = score_grads(sc + 1)
                dq_ref[rows] = (_own_head_rows(_dot(kband_t[:, win], ds_bf)) * SCALE).astype(BF16)
                dkacc[win] += _dot(ds_bf, q_rows)
                dvacc[win] += _dot(p_bf, da_rows)

        dk_ref[...] = dkacc[:TA].astype(BF16)
        dv_ref[...] = dvacc[:TA].astype(BF16)
        dkacc[:TA] = dkacc[TA:]
        dvacc[:TA] = dvacc[TA:]
        dkacc[TA:] = jnp.zeros((TA, GROUP), F32)
        dvacc[TA:] = jnp.zeros((TA, GROUP), F32)

    last = n_tiles - 1
    cur = pl.BlockSpec((TA, GROUP), lambda p, i: (jnp.minimum(i, last), p))
    older = pl.BlockSpec((TA, GROUP), lambda p, i: (jnp.maximum(i - 1, 0), p))
    dy_blk = pl.BlockSpec((TA, GROUP), lambda p, i: (jnp.minimum(i, last), N_PAIR + p))
    per_pair = pl.BlockSpec((1, WINDOW, PAIR_LANES), lambda p, i: (p, 0, 0))

    def cols(first):
        return pl.BlockSpec((TA, GROUP), lambda p, i: (jnp.minimum(i, last), first + p))

    def kept(rows):
        return pl.BlockSpec((1, SUPERS_PER_TILE, rows, PAIR_LANES), lambda p, i: (p, jnp.minimum(i, last), 0, 0))

    def out(dtype):
        return jax.ShapeDtypeStruct((seq, ATTN_WIDTH), dtype)

    return pl.pallas_call(
        body,
        name="attn_bwd",
        grid=(N_PAIR, n_tiles + 1),
        in_specs=[cols(Q_BLOCK), cols(K_BLOCK), cols(V_BLOCK), cur, cols(AG_BLOCK), dy_blk, kept(WINDOW), kept(8)],
        out_specs=[cur, older, older, cur, per_pair],
        out_shape=[out(BF16), out(BF16), out(BF16), out(F32),
                   jax.ShapeDtypeStruct((N_PAIR, WINDOW, PAIR_LANES), F32)],
        scratch_shapes=[
            pltpu.VMEM((2 * TA, GROUP), BF16), pltpu.VMEM((GROUP, 2 * TA), BF16),
            pltpu.VMEM((2 * TA, GROUP), F32), pltpu.VMEM((2 * TA, GROUP), F32)],
        compiler_params=pltpu.CompilerParams(
            dimension_semantics=("arbitrary", "arbitrary"), vmem_limit_bytes=VMEM_LIMIT),
    )(proj_bf, proj_bf, proj_bf, a, proj, dy, e_all, inv_all)


BIN_ROWS = 136


def _bias_bins(db_t):
    idx_t = jnp.asarray(_rel_index_tile())

    def body(db_ref, idx_ref, out_ref):
        lane = lax.broadcasted_iota(jnp.int32, (1, GROUP), 1)
        row = lax.broadcasted_iota(jnp.int32, (BIN_ROWS, GROUP), 0)
        out = jnp.zeros((BIN_ROWS, GROUP), F32)
        for r in range(N_REL - 1):
            lo = 0 if r == 0 else ((BAND - 2 * CHUNK + r) // 8) * 8
            hi = WINDOW if r == 0 else min(WINDOW, lo + SUPER + 8)
            hit = jnp.where(idx_ref[lo:hi] == r, db_ref[0, lo:hi], 0.0)
            col = jnp.sum(hit, axis=0, keepdims=True)
            s0 = jnp.sum(col[:, :SUPER], axis=1, keepdims=True)
            s1 = jnp.sum(col[:, SUPER:], axis=1, keepdims=True)
            val = jnp.where(lane == 0, s0, jnp.where(lane == 1, s1, 0.0))
            out = jnp.where(row == r, val, out)
        out_ref[0] = out

    return pl.pallas_call(
        body,
        name="bias_bins",
        grid=(N_PAIR,),
        in_specs=[pl.BlockSpec((1, WINDOW, PAIR_LANES), lambda p: (p, 0, 0)), VMEM],
        out_specs=pl.BlockSpec((1, BIN_ROWS, GROUP), lambda p: (p, 0, 0)),
        out_shape=jax.ShapeDtypeStruct((N_PAIR, BIN_ROWS, GROUP), F32),
        compiler_params=pltpu.CompilerParams(dimension_semantics=("arbitrary",)),
    )(db_t, idx_t)


def _out_loss(x, tgt, proj, ya, wout_full, g2, pw_bf, ps):
    seq = x.shape[0]
    n_tiles = seq // TS

    def body(x_ref, t_ref, pv_ref, pg_ref, ya_ref, w_ref, g2_ref, pw_ref, ps_ref,
             dx2_ref, dy_ref, gw_ref, gg_ref, loss_ref, sq_ref, halo_ref, wt_ref):
        i = pl.program_id(0)

        @pl.when(i == 0)
        def _():
            gw_ref[...] = jnp.zeros_like(gw_ref)
            gg_ref[...] = jnp.zeros_like(gg_ref)
            sq_ref[...] = jnp.zeros_like(sq_ref)
            halo_ref[...] = jnp.zeros_like(halo_ref)
            for b in range(N_CHIP):
                wt_ref[b] = w_ref[b].T

        pv = pv_ref[...]
        pg = pg_ref[...]
        diffs = [d.astype(BF16) for d in _pool_diffs(pv, halo_ref[...], i)]
        halo_ref[...] = pv[TS - HALO:, :]
        yp = ((_pool_mix(diffs, pw_ref) * ps_ref[...]) * (pg * _sigmoid(pg))).astype(BF16)
        g2v = g2_ref[...]

        def parts(rows):
            return [yp[rows, :SHARD_OUT], yp[rows, SHARD_OUT:], ya_ref[rows, :SHARD_OUT], ya_ref[rows, SHARD_OUT:]]

        def project(rows, ys):
            x2 = x_ref[rows]
            for b in range(N_CHIP):
                x2 = x2 + _dot(ys[b], w_ref[b])
            return x2

        def norm_loss(rows, x2):
            r = lax.rsqrt(jnp.mean(x2 * x2, axis=-1, keepdims=True) + EPS)
            xh = x2 * r
            diff = xh * g2v - t_ref[rows]
            sq_ref[...] += jnp.sum(diff * diff, axis=0, keepdims=True)
            dfin = diff * (1.0 / D_MODEL)
            gg_ref[...] += jnp.sum(dfin * xh, axis=0, keepdims=True)
            dxh = dfin * g2v
            dx2 = r * (dxh - xh * jnp.mean(dxh * xh, axis=-1, keepdims=True))
            dx2_ref[rows] = dx2
            return dx2.astype(BF16)

        def back(rows, ys, dx2_bf):
            for b in range(N_CHIP):
                gw_ref[b] += _tdot(ys[b], dx2_bf)
                dy_ref[rows, b * SHARD_OUT:(b + 1) * SHARD_OUT] = _dot(dx2_bf, wt_ref[b])

        n_parts = 2
        part = TS // n_parts
        spans = [slice(r * part, (r + 1) * part) for r in range(n_parts)]
        ys = [parts(rows) for rows in spans]
        x2_next = project(spans[0], ys[0])
        for r in range(n_parts):
            x2 = x2_next
            if r + 1 < n_parts:
                x2_next = project(spans[r + 1], ys[r + 1])
            back(spans[r], ys[r], norm_loss(spans[r], x2))

        @pl.when(i == n_tiles - 1)
        def _():
            total = jnp.sum(sq_ref[...], axis=1, keepdims=True) * (0.5 / D_MODEL)
            loss_ref[...] = jnp.broadcast_to(total, loss_ref.shape)

    def rows(width, col=0):
        return pl.BlockSpec((TS, width), lambda i: (i, col))

    return pl.pallas_call(
        body,
        name="out_loss",
        grid=(n_tiles,),
        in_specs=[rows(D_MODEL), rows(D_MODEL), rows(POOL_WIDTH, 0), rows(POOL_WIDTH, 1), rows(ATTN_WIDTH),
                  VMEM, VMEM, VMEM, VMEM],
        out_specs=[rows(D_MODEL), rows(D_MODEL), VMEM, VMEM, VMEM],
        out_shape=[
            jax.ShapeDtypeStruct((seq, D_MODEL), F32), jax.ShapeDtypeStruct((seq, D_MODEL), F32),
            jax.ShapeDtypeStruct((N_CHIP, SHARD_OUT, D_MODEL), F32), jax.ShapeDtypeStruct((1, D_MODEL), F32),
            jax.ShapeDtypeStruct((8, GROUP), F32)],
        scratch_shapes=[pltpu.VMEM((1, D_MODEL), F32), pltpu.VMEM((HALO, POOL_WIDTH), F32),
                        pltpu.VMEM((N_CHIP, D_MODEL, SHARD_OUT), BF16)],
        compiler_params=pltpu.CompilerParams(dimension_semantics=("arbitrary",), vmem_limit_bytes=VMEM_LIMIT),
    )(x, tgt, proj, proj, ya, wout_full, g2, pw_bf, ps)


def _inproj_bwd(x, dx2, dy, proj, dq, dk, dv, dag, win_full, pw_bf, g1, ps):
    seq = x.shape[0]
    n_tiles = seq // TS

    def body(x_ref, dx2_ref, dyp_ref, pv_ref, pvprev_ref, pg_ref, dq_ref, dk_ref, dv_ref, dag_ref,
             w_ref, pw_ref, g1_ref, ps_ref, gx_ref, dproj_ref, ht_ref, gg_ref, gps_ref, gpw_ref, halo_ref):
        i = pl.program_id(0)
        tile = n_tiles - 1 - i

        @pl.when(i == 0)
        def _():
            gg_ref[...] = jnp.zeros_like(gg_ref)
            gps_ref[...] = jnp.zeros_like(gps_ref)
            gpw_ref[...] = jnp.zeros_like(gpw_ref)
            halo_ref[...] = jnp.zeros_like(halo_ref)

        pv_t = pv_ref[...]
        pg_t = pg_ref[...]
        prev_rows = jnp.where(tile > 0, pvprev_ref[...], 0.0)
        diffs = [d.astype(BF16) for d in _pool_diffs(pv_t, prev_rows, tile)]
        mixed = _pool_mix(diffs, pw_ref)
        sg = _sigmoid(pg_t)
        silu = pg_t * sg
        dyp = dyp_ref[...]
        psv = ps_ref[...]
        gps_ref[...] += jnp.sum(dyp * mixed * silu, axis=0, keepdims=True)
        dmixed = (dyp * psv * silu).astype(BF16)
        dpg = dyp * (mixed * psv) * (sg * (1.0 + pg_t * (1.0 - sg)))
        counts = _row_counts(tile, TS)
        dds = []
        for g in range(len(POOL_WINDOWS)):
            dm_g = dmixed[:, g * GROUP:(g + 1) * GROUP]
            gpw_ref[g] += _tdot(diffs[g], dm_g)
            dds.append(_dot_t(dm_g, pw_ref[g]))
        dd = jnp.concatenate(dds, axis=1)
        spread = jnp.concatenate([dds[g] / counts[g] for g in range(len(POOL_WINDOWS))], axis=1)
        sums = _window_sums(jnp.concatenate([spread, halo_ref[...]], axis=0), forward=True)
        halo_ref[...] = spread[:HALO]
        dpv = jnp.concatenate(
            [sums[g][:TS, g * GROUP:(g + 1) * GROUP] for g in range(len(POOL_WINDOWS))], axis=1) - dd

        xt = x_ref[...]
        r = lax.rsqrt(jnp.mean(xt * xt, axis=-1, keepdims=True) + EPS)
        xh = xt * r
        g1v = g1_ref[...]
        ht_ref[...] = (xh * g1v).astype(BF16).T
        dproj = jnp.concatenate(
            [dpv.astype(BF16), dpg.astype(BF16), dq_ref[...], dk_ref[...], dv_ref[...], dag_ref[...].astype(BF16)],
            axis=1)
        dproj_ref[...] = dproj
        dh = _dot_t(dproj[:, :SHARD_IN], w_ref[0])
        for chip in range(1, N_CHIP):
            dh = dh + _dot_t(dproj[:, chip * SHARD_IN:(chip + 1) * SHARD_IN], w_ref[chip])

        gg_ref[...] += jnp.sum(dh * xh, axis=0, keepdims=True)
        dxh = dh * g1v
        gx_ref[...] = dx2_ref[...] + r * (dxh - xh * jnp.mean(dxh * xh, axis=-1, keepdims=True))

    def rows(width, col=0):
        return pl.BlockSpec((TS, width), lambda i: (n_tiles - 1 - i, col))

    prev = pl.BlockSpec((HALO, POOL_WIDTH), lambda i: (jnp.maximum((n_tiles - 1 - i) * (TS // HALO) - 1, 0), 0))
    return pl.pallas_call(
        body,
        name="inproj_bwd",
        grid=(n_tiles,),
        in_specs=[rows(D_MODEL), rows(D_MODEL), rows(POOL_WIDTH), rows(POOL_WIDTH), prev, rows(POOL_WIDTH, 1),
                  rows(ATTN_WIDTH), rows(ATTN_WIDTH), rows(ATTN_WIDTH), rows(ATTN_WIDTH), VMEM, VMEM, VMEM, VMEM],
        out_specs=[rows(D_MODEL), rows(N_CHIP * SHARD_IN),
                   pl.BlockSpec((D_MODEL, TS), lambda i: (0, n_tiles - 1 - i)), VMEM, VMEM, VMEM],
        out_shape=[
            jax.ShapeDtypeStruct((seq, D_MODEL), F32),
            jax.ShapeDtypeStruct((seq, N_CHIP * SHARD_IN), BF16),
            jax.ShapeDtypeStruct((D_MODEL, seq), BF16),
            jax.ShapeDtypeStruct((1, D_MODEL), F32),
            jax.ShapeDtypeStruct((1, POOL_WIDTH), F32),
            jax.ShapeDtypeStruct((len(POOL_WINDOWS), GROUP, GROUP), F32)],
        scratch_shapes=[pltpu.VMEM((HALO, POOL_WIDTH), F32)],
        compiler_params=pltpu.CompilerParams(dimension_semantics=("arbitrary",), vmem_limit_bytes=VMEM_LIMIT),
    )(x, dx2, dy, proj, proj, proj, dq, dk, dv, dag, win_full, pw_bf, g1, ps)


PASS_PEER = (2, 0, 1)


def _gw_reduce(ht, dproj, gwout, small):
    seq = ht.shape[1]
    tm = min(2 * TS, seq // 8)
    n_tiles = seq // tm
    half_small = SMALL_ROWS // 2
    cx, cy = lax.axis_index("x"), lax.axis_index("y")
    outer = _other_chips(cx, cy)
    order = jnp.stack([outer[n][1] for n in PASS_PEER] + [2 * cx + cy]).astype(jnp.int32)

    def body(order_ref, ht_ref, dp_ref, gwout_ref, small_ref, gin_final, gout_final, small_final,
             hbuf, acc, pair_in, pair_out, pair_small, tx_in, tx_out, rx_in, rx_out, rx_small,
             gin_out, gout_out, small_out, send_sems, recv_sems):
        j = pl.program_id(0)
        i = pl.program_id(1)
        x, y, c = _my_place()
        b = 2 * x + y
        sibling = (x, y, 1 - c)
        others = _other_chips(x, y)
        mine_in = pl.ds(pl.multiple_of(c * HALF_IN, HALF_IN), HALF_IN)
        mine_out = pl.ds(pl.multiple_of(c * HALF_OUT, HALF_OUT), HALF_OUT)
        mine_small = pl.ds(pl.multiple_of(c * half_small, 8), half_small)
        theirs_in = pl.ds(pl.multiple_of((1 - c) * HALF_IN, HALF_IN), HALF_IN)
        theirs_out = pl.ds(pl.multiple_of((1 - c) * HALF_OUT, HALF_OUT), HALF_OUT)
        theirs_small = pl.ds(pl.multiple_of((1 - c) * half_small, 8), half_small)

        def copy(k, src, dst, to):
            return pltpu.make_async_remote_copy(
                src_ref=src, dst_ref=dst, send_sem=send_sems.at[k], recv_sem=recv_sems.at[k],
                device_id=to, device_id_type=MESH)

        swap_out = copy(0, gwout_ref.at[:, theirs_out], pair_out, sibling)
        swap_small = copy(1, small_ref, pair_small, sibling)

        def swap_in(p):
            return copy(2 + p, acc.at[p % 2, theirs_in], pair_in.at[p % 2], sibling)

        def to_chip(n, t):
            to = (*others[n][0], c)
            if t == 0:
                return copy(6 + 3 * n, tx_in.at[n], rx_in.at[n], to)
            if t == 1:
                return copy(7 + 3 * n, tx_out.at[n], rx_out.at[n], to)
            return copy(8 + 3 * n, pair_small.at[mine_small], rx_small.at[b], to)

        share_in = copy(15, gin_out.at[mine_in], gin_out.at[mine_in], sibling)
        share_out = copy(16, gout_out.at[mine_out], gout_out.at[mine_out], sibling)
        share_small = copy(17, small_out.at[mine_small], small_out.at[mine_small], sibling)

        def at(jj, ii):
            return (j == jj) & (i == ii)

        par = j % 2

        @pl.when(i == 0)
        def _():
            acc[par] = jnp.zeros((D_MODEL, SHARD_IN), F32)

        cols = pl.ds(pl.multiple_of(i * tm, tm), tm)

        @pl.when(j == 0)
        def _():
            hbuf[:, cols] = ht_ref[...]

        acc[par] += _dot(hbuf[:, cols], dp_ref[...])

        @pl.when(at(0, 0))
        def _():
            swap_out.start()
            swap_small.start()

        @pl.when(at(0, 2))
        def _():
            swap_out.wait_recv()
            swap_small.wait_recv()
            for chip in range(N_CHIP):
                pair_out[chip] = gwout_ref[chip, mine_out] + pair_out[chip]
            pair_small[...] = small_ref[...] + pair_small[...]
            rx_small[b] = pair_small[mine_small]
            for n in range(N_CHIP - 1):
                tx_out[n] = pair_out[others[n][1]].astype(BF16)
                to_chip(n, 1).start()
                to_chip(n, 2).start()

        @pl.when(at(1, 4))
        def _():
            total_out = pair_out[b]
            for n in range(N_CHIP - 1):
                to_chip(n, 1).wait_recv()
                copy(8 + 3 * n, pair_small.at[mine_small], rx_small.at[others[n][1]], (*others[n][0], c)).wait_recv()
                total_out = total_out + rx_out[n].astype(F32)
            gout_out[mine_out] = total_out
            small_out[mine_small] = ((rx_small[0] + rx_small[1]) + rx_small[2]) + rx_small[3]
            share_out.start()
            share_small.start()

        for p in range(N_CHIP - 1):
            n = PASS_PEER[p]

            @pl.when(at(p + 1, 0))
            def _(p=p):
                swap_in(p).start()

            @pl.when(at(p + 1, 2))
            def _(p=p, n=n):
                swap_in(p).wait_recv()
                swap_in(p).wait_send()
                tx_in[n] = (acc[p % 2, mine_in] + pair_in[p % 2]).astype(BF16)
                to_chip(n, 0).start()

        @pl.when(at(N_CHIP - 1, n_tiles - 1))
        def _():
            last = N_CHIP - 1
            swap_in(last).start()
            swap_in(last).wait_recv()
            total_in = acc[last % 2, mine_in] + pair_in[last % 2]
            for n in range(N_CHIP - 1):
                to_chip(n, 0).wait_recv()
                total_in = total_in + rx_in[n].astype(F32)
            gin_out[mine_in] = total_in
            share_in.start()
            copy(15, gin_out.at[theirs_in], gin_out.at[theirs_in], sibling).wait_recv()
            copy(16, gout_out.at[theirs_out], gout_out.at[theirs_out], sibling).wait_recv()
            copy(17, small_out.at[theirs_small], small_out.at[theirs_small], sibling).wait_recv()
            swap_out.wait_send()
            swap_small.wait_send()
            swap_in(last).wait_send()
            for n in range(N_CHIP - 1):
                for t in range(3):
                    to_chip(n, t).wait_send()
            share_in.wait_send()
            share_out.wait_send()
            share_small.wait_send()
            gin_final[...] = gin_out[...]
            gout_final[...] = gout_out[...]
            small_final[...] = small_out[...]

    assert n_tiles >= 5, "the reduction's steps are spread over the first five token steps of a pass"
    grid_spec = pltpu.PrefetchScalarGridSpec(
        num_scalar_prefetch=1,
        grid=(N_CHIP, n_tiles),
        in_specs=[
            pl.BlockSpec((D_MODEL, tm), lambda j, i, o: (0, jnp.where(j == 0, i, n_tiles - 1))),
            pl.BlockSpec((tm, SHARD_IN), lambda j, i, o: (i, o[j])),
            VMEM, VMEM],
        out_specs=[VMEM, VMEM, VMEM],
        scratch_shapes=[
            pltpu.VMEM((D_MODEL, seq), BF16),
            pltpu.VMEM((2, D_MODEL, SHARD_IN), F32),
            pltpu.VMEM((2, HALF_IN, SHARD_IN), F32),
            pltpu.VMEM((N_CHIP, HALF_OUT, D_MODEL), F32),
            pltpu.VMEM((SMALL_ROWS, GROUP), F32),
            pltpu.VMEM((N_CHIP - 1, HALF_IN, SHARD_IN), BF16),
            pltpu.VMEM((N_CHIP - 1, HALF_OUT, D_MODEL), BF16),
            pltpu.VMEM((N_CHIP - 1, HALF_IN, SHARD_IN), BF16),
            pltpu.VMEM((N_CHIP - 1, HALF_OUT, D_MODEL), BF16),
            pltpu.VMEM((N_CHIP, half_small, GROUP), F32),
            pltpu.VMEM((D_MODEL, SHARD_IN), F32),
            pltpu.VMEM((SHARD_OUT, D_MODEL), F32),
            pltpu.VMEM((SMALL_ROWS, GROUP), F32),
            pltpu.SemaphoreType.DMA((18,)),
            pltpu.SemaphoreType.DMA((18,)),
        ],
    )
    return pl.pallas_call(
        body,
        name="gw_reduce",
        grid_spec=grid_spec,
        out_shape=(
            jax.ShapeDtypeStruct((D_MODEL, SHARD_IN), F32),
            jax.ShapeDtypeStruct((SHARD_OUT, D_MODEL), F32),
            jax.ShapeDtypeStruct((SMALL_ROWS, GROUP), F32),
        ),
        compiler_params=pltpu.CompilerParams(
            dimension_semantics=("arbitrary", "arbitrary"), vmem_limit_bytes=VMEM_LIMIT),
    )(order, ht, dproj, gwout, small)


def _adamw(name, w, g, m, v, block_rows):
    rows, cols = w.shape

    def body(w_ref, g_ref, m_ref, v_ref, d_ref, m_out, v_out):
        grad = g_ref[...]
        m_new = ADAM_B1 * m_ref[...] + (1.0 - ADAM_B1) * grad
        v_new = ADAM_B2 * v_ref[...] + (1.0 - ADAM_B2) * (grad * grad)
        m_hat = m_new / (1.0 - ADAM_B1 ** ADAM_STEP)
        v_hat = v_new / (1.0 - ADAM_B2 ** ADAM_STEP)
        d_ref[...] = -ADAM_LR * (m_hat / (jnp.sqrt(v_hat) + ADAM_EPS) + ADAM_WD * w_ref[...])
        m_out[...] = m_new
        v_out[...] = v_new

    blk = pl.BlockSpec((block_rows, cols), lambda i: (i, 0))
    shape = jax.ShapeDtypeStruct((rows, cols), F32)
    return pl.pallas_call(
        body,
        name=name,
        grid=(rows // block_rows,),
        in_specs=[blk] * 4,
        out_specs=[blk] * 3,
        out_shape=[shape] * 3,
        compiler_params=pltpu.CompilerParams(dimension_semantics=("arbitrary",)),
    )(w, g, m, v)


def _pack_small(norm_gain, pool_w, pool_scale, rel_bias, final_gain, loss_rows):
    parts = [
        norm_gain.reshape(8, GROUP),
        pool_w.reshape(len(POOL_WINDOWS) * GROUP, GROUP),
        jnp.pad(pool_scale.reshape(4, GROUP), ((0, 4), (0, 0))),
        jnp.pad(rel_bias.reshape(8, N_REL), ((0, 0), (0, 2 * GROUP - N_REL))).reshape(16, GROUP),
        final_gain.reshape(8, GROUP),
        loss_rows,
    ]
    return jnp.concatenate(parts, axis=0)


def _unpack_small(block):
    norm_gain = block[0:8].reshape(1, D_MODEL)
    pool_w = block[8:520].reshape(1, len(POOL_WINDOWS), GROUP, GROUP)
    pool_scale = block[520:524].reshape(1, POOL_WIDTH)
    rel_bias = block[528:544].reshape(8, 2 * GROUP)[:, :N_REL].reshape(1, 8, N_REL)
    final_gain = block[544:552].reshape(D_MODEL)
    return norm_gain, pool_w, pool_scale, rel_bias, final_gain


def kernel(x, norm_gain, w_in, pool_w, pool_scale, rel_bias, w_out, final_norm_gain, loss_target, m_norm_gain, m_w_in, m_pool_w, m_pool_scale, m_rel_bias, m_w_out, m_final_norm_gain, v_norm_gain, v_w_in, v_pool_w, v_pool_scale, v_rel_bias, v_w_out, v_final_norm_gain):
    assert x.shape[1] % TS == 0 and x.shape[2] == D_MODEL
    xs = x[0]
    tgt = loss_target[0]
    g1 = norm_gain.reshape(1, D_MODEL)
    g2 = final_norm_gain.reshape(1, D_MODEL)
    ps = pool_scale.reshape(1, POOL_WIDTH)
    pw_bf = pool_w[0].astype(BF16)

    proj, proj_bf, win_full, wout_full = _gather_inproj(xs, g1, w_in[0].astype(BF16), w_out[0].astype(BF16))
    bias_t = _bias_tile(rel_bias[0])

    a, ya, e_all, inv_all = _attn_fwd(proj, proj_bf, bias_t)
    dx2, dy, gwout, gg2, loss_rows = _out_loss(xs, tgt, proj, ya, wout_full, g2, pw_bf, ps)
    dq, dk, dv, dag, db_t = _attn_bwd(proj, proj_bf, a, dy, e_all, inv_all)
    bins = _bias_bins(db_t)
    gx, dproj, ht, gg1, gps, gpw = _inproj_bwd(xs, dx2, dy, proj, dq, dk, dv, dag, win_full, pw_bf, g1, ps)

    g_bias = bins[:, :N_REL, :2].transpose(0, 2, 1).reshape(8, N_REL)
    small = _pack_small(gg1, gpw, gps, g_bias, gg2, loss_rows)
    g_win, g_wout, g_small = _gw_reduce(ht, dproj, gwout, small)
    loss = g_small[LOSS_ROW, 0]

    zeros8 = jnp.zeros((8, GROUP), F32)
    w_small = _pack_small(norm_gain, pool_w, pool_scale, rel_bias, final_norm_gain, zeros8)
    m_small = _pack_small(m_norm_gain, m_pool_w, m_pool_scale, m_rel_bias, m_final_norm_gain, zeros8)
    v_small = _pack_small(v_norm_gain, v_pool_w, v_pool_scale, v_rel_bias, v_final_norm_gain, zeros8)

    d_win, m_win, v_win = _adamw("adamw_w_in", w_in[0], g_win, m_w_in[0], v_w_in[0], 256)
    d_wout, m_wout, v_wout = _adamw("adamw_w_out", w_out[0], g_wout, m_w_out[0], v_w_out[0], 128)
    d_small, m_new_small, v_new_small = _adamw("adamw_small", w_small, g_small, m_small, v_small, SMALL_ROWS // 2)

    def full(win_part, wout_part, block):
        ng, pw, psc, rb, fg = _unpack_small(block)
        return [ng, win_part[None], pw, psc, rb, wout_part[None], fg]

    grads = full(g_win, g_wout, g_small)
    deltas = full(d_win, d_wout, d_small)
    new_m = full(m_win, m_wout, m_new_small)
    new_v = full(v_win, v_wout, v_new_small)
    return (loss, gx[None], *grads, *deltas, *new_m, *new_v)
```

```python
import numpy as np
import jax
import jax.numpy as jnp
from jax import lax
from jax.experimental import pallas as pl
from jax.experimental.pallas import tpu as pltpu

F32 = jnp.float32
BF16 = jnp.bfloat16

D_MODEL = 1024
POOL_WIDTH = 512
ATTN_WIDTH = 512
POOL_WINDOWS = (2, 4, 8, 16)
GROUP = 128
CHUNK = 64
LEFT_CHUNKS = 8
BAND = (LEFT_CHUNKS + 1) * CHUNK
HEAD_DIM = 64
N_PAIR = 4
MAX_REL = 64
N_REL = 2 * MAX_REL + 1
EPS = 1e-6
MASK_VALUE = -1e30
SCALE = 0.125

ADAM_LR = 0.001
ADAM_B1 = 0.9
ADAM_B2 = 0.999
ADAM_EPS = 1e-08
ADAM_WD = 0.01
ADAM_STEP = 10

TS = LEFT_CHUNKS * CHUNK
SUPER = 2 * CHUNK
WINDOW = BAND + CHUNK
TA = 4 * TS
WIN_BASE = TA - LEFT_CHUNKS * CHUNK
SUPERS_PER_TILE = TA // SUPER
PAIR_LANES = 2 * SUPER
HALO = 16
N_CHIP = 4
SHARD_IN = 768
SHARD_OUT = 256
PIECE = 256
PIECES_PER_SHARD = SHARD_IN // PIECE
HALF_IN = D_MODEL // 2
HALF_OUT = SHARD_OUT // 2
SMALL_ROWS = 560
LOSS_ROW = 552
VMEM_LIMIT = 60 * 1024 * 1024

MESH = pl.DeviceIdType.MESH
ANY = pl.BlockSpec(memory_space=pl.ANY)
VMEM = pl.BlockSpec(memory_space=pltpu.VMEM)


def _sigmoid(x):
    return 1.0 / (1.0 + jnp.exp(-x))


def _dot(a, b):
    return jnp.dot(a, b, preferred_element_type=F32)


def _dot_t(a, b):
    return lax.dot_general(a, b, (((1,), (1,)), ((), ())), preferred_element_type=F32)


def _tdot(a, b):
    return lax.dot_general(a, b, (((0,), (0,)), ((), ())), preferred_element_type=F32)


def _my_place():
    return lax.axis_index("x"), lax.axis_index("y"), lax.axis_index("c")


def _other_chips(x, y):
    places = [(1 - x, y), (x, 1 - y), (1 - x, 1 - y)]
    return [(p, 2 * p[0] + p[1]) for p in places]


def _gather_inproj(x, g1, win_sh, wout_sh):
    seq = x.shape[0]
    tm = min(seq, 4 * TS)
    n_tiles = seq // tm
    cx, cy = lax.axis_index("x"), lax.axis_index("y")
    order = jnp.stack([2 * cx + cy] + [chip for _, chip in _other_chips(cx, cy)]).astype(jnp.int32)

    def body(order_ref, x_ref, g1_ref, win_ref, wout_ref, proj_bf_ref, win_full, wout_full,
             hbuf, wbuf, send_sems, recv_sems, local_sems):
        j = pl.program_id(0)
        i = pl.program_id(1)
        x_, y_, c = _my_place()
        b = 2 * x_ + y_
        sibling = (x_, y_, 1 - c)
        others = _other_chips(x_, y_)

        def halves(chip, core):
            return (
                win_full.at[chip, pl.ds(core * HALF_IN, HALF_IN)],
                wout_full.at[chip, pl.ds(core * HALF_OUT, HALF_OUT)],
            )

        def copy(k, src, dst, to):
            return pltpu.make_async_remote_copy(
                src_ref=src, dst_ref=dst, send_sem=send_sems.at[k], recv_sem=recv_sems.at[k],
                device_id=to, device_id_type=MESH)

        own = [
            pltpu.make_async_copy(win_ref, win_full.at[b], local_sems.at[0]),
            pltpu.make_async_copy(wout_ref, wout_full.at[b], local_sems.at[1]),
        ]
        mine_src = (win_ref.at[pl.ds(c * HALF_IN, HALF_IN)], wout_ref.at[pl.ds(c * HALF_OUT, HALF_OUT)])

        def direct(n, t):
            return copy(2 * n + t, mine_src[t], halves(b, c)[t], (*others[n][0], c))

        def arrival(n, t):
            landed = halves(others[n][1], c)[t]
            return copy(2 * n + t, landed, landed, (*others[n][0], c))

        def passing(n, t):
            landed = halves(others[n][1], c)[t]
            return copy(6 + 2 * n + t, landed, landed, sibling)

        def from_sibling(n, t):
            landed = halves(others[n][1], 1 - c)[t]
            return copy(6 + 2 * n + t, landed, landed, sibling)

        @pl.when((j == 0) & (i == 0))
        def _():
            first = pltpu.make_async_copy(win_ref, wbuf, local_sems.at[2])
            first.start()
            for cp in own:
                cp.start()
            for t in range(2):
                for n in range(N_CHIP - 1):
                    direct(n, t).start()
            first.wait()

        for n in range(N_CHIP - 1):
            @pl.when((j == n + 1) & (i == 0))
            def _(n=n):
                arrival(n, 0).wait_recv()
                passing(n, 0).start()
                from_sibling(n, 0).wait_recv()
                load = pltpu.make_async_copy(win_full.at[others[n][1]], wbuf, local_sems.at[2])
                load.start()
                load.wait()

        rows = pl.ds(pl.multiple_of(i * tm, tm), tm)

        @pl.when(j == 0)
        def _():
            xt = x_ref[...]
            r = lax.rsqrt(jnp.mean(xt * xt, axis=-1, keepdims=True) + EPS)
            hbuf[rows] = ((xt * r) * g1_ref[...]).astype(BF16)

        proj_bf_ref[...] = _dot(hbuf[rows], wbuf[...]).astype(BF16)

        @pl.when((j == N_CHIP - 1) & (i == n_tiles - 1))
        def _():
            for n in range(N_CHIP - 1):
                arrival(n, 1).wait_recv()
                passing(n, 1).start()
            for n in range(N_CHIP - 1):
                from_sibling(n, 1).wait_recv()
            for n in range(N_CHIP - 1):
                for t in range(2):
                    direct(n, t).wait_send()
                    passing(n, t).wait_send()
            for cp in own:
                cp.wait()

    last = n_tiles - 1
    grid_spec = pltpu.PrefetchScalarGridSpec(
        num_scalar_prefetch=1,
        grid=(N_CHIP, n_tiles),
        in_specs=[
            pl.BlockSpec((tm, D_MODEL), lambda j, i, o: (jnp.where(j == 0, i, last), 0)),
            VMEM, ANY, ANY],
        out_specs=[pl.BlockSpec((tm, SHARD_IN), lambda j, i, o: (i, o[j])), ANY, ANY],
        scratch_shapes=[
            pltpu.VMEM((seq, D_MODEL), BF16), pltpu.VMEM((D_MODEL, SHARD_IN), BF16),
            pltpu.SemaphoreType.DMA((12,)), pltpu.SemaphoreType.DMA((12,)), pltpu.SemaphoreType.DMA((3,))],
    )
    return pl.pallas_call(
        body,
        name="gather_inproj",
        grid_spec=grid_spec,
        out_shape=(
            jax.ShapeDtypeStruct((seq, N_CHIP * SHARD_IN), BF16),
            jax.ShapeDtypeStruct((N_CHIP, D_MODEL, SHARD_IN), BF16),
            jax.ShapeDtypeStruct((N_CHIP, SHARD_OUT, D_MODEL), BF16),
        ),
        compiler_params=pltpu.CompilerParams(
            dimension_semantics=("arbitrary", "arbitrary"), vmem_limit_bytes=VMEM_LIMIT),
    )(order, x, g1, win_sh, wout_sh)


def _window_sums(ext, forward):
    n = ext.shape[0]
    sums = []
    acc = ext
    for step in (1, 2, 4, 8):
        acc = acc + pltpu.roll(acc, (n - step) if forward else step, 0)
        sums.append(acc)
    return sums


def _row_counts(tile, rows):
    t = tile * TS + lax.broadcasted_iota(jnp.int32, (rows, GROUP), 0)
    return [jnp.minimum(t + 1, w).astype(F32) for w in POOL_WINDOWS]


def _pool_diffs(pv, prev_rows, tile):
    ext = jnp.concatenate([prev_rows, pv], axis=0)
    sums = _window_sums(ext, forward=False)
    counts = _row_counts(tile, TS)
    out = []
    for g in range(len(POOL_WINDOWS)):
        cols = slice(g * GROUP, (g + 1) * GROUP)
        out.append(sums[g][HALO:, cols] / counts[g] - pv[:, cols])
    return out


def _pool_mix(diffs, pw_ref):
    return jnp.concatenate([_dot(diffs[g], pw_ref[g]) for g in range(len(POOL_WINDOWS))], axis=1)


Q_BLOCK, K_BLOCK, V_BLOCK, AG_BLOCK = 8, 12, 16, 20


def _bias_tile(rel_bias):
    flat = jnp.concatenate(
        [jnp.broadcast_to(rel_bias[:, :1], (rel_bias.shape[0], BAND - CHUNK - 1)), rel_bias[:, :2 * MAX_REL]], axis=1)
    rows = [flat[:, CHUNK - 1 - i:CHUNK - 1 - i + BAND] for i in range(CHUNK)]
    bias = jnp.stack(rows, axis=1)
    first = jnp.pad(bias, ((0, 0), (0, 0), (0, CHUNK)), constant_values=MASK_VALUE)
    second = jnp.pad(bias, ((0, 0), (0, 0), (CHUNK, 0)), constant_values=MASK_VALUE)
    both = jnp.concatenate([first, second], axis=1)
    return both.reshape(N_PAIR, PAIR_LANES, WINDOW).transpose(0, 2, 1)


def _rel_index_tile():
    j = np.arange(WINDOW)[:, None]
    q = np.arange(PAIR_LANES)[None, :] % SUPER
    band_key = j - CHUNK * (q // CHUNK)
    idx = np.clip(band_key - LEFT_CHUNKS * CHUNK - q % CHUNK, -MAX_REL, MAX_REL) + MAX_REL
    return np.where((band_key >= 0) & (band_key < BAND), idx, -1).astype(np.int32)


def _by_head(block):
    low = lax.broadcasted_iota(jnp.int32, block.shape, 1) < HEAD_DIM
    zero = jnp.zeros_like(block)
    return jnp.concatenate([jnp.where(low, block, zero), jnp.where(low, zero, block)], axis=0)


def _own_head_rows(cross):
    head_of_row = lax.broadcasted_iota(jnp.int32, cross.shape, 0) >= HEAD_DIM
    head_of_lane = lax.broadcasted_iota(jnp.int32, cross.shape, 1) >= SUPER
    both = jnp.where(jnp.logical_xor(head_of_row, head_of_lane), 0.0, cross).T
    return both[:SUPER] + both[SUPER:]


def _with_mask_lane(q_rows):
    lane = lax.broadcasted_iota(jnp.int32, q_rows.shape, 1)
    return jnp.concatenate([q_rows, jnp.where(lane == 0, MASK_VALUE, 0.0).astype(q_rows.dtype)], axis=1)


def _band_exp(kb, q_rows, bias):
    s = _dot_t(kb, _with_mask_lane(q_rows)) + bias
    e = jnp.exp(s - jnp.max(s, axis=0, keepdims=True))
    return e, jnp.sum(e, axis=0, keepdims=True)


def _window(sc):
    return slice(WIN_BASE + sc * SUPER, WIN_BASE + sc * SUPER + WINDOW)


def _shift_band(i, band_ref, new_ref):
    @pl.when(i == 0)
    def _():
        band_ref[:TA] = jnp.zeros((TA, GROUP), band_ref.dtype)

    @pl.when(i > 0)
    def _():
        band_ref[:TA] = band_ref[TA:]

    band_ref[TA:] = new_ref[...].astype(band_ref.dtype)


def _shift_key_band(i, band_ref, new_ref):
    @pl.when(i == 0)
    def _():
        lane = lax.broadcasted_iota(jnp.int32, (TA, 2 * GROUP), 1)
        band_ref[:TA] = jnp.where(lane == GROUP, 1.0, 0.0).astype(band_ref.dtype)
        band_ref[TA:, GROUP:] = jnp.zeros((TA, GROUP), band_ref.dtype)

    @pl.when(i > 0)
    def _():
        band_ref[:TA] = band_ref[TA:]

    band_ref[TA:, :GROUP] = new_ref[...].astype(band_ref.dtype)


def _shift_band_t(i, band_ref, new_ref):
    @pl.when(i == 0)
    def _():
        band_ref[:, :TA] = jnp.zeros((GROUP, TA), band_ref.dtype)

    @pl.when(i > 0)
    def _():
        band_ref[:, :TA] = band_ref[:, TA:]

    band_ref[:, TA:] = new_ref[...].astype(band_ref.dtype).T


def _scaled_queries(q_ref, rows):
    return _by_head((q_ref[rows] * SCALE).astype(BF16))


def _attn_fwd(proj_bf, bias_t):
    seq = proj_bf.shape[0]
    n_tiles = seq // TA

    def body(q_ref, k_ref, v_ref, ag_ref, bias_ref, a_ref, ya_ref, e_ref, inv_ref, kband, vband_t):
        i = pl.program_id(1)
        _shift_key_band(i, kband, k_ref)
        _shift_band_t(i, vband_t, v_ref)

        def weights(sc):
            rows = slice(sc * SUPER, (sc + 1) * SUPER)
            win = _window(sc)
            e, total = _band_exp(kband[win], _scaled_queries(q_ref, rows), bias_ref[0])
            e_ref[0, sc] = e.astype(BF16)
            inv_total = 1.0 / total
            inv_ref[0, sc] = jnp.broadcast_to(inv_total, (8, PAIR_LANES))
            return inv_total

        nxt = weights(0)
        for sc in range(SUPERS_PER_TILE):
            rows = slice(sc * SUPER, (sc + 1) * SUPER)
            win = _window(sc)
            inv_total = nxt
            if sc + 1 < SUPERS_PER_TILE:
                nxt = weights(sc + 1)
            a = _own_head_rows(_dot(vband_t[:, win], e_ref[0, sc]) * inv_total)
            a_ref[rows] = a.astype(BF16)
            g = ag_ref[rows].astype(F32)
            ya_ref[rows] = (a * (g * _sigmoid(g))).astype(BF16)

    blk = pl.BlockSpec((TA, GROUP), lambda p, i: (i, p))

    def cols(first):
        return pl.BlockSpec((TA, GROUP), lambda p, i: (i, first + p))

    return pl.pallas_call(
        body,
        name="attn_fwd",
        grid=(N_PAIR, n_tiles),
        in_specs=[cols(Q_BLOCK), cols(K_BLOCK), cols(V_BLOCK), cols(AG_BLOCK),
                  pl.BlockSpec((1, WINDOW, PAIR_LANES), lambda p, i: (p, 0, 0))],
        out_specs=[blk, blk,
                   pl.BlockSpec((1, SUPERS_PER_TILE, WINDOW, PAIR_LANES), lambda p, i: (p, i, 0, 0)),
                   pl.BlockSpec((1, SUPERS_PER_TILE, 8, PAIR_LANES), lambda p, i: (p, i, 0, 0))],
        out_shape=[jax.ShapeDtypeStruct((seq, ATTN_WIDTH), BF16), jax.ShapeDtypeStruct((seq, ATTN_WIDTH), BF16),
                   jax.ShapeDtypeStruct((N_PAIR, seq // SUPER, WINDOW, PAIR_LANES), BF16),
                   jax.ShapeDtypeStruct((N_PAIR, seq // SUPER, 8, PAIR_LANES), F32)],
        scratch_shapes=[pltpu.VMEM((2 * TA, 2 * GROUP), BF16), pltpu.VMEM((GROUP, 2 * TA), BF16)],
        compiler_params=pltpu.CompilerParams(
            dimension_semantics=("arbitrary", "arbitrary"), vmem_limit_bytes=VMEM_LIMIT),
    )(proj_bf, proj_bf, proj_bf, proj_bf, bias_t)


def _attn_bwd(proj_bf, a, dy, e_all, inv_all):
    seq = proj_bf.shape[0]
    n_tiles = seq // TA

    def body(q_ref, k_ref, v_ref, a_ref, ag_ref, dy_ref, e_ref, inv_ref,
             dq_ref, dk_ref, dv_ref, dag_ref, db_ref, vband, kband_t, dkacc, dvacc):
        i = pl.program_id(1)

        @pl.when(i == 0)
        def _():
            dkacc[...] = jnp.zeros_like(dkacc)
            dvacc[...] = jnp.zeros_like(dvacc)
            db_ref[...] = jnp.zeros_like(db_ref)

        @pl.when(i < n_tiles)
        def _():
            _shift_band(i, vband, v_ref)
            _shift_band_t(i, kband_t, k_ref)

            def score_grads(sc):
                rows = slice(sc * SUPER, (sc + 1) * SUPER)
                win = _window(sc)
                q_rows = _scaled_queries(q_ref, rows)
                g = ag_ref[rows].astype(F32)
                sg = _sigmoid(g)
                dyc = dy_ref[rows].astype(F32)
                dag_ref[rows] = (dyc * a_ref[rows].astype(F32) * (sg * (1.0 + g * (1.0 - sg)))).astype(BF16)
                da_rows = _by_head((dyc * (g * sg)).astype(BF16))
                p = e_ref[0, sc].astype(F32) * inv_ref[0, sc, :1]
                dp = _dot_t(vband[win], da_rows)
                ds = p * (dp - jnp.sum(p * dp, axis=0, keepdims=True))
                db_ref[0] += ds
                return q_rows, da_rows, p.astype(BF16), ds.astype(BF16)

            nxt = score_grads(0)
            for sc in range(SUPERS_PER_TILE):
                rows = slice(sc * SUPER, (sc + 1) * SUPER)
                win = _window(sc)
                q_rows, da_rows, p_bf, ds_bf = nxt
                if sc + 1 < SUPERS_PER_TILE:
                    nxt = score_grads(sc + 1)
                dq_ref[rows] = (_own_head_rows(_dot(kband_t[:, win], ds_bf)) * SCALE).astype(BF16)
                dkacc[win] += _dot(ds_bf, q_rows)
                dvacc[win] += _dot(p_bf, da_rows)

        dk_ref[...] = dkacc[:TA].astype(BF16)
        dv_ref[...] = dvacc[:TA].astype(BF16)
        dkacc[:TA] = dkacc[TA:]
        dvacc[:TA] = dvacc[TA:]
        dkacc[TA:] = jnp.zeros((TA, GROUP), F32)
        dvacc[TA:] = jnp.zeros((TA, GROUP), F32)

    last = n_tiles - 1
    cur = pl.BlockSpec((TA, GROUP), lambda p, i: (jnp.minimum(i, last), p))
    older = pl.BlockSpec((TA, GROUP), lambda p, i: (jnp.maximum(i - 1, 0), p))
    dy_blk = pl.BlockSpec((TA, GROUP), lambda p, i: (jnp.minimum(i, last), N_PAIR + p))
    per_pair = pl.BlockSpec((1, WINDOW, PAIR_LANES), lambda p, i: (p, 0, 0))

    def cols(first):
        return pl.BlockSpec((TA, GROUP), lambda p, i: (jnp.minimum(i, last), first + p))

    def kept(rows):
        return pl.BlockSpec((1, SUPERS_PER_TILE, rows, PAIR_LANES), lambda p, i: (p, jnp.minimum(i, last), 0, 0))

    def out(dtype):
        return jax.ShapeDtypeStruct((seq, ATTN_WIDTH), dtype)

    return pl.pallas_call(
        body,
        name="attn_bwd",
        grid=(N_PAIR, n_tiles + 1),
        in_specs=[cols(Q_BLOCK), cols(K_BLOCK), cols(V_BLOCK), cur, cols(AG_BLOCK), dy_blk, kept(WINDOW), kept(8)],
        out_specs=[cur, older, older, cur, per_pair],
        out_shape=[out(BF16), out(BF16), out(BF16), out(BF16),
                   jax.ShapeDtypeStruct((N_PAIR, WINDOW, PAIR_LANES), F32)],
        scratch_shapes=[
            pltpu.VMEM((2 * TA, GROUP), BF16), pltpu.VMEM((GROUP, 2 * TA), BF16),
            pltpu.VMEM((2 * TA, GROUP), F32), pltpu.VMEM((2 * TA, GROUP), F32)],
        compiler_params=pltpu.CompilerParams(
            dimension_semantics=("arbitrary", "arbitrary"), vmem_limit_bytes=VMEM_LIMIT),
    )(proj_bf, proj_bf, proj_bf, a, proj_bf, dy, e_all, inv_all)


BIN_ROWS = 136


def _bias_bins(db_t):
    idx_t = jnp.asarray(_rel_index_tile())

    def body(db_ref, idx_ref, out_ref):
        lane = lax.broadcasted_iota(jnp.int32, (1, GROUP), 1)
        row = lax.broadcasted_iota(jnp.int32, (BIN_ROWS, GROUP), 0)
        out = jnp.zeros((BIN_ROWS, GROUP), F32)
        for r in range(N_REL - 1):
            lo = 0 if r == 0 else ((BAND - 2 * CHUNK + r) // 8) * 8
            hi = WINDOW if r == 0 else min(WINDOW, lo + SUPER + 8)
            hit = jnp.where(idx_ref[lo:hi] == r, db_ref[0, lo:hi], 0.0)
            col = jnp.sum(hit, axis=0, keepdims=True)
            s0 = jnp.sum(col[:, :SUPER], axis=1, keepdims=True)
            s1 = jnp.sum(col[:, SUPER:], axis=1, keepdims=True)
            val = jnp.where(lane == 0, s0, jnp.where(lane == 1, s1, 0.0))
            out = jnp.where(row == r, val, out)
        out_ref[0] = out

    return pl.pallas_call(
        body,
        name="bias_bins",
        grid=(N_PAIR,),
        in_specs=[pl.BlockSpec((1, WINDOW, PAIR_LANES), lambda p: (p, 0, 0)), VMEM],
        out_specs=pl.BlockSpec((1, BIN_ROWS, GROUP), lambda p: (p, 0, 0)),
        out_shape=jax.ShapeDtypeStruct((N_PAIR, BIN_ROWS, GROUP), F32),
        compiler_params=pltpu.CompilerParams(dimension_semantics=("arbitrary",)),
    )(db_t, idx_t)


def _out_loss(x, tgt, proj, ya, wout_full, g2, pw_bf, ps):
    seq = x.shape[0]
    n_tiles = seq // TS

    def body(x_ref, t_ref, pv_ref, pg_ref, ya_ref, w_ref, g2_ref, pw_ref, ps_ref,
             dx2_ref, dy_ref, gw_ref, gg_ref, loss_ref, sq_ref, halo_ref, wt_ref):
        i = pl.program_id(0)

        @pl.when(i == 0)
        def _():
            gw_ref[...] = jnp.zeros_like(gw_ref)
            gg_ref[...] = jnp.zeros_like(gg_ref)
            sq_ref[...] = jnp.zeros_like(sq_ref)
            halo_ref[...] = jnp.zeros_like(halo_ref)
            for b in range(N_CHIP):
                wt_ref[b] = w_ref[b].T

        pv = pv_ref[...].astype(F32)
        pg = pg_ref[...].astype(F32)
        diffs = [d.astype(BF16) for d in _pool_diffs(pv, halo_ref[...], i)]
        halo_ref[...] = pv[TS - HALO:, :]
        yp = ((_pool_mix(diffs, pw_ref) * ps_ref[...]) * (pg * _sigmoid(pg))).astype(BF16)
        g2v = g2_ref[...]

        def parts(rows):
            return [yp[rows, :SHARD_OUT], yp[rows, SHARD_OUT:], ya_ref[rows, :SHARD_OUT], ya_ref[rows, SHARD_OUT:]]

        def project(rows, ys):
            x2 = x_ref[rows]
            for b in range(N_CHIP):
                x2 = x2 + _dot(ys[b], w_ref[b])
            return x2

        def norm_loss(rows, x2):
            r = lax.rsqrt(jnp.mean(x2 * x2, axis=-1, keepdims=True) + EPS)
            xh = x2 * r
            diff = xh * g2v - t_ref[rows]
            sq_ref[...] += jnp.sum(diff * diff, axis=0, keepdims=True)
            dfin = diff * (1.0 / D_MODEL)
            gg_ref[...] += jnp.sum(dfin * xh, axis=0, keepdims=True)
            dxh = dfin * g2v
            dx2 = r * (dxh - xh * jnp.mean(dxh * xh, axis=-1, keepdims=True))
            dx2_ref[rows] = dx2
            return dx2.astype(BF16)

        def back(rows, ys, dx2_bf):
            for b in range(N_CHIP):
                gw_ref[b] += _tdot(ys[b], dx2_bf)
                dy_ref[rows, b * SHARD_OUT:(b + 1) * SHARD_OUT] = _dot(dx2_bf, wt_ref[b]).astype(BF16)

        n_parts = 2
        part = TS // n_parts
        spans = [slice(r * part, (r + 1) * part) for r in range(n_parts)]
        ys = [parts(rows) for rows in spans]
        x2_next = project(spans[0], ys[0])
        for r in range(n_parts):
            x2 = x2_next
            if r + 1 < n_parts:
                x2_next = project(spans[r + 1], ys[r + 1])
            back(spans[r], ys[r], norm_loss(spans[r], x2))

        @pl.when(i == n_tiles - 1)
        def _():
            total = jnp.sum(sq_ref[...], axis=1, keepdims=True) * (0.5 / D_MODEL)
            loss_ref[...] = jnp.broadcast_to(total, loss_ref.shape)

    def rows(width, col=0):
        return pl.BlockSpec((TS, width), lambda i: (i, col))

    return pl.pallas_call(
        body,
        name="out_loss",
        grid=(n_tiles,),
        in_specs=[rows(D_MODEL), rows(D_MODEL), rows(POOL_WIDTH, 0), rows(POOL_WIDTH, 1), rows(ATTN_WIDTH),
                  VMEM, VMEM, VMEM, VMEM],
        out_specs=[rows(D_MODEL), rows(D_MODEL), VMEM, VMEM, VMEM],
        out_shape=[
            jax.ShapeDtypeStruct((seq, D_MODEL), F32), jax.ShapeDtypeStruct((seq, D_MODEL), BF16),
            jax.ShapeDtypeStruct((N_CHIP, SHARD_OUT, D_MODEL), F32), jax.ShapeDtypeStruct((1, D_MODEL), F32),
            jax.ShapeDtypeStruct((8, GROUP), F32)],
        scratch_shapes=[pltpu.VMEM((1, D_MODEL), F32), pltpu.VMEM((HALO, POOL_WIDTH), F32),
                        pltpu.VMEM((N_CHIP, D_MODEL, SHARD_OUT), BF16)],
        compiler_params=pltpu.CompilerParams(dimension_semantics=("arbitrary",), vmem_limit_bytes=VMEM_LIMIT),
    )(x, tgt, proj, proj, ya, wout_full, g2, pw_bf, ps)


def _inproj_bwd(x, dx2, dy, proj, dq, dk, dv, dag, win_full, pw_bf, g1, ps):
    seq = x.shape[0]
    n_tiles = seq // TS

    def body(x_ref, dx2_ref, dyp_ref, pv_ref, pvprev_ref, pg_ref, dq_ref, dk_ref, dv_ref, dag_ref,
             w_ref, pw_ref, g1_ref, ps_ref, gx_ref, dproj_ref, ht_ref, gg_ref, gps_ref, gpw_ref, halo_ref):
        i = pl.program_id(0)
        tile = n_tiles - 1 - i

        @pl.when(i == 0)
        def _():
            gg_ref[...] = jnp.zeros_like(gg_ref)
            gps_ref[...] = jnp.zeros_like(gps_ref)
            gpw_ref[...] = jnp.zeros_like(gpw_ref)
            halo_ref[...] = jnp.zeros_like(halo_ref)

        pv_t = pv_ref[...].astype(F32)
        pg_t = pg_ref[...].astype(F32)
        prev_rows = jnp.where(tile > 0, pvprev_ref[...].astype(F32), 0.0)
        diffs = [d.astype(BF16) for d in _pool_diffs(pv_t, prev_rows, tile)]
        mixed = _pool_mix(diffs, pw_ref)
        sg = _sigmoid(pg_t)
        silu = pg_t * sg
        dyp = dyp_ref[...].astype(F32)
        psv = ps_ref[...]
        gps_ref[...] += jnp.sum(dyp * mixed * silu, axis=0, keepdims=True)
        dmixed = (dyp * psv * silu).astype(BF16)
        dpg = dyp * (mixed * psv) * (sg * (1.0 + pg_t * (1.0 - sg)))
        counts = _row_counts(tile, TS)
        dds = []
        for g in range(len(POOL_WINDOWS)):
            dm_g = dmixed[:, g * GROUP:(g + 1) * GROUP]
            gpw_ref[g] += _tdot(diffs[g], dm_g)
            dds.append(_dot_t(dm_g, pw_ref[g]))
        dd = jnp.concatenate(dds, axis=1)
        spread = jnp.concatenate([dds[g] / counts[g] for g in range(len(POOL_WINDOWS))], axis=1)
        sums = _window_sums(jnp.concatenate([spread, halo_ref[...]], axis=0), forward=True)
        halo_ref[...] = spread[:HALO]
        dpv = jnp.concatenate(
            [sums[g][:TS, g * GROUP:(g + 1) * GROUP] for g in range(len(POOL_WINDOWS))], axis=1) - dd

        xt = x_ref[...]
        r = lax.rsqrt(jnp.mean(xt * xt, axis=-1, keepdims=True) + EPS)
        xh = xt * r
        g1v = g1_ref[...]
        ht_ref[...] = (xh * g1v).astype(BF16).T
        dproj = jnp.concatenate(
            [dpv.astype(BF16), dpg.astype(BF16), dq_ref[...], dk_ref[...], dv_ref[...], dag_ref[...]],
            axis=1)
        dproj_ref[...] = dproj
        dh = _dot_t(dproj[:, :SHARD_IN], w_ref[0])
        for chip in range(1, N_CHIP):
            dh = dh + _dot_t(dproj[:, chip * SHARD_IN:(chip + 1) * SHARD_IN], w_ref[chip])

        gg_ref[...] += jnp.sum(dh * xh, axis=0, keepdims=True)
        dxh = dh * g1v
        gx_ref[...] = dx2_ref[...] + r * (dxh - xh * jnp.mean(dxh * xh, axis=-1, keepdims=True))

    def rows(width, col=0):
        return pl.BlockSpec((TS, width), lambda i: (n_tiles - 1 - i, col))

    prev = pl.BlockSpec((HALO, POOL_WIDTH), lambda i: (jnp.maximum((n_tiles - 1 - i) * (TS // HALO) - 1, 0), 0))
    return pl.pallas_call(
        body,
        name="inproj_bwd",
        grid=(n_tiles,),
        in_specs=[rows(D_MODEL), rows(D_MODEL), rows(POOL_WIDTH), rows(POOL_WIDTH), prev, rows(POOL_WIDTH, 1),
                  rows(ATTN_WIDTH), rows(ATTN_WIDTH), rows(ATTN_WIDTH), rows(ATTN_WIDTH), VMEM, VMEM, VMEM, VMEM],
        out_specs=[rows(D_MODEL), rows(N_CHIP * SHARD_IN),
                   pl.BlockSpec((D_MODEL, TS), lambda i: (0, n_tiles - 1 - i)), VMEM, VMEM, VMEM],
        out_shape=[
            jax.ShapeDtypeStruct((seq, D_MODEL), F32),
            jax.ShapeDtypeStruct((seq, N_CHIP * SHARD_IN), BF16),
            jax.ShapeDtypeStruct((D_MODEL, seq), BF16),
            jax.ShapeDtypeStruct((1, D_MODEL), F32),
            jax.ShapeDtypeStruct((1, POOL_WIDTH), F32),
            jax.ShapeDtypeStruct((len(POOL_WINDOWS), GROUP, GROUP), F32)],
        scratch_shapes=[pltpu.VMEM((HALO, POOL_WIDTH), F32)],
        compiler_params=pltpu.CompilerParams(dimension_semantics=("arbitrary",), vmem_limit_bytes=VMEM_LIMIT),
    )(x, dx2, dy, proj, proj, proj, dq, dk, dv, dag, win_full, pw_bf, g1, ps)


PASS_PEER = (2, 0, 1)


def _gw_reduce(ht, dproj, gwout, small):
    seq = ht.shape[1]
    tm = min(2 * TS, seq // 8)
    n_tiles = seq // tm
    half_small = SMALL_ROWS // 2
    cx, cy = lax.axis_index("x"), lax.axis_index("y")
    outer = _other_chips(cx, cy)
    order = jnp.stack([outer[n][1] for n in PASS_PEER] + [2 * cx + cy]).astype(jnp.int32)

    def body(order_ref, ht_ref, dp_ref, gwout_ref, small_ref, gin_final, gout_final, small_final,
             hbuf, acc, pair_in, pair_out, pair_small, tx_in, tx_out, rx_in, rx_out, rx_small,
             gin_out, gout_out, small_out, send_sems, recv_sems):
        j = pl.program_id(0)
        i = pl.program_id(1)
        x, y, c = _my_place()
        b = 2 * x + y
        sibling = (x, y, 1 - c)
        others = _other_chips(x, y)
        mine_in = pl.ds(pl.multiple_of(c * HALF_IN, HALF_IN), HALF_IN)
        mine_out = pl.ds(pl.multiple_of(c * HALF_OUT, HALF_OUT), HALF_OUT)
        mine_small = pl.ds(pl.multiple_of(c * half_small, 8), half_small)
        theirs_in = pl.ds(pl.multiple_of((1 - c) * HALF_IN, HALF_IN), HALF_IN)
        theirs_out = pl.ds(pl.multiple_of((1 - c) * HALF_OUT, HALF_OUT), HALF_OUT)
        theirs_small = pl.ds(pl.multiple_of((1 - c) * half_small, 8), half_small)

        def copy(k, src, dst, to):
            return pltpu.make_async_remote_copy(
                src_ref=src, dst_ref=dst, send_sem=send_sems.at[k], recv_sem=recv_sems.at[k],
                device_id=to, device_id_type=MESH)

        swap_out = copy(0, gwout_ref.at[:, theirs_out], pair_out, sibling)
        swap_small = copy(1, small_ref, pair_small, sibling)

        def swap_in(p):
            return copy(2 + p, acc.at[p % 2, theirs_in], pair_in.at[p % 2], sibling)

        def to_chip(n, t):
            to = (*others[n][0], c)
            if t == 0:
                return copy(6 + 3 * n, tx_in.at[n], rx_in.at[n], to)
            if t == 1:
                return copy(7 + 3 * n, tx_out.at[n], rx_out.at[n], to)
            return copy(8 + 3 * n, pair_small.at[mine_small], rx_small.at[b], to)

        share_in = copy(15, gin_out.at[mine_in], gin_out.at[mine_in], sibling)
        share_out = copy(16, gout_out.at[mine_out], gout_out.at[mine_out], sibling)
        share_small = copy(17, small_out.at[mine_small], small_out.at[mine_small], sibling)

        def at(jj, ii):
            return (j == jj) & (i == ii)

        par = j % 2

        @pl.when(i == 0)
        def _():
            acc[par] = jnp.zeros((D_MODEL, SHARD_IN), F32)

        cols = pl.ds(pl.multiple_of(i * tm, tm), tm)

        @pl.when(j == 0)
        def _():
            hbuf[:, cols] = ht_ref[...]

        acc[par] += _dot(hbuf[:, cols], dp_ref[...])

        @pl.when(at(0, 0))
        def _():
            swap_out.start()
            swap_small.start()

        @pl.when(at(0, 2))
        def _():
            swap_out.wait_recv()
            swap_small.wait_recv()
            for chip in range(N_CHIP):
                pair_out[chip] = gwout_ref[chip, mine_out] + pair_out[chip]
            pair_small[...] = small_ref[...] + pair_small[...]
            rx_small[b] = pair_small[mine_small]
            for n in range(N_CHIP - 1):
                tx_out[n] = pair_out[others[n][1]].astype(BF16)
                to_chip(n, 1).start()
                to_chip(n, 2).start()

        @pl.when(at(1, 4))
        def _():
            total_out = pair_out[b]
            for n in range(N_CHIP - 1):
                to_chip(n, 1).wait_recv()
                copy(8 + 3 * n, pair_small.at[mine_small], rx_small.at[others[n][1]], (*others[n][0], c)).wait_recv()
                total_out = total_out + rx_out[n].astype(F32)
            gout_out[mine_out] = total_out
            small_out[mine_small] = ((rx_small[0] + rx_small[1]) + rx_small[2]) + rx_small[3]
            share_out.start()
            share_small.start()

        for p in range(N_CHIP - 1):
            n = PASS_PEER[p]

            @pl.when(at(p + 1, 0))
            def _(p=p):
                swap_in(p).start()

            @pl.when(at(p + 1, 2))
            def _(p=p, n=n):
                swap_in(p).wait_recv()
                swap_in(p).wait_send()
                tx_in[n] = (acc[p % 2, mine_in] + pair_in[p % 2]).astype(BF16)
                to_chip(n, 0).start()

        @pl.when(at(N_CHIP - 1, n_tiles - 1))
        def _():
            last = N_CHIP - 1
            swap_in(last).start()
            swap_in(last).wait_recv()
            total_in = acc[last % 2, mine_in] + pair_in[last % 2]
            for n in range(N_CHIP - 1):
                to_chip(n, 0).wait_recv()
                total_in = total_in + rx_in[n].astype(F32)
            gin_out[mine_in] = total_in
            share_in.start()
            copy(15, gin_out.at[theirs_in], gin_out.at[theirs_in], sibling).wait_recv()
            copy(16, gout_out.at[theirs_out], gout_out.at[theirs_out], sibling).wait_recv()
            copy(17, small_out.at[theirs_small], small_out.at[theirs_small], sibling).wait_recv()
            swap_out.wait_send()
            swap_small.wait_send()
            swap_in(last).wait_send()
            for n in range(N_CHIP - 1):
                for t in range(3):
                    to_chip(n, t).wait_send()
            share_in.wait_send()
            share_out.wait_send()
            share_small.wait_send()
            gin_final[...] = gin_out[...]
            gout_final[...] = gout_out[...]
            small_final[...] = small_out[...]

    assert n_tiles >= 5, "the reduction's steps are spread over the first five token steps of a pass"
    grid_spec = pltpu.PrefetchScalarGridSpec(
        num_scalar_prefetch=1,
        grid=(N_CHIP, n_tiles),
        in_specs=[
            pl.BlockSpec((D_MODEL, tm), lambda j, i, o: (0, jnp.where(j == 0, i, n_tiles - 1))),
            pl.BlockSpec((tm, SHARD_IN), lambda j, i, o: (i, o[j])),
            VMEM, VMEM],
        out_specs=[VMEM, VMEM, VMEM],
        scratch_shapes=[
            pltpu.VMEM((D_MODEL, seq), BF16),
            pltpu.VMEM((2, D_MODEL, SHARD_IN), F32),
            pltpu.VMEM((2, HALF_IN, SHARD_IN), F32),
            pltpu.VMEM((N_CHIP, HALF_OUT, D_MODEL), F32),
            pltpu.VMEM((SMALL_ROWS, GROUP), F32),
            pltpu.VMEM((N_CHIP - 1, HALF_IN, SHARD_IN), BF16),
            pltpu.VMEM((N_CHIP - 1, HALF_OUT, D_MODEL), BF16),
            pltpu.VMEM((N_CHIP - 1, HALF_IN, SHARD_IN), BF16),
            pltpu.VMEM((N_CHIP - 1, HALF_OUT, D_MODEL), BF16),
            pltpu.VMEM((N_CHIP, half_small, GROUP), F32),
            pltpu.VMEM((D_MODEL, SHARD_IN), F32),
            pltpu.VMEM((SHARD_OUT, D_MODEL), F32),
            pltpu.VMEM((SMALL_ROWS, GROUP), F32),
            pltpu.SemaphoreType.DMA((18,)),
            pltpu.SemaphoreType.DMA((18,)),
        ],
    )
    return pl.pallas_call(
        body,
        name="gw_reduce",
        grid_spec=grid_spec,
        out_shape=(
            jax.ShapeDtypeStruct((D_MODEL, SHARD_IN), F32),
            jax.ShapeDtypeStruct((SHARD_OUT, D_MODEL), F32),
            jax.ShapeDtypeStruct((SMALL_ROWS, GROUP), F32),
        ),
        compiler_params=pltpu.CompilerParams(
            dimension_semantics=("arbitrary", "arbitrary"), vmem_limit_bytes=VMEM_LIMIT),
    )(order, ht, dproj, gwout, small)


def _adamw(name, w, g, m, v, block_rows):
    rows, cols = w.shape

    def body(w_ref, g_ref, m_ref, v_ref, d_ref, m_out, v_out):
        grad = g_ref[...]
        m_new = ADAM_B1 * m_ref[...] + (1.0 - ADAM_B1) * grad
        v_new = ADAM_B2 * v_ref[...] + (1.0 - ADAM_B2) * (grad * grad)
        m_hat = m_new / (1.0 - ADAM_B1 ** ADAM_STEP)
        v_hat = v_new / (1.0 - ADAM_B2 ** ADAM_STEP)
        d_ref[...] = -ADAM_LR * (m_hat / (jnp.sqrt(v_hat) + ADAM_EPS) + ADAM_WD * w_ref[...])
        m_out[...] = m_new
        v_out[...] = v_new

    blk = pl.BlockSpec((block_rows, cols), lambda i: (i, 0))
    shape = jax.ShapeDtypeStruct((rows, cols), F32)
    return pl.pallas_call(
        body,
        name=name,
        grid=(rows // block_rows,),
        in_specs=[blk] * 4,
        out_specs=[blk] * 3,
        out_shape=[shape] * 3,
        compiler_params=pltpu.CompilerParams(dimension_semantics=("arbitrary",)),
    )(w, g, m, v)


def _pack_small(norm_gain, pool_w, pool_scale, rel_bias, final_gain, loss_rows):
    parts = [
        norm_gain.reshape(8, GROUP),
        pool_w.reshape(len(POOL_WINDOWS) * GROUP, GROUP),
        jnp.pad(pool_scale.reshape(4, GROUP), ((0, 4), (0, 0))),
        jnp.pad(rel_bias.reshape(8, N_REL), ((0, 0), (0, 2 * GROUP - N_REL))).reshape(16, GROUP),
        final_gain.reshape(8, GROUP),
        loss_rows,
    ]
    return jnp.concatenate(parts, axis=0)


def _unpack_small(block):
    norm_gain = block[0:8].reshape(1, D_MODEL)
    pool_w = block[8:520].reshape(1, len(POOL_WINDOWS), GROUP, GROUP)
    pool_scale = block[520:524].reshape(1, POOL_WIDTH)
    rel_bias = block[528:544].reshape(8, 2 * GROUP)[:, :N_REL].reshape(1, 8, N_REL)
    final_gain = block[544:552].reshape(D_MODEL)
    return norm_gain, pool_w, pool_scale, rel_bias, final_gain


def kernel(x, norm_gain, w_in, pool_w, pool_scale, rel_bias, w_out, final_norm_gain, loss_target, m_norm_gain, m_w_in, m_pool_w, m_pool_scale, m_rel_bias, m_w_out, m_final_norm_gain, v_norm_gain, v_w_in, v_pool_w, v_pool_scale, v_rel_bias, v_w_out, v_final_norm_gain):
    assert x.shape[1] % TS == 0 and x.shape[2] == D_MODEL
    xs = x[0]
    tgt = loss_target[0]
    g1 = norm_gain.reshape(1, D_MODEL)
    g2 = final_norm_gain.reshape(1, D_MODEL)
    ps = pool_scale.reshape(1, POOL_WIDTH)
    pw_bf = pool_w[0].astype(BF16)

    proj, win_full, wout_full = _gather_inproj(xs, g1, w_in[0].astype(BF16), w_out[0].astype(BF16))
    bias_t = _bias_tile(rel_bias[0])

    a, ya, e_all, inv_all = _attn_fwd(proj, bias_t)
    dx2, dy, gwout, gg2, loss_rows = _out_loss(xs, tgt, proj, ya, wout_full, g2, pw_bf, ps)
    dq, dk, dv, dag, db_t = _attn_bwd(proj, a, dy, e_all, inv_all)
    bins = _bias_bins(db_t)
    gx, dproj, ht, gg1, gps, gpw = _inproj_bwd(xs, dx2, dy, proj, dq, dk, dv, dag, win_full, pw_bf, g1, ps)

    g_bias = bins[:, :N_REL, :2].transpose(0, 2, 1).reshape(8, N_REL)
    small = _pack_small(gg1, gpw, gps, g_bias, gg2, loss_rows)
    g_win, g_wout, g_small = _gw_reduce(ht, dproj, gwout, small)
    loss = g_small[LOSS_ROW, 0]

    zeros8 = jnp.zeros((8, GROUP), F32)
    w_small = _pack_small(norm_gain, pool_w, pool_scale, rel_bias, final_norm_gain, zeros8)
    m_small = _pack_small(m_norm_gain, m_pool_w, m_pool_scale, m_rel_bias, m_final_norm_gain, zeros8)
    v_small = _pack_small(v_norm_gain, v_pool_w, v_pool_scale, v_rel_bias, v_final_norm_gain, zeros8)

    d_win, m_win, v_win = _adamw("adamw_w_in", w_in[0], g_win, m_w_in[0], v_w_in[0], 256)
    d_wout, m_wout, v_wout = _adamw("adamw_w_out", w_out[0], g_wout, m_w_out[0], v_w_out[0], 128)
    d_small, m_new_small, v_new_small = _adamw("adamw_small", w_small, g_small, m_small, v_small, SMALL_ROWS // 2)

    def full(win_part, wout_part, block):
        ng, pw, psc, rb, fg = _unpack_small(block)
        return [ng, win_part[None], pw, psc, rb, wout_part[None], fg]

    grads = full(g_win, g_wout, g_small)
    deltas = full(d_win, d_wout, d_small)
    new_m = full(m_win, m_wout, m_new_small)
    new_v = full(v_win, v_wout, v_new_small)
    return (loss, gx[None], *grads, *deltas, *new_m, *new_v)
```

```python
import numpy as np
import jax
import jax.numpy as jnp
from jax import lax
from jax.experimental import pallas as pl
from jax.experimental.pallas import tpu as pltpu

F32 = jnp.float32
BF16 = jnp.bfloat16

D_MODEL = 1024
POOL_WIDTH = 512
ATTN_WIDTH = 512
POOL_WINDOWS = (2, 4, 8, 16)
GROUP = 128
CHUNK = 64
LEFT_CHUNKS = 8
BAND = (LEFT_CHUNKS + 1) * CHUNK
HEAD_DIM = 64
N_PAIR = 4
MAX_REL = 64
N_REL = 2 * MAX_REL + 1
EPS = 1e-6
MASK_VALUE = -1e30
SCALE = 0.125

ADAM_LR = 0.001
ADAM_B1 = 0.9
ADAM_B2 = 0.999
ADAM_EPS = 1e-08
ADAM_WD = 0.01
ADAM_STEP = 10

TS = LEFT_CHUNKS * CHUNK
SUPER = 2 * CHUNK
WINDOW = BAND + CHUNK
TA = 4 * TS
WIN_BASE = TA - LEFT_CHUNKS * CHUNK
SUPERS_PER_TILE = TA // SUPER
PAIR_LANES = 2 * SUPER
HALO = 16
N_CHIP = 4
SHARD_IN = 768
SHARD_OUT = 256
PIECE = 256
PIECES_PER_SHARD = SHARD_IN // PIECE
HALF_IN = D_MODEL // 2
HALF_OUT = SHARD_OUT // 2
SMALL_ROWS = 560
LOSS_ROW = 552
VMEM_LIMIT = 60 * 1024 * 1024

MESH = pl.DeviceIdType.MESH
ANY = pl.BlockSpec(memory_space=pl.ANY)
VMEM = pl.BlockSpec(memory_space=pltpu.VMEM)


def _sigmoid(x):
    return 1.0 / (1.0 + jnp.exp(-x))


def _dot(a, b):
    return jnp.dot(a, b, preferred_element_type=F32)


def _dot_t(a, b):
    return lax.dot_general(a, b, (((1,), (1,)), ((), ())), preferred_element_type=F32)


def _tdot(a, b):
    return lax.dot_general(a, b, (((0,), (0,)), ((), ())), preferred_element_type=F32)


def _my_place():
    return lax.axis_index("x"), lax.axis_index("y"), lax.axis_index("c")


def _other_chips(x, y):
    places = [(1 - x, y), (x, 1 - y), (1 - x, 1 - y)]
    return [(p, 2 * p[0] + p[1]) for p in places]


def _gather_inproj(x, g1, win_sh, wout_sh):
    seq = x.shape[0]
    tm = min(seq, 4 * TS)
    n_tiles = seq // tm
    cx, cy = lax.axis_index("x"), lax.axis_index("y")
    order = jnp.stack([2 * cx + cy] + [chip for _, chip in _other_chips(cx, cy)]).astype(jnp.int32)

    def body(order_ref, x_ref, g1_ref, win_ref, wout_ref, proj_bf_ref, win_full, wout_full,
             hbuf, wbuf, send_sems, recv_sems, local_sems):
        j = pl.program_id(0)
        i = pl.program_id(1)
        x_, y_, c = _my_place()
        b = 2 * x_ + y_
        sibling = (x_, y_, 1 - c)
        others = _other_chips(x_, y_)

        def halves(chip, core):
            return (
                win_full.at[chip, pl.ds(core * HALF_IN, HALF_IN)],
                wout_full.at[chip, pl.ds(core * HALF_OUT, HALF_OUT)],
            )

        def copy(k, src, dst, to):
            return pltpu.make_async_remote_copy(
                src_ref=src, dst_ref=dst, send_sem=send_sems.at[k], recv_sem=recv_sems.at[k],
                device_id=to, device_id_type=MESH)

        own = [
            pltpu.make_async_copy(win_ref, win_full.at[b], local_sems.at[0]),
            pltpu.make_async_copy(wout_ref, wout_full.at[b], local_sems.at[1]),
        ]
        mine_src = (win_ref.at[pl.ds(c * HALF_IN, HALF_IN)], wout_ref.at[pl.ds(c * HALF_OUT, HALF_OUT)])

        def direct(n, t):
            return copy(2 * n + t, mine_src[t], halves(b, c)[t], (*others[n][0], c))

        def arrival(n, t):
            landed = halves(others[n][1], c)[t]
            return copy(2 * n + t, landed, landed, (*others[n][0], c))

        def passing(n, t):
            landed = halves(others[n][1], c)[t]
            return copy(6 + 2 * n + t, landed, landed, sibling)

        def from_sibling(n, t):
            landed = halves(others[n][1], 1 - c)[t]
            return copy(6 + 2 * n + t, landed, landed, sibling)

        @pl.when((j == 0) & (i == 0))
        def _():
            first = pltpu.make_async_copy(win_ref, wbuf, local_sems.at[2])
            first.start()
            for cp in own:
                cp.start()
            for t in range(2):
                for n in range(N_CHIP - 1):
                    direct(n, t).start()
            first.wait()

        for n in range(N_CHIP - 1):
            @pl.when((j == n + 1) & (i == 0))
            def _(n=n):
                arrival(n, 0).wait_recv()
                passing(n, 0).start()
                from_sibling(n, 0).wait_recv()
                load = pltpu.make_async_copy(win_full.at[others[n][1]], wbuf, local_sems.at[2])
                load.start()
                load.wait()

        rows = pl.ds(pl.multiple_of(i * tm, tm), tm)

        @pl.when(j == 0)
        def _():
            xt = x_ref[...]
            r = lax.rsqrt(jnp.mean(xt * xt, axis=-1, keepdims=True) + EPS)
            hbuf[rows] = ((xt * r) * g1_ref[...]).astype(BF16)

        proj_bf_ref[...] = _dot(hbuf[rows], wbuf[...]).astype(BF16)

        @pl.when((j == N_CHIP - 1) & (i == n_tiles - 1))
        def _():
            for n in range(N_CHIP - 1):
                arrival(n, 1).wait_recv()
                passing(n, 1).start()
            for n in range(N_CHIP - 1):
                from_sibling(n, 1).wait_recv()
            for n in range(N_CHIP - 1):
                for t in range(2):
                    direct(n, t).wait_send()
                    passing(n, t).wait_send()
            for cp in own:
                cp.wait()

    last = n_tiles - 1
    grid_spec = pltpu.PrefetchScalarGridSpec(
        num_scalar_prefetch=1,
        grid=(N_CHIP, n_tiles),
        in_specs=[
            pl.BlockSpec((tm, D_MODEL), lambda j, i, o: (jnp.where(j == 0, i, last), 0)),
            VMEM, ANY, ANY],
        out_specs=[pl.BlockSpec((tm, SHARD_IN), lambda j, i, o: (i, o[j])), ANY, ANY],
        scratch_shapes=[
            pltpu.VMEM((seq, D_MODEL), BF16), pltpu.VMEM((D_MODEL, SHARD_IN), BF16),
            pltpu.SemaphoreType.DMA((12,)), pltpu.SemaphoreType.DMA((12,)), pltpu.SemaphoreType.DMA((3,))],
    )
    return pl.pallas_call(
        body,
        name="gather_inproj",
        grid_spec=grid_spec,
        out_shape=(
            jax.ShapeDtypeStruct((seq, N_CHIP * SHARD_IN), BF16),
            jax.ShapeDtypeStruct((N_CHIP, D_MODEL, SHARD_IN), BF16),
            jax.ShapeDtypeStruct((N_CHIP, SHARD_OUT, D_MODEL), BF16),
        ),
        compiler_params=pltpu.CompilerParams(
            dimension_semantics=("arbitrary", "arbitrary"), vmem_limit_bytes=VMEM_LIMIT),
    )(order, x, g1, win_sh, wout_sh)


def _window_sums(ext, forward):
    n = ext.shape[0]
    sums = []
    acc = ext
    for step in (1, 2, 4, 8):
        acc = acc + pltpu.roll(acc, (n - step) if forward else step, 0)
        sums.append(acc)
    return sums


def _row_counts(tile, rows):
    t = tile * rows + lax.broadcasted_iota(jnp.int32, (rows, GROUP), 0)
    return [jnp.minimum(t + 1, w).astype(F32) for w in POOL_WINDOWS]


def _pool_diffs(pv, prev_rows, tile):
    ext = jnp.concatenate([prev_rows, pv], axis=0)
    sums = _window_sums(ext, forward=False)
    counts = _row_counts(tile, pv.shape[0])
    out = []
    for g in range(len(POOL_WINDOWS)):
        cols = slice(g * GROUP, (g + 1) * GROUP)
        out.append(sums[g][HALO:, cols] / counts[g] - pv[:, cols])
    return out


def _pool_mix(diffs, pw_ref):
    return jnp.concatenate([_dot(diffs[g], pw_ref[g]) for g in range(len(POOL_WINDOWS))], axis=1)


Q_BLOCK, K_BLOCK, V_BLOCK, AG_BLOCK = 8, 12, 16, 20


def _bias_tile(rel_bias):
    flat = jnp.concatenate(
        [jnp.broadcast_to(rel_bias[:, :1], (rel_bias.shape[0], BAND - CHUNK - 1)), rel_bias[:, :2 * MAX_REL]], axis=1)
    rows = [flat[:, CHUNK - 1 - i:CHUNK - 1 - i + BAND] for i in range(CHUNK)]
    bias = jnp.stack(rows, axis=1)
    first = jnp.pad(bias, ((0, 0), (0, 0), (0, CHUNK)), constant_values=MASK_VALUE)
    second = jnp.pad(bias, ((0, 0), (0, 0), (CHUNK, 0)), constant_values=MASK_VALUE)
    both = jnp.concatenate([first, second], axis=1)
    return both.reshape(N_PAIR, PAIR_LANES, WINDOW).transpose(0, 2, 1)


def _rel_index_tile():
    j = np.arange(WINDOW)[:, None]
    q = np.arange(PAIR_LANES)[None, :] % SUPER
    band_key = j - CHUNK * (q // CHUNK)
    idx = np.clip(band_key - LEFT_CHUNKS * CHUNK - q % CHUNK, -MAX_REL, MAX_REL) + MAX_REL
    return np.where((band_key >= 0) & (band_key < BAND), idx, -1).astype(np.int32)


def _by_head(block):
    low = lax.broadcasted_iota(jnp.int32, block.shape, 1) < HEAD_DIM
    zero = jnp.zeros_like(block)
    return jnp.concatenate([jnp.where(low, block, zero), jnp.where(low, zero, block)], axis=0)


def _own_head_rows(cross):
    head_of_row = lax.broadcasted_iota(jnp.int32, cross.shape, 0) >= HEAD_DIM
    head_of_lane = lax.broadcasted_iota(jnp.int32, cross.shape, 1) >= SUPER
    both = jnp.where(jnp.logical_xor(head_of_row, head_of_lane), 0.0, cross).T
    return both[:SUPER] + both[SUPER:]


def _with_mask_lane(q_rows):
    lane = lax.broadcasted_iota(jnp.int32, q_rows.shape, 1)
    return jnp.concatenate([q_rows, jnp.where(lane == 0, MASK_VALUE, 0.0).astype(q_rows.dtype)], axis=1)


def _band_exp(kb, q_rows, bias):
    s = _dot_t(kb, _with_mask_lane(q_rows)) + bias
    e = jnp.exp(s - jnp.max(s, axis=0, keepdims=True))
    return e, jnp.sum(e, axis=0, keepdims=True)


def _window(sc):
    return slice(WIN_BASE + sc * SUPER, WIN_BASE + sc * SUPER + WINDOW)


def _shift_band(i, band_ref, new_ref):
    @pl.when(i == 0)
    def _():
        band_ref[:TA] = jnp.zeros((TA, GROUP), band_ref.dtype)

    @pl.when(i > 0)
    def _():
        band_ref[:TA] = band_ref[TA:]

    band_ref[TA:] = new_ref[...].astype(band_ref.dtype)


def _shift_key_band(i, band_ref, new_ref):
    @pl.when(i == 0)
    def _():
        lane = lax.broadcasted_iota(jnp.int32, (TA, 2 * GROUP), 1)
        band_ref[:TA] = jnp.where(lane == GROUP, 1.0, 0.0).astype(band_ref.dtype)
        band_ref[TA:, GROUP:] = jnp.zeros((TA, GROUP), band_ref.dtype)

    @pl.when(i > 0)
    def _():
        band_ref[:TA] = band_ref[TA:]

    band_ref[TA:, :GROUP] = new_ref[...].astype(band_ref.dtype)


def _shift_band_t(i, band_ref, new_ref):
    @pl.when(i == 0)
    def _():
        band_ref[:, :TA] = jnp.zeros((GROUP, TA), band_ref.dtype)

    @pl.when(i > 0)
    def _():
        band_ref[:, :TA] = band_ref[:, TA:]

    band_ref[:, TA:] = new_ref[...].astype(band_ref.dtype).T


def _scaled_queries(q_ref, rows):
    return _by_head((q_ref[rows] * SCALE).astype(BF16))


def _attn_fwd(proj_bf, bias_t):
    seq = proj_bf.shape[0]
    n_tiles = seq // TA

    def body(q_ref, k_ref, v_ref, ag_ref, bias_ref, a_ref, ya_ref, e_ref, inv_ref, kband, vband_t):
        i = pl.program_id(1)
        _shift_key_band(i, kband, k_ref)
        _shift_band_t(i, vband_t, v_ref)

        def weights(sc):
            rows = slice(sc * SUPER, (sc + 1) * SUPER)
            win = _window(sc)
            e, total = _band_exp(kband[win], _scaled_queries(q_ref, rows), bias_ref[0])
            e_ref[0, sc] = e.astype(BF16)
            inv_total = 1.0 / total
            inv_ref[0, sc] = jnp.broadcast_to(inv_total, (8, PAIR_LANES))
            return inv_total

        nxt = weights(0)
        for sc in range(SUPERS_PER_TILE):
            rows = slice(sc * SUPER, (sc + 1) * SUPER)
            win = _window(sc)
            inv_total = nxt
            if sc + 1 < SUPERS_PER_TILE:
                nxt = weights(sc + 1)
            a = _own_head_rows(_dot(vband_t[:, win], e_ref[0, sc]) * inv_total)
            a_ref[rows] = a.astype(BF16)
            g = ag_ref[rows].astype(F32)
            ya_ref[rows] = (a * (g * _sigmoid(g))).astype(BF16)

    blk = pl.BlockSpec((TA, GROUP), lambda p, i: (i, p))

    def cols(first):
        return pl.BlockSpec((TA, GROUP), lambda p, i: (i, first + p))

    return pl.pallas_call(
        body,
        name="attn_fwd",
        grid=(N_PAIR, n_tiles),
        in_specs=[cols(Q_BLOCK), cols(K_BLOCK), cols(V_BLOCK), cols(AG_BLOCK),
                  pl.BlockSpec((1, WINDOW, PAIR_LANES), lambda p, i: (p, 0, 0))],
        out_specs=[blk, blk,
                   pl.BlockSpec((1, SUPERS_PER_TILE, WINDOW, PAIR_LANES), lambda p, i: (p, i, 0, 0)),
                   pl.BlockSpec((1, SUPERS_PER_TILE, 8, PAIR_LANES), lambda p, i: (p, i, 0, 0))],
        out_shape=[jax.ShapeDtypeStruct((seq, ATTN_WIDTH), BF16), jax.ShapeDtypeStruct((seq, ATTN_WIDTH), BF16),
                   jax.ShapeDtypeStruct((N_PAIR, seq // SUPER, WINDOW, PAIR_LANES), BF16),
                   jax.ShapeDtypeStruct((N_PAIR, seq // SUPER, 8, PAIR_LANES), F32)],
        scratch_shapes=[pltpu.VMEM((2 * TA, 2 * GROUP), BF16), pltpu.VMEM((GROUP, 2 * TA), BF16)],
        compiler_params=pltpu.CompilerParams(
            dimension_semantics=("arbitrary", "arbitrary"), vmem_limit_bytes=VMEM_LIMIT),
    )(proj_bf, proj_bf, proj_bf, proj_bf, bias_t)


def _attn_bwd(proj_bf, a, dy, e_all, inv_all):
    seq = proj_bf.shape[0]
    n_tiles = seq // TA

    def body(q_ref, k_ref, v_ref, a_ref, ag_ref, dy_ref, e_ref, inv_ref,
             dq_ref, dk_ref, dv_ref, dag_ref, db_ref, vband, kband_t, dkacc, dvacc):
        i = pl.program_id(1)

        @pl.when(i == 0)
        def _():
            dkacc[...] = jnp.zeros_like(dkacc)
            dvacc[...] = jnp.zeros_like(dvacc)
            db_ref[...] = jnp.zeros_like(db_ref)

        @pl.when(i < n_tiles)
        def _():
            _shift_band(i, vband, v_ref)
            _shift_band_t(i, kband_t, k_ref)

            def score_grads(sc):
                rows = slice(sc * SUPER, (sc + 1) * SUPER)
                win = _window(sc)
                q_rows = _scaled_queries(q_ref, rows)
                g = ag_ref[rows].astype(F32)
                sg = _sigmoid(g)
                dyc = dy_ref[rows].astype(F32)
                dag_ref[rows] = (dyc * a_ref[rows].astype(F32) * (sg * (1.0 + g * (1.0 - sg)))).astype(BF16)
                da_rows = _by_head((dyc * (g * sg)).astype(BF16))
                p = e_ref[0, sc].astype(F32) * inv_ref[0, sc, :1]
                dp = _dot_t(vband[win], da_rows)
                ds = p * (dp - jnp.sum(p * dp, axis=0, keepdims=True))
                db_ref[0] += ds
                return q_rows, da_rows, p.astype(BF16), ds.astype(BF16)

            nxt = score_grads(0)
            for sc in range(SUPERS_PER_TILE):
                rows = slice(sc * SUPER, (sc + 1) * SUPER)
                win = _window(sc)
                q_rows, da_rows, p_bf, ds_bf = nxt
                if sc + 1 < SUPERS_PER_TILE:
                    nxt = score_grads(sc + 1)
                dq_ref[rows] = (_own_head_rows(_dot(kband_t[:, win], ds_bf)) * SCALE).astype(BF16)
                dkacc[win] += _dot(ds_bf, q_rows)
                dvacc[win] += _dot(p_bf, da_rows)

        dk_ref[...] = dkacc[:TA].astype(BF16)
        dv_ref[...] = dvacc[:TA].astype(BF16)
        dkacc[:TA] = dkacc[TA:]
        dvacc[:TA] = dvacc[TA:]
        dkacc[TA:] = jnp.zeros((TA, GROUP), F32)
        dvacc[TA:] = jnp.zeros((TA, GROUP), F32)

    last = n_tiles - 1
    cur = pl.BlockSpec((TA, GROUP), lambda p, i: (jnp.minimum(i, last), p))
    older = pl.BlockSpec((TA, GROUP), lambda p, i: (jnp.maximum(i - 1, 0), p))
    dy_blk = pl.BlockSpec((TA, GROUP), lambda p, i: (jnp.minimum(i, last), N_PAIR + p))
    per_pair = pl.BlockSpec((1, WINDOW, PAIR_LANES), lambda p, i: (p, 0, 0))

    def cols(first):
        return pl.BlockSpec((TA, GROUP), lambda p, i: (jnp.minimum(i, last), first + p))

    def kept(rows):
        return pl.BlockSpec((1, SUPERS_PER_TILE, rows, PAIR_LANES), lambda p, i: (p, jnp.minimum(i, last), 0, 0))

    def out(dtype):
        return jax.ShapeDtypeStruct((seq, ATTN_WIDTH), dtype)

    return pl.pallas_call(
        body,
        name="attn_bwd",
        grid=(N_PAIR, n_tiles + 1),
        in_specs=[cols(Q_BLOCK), cols(K_BLOCK), cols(V_BLOCK), cur, cols(AG_BLOCK), dy_blk, kept(WINDOW), kept(8)],
        out_specs=[cur, older, older, cur, per_pair],
        out_shape=[out(BF16), out(BF16), out(BF16), out(BF16),
                   jax.ShapeDtypeStruct((N_PAIR, WINDOW, PAIR_LANES), F32)],
        scratch_shapes=[
            pltpu.VMEM((2 * TA, GROUP), BF16), pltpu.VMEM((GROUP, 2 * TA), BF16),
            pltpu.VMEM((2 * TA, GROUP), F32), pltpu.VMEM((2 * TA, GROUP), F32)],
        compiler_params=pltpu.CompilerParams(
            dimension_semantics=("arbitrary", "arbitrary"), vmem_limit_bytes=VMEM_LIMIT),
    )(proj_bf, proj_bf, proj_bf, a, proj_bf, dy, e_all, inv_all)


BIN_ROWS = 136


def _bias_bins(db_t):
    idx_t = jnp.asarray(_rel_index_tile())

    def body(db_ref, idx_ref, out_ref):
        lane = lax.broadcasted_iota(jnp.int32, (1, GROUP), 1)
        row = lax.broadcasted_iota(jnp.int32, (BIN_ROWS, GROUP), 0)
        out = jnp.zeros((BIN_ROWS, GROUP), F32)
        for r in range(N_REL - 1):
            lo = 0 if r == 0 else ((BAND - 2 * CHUNK + r) // 8) * 8
            hi = WINDOW if r == 0 else min(WINDOW, lo + SUPER + 8)
            hit = jnp.where(idx_ref[lo:hi] == r, db_ref[0, lo:hi], 0.0)
            col = jnp.sum(hit, axis=0, keepdims=True)
            s0 = jnp.sum(col[:, :SUPER], axis=1, keepdims=True)
            s1 = jnp.sum(col[:, SUPER:], axis=1, keepdims=True)
            val = jnp.where(lane == 0, s0, jnp.where(lane == 1, s1, 0.0))
            out = jnp.where(row == r, val, out)
        out_ref[0] = out

    return pl.pallas_call(
        body,
        name="bias_bins",
        grid=(N_PAIR,),
        in_specs=[pl.BlockSpec((1, WINDOW, PAIR_LANES), lambda p: (p, 0, 0)), VMEM],
        out_specs=pl.BlockSpec((1, BIN_ROWS, GROUP), lambda p: (p, 0, 0)),
        out_shape=jax.ShapeDtypeStruct((N_PAIR, BIN_ROWS, GROUP), F32),
        compiler_params=pltpu.CompilerParams(dimension_semantics=("arbitrary",)),
    )(db_t, idx_t)


def _out_loss(x, tgt, proj, ya, wout_full, g2, pw_bf, ps):
    seq = x.shape[0]
    to = min(seq, 2 * TS)
    n_tiles = seq // to

    def body(x_ref, t_ref, pv_ref, pg_ref, ya_ref, w_ref, g2_ref, pw_ref, ps_ref,
             dx2_ref, dy_ref, gw_ref, gg_ref, loss_ref, sq_ref, halo_ref, wt_ref):
        i = pl.program_id(0)

        @pl.when(i == 0)
        def _():
            gw_ref[...] = jnp.zeros_like(gw_ref)
            gg_ref[...] = jnp.zeros_like(gg_ref)
            sq_ref[...] = jnp.zeros_like(sq_ref)
            halo_ref[...] = jnp.zeros_like(halo_ref)
            for b in range(N_CHIP):
                wt_ref[b] = w_ref[b].T

        pv = pv_ref[...].astype(F32)
        pg = pg_ref[...].astype(F32)
        diffs = [d.astype(BF16) for d in _pool_diffs(pv, halo_ref[...], i)]
        halo_ref[...] = pv[to - HALO:, :]
        yp = ((_pool_mix(diffs, pw_ref) * ps_ref[...]) * (pg * _sigmoid(pg))).astype(BF16)
        g2v = g2_ref[...]

        def parts(rows):
            return [yp[rows, :SHARD_OUT], yp[rows, SHARD_OUT:], ya_ref[rows, :SHARD_OUT], ya_ref[rows, SHARD_OUT:]]

        def project(rows, ys):
            x2 = x_ref[rows]
            for b in range(N_CHIP):
                x2 = x2 + _dot(ys[b], w_ref[b])
            return x2

        def norm_loss(rows, x2):
            r = lax.rsqrt(jnp.mean(x2 * x2, axis=-1, keepdims=True) + EPS)
            xh = x2 * r
            diff = xh * g2v - t_ref[rows]
            sq_ref[...] += jnp.sum(diff * diff, axis=0, keepdims=True)
            dfin = diff * (1.0 / D_MODEL)
            gg_ref[...] += jnp.sum(dfin * xh, axis=0, keepdims=True)
            dxh = dfin * g2v
            dx2 = r * (dxh - xh * jnp.mean(dxh * xh, axis=-1, keepdims=True))
            dx2_ref[rows] = dx2
            return dx2.astype(BF16)

        def back(rows, ys, dx2_bf):
            for b in range(N_CHIP):
                gw_ref[b] += _tdot(ys[b], dx2_bf)
                dy_ref[rows, b * SHARD_OUT:(b + 1) * SHARD_OUT] = _dot(dx2_bf, wt_ref[b]).astype(BF16)

        n_parts = 2
        part = to // n_parts
        spans = [slice(r * part, (r + 1) * part) for r in range(n_parts)]
        ys = [parts(rows) for rows in spans]
        x2_next = project(spans[0], ys[0])
        for r in range(n_parts):
            x2 = x2_next
            if r + 1 < n_parts:
                x2_next = project(spans[r + 1], ys[r + 1])
            back(spans[r], ys[r], norm_loss(spans[r], x2))

        @pl.when(i == n_tiles - 1)
        def _():
            total = jnp.sum(sq_ref[...], axis=1, keepdims=True) * (0.5 / D_MODEL)
            loss_ref[...] = jnp.broadcast_to(total, loss_ref.shape)

    def rows(width, col=0):
        return pl.BlockSpec((to, width), lambda i: (i, col))

    return pl.pallas_call(
        body,
        name="out_loss",
        grid=(n_tiles,),
        in_specs=[rows(D_MODEL), rows(D_MODEL), rows(POOL_WIDTH, 0), rows(POOL_WIDTH, 1), rows(ATTN_WIDTH),
                  VMEM, VMEM, VMEM, VMEM],
        out_specs=[rows(D_MODEL), rows(D_MODEL), VMEM, VMEM, VMEM],
        out_shape=[
            jax.ShapeDtypeStruct((seq, D_MODEL), F32), jax.ShapeDtypeStruct((seq, D_MODEL), BF16),
            jax.ShapeDtypeStruct((N_CHIP, SHARD_OUT, D_MODEL), F32), jax.ShapeDtypeStruct((1, D_MODEL), F32),
            jax.ShapeDtypeStruct((8, GROUP), F32)],
        scratch_shapes=[pltpu.VMEM((1, D_MODEL), F32), pltpu.VMEM((HALO, POOL_WIDTH), F32),
                        pltpu.VMEM((N_CHIP, D_MODEL, SHARD_OUT), BF16)],
        compiler_params=pltpu.CompilerParams(dimension_semantics=("arbitrary",), vmem_limit_bytes=VMEM_LIMIT),
    )(x, tgt, proj, proj, ya, wout_full, g2, pw_bf, ps)


def _inproj_bwd(x, dx2, dy, proj, dq, dk, dv, dag, win_full, pw_bf, g1, ps):
    seq = x.shape[0]
    n_tiles = seq // TS

    def body(x_ref, dx2_ref, dyp_ref, pv_ref, pvprev_ref, pg_ref, dq_ref, dk_ref, dv_ref, dag_ref,
             w_ref, pw_ref, g1_ref, ps_ref, gx_ref, dproj_ref, ht_ref, gg_ref, gps_ref, gpw_ref, halo_ref):
        i = pl.program_id(0)
        tile = n_tiles - 1 - i

        @pl.when(i == 0)
        def _():
            gg_ref[...] = jnp.zeros_like(gg_ref)
            gps_ref[...] = jnp.zeros_like(gps_ref)
            gpw_ref[...] = jnp.zeros_like(gpw_ref)
            halo_ref[...] = jnp.zeros_like(halo_ref)

        pv_t = pv_ref[...].astype(F32)
        pg_t = pg_ref[...].astype(F32)
        prev_rows = jnp.where(tile > 0, pvprev_ref[...].astype(F32), 0.0)
        diffs = [d.astype(BF16) for d in _pool_diffs(pv_t, prev_rows, tile)]
        mixed = _pool_mix(diffs, pw_ref)
        sg = _sigmoid(pg_t)
        silu = pg_t * sg
        dyp = dyp_ref[...].astype(F32)
        psv = ps_ref[...]
        gps_ref[...] += jnp.sum(dyp * mixed * silu, axis=0, keepdims=True)
        dmixed = (dyp * psv * silu).astype(BF16)
        dpg = dyp * (mixed * psv) * (sg * (1.0 + pg_t * (1.0 - sg)))
        counts = _row_counts(tile, TS)
        dds = []
        for g in range(len(POOL_WINDOWS)):
            dm_g = dmixed[:, g * GROUP:(g + 1) * GROUP]
            gpw_ref[g] += _tdot(diffs[g], dm_g)
            dds.append(_dot_t(dm_g, pw_ref[g]))
        dd = jnp.concatenate(dds, axis=1)
        spread = jnp.concatenate([dds[g] / counts[g] for g in range(len(POOL_WINDOWS))], axis=1)
        sums = _window_sums(jnp.concatenate([spread, halo_ref[...]], axis=0), forward=True)
        halo_ref[...] = spread[:HALO]
        dpv = jnp.concatenate(
            [sums[g][:TS, g * GROUP:(g + 1) * GROUP] for g in range(len(POOL_WINDOWS))], axis=1) - dd

        xt = x_ref[...]
        r = lax.rsqrt(jnp.mean(xt * xt, axis=-1, keepdims=True) + EPS)
        xh = xt * r
        g1v = g1_ref[...]
        ht_ref[...] = (xh * g1v).astype(BF16).T
        dproj = jnp.concatenate(
            [dpv.astype(BF16), dpg.astype(BF16), dq_ref[...], dk_ref[...], dv_ref[...], dag_ref[...]],
            axis=1)
        dproj_ref[...] = dproj
        dh = _dot_t(dproj[:, :SHARD_IN], w_ref[0])
        for chip in range(1, N_CHIP):
            dh = dh + _dot_t(dproj[:, chip * SHARD_IN:(chip + 1) * SHARD_IN], w_ref[chip])

        gg_ref[...] += jnp.sum(dh * xh, axis=0, keepdims=True)
        dxh = dh * g1v
        gx_ref[...] = dx2_ref[...] + r * (dxh - xh * jnp.mean(dxh * xh, axis=-1, keepdims=True))

    def rows(width, col=0):
        return pl.BlockSpec((TS, width), lambda i: (n_tiles - 1 - i, col))

    prev = pl.BlockSpec((HALO, POOL_WIDTH), lambda i: (jnp.maximum((n_tiles - 1 - i) * (TS // HALO) - 1, 0), 0))
    return pl.pallas_call(
        body,
        name="inproj_bwd",
        grid=(n_tiles,),
        in_specs=[rows(D_MODEL), rows(D_MODEL), rows(POOL_WIDTH), rows(POOL_WIDTH), prev, rows(POOL_WIDTH, 1),
                  rows(ATTN_WIDTH), rows(ATTN_WIDTH), rows(ATTN_WIDTH), rows(ATTN_WIDTH), VMEM, VMEM, VMEM, VMEM],
        out_specs=[rows(D_MODEL), rows(N_CHIP * SHARD_IN),
                   pl.BlockSpec((D_MODEL, TS), lambda i: (0, n_tiles - 1 - i)), VMEM, VMEM, VMEM],
        out_shape=[
            jax.ShapeDtypeStruct((seq, D_MODEL), F32),
            jax.ShapeDtypeStruct((seq, N_CHIP * SHARD_IN), BF16),
            jax.ShapeDtypeStruct((D_MODEL, seq), BF16),
            jax.ShapeDtypeStruct((1, D_MODEL), F32),
            jax.ShapeDtypeStruct((1, POOL_WIDTH), F32),
            jax.ShapeDtypeStruct((len(POOL_WINDOWS), GROUP, GROUP), F32)],
        scratch_shapes=[pltpu.VMEM((HALO, POOL_WIDTH), F32)],
        compiler_params=pltpu.CompilerParams(dimension_semantics=("arbitrary",), vmem_limit_bytes=VMEM_LIMIT),
    )(x, dx2, dy, proj, proj, proj, dq, dk, dv, dag, win_full, pw_bf, g1, ps)


PASS_PEER = (2, 0, 1)


def _gw_reduce(ht, dproj, gwout, small):
    seq = ht.shape[1]
    tm = min(2 * TS, seq // 4)
    n_tiles = seq // tm
    half_small = SMALL_ROWS // 2
    cx, cy = lax.axis_index("x"), lax.axis_index("y")
    outer = _other_chips(cx, cy)
    order = jnp.stack([outer[n][1] for n in PASS_PEER] + [2 * cx + cy]).astype(jnp.int32)

    def body(order_ref, ht_ref, dp_ref, gwout_ref, small_ref, gin_final, gout_final, small_final,
             hbuf, acc, pair_in, pair_out, pair_small, tx_in, tx_out, rx_in, rx_out, rx_small,
             gin_out, gout_out, small_out, send_sems, recv_sems, out_sems):
        j = pl.program_id(0)
        i = pl.program_id(1)
        x, y, c = _my_place()
        b = 2 * x + y
        sibling = (x, y, 1 - c)
        others = _other_chips(x, y)
        mine_in = pl.ds(pl.multiple_of(c * HALF_IN, HALF_IN), HALF_IN)
        mine_out = pl.ds(pl.multiple_of(c * HALF_OUT, HALF_OUT), HALF_OUT)
        mine_small = pl.ds(pl.multiple_of(c * half_small, 8), half_small)
        theirs_in = pl.ds(pl.multiple_of((1 - c) * HALF_IN, HALF_IN), HALF_IN)
        theirs_out = pl.ds(pl.multiple_of((1 - c) * HALF_OUT, HALF_OUT), HALF_OUT)
        theirs_small = pl.ds(pl.multiple_of((1 - c) * half_small, 8), half_small)

        def copy(k, src, dst, to):
            return pltpu.make_async_remote_copy(
                src_ref=src, dst_ref=dst, send_sem=send_sems.at[k], recv_sem=recv_sems.at[k],
                device_id=to, device_id_type=MESH)

        swap_out = copy(0, gwout_ref.at[:, theirs_out], pair_out, sibling)
        swap_small = copy(1, small_ref, pair_small, sibling)

        def swap_in(p):
            return copy(2 + p, acc.at[p % 2, theirs_in], pair_in.at[p % 2], sibling)

        def to_chip(n, t):
            to = (*others[n][0], c)
            if t == 0:
                return copy(6 + 3 * n, tx_in.at[n], rx_in.at[n], to)
            if t == 1:
                return copy(7 + 3 * n, tx_out.at[n], rx_out.at[n], to)
            return copy(8 + 3 * n, pair_small.at[mine_small], rx_small.at[b], to)

        share_in = copy(15, gin_out.at[mine_in], gin_out.at[mine_in], sibling)
        share_out = copy(16, gout_out.at[mine_out], gout_out.at[mine_out], sibling)
        share_small = copy(17, small_out.at[mine_small], small_out.at[mine_small], sibling)

        def at(jj, ii):
            return (j == jj) & (i == ii)

        par = j % 2

        @pl.when(i == 0)
        def _():
            acc[par] = jnp.zeros((D_MODEL, SHARD_IN), F32)

        cols = pl.ds(pl.multiple_of(i * tm, tm), tm)

        @pl.when(j == 0)
        def _():
            hbuf[:, cols] = ht_ref[...]

        acc[par] += _dot(hbuf[:, cols], dp_ref[...])

        @pl.when(at(0, 0))
        def _():
            swap_out.start()
            swap_small.start()

        @pl.when(at(0, 2))
        def _():
            swap_out.wait_recv()
            swap_small.wait_recv()
            for chip in range(N_CHIP):
                pair_out[chip] = gwout_ref[chip, mine_out] + pair_out[chip]
            pair_small[...] = small_ref[...] + pair_small[...]
            rx_small[b] = pair_small[mine_small]
            for n in range(N_CHIP - 1):
                tx_out[n] = pair_out[others[n][1]].astype(BF16)
                to_chip(n, 1).start()
                to_chip(n, 2).start()

        @pl.when(at(1, 3))
        def _():
            total_out = pair_out[b]
            for n in range(N_CHIP - 1):
                to_chip(n, 1).wait_recv()
                copy(8 + 3 * n, pair_small.at[mine_small], rx_small.at[others[n][1]], (*others[n][0], c)).wait_recv()
                total_out = total_out + rx_out[n].astype(F32)
            gout_out[mine_out] = total_out
            small_out[mine_small] = ((rx_small[0] + rx_small[1]) + rx_small[2]) + rx_small[3]
            share_out.start()
            share_small.start()

        for p in range(N_CHIP - 1):
            n = PASS_PEER[p]

            @pl.when(at(p + 1, 0))
            def _(p=p):
                swap_in(p).start()

            @pl.when(at(p + 1, 2))
            def _(p=p, n=n):
                swap_in(p).wait_recv()
                swap_in(p).wait_send()
                tx_in[n] = (acc[p % 2, mine_in] + pair_in[p % 2]).astype(BF16)
                to_chip(n, 0).start()

        @pl.when(at(N_CHIP - 1, n_tiles - 1))
        def _():
            last = N_CHIP - 1
            swap_in(last).start()
            swap_in(last).wait_recv()
            total_in = acc[last % 2, mine_in] + pair_in[last % 2]
            for n in range(N_CHIP - 1):
                to_chip(n, 0).wait_recv()
                total_in = total_in + rx_in[n].astype(F32)
            gin_out[mine_in] = total_in
            share_in.start()
            copy(15, gin_out.at[theirs_in], gin_out.at[theirs_in], sibling).wait_recv()
            copy(16, gout_out.at[theirs_out], gout_out.at[theirs_out], sibling).wait_recv()
            copy(17, small_out.at[theirs_small], small_out.at[theirs_small], sibling).wait_recv()
            swap_out.wait_send()
            swap_small.wait_send()
            swap_in(last).wait_send()
            for n in range(N_CHIP - 1):
                for t in range(3):
                    to_chip(n, t).wait_send()
            share_in.wait_send()
            share_out.wait_send()
            share_small.wait_send()
            outs = [pltpu.make_async_copy(src, dst, out_sems.at[k]) for k, (src, dst) in enumerate(
                [(gin_out, gin_final), (gout_out, gout_final), (small_out, small_final)])]
            for cp in outs:
                cp.start()
            for cp in outs:
                cp.wait()

    assert n_tiles >= 4, "the reduction's steps are spread over the first four token steps of a pass"
    grid_spec = pltpu.PrefetchScalarGridSpec(
        num_scalar_prefetch=1,
        grid=(N_CHIP, n_tiles),
        in_specs=[
            pl.BlockSpec((D_MODEL, tm), lambda j, i, o: (0, jnp.where(j == 0, i, n_tiles - 1))),
            pl.BlockSpec((tm, SHARD_IN), lambda j, i, o: (i, o[j])),
            VMEM, VMEM],
        out_specs=[ANY, ANY, ANY],
        scratch_shapes=[
            pltpu.VMEM((D_MODEL, seq), BF16),
            pltpu.VMEM((2, D_MODEL, SHARD_IN), F32),
            pltpu.VMEM((2, HALF_IN, SHARD_IN), F32),
            pltpu.VMEM((N_CHIP, HALF_OUT, D_MODEL), F32),
            pltpu.VMEM((SMALL_ROWS, GROUP), F32),
            pltpu.VMEM((N_CHIP - 1, HALF_IN, SHARD_IN), BF16),
            pltpu.VMEM((N_CHIP - 1, HALF_OUT, D_MODEL), BF16),
            pltpu.VMEM((N_CHIP - 1, HALF_IN, SHARD_IN), BF16),
            pltpu.VMEM((N_CHIP - 1, HALF_OUT, D_MODEL), BF16),
            pltpu.VMEM((N_CHIP, half_small, GROUP), F32),
            pltpu.VMEM((D_MODEL, SHARD_IN), F32),
            pltpu.VMEM((SHARD_OUT, D_MODEL), F32),
            pltpu.VMEM((SMALL_ROWS, GROUP), F32),
            pltpu.SemaphoreType.DMA((18,)),
            pltpu.SemaphoreType.DMA((18,)),
            pltpu.SemaphoreType.DMA((3,)),
        ],
    )
    return pl.pallas_call(
        body,
        name="gw_reduce",
        grid_spec=grid_spec,
        out_shape=(
            jax.ShapeDtypeStruct((D_MODEL, SHARD_IN), F32),
            jax.ShapeDtypeStruct((SHARD_OUT, D_MODEL), F32),
            jax.ShapeDtypeStruct((SMALL_ROWS, GROUP), F32),
        ),
        compiler_params=pltpu.CompilerParams(
            dimension_semantics=("arbitrary", "arbitrary"), vmem_limit_bytes=VMEM_LIMIT),
    )(order, ht, dproj, gwout, small)


def _adamw(name, w, g, m, v, block_rows):
    rows, cols = w.shape

    def body(w_ref, g_ref, m_ref, v_ref, d_ref, m_out, v_out):
        grad = g_ref[...]
        m_new = ADAM_B1 * m_ref[...] + (1.0 - ADAM_B1) * grad
        v_new = ADAM_B2 * v_ref[...] + (1.0 - ADAM_B2) * (grad * grad)
        m_hat = m_new / (1.0 - ADAM_B1 ** ADAM_STEP)
        v_hat = v_new / (1.0 - ADAM_B2 ** ADAM_STEP)
        d_ref[...] = -ADAM_LR * (m_hat / (jnp.sqrt(v_hat) + ADAM_EPS) + ADAM_WD * w_ref[...])
        m_out[...] = m_new
        v_out[...] = v_new

    blk = pl.BlockSpec((block_rows, cols), lambda i: (i, 0))
    shape = jax.ShapeDtypeStruct((rows, cols), F32)
    return pl.pallas_call(
        body,
        name=name,
        grid=(rows // block_rows,),
        in_specs=[blk] * 4,
        out_specs=[blk] * 3,
        out_shape=[shape] * 3,
        compiler_params=pltpu.CompilerParams(dimension_semantics=("arbitrary",)),
    )(w, g, m, v)


def _pack_small(norm_gain, pool_w, pool_scale, rel_bias, final_gain, loss_rows):
    parts = [
        norm_gain.reshape(8, GROUP),
        pool_w.reshape(len(POOL_WINDOWS) * GROUP, GROUP),
        jnp.pad(pool_scale.reshape(4, GROUP), ((0, 4), (0, 0))),
        jnp.pad(rel_bias.reshape(8, N_REL), ((0, 0), (0, 2 * GROUP - N_REL))).reshape(16, GROUP),
        final_gain.reshape(8, GROUP),
        loss_rows,
    ]
    return jnp.concatenate(parts, axis=0)


def _unpack_small(block):
    norm_gain = block[0:8].reshape(1, D_MODEL)
    pool_w = block[8:520].reshape(1, len(POOL_WINDOWS), GROUP, GROUP)
    pool_scale = block[520:524].reshape(1, POOL_WIDTH)
    rel_bias = block[528:544].reshape(8, 2 * GROUP)[:, :N_REL].reshape(1, 8, N_REL)
    final_gain = block[544:552].reshape(D_MODEL)
    return norm_gain, pool_w, pool_scale, rel_bias, final_gain


def kernel(x, norm_gain, w_in, pool_w, pool_scale, rel_bias, w_out, final_norm_gain, loss_target, m_norm_gain, m_w_in, m_pool_w, m_pool_scale, m_rel_bias, m_w_out, m_final_norm_gain, v_norm_gain, v_w_in, v_pool_w, v_pool_scale, v_rel_bias, v_w_out, v_final_norm_gain):
    assert x.shape[1] % TS == 0 and x.shape[2] == D_MODEL
    xs = x[0]
    tgt = loss_target[0]
    g1 = norm_gain.reshape(1, D_MODEL)
    g2 = final_norm_gain.reshape(1, D_MODEL)
    ps = pool_scale.reshape(1, POOL_WIDTH)
    pw_bf = pool_w[0].astype(BF16)

    proj, win_full, wout_full = _gather_inproj(xs, g1, w_in[0].astype(BF16), w_out[0].astype(BF16))
    bias_t = _bias_tile(rel_bias[0])

    a, ya, e_all, inv_all = _attn_fwd(proj, bias_t)
    dx2, dy, gwout, gg2, loss_rows = _out_loss(xs, tgt, proj, ya, wout_full, g2, pw_bf, ps)
    dq, dk, dv, dag, db_t = _attn_bwd(proj, a, dy, e_all, inv_all)
    bins = _bias_bins(db_t)
    gx, dproj, ht, gg1, gps, gpw = _inproj_bwd(xs, dx2, dy, proj, dq, dk, dv, dag, win_full, pw_bf, g1, ps)

    g_bias = bins[:, :N_REL, :2].transpose(0, 2, 1).reshape(8, N_REL)
    small = _pack_small(gg1, gpw, gps, g_bias, gg2, loss_rows)
    g_win, g_wout, g_small = _gw_reduce(ht, dproj, gwout, small)
    loss = g_small[LOSS_ROW, 0]

    zeros8 = jnp.zeros((8, GROUP), F32)
    w_small = _pack_small(norm_gain, pool_w, pool_scale, rel_bias, final_norm_gain, zeros8)
    m_small = _pack_small(m_norm_gain, m_pool_w, m_pool_scale, m_rel_bias, m_final_norm_gain, zeros8)
    v_small = _pack_small(v_norm_gain, v_pool_w, v_pool_scale, v_rel_bias, v_final_norm_gain, zeros8)

    d_win, m_win, v_win = _adamw("adamw_w_in", w_in[0], g_win, m_w_in[0], v_w_in[0], 256)
    d_wout, m_wout, v_wout = _adamw("adamw_w_out", w_out[0], g_wout, m_w_out[0], v_w_out[0], 128)
    d_small, m_new_small, v_new_small = _adamw("adamw_small", w_small, g_small, m_small, v_small, SMALL_ROWS // 2)

    def full(win_part, wout_part, block):
        ng, pw, psc, rb, fg = _unpack_small(block)
        return [ng, win_part[None], pw, psc, rb, wout_part[None], fg]

    grads = full(g_win, g_wout, g_small)
    deltas = full(d_win, d_wout, d_small)
    new_m = full(m_win, m_wout, m_new_small)
    new_v = full(v_win, v_wout, v_new_small)
    return (loss, gx[None], *grads, *deltas, *new_m, *new_v)
```

```python
import numpy as np
import jax
import jax.numpy as jnp
from jax import lax
from jax.experimental import pallas as pl
from jax.experimental.pallas import tpu as pltpu

F32 = jnp.float32
BF16 = jnp.bfloat16

D_MODEL = 1024
POOL_WIDTH = 512
ATTN_WIDTH = 512
POOL_WINDOWS = (2, 4, 8, 16)
GROUP = 128
CHUNK = 64
LEFT_CHUNKS = 8
BAND = (LEFT_CHUNKS + 1) * CHUNK
HEAD_DIM = 64
N_PAIR = 4
MAX_REL = 64
N_REL = 2 * MAX_REL + 1
EPS = 1e-6
MASK_VALUE = -1e30
SCALE = 0.125

ADAM_LR = 0.001
ADAM_B1 = 0.9
ADAM_B2 = 0.999
ADAM_EPS = 1e-08
ADAM_WD = 0.01
ADAM_STEP = 10

TS = LEFT_CHUNKS * CHUNK
SUPER = 2 * CHUNK
WINDOW = BAND + CHUNK
TA = 4 * TS
WIN_BASE = TA - LEFT_CHUNKS * CHUNK
SUPERS_PER_TILE = TA // SUPER
PAIR_LANES = 2 * SUPER
HALO = 16
N_CHIP = 4
SHARD_IN = 768
SHARD_OUT = 256
PIECE = 256
PIECES_PER_SHARD = SHARD_IN // PIECE
HALF_IN = D_MODEL // 2
HALF_OUT = SHARD_OUT // 2
SMALL_ROWS = 560
LOSS_ROW = 552
VMEM_LIMIT = 60 * 1024 * 1024

MESH = pl.DeviceIdType.MESH
ANY = pl.BlockSpec(memory_space=pl.ANY)
VMEM = pl.BlockSpec(memory_space=pltpu.VMEM)


def _sigmoid(x):
    return 1.0 / (1.0 + jnp.exp(-x))


def _dot(a, b):
    return jnp.dot(a, b, preferred_element_type=F32)


def _dot_t(a, b):
    return lax.dot_general(a, b, (((1,), (1,)), ((), ())), preferred_element_type=F32)


def _tdot(a, b):
    return lax.dot_general(a, b, (((0,), (0,)), ((), ())), preferred_element_type=F32)


def _my_place():
    return lax.axis_index("x"), lax.axis_index("y"), lax.axis_index("c")


def _other_chips(x, y):
    places = [(1 - x, y), (x, 1 - y), (1 - x, 1 - y)]
    return [(p, 2 * p[0] + p[1]) for p in places]


def _gather_inproj(x, g1, win_sh, wout_sh):
    seq = x.shape[0]
    tm = min(seq, 4 * TS)
    n_tiles = seq // tm
    cx, cy = lax.axis_index("x"), lax.axis_index("y")
    order = jnp.stack([2 * cx + cy] + [chip for _, chip in _other_chips(cx, cy)]).astype(jnp.int32)

    def body(order_ref, x_ref, g1_ref, win_ref, wout_ref, proj_bf_ref, win_full, wout_full,
             hbuf, wbuf, send_sems, recv_sems, local_sems):
        j = pl.program_id(0)
        i = pl.program_id(1)
        x_, y_, c = _my_place()
        b = 2 * x_ + y_
        sibling = (x_, y_, 1 - c)
        others = _other_chips(x_, y_)

        def halves(chip, core):
            return (
                win_full.at[chip, pl.ds(core * HALF_IN, HALF_IN)],
                wout_full.at[chip, pl.ds(core * HALF_OUT, HALF_OUT)],
            )

        def copy(k, src, dst, to):
            return pltpu.make_async_remote_copy(
                src_ref=src, dst_ref=dst, send_sem=send_sems.at[k], recv_sem=recv_sems.at[k],
                device_id=to, device_id_type=MESH)

        own = [
            pltpu.make_async_copy(win_ref, win_full.at[b], local_sems.at[0]),
            pltpu.make_async_copy(wout_ref, wout_full.at[b], local_sems.at[1]),
        ]
        mine_src = (win_ref.at[pl.ds(c * HALF_IN, HALF_IN)], wout_ref.at[pl.ds(c * HALF_OUT, HALF_OUT)])

        def direct(n, t):
            return copy(2 * n + t, mine_src[t], halves(b, c)[t], (*others[n][0], c))

        def arrival(n, t):
            landed = halves(others[n][1], c)[t]
            return copy(2 * n + t, landed, landed, (*others[n][0], c))

        def passing(n, t):
            landed = halves(others[n][1], c)[t]
            return copy(6 + 2 * n + t, landed, landed, sibling)

        def from_sibling(n, t):
            landed = halves(others[n][1], 1 - c)[t]
            return copy(6 + 2 * n + t, landed, landed, sibling)

        @pl.when((j == 0) & (i == 0))
        def _():
            first = pltpu.make_async_copy(win_ref, wbuf.at[0], local_sems.at[2])
            first.start()
            for cp in own:
                cp.start()
            for t in range(2):
                for n in range(N_CHIP - 1):
                    direct(n, t).start()
            first.wait()

        def load(n):
            return pltpu.make_async_copy(win_full.at[others[n][1]], wbuf.at[(n + 1) % 2], local_sems.at[2])

        for n in range(N_CHIP - 1):
            @pl.when((j == n) & (i == n_tiles - 1))
            def _(n=n):
                arrival(n, 0).wait_recv()
                passing(n, 0).start()

            @pl.when((j == n + 1) & (i == 0))
            def _(n=n):
                load(n).wait()

        rows = pl.ds(pl.multiple_of(i * tm, tm), tm)

        @pl.when(j == 0)
        def _():
            xt = x_ref[...]
            r = lax.rsqrt(jnp.mean(xt * xt, axis=-1, keepdims=True) + EPS)
            hbuf[rows] = ((xt * r) * g1_ref[...]).astype(BF16)

        proj_bf_ref[...] = _dot(hbuf[rows], wbuf[j % 2]).astype(BF16)

        for n in range(N_CHIP - 1):
            @pl.when((j == n) & (i == n_tiles - 1))
            def _(n=n):
                from_sibling(n, 0).wait_recv()
                load(n).start()

        @pl.when((j == N_CHIP - 1) & (i == n_tiles - 1))
        def _():
            for n in range(N_CHIP - 1):
                arrival(n, 1).wait_recv()
                passing(n, 1).start()
            for n in range(N_CHIP - 1):
                from_sibling(n, 1).wait_recv()
            for n in range(N_CHIP - 1):
                for t in range(2):
                    direct(n, t).wait_send()
                    passing(n, t).wait_send()
            for cp in own:
                cp.wait()

    last = n_tiles - 1
    grid_spec = pltpu.PrefetchScalarGridSpec(
        num_scalar_prefetch=1,
        grid=(N_CHIP, n_tiles),
        in_specs=[
            pl.BlockSpec((tm, D_MODEL), lambda j, i, o: (jnp.where(j == 0, i, last), 0)),
            VMEM, ANY, ANY],
        out_specs=[pl.BlockSpec((tm, SHARD_IN), lambda j, i, o: (i, o[j])), ANY, ANY],
        scratch_shapes=[
            pltpu.VMEM((seq, D_MODEL), BF16), pltpu.VMEM((2, D_MODEL, SHARD_IN), BF16),
            pltpu.SemaphoreType.DMA((12,)), pltpu.SemaphoreType.DMA((12,)), pltpu.SemaphoreType.DMA((3,))],
    )
    return pl.pallas_call(
        body,
        name="gather_inproj",
        grid_spec=grid_spec,
        out_shape=(
            jax.ShapeDtypeStruct((seq, N_CHIP * SHARD_IN), BF16),
            jax.ShapeDtypeStruct((N_CHIP, D_MODEL, SHARD_IN), BF16),
            jax.ShapeDtypeStruct((N_CHIP, SHARD_OUT, D_MODEL), BF16),
        ),
        compiler_params=pltpu.CompilerParams(
            dimension_semantics=("arbitrary", "arbitrary"), vmem_limit_bytes=VMEM_LIMIT),
    )(order, x, g1, win_sh, wout_sh)


def _window_sums(ext, forward):
    n = ext.shape[0]
    sums = []
    acc = ext
    for step in (1, 2, 4, 8):
        acc = acc + pltpu.roll(acc, (n - step) if forward else step, 0)
        sums.append(acc)
    return sums


def _row_counts(tile, rows):
    t = tile * rows + lax.broadcasted_iota(jnp.int32, (rows, GROUP), 0)
    return [jnp.minimum(t + 1, w).astype(F32) for w in POOL_WINDOWS]


def _pool_diffs(pv, prev_rows, tile):
    ext = jnp.concatenate([prev_rows, pv], axis=0)
    sums = _window_sums(ext, forward=False)
    counts = _row_counts(tile, pv.shape[0])
    out = []
    for g in range(len(POOL_WINDOWS)):
        cols = slice(g * GROUP, (g + 1) * GROUP)
        out.append(sums[g][HALO:, cols] / counts[g] - pv[:, cols])
    return out


def _pool_mix(diffs, pw_ref):
    return jnp.concatenate([_dot(diffs[g], pw_ref[g]) for g in range(len(POOL_WINDOWS))], axis=1)


Q_BLOCK, K_BLOCK, V_BLOCK, AG_BLOCK = 8, 12, 16, 20


def _bias_tile(rel_bias):
    flat = jnp.concatenate(
        [jnp.broadcast_to(rel_bias[:, :1], (rel_bias.shape[0], BAND - CHUNK - 1)), rel_bias[:, :2 * MAX_REL]], axis=1)
    rows = [flat[:, CHUNK - 1 - i:CHUNK - 1 - i + BAND] for i in range(CHUNK)]
    bias = jnp.stack(rows, axis=1)
    first = jnp.pad(bias, ((0, 0), (0, 0), (0, CHUNK)), constant_values=MASK_VALUE)
    second = jnp.pad(bias, ((0, 0), (0, 0), (CHUNK, 0)), constant_values=MASK_VALUE)
    both = jnp.concatenate([first, second], axis=1)
    return both.reshape(N_PAIR, PAIR_LANES, WINDOW).transpose(0, 2, 1)


def _rel_index_tile():
    j = np.arange(WINDOW)[:, None]
    q = np.arange(PAIR_LANES)[None, :] % SUPER
    band_key = j - CHUNK * (q // CHUNK)
    idx = np.clip(band_key - LEFT_CHUNKS * CHUNK - q % CHUNK, -MAX_REL, MAX_REL) + MAX_REL
    return np.where((band_key >= 0) & (band_key < BAND), idx, -1).astype(np.int32)


def _by_head(block):
    low = lax.broadcasted_iota(jnp.int32, block.shape, 1) < HEAD_DIM
    zero = jnp.zeros_like(block)
    return jnp.concatenate([jnp.where(low, block, zero), jnp.where(low, zero, block)], axis=0)


def _own_head_rows(cross):
    head_of_row = lax.broadcasted_iota(jnp.int32, cross.shape, 0) >= HEAD_DIM
    head_of_lane = lax.broadcasted_iota(jnp.int32, cross.shape, 1) >= SUPER
    both = jnp.where(jnp.logical_xor(head_of_row, head_of_lane), 0.0, cross).T
    return both[:SUPER] + both[SUPER:]


def _with_mask_lane(q_rows):
    lane = lax.broadcasted_iota(jnp.int32, q_rows.shape, 1)
    return jnp.concatenate([q_rows, jnp.where(lane == 0, MASK_VALUE, 0.0).astype(q_rows.dtype)], axis=1)


def _band_exp(kb, q_rows, bias):
    s = _dot_t(kb, _with_mask_lane(q_rows)) + bias
    e = jnp.exp(s - jnp.max(s, axis=0, keepdims=True))
    return e, jnp.sum(e, axis=0, keepdims=True)


def _window(sc):
    return slice(WIN_BASE + sc * SUPER, WIN_BASE + sc * SUPER + WINDOW)


def _shift_band(i, band_ref, new_ref):
    @pl.when(i == 0)
    def _():
        band_ref[:TA] = jnp.zeros((TA, GROUP), band_ref.dtype)

    @pl.when(i > 0)
    def _():
        band_ref[:TA] = band_ref[TA:]

    band_ref[TA:] = new_ref[...].astype(band_ref.dtype)


def _shift_key_band(i, band_ref, new_ref):
    @pl.when(i == 0)
    def _():
        lane = lax.broadcasted_iota(jnp.int32, (TA, 2 * GROUP), 1)
        band_ref[:TA] = jnp.where(lane == GROUP, 1.0, 0.0).astype(band_ref.dtype)
        band_ref[TA:, GROUP:] = jnp.zeros((TA, GROUP), band_ref.dtype)

    @pl.when(i > 0)
    def _():
        band_ref[:TA] = band_ref[TA:]

    band_ref[TA:, :GROUP] = new_ref[...].astype(band_ref.dtype)


def _shift_band_t(i, band_ref, new_ref):
    @pl.when(i == 0)
    def _():
        band_ref[:, :TA] = jnp.zeros((GROUP, TA), band_ref.dtype)

    @pl.when(i > 0)
    def _():
        band_ref[:, :TA] = band_ref[:, TA:]

    band_ref[:, TA:] = new_ref[...].astype(band_ref.dtype).T


def _scaled_queries(q_ref, rows):
    return _by_head((q_ref[rows] * SCALE).astype(BF16))


def _attn_fwd(proj_bf, bias_t):
    seq = proj_bf.shape[0]
    n_tiles = seq // TA

    def body(q_ref, k_ref, v_ref, ag_ref, bias_ref, a_ref, ya_ref, e_ref, inv_ref, kband, vband_t):
        i = pl.program_id(1)
        _shift_key_band(i, kband, k_ref)
        _shift_band_t(i, vband_t, v_ref)

        def weights(sc):
            rows = slice(sc * SUPER, (sc + 1) * SUPER)
            win = _window(sc)
            e, total = _band_exp(kband[win], _scaled_queries(q_ref, rows), bias_ref[0])
            e_ref[0, sc] = e.astype(BF16)
            inv_total = 1.0 / total
            inv_ref[0, sc] = jnp.broadcast_to(inv_total, (8, PAIR_LANES))
            return inv_total

        nxt = weights(0)
        for sc in range(SUPERS_PER_TILE):
            rows = slice(sc * SUPER, (sc + 1) * SUPER)
            win = _window(sc)
            inv_total = nxt
            if sc + 1 < SUPERS_PER_TILE:
                nxt = weights(sc + 1)
            a = _own_head_rows(_dot(vband_t[:, win], e_ref[0, sc]) * inv_total)
            a_ref[rows] = a.astype(BF16)
            g = ag_ref[rows].astype(F32)
            ya_ref[rows] = (a * (g * _sigmoid(g))).astype(BF16)

    blk = pl.BlockSpec((TA, GROUP), lambda p, i: (i, p))

    def cols(first):
        return pl.BlockSpec((TA, GROUP), lambda p, i: (i, first + p))

    return pl.pallas_call(
        body,
        name="attn_fwd",
        grid=(N_PAIR, n_tiles),
        in_specs=[cols(Q_BLOCK), cols(K_BLOCK), cols(V_BLOCK), cols(AG_BLOCK),
                  pl.BlockSpec((1, WINDOW, PAIR_LANES), lambda p, i: (p, 0, 0))],
        out_specs=[blk, blk,
                   pl.BlockSpec((1, SUPERS_PER_TILE, WINDOW, PAIR_LANES), lambda p, i: (p, i, 0, 0)),
                   pl.BlockSpec((1, SUPERS_PER_TILE, 8, PAIR_LANES), lambda p, i: (p, i, 0, 0))],
        out_shape=[jax.ShapeDtypeStruct((seq, ATTN_WIDTH), BF16), jax.ShapeDtypeStruct((seq, ATTN_WIDTH), BF16),
                   jax.ShapeDtypeStruct((N_PAIR, seq // SUPER, WINDOW, PAIR_LANES), BF16),
                   jax.ShapeDtypeStruct((N_PAIR, seq // SUPER, 8, PAIR_LANES), F32)],
        scratch_shapes=[pltpu.VMEM((2 * TA, 2 * GROUP), BF16), pltpu.VMEM((GROUP, 2 * TA), BF16)],
        compiler_params=pltpu.CompilerParams(
            dimension_semantics=("arbitrary", "arbitrary"), vmem_limit_bytes=VMEM_LIMIT),
    )(proj_bf, proj_bf, proj_bf, proj_bf, bias_t)


def _attn_bwd(proj_bf, a, dy, e_all, inv_all):
    seq = proj_bf.shape[0]
    n_tiles = seq // TA

    def body(q_ref, k_ref, v_ref, a_ref, ag_ref, dy_ref, e_ref, inv_ref,
             dq_ref, dk_ref, dv_ref, dag_ref, db_ref, vband, kband_t, dkacc, dvacc):
        i = pl.program_id(1)

        @pl.when(i == 0)
        def _():
            dkacc[...] = jnp.zeros_like(dkacc)
            dvacc[...] = jnp.zeros_like(dvacc)
            db_ref[...] = jnp.zeros_like(db_ref)

        @pl.when(i < n_tiles)
        def _():
            _shift_band(i, vband, v_ref)
            _shift_band_t(i, kband_t, k_ref)

            def score_grads(sc):
                rows = slice(sc * SUPER, (sc + 1) * SUPER)
                win = _window(sc)
                q_rows = _scaled_queries(q_ref, rows)
                g = ag_ref[rows].astype(F32)
                sg = _sigmoid(g)
                dyc = dy_ref[rows].astype(F32)
                dag_ref[rows] = (dyc * a_ref[rows].astype(F32) * (sg * (1.0 + g * (1.0 - sg)))).astype(BF16)
                da_rows = _by_head((dyc * (g * sg)).astype(BF16))
                p = e_ref[0, sc].astype(F32) * inv_ref[0, sc, :1]
                dp = _dot_t(vband[win], da_rows)
                ds = p * (dp - jnp.sum(p * dp, axis=0, keepdims=True))
                db_ref[0] += ds
                return q_rows, da_rows, p.astype(BF16), ds.astype(BF16)

            nxt = score_grads(0)
            for sc in range(SUPERS_PER_TILE):
                rows = slice(sc * SUPER, (sc + 1) * SUPER)
                win = _window(sc)
                q_rows, da_rows, p_bf, ds_bf = nxt
                if sc + 1 < SUPERS_PER_TILE:
                    nxt = score_grads(sc + 1)
                dq_ref[rows] = (_own_head_rows(_dot(kband_t[:, win], ds_bf)) * SCALE).astype(BF16)
                dkacc[win] += _dot(ds_bf, q_rows)
                dvacc[win] += _dot(p_bf, da_rows)

        dk_ref[...] = dkacc[:TA].astype(BF16)
        dv_ref[...] = dvacc[:TA].astype(BF16)
        dkacc[:TA] = dkacc[TA:]
        dvacc[:TA] = dvacc[TA:]
        dkacc[TA:] = jnp.zeros((TA, GROUP), F32)
        dvacc[TA:] = jnp.zeros((TA, GROUP), F32)

    last = n_tiles - 1
    cur = pl.BlockSpec((TA, GROUP), lambda p, i: (jnp.minimum(i, last), p))
    older = pl.BlockSpec((TA, GROUP), lambda p, i: (jnp.maximum(i - 1, 0), p))
    dy_blk = pl.BlockSpec((TA, GROUP), lambda p, i: (jnp.minimum(i, last), N_PAIR + p))
    per_pair = pl.BlockSpec((1, WINDOW, PAIR_LANES), lambda p, i: (p, 0, 0))

    def cols(first):
        return pl.BlockSpec((TA, GROUP), lambda p, i: (jnp.minimum(i, last), first + p))

    def kept(rows):
        return pl.BlockSpec((1, SUPERS_PER_TILE, rows, PAIR_LANES), lambda p, i: (p, jnp.minimum(i, last), 0, 0))

    def out(dtype):
        return jax.ShapeDtypeStruct((seq, ATTN_WIDTH), dtype)

    return pl.pallas_call(
        body,
        name="attn_bwd",
        grid=(N_PAIR, n_tiles + 1),
        in_specs=[cols(Q_BLOCK), cols(K_BLOCK), cols(V_BLOCK), cur, cols(AG_BLOCK), dy_blk, kept(WINDOW), kept(8)],
        out_specs=[cur, older, older, cur, per_pair],
        out_shape=[out(BF16), out(BF16), out(BF16), out(BF16),
                   jax.ShapeDtypeStruct((N_PAIR, WINDOW, PAIR_LANES), F32)],
        scratch_shapes=[
            pltpu.VMEM((2 * TA, GROUP), BF16), pltpu.VMEM((GROUP, 2 * TA), BF16),
            pltpu.VMEM((2 * TA, GROUP), F32), pltpu.VMEM((2 * TA, GROUP), F32)],
        compiler_params=pltpu.CompilerParams(
            dimension_semantics=("arbitrary", "arbitrary"), vmem_limit_bytes=VMEM_LIMIT),
    )(proj_bf, proj_bf, proj_bf, a, proj_bf, dy, e_all, inv_all)


BIN_ROWS = 136


def _bias_bins(db_t):
    idx_t = jnp.asarray(_rel_index_tile())

    def body(db_ref, idx_ref, out_ref):
        lane = lax.broadcasted_iota(jnp.int32, (1, GROUP), 1)
        row = lax.broadcasted_iota(jnp.int32, (BIN_ROWS, GROUP), 0)
        out = jnp.zeros((BIN_ROWS, GROUP), F32)
        for r in range(N_REL - 1):
            lo = 0 if r == 0 else ((BAND - 2 * CHUNK + r) // 8) * 8
            hi = WINDOW if r == 0 else min(WINDOW, lo + SUPER + 8)
            hit = jnp.where(idx_ref[lo:hi] == r, db_ref[0, lo:hi], 0.0)
            col = jnp.sum(hit, axis=0, keepdims=True)
            s0 = jnp.sum(col[:, :SUPER], axis=1, keepdims=True)
            s1 = jnp.sum(col[:, SUPER:], axis=1, keepdims=True)
            val = jnp.where(lane == 0, s0, jnp.where(lane == 1, s1, 0.0))
            out = jnp.where(row == r, val, out)
        out_ref[0] = out

    return pl.pallas_call(
        body,
        name="bias_bins",
        grid=(N_PAIR,),
        in_specs=[pl.BlockSpec((1, WINDOW, PAIR_LANES), lambda p: (p, 0, 0)), VMEM],
        out_specs=pl.BlockSpec((1, BIN_ROWS, GROUP), lambda p: (p, 0, 0)),
        out_shape=jax.ShapeDtypeStruct((N_PAIR, BIN_ROWS, GROUP), F32),
        compiler_params=pltpu.CompilerParams(dimension_semantics=("arbitrary",)),
    )(db_t, idx_t)


def _out_loss(x, tgt, proj, ya, wout_full, g2, pw_bf, ps):
    seq = x.shape[0]
    to = min(seq, 2 * TS)
    n_tiles = seq // to

    def body(x_ref, t_ref, pv_ref, pg_ref, ya_ref, w_ref, g2_ref, pw_ref, ps_ref,
             dx2_ref, dy_ref, gw_ref, gg_ref, loss_ref, sq_ref, halo_ref, wt_ref):
        i = pl.program_id(0)

        @pl.when(i == 0)
        def _():
            gw_ref[...] = jnp.zeros_like(gw_ref)
            gg_ref[...] = jnp.zeros_like(gg_ref)
            sq_ref[...] = jnp.zeros_like(sq_ref)
            halo_ref[...] = jnp.zeros_like(halo_ref)
            for b in range(N_CHIP):
                wt_ref[b] = w_ref[b].T

        pv = pv_ref[...].astype(F32)
        pg = pg_ref[...].astype(F32)
        diffs = [d.astype(BF16) for d in _pool_diffs(pv, halo_ref[...], i)]
        halo_ref[...] = pv[to - HALO:, :]
        yp = ((_pool_mix(diffs, pw_ref) * ps_ref[...]) * (pg * _sigmoid(pg))).astype(BF16)
        g2v = g2_ref[...]

        def parts(rows):
            return [yp[rows, :SHARD_OUT], yp[rows, SHARD_OUT:], ya_ref[rows, :SHARD_OUT], ya_ref[rows, SHARD_OUT:]]

        def project(rows, ys):
            x2 = x_ref[rows]
            for b in range(N_CHIP):
                x2 = x2 + _dot(ys[b], w_ref[b])
            return x2

        def norm_loss(rows, x2):
            r = lax.rsqrt(jnp.mean(x2 * x2, axis=-1, keepdims=True) + EPS)
            xh = x2 * r
            diff = xh * g2v - t_ref[rows]
            sq_ref[...] += jnp.sum(diff * diff, axis=0, keepdims=True)
            dfin = diff * (1.0 / D_MODEL)
            gg_ref[...] += jnp.sum(dfin * xh, axis=0, keepdims=True)
            dxh = dfin * g2v
            dx2 = r * (dxh - xh * jnp.mean(dxh * xh, axis=-1, keepdims=True))
            dx2_ref[rows] = dx2
            return dx2.astype(BF16)

        def back(rows, ys, dx2_bf):
            for b in range(N_CHIP):
                gw_ref[b] += _tdot(ys[b], dx2_bf)
                dy_ref[rows, b * SHARD_OUT:(b + 1) * SHARD_OUT] = _dot(dx2_bf, wt_ref[b]).astype(BF16)

        n_parts = 2
        part = to // n_parts
        spans = [slice(r * part, (r + 1) * part) for r in range(n_parts)]
        ys = [parts(rows) for rows in spans]
        x2_next = project(spans[0], ys[0])
        for r in range(n_parts):
            x2 = x2_next
            if r + 1 < n_parts:
                x2_next = project(spans[r + 1], ys[r + 1])
            back(spans[r], ys[r], norm_loss(spans[r], x2))

        @pl.when(i == n_tiles - 1)
        def _():
            total = jnp.sum(sq_ref[...], axis=1, keepdims=True) * (0.5 / D_MODEL)
            loss_ref[...] = jnp.broadcast_to(total, loss_ref.shape)

    def rows(width, col=0):
        return pl.BlockSpec((to, width), lambda i: (i, col))

    return pl.pallas_call(
        body,
        name="out_loss",
        grid=(n_tiles,),
        in_specs=[rows(D_MODEL), rows(D_MODEL), rows(POOL_WIDTH, 0), rows(POOL_WIDTH, 1), rows(ATTN_WIDTH),
                  VMEM, VMEM, VMEM, VMEM],
        out_specs=[rows(D_MODEL), rows(D_MODEL), VMEM, VMEM, VMEM],
        out_shape=[
            jax.ShapeDtypeStruct((seq, D_MODEL), F32), jax.ShapeDtypeStruct((seq, D_MODEL), BF16),
            jax.ShapeDtypeStruct((N_CHIP, SHARD_OUT, D_MODEL), F32), jax.ShapeDtypeStruct((1, D_MODEL), F32),
            jax.ShapeDtypeStruct((8, GROUP), F32)],
        scratch_shapes=[pltpu.VMEM((1, D_MODEL), F32), pltpu.VMEM((HALO, POOL_WIDTH), F32),
                        pltpu.VMEM((N_CHIP, D_MODEL, SHARD_OUT), BF16)],
        compiler_params=pltpu.CompilerParams(dimension_semantics=("arbitrary",), vmem_limit_bytes=VMEM_LIMIT),
    )(x, tgt, proj, proj, ya, wout_full, g2, pw_bf, ps)


def _inproj_bwd(x, dx2, dy, proj, dq, dk, dv, dag, win_full, pw_bf, g1, ps):
    seq = x.shape[0]
    n_tiles = seq // TS

    def body(x_ref, dx2_ref, dyp_ref, pv_ref, pvprev_ref, pg_ref, dq_ref, dk_ref, dv_ref, dag_ref,
             w_ref, pw_ref, g1_ref, ps_ref, gx_ref, dproj_ref, ht_ref, gg_ref, gps_ref, gpw_ref, halo_ref):
        i = pl.program_id(0)
        tile = n_tiles - 1 - i

        @pl.when(i == 0)
        def _():
            gg_ref[...] = jnp.zeros_like(gg_ref)
            gps_ref[...] = jnp.zeros_like(gps_ref)
            gpw_ref[...] = jnp.zeros_like(gpw_ref)
            halo_ref[...] = jnp.zeros_like(halo_ref)

        pv_t = pv_ref[...].astype(F32)
        pg_t = pg_ref[...].astype(F32)
        prev_rows = jnp.where(tile > 0, pvprev_ref[...].astype(F32), 0.0)
        diffs = [d.astype(BF16) for d in _pool_diffs(pv_t, prev_rows, tile)]
        mixed = _pool_mix(diffs, pw_ref)
        sg = _sigmoid(pg_t)
        silu = pg_t * sg
        dyp = dyp_ref[...].astype(F32)
        psv = ps_ref[...]
        gps_ref[...] += jnp.sum(dyp * mixed * silu, axis=0, keepdims=True)
        dmixed = (dyp * psv * silu).astype(BF16)
        dpg = dyp * (mixed * psv) * (sg * (1.0 + pg_t * (1.0 - sg)))
        counts = _row_counts(tile, TS)
        dds = []
        for g in range(len(POOL_WINDOWS)):
            dm_g = dmixed[:, g * GROUP:(g + 1) * GROUP]
            gpw_ref[g] += _tdot(diffs[g], dm_g)
            dds.append(_dot_t(dm_g, pw_ref[g]))
        dd = jnp.concatenate(dds, axis=1)
        spread = jnp.concatenate([dds[g] / counts[g] for g in range(len(POOL_WINDOWS))], axis=1)
        sums = _window_sums(jnp.concatenate([spread, halo_ref[...]], axis=0), forward=True)
        halo_ref[...] = spread[:HALO]
        dpv = jnp.concatenate(
            [sums[g][:TS, g * GROUP:(g + 1) * GROUP] for g in range(len(POOL_WINDOWS))], axis=1) - dd

        xt = x_ref[...]
        r = lax.rsqrt(jnp.mean(xt * xt, axis=-1, keepdims=True) + EPS)
        xh = xt * r
        g1v = g1_ref[...]
        ht_ref[...] = (xh * g1v).astype(BF16).T
        dproj = jnp.concatenate(
            [dpv.astype(BF16), dpg.astype(BF16), dq_ref[...], dk_ref[...], dv_ref[...], dag_ref[...]],
            axis=1)
        dproj_ref[...] = dproj
        dh = _dot_t(dproj[:, :SHARD_IN], w_ref[0])
        for chip in range(1, N_CHIP):
            dh = dh + _dot_t(dproj[:, chip * SHARD_IN:(chip + 1) * SHARD_IN], w_ref[chip])

        gg_ref[...] += jnp.sum(dh * xh, axis=0, keepdims=True)
        dxh = dh * g1v
        gx_ref[...] = dx2_ref[...] + r * (dxh - xh * jnp.mean(dxh * xh, axis=-1, keepdims=True))

    def rows(width, col=0):
        return pl.BlockSpec((TS, width), lambda i: (n_tiles - 1 - i, col))

    prev = pl.BlockSpec((HALO, POOL_WIDTH), lambda i: (jnp.maximum((n_tiles - 1 - i) * (TS // HALO) - 1, 0), 0))
    return pl.pallas_call(
        body,
        name="inproj_bwd",
        grid=(n_tiles,),
        in_specs=[rows(D_MODEL), rows(D_MODEL), rows(POOL_WIDTH), rows(POOL_WIDTH), prev, rows(POOL_WIDTH, 1),
                  rows(ATTN_WIDTH), rows(ATTN_WIDTH), rows(ATTN_WIDTH), rows(ATTN_WIDTH), VMEM, VMEM, VMEM, VMEM],
        out_specs=[rows(D_MODEL), rows(N_CHIP * SHARD_IN),
                   pl.BlockSpec((D_MODEL, TS), lambda i: (0, n_tiles - 1 - i)), VMEM, VMEM, VMEM],
        out_shape=[
            jax.ShapeDtypeStruct((seq, D_MODEL), F32),
            jax.ShapeDtypeStruct((seq, N_CHIP * SHARD_IN), BF16),
            jax.ShapeDtypeStruct((D_MODEL, seq), BF16),
            jax.ShapeDtypeStruct((1, D_MODEL), F32),
            jax.ShapeDtypeStruct((1, POOL_WIDTH), F32),
            jax.ShapeDtypeStruct((len(POOL_WINDOWS), GROUP, GROUP), F32)],
        scratch_shapes=[pltpu.VMEM((HALO, POOL_WIDTH), F32)],
        compiler_params=pltpu.CompilerParams(dimension_semantics=("arbitrary",), vmem_limit_bytes=VMEM_LIMIT),
    )(x, dx2, dy, proj, proj, proj, dq, dk, dv, dag, win_full, pw_bf, g1, ps)


PASS_PEER = (2, 0, 1)


def _gw_reduce(ht, dproj, gwout, small):
    seq = ht.shape[1]
    tm = min(2 * TS, seq // 4)
    n_tiles = seq // tm
    half_small = SMALL_ROWS // 2
    cx, cy = lax.axis_index("x"), lax.axis_index("y")
    outer = _other_chips(cx, cy)
    order = jnp.stack([outer[n][1] for n in PASS_PEER] + [2 * cx + cy]).astype(jnp.int32)

    def body(order_ref, ht_ref, dp_ref, gwout_ref, small_ref, gin_final, gout_final, small_final,
             hbuf, acc, pair_in, pair_out, pair_small, tx_in, tx_out, rx_in, rx_out, rx_small,
             gin_out, gout_out, small_out, send_sems, recv_sems, out_sems):
        j = pl.program_id(0)
        i = pl.program_id(1)
        x, y, c = _my_place()
        b = 2 * x + y
        sibling = (x, y, 1 - c)
        others = _other_chips(x, y)
        mine_in = pl.ds(pl.multiple_of(c * HALF_IN, HALF_IN), HALF_IN)
        mine_out = pl.ds(pl.multiple_of(c * HALF_OUT, HALF_OUT), HALF_OUT)
        mine_small = pl.ds(pl.multiple_of(c * half_small, 8), half_small)
        theirs_in = pl.ds(pl.multiple_of((1 - c) * HALF_IN, HALF_IN), HALF_IN)
        theirs_out = pl.ds(pl.multiple_of((1 - c) * HALF_OUT, HALF_OUT), HALF_OUT)
        theirs_small = pl.ds(pl.multiple_of((1 - c) * half_small, 8), half_small)

        def copy(k, src, dst, to):
            return pltpu.make_async_remote_copy(
                src_ref=src, dst_ref=dst, send_sem=send_sems.at[k], recv_sem=recv_sems.at[k],
                device_id=to, device_id_type=MESH)

        swap_out = copy(0, gwout_ref.at[:, theirs_out], pair_out, sibling)
        swap_small = copy(1, small_ref, pair_small, sibling)

        def swap_in(p):
            return copy(2 + p, acc.at[p % 2, theirs_in], pair_in.at[p % 2], sibling)

        def to_chip(n, t):
            to = (*others[n][0], c)
            if t == 0:
                return copy(6 + 3 * n, tx_in.at[n], rx_in.at[n], to)
            if t == 1:
                return copy(7 + 3 * n, tx_out.at[n], rx_out.at[n], to)
            return copy(8 + 3 * n, pair_small.at[mine_small], rx_small.at[b], to)

        share_in = copy(15, gin_out.at[mine_in], gin_out.at[mine_in], sibling)
        share_out = copy(16, gout_out.at[mine_out], gout_out.at[mine_out], sibling)
        share_small = copy(17, small_out.at[mine_small], small_out.at[mine_small], sibling)

        def at(jj, ii):
            return (j == jj) & (i == ii)

        par = j % 2

        @pl.when(i == 0)
        def _():
            acc[par] = jnp.zeros((D_MODEL, SHARD_IN), F32)

        cols = pl.ds(pl.multiple_of(i * tm, tm), tm)

        @pl.when(j == 0)
        def _():
            hbuf[:, cols] = ht_ref[...]

        acc[par] += _dot(hbuf[:, cols], dp_ref[...])

        @pl.when(at(0, 0))
        def _():
            swap_out.start()
            swap_small.start()

        @pl.when(at(0, 2))
        def _():
            swap_out.wait_recv()
            swap_small.wait_recv()
            for chip in range(N_CHIP):
                pair_out[chip] = gwout_ref[chip, mine_out] + pair_out[chip]
            pair_small[...] = small_ref[...] + pair_small[...]
            rx_small[b] = pair_small[mine_small]
            for n in range(N_CHIP - 1):
                tx_out[n] = pair_out[others[n][1]].astype(BF16)
                to_chip(n, 1).start()
                to_chip(n, 2).start()

        @pl.when(at(1, 3))
        def _():
            total_out = pair_out[b]
            for n in range(N_CHIP - 1):
                to_chip(n, 1).wait_recv()
                copy(8 + 3 * n, pair_small.at[mine_small], rx_small.at[others[n][1]], (*others[n][0], c)).wait_recv()
                total_out = total_out + rx_out[n].astype(F32)
            gout_out[mine_out] = total_out
            small_out[mine_small] = ((rx_small[0] + rx_small[1]) + rx_small[2]) + rx_small[3]
            share_out.start()
            share_small.start()

        for p in range(N_CHIP - 1):
            n = PASS_PEER[p]

            @pl.when(at(p + 1, 0))
            def _(p=p):
                swap_in(p).start()

            @pl.when(at(p + 1, 2))
            def _(p=p, n=n):
                swap_in(p).wait_recv()
                swap_in(p).wait_send()
                tx_in[n] = (acc[p % 2, mine_in] + pair_in[p % 2]).astype(BF16)
                to_chip(n, 0).start()

        @pl.when(at(N_CHIP - 1, n_tiles - 1))
        def _():
            last = N_CHIP - 1
            swap_in(last).start()
            swap_in(last).wait_recv()
            total_in = acc[last % 2, mine_in] + pair_in[last % 2]
            for n in range(N_CHIP - 1):
                to_chip(n, 0).wait_recv()
                total_in = total_in + rx_in[n].astype(F32)
            gin_out[mine_in] = total_in
            share_in.start()
            copy(15, gin_out.at[theirs_in], gin_out.at[theirs_in], sibling).wait_recv()
            copy(16, gout_out.at[theirs_out], gout_out.at[theirs_out], sibling).wait_recv()
            copy(17, small_out.at[theirs_small], small_out.at[theirs_small], sibling).wait_recv()
            swap_out.wait_send()
            swap_small.wait_send()
            swap_in(last).wait_send()
            for n in range(N_CHIP - 1):
                for t in range(3):
                    to_chip(n, t).wait_send()
            share_in.wait_send()
            share_out.wait_send()
            share_small.wait_send()
            outs = [pltpu.make_async_copy(src, dst, out_sems.at[k]) for k, (src, dst) in enumerate(
                [(gin_out, gin_final), (gout_out, gout_final), (small_out, small_final)])]
            for cp in outs:
                cp.start()
            for cp in outs:
                cp.wait()

    assert n_tiles >= 4, "the reduction's steps are spread over the first four token steps of a pass"
    grid_spec = pltpu.PrefetchScalarGridSpec(
        num_scalar_prefetch=1,
        grid=(N_CHIP, n_tiles),
        in_specs=[
            pl.BlockSpec((D_MODEL, tm), lambda j, i, o: (0, jnp.where(j == 0, i, n_tiles - 1))),
            pl.BlockSpec((tm, SHARD_IN), lambda j, i, o: (i, o[j])),
            VMEM, VMEM],
        out_specs=[ANY, ANY, ANY],
        scratch_shapes=[
            pltpu.VMEM((D_MODEL, seq), BF16),
            pltpu.VMEM((2, D_MODEL, SHARD_IN), F32),
            pltpu.VMEM((2, HALF_IN, SHARD_IN), F32),
            pltpu.VMEM((N_CHIP, HALF_OUT, D_MODEL), F32),
            pltpu.VMEM((SMALL_ROWS, GROUP), F32),
            pltpu.VMEM((N_CHIP - 1, HALF_IN, SHARD_IN), BF16),
            pltpu.VMEM((N_CHIP - 1, HALF_OUT, D_MODEL), BF16),
            pltpu.VMEM((N_CHIP - 1, HALF_IN, SHARD_IN), BF16),
            pltpu.VMEM((N_CHIP - 1, HALF_OUT, D_MODEL), BF16),
            pltpu.VMEM((N_CHIP, half_small, GROUP), F32),
            pltpu.VMEM((D_MODEL, SHARD_IN), F32),
            pltpu.VMEM((SHARD_OUT, D_MODEL), F32),
            pltpu.VMEM((SMALL_ROWS, GROUP), F32),
            pltpu.SemaphoreType.DMA((18,)),
            pltpu.SemaphoreType.DMA((18,)),
            pltpu.SemaphoreType.DMA((3,)),
        ],
    )
    return pl.pallas_call(
        body,
        name="gw_reduce",
        grid_spec=grid_spec,
        out_shape=(
            jax.ShapeDtypeStruct((D_MODEL, SHARD_IN), F32),
            jax.ShapeDtypeStruct((SHARD_OUT, D_MODEL), F32),
            jax.ShapeDtypeStruct((SMALL_ROWS, GROUP), F32),
        ),
        compiler_params=pltpu.CompilerParams(
            dimension_semantics=("arbitrary", "arbitrary"), vmem_limit_bytes=VMEM_LIMIT),
    )(order, ht, dproj, gwout, small)


def _adamw(name, w, g, m, v, block_rows):
    rows, cols = w.shape

    def body(w_ref, g_ref, m_ref, v_ref, d_ref, m_out, v_out):
        grad = g_ref[...]
        m_new = ADAM_B1 * m_ref[...] + (1.0 - ADAM_B1) * grad
        v_new = ADAM_B2 * v_ref[...] + (1.0 - ADAM_B2) * (grad * grad)
        m_hat = m_new / (1.0 - ADAM_B1 ** ADAM_STEP)
        v_hat = v_new / (1.0 - ADAM_B2 ** ADAM_STEP)
        d_ref[...] = -ADAM_LR * (m_hat / (jnp.sqrt(v_hat) + ADAM_EPS) + ADAM_WD * w_ref[...])
        m_out[...] = m_new
        v_out[...] = v_new

    blk = pl.BlockSpec((block_rows, cols), lambda i: (i, 0))
    shape = jax.ShapeDtypeStruct((rows, cols), F32)
    return pl.pallas_call(
        body,
        name=name,
        grid=(rows // block_rows,),
        in_specs=[blk] * 4,
        out_specs=[blk] * 3,
        out_shape=[shape] * 3,
        compiler_params=pltpu.CompilerParams(dimension_semantics=("arbitrary",)),
    )(w, g, m, v)


def _pack_small(norm_gain, pool_w, pool_scale, rel_bias, final_gain, loss_rows):
    parts = [
        norm_gain.reshape(8, GROUP),
        pool_w.reshape(len(POOL_WINDOWS) * GROUP, GROUP),
        jnp.pad(pool_scale.reshape(4, GROUP), ((0, 4), (0, 0))),
        jnp.pad(rel_bias.reshape(8, N_REL), ((0, 0), (0, 2 * GROUP - N_REL))).reshape(16, GROUP),
        final_gain.reshape(8, GROUP),
        loss_rows,
    ]
    return jnp.concatenate(parts, axis=0)


def _unpack_small(block):
    norm_gain = block[0:8].reshape(1, D_MODEL)
    pool_w = block[8:520].reshape(1, len(POOL_WINDOWS), GROUP, GROUP)
    pool_scale = block[520:524].reshape(1, POOL_WIDTH)
    rel_bias = block[528:544].reshape(8, 2 * GROUP)[:, :N_REL].reshape(1, 8, N_REL)
    final_gain = block[544:552].reshape(D_MODEL)
    return norm_gain, pool_w, pool_scale, rel_bias, final_gain


def kernel(x, norm_gain, w_in, pool_w, pool_scale, rel_bias, w_out, final_norm_gain, loss_target, m_norm_gain, m_w_in, m_pool_w, m_pool_scale, m_rel_bias, m_w_out, m_final_norm_gain, v_norm_gain, v_w_in, v_pool_w, v_pool_scale, v_rel_bias, v_w_out, v_final_norm_gain):
    assert x.shape[1] % TS == 0 and x.shape[2] == D_MODEL
    xs = x[0]
    tgt = loss_target[0]
    g1 = norm_gain.reshape(1, D_MODEL)
    g2 = final_norm_gain.reshape(1, D_MODEL)
    ps = pool_scale.reshape(1, POOL_WIDTH)
    pw_bf = pool_w[0].astype(BF16)

    proj, win_full, wout_full = _gather_inproj(xs, g1, w_in[0].astype(BF16), w_out[0].astype(BF16))
    bias_t = _bias_tile(rel_bias[0])

    a, ya, e_all, inv_all = _attn_fwd(proj, bias_t)
    dx2, dy, gwout, gg2, loss_rows = _out_loss(xs, tgt, proj, ya, wout_full, g2, pw_bf, ps)
    dq, dk, dv, dag, db_t = _attn_bwd(proj, a, dy, e_all, inv_all)
    bins = _bias_bins(db_t)
    gx, dproj, ht, gg1, gps, gpw = _inproj_bwd(xs, dx2, dy, proj, dq, dk, dv, dag, win_full, pw_bf, g1, ps)

    g_bias = bins[:, :N_REL, :2].transpose(0, 2, 1).reshape(8, N_REL)
    small = _pack_small(gg1, gpw, gps, g_bias, gg2, loss_rows)
    g_win, g_wout, g_small = _gw_reduce(ht, dproj, gwout, small)
    loss = g_small[LOSS_ROW, 0]

    zeros8 = jnp.zeros((8, GROUP), F32)
    w_small = _pack_small(norm_gain, pool_w, pool_scale, rel_bias, final_norm_gain, zeros8)
    m_small = _pack_small(m_norm_gain, m_pool_w, m_pool_scale, m_rel_bias, m_final_norm_gain, zeros8)
    v_small = _pack_small(v_norm_gain, v_pool_w, v_pool_scale, v_rel_bias, v_final_norm_gain, zeros8)

    d_win, m_win, v_win = _adamw("adamw_w_in", w_in[0], g_win, m_w_in[0], v_w_in[0], 256)
    d_wout, m_wout, v_wout = _adamw("adamw_w_out", w_out[0], g_wout, m_w_out[0], v_w_out[0], 128)
    d_small, m_new_small, v_new_small = _adamw("adamw_small", w_small, g_small, m_small, v_small, SMALL_ROWS // 2)

    def full(win_part, wout_part, block):
        ng, pw, psc, rb, fg = _unpack_small(block)
        return [ng, win_part[None], pw, psc, rb, wout_part[None], fg]

    grads = full(g_win, g_wout, g_small)
    deltas = full(d_win, d_wout, d_small)
    new_m = full(m_win, m_wout, m_new_small)
    new_v = full(v_win, v_wout, v_new_small)
    return (loss, gx[None], *grads, *deltas, *new_m, *new_v)
```

```python
import numpy as np
import jax
import jax.numpy as jnp
from jax import lax
from jax.experimental import pallas as pl
from jax.experimental.pallas import tpu as pltpu

F32 = jnp.float32
BF16 = jnp.bfloat16

D_MODEL = 1024
POOL_WIDTH = 512
ATTN_WIDTH = 512
POOL_WINDOWS = (2, 4, 8, 16)
GROUP = 128
CHUNK = 64
LEFT_CHUNKS = 8
BAND = (LEFT_CHUNKS + 1) * CHUNK
HEAD_DIM = 64
N_PAIR = 4
MAX_REL = 64
N_REL = 2 * MAX_REL + 1
EPS = 1e-6
MASK_VALUE = -1e30
SCALE = 0.125

ADAM_LR = 0.001
ADAM_B1 = 0.9
ADAM_B2 = 0.999
ADAM_EPS = 1e-08
ADAM_WD = 0.01
ADAM_STEP = 10

TS = LEFT_CHUNKS * CHUNK
SUPER = 2 * CHUNK
WINDOW = BAND + CHUNK
TA = 4 * TS
WIN_BASE = TA - LEFT_CHUNKS * CHUNK
SUPERS_PER_TILE = TA // SUPER
PAIR_LANES = 2 * SUPER
HALO = 16
N_CHIP = 4
SHARD_IN = 768
SHARD_OUT = 256
PIECE = 256
PIECES_PER_SHARD = SHARD_IN // PIECE
HALF_IN = D_MODEL // 2
HALF_OUT = SHARD_OUT // 2
SMALL_ROWS = 560
LOSS_ROW = 552
VMEM_LIMIT = 60 * 1024 * 1024

MESH = pl.DeviceIdType.MESH
ANY = pl.BlockSpec(memory_space=pl.ANY)
VMEM = pl.BlockSpec(memory_space=pltpu.VMEM)


def _sigmoid(x):
    return 1.0 / (1.0 + jnp.exp(-x))


def _dot(a, b):
    return jnp.dot(a, b, preferred_element_type=F32)


def _dot_t(a, b):
    return lax.dot_general(a, b, (((1,), (1,)), ((), ())), preferred_element_type=F32)


def _tdot(a, b):
    return lax.dot_general(a, b, (((0,), (0,)), ((), ())), preferred_element_type=F32)


def _my_place():
    return lax.axis_index("x"), lax.axis_index("y"), lax.axis_index("c")


def _other_chips(x, y):
    places = [(1 - x, y), (x, 1 - y), (1 - x, 1 - y)]
    return [(p, 2 * p[0] + p[1]) for p in places]


def _gather_inproj(x, g1, win_sh, wout_sh):
    seq = x.shape[0]
    tm = min(seq, 4 * TS)
    n_tiles = seq // tm
    cx, cy = lax.axis_index("x"), lax.axis_index("y")
    order = jnp.stack([2 * cx + cy] + [chip for _, chip in _other_chips(cx, cy)]).astype(jnp.int32)

    def body(order_ref, x_ref, g1_ref, win_ref, wout_ref, proj_bf_ref, win_full, wout_full,
             hbuf, wbuf, send_sems, recv_sems, local_sems):
        j = pl.program_id(0)
        i = pl.program_id(1)
        x_, y_, c = _my_place()
        b = 2 * x_ + y_
        sibling = (x_, y_, 1 - c)
        others = _other_chips(x_, y_)

        def halves(chip, core):
            return (
                win_full.at[chip, pl.ds(core * HALF_IN, HALF_IN)],
                wout_full.at[chip, pl.ds(core * HALF_OUT, HALF_OUT)],
            )

        def copy(k, src, dst, to):
            return pltpu.make_async_remote_copy(
                src_ref=src, dst_ref=dst, send_sem=send_sems.at[k], recv_sem=recv_sems.at[k],
                device_id=to, device_id_type=MESH)

        own = [
            pltpu.make_async_copy(win_ref, win_full.at[b], local_sems.at[0]),
            pltpu.make_async_copy(wout_ref, wout_full.at[b], local_sems.at[1]),
        ]
        mine_src = (win_ref.at[pl.ds(c * HALF_IN, HALF_IN)], wout_ref.at[pl.ds(c * HALF_OUT, HALF_OUT)])

        def direct(n, t):
            return copy(2 * n + t, mine_src[t], halves(b, c)[t], (*others[n][0], c))

        def arrival(n, t):
            landed = halves(others[n][1], c)[t]
            return copy(2 * n + t, landed, landed, (*others[n][0], c))

        def passing(n, t):
            landed = halves(others[n][1], c)[t]
            return copy(6 + 2 * n + t, landed, landed, sibling)

        def from_sibling(n, t):
            landed = halves(others[n][1], 1 - c)[t]
            return copy(6 + 2 * n + t, landed, landed, sibling)

        @pl.when((j == 0) & (i == 0))
        def _():
            first = pltpu.make_async_copy(win_ref, wbuf.at[0], local_sems.at[2])
            first.start()
            for cp in own:
                cp.start()
            for t in range(2):
                for n in range(N_CHIP - 1):
                    direct(n, t).start()
            first.wait()

        def load(n):
            return pltpu.make_async_copy(win_full.at[others[n][1]], wbuf.at[(n + 1) % 2], local_sems.at[2])

        for n in range(N_CHIP - 1):
            @pl.when((j == n) & (i == n_tiles - 1))
            def _(n=n):
                arrival(n, 0).wait_recv()
                passing(n, 0).start()

            @pl.when((j == n + 1) & (i == 0))
            def _(n=n):
                load(n).wait()

        rows = pl.ds(pl.multiple_of(i * tm, tm), tm)

        @pl.when(j == 0)
        def _():
            xt = x_ref[...]
            r = lax.rsqrt(jnp.mean(xt * xt, axis=-1, keepdims=True) + EPS)
            hbuf[rows] = ((xt * r) * g1_ref[...]).astype(BF16)

        proj_bf_ref[...] = _dot(hbuf[rows], wbuf[j % 2]).astype(BF16)

        for n in range(N_CHIP - 1):
            @pl.when((j == n) & (i == n_tiles - 1))
            def _(n=n):
                from_sibling(n, 0).wait_recv()
                load(n).start()

        @pl.when((j == N_CHIP - 1) & (i == n_tiles - 1))
        def _():
            for n in range(N_CHIP - 1):
                arrival(n, 1).wait_recv()
                passing(n, 1).start()
            for n in range(N_CHIP - 1):
                from_sibling(n, 1).wait_recv()
            for n in range(N_CHIP - 1):
                for t in range(2):
                    direct(n, t).wait_send()
                    passing(n, t).wait_send()
            for cp in own:
                cp.wait()

    last = n_tiles - 1
    grid_spec = pltpu.PrefetchScalarGridSpec(
        num_scalar_prefetch=1,
        grid=(N_CHIP, n_tiles),
        in_specs=[
            pl.BlockSpec((tm, D_MODEL), lambda j, i, o: (jnp.where(j == 0, i, last), 0)),
            VMEM, ANY, ANY],
        out_specs=[pl.BlockSpec((tm, SHARD_IN), lambda j, i, o: (i, o[j])), ANY, ANY],
        scratch_shapes=[
            pltpu.VMEM((seq, D_MODEL), BF16), pltpu.VMEM((2, D_MODEL, SHARD_IN), BF16),
            pltpu.SemaphoreType.DMA((12,)), pltpu.SemaphoreType.DMA((12,)), pltpu.SemaphoreType.DMA((3,))],
    )
    return pl.pallas_call(
        body,
        name="gather_inproj",
        grid_spec=grid_spec,
        out_shape=(
            jax.ShapeDtypeStruct((seq, N_CHIP * SHARD_IN), BF16),
            jax.ShapeDtypeStruct((N_CHIP, D_MODEL, SHARD_IN), BF16),
            jax.ShapeDtypeStruct((N_CHIP, SHARD_OUT, D_MODEL), BF16),
        ),
        compiler_params=pltpu.CompilerParams(
            dimension_semantics=("arbitrary", "arbitrary"), vmem_limit_bytes=VMEM_LIMIT),
    )(order, x, g1, win_sh, wout_sh)


def _window_sums(ext, forward):
    n = ext.shape[0]
    sums = []
    acc = ext
    for step in (1, 2, 4, 8):
        acc = acc + pltpu.roll(acc, (n - step) if forward else step, 0)
        sums.append(acc)
    return sums


def _row_counts(tile, rows):
    t = tile * rows + lax.broadcasted_iota(jnp.int32, (rows, GROUP), 0)
    return [jnp.minimum(t + 1, w).astype(F32) for w in POOL_WINDOWS]


def _pool_diffs(pv, prev_rows, tile):
    ext = jnp.concatenate([prev_rows, pv], axis=0)
    sums = _window_sums(ext, forward=False)
    counts = _row_counts(tile, pv.shape[0])
    out = []
    for g in range(len(POOL_WINDOWS)):
        cols = slice(g * GROUP, (g + 1) * GROUP)
        out.append(sums[g][HALO:, cols] / counts[g] - pv[:, cols])
    return out


def _pool_mix(diffs, pw_ref):
    return jnp.concatenate([_dot(diffs[g], pw_ref[g]) for g in range(len(POOL_WINDOWS))], axis=1)


Q_BLOCK, K_BLOCK, V_BLOCK, AG_BLOCK = 8, 12, 16, 20


def _bias_tile(rel_bias):
    flat = jnp.concatenate(
        [jnp.broadcast_to(rel_bias[:, :1], (rel_bias.shape[0], BAND - CHUNK - 1)), rel_bias[:, :2 * MAX_REL]], axis=1)
    rows = [flat[:, CHUNK - 1 - i:CHUNK - 1 - i + BAND] for i in range(CHUNK)]
    bias = jnp.stack(rows, axis=1)
    first = jnp.pad(bias, ((0, 0), (0, 0), (0, CHUNK)), constant_values=MASK_VALUE)
    second = jnp.pad(bias, ((0, 0), (0, 0), (CHUNK, 0)), constant_values=MASK_VALUE)
    both = jnp.concatenate([first, second], axis=1)
    return both.reshape(N_PAIR, PAIR_LANES, WINDOW).transpose(0, 2, 1)


def _rel_index_tile():
    j = np.arange(WINDOW)[:, None]
    q = np.arange(PAIR_LANES)[None, :] % SUPER
    band_key = j - CHUNK * (q // CHUNK)
    idx = np.clip(band_key - LEFT_CHUNKS * CHUNK - q % CHUNK, -MAX_REL, MAX_REL) + MAX_REL
    return np.where((band_key >= 0) & (band_key < BAND), idx, -1).astype(np.int32)


def _by_head(block):
    low = lax.broadcasted_iota(jnp.int32, block.shape, 1) < HEAD_DIM
    zero = jnp.zeros_like(block)
    return jnp.concatenate([jnp.where(low, block, zero), jnp.where(low, zero, block)], axis=0)


def _own_head_rows(cross):
    head_of_row = lax.broadcasted_iota(jnp.int32, cross.shape, 0) >= HEAD_DIM
    head_of_lane = lax.broadcasted_iota(jnp.int32, cross.shape, 1) >= SUPER
    both = jnp.where(jnp.logical_xor(head_of_row, head_of_lane), 0.0, cross).T
    return both[:SUPER] + both[SUPER:]


def _with_mask_lane(q_rows):
    lane = lax.broadcasted_iota(jnp.int32, q_rows.shape, 1)
    return jnp.concatenate([q_rows, jnp.where(lane == 0, MASK_VALUE, 0.0).astype(q_rows.dtype)], axis=1)


def _band_exp(kb, q_rows, bias):
    s = _dot_t(kb, _with_mask_lane(q_rows)) + bias
    e = jnp.exp(s - jnp.max(s, axis=0, keepdims=True))
    return e, jnp.sum(e, axis=0, keepdims=True)


def _window(sc):
    return slice(WIN_BASE + sc * SUPER, WIN_BASE + sc * SUPER + WINDOW)


def _shift_band(i, band_ref, new_ref):
    @pl.when(i == 0)
    def _():
        band_ref[:TA] = jnp.zeros((TA, GROUP), band_ref.dtype)

    @pl.when(i > 0)
    def _():
        band_ref[:TA] = band_ref[TA:]

    band_ref[TA:] = new_ref[...].astype(band_ref.dtype)


def _shift_key_band(i, band_ref, new_ref):
    @pl.when(i == 0)
    def _():
        lane = lax.broadcasted_iota(jnp.int32, (TA, 2 * GROUP), 1)
        band_ref[:TA] = jnp.where(lane == GROUP, 1.0, 0.0).astype(band_ref.dtype)
        band_ref[TA:, GROUP:] = jnp.zeros((TA, GROUP), band_ref.dtype)

    @pl.when(i > 0)
    def _():
        band_ref[:TA] = band_ref[TA:]

    band_ref[TA:, :GROUP] = new_ref[...].astype(band_ref.dtype)


def _shift_band_t(i, band_ref, new_ref):
    @pl.when(i == 0)
    def _():
        band_ref[:, :TA] = jnp.zeros((GROUP, TA), band_ref.dtype)

    @pl.when(i > 0)
    def _():
        band_ref[:, :TA] = band_ref[:, TA:]

    band_ref[:, TA:] = new_ref[...].astype(band_ref.dtype).T


def _scaled_queries(q_ref, rows):
    return _by_head((q_ref[rows] * SCALE).astype(BF16))


def _attn_fwd(proj_bf, bias_t):
    seq = proj_bf.shape[0]
    n_tiles = seq // TA

    def body(q_ref, k_ref, v_ref, ag_ref, bias_ref, a_ref, ya_ref, e_ref, inv_ref, kband, vband_t):
        i = pl.program_id(1)
        _shift_key_band(i, kband, k_ref)
        _shift_band_t(i, vband_t, v_ref)

        def weights(sc):
            rows = slice(sc * SUPER, (sc + 1) * SUPER)
            win = _window(sc)
            e, total = _band_exp(kband[win], _scaled_queries(q_ref, rows), bias_ref[0])
            e_ref[0, sc] = e.astype(BF16)
            inv_total = 1.0 / total
            inv_ref[0, sc] = jnp.broadcast_to(inv_total, (8, PAIR_LANES))
            return inv_total

        nxt = weights(0)
        for sc in range(SUPERS_PER_TILE):
            rows = slice(sc * SUPER, (sc + 1) * SUPER)
            win = _window(sc)
            inv_total = nxt
            if sc + 1 < SUPERS_PER_TILE:
                nxt = weights(sc + 1)
            a = _own_head_rows(_dot(vband_t[:, win], e_ref[0, sc]) * inv_total)
            a_ref[rows] = a.astype(BF16)
            g = ag_ref[rows].astype(F32)
            ya_ref[rows] = (a * (g * _sigmoid(g))).astype(BF16)

    blk = pl.BlockSpec((TA, GROUP), lambda p, i: (i, p))

    def cols(first):
        return pl.BlockSpec((TA, GROUP), lambda p, i: (i, first + p))

    return pl.pallas_call(
        body,
        name="attn_fwd",
        grid=(N_PAIR, n_tiles),
        in_specs=[cols(Q_BLOCK), cols(K_BLOCK), cols(V_BLOCK), cols(AG_BLOCK),
                  pl.BlockSpec((1, WINDOW, PAIR_LANES), lambda p, i: (p, 0, 0))],
        out_specs=[blk, blk,
                   pl.BlockSpec((1, SUPERS_PER_TILE, WINDOW, PAIR_LANES), lambda p, i: (p, i, 0, 0)),
                   pl.BlockSpec((1, SUPERS_PER_TILE, 8, PAIR_LANES), lambda p, i: (p, i, 0, 0))],
        out_shape=[jax.ShapeDtypeStruct((seq, ATTN_WIDTH), BF16), jax.ShapeDtypeStruct((seq, ATTN_WIDTH), BF16),
                   jax.ShapeDtypeStruct((N_PAIR, seq // SUPER, WINDOW, PAIR_LANES), BF16),
                   jax.ShapeDtypeStruct((N_PAIR, seq // SUPER, 8, PAIR_LANES), F32)],
        scratch_shapes=[pltpu.VMEM((2 * TA, 2 * GROUP), BF16), pltpu.VMEM((GROUP, 2 * TA), BF16)],
        compiler_params=pltpu.CompilerParams(
            dimension_semantics=("arbitrary", "arbitrary"), vmem_limit_bytes=VMEM_LIMIT),
    )(proj_bf, proj_bf, proj_bf, proj_bf, bias_t)


def _attn_bwd(proj_bf, a, dy, e_all, inv_all):
    seq = proj_bf.shape[0]
    n_tiles = seq // TA

    def body(q_ref, k_ref, v_ref, a_ref, ag_ref, dy_ref, e_ref, inv_ref,
             dq_ref, dk_ref, dv_ref, dag_ref, db_ref, vband, kband_t, dkacc, dvacc):
        i = pl.program_id(1)

        @pl.when(i == 0)
        def _():
            dkacc[...] = jnp.zeros_like(dkacc)
            dvacc[...] = jnp.zeros_like(dvacc)
            db_ref[...] = jnp.zeros_like(db_ref)

        @pl.when(i < n_tiles)
        def _():
            _shift_band(i, vband, v_ref)
            _shift_band_t(i, kband_t, k_ref)

            def score_grads(sc):
                rows = slice(sc * SUPER, (sc + 1) * SUPER)
                win = _window(sc)
                q_rows = _scaled_queries(q_ref, rows)
                g = ag_ref[rows].astype(F32)
                sg = _sigmoid(g)
                dyc = dy_ref[rows].astype(F32)
                dag_ref[rows] = (dyc * a_ref[rows].astype(F32) * (sg * (1.0 + g * (1.0 - sg)))).astype(BF16)
                da_rows = _by_head((dyc * (g * sg)).astype(BF16))
                p = e_ref[0, sc].astype(F32) * inv_ref[0, sc, :1]
                dp = _dot_t(vband[win], da_rows)
                ds = p * (dp - jnp.sum(p * dp, axis=0, keepdims=True))
                db_ref[0] += ds
                return q_rows, da_rows, p.astype(BF16), ds.astype(BF16)

            nxt = score_grads(0)
            for sc in range(SUPERS_PER_TILE):
                rows = slice(sc * SUPER, (sc + 1) * SUPER)
                win = _window(sc)
                q_rows, da_rows, p_bf, ds_bf = nxt
                if sc + 1 < SUPERS_PER_TILE:
                    nxt = score_grads(sc + 1)
                dq_ref[rows] = (_own_head_rows(_dot(kband_t[:, win], ds_bf)) * SCALE).astype(BF16)
                dkacc[win] += _dot(ds_bf, q_rows)
                dvacc[win] += _dot(p_bf, da_rows)

        dk_ref[...] = dkacc[:TA].astype(BF16)
        dv_ref[...] = dvacc[:TA].astype(BF16)
        dkacc[:TA] = dkacc[TA:]
        dvacc[:TA] = dvacc[TA:]
        dkacc[TA:] = jnp.zeros((TA, GROUP), F32)
        dvacc[TA:] = jnp.zeros((TA, GROUP), F32)

    last = n_tiles - 1
    cur = pl.BlockSpec((TA, GROUP), lambda p, i: (jnp.minimum(i, last), p))
    older = pl.BlockSpec((TA, GROUP), lambda p, i: (jnp.maximum(i - 1, 0), p))
    dy_blk = pl.BlockSpec((TA, GROUP), lambda p, i: (jnp.minimum(i, last), N_PAIR + p))
    per_pair = pl.BlockSpec((1, WINDOW, PAIR_LANES), lambda p, i: (p, 0, 0))

    def cols(first):
        return pl.BlockSpec((TA, GROUP), lambda p, i: (jnp.minimum(i, last), first + p))

    def kept(rows):
        return pl.BlockSpec((1, SUPERS_PER_TILE, rows, PAIR_LANES), lambda p, i: (p, jnp.minimum(i, last), 0, 0))

    def out(dtype):
        return jax.ShapeDtypeStruct((seq, ATTN_WIDTH), dtype)

    return pl.pallas_call(
        body,
        name="attn_bwd",
        grid=(N_PAIR, n_tiles + 1),
        in_specs=[cols(Q_BLOCK), cols(K_BLOCK), cols(V_BLOCK), cur, cols(AG_BLOCK), dy_blk, kept(WINDOW), kept(8)],
        out_specs=[cur, older, older, cur, per_pair],
        out_shape=[out(BF16), out(BF16), out(BF16), out(BF16),
                   jax.ShapeDtypeStruct((N_PAIR, WINDOW, PAIR_LANES), F32)],
        scratch_shapes=[
            pltpu.VMEM((2 * TA, GROUP), BF16), pltpu.VMEM((GROUP, 2 * TA), BF16),
            pltpu.VMEM((2 * TA, GROUP), F32), pltpu.VMEM((2 * TA, GROUP), F32)],
        compiler_params=pltpu.CompilerParams(
            dimension_semantics=("arbitrary", "arbitrary"), vmem_limit_bytes=VMEM_LIMIT),
    )(proj_bf, proj_bf, proj_bf, a, proj_bf, dy, e_all, inv_all)


BIN_ROWS = 136


def _bias_bins(db_t):
    idx_t = jnp.asarray(_rel_index_tile())

    def body(db_ref, idx_ref, out_ref):
        lane = lax.broadcasted_iota(jnp.int32, (1, GROUP), 1)
        row = lax.broadcasted_iota(jnp.int32, (BIN_ROWS, GROUP), 0)
        out = jnp.zeros((BIN_ROWS, GROUP), F32)
        for r in range(N_REL - 1):
            lo = 0 if r == 0 else ((BAND - 2 * CHUNK + r) // 8) * 8
            hi = WINDOW if r == 0 else min(WINDOW, lo + SUPER + 8)
            hit = jnp.where(idx_ref[lo:hi] == r, db_ref[0, lo:hi], 0.0)
            col = jnp.sum(hit, axis=0, keepdims=True)
            s0 = jnp.sum(col[:, :SUPER], axis=1, keepdims=True)
            s1 = jnp.sum(col[:, SUPER:], axis=1, keepdims=True)
            val = jnp.where(lane == 0, s0, jnp.where(lane == 1, s1, 0.0))
            out = jnp.where(row == r, val, out)
        out_ref[0] = out

    return pl.pallas_call(
        body,
        name="bias_bins",
        grid=(N_PAIR,),
        in_specs=[pl.BlockSpec((1, WINDOW, PAIR_LANES), lambda p: (p, 0, 0)), VMEM],
        out_specs=pl.BlockSpec((1, BIN_ROWS, GROUP), lambda p: (p, 0, 0)),
        out_shape=jax.ShapeDtypeStruct((N_PAIR, BIN_ROWS, GROUP), F32),
        compiler_params=pltpu.CompilerParams(dimension_semantics=("arbitrary",)),
    )(db_t, idx_t)


def _out_loss(x, tgt, proj, ya, wout_full, g2, pw_bf, ps):
    seq = x.shape[0]
    to = min(seq, 2 * TS)
    n_tiles = seq // to

    def body(x_ref, t_ref, pv_ref, pg_ref, ya_ref, w_ref, g2_ref, pw_ref, ps_ref,
             dx2_ref, dy_ref, gw_ref, gg_ref, loss_ref, sq_ref, halo_ref, wt_ref):
        i = pl.program_id(0)

        @pl.when(i == 0)
        def _():
            gw_ref[...] = jnp.zeros_like(gw_ref)
            gg_ref[...] = jnp.zeros_like(gg_ref)
            sq_ref[...] = jnp.zeros_like(sq_ref)
            halo_ref[...] = jnp.zeros_like(halo_ref)
            for b in range(N_CHIP):
                wt_ref[b] = w_ref[b].T

        g2v = g2_ref[...]
        n_parts = 4
        part = to // n_parts
        last_rows = halo_ref[...]
        halo_ref[...] = pv_ref[to - HALO:, :].astype(F32)

        def parts(r, rows):
            pv = pv_ref[rows].astype(F32)
            pg = pg_ref[rows].astype(F32)
            before = last_rows if r == 0 else pv_ref[rows.start - HALO:rows.start].astype(F32)
            diffs = [d.astype(BF16) for d in _pool_diffs(pv, before, i * n_parts + r)]
            yp = ((_pool_mix(diffs, pw_ref) * ps_ref[...]) * (pg * _sigmoid(pg))).astype(BF16)
            return [yp[:, :SHARD_OUT], yp[:, SHARD_OUT:], ya_ref[rows, :SHARD_OUT], ya_ref[rows, SHARD_OUT:]]

        def project(rows, ys):
            x2 = x_ref[rows]
            for b in range(N_CHIP):
                x2 = x2 + _dot(ys[b], w_ref[b])
            return x2

        def norm_loss(rows, x2):
            r = lax.rsqrt(jnp.mean(x2 * x2, axis=-1, keepdims=True) + EPS)
            xh = x2 * r
            diff = xh * g2v - t_ref[rows]
            sq_ref[...] += jnp.sum(diff * diff, axis=0, keepdims=True)
            dfin = diff * (1.0 / D_MODEL)
            gg_ref[...] += jnp.sum(dfin * xh, axis=0, keepdims=True)
            dxh = dfin * g2v
            dx2 = r * (dxh - xh * jnp.mean(dxh * xh, axis=-1, keepdims=True))
            dx2_ref[rows] = dx2
            return dx2.astype(BF16)

        def back(rows, ys, dx2_bf):
            for b in range(N_CHIP):
                gw_ref[b] += _tdot(ys[b], dx2_bf)
                dy_ref[rows, b * SHARD_OUT:(b + 1) * SHARD_OUT] = _dot(dx2_bf, wt_ref[b]).astype(BF16)

        spans = [slice(r * part, (r + 1) * part) for r in range(n_parts)]
        ys = {0: parts(0, spans[0])}
        x2_next = project(spans[0], ys[0])
        for r in range(n_parts):
            x2 = x2_next
            if r + 1 < n_parts:
                ys[r + 1] = parts(r + 1, spans[r + 1])
                x2_next = project(spans[r + 1], ys[r + 1])
            back(spans[r], ys.pop(r), norm_loss(spans[r], x2))

        @pl.when(i == n_tiles - 1)
        def _():
            total = jnp.sum(sq_ref[...], axis=1, keepdims=True) * (0.5 / D_MODEL)
            loss_ref[...] = jnp.broadcast_to(total, loss_ref.shape)

    def rows(width, col=0):
        return pl.BlockSpec((to, width), lambda i: (i, col))

    return pl.pallas_call(
        body,
        name="out_loss",
        grid=(n_tiles,),
        in_specs=[rows(D_MODEL), rows(D_MODEL), rows(POOL_WIDTH, 0), rows(POOL_WIDTH, 1), rows(ATTN_WIDTH),
                  VMEM, VMEM, VMEM, VMEM],
        out_specs=[rows(D_MODEL), rows(D_MODEL), VMEM, VMEM, VMEM],
        out_shape=[
            jax.ShapeDtypeStruct((seq, D_MODEL), F32), jax.ShapeDtypeStruct((seq, D_MODEL), BF16),
            jax.ShapeDtypeStruct((N_CHIP, SHARD_OUT, D_MODEL), F32), jax.ShapeDtypeStruct((1, D_MODEL), F32),
            jax.ShapeDtypeStruct((8, GROUP), F32)],
        scratch_shapes=[pltpu.VMEM((1, D_MODEL), F32), pltpu.VMEM((HALO, POOL_WIDTH), F32),
                        pltpu.VMEM((N_CHIP, D_MODEL, SHARD_OUT), BF16)],
        compiler_params=pltpu.CompilerParams(dimension_semantics=("arbitrary",), vmem_limit_bytes=VMEM_LIMIT),
    )(x, tgt, proj, proj, ya, wout_full, g2, pw_bf, ps)


def _inproj_bwd(x, dx2, dy, proj, dq, dk, dv, dag, win_full, pw_bf, g1, ps):
    seq = x.shape[0]
    n_tiles = seq // TS

    def body(x_ref, dx2_ref, dyp_ref, pv_ref, pvprev_ref, pg_ref, dq_ref, dk_ref, dv_ref, dag_ref,
             w_ref, pw_ref, g1_ref, ps_ref, gx_ref, dproj_ref, ht_ref, gg_ref, gps_ref, gpw_ref, halo_ref):
        i = pl.program_id(0)
        tile = n_tiles - 1 - i

        @pl.when(i == 0)
        def _():
            gg_ref[...] = jnp.zeros_like(gg_ref)
            gps_ref[...] = jnp.zeros_like(gps_ref)
            gpw_ref[...] = jnp.zeros_like(gpw_ref)
            halo_ref[...] = jnp.zeros_like(halo_ref)

        pv_t = pv_ref[...].astype(F32)
        pg_t = pg_ref[...].astype(F32)
        prev_rows = jnp.where(tile > 0, pvprev_ref[...].astype(F32), 0.0)
        diffs = [d.astype(BF16) for d in _pool_diffs(pv_t, prev_rows, tile)]
        mixed = _pool_mix(diffs, pw_ref)
        sg = _sigmoid(pg_t)
        silu = pg_t * sg
        dyp = dyp_ref[...].astype(F32)
        psv = ps_ref[...]
        gps_ref[...] += jnp.sum(dyp * mixed * silu, axis=0, keepdims=True)
        dmixed = (dyp * psv * silu).astype(BF16)
        dpg = dyp * (mixed * psv) * (sg * (1.0 + pg_t * (1.0 - sg)))
        counts = _row_counts(tile, TS)
        dds = []
        for g in range(len(POOL_WINDOWS)):
            dm_g = dmixed[:, g * GROUP:(g + 1) * GROUP]
            gpw_ref[g] += _tdot(diffs[g], dm_g)
            dds.append(_dot_t(dm_g, pw_ref[g]))
        dd = jnp.concatenate(dds, axis=1)
        spread = jnp.concatenate([dds[g] / counts[g] for g in range(len(POOL_WINDOWS))], axis=1)
        sums = _window_sums(jnp.concatenate([spread, halo_ref[...]], axis=0), forward=True)
        halo_ref[...] = spread[:HALO]
        dpv = jnp.concatenate(
            [sums[g][:TS, g * GROUP:(g + 1) * GROUP] for g in range(len(POOL_WINDOWS))], axis=1) - dd

        xt = x_ref[...]
        r = lax.rsqrt(jnp.mean(xt * xt, axis=-1, keepdims=True) + EPS)
        xh = xt * r
        g1v = g1_ref[...]
        ht_ref[...] = (xh * g1v).astype(BF16).T
        dproj = jnp.concatenate(
            [dpv.astype(BF16), dpg.astype(BF16), dq_ref[...], dk_ref[...], dv_ref[...], dag_ref[...]],
            axis=1)
        dproj_ref[...] = dproj
        dh = _dot_t(dproj[:, :SHARD_IN], w_ref[0])
        for chip in range(1, N_CHIP):
            dh = dh + _dot_t(dproj[:, chip * SHARD_IN:(chip + 1) * SHARD_IN], w_ref[chip])

        gg_ref[...] += jnp.sum(dh * xh, axis=0, keepdims=True)
        dxh = dh * g1v
        gx_ref[...] = dx2_ref[...] + r * (dxh - xh * jnp.mean(dxh * xh, axis=-1, keepdims=True))

    def rows(width, col=0):
        return pl.BlockSpec((TS, width), lambda i: (n_tiles - 1 - i, col))

    prev = pl.BlockSpec((HALO, POOL_WIDTH), lambda i: (jnp.maximum((n_tiles - 1 - i) * (TS // HALO) - 1, 0), 0))
    return pl.pallas_call(
        body,
        name="inproj_bwd",
        grid=(n_tiles,),
        in_specs=[rows(D_MODEL), rows(D_MODEL), rows(POOL_WIDTH), rows(POOL_WIDTH), prev, rows(POOL_WIDTH, 1),
                  rows(ATTN_WIDTH), rows(ATTN_WIDTH), rows(ATTN_WIDTH), rows(ATTN_WIDTH), VMEM, VMEM, VMEM, VMEM],
        out_specs=[rows(D_MODEL), rows(N_CHIP * SHARD_IN),
                   pl.BlockSpec((D_MODEL, TS), lambda i: (0, n_tiles - 1 - i)), VMEM, VMEM, VMEM],
        out_shape=[
            jax.ShapeDtypeStruct((seq, D_MODEL), F32),
            jax.ShapeDtypeStruct((seq, N_CHIP * SHARD_IN), BF16),
            jax.ShapeDtypeStruct((D_MODEL, seq), BF16),
            jax.ShapeDtypeStruct((1, D_MODEL), F32),
            jax.ShapeDtypeStruct((1, POOL_WIDTH), F32),
            jax.ShapeDtypeStruct((len(POOL_WINDOWS), GROUP, GROUP), F32)],
        scratch_shapes=[pltpu.VMEM((HALO, POOL_WIDTH), F32)],
        compiler_params=pltpu.CompilerParams(dimension_semantics=("arbitrary",), vmem_limit_bytes=VMEM_LIMIT),
    )(x, dx2, dy, proj, proj, proj, dq, dk, dv, dag, win_full, pw_bf, g1, ps)


PASS_PEER = (2, 0, 1)


def _gw_reduce(ht, dproj, gwout, small):
    seq = ht.shape[1]
    tm = min(2 * TS, seq // 4)
    n_tiles = seq // tm
    half_small = SMALL_ROWS // 2
    cx, cy = lax.axis_index("x"), lax.axis_index("y")
    outer = _other_chips(cx, cy)
    order = jnp.stack([outer[n][1] for n in PASS_PEER] + [2 * cx + cy]).astype(jnp.int32)

    def body(order_ref, ht_ref, dp_ref, gwout_ref, small_ref, gin_final, gout_final, small_final,
             hbuf, acc, pair_in, pair_out, pair_small, tx_in, tx_out, rx_in, rx_out, rx_small,
             gin_out, gout_out, small_out, send_sems, recv_sems, out_sems):
        j = pl.program_id(0)
        i = pl.program_id(1)
        x, y, c = _my_place()
        b = 2 * x + y
        sibling = (x, y, 1 - c)
        others = _other_chips(x, y)
        mine_in = pl.ds(pl.multiple_of(c * HALF_IN, HALF_IN), HALF_IN)
        mine_out = pl.ds(pl.multiple_of(c * HALF_OUT, HALF_OUT), HALF_OUT)
        mine_small = pl.ds(pl.multiple_of(c * half_small, 8), half_small)
        theirs_in = pl.ds(pl.multiple_of((1 - c) * HALF_IN, HALF_IN), HALF_IN)
        theirs_out = pl.ds(pl.multiple_of((1 - c) * HALF_OUT, HALF_OUT), HALF_OUT)
        theirs_small = pl.ds(pl.multiple_of((1 - c) * half_small, 8), half_small)

        def copy(k, src, dst, to):
            return pltpu.make_async_remote_copy(
                src_ref=src, dst_ref=dst, send_sem=send_sems.at[k], recv_sem=recv_sems.at[k],
                device_id=to, device_id_type=MESH)

        swap_out = copy(0, gwout_ref.at[:, theirs_out], pair_out, sibling)
        swap_small = copy(1, small_ref, pair_small, sibling)

        def swap_in(p):
            return copy(2 + p, acc.at[p % 2, theirs_in], pair_in.at[p % 2], sibling)

        def to_chip(n, t):
            to = (*others[n][0], c)
            if t == 0:
                return copy(6 + 3 * n, tx_in.at[n], rx_in.at[n], to)
            if t == 1:
                return copy(7 + 3 * n, tx_out.at[n], rx_out.at[n], to)
            return copy(8 + 3 * n, pair_small.at[mine_small], rx_small.at[b], to)

        share_in = copy(15, gin_out.at[mine_in], gin_out.at[mine_in], sibling)
        share_out = copy(16, gout_out.at[mine_out], gout_out.at[mine_out], sibling)
        share_small = copy(17, small_out.at[mine_small], small_out.at[mine_small], sibling)

        def at(jj, ii):
            return (j == jj) & (i == ii)

        par = j % 2

        @pl.when(i == 0)
        def _():
            acc[par] = jnp.zeros((D_MODEL, SHARD_IN), F32)

        cols = pl.ds(pl.multiple_of(i * tm, tm), tm)

        @pl.when(j == 0)
        def _():
            hbuf[:, cols] = ht_ref[...]

        acc[par] += _dot(hbuf[:, cols], dp_ref[...])

        @pl.when(at(0, 0))
        def _():
            swap_out.start()
            swap_small.start()

        @pl.when(at(0, 2))
        def _():
            swap_out.wait_recv()
            swap_small.wait_recv()
            for chip in range(N_CHIP):
                pair_out[chip] = gwout_ref[chip, mine_out] + pair_out[chip]
            pair_small[...] = small_ref[...] + pair_small[...]
            rx_small[b] = pair_small[mine_small]
            for n in range(N_CHIP - 1):
                tx_out[n] = pair_out[others[n][1]].astype(BF16)
                to_chip(n, 1).start()
                to_chip(n, 2).start()

        @pl.when(at(1, 3))
        def _():
            total_out = pair_out[b]
            for n in range(N_CHIP - 1):
                to_chip(n, 1).wait_recv()
                copy(8 + 3 * n, pair_small.at[mine_small], rx_small.at[others[n][1]], (*others[n][0], c)).wait_recv()
                total_out = total_out + rx_out[n].astype(F32)
            gout_out[mine_out] = total_out
            small_out[mine_small] = ((rx_small[0] + rx_small[1]) + rx_small[2]) + rx_small[3]
            share_out.start()
            share_small.start()

        for p in range(N_CHIP - 1):
            n = PASS_PEER[p]

            @pl.when(at(p + 1, 0))
            def _(p=p):
                swap_in(p).start()

            @pl.when(at(p + 1, 2))
            def _(p=p, n=n):
                swap_in(p).wait_recv()
                swap_in(p).wait_send()
                tx_in[n] = (acc[p % 2, mine_in] + pair_in[p % 2]).astype(BF16)
                to_chip(n, 0).start()

        @pl.when(at(N_CHIP - 1, n_tiles - 1))
        def _():
            last = N_CHIP - 1
            swap_in(last).start()
            swap_in(last).wait_recv()
            total_in = acc[last % 2, mine_in] + pair_in[last % 2]
            for n in range(N_CHIP - 1):
                to_chip(n, 0).wait_recv()
                total_in = total_in + rx_in[n].astype(F32)
            gin_out[mine_in] = total_in
            share_in.start()
            copy(15, gin_out.at[theirs_in], gin_out.at[theirs_in], sibling).wait_recv()
            copy(16, gout_out.at[theirs_out], gout_out.at[theirs_out], sibling).wait_recv()
            copy(17, small_out.at[theirs_small], small_out.at[theirs_small], sibling).wait_recv()
            swap_out.wait_send()
            swap_small.wait_send()
            swap_in(last).wait_send()
            for n in range(N_CHIP - 1):
                for t in range(3):
                    to_chip(n, t).wait_send()
            share_in.wait_send()
            share_out.wait_send()
            share_small.wait_send()
            outs = [pltpu.make_async_copy(src, dst, out_sems.at[k]) for k, (src, dst) in enumerate(
                [(gin_out, gin_final), (gout_out, gout_final), (small_out, small_final)])]
            for cp in outs:
                cp.start()
            for cp in outs:
                cp.wait()

    assert n_tiles >= 4, "the reduction's steps are spread over the first four token steps of a pass"
    grid_spec = pltpu.PrefetchScalarGridSpec(
        num_scalar_prefetch=1,
        grid=(N_CHIP, n_tiles),
        in_specs=[
            pl.BlockSpec((D_MODEL, tm), lambda j, i, o: (0, jnp.where(j == 0, i, n_tiles - 1))),
            pl.BlockSpec((tm, SHARD_IN), lambda j, i, o: (i, o[j])),
            VMEM, VMEM],
        out_specs=[ANY, ANY, ANY],
        scratch_shapes=[
            pltpu.VMEM((D_MODEL, seq), BF16),
            pltpu.VMEM((2, D_MODEL, SHARD_IN), F32),
            pltpu.VMEM((2, HALF_IN, SHARD_IN), F32),
            pltpu.VMEM((N_CHIP, HALF_OUT, D_MODEL), F32),
            pltpu.VMEM((SMALL_ROWS, GROUP), F32),
            pltpu.VMEM((N_CHIP - 1, HALF_IN, SHARD_IN), BF16),
            pltpu.VMEM((N_CHIP - 1, HALF_OUT, D_MODEL), BF16),
            pltpu.VMEM((N_CHIP - 1, HALF_IN, SHARD_IN), BF16),
            pltpu.VMEM((N_CHIP - 1, HALF_OUT, D_MODEL), BF16),
            pltpu.VMEM((N_CHIP, half_small, GROUP), F32),
            pltpu.VMEM((D_MODEL, SHARD_IN), F32),
            pltpu.VMEM((SHARD_OUT, D_MODEL), F32),
            pltpu.VMEM((SMALL_ROWS, GROUP), F32),
            pltpu.SemaphoreType.DMA((18,)),
            pltpu.SemaphoreType.DMA((18,)),
            pltpu.SemaphoreType.DMA((3,)),
        ],
    )
    return pl.pallas_call(
        body,
        name="gw_reduce",
        grid_spec=grid_spec,
        out_shape=(
            jax.ShapeDtypeStruct((D_MODEL, SHARD_IN), F32),
            jax.ShapeDtypeStruct((SHARD_OUT, D_MODEL), F32),
            jax.ShapeDtypeStruct((SMALL_ROWS, GROUP), F32),
        ),
        compiler_params=pltpu.CompilerParams(
            dimension_semantics=("arbitrary", "arbitrary"), vmem_limit_bytes=VMEM_LIMIT),
    )(order, ht, dproj, gwout, small)


def _adamw(name, w, g, m, v, block_rows):
    rows, cols = w.shape

    def body(w_ref, g_ref, m_ref, v_ref, d_ref, m_out, v_out):
        grad = g_ref[...]
        m_new = ADAM_B1 * m_ref[...] + (1.0 - ADAM_B1) * grad
        v_new = ADAM_B2 * v_ref[...] + (1.0 - ADAM_B2) * (grad * grad)
        m_hat = m_new / (1.0 - ADAM_B1 ** ADAM_STEP)
        v_hat = v_new / (1.0 - ADAM_B2 ** ADAM_STEP)
        d_ref[...] = -ADAM_LR * (m_hat / (jnp.sqrt(v_hat) + ADAM_EPS) + ADAM_WD * w_ref[...])
        m_out[...] = m_new
        v_out[...] = v_new

    blk = pl.BlockSpec((block_rows, cols), lambda i: (i, 0))
    shape = jax.ShapeDtypeStruct((rows, cols), F32)
    return pl.pallas_call(
        body,
        name=name,
        grid=(rows // block_rows,),
        in_specs=[blk] * 4,
        out_specs=[blk] * 3,
        out_shape=[shape] * 3,
        compiler_params=pltpu.CompilerParams(dimension_semantics=("arbitrary",)),
    )(w, g, m, v)


def _pack_small(norm_gain, pool_w, pool_scale, rel_bias, final_gain, loss_rows):
    parts = [
        norm_gain.reshape(8, GROUP),
        pool_w.reshape(len(POOL_WINDOWS) * GROUP, GROUP),
        jnp.pad(pool_scale.reshape(4, GROUP), ((0, 4), (0, 0))),
        jnp.pad(rel_bias.reshape(8, N_REL), ((0, 0), (0, 2 * GROUP - N_REL))).reshape(16, GROUP),
        final_gain.reshape(8, GROUP),
        loss_rows,
    ]
    return jnp.concatenate(parts, axis=0)


def _unpack_small(block):
    norm_gain = block[0:8].reshape(1, D_MODEL)
    pool_w = block[8:520].reshape(1, len(POOL_WINDOWS), GROUP, GROUP)
    pool_scale = block[520:524].reshape(1, POOL_WIDTH)
    rel_bias = block[528:544].reshape(8, 2 * GROUP)[:, :N_REL].reshape(1, 8, N_REL)
    final_gain = block[544:552].reshape(D_MODEL)
    return norm_gain, pool_w, pool_scale, rel_bias, final_gain


def kernel(x, norm_gain, w_in, pool_w, pool_scale, rel_bias, w_out, final_norm_gain, loss_target, m_norm_gain, m_w_in, m_pool_w, m_pool_scale, m_rel_bias, m_w_out, m_final_norm_gain, v_norm_gain, v_w_in, v_pool_w, v_pool_scale, v_rel_bias, v_w_out, v_final_norm_gain):
    assert x.shape[1] % TS == 0 and x.shape[2] == D_MODEL
    xs = x[0]
    tgt = loss_target[0]
    g1 = norm_gain.reshape(1, D_MODEL)
    g2 = final_norm_gain.reshape(1, D_MODEL)
    ps = pool_scale.reshape(1, POOL_WIDTH)
    pw_bf = pool_w[0].astype(BF16)

    proj, win_full, wout_full = _gather_inproj(xs, g1, w_in[0].astype(BF16), w_out[0].astype(BF16))
    bias_t = _bias_tile(rel_bias[0])

    a, ya, e_all, inv_all = _attn_fwd(proj, bias_t)
    dx2, dy, gwout, gg2, loss_rows = _out_loss(xs, tgt, proj, ya, wout_full, g2, pw_bf, ps)
    dq, dk, dv, dag, db_t = _attn_bwd(proj, a, dy, e_all, inv_all)
    bins = _bias_bins(db_t)
    gx, dproj, ht, gg1, gps, gpw = _inproj_bwd(xs, dx2, dy, proj, dq, dk, dv, dag, win_full, pw_bf, g1, ps)

    g_bias = bins[:, :N_REL, :2].transpose(0, 2, 1).reshape(8, N_REL)
    small = _pack_small(gg1, gpw, gps, g_bias, gg2, loss_rows)
    g_win, g_wout, g_small = _gw_reduce(ht, dproj, gwout, small)
    loss = g_small[LOSS_ROW, 0]

    zeros8 = jnp.zeros((8, GROUP), F32)
    w_small = _pack_small(norm_gain, pool_w, pool_scale, rel_bias, final_norm_gain, zeros8)
    m_small = _pack_small(m_norm_gain, m_pool_w, m_pool_scale, m_rel_bias, m_final_norm_gain, zeros8)
    v_small = _pack_small(v_norm_gain, v_pool_w, v_pool_scale, v_rel_bias, v_final_norm_gain, zeros8)

    d_win, m_win, v_win = _adamw("adamw_w_in", w_in[0], g_win, m_w_in[0], v_w_in[0], 256)
    d_wout, m_wout, v_wout = _adamw("adamw_w_out", w_out[0], g_wout, m_w_out[0], v_w_out[0], 128)
    d_small, m_new_small, v_new_small = _adamw("adamw_small", w_small, g_small, m_small, v_small, SMALL_ROWS // 2)

    def full(win_part, wout_part, block):
        ng, pw, psc, rb, fg = _unpack_small(block)
        return [ng, win_part[None], pw, psc, rb, wout_part[None], fg]

    grads = full(g_win, g_wout, g_small)
    deltas = full(d_win, d_wout, d_small)
    new_m = full(m_win, m_wout, m_new_small)
    new_v = full(v_win, v_wout, v_new_small)
    return (loss, gx[None], *grads, *deltas, *new_m, *new_v)
```

```python
import numpy as np
import jax
import jax.numpy as jnp
from jax import lax
from jax.experimental import pallas as pl
from jax.experimental.pallas import tpu as pltpu

F32 = jnp.float32
BF16 = jnp.bfloat16

D_MODEL = 1024
POOL_WIDTH = 512
ATTN_WIDTH = 512
POOL_WINDOWS = (2, 4, 8, 16)
GROUP = 128
CHUNK = 64
LEFT_CHUNKS = 8
BAND = (LEFT_CHUNKS + 1) * CHUNK
HEAD_DIM = 64
N_PAIR = 4
MAX_REL = 64
N_REL = 2 * MAX_REL + 1
EPS = 1e-6
MASK_VALUE = -1e30
SCALE = 0.125

ADAM_LR = 0.001
ADAM_B1 = 0.9
ADAM_B2 = 0.999
ADAM_EPS = 1e-08
ADAM_WD = 0.01
ADAM_STEP = 10

TS = LEFT_CHUNKS * CHUNK
SUPER = 2 * CHUNK
WINDOW = BAND + CHUNK
TA = 4 * TS
WIN_BASE = TA - LEFT_CHUNKS * CHUNK
SUPERS_PER_TILE = TA // SUPER
PAIR_LANES = 2 * SUPER
HALO = 16
N_CHIP = 4
SHARD_IN = 768
SHARD_OUT = 256
PIECE = 256
PIECES_PER_SHARD = SHARD_IN // PIECE
HALF_IN = D_MODEL // 2
HALF_OUT = SHARD_OUT // 2
SMALL_ROWS = 560
LOSS_ROW = 552
VMEM_LIMIT = 60 * 1024 * 1024

MESH = pl.DeviceIdType.MESH
ANY = pl.BlockSpec(memory_space=pl.ANY)
VMEM = pl.BlockSpec(memory_space=pltpu.VMEM)


def _sigmoid(x):
    return 1.0 / (1.0 + jnp.exp(-x))


def _dot(a, b):
    return jnp.dot(a, b, preferred_element_type=F32)


def _dot_t(a, b):
    return lax.dot_general(a, b, (((1,), (1,)), ((), ())), preferred_element_type=F32)


def _tdot(a, b):
    return lax.dot_general(a, b, (((0,), (0,)), ((), ())), preferred_element_type=F32)


def _my_place():
    return lax.axis_index("x"), lax.axis_index("y"), lax.axis_index("c")


def _other_chips(x, y):
    places = [(1 - x, y), (x, 1 - y), (1 - x, 1 - y)]
    return [(p, 2 * p[0] + p[1]) for p in places]


def _gather_inproj(x, g1, win_sh, wout_sh):
    seq = x.shape[0]
    tm = min(seq, 4 * TS)
    n_tiles = seq // tm
    cx, cy = lax.axis_index("x"), lax.axis_index("y")
    order = jnp.stack([2 * cx + cy] + [chip for _, chip in _other_chips(cx, cy)]).astype(jnp.int32)

    def body(order_ref, x_ref, g1_ref, win_ref, wout_ref, proj_bf_ref, win_full, wout_full,
             hbuf, wbuf, send_sems, recv_sems, local_sems):
        j = pl.program_id(0)
        i = pl.program_id(1)
        x_, y_, c = _my_place()
        b = 2 * x_ + y_
        sibling = (x_, y_, 1 - c)
        others = _other_chips(x_, y_)

        def halves(chip, core):
            return (
                win_full.at[chip, pl.ds(core * HALF_IN, HALF_IN)],
                wout_full.at[chip, pl.ds(core * HALF_OUT, HALF_OUT)],
            )

        def copy(k, src, dst, to):
            return pltpu.make_async_remote_copy(
                src_ref=src, dst_ref=dst, send_sem=send_sems.at[k], recv_sem=recv_sems.at[k],
                device_id=to, device_id_type=MESH)

        own = [
            pltpu.make_async_copy(win_ref, win_full.at[b], local_sems.at[0]),
            pltpu.make_async_copy(wout_ref, wout_full.at[b], local_sems.at[1]),
        ]
        mine_src = (win_ref.at[pl.ds(c * HALF_IN, HALF_IN)], wout_ref.at[pl.ds(c * HALF_OUT, HALF_OUT)])

        def direct(n, t):
            return copy(2 * n + t, mine_src[t], halves(b, c)[t], (*others[n][0], c))

        def arrival(n, t):
            landed = halves(others[n][1], c)[t]
            return copy(2 * n + t, landed, landed, (*others[n][0], c))

        def passing(n, t):
            landed = halves(others[n][1], c)[t]
            return copy(6 + 2 * n + t, landed, landed, sibling)

        def from_sibling(n, t):
            landed = halves(others[n][1], 1 - c)[t]
            return copy(6 + 2 * n + t, landed, landed, sibling)

        @pl.when((j == 0) & (i == 0))
        def _():
            first = pltpu.make_async_copy(win_ref, wbuf.at[0], local_sems.at[2])
            first.start()
            for cp in own:
                cp.start()
            for t in range(2):
                for n in range(N_CHIP - 1):
                    direct(n, t).start()
            first.wait()

        def load(n):
            return pltpu.make_async_copy(win_full.at[others[n][1]], wbuf.at[(n + 1) % 2], local_sems.at[2])

        for n in range(N_CHIP - 1):
            @pl.when((j == n) & (i == n_tiles - 1))
            def _(n=n):
                arrival(n, 0).wait_recv()
                passing(n, 0).start()

            @pl.when((j == n + 1) & (i == 0))
            def _(n=n):
                load(n).wait()

        rows = pl.ds(pl.multiple_of(i * tm, tm), tm)

        @pl.when(j == 0)
        def _():
            xt = x_ref[...]
            r = lax.rsqrt(jnp.mean(xt * xt, axis=-1, keepdims=True) + EPS)
            hbuf[rows] = ((xt * r) * g1_ref[...]).astype(BF16)

        proj_bf_ref[...] = _dot(hbuf[rows], wbuf[j % 2]).astype(BF16)

        for n in range(N_CHIP - 1):
            @pl.when((j == n) & (i == n_tiles - 1))
            def _(n=n):
                from_sibling(n, 0).wait_recv()
                load(n).start()

        @pl.when((j == N_CHIP - 1) & (i == n_tiles - 1))
        def _():
            for n in range(N_CHIP - 1):
                arrival(n, 1).wait_recv()
                passing(n, 1).start()
            for n in range(N_CHIP - 1):
                from_sibling(n, 1).wait_recv()
            for n in range(N_CHIP - 1):
                for t in range(2):
                    direct(n, t).wait_send()
                    passing(n, t).wait_send()
            for cp in own:
                cp.wait()

    last = n_tiles - 1
    grid_spec = pltpu.PrefetchScalarGridSpec(
        num_scalar_prefetch=1,
        grid=(N_CHIP, n_tiles),
        in_specs=[
            pl.BlockSpec((tm, D_MODEL), lambda j, i, o: (jnp.where(j == 0, i, last), 0)),
            VMEM, ANY, ANY],
        out_specs=[pl.BlockSpec((tm, SHARD_IN), lambda j, i, o: (i, o[j])), ANY, ANY],
        scratch_shapes=[
            pltpu.VMEM((seq, D_MODEL), BF16), pltpu.VMEM((2, D_MODEL, SHARD_IN), BF16),
            pltpu.SemaphoreType.DMA((12,)), pltpu.SemaphoreType.DMA((12,)), pltpu.SemaphoreType.DMA((3,))],
    )
    return pl.pallas_call(
        body,
        name="gather_inproj",
        grid_spec=grid_spec,
        out_shape=(
            jax.ShapeDtypeStruct((seq, N_CHIP * SHARD_IN), BF16),
            jax.ShapeDtypeStruct((N_CHIP, D_MODEL, SHARD_IN), BF16),
            jax.ShapeDtypeStruct((N_CHIP, SHARD_OUT, D_MODEL), BF16),
        ),
        compiler_params=pltpu.CompilerParams(
            dimension_semantics=("arbitrary", "arbitrary"), vmem_limit_bytes=VMEM_LIMIT),
    )(order, x, g1, win_sh, wout_sh)


def _window_sums(ext, forward):
    n = ext.shape[0]
    sums = []
    acc = ext
    for step in (1, 2, 4, 8):
        acc = acc + pltpu.roll(acc, (n - step) if forward else step, 0)
        sums.append(acc)
    return sums


def _row_counts(tile, rows):
    t = tile * rows + lax.broadcasted_iota(jnp.int32, (rows, GROUP), 0)
    return [jnp.minimum(t + 1, w).astype(F32) for w in POOL_WINDOWS]


def _pool_diffs(pv, prev_rows, tile):
    ext = jnp.concatenate([prev_rows, pv], axis=0)
    sums = _window_sums(ext, forward=False)
    counts = _row_counts(tile, pv.shape[0])
    out = []
    for g in range(len(POOL_WINDOWS)):
        cols = slice(g * GROUP, (g + 1) * GROUP)
        out.append(sums[g][HALO:, cols] / counts[g] - pv[:, cols])
    return out


def _pool_mix(diffs, pw_ref):
    return jnp.concatenate([_dot(diffs[g], pw_ref[g]) for g in range(len(POOL_WINDOWS))], axis=1)


Q_BLOCK, K_BLOCK, V_BLOCK, AG_BLOCK = 8, 12, 16, 20


def _bias_tile(rel_bias):
    flat = jnp.concatenate(
        [jnp.broadcast_to(rel_bias[:, :1], (rel_bias.shape[0], BAND - CHUNK - 1)), rel_bias[:, :2 * MAX_REL]], axis=1)
    rows = [flat[:, CHUNK - 1 - i:CHUNK - 1 - i + BAND] for i in range(CHUNK)]
    bias = jnp.stack(rows, axis=1)
    first = jnp.pad(bias, ((0, 0), (0, 0), (0, CHUNK)), constant_values=MASK_VALUE)
    second = jnp.pad(bias, ((0, 0), (0, 0), (CHUNK, 0)), constant_values=MASK_VALUE)
    both = jnp.concatenate([first, second], axis=1)
    return both.reshape(N_PAIR, PAIR_LANES, WINDOW).transpose(0, 2, 1)


def _rel_index_tile():
    j = np.arange(WINDOW)[:, None]
    q = np.arange(PAIR_LANES)[None, :] % SUPER
    band_key = j - CHUNK * (q // CHUNK)
    idx = np.clip(band_key - LEFT_CHUNKS * CHUNK - q % CHUNK, -MAX_REL, MAX_REL) + MAX_REL
    return np.where((band_key >= 0) & (band_key < BAND), idx, -1).astype(np.int32)


def _by_head(block):
    low = lax.broadcasted_iota(jnp.int32, block.shape, 1) < HEAD_DIM
    zero = jnp.zeros_like(block)
    return jnp.concatenate([jnp.where(low, block, zero), jnp.where(low, zero, block)], axis=0)


def _own_head_rows(cross):
    head_of_row = lax.broadcasted_iota(jnp.int32, cross.shape, 0) >= HEAD_DIM
    head_of_lane = lax.broadcasted_iota(jnp.int32, cross.shape, 1) >= SUPER
    both = jnp.where(jnp.logical_xor(head_of_row, head_of_lane), 0.0, cross).T
    return both[:SUPER] + both[SUPER:]


def _with_mask_lane(q_rows):
    lane = lax.broadcasted_iota(jnp.int32, q_rows.shape, 1)
    return jnp.concatenate([q_rows, jnp.where(lane == 0, MASK_VALUE, 0.0).astype(q_rows.dtype)], axis=1)


def _band_exp(kb, q_rows, bias):
    s = _dot_t(kb, _with_mask_lane(q_rows)) + bias
    e = jnp.exp(s - jnp.max(s, axis=0, keepdims=True))
    return e, jnp.sum(e, axis=0, keepdims=True)


def _window(sc):
    return slice(WIN_BASE + sc * SUPER, WIN_BASE + sc * SUPER + WINDOW)


def _shift_band(i, band_ref, new_ref):
    @pl.when(i == 0)
    def _():
        band_ref[:TA] = jnp.zeros((TA, GROUP), band_ref.dtype)

    @pl.when(i > 0)
    def _():
        band_ref[:TA] = band_ref[TA:]

    band_ref[TA:] = new_ref[...].astype(band_ref.dtype)


def _shift_key_band(i, band_ref, new_ref):
    @pl.when(i == 0)
    def _():
        lane = lax.broadcasted_iota(jnp.int32, (TA, 2 * GROUP), 1)
        band_ref[:TA] = jnp.where(lane == GROUP, 1.0, 0.0).astype(band_ref.dtype)
        band_ref[TA:, GROUP:] = jnp.zeros((TA, GROUP), band_ref.dtype)

    @pl.when(i > 0)
    def _():
        band_ref[:TA] = band_ref[TA:]

    band_ref[TA:, :GROUP] = new_ref[...].astype(band_ref.dtype)


def _shift_band_t(i, band_ref, new_ref):
    @pl.when(i == 0)
    def _():
        band_ref[:, :TA] = jnp.zeros((GROUP, TA), band_ref.dtype)

    @pl.when(i > 0)
    def _():
        band_ref[:, :TA] = band_ref[:, TA:]

    band_ref[:, TA:] = new_ref[...].astype(band_ref.dtype).T


def _scaled_queries(q_ref, rows):
    return _by_head((q_ref[rows] * SCALE).astype(BF16))


def _attn_fwd(proj_bf, bias_t):
    seq = proj_bf.shape[0]
    n_tiles = seq // TA

    def body(q_ref, k_ref, v_ref, ag_ref, bias_ref, a_ref, ya_ref, e_ref, inv_ref, kband, vband_t):
        i = pl.program_id(1)
        _shift_key_band(i, kband, k_ref)
        _shift_band_t(i, vband_t, v_ref)

        def weights(sc):
            rows = slice(sc * SUPER, (sc + 1) * SUPER)
            win = _window(sc)
            e, total = _band_exp(kband[win], _scaled_queries(q_ref, rows), bias_ref[0])
            e_ref[0, sc] = e.astype(BF16)
            inv_total = 1.0 / total
            inv_ref[0, sc] = jnp.broadcast_to(inv_total, (8, PAIR_LANES))
            return inv_total

        nxt = weights(0)
        for sc in range(SUPERS_PER_TILE):
            rows = slice(sc * SUPER, (sc + 1) * SUPER)
            win = _window(sc)
            inv_total = nxt
            if sc + 1 < SUPERS_PER_TILE:
                nxt = weights(sc + 1)
            a = _own_head_rows(_dot(vband_t[:, win], e_ref[0, sc]) * inv_total)
            a_ref[rows] = a.astype(BF16)
            g = ag_ref[rows].astype(F32)
            ya_ref[rows] = (a * (g * _sigmoid(g))).astype(BF16)

    blk = pl.BlockSpec((TA, GROUP), lambda p, i: (i, p))

    def cols(first):
        return pl.BlockSpec((TA, GROUP), lambda p, i: (i, first + p))

    return pl.pallas_call(
        body,
        name="attn_fwd",
        grid=(N_PAIR, n_tiles),
        in_specs=[cols(Q_BLOCK), cols(K_BLOCK), cols(V_BLOCK), cols(AG_BLOCK),
                  pl.BlockSpec((1, WINDOW, PAIR_LANES), lambda p, i: (p, 0, 0))],
        out_specs=[blk, blk,
                   pl.BlockSpec((1, SUPERS_PER_TILE, WINDOW, PAIR_LANES), lambda p, i: (p, i, 0, 0)),
                   pl.BlockSpec((1, SUPERS_PER_TILE, 8, PAIR_LANES), lambda p, i: (p, i, 0, 0))],
        out_shape=[jax.ShapeDtypeStruct((seq, ATTN_WIDTH), BF16), jax.ShapeDtypeStruct((seq, ATTN_WIDTH), BF16),
                   jax.ShapeDtypeStruct((N_PAIR, seq // SUPER, WINDOW, PAIR_LANES), BF16),
                   jax.ShapeDtypeStruct((N_PAIR, seq // SUPER, 8, PAIR_LANES), F32)],
        scratch_shapes=[pltpu.VMEM((2 * TA, 2 * GROUP), BF16), pltpu.VMEM((GROUP, 2 * TA), BF16)],
        compiler_params=pltpu.CompilerParams(
            dimension_semantics=("arbitrary", "arbitrary"), vmem_limit_bytes=VMEM_LIMIT),
    )(proj_bf, proj_bf, proj_bf, proj_bf, bias_t)


BIN_ROWS = 136


def _bias_bin_sums(db_ref, idx_ref):
    lane = lax.broadcasted_iota(jnp.int32, (1, GROUP), 1)
    row = lax.broadcasted_iota(jnp.int32, (BIN_ROWS, GROUP), 0)
    out = jnp.zeros((BIN_ROWS, GROUP), F32)
    for r in range(N_REL - 1):
        lo = 0 if r == 0 else ((BAND - 2 * CHUNK + r) // 8) * 8
        hi = WINDOW if r == 0 else min(WINDOW, lo + SUPER + 8)
        hit = jnp.where(idx_ref[lo:hi] == r, db_ref[lo:hi], 0.0)
        col = jnp.sum(hit, axis=0, keepdims=True)
        s0 = jnp.sum(col[:, :SUPER], axis=1, keepdims=True)
        s1 = jnp.sum(col[:, SUPER:], axis=1, keepdims=True)
        val = jnp.where(lane == 0, s0, jnp.where(lane == 1, s1, 0.0))
        out = jnp.where(row == r, val, out)
    return out


def _attn_bwd(proj_bf, a, dy, e_all, inv_all):
    seq = proj_bf.shape[0]
    n_tiles = seq // TA

    def body(q_ref, k_ref, v_ref, a_ref, ag_ref, dy_ref, e_ref, inv_ref, idx_ref,
             dq_ref, dk_ref, dv_ref, dag_ref, bins_ref, vband, kband_t, dkacc, dvacc, db_ref):
        i = pl.program_id(1)

        @pl.when(i == 0)
        def _():
            dkacc[...] = jnp.zeros_like(dkacc)
            dvacc[...] = jnp.zeros_like(dvacc)
            db_ref[...] = jnp.zeros_like(db_ref)

        @pl.when(i < n_tiles)
        def _():
            _shift_band(i, vband, v_ref)
            _shift_band_t(i, kband_t, k_ref)

            def score_grads(sc):
                rows = slice(sc * SUPER, (sc + 1) * SUPER)
                win = _window(sc)
                q_rows = _scaled_queries(q_ref, rows)
                g = ag_ref[rows].astype(F32)
                sg = _sigmoid(g)
                dyc = dy_ref[rows].astype(F32)
                dag_ref[rows] = (dyc * a_ref[rows].astype(F32) * (sg * (1.0 + g * (1.0 - sg)))).astype(BF16)
                da_rows = _by_head((dyc * (g * sg)).astype(BF16))
                p = e_ref[0, sc].astype(F32) * inv_ref[0, sc, :1]
                dp = _dot_t(vband[win], da_rows)
                ds = p * (dp - jnp.sum(p * dp, axis=0, keepdims=True))
                db_ref[...] += ds
                return q_rows, da_rows, p.astype(BF16), ds.astype(BF16)

            nxt = score_grads(0)
            for sc in range(SUPERS_PER_TILE):
                rows = slice(sc * SUPER, (sc + 1) * SUPER)
                win = _window(sc)
                q_rows, da_rows, p_bf, ds_bf = nxt
                if sc + 1 < SUPERS_PER_TILE:
                    nxt = score_grads(sc + 1)
                dq_ref[rows] = (_own_head_rows(_dot(kband_t[:, win], ds_bf)) * SCALE).astype(BF16)
                dkacc[win] += _dot(ds_bf, q_rows)
                dvacc[win] += _dot(p_bf, da_rows)

        dk_ref[...] = dkacc[:TA].astype(BF16)
        dv_ref[...] = dvacc[:TA].astype(BF16)
        dkacc[:TA] = dkacc[TA:]
        dvacc[:TA] = dvacc[TA:]
        dkacc[TA:] = jnp.zeros((TA, GROUP), F32)
        dvacc[TA:] = jnp.zeros((TA, GROUP), F32)

        @pl.when(i == n_tiles)
        def _():
            bins_ref[0] = _bias_bin_sums(db_ref, idx_ref)

    last = n_tiles - 1
    cur = pl.BlockSpec((TA, GROUP), lambda p, i: (jnp.minimum(i, last), p))
    older = pl.BlockSpec((TA, GROUP), lambda p, i: (jnp.maximum(i - 1, 0), p))
    dy_blk = pl.BlockSpec((TA, GROUP), lambda p, i: (jnp.minimum(i, last), N_PAIR + p))
    per_pair = pl.BlockSpec((1, BIN_ROWS, GROUP), lambda p, i: (p, 0, 0))

    def cols(first):
        return pl.BlockSpec((TA, GROUP), lambda p, i: (jnp.minimum(i, last), first + p))

    def kept(rows):
        return pl.BlockSpec((1, SUPERS_PER_TILE, rows, PAIR_LANES), lambda p, i: (p, jnp.minimum(i, last), 0, 0))

    def out(dtype):
        return jax.ShapeDtypeStruct((seq, ATTN_WIDTH), dtype)

    return pl.pallas_call(
        body,
        name="attn_bwd",
        grid=(N_PAIR, n_tiles + 1),
        in_specs=[cols(Q_BLOCK), cols(K_BLOCK), cols(V_BLOCK), cur, cols(AG_BLOCK), dy_blk, kept(WINDOW), kept(8),
                  VMEM],
        out_specs=[cur, older, older, cur, per_pair],
        out_shape=[out(BF16), out(BF16), out(BF16), out(BF16),
                   jax.ShapeDtypeStruct((N_PAIR, BIN_ROWS, GROUP), F32)],
        scratch_shapes=[
            pltpu.VMEM((2 * TA, GROUP), BF16), pltpu.VMEM((GROUP, 2 * TA), BF16),
            pltpu.VMEM((2 * TA, GROUP), F32), pltpu.VMEM((2 * TA, GROUP), F32),
            pltpu.VMEM((WINDOW, PAIR_LANES), F32)],
        compiler_params=pltpu.CompilerParams(
            dimension_semantics=("arbitrary", "arbitrary"), vmem_limit_bytes=VMEM_LIMIT),
    )(proj_bf, proj_bf, proj_bf, a, proj_bf, dy, e_all, inv_all, jnp.asarray(_rel_index_tile()))


def _out_loss(x, tgt, proj, ya, wout_full, g2, pw_bf, ps):
    seq = x.shape[0]
    to = min(seq, 2 * TS)
    n_tiles = seq // to

    def body(x_ref, t_ref, pv_ref, pg_ref, ya_ref, w_ref, g2_ref, pw_ref, ps_ref,
             dx2_ref, dy_ref, gw_ref, gg_ref, loss_ref, sq_ref, halo_ref, wt_ref):
        i = pl.program_id(0)

        @pl.when(i == 0)
        def _():
            gw_ref[...] = jnp.zeros_like(gw_ref)
            gg_ref[...] = jnp.zeros_like(gg_ref)
            sq_ref[...] = jnp.zeros_like(sq_ref)
            halo_ref[...] = jnp.zeros_like(halo_ref)
            for b in range(N_CHIP):
                wt_ref[b] = w_ref[b].T

        pv = pv_ref[...].astype(F32)
        pg = pg_ref[...].astype(F32)
        diffs = [d.astype(BF16) for d in _pool_diffs(pv, halo_ref[...], i)]
        halo_ref[...] = pv[to - HALO:, :]
        yp = ((_pool_mix(diffs, pw_ref) * ps_ref[...]) * (pg * _sigmoid(pg))).astype(BF16)
        g2v = g2_ref[...]

        def parts(rows):
            return [yp[rows, :SHARD_OUT], yp[rows, SHARD_OUT:], ya_ref[rows, :SHARD_OUT], ya_ref[rows, SHARD_OUT:]]

        def project(rows, ys):
            x2 = x_ref[rows]
            for b in range(N_CHIP):
                x2 = x2 + _dot(ys[b], w_ref[b])
            return x2

        def norm_loss(rows, x2):
            r = lax.rsqrt(jnp.mean(x2 * x2, axis=-1, keepdims=True) + EPS)
            xh = x2 * r
            diff = xh * g2v - t_ref[rows]
            sq_ref[...] += jnp.sum(diff * diff, axis=0, keepdims=True)
            dfin = diff * (1.0 / D_MODEL)
            gg_ref[...] += jnp.sum(dfin * xh, axis=0, keepdims=True)
            dxh = dfin * g2v
            dx2 = r * (dxh - xh * jnp.mean(dxh * xh, axis=-1, keepdims=True))
            dx2_ref[rows] = dx2
            return dx2.astype(BF16)

        def back(rows, ys, dx2_bf):
            for b in range(N_CHIP):
                gw_ref[b] += _tdot(ys[b], dx2_bf)
                dy_ref[rows, b * SHARD_OUT:(b + 1) * SHARD_OUT] = _dot(dx2_bf, wt_ref[b]).astype(BF16)

        n_parts = 2
        part = to // n_parts
        spans = [slice(r * part, (r + 1) * part) for r in range(n_parts)]
        ys = [parts(rows) for rows in spans]
        x2_next = project(spans[0], ys[0])
        for r in range(n_parts):
            x2 = x2_next
            if r + 1 < n_parts:
                x2_next = project(spans[r + 1], ys[r + 1])
            back(spans[r], ys[r], norm_loss(spans[r], x2))

        @pl.when(i == n_tiles - 1)
        def _():
            total = jnp.sum(sq_ref[...], axis=1, keepdims=True) * (0.5 / D_MODEL)
            loss_ref[...] = jnp.broadcast_to(total, loss_ref.shape)

    def rows(width, col=0):
        return pl.BlockSpec((to, width), lambda i: (i, col))

    return pl.pallas_call(
        body,
        name="out_loss",
        grid=(n_tiles,),
        in_specs=[rows(D_MODEL), rows(D_MODEL), rows(POOL_WIDTH, 0), rows(POOL_WIDTH, 1), rows(ATTN_WIDTH),
                  VMEM, VMEM, VMEM, VMEM],
        out_specs=[rows(D_MODEL), rows(D_MODEL), VMEM, VMEM, VMEM],
        out_shape=[
            jax.ShapeDtypeStruct((seq, D_MODEL), F32), jax.ShapeDtypeStruct((seq, D_MODEL), BF16),
            jax.ShapeDtypeStruct((N_CHIP, SHARD_OUT, D_MODEL), F32), jax.ShapeDtypeStruct((1, D_MODEL), F32),
            jax.ShapeDtypeStruct((8, GROUP), F32)],
        scratch_shapes=[pltpu.VMEM((1, D_MODEL), F32), pltpu.VMEM((HALO, POOL_WIDTH), F32),
                        pltpu.VMEM((N_CHIP, D_MODEL, SHARD_OUT), BF16)],
        compiler_params=pltpu.CompilerParams(dimension_semantics=("arbitrary",), vmem_limit_bytes=VMEM_LIMIT),
    )(x, tgt, proj, proj, ya, wout_full, g2, pw_bf, ps)


def _inproj_bwd(x, dx2, dy, proj, dq, dk, dv, dag, win_full, pw_bf, g1, ps):
    seq = x.shape[0]
    n_tiles = seq // TS

    def body(x_ref, dx2_ref, dyp_ref, pv_ref, pvprev_ref, pg_ref, dq_ref, dk_ref, dv_ref, dag_ref,
             w_ref, pw_ref, g1_ref, ps_ref, gx_ref, dproj_ref, ht_ref, gg_ref, gps_ref, gpw_ref, halo_ref):
        i = pl.program_id(0)
        tile = n_tiles - 1 - i

        @pl.when(i == 0)
        def _():
            gg_ref[...] = jnp.zeros_like(gg_ref)
            gps_ref[...] = jnp.zeros_like(gps_ref)
            gpw_ref[...] = jnp.zeros_like(gpw_ref)
            halo_ref[...] = jnp.zeros_like(halo_ref)

        pv_t = pv_ref[...].astype(F32)
        pg_t = pg_ref[...].astype(F32)
        prev_rows = jnp.where(tile > 0, pvprev_ref[...].astype(F32), 0.0)
        diffs = [d.astype(BF16) for d in _pool_diffs(pv_t, prev_rows, tile)]
        mixed = _pool_mix(diffs, pw_ref)
        sg = _sigmoid(pg_t)
        silu = pg_t * sg
        dyp = dyp_ref[...].astype(F32)
        psv = ps_ref[...]
        gps_ref[...] += jnp.sum(dyp * mixed * silu, axis=0, keepdims=True)
        dmixed = (dyp * psv * silu).astype(BF16)
        dpg = dyp * (mixed * psv) * (sg * (1.0 + pg_t * (1.0 - sg)))
        counts = _row_counts(tile, TS)
        dds = []
        for g in range(len(POOL_WINDOWS)):
            dm_g = dmixed[:, g * GROUP:(g + 1) * GROUP]
            gpw_ref[g] += _tdot(diffs[g], dm_g)
            dds.append(_dot_t(dm_g, pw_ref[g]))
        dd = jnp.concatenate(dds, axis=1)
        spread = jnp.concatenate([dds[g] / counts[g] for g in range(len(POOL_WINDOWS))], axis=1)
        sums = _window_sums(jnp.concatenate([spread, halo_ref[...]], axis=0), forward=True)
        halo_ref[...] = spread[:HALO]
        dpv = jnp.concatenate(
            [sums[g][:TS, g * GROUP:(g + 1) * GROUP] for g in range(len(POOL_WINDOWS))], axis=1) - dd

        xt = x_ref[...]
        r = lax.rsqrt(jnp.mean(xt * xt, axis=-1, keepdims=True) + EPS)
        xh = xt * r
        g1v = g1_ref[...]
        ht_ref[...] = (xh * g1v).astype(BF16).T
        dproj = jnp.concatenate(
            [dpv.astype(BF16), dpg.astype(BF16), dq_ref[...], dk_ref[...], dv_ref[...], dag_ref[...]],
            axis=1)
        dproj_ref[...] = dproj
        dh = _dot_t(dproj[:, :SHARD_IN], w_ref[0])
        for chip in range(1, N_CHIP):
            dh = dh + _dot_t(dproj[:, chip * SHARD_IN:(chip + 1) * SHARD_IN], w_ref[chip])

        gg_ref[...] += jnp.sum(dh * xh, axis=0, keepdims=True)
        dxh = dh * g1v
        gx_ref[...] = dx2_ref[...] + r * (dxh - xh * jnp.mean(dxh * xh, axis=-1, keepdims=True))

    def rows(width, col=0):
        return pl.BlockSpec((TS, width), lambda i: (n_tiles - 1 - i, col))

    prev = pl.BlockSpec((HALO, POOL_WIDTH), lambda i: (jnp.maximum((n_tiles - 1 - i) * (TS // HALO) - 1, 0), 0))
    return pl.pallas_call(
        body,
        name="inproj_bwd",
        grid=(n_tiles,),
        in_specs=[rows(D_MODEL), rows(D_MODEL), rows(POOL_WIDTH), rows(POOL_WIDTH), prev, rows(POOL_WIDTH, 1),
                  rows(ATTN_WIDTH), rows(ATTN_WIDTH), rows(ATTN_WIDTH), rows(ATTN_WIDTH), VMEM, VMEM, VMEM, VMEM],
        out_specs=[rows(D_MODEL), rows(N_CHIP * SHARD_IN),
                   pl.BlockSpec((D_MODEL, TS), lambda i: (0, n_tiles - 1 - i)), VMEM, VMEM, VMEM],
        out_shape=[
            jax.ShapeDtypeStruct((seq, D_MODEL), F32),
            jax.ShapeDtypeStruct((seq, N_CHIP * SHARD_IN), BF16),
            jax.ShapeDtypeStruct((D_MODEL, seq), BF16),
            jax.ShapeDtypeStruct((1, D_MODEL), F32),
            jax.ShapeDtypeStruct((1, POOL_WIDTH), F32),
            jax.ShapeDtypeStruct((len(POOL_WINDOWS), GROUP, GROUP), F32)],
        scratch_shapes=[pltpu.VMEM((HALO, POOL_WIDTH), F32)],
        compiler_params=pltpu.CompilerParams(dimension_semantics=("arbitrary",), vmem_limit_bytes=VMEM_LIMIT),
    )(x, dx2, dy, proj, proj, proj, dq, dk, dv, dag, win_full, pw_bf, g1, ps)


PASS_PEER = (2, 0, 1)


def _gw_reduce(ht, dproj, gwout, small):
    seq = ht.shape[1]
    tm = min(4 * TS, seq // 4)
    n_tiles = seq // tm
    half_small = SMALL_ROWS // 2
    cx, cy = lax.axis_index("x"), lax.axis_index("y")
    outer = _other_chips(cx, cy)
    order = jnp.stack([outer[n][1] for n in PASS_PEER] + [2 * cx + cy]).astype(jnp.int32)

    def body(order_ref, ht_ref, dp_ref, gwout_ref, small_ref, gin_final, gout_final, small_final,
             acc, pair_in, pair_out, pair_small, tx_in, tx_out, rx_in, rx_out, rx_small,
             gin_out, gout_out, small_out, send_sems, recv_sems, out_sems):
        j = pl.program_id(0)
        i = pl.program_id(1)
        x, y, c = _my_place()
        b = 2 * x + y
        sibling = (x, y, 1 - c)
        others = _other_chips(x, y)
        mine_in = pl.ds(pl.multiple_of(c * HALF_IN, HALF_IN), HALF_IN)
        mine_out = pl.ds(pl.multiple_of(c * HALF_OUT, HALF_OUT), HALF_OUT)
        mine_small = pl.ds(pl.multiple_of(c * half_small, 8), half_small)
        theirs_in = pl.ds(pl.multiple_of((1 - c) * HALF_IN, HALF_IN), HALF_IN)
        theirs_out = pl.ds(pl.multiple_of((1 - c) * HALF_OUT, HALF_OUT), HALF_OUT)
        theirs_small = pl.ds(pl.multiple_of((1 - c) * half_small, 8), half_small)

        def copy(k, src, dst, to):
            return pltpu.make_async_remote_copy(
                src_ref=src, dst_ref=dst, send_sem=send_sems.at[k], recv_sem=recv_sems.at[k],
                device_id=to, device_id_type=MESH)

        swap_out = copy(0, gwout_ref.at[:, theirs_out], pair_out, sibling)
        swap_small = copy(1, small_ref, pair_small, sibling)

        def swap_in(p):
            return copy(2 + p, acc.at[p % 2, theirs_in], pair_in.at[p % 2], sibling)

        def to_chip(n, t):
            to = (*others[n][0], c)
            if t == 0:
                return copy(6 + 3 * n, tx_in.at[n], rx_in.at[n], to)
            if t == 1:
                return copy(7 + 3 * n, tx_out.at[n], rx_out.at[n], to)
            return copy(8 + 3 * n, pair_small.at[mine_small], rx_small.at[b], to)

        share_in = copy(15, gin_out.at[mine_in], gin_out.at[mine_in], sibling)
        share_out = copy(16, gout_out.at[mine_out], gout_out.at[mine_out], sibling)
        share_small = copy(17, small_out.at[mine_small], small_out.at[mine_small], sibling)

        def at(jj, ii):
            return (j == jj) & (i == ii)

        par = j % 2

        @pl.when(i == 0)
        def _():
            acc[par] = jnp.zeros((D_MODEL, SHARD_IN), F32)

        acc[par] += _dot(ht_ref[...], dp_ref[...])

        @pl.when(at(0, 0))
        def _():
            swap_out.start()
            swap_small.start()

        @pl.when(at(0, 2))
        def _():
            swap_out.wait_recv()
            swap_small.wait_recv()
            for chip in range(N_CHIP):
                pair_out[chip] = gwout_ref[chip, mine_out] + pair_out[chip]
            pair_small[...] = small_ref[...] + pair_small[...]
            rx_small[b] = pair_small[mine_small]
            for n in range(N_CHIP - 1):
                tx_out[n] = pair_out[others[n][1]].astype(BF16)
                to_chip(n, 1).start()
                to_chip(n, 2).start()

        @pl.when(at(1, 3))
        def _():
            total_out = pair_out[b]
            for n in range(N_CHIP - 1):
                to_chip(n, 1).wait_recv()
                copy(8 + 3 * n, pair_small.at[mine_small], rx_small.at[others[n][1]], (*others[n][0], c)).wait_recv()
                total_out = total_out + rx_out[n].astype(F32)
            gout_out[mine_out] = total_out
            small_out[mine_small] = ((rx_small[0] + rx_small[1]) + rx_small[2]) + rx_small[3]
            share_out.start()
            share_small.start()

        for p in range(N_CHIP - 1):
            n = PASS_PEER[p]

            @pl.when(at(p + 1, 0))
            def _(p=p):
                swap_in(p).start()

            @pl.when(at(p + 1, 2))
            def _(p=p, n=n):
                swap_in(p).wait_recv()
                swap_in(p).wait_send()
                tx_in[n] = (acc[p % 2, mine_in] + pair_in[p % 2]).astype(BF16)
                to_chip(n, 0).start()

        @pl.when(at(N_CHIP - 1, n_tiles - 1))
        def _():
            last = N_CHIP - 1
            swap_in(last).start()
            swap_in(last).wait_recv()
            total_in = acc[last % 2, mine_in] + pair_in[last % 2]
            for n in range(N_CHIP - 1):
                to_chip(n, 0).wait_recv()
                total_in = total_in + rx_in[n].astype(F32)
            gin_out[mine_in] = total_in
            share_in.start()
            copy(15, gin_out.at[theirs_in], gin_out.at[theirs_in], sibling).wait_recv()
            copy(16, gout_out.at[theirs_out], gout_out.at[theirs_out], sibling).wait_recv()
            copy(17, small_out.at[theirs_small], small_out.at[theirs_small], sibling).wait_recv()
            swap_out.wait_send()
            swap_small.wait_send()
            swap_in(last).wait_send()
            for n in range(N_CHIP - 1):
                for t in range(3):
                    to_chip(n, t).wait_send()
            share_in.wait_send()
            share_out.wait_send()
            share_small.wait_send()
            outs = [pltpu.make_async_copy(src, dst, out_sems.at[k]) for k, (src, dst) in enumerate(
                [(gin_out, gin_final), (gout_out, gout_final), (small_out, small_final)])]
            for cp in outs:
                cp.start()
            for cp in outs:
                cp.wait()

    assert n_tiles >= 4, "the reduction's steps are spread over the first four token steps of a pass"
    grid_spec = pltpu.PrefetchScalarGridSpec(
        num_scalar_prefetch=1,
        grid=(N_CHIP, n_tiles),
        in_specs=[
            pl.BlockSpec((D_MODEL, tm), lambda j, i, o: (0, i)),
            pl.BlockSpec((tm, SHARD_IN), lambda j, i, o: (i, o[j])),
            VMEM, VMEM],
        out_specs=[ANY, ANY, ANY],
        scratch_shapes=[
            pltpu.VMEM((2, D_MODEL, SHARD_IN), F32),
            pltpu.VMEM((2, HALF_IN, SHARD_IN), F32),
            pltpu.VMEM((N_CHIP, HALF_OUT, D_MODEL), F32),
            pltpu.VMEM((SMALL_ROWS, GROUP), F32),
            pltpu.VMEM((N_CHIP - 1, HALF_IN, SHARD_IN), BF16),
            pltpu.VMEM((N_CHIP - 1, HALF_OUT, D_MODEL), BF16),
            pltpu.VMEM((N_CHIP - 1, HALF_IN, SHARD_IN), BF16),
            pltpu.VMEM((N_CHIP - 1, HALF_OUT, D_MODEL), BF16),
            pltpu.VMEM((N_CHIP, half_small, GROUP), F32),
            pltpu.VMEM((D_MODEL, SHARD_IN), F32),
            pltpu.VMEM((SHARD_OUT, D_MODEL), F32),
            pltpu.VMEM((SMALL_ROWS, GROUP), F32),
            pltpu.SemaphoreType.DMA((18,)),
            pltpu.SemaphoreType.DMA((18,)),
            pltpu.SemaphoreType.DMA((3,)),
        ],
    )
    return pl.pallas_call(
        body,
        name="gw_reduce",
        grid_spec=grid_spec,
        out_shape=(
            jax.ShapeDtypeStruct((D_MODEL, SHARD_IN), F32),
            jax.ShapeDtypeStruct((SHARD_OUT, D_MODEL), F32),
            jax.ShapeDtypeStruct((SMALL_ROWS, GROUP), F32),
        ),
        compiler_params=pltpu.CompilerParams(
            dimension_semantics=("arbitrary", "arbitrary"), vmem_limit_bytes=VMEM_LIMIT),
    )(order, ht, dproj, gwout, small)


def _adamw(name, w, g, m, v, block_rows):
    rows, cols = w.shape

    def body(w_ref, g_ref, m_ref, v_ref, d_ref, m_out, v_out):
        grad = g_ref[...]
        m_new = ADAM_B1 * m_ref[...] + (1.0 - ADAM_B1) * grad
        v_new = ADAM_B2 * v_ref[...] + (1.0 - ADAM_B2) * (grad * grad)
        m_hat = m_new / (1.0 - ADAM_B1 ** ADAM_STEP)
        v_hat = v_new / (1.0 - ADAM_B2 ** ADAM_STEP)
        d_ref[...] = -ADAM_LR * (m_hat / (jnp.sqrt(v_hat) + ADAM_EPS) + ADAM_WD * w_ref[...])
        m_out[...] = m_new
        v_out[...] = v_new

    blk = pl.BlockSpec((block_rows, cols), lambda i: (i, 0))
    shape = jax.ShapeDtypeStruct((rows, cols), F32)
    return pl.pallas_call(
        body,
        name=name,
        grid=(rows // block_rows,),
        in_specs=[blk] * 4,
        out_specs=[blk] * 3,
        out_shape=[shape] * 3,
        compiler_params=pltpu.CompilerParams(dimension_semantics=("arbitrary",)),
    )(w, g, m, v)


def _pack_small(norm_gain, pool_w, pool_scale, rel_bias, final_gain, loss_rows):
    parts = [
        norm_gain.reshape(8, GROUP),
        pool_w.reshape(len(POOL_WINDOWS) * GROUP, GROUP),
        jnp.pad(pool_scale.reshape(4, GROUP), ((0, 4), (0, 0))),
        jnp.pad(rel_bias.reshape(8, N_REL), ((0, 0), (0, 2 * GROUP - N_REL))).reshape(16, GROUP),
        final_gain.reshape(8, GROUP),
        loss_rows,
    ]
    return jnp.concatenate(parts, axis=0)


def _unpack_small(block):
    norm_gain = block[0:8].reshape(1, D_MODEL)
    pool_w = block[8:520].reshape(1, len(POOL_WINDOWS), GROUP, GROUP)
    pool_scale = block[520:524].reshape(1, POOL_WIDTH)
    rel_bias = block[528:544].reshape(8, 2 * GROUP)[:, :N_REL].reshape(1, 8, N_REL)
    final_gain = block[544:552].reshape(D_MODEL)
    return norm_gain, pool_w, pool_scale, rel_bias, final_gain


def kernel(x, norm_gain, w_in, pool_w, pool_scale, rel_bias, w_out, final_norm_gain, loss_target, m_norm_gain, m_w_in, m_pool_w, m_pool_scale, m_rel_bias, m_w_out, m_final_norm_gain, v_norm_gain, v_w_in, v_pool_w, v_pool_scale, v_rel_bias, v_w_out, v_final_norm_gain):
    assert x.shape[1] % TS == 0 and x.shape[2] == D_MODEL
    xs = x[0]
    tgt = loss_target[0]
    g1 = norm_gain.reshape(1, D_MODEL)
    g2 = final_norm_gain.reshape(1, D_MODEL)
    ps = pool_scale.reshape(1, POOL_WIDTH)
    pw_bf = pool_w[0].astype(BF16)

    proj, win_full, wout_full = _gather_inproj(xs, g1, w_in[0].astype(BF16), w_out[0].astype(BF16))
    bias_t = _bias_tile(rel_bias[0])

    a, ya, e_all, inv_all = _attn_fwd(proj, bias_t)
    dx2, dy, gwout, gg2, loss_rows = _out_loss(xs, tgt, proj, ya, wout_full, g2, pw_bf, ps)
    dq, dk, dv, dag, bins = _attn_bwd(proj, a, dy, e_all, inv_all)
    gx, dproj, ht, gg1, gps, gpw = _inproj_bwd(xs, dx2, dy, proj, dq, dk, dv, dag, win_full, pw_bf, g1, ps)

    g_bias = bins[:, :N_REL, :2].transpose(0, 2, 1).reshape(8, N_REL)
    small = _pack_small(gg1, gpw, gps, g_bias, gg2, loss_rows)
    g_win, g_wout, g_small = _gw_reduce(ht, dproj, gwout, small)
    loss = g_small[LOSS_ROW, 0]

    zeros8 = jnp.zeros((8, GROUP), F32)
    w_small = _pack_small(norm_gain, pool_w, pool_scale, rel_bias, final_norm_gain, zeros8)
    m_small = _pack_small(m_norm_gain, m_pool_w, m_pool_scale, m_rel_bias, m_final_norm_gain, zeros8)
    v_small = _pack_small(v_norm_gain, v_pool_w, v_pool_scale, v_rel_bias, v_final_norm_gain, zeros8)

    d_win, m_win, v_win = _adamw("adamw_w_in", w_in[0], g_win, m_w_in[0], v_w_in[0], 256)
    d_wout, m_wout, v_wout = _adamw("adamw_w_out", w_out[0], g_wout, m_w_out[0], v_w_out[0], 128)
    d_small, m_new_small, v_new_small = _adamw("adamw_small", w_small, g_small, m_small, v_small, SMALL_ROWS // 2)

    def full(win_part, wout_part, block):
        ng, pw, psc, rb, fg = _unpack_small(block)
        return [ng, win_part[None], pw, psc, rb, wout_part[None], fg]

    grads = full(g_win, g_wout, g_small)
    deltas = full(d_win, d_wout, d_small)
    new_m = full(m_win, m_wout, m_new_small)
    new_v = full(v_win, v_wout, v_new_small)
    return (loss, gx[None], *grads, *deltas, *new_m, *new_v)
```

```python
import numpy as np
import jax
import jax.numpy as jnp
from jax import lax
from jax.experimental import pallas as pl
from jax.experimental.pallas import tpu as pltpu

F32 = jnp.float32
BF16 = jnp.bfloat16

D_MODEL = 1024
POOL_WIDTH = 512
ATTN_WIDTH = 512
POOL_WINDOWS = (2, 4, 8, 16)
GROUP = 128
CHUNK = 64
LEFT_CHUNKS = 8
BAND = (LEFT_CHUNKS + 1) * CHUNK
HEAD_DIM = 64
N_PAIR = 4
MAX_REL = 64
N_REL = 2 * MAX_REL + 1
EPS = 1e-6
MASK_VALUE = -1e30
SCALE = 0.125

ADAM_LR = 0.001
ADAM_B1 = 0.9
ADAM_B2 = 0.999
ADAM_EPS = 1e-08
ADAM_WD = 0.01
ADAM_STEP = 10

TS = LEFT_CHUNKS * CHUNK
SUPER = 2 * CHUNK
WINDOW = BAND + CHUNK
TA = 4 * TS
WIN_BASE = TA - LEFT_CHUNKS * CHUNK
SUPERS_PER_TILE = TA // SUPER
PAIR_LANES = 2 * SUPER
HALO = 16
N_CHIP = 4
SHARD_IN = 768
SHARD_OUT = 256
PIECE = 256
PIECES_PER_SHARD = SHARD_IN // PIECE
HALF_IN = D_MODEL // 2
HALF_OUT = SHARD_OUT // 2
SMALL_ROWS = 560
LOSS_ROW = 552
VMEM_LIMIT = 60 * 1024 * 1024

MESH = pl.DeviceIdType.MESH
ANY = pl.BlockSpec(memory_space=pl.ANY)
VMEM = pl.BlockSpec(memory_space=pltpu.VMEM)


def _sigmoid(x):
    return 1.0 / (1.0 + jnp.exp(-x))


def _dot(a, b):
    return jnp.dot(a, b, preferred_element_type=F32)


def _dot_t(a, b):
    return lax.dot_general(a, b, (((1,), (1,)), ((), ())), preferred_element_type=F32)


def _tdot(a, b):
    return lax.dot_general(a, b, (((0,), (0,)), ((), ())), preferred_element_type=F32)


def _my_place():
    return lax.axis_index("x"), lax.axis_index("y"), lax.axis_index("c")


def _other_chips(x, y):
    places = [(1 - x, y), (x, 1 - y), (1 - x, 1 - y)]
    return [(p, 2 * p[0] + p[1]) for p in places]


def _gather_inproj(x, g1, win_sh, wout_sh):
    seq = x.shape[0]
    tm = min(seq, 4 * TS)
    n_tiles = seq // tm
    cx, cy = lax.axis_index("x"), lax.axis_index("y")
    order = jnp.stack([2 * cx + cy] + [chip for _, chip in _other_chips(cx, cy)]).astype(jnp.int32)

    def body(order_ref, x_ref, g1_ref, win_ref, wout_ref, proj_bf_ref, win_full, wout_full,
             hbuf, wbuf, send_sems, recv_sems, local_sems):
        j = pl.program_id(0)
        i = pl.program_id(1)
        x_, y_, c = _my_place()
        b = 2 * x_ + y_
        sibling = (x_, y_, 1 - c)
        others = _other_chips(x_, y_)

        def halves(chip, core):
            return (
                win_full.at[chip, pl.ds(core * HALF_IN, HALF_IN)],
                wout_full.at[chip, pl.ds(core * HALF_OUT, HALF_OUT)],
            )

        def copy(k, src, dst, to):
            return pltpu.make_async_remote_copy(
                src_ref=src, dst_ref=dst, send_sem=send_sems.at[k], recv_sem=recv_sems.at[k],
                device_id=to, device_id_type=MESH)

        own = [
            pltpu.make_async_copy(win_ref, win_full.at[b], local_sems.at[0]),
            pltpu.make_async_copy(wout_ref, wout_full.at[b], local_sems.at[1]),
        ]
        mine_src = (win_ref.at[pl.ds(c * HALF_IN, HALF_IN)], wout_ref.at[pl.ds(c * HALF_OUT, HALF_OUT)])

        def direct(n, t):
            return copy(2 * n + t, mine_src[t], halves(b, c)[t], (*others[n][0], c))

        def arrival(n, t):
            landed = halves(others[n][1], c)[t]
            return copy(2 * n + t, landed, landed, (*others[n][0], c))

        def passing(n, t):
            landed = halves(others[n][1], c)[t]
            return copy(6 + 2 * n + t, landed, landed, sibling)

        def from_sibling(n, t):
            landed = halves(others[n][1], 1 - c)[t]
            return copy(6 + 2 * n + t, landed, landed, sibling)

        @pl.when((j == 0) & (i == 0))
        def _():
            first = pltpu.make_async_copy(win_ref, wbuf.at[0], local_sems.at[2])
            first.start()
            for cp in own:
                cp.start()
            for t in range(2):
                for n in range(N_CHIP - 1):
                    direct(n, t).start()
            first.wait()

        def load(n):
            return pltpu.make_async_copy(win_full.at[others[n][1]], wbuf.at[(n + 1) % 2], local_sems.at[2])

        for n in range(N_CHIP - 1):
            @pl.when((j == n) & (i == n_tiles - 1))
            def _(n=n):
                arrival(n, 0).wait_recv()
                passing(n, 0).start()

            @pl.when((j == n + 1) & (i == 0))
            def _(n=n):
                load(n).wait()

        rows = pl.ds(pl.multiple_of(i * tm, tm), tm)

        @pl.when(j == 0)
        def _():
            xt = x_ref[...]
            r = lax.rsqrt(jnp.mean(xt * xt, axis=-1, keepdims=True) + EPS)
            hbuf[rows] = ((xt * r) * g1_ref[...]).astype(BF16)

        proj_bf_ref[...] = _dot(hbuf[rows], wbuf[j % 2]).astype(BF16)

        for n in range(N_CHIP - 1):
            @pl.when((j == n) & (i == n_tiles - 1))
            def _(n=n):
                from_sibling(n, 0).wait_recv()
                load(n).start()

        @pl.when((j == N_CHIP - 1) & (i == n_tiles - 1))
        def _():
            for n in range(N_CHIP - 1):
                arrival(n, 1).wait_recv()
                passing(n, 1).start()
            for n in range(N_CHIP - 1):
                from_sibling(n, 1).wait_recv()
            for n in range(N_CHIP - 1):
                for t in range(2):
                    direct(n, t).wait_send()
                    passing(n, t).wait_send()
            for cp in own:
                cp.wait()

    last = n_tiles - 1
    grid_spec = pltpu.PrefetchScalarGridSpec(
        num_scalar_prefetch=1,
        grid=(N_CHIP, n_tiles),
        in_specs=[
            pl.BlockSpec((tm, D_MODEL), lambda j, i, o: (jnp.where(j == 0, i, last), 0)),
            VMEM, ANY, ANY],
        out_specs=[pl.BlockSpec((tm, SHARD_IN), lambda j, i, o: (i, o[j])), ANY, ANY],
        scratch_shapes=[
            pltpu.VMEM((seq, D_MODEL), BF16), pltpu.VMEM((2, D_MODEL, SHARD_IN), BF16),
            pltpu.SemaphoreType.DMA((12,)), pltpu.SemaphoreType.DMA((12,)), pltpu.SemaphoreType.DMA((3,))],
    )
    return pl.pallas_call(
        body,
        name="gather_inproj",
        grid_spec=grid_spec,
        out_shape=(
            jax.ShapeDtypeStruct((seq, N_CHIP * SHARD_IN), BF16),
            jax.ShapeDtypeStruct((N_CHIP, D_MODEL, SHARD_IN), BF16),
            jax.ShapeDtypeStruct((N_CHIP, SHARD_OUT, D_MODEL), BF16),
        ),
        compiler_params=pltpu.CompilerParams(
            dimension_semantics=("arbitrary", "arbitrary"), vmem_limit_bytes=VMEM_LIMIT),
    )(order, x, g1, win_sh, wout_sh)


def _window_sums(ext, forward):
    n = ext.shape[0]
    sums = []
    acc = ext
    for step in (1, 2, 4, 8):
        acc = acc + pltpu.roll(acc, (n - step) if forward else step, 0)
        sums.append(acc)
    return sums


def _row_counts(tile, rows):
    t = tile * rows + lax.broadcasted_iota(jnp.int32, (rows, GROUP), 0)
    return [jnp.minimum(t + 1, w).astype(F32) for w in POOL_WINDOWS]


def _pool_diffs(pv, prev_rows, tile):
    ext = jnp.concatenate([prev_rows, pv], axis=0)
    sums = _window_sums(ext, forward=False)
    counts = _row_counts(tile, pv.shape[0])
    out = []
    for g in range(len(POOL_WINDOWS)):
        cols = slice(g * GROUP, (g + 1) * GROUP)
        out.append(sums[g][HALO:, cols] / counts[g] - pv[:, cols])
    return out


def _pool_mix(diffs, pw_ref):
    return jnp.concatenate([_dot(diffs[g], pw_ref[g]) for g in range(len(POOL_WINDOWS))], axis=1)


Q_BLOCK, K_BLOCK, V_BLOCK, AG_BLOCK = 8, 12, 16, 20


def _bias_tile(rel_bias):
    flat = jnp.concatenate(
        [jnp.broadcast_to(rel_bias[:, :1], (rel_bias.shape[0], BAND - CHUNK - 1)), rel_bias[:, :2 * MAX_REL]], axis=1)
    rows = [flat[:, CHUNK - 1 - i:CHUNK - 1 - i + BAND] for i in range(CHUNK)]
    bias = jnp.stack(rows, axis=1)
    first = jnp.pad(bias, ((0, 0), (0, 0), (0, CHUNK)), constant_values=MASK_VALUE)
    second = jnp.pad(bias, ((0, 0), (0, 0), (CHUNK, 0)), constant_values=MASK_VALUE)
    both = jnp.concatenate([first, second], axis=1)
    return both.reshape(N_PAIR, PAIR_LANES, WINDOW).transpose(0, 2, 1)


def _rel_index_tile():
    j = np.arange(WINDOW)[:, None]
    q = np.arange(PAIR_LANES)[None, :] % SUPER
    band_key = j - CHUNK * (q // CHUNK)
    idx = np.clip(band_key - LEFT_CHUNKS * CHUNK - q % CHUNK, -MAX_REL, MAX_REL) + MAX_REL
    return np.where((band_key >= 0) & (band_key < BAND), idx, -1).astype(np.int32)


def _by_head(block):
    low = lax.broadcasted_iota(jnp.int32, block.shape, 1) < HEAD_DIM
    zero = jnp.zeros_like(block)
    return jnp.concatenate([jnp.where(low, block, zero), jnp.where(low, zero, block)], axis=0)


def _own_head_rows(cross):
    head_of_row = lax.broadcasted_iota(jnp.int32, cross.shape, 0) >= HEAD_DIM
    head_of_lane = lax.broadcasted_iota(jnp.int32, cross.shape, 1) >= SUPER
    both = jnp.where(jnp.logical_xor(head_of_row, head_of_lane), 0.0, cross).T
    return both[:SUPER] + both[SUPER:]


def _with_mask_lane(q_rows):
    lane = lax.broadcasted_iota(jnp.int32, q_rows.shape, 1)
    return jnp.concatenate([q_rows, jnp.where(lane == 0, MASK_VALUE, 0.0).astype(q_rows.dtype)], axis=1)


def _band_exp(kb, q_rows, bias):
    s = _dot_t(kb, _with_mask_lane(q_rows)) + bias
    e = jnp.exp(s - jnp.max(s, axis=0, keepdims=True))
    return e, jnp.sum(e, axis=0, keepdims=True)


def _window(sc):
    return slice(WIN_BASE + sc * SUPER, WIN_BASE + sc * SUPER + WINDOW)


def _shift_band(i, band_ref, new_ref):
    @pl.when(i == 0)
    def _():
        band_ref[:TA] = jnp.zeros((TA, GROUP), band_ref.dtype)

    @pl.when(i > 0)
    def _():
        band_ref[:TA] = band_ref[TA:]

    band_ref[TA:] = new_ref[...].astype(band_ref.dtype)


def _shift_key_band(i, band_ref, new_ref):
    @pl.when(i == 0)
    def _():
        lane = lax.broadcasted_iota(jnp.int32, (TA, 2 * GROUP), 1)
        band_ref[:TA] = jnp.where(lane == GROUP, 1.0, 0.0).astype(band_ref.dtype)
        band_ref[TA:, GROUP:] = jnp.zeros((TA, GROUP), band_ref.dtype)

    @pl.when(i > 0)
    def _():
        band_ref[:TA] = band_ref[TA:]

    band_ref[TA:, :GROUP] = new_ref[...].astype(band_ref.dtype)


def _shift_band_t(i, band_ref, new_ref):
    @pl.when(i == 0)
    def _():
        band_ref[:, :TA] = jnp.zeros((GROUP, TA), band_ref.dtype)

    @pl.when(i > 0)
    def _():
        band_ref[:, :TA] = band_ref[:, TA:]

    band_ref[:, TA:] = new_ref[...].astype(band_ref.dtype).T


def _scaled_queries(q_ref, rows):
    return _by_head((q_ref[rows] * SCALE).astype(BF16))


def _attn_fwd(proj_bf, bias_t):
    seq = proj_bf.shape[0]
    n_tiles = seq // TA

    def body(q_ref, k_ref, v_ref, ag_ref, bias_ref, a_ref, ya_ref, e_ref, inv_ref, kband, vband_t):
        i = pl.program_id(1)
        _shift_key_band(i, kband, k_ref)
        _shift_band_t(i, vband_t, v_ref)

        def weights(sc):
            rows = slice(sc * SUPER, (sc + 1) * SUPER)
            win = _window(sc)
            e, total = _band_exp(kband[win], _scaled_queries(q_ref, rows), bias_ref[0])
            e_ref[0, sc] = e.astype(BF16)
            inv_total = 1.0 / total
            inv_ref[0, sc] = jnp.broadcast_to(inv_total, (8, PAIR_LANES))
            return inv_total

        nxt = weights(0)
        for sc in range(SUPERS_PER_TILE):
            rows = slice(sc * SUPER, (sc + 1) * SUPER)
            win = _window(sc)
            inv_total = nxt
            if sc + 1 < SUPERS_PER_TILE:
                nxt = weights(sc + 1)
            a = _own_head_rows(_dot(vband_t[:, win], e_ref[0, sc]) * inv_total)
            a_ref[rows] = a.astype(BF16)
            g = ag_ref[rows].astype(F32)
            ya_ref[rows] = (a * (g * _sigmoid(g))).astype(BF16)

    blk = pl.BlockSpec((TA, GROUP), lambda p, i: (i, p))

    def cols(first):
        return pl.BlockSpec((TA, GROUP), lambda p, i: (i, first + p))

    return pl.pallas_call(
        body,
        name="attn_fwd",
        grid=(N_PAIR, n_tiles),
        in_specs=[cols(Q_BLOCK), cols(K_BLOCK), cols(V_BLOCK), cols(AG_BLOCK),
                  pl.BlockSpec((1, WINDOW, PAIR_LANES), lambda p, i: (p, 0, 0))],
        out_specs=[blk, blk,
                   pl.BlockSpec((1, SUPERS_PER_TILE, WINDOW, PAIR_LANES), lambda p, i: (p, i, 0, 0)),
                   pl.BlockSpec((1, SUPERS_PER_TILE, 8, PAIR_LANES), lambda p, i: (p, i, 0, 0))],
        out_shape=[jax.ShapeDtypeStruct((seq, ATTN_WIDTH), BF16), jax.ShapeDtypeStruct((seq, ATTN_WIDTH), BF16),
                   jax.ShapeDtypeStruct((N_PAIR, seq // SUPER, WINDOW, PAIR_LANES), BF16),
                   jax.ShapeDtypeStruct((N_PAIR, seq // SUPER, 8, PAIR_LANES), F32)],
        scratch_shapes=[pltpu.VMEM((2 * TA, 2 * GROUP), BF16), pltpu.VMEM((GROUP, 2 * TA), BF16)],
        compiler_params=pltpu.CompilerParams(
            dimension_semantics=("arbitrary", "arbitrary"), vmem_limit_bytes=VMEM_LIMIT),
    )(proj_bf, proj_bf, proj_bf, proj_bf, bias_t)


BIN_ROWS = 136


def _bias_bin_sums(db_ref, idx_ref):
    lane = lax.broadcasted_iota(jnp.int32, (1, GROUP), 1)
    row = lax.broadcasted_iota(jnp.int32, (BIN_ROWS, GROUP), 0)
    out = jnp.zeros((BIN_ROWS, GROUP), F32)
    for r in range(N_REL - 1):
        lo = 0 if r == 0 else ((BAND - 2 * CHUNK + r) // 8) * 8
        hi = WINDOW if r == 0 else min(WINDOW, lo + SUPER + 8)
        hit = jnp.where(idx_ref[lo:hi] == r, db_ref[lo:hi], 0.0)
        col = jnp.sum(hit, axis=0, keepdims=True)
        s0 = jnp.sum(col[:, :SUPER], axis=1, keepdims=True)
        s1 = jnp.sum(col[:, SUPER:], axis=1, keepdims=True)
        val = jnp.where(lane == 0, s0, jnp.where(lane == 1, s1, 0.0))
        out = jnp.where(row == r, val, out)
    return out


def _attn_bwd(proj_bf, a, dy, e_all, inv_all):
    seq = proj_bf.shape[0]
    n_tiles = seq // TA

    def body(q_ref, k_ref, v_ref, a_ref, ag_ref, dy_ref, e_ref, inv_ref, idx_ref,
             dq_ref, dk_ref, dv_ref, dag_ref, bins_ref, vband, kband_t, dkacc, dvacc, db_ref):
        i = pl.program_id(1)

        @pl.when(i == 0)
        def _():
            dkacc[...] = jnp.zeros_like(dkacc)
            dvacc[...] = jnp.zeros_like(dvacc)
            db_ref[...] = jnp.zeros_like(db_ref)

        @pl.when(i < n_tiles)
        def _():
            _shift_band(i, vband, v_ref)
            _shift_band_t(i, kband_t, k_ref)

            def score_grads(sc):
                rows = slice(sc * SUPER, (sc + 1) * SUPER)
                win = _window(sc)
                q_rows = _scaled_queries(q_ref, rows)
                g = ag_ref[rows].astype(F32)
                sg = _sigmoid(g)
                dyc = dy_ref[rows].astype(F32)
                dag_ref[rows] = (dyc * a_ref[rows].astype(F32) * (sg * (1.0 + g * (1.0 - sg)))).astype(BF16)
                da_rows = _by_head((dyc * (g * sg)).astype(BF16))
                p = e_ref[0, sc].astype(F32) * inv_ref[0, sc, :1]
                dp = _dot_t(vband[win], da_rows)
                ds = p * (dp - jnp.sum(p * dp, axis=0, keepdims=True))
                db_ref[...] += ds
                return q_rows, da_rows, p.astype(BF16), ds.astype(BF16)

            nxt = score_grads(0)
            for sc in range(SUPERS_PER_TILE):
                rows = slice(sc * SUPER, (sc + 1) * SUPER)
                win = _window(sc)
                q_rows, da_rows, p_bf, ds_bf = nxt
                if sc + 1 < SUPERS_PER_TILE:
                    nxt = score_grads(sc + 1)
                dq_ref[rows] = (_own_head_rows(_dot(kband_t[:, win], ds_bf)) * SCALE).astype(BF16)
                dkacc[win] += _dot(ds_bf, q_rows)
                dvacc[win] += _dot(p_bf, da_rows)

        dk_ref[...] = dkacc[:TA].astype(BF16)
        dv_ref[...] = dvacc[:TA].astype(BF16)
        dkacc[:TA] = dkacc[TA:]
        dvacc[:TA] = dvacc[TA:]
        dkacc[TA:] = jnp.zeros((TA, GROUP), F32)
        dvacc[TA:] = jnp.zeros((TA, GROUP), F32)

        @pl.when(i == n_tiles)
        def _():
            bins_ref[0] = _bias_bin_sums(db_ref, idx_ref)

    last = n_tiles - 1
    cur = pl.BlockSpec((TA, GROUP), lambda p, i: (jnp.minimum(i, last), p))
    older = pl.BlockSpec((TA, GROUP), lambda p, i: (jnp.maximum(i - 1, 0), p))
    dy_blk = pl.BlockSpec((TA, GROUP), lambda p, i: (jnp.minimum(i, last), N_PAIR + p))
    per_pair = pl.BlockSpec((1, BIN_ROWS, GROUP), lambda p, i: (p, 0, 0))

    def cols(first):
        return pl.BlockSpec((TA, GROUP), lambda p, i: (jnp.minimum(i, last), first + p))

    def kept(rows):
        return pl.BlockSpec((1, SUPERS_PER_TILE, rows, PAIR_LANES), lambda p, i: (p, jnp.minimum(i, last), 0, 0))

    def out(dtype):
        return jax.ShapeDtypeStruct((seq, ATTN_WIDTH), dtype)

    return pl.pallas_call(
        body,
        name="attn_bwd",
        grid=(N_PAIR, n_tiles + 1),
        in_specs=[cols(Q_BLOCK), cols(K_BLOCK), cols(V_BLOCK), cur, cols(AG_BLOCK), dy_blk, kept(WINDOW), kept(8),
                  VMEM],
        out_specs=[cur, older, older, cur, per_pair],
        out_shape=[out(BF16), out(BF16), out(BF16), out(BF16),
                   jax.ShapeDtypeStruct((N_PAIR, BIN_ROWS, GROUP), F32)],
        scratch_shapes=[
            pltpu.VMEM((2 * TA, GROUP), BF16), pltpu.VMEM((GROUP, 2 * TA), BF16),
            pltpu.VMEM((2 * TA, GROUP), F32), pltpu.VMEM((2 * TA, GROUP), F32),
            pltpu.VMEM((WINDOW, PAIR_LANES), F32)],
        compiler_params=pltpu.CompilerParams(
            dimension_semantics=("arbitrary", "arbitrary"), vmem_limit_bytes=VMEM_LIMIT),
    )(proj_bf, proj_bf, proj_bf, a, proj_bf, dy, e_all, inv_all, jnp.asarray(_rel_index_tile()))


def _out_loss(x, tgt, proj, ya, wout_full, g2, pw_bf, ps):
    seq = x.shape[0]
    to = min(seq, 2 * TS)
    n_tiles = seq // to

    def body(x_ref, t_ref, pv_ref, pg_ref, ya_ref, w_ref, g2_ref, pw_ref, ps_ref,
             dx2_ref, dy_ref, gw_ref, gg_ref, loss_ref, sq_ref, halo_ref, wt_ref):
        i = pl.program_id(0)

        @pl.when(i == 0)
        def _():
            gw_ref[...] = jnp.zeros_like(gw_ref)
            gg_ref[...] = jnp.zeros_like(gg_ref)
            sq_ref[...] = jnp.zeros_like(sq_ref)
            halo_ref[...] = jnp.zeros_like(halo_ref)
            for b in range(N_CHIP):
                wt_ref[b] = w_ref[b].T

        pv = pv_ref[...].astype(F32)
        pg = pg_ref[...].astype(F32)
        diffs = [d.astype(BF16) for d in _pool_diffs(pv, halo_ref[...], i)]
        halo_ref[...] = pv[to - HALO:, :]
        yp = ((_pool_mix(diffs, pw_ref) * ps_ref[...]) * (pg * _sigmoid(pg))).astype(BF16)
        g2v = g2_ref[...]

        def parts(rows):
            return [yp[rows, :SHARD_OUT], yp[rows, SHARD_OUT:], ya_ref[rows, :SHARD_OUT], ya_ref[rows, SHARD_OUT:]]

        def project(rows, ys):
            x2 = x_ref[rows]
            for b in range(N_CHIP):
                x2 = x2 + _dot(ys[b], w_ref[b])
            return x2

        def norm_loss(rows, x2):
            r = lax.rsqrt(jnp.mean(x2 * x2, axis=-1, keepdims=True) + EPS)
            xh = x2 * r
            diff = xh * g2v - t_ref[rows]
            sq_ref[...] += jnp.sum(diff * diff, axis=0, keepdims=True)
            dfin = diff * (1.0 / D_MODEL)
            gg_ref[...] += jnp.sum(dfin * xh, axis=0, keepdims=True)
            dxh = dfin * g2v
            dx2 = r * (dxh - xh * jnp.mean(dxh * xh, axis=-1, keepdims=True))
            dx2_ref[rows] = dx2
            return dx2.astype(BF16)

        def back(rows, ys, dx2_bf):
            for b in range(N_CHIP):
                gw_ref[b] += _tdot(ys[b], dx2_bf)
                dy_ref[rows, b * SHARD_OUT:(b + 1) * SHARD_OUT] = _dot(dx2_bf, wt_ref[b]).astype(BF16)

        n_parts = 2
        part = to // n_parts
        spans = [slice(r * part, (r + 1) * part) for r in range(n_parts)]
        ys = [parts(rows) for rows in spans]
        x2_next = project(spans[0], ys[0])
        for r in range(n_parts):
            x2 = x2_next
            if r + 1 < n_parts:
                x2_next = project(spans[r + 1], ys[r + 1])
            back(spans[r], ys[r], norm_loss(spans[r], x2))

        @pl.when(i == n_tiles - 1)
        def _():
            total = jnp.sum(sq_ref[...], axis=1, keepdims=True) * (0.5 / D_MODEL)
            loss_ref[...] = jnp.broadcast_to(total, loss_ref.shape)

    def rows(width, col=0):
        return pl.BlockSpec((to, width), lambda i: (i, col))

    return pl.pallas_call(
        body,
        name="out_loss",
        grid=(n_tiles,),
        in_specs=[rows(D_MODEL), rows(D_MODEL), rows(POOL_WIDTH, 0), rows(POOL_WIDTH, 1), rows(ATTN_WIDTH),
                  VMEM, VMEM, VMEM, VMEM],
        out_specs=[rows(D_MODEL), rows(D_MODEL), VMEM, VMEM, VMEM],
        out_shape=[
            jax.ShapeDtypeStruct((seq, D_MODEL), F32), jax.ShapeDtypeStruct((seq, D_MODEL), BF16),
            jax.ShapeDtypeStruct((N_CHIP, SHARD_OUT, D_MODEL), F32), jax.ShapeDtypeStruct((1, D_MODEL), F32),
            jax.ShapeDtypeStruct((8, GROUP), F32)],
        scratch_shapes=[pltpu.VMEM((1, D_MODEL), F32), pltpu.VMEM((HALO, POOL_WIDTH), F32),
                        pltpu.VMEM((N_CHIP, D_MODEL, SHARD_OUT), BF16)],
        compiler_params=pltpu.CompilerParams(dimension_semantics=("arbitrary",), vmem_limit_bytes=VMEM_LIMIT),
    )(x, tgt, proj, proj, ya, wout_full, g2, pw_bf, ps)


def _inproj_bwd(x, dx2, dy, proj, dq, dk, dv, dag, win_full, pw_bf, g1, ps):
    seq = x.shape[0]
    n_tiles = seq // TS

    def body(x_ref, dx2_ref, dyp_ref, pv_ref, pvprev_ref, pg_ref, dq_ref, dk_ref, dv_ref, dag_ref,
             w_ref, pw_ref, g1_ref, ps_ref, gx_ref, dproj_ref, ht_ref, gg_ref, gps_ref, gpw_ref, halo_ref):
        i = pl.program_id(0)
        tile = n_tiles - 1 - i

        @pl.when(i == 0)
        def _():
            gg_ref[...] = jnp.zeros_like(gg_ref)
            gps_ref[...] = jnp.zeros_like(gps_ref)
            gpw_ref[...] = jnp.zeros_like(gpw_ref)
            halo_ref[...] = jnp.zeros_like(halo_ref)

        pv_t = pv_ref[...].astype(F32)
        pg_t = pg_ref[...].astype(F32)
        prev_rows = jnp.where(tile > 0, pvprev_ref[...].astype(F32), 0.0)
        diffs = [d.astype(BF16) for d in _pool_diffs(pv_t, prev_rows, tile)]
        mixed = _pool_mix(diffs, pw_ref)
        sg = _sigmoid(pg_t)
        silu = pg_t * sg
        dyp = dyp_ref[...].astype(F32)
        psv = ps_ref[...]
        gps_ref[...] += jnp.sum(dyp * mixed * silu, axis=0, keepdims=True)
        dmixed = (dyp * psv * silu).astype(BF16)
        dpg = dyp * (mixed * psv) * (sg * (1.0 + pg_t * (1.0 - sg)))
        counts = _row_counts(tile, TS)
        dds = []
        for g in range(len(POOL_WINDOWS)):
            dm_g = dmixed[:, g * GROUP:(g + 1) * GROUP]
            gpw_ref[g] += _tdot(diffs[g], dm_g)
            dds.append(_dot_t(dm_g, pw_ref[g]))
        dd = jnp.concatenate(dds, axis=1)
        spread = jnp.concatenate([dds[g] / counts[g] for g in range(len(POOL_WINDOWS))], axis=1)
        sums = _window_sums(jnp.concatenate([spread, halo_ref[...]], axis=0), forward=True)
        halo_ref[...] = spread[:HALO]
        dpv = jnp.concatenate(
            [sums[g][:TS, g * GROUP:(g + 1) * GROUP] for g in range(len(POOL_WINDOWS))], axis=1) - dd

        xt = x_ref[...]
        r = lax.rsqrt(jnp.mean(xt * xt, axis=-1, keepdims=True) + EPS)
        xh = xt * r
        g1v = g1_ref[...]
        ht_ref[...] = (xh * g1v).astype(BF16).T
        dproj = jnp.concatenate(
            [dpv.astype(BF16), dpg.astype(BF16), dq_ref[...], dk_ref[...], dv_ref[...], dag_ref[...]],
            axis=1)
        dproj_ref[...] = dproj
        dh = _dot_t(dproj[:, :SHARD_IN], w_ref[0])
        for chip in range(1, N_CHIP):
            dh = dh + _dot_t(dproj[:, chip * SHARD_IN:(chip + 1) * SHARD_IN], w_ref[chip])

        gg_ref[...] += jnp.sum(dh * xh, axis=0, keepdims=True)
        dxh = dh * g1v
        gx_ref[...] = dx2_ref[...] + r * (dxh - xh * jnp.mean(dxh * xh, axis=-1, keepdims=True))

    def rows(width, col=0):
        return pl.BlockSpec((TS, width), lambda i: (n_tiles - 1 - i, col))

    prev = pl.BlockSpec((HALO, POOL_WIDTH), lambda i: (jnp.maximum((n_tiles - 1 - i) * (TS // HALO) - 1, 0), 0))
    return pl.pallas_call(
        body,
        name="inproj_bwd",
        grid=(n_tiles,),
        in_specs=[rows(D_MODEL), rows(D_MODEL), rows(POOL_WIDTH), rows(POOL_WIDTH), prev, rows(POOL_WIDTH, 1),
                  rows(ATTN_WIDTH), rows(ATTN_WIDTH), rows(ATTN_WIDTH), rows(ATTN_WIDTH), VMEM, VMEM, VMEM, VMEM],
        out_specs=[rows(D_MODEL), rows(N_CHIP * SHARD_IN),
                   pl.BlockSpec((D_MODEL, TS), lambda i: (0, n_tiles - 1 - i)), VMEM, VMEM, VMEM],
        out_shape=[
            jax.ShapeDtypeStruct((seq, D_MODEL), F32),
            jax.ShapeDtypeStruct((seq, N_CHIP * SHARD_IN), BF16),
            jax.ShapeDtypeStruct((D_MODEL, seq), BF16),
            jax.ShapeDtypeStruct((1, D_MODEL), F32),
            jax.ShapeDtypeStruct((1, POOL_WIDTH), F32),
            jax.ShapeDtypeStruct((len(POOL_WINDOWS), GROUP, GROUP), F32)],
        scratch_shapes=[pltpu.VMEM((HALO, POOL_WIDTH), F32)],
        compiler_params=pltpu.CompilerParams(dimension_semantics=("arbitrary",), vmem_limit_bytes=VMEM_LIMIT),
    )(x, dx2, dy, proj, proj, proj, dq, dk, dv, dag, win_full, pw_bf, g1, ps)


PASS_PEER = (2, 0, 1)


def _gw_reduce(ht, dproj, gwout, small):
    seq = ht.shape[1]
    tm = min(2 * TS, seq // 4)
    n_tiles = seq // tm
    half_small = SMALL_ROWS // 2
    cx, cy = lax.axis_index("x"), lax.axis_index("y")
    outer = _other_chips(cx, cy)
    order = jnp.stack([outer[n][1] for n in PASS_PEER] + [2 * cx + cy]).astype(jnp.int32)

    def body(order_ref, ht_ref, dp_ref, gwout_ref, small_ref, gin_final, gout_final, small_final,
             hbuf, acc, pair_in, pair_out, pair_small, tx_in, tx_out, rx_in, rx_out, rx_small,
             gin_out, gout_out, small_out, send_sems, recv_sems, out_sems):
        j = pl.program_id(0)
        i = pl.program_id(1)
        x, y, c = _my_place()
        b = 2 * x + y
        sibling = (x, y, 1 - c)
        others = _other_chips(x, y)
        mine_in = pl.ds(pl.multiple_of(c * HALF_IN, HALF_IN), HALF_IN)
        mine_out = pl.ds(pl.multiple_of(c * HALF_OUT, HALF_OUT), HALF_OUT)
        mine_small = pl.ds(pl.multiple_of(c * half_small, 8), half_small)
        theirs_in = pl.ds(pl.multiple_of((1 - c) * HALF_IN, HALF_IN), HALF_IN)
        theirs_out = pl.ds(pl.multiple_of((1 - c) * HALF_OUT, HALF_OUT), HALF_OUT)
        theirs_small = pl.ds(pl.multiple_of((1 - c) * half_small, 8), half_small)

        def copy(k, src, dst, to):
            return pltpu.make_async_remote_copy(
                src_ref=src, dst_ref=dst, send_sem=send_sems.at[k], recv_sem=recv_sems.at[k],
                device_id=to, device_id_type=MESH)

        swap_out = copy(0, gwout_ref.at[:, theirs_out], pair_out, sibling)
        swap_small = copy(1, small_ref, pair_small, sibling)

        def swap_in(p):
            return copy(2 + p, acc.at[p % 2, theirs_in], pair_in.at[p % 2], sibling)

        def to_chip(n, t):
            to = (*others[n][0], c)
            if t == 0:
                return copy(6 + 3 * n, tx_in.at[n], rx_in.at[n], to)
            if t == 1:
                return copy(7 + 3 * n, tx_out.at[n], rx_out.at[n], to)
            return copy(8 + 3 * n, pair_small.at[mine_small], rx_small.at[b], to)

        share_in = copy(15, gin_out.at[mine_in], gin_out.at[mine_in], sibling)
        share_out = copy(16, gout_out.at[mine_out], gout_out.at[mine_out], sibling)
        share_small = copy(17, small_out.at[mine_small], small_out.at[mine_small], sibling)

        def at(jj, ii):
            return (j == jj) & (i == ii)

        par = j % 2

        @pl.when(i == 0)
        def _():
            acc[par] = jnp.zeros((D_MODEL, SHARD_IN), F32)

        cols = pl.ds(pl.multiple_of(i * tm, tm), tm)

        @pl.when(j == 0)
        def _():
            hbuf[:, cols] = ht_ref[...]

        acc[par] += _dot(hbuf[:, cols], dp_ref[...])

        @pl.when(at(0, 0))
        def _():
            swap_out.start()
            swap_small.start()

        @pl.when(at(0, 2))
        def _():
            swap_out.wait_recv()
            swap_small.wait_recv()
            for chip in range(N_CHIP):
                pair_out[chip] = gwout_ref[chip, mine_out] + pair_out[chip]
            pair_small[...] = small_ref[...] + pair_small[...]
            rx_small[b] = pair_small[mine_small]
            for n in range(N_CHIP - 1):
                tx_out[n] = pair_out[others[n][1]].astype(BF16)
                to_chip(n, 1).start()
                to_chip(n, 2).start()

        @pl.when(at(1, 3))
        def _():
            total_out = pair_out[b]
            for n in range(N_CHIP - 1):
                to_chip(n, 1).wait_recv()
                copy(8 + 3 * n, pair_small.at[mine_small], rx_small.at[others[n][1]], (*others[n][0], c)).wait_recv()
                total_out = total_out + rx_out[n].astype(F32)
            gout_out[mine_out] = total_out
            small_out[mine_small] = ((rx_small[0] + rx_small[1]) + rx_small[2]) + rx_small[3]
            share_out.start()
            share_small.start()

        for p in range(N_CHIP - 1):
            n = PASS_PEER[p]

            @pl.when(at(p + 1, 0))
            def _(p=p):
                swap_in(p).start()

            @pl.when(at(p + 1, 2))
            def _(p=p, n=n):
                swap_in(p).wait_recv()
                swap_in(p).wait_send()
                tx_in[n] = (acc[p % 2, mine_in] + pair_in[p % 2]).astype(BF16)
                to_chip(n, 0).start()

        @pl.when(at(N_CHIP - 1, n_tiles - 1))
        def _():
            last = N_CHIP - 1
            swap_in(last).start()
            swap_in(last).wait_recv()
            total_in = acc[last % 2, mine_in] + pair_in[last % 2]
            for n in range(N_CHIP - 1):
                to_chip(n, 0).wait_recv()
                total_in = total_in + rx_in[n].astype(F32)
            gin_out[mine_in] = total_in
            share_in.start()
            copy(15, gin_out.at[theirs_in], gin_out.at[theirs_in], sibling).wait_recv()
            copy(16, gout_out.at[theirs_out], gout_out.at[theirs_out], sibling).wait_recv()
            copy(17, small_out.at[theirs_small], small_out.at[theirs_small], sibling).wait_recv()
            swap_out.wait_send()
            swap_small.wait_send()
            swap_in(last).wait_send()
            for n in range(N_CHIP - 1):
                for t in range(3):
                    to_chip(n, t).wait_send()
            share_in.wait_send()
            share_out.wait_send()
            share_small.wait_send()
            outs = [pltpu.make_async_copy(src, dst, out_sems.at[k]) for k, (src, dst) in enumerate(
                [(gin_out, gin_final), (gout_out, gout_final), (small_out, small_final)])]
            for cp in outs:
                cp.start()
            for cp in outs:
                cp.wait()

    assert n_tiles >= 4, "the reduction's steps are spread over the first four token steps of a pass"
    grid_spec = pltpu.PrefetchScalarGridSpec(
        num_scalar_prefetch=1,
        grid=(N_CHIP, n_tiles),
        in_specs=[
            pl.BlockSpec((D_MODEL, tm), lambda j, i, o: (0, jnp.where(j == 0, i, n_tiles - 1))),
            pl.BlockSpec((tm, SHARD_IN), lambda j, i, o: (i, o[j])),
            VMEM, VMEM],
        out_specs=[ANY, ANY, ANY],
        scratch_shapes=[
            pltpu.VMEM((D_MODEL, seq), BF16),
            pltpu.VMEM((2, D_MODEL, SHARD_IN), F32),
            pltpu.VMEM((2, HALF_IN, SHARD_IN), F32),
            pltpu.VMEM((N_CHIP, HALF_OUT, D_MODEL), F32),
            pltpu.VMEM((SMALL_ROWS, GROUP), F32),
            pltpu.VMEM((N_CHIP - 1, HALF_IN, SHARD_IN), BF16),
            pltpu.VMEM((N_CHIP - 1, HALF_OUT, D_MODEL), BF16),
            pltpu.VMEM((N_CHIP - 1, HALF_IN, SHARD_IN), BF16),
            pltpu.VMEM((N_CHIP - 1, HALF_OUT, D_MODEL), BF16),
            pltpu.VMEM((N_CHIP, half_small, GROUP), F32),
            pltpu.VMEM((D_MODEL, SHARD_IN), F32),
            pltpu.VMEM((SHARD_OUT, D_MODEL), F32),
            pltpu.VMEM((SMALL_ROWS, GROUP), F32),
            pltpu.SemaphoreType.DMA((18,)),
            pltpu.SemaphoreType.DMA((18,)),
            pltpu.SemaphoreType.DMA((3,)),
        ],
    )
    return pl.pallas_call(
        body,
        name="gw_reduce",
        grid_spec=grid_spec,
        out_shape=(
            jax.ShapeDtypeStruct((D_MODEL, SHARD_IN), F32),
            jax.ShapeDtypeStruct((SHARD_OUT, D_MODEL), F32),
            jax.ShapeDtypeStruct((SMALL_ROWS, GROUP), F32),
        ),
        compiler_params=pltpu.CompilerParams(
            dimension_semantics=("arbitrary", "arbitrary"), vmem_limit_bytes=VMEM_LIMIT),
    )(order, ht, dproj, gwout, small)


def _adamw(name, w, g, m, v, block_rows):
    rows, cols = w.shape

    def body(w_ref, g_ref, m_ref, v_ref, d_ref, m_out, v_out):
        grad = g_ref[...]
        m_new = ADAM_B1 * m_ref[...] + (1.0 - ADAM_B1) * grad
        v_new = ADAM_B2 * v_ref[...] + (1.0 - ADAM_B2) * (grad * grad)
        m_hat = m_new / (1.0 - ADAM_B1 ** ADAM_STEP)
        v_hat = v_new / (1.0 - ADAM_B2 ** ADAM_STEP)
        d_ref[...] = -ADAM_LR * (m_hat / (jnp.sqrt(v_hat) + ADAM_EPS) + ADAM_WD * w_ref[...])
        m_out[...] = m_new
        v_out[...] = v_new

    blk = pl.BlockSpec((block_rows, cols), lambda i: (i, 0))
    shape = jax.ShapeDtypeStruct((rows, cols), F32)
    return pl.pallas_call(
        body,
        name=name,
        grid=(rows // block_rows,),
        in_specs=[blk] * 4,
        out_specs=[blk] * 3,
        out_shape=[shape] * 3,
        compiler_params=pltpu.CompilerParams(dimension_semantics=("arbitrary",)),
    )(w, g, m, v)


def _pack_small(norm_gain, pool_w, pool_scale, rel_bias, final_gain, loss_rows):
    parts = [
        norm_gain.reshape(8, GROUP),
        pool_w.reshape(len(POOL_WINDOWS) * GROUP, GROUP),
        jnp.pad(pool_scale.reshape(4, GROUP), ((0, 4), (0, 0))),
        jnp.pad(rel_bias.reshape(8, N_REL), ((0, 0), (0, 2 * GROUP - N_REL))).reshape(16, GROUP),
        final_gain.reshape(8, GROUP),
        loss_rows,
    ]
    return jnp.concatenate(parts, axis=0)


def _unpack_small(block):
    norm_gain = block[0:8].reshape(1, D_MODEL)
    pool_w = block[8:520].reshape(1, len(POOL_WINDOWS), GROUP, GROUP)
    pool_scale = block[520:524].reshape(1, POOL_WIDTH)
    rel_bias = block[528:544].reshape(8, 2 * GROUP)[:, :N_REL].reshape(1, 8, N_REL)
    final_gain = block[544:552].reshape(D_MODEL)
    return norm_gain, pool_w, pool_scale, rel_bias, final_gain


def kernel(x, norm_gain, w_in, pool_w, pool_scale, rel_bias, w_out, final_norm_gain, loss_target, m_norm_gain, m_w_in, m_pool_w, m_pool_scale, m_rel_bias, m_w_out, m_final_norm_gain, v_norm_gain, v_w_in, v_pool_w, v_pool_scale, v_rel_bias, v_w_out, v_final_norm_gain):
    assert x.shape[1] % TS == 0 and x.shape[2] == D_MODEL
    xs = x[0]
    tgt = loss_target[0]
    g1 = norm_gain.reshape(1, D_MODEL)
    g2 = final_norm_gain.reshape(1, D_MODEL)
    ps = pool_scale.reshape(1, POOL_WIDTH)
    pw_bf = pool_w[0].astype(BF16)

    proj, win_full, wout_full = _gather_inproj(xs, g1, w_in[0].astype(BF16), w_out[0].astype(BF16))
    bias_t = _bias_tile(rel_bias[0])

    a, ya, e_all, inv_all = _attn_fwd(proj, bias_t)
    dx2, dy, gwout, gg2, loss_rows = _out_loss(xs, tgt, proj, ya, wout_full, g2, pw_bf, ps)
    dq, dk, dv, dag, bins = _attn_bwd(proj, a, dy, e_all, inv_all)
    gx, dproj, ht, gg1, gps, gpw = _inproj_bwd(xs, dx2, dy, proj, dq, dk, dv, dag, win_full, pw_bf, g1, ps)

    g_bias = bins[:, :N_REL, :2].transpose(0, 2, 1).reshape(8, N_REL)
    small = _pack_small(gg1, gpw, gps, g_bias, gg2, loss_rows)
    g_win, g_wout, g_small = _gw_reduce(ht, dproj, gwout, small)
    loss = g_small[LOSS_ROW, 0]

    zeros8 = jnp.zeros((8, GROUP), F32)
    w_small = _pack_small(norm_gain, pool_w, pool_scale, rel_bias, final_norm_gain, zeros8)
    m_small = _pack_small(m_norm_gain, m_pool_w, m_pool_scale, m_rel_bias, m_final_norm_gain, zeros8)
    v_small = _pack_small(v_norm_gain, v_pool_w, v_pool_scale, v_rel_bias, v_final_norm_gain, zeros8)

    d_win, m_win, v_win = _adamw("adamw_w_in", w_in[0], g_win, m_w_in[0], v_w_in[0], 256)
    d_wout, m_wout, v_wout = _adamw("adamw_w_out", w_out[0], g_wout, m_w_out[0], v_w_out[0], 128)
    d_small, m_new_small, v_new_small = _adamw("adamw_small", w_small, g_small, m_small, v_small, SMALL_ROWS // 2)

    def full(win_part, wout_part, block):
        ng, pw, psc, rb, fg = _unpack_small(block)
        return [ng, win_part[None], pw, psc, rb, wout_part[None], fg]

    grads = full(g_win, g_wout, g_small)
    deltas = full(d_win, d_wout, d_small)
    new_m = full(m_win, m_wout, m_new_small)
    new_v = full(v_win, v_wout, v_new_small)
    return (loss, gx[None], *grads, *deltas, *new_m, *new_v)
```

```python
import numpy as np
import jax
import jax.numpy as jnp
from jax import lax
from jax.experimental import pallas as pl
from jax.experimental.pallas import tpu as pltpu

F32 = jnp.float32
BF16 = jnp.bfloat16

D_MODEL = 1024
POOL_WIDTH = 512
ATTN_WIDTH = 512
POOL_WINDOWS = (2, 4, 8, 16)
GROUP = 128
CHUNK = 64
LEFT_CHUNKS = 8
BAND = (LEFT_CHUNKS + 1) * CHUNK
HEAD_DIM = 64
N_PAIR = 4
MAX_REL = 64
N_REL = 2 * MAX_REL + 1
EPS = 1e-6
MASK_VALUE = -1e30
SCALE = 0.125
LOG2E = 1.4426950408889634

ADAM_LR = 0.001
ADAM_B1 = 0.9
ADAM_B2 = 0.999
ADAM_EPS = 1e-08
ADAM_WD = 0.01
ADAM_STEP = 10

TS = LEFT_CHUNKS * CHUNK
SUPER = 2 * CHUNK
WINDOW = BAND + CHUNK
TA = 4 * TS
WIN_BASE = TA - LEFT_CHUNKS * CHUNK
SUPERS_PER_TILE = TA // SUPER
PAIR_LANES = 2 * SUPER
HALO = 16
N_CHIP = 4
SHARD_IN = 768
SHARD_OUT = 256
PIECE = 256
PIECES_PER_SHARD = SHARD_IN // PIECE
HALF_IN = D_MODEL // 2
HALF_OUT = SHARD_OUT // 2
SMALL_ROWS = 560
LOSS_ROW = 552
VMEM_LIMIT = 60 * 1024 * 1024

MESH = pl.DeviceIdType.MESH
ANY = pl.BlockSpec(memory_space=pl.ANY)
VMEM = pl.BlockSpec(memory_space=pltpu.VMEM)


def _sigmoid(x):
    return 1.0 / (1.0 + jnp.exp(-x))


def _dot(a, b):
    return jnp.dot(a, b, preferred_element_type=F32)


def _dot_t(a, b):
    return lax.dot_general(a, b, (((1,), (1,)), ((), ())), preferred_element_type=F32)


def _tdot(a, b):
    return lax.dot_general(a, b, (((0,), (0,)), ((), ())), preferred_element_type=F32)


def _my_place():
    return lax.axis_index("x"), lax.axis_index("y"), lax.axis_index("c")


def _other_chips(x, y):
    places = [(1 - x, y), (x, 1 - y), (1 - x, 1 - y)]
    return [(p, 2 * p[0] + p[1]) for p in places]


def _gather_inproj(x, g1, win_sh, wout_sh):
    seq = x.shape[0]
    tm = min(seq, 4 * TS)
    n_tiles = seq // tm
    cx, cy = lax.axis_index("x"), lax.axis_index("y")
    order = jnp.stack([2 * cx + cy] + [chip for _, chip in _other_chips(cx, cy)]).astype(jnp.int32)

    def body(order_ref, x_ref, g1_ref, win_ref, wout_ref, proj_bf_ref, win_full, wout_full,
             hbuf, wbuf, send_sems, recv_sems, local_sems):
        j = pl.program_id(0)
        i = pl.program_id(1)
        x_, y_, c = _my_place()
        b = 2 * x_ + y_
        sibling = (x_, y_, 1 - c)
        others = _other_chips(x_, y_)

        def halves(chip, core):
            return (
                win_full.at[chip, pl.ds(core * HALF_IN, HALF_IN)],
                wout_full.at[chip, pl.ds(core * HALF_OUT, HALF_OUT)],
            )

        def copy(k, src, dst, to):
            return pltpu.make_async_remote_copy(
                src_ref=src, dst_ref=dst, send_sem=send_sems.at[k], recv_sem=recv_sems.at[k],
                device_id=to, device_id_type=MESH)

        own = [
            pltpu.make_async_copy(win_ref, win_full.at[b], local_sems.at[0]),
            pltpu.make_async_copy(wout_ref, wout_full.at[b], local_sems.at[1]),
        ]
        mine_src = (win_ref.at[pl.ds(c * HALF_IN, HALF_IN)], wout_ref.at[pl.ds(c * HALF_OUT, HALF_OUT)])

        def direct(n, t):
            return copy(2 * n + t, mine_src[t], halves(b, c)[t], (*others[n][0], c))

        def arrival(n, t):
            landed = halves(others[n][1], c)[t]
            return copy(2 * n + t, landed, landed, (*others[n][0], c))

        def passing(n, t):
            landed = halves(others[n][1], c)[t]
            return copy(6 + 2 * n + t, landed, landed, sibling)

        def from_sibling(n, t):
            landed = halves(others[n][1], 1 - c)[t]
            return copy(6 + 2 * n + t, landed, landed, sibling)

        @pl.when((j == 0) & (i == 0))
        def _():
            first = pltpu.make_async_copy(win_ref, wbuf.at[0], local_sems.at[2])
            first.start()
            for cp in own:
                cp.start()
            for t in range(2):
                for n in range(N_CHIP - 1):
                    direct(n, t).start()
            first.wait()

        def load(n):
            return pltpu.make_async_copy(win_full.at[others[n][1]], wbuf.at[(n + 1) % 2], local_sems.at[2])

        for n in range(N_CHIP - 1):
            @pl.when((j == n) & (i == n_tiles - 1))
            def _(n=n):
                arrival(n, 0).wait_recv()
                passing(n, 0).start()

            @pl.when((j == n + 1) & (i == 0))
            def _(n=n):
                load(n).wait()

        rows = pl.ds(pl.multiple_of(i * tm, tm), tm)

        @pl.when(j == 0)
        def _():
            xt = x_ref[...]
            r = lax.rsqrt(jnp.mean(xt * xt, axis=-1, keepdims=True) + EPS)
            hbuf[rows] = ((xt * r) * g1_ref[...]).astype(BF16)

        proj_bf_ref[...] = _dot(hbuf[rows], wbuf[j % 2]).astype(BF16)

        for n in range(N_CHIP - 1):
            @pl.when((j == n) & (i == n_tiles - 1))
            def _(n=n):
                from_sibling(n, 0).wait_recv()
                load(n).start()

        @pl.when((j == N_CHIP - 1) & (i == n_tiles - 1))
        def _():
            for n in range(N_CHIP - 1):
                arrival(n, 1).wait_recv()
                passing(n, 1).start()
            for n in range(N_CHIP - 1):
                from_sibling(n, 1).wait_recv()
            for n in range(N_CHIP - 1):
                for t in range(2):
                    direct(n, t).wait_send()
                    passing(n, t).wait_send()
            for cp in own:
                cp.wait()

    last = n_tiles - 1
    grid_spec = pltpu.PrefetchScalarGridSpec(
        num_scalar_prefetch=1,
        grid=(N_CHIP, n_tiles),
        in_specs=[
            pl.BlockSpec((tm, D_MODEL), lambda j, i, o: (jnp.where(j == 0, i, last), 0)),
            VMEM, ANY, ANY],
        out_specs=[pl.BlockSpec((tm, SHARD_IN), lambda j, i, o: (i, o[j])), ANY, ANY],
        scratch_shapes=[
            pltpu.VMEM((seq, D_MODEL), BF16), pltpu.VMEM((2, D_MODEL, SHARD_IN), BF16),
            pltpu.SemaphoreType.DMA((12,)), pltpu.SemaphoreType.DMA((12,)), pltpu.SemaphoreType.DMA((3,))],
    )
    return pl.pallas_call(
        body,
        name="gather_inproj",
        grid_spec=grid_spec,
        out_shape=(
            jax.ShapeDtypeStruct((seq, N_CHIP * SHARD_IN), BF16),
            jax.ShapeDtypeStruct((N_CHIP, D_MODEL, SHARD_IN), BF16),
            jax.ShapeDtypeStruct((N_CHIP, SHARD_OUT, D_MODEL), BF16),
        ),
        compiler_params=pltpu.CompilerParams(
            dimension_semantics=("arbitrary", "arbitrary"), vmem_limit_bytes=VMEM_LIMIT),
    )(order, x, g1, win_sh, wout_sh)


def _window_sums(ext, forward):
    n = ext.shape[0]
    sums = []
    acc = ext
    for step in (1, 2, 4, 8):
        acc = acc + pltpu.roll(acc, (n - step) if forward else step, 0)
        sums.append(acc)
    return sums


def _row_counts(tile, rows):
    t = tile * rows + lax.broadcasted_iota(jnp.int32, (rows, GROUP), 0)
    return [jnp.minimum(t + 1, w).astype(F32) for w in POOL_WINDOWS]


def _pool_diffs(pv, prev_rows, tile):
    ext = jnp.concatenate([prev_rows, pv], axis=0)
    sums = _window_sums(ext, forward=False)
    counts = _row_counts(tile, pv.shape[0])
    out = []
    for g in range(len(POOL_WINDOWS)):
        cols = slice(g * GROUP, (g + 1) * GROUP)
        out.append(sums[g][HALO:, cols] / counts[g] - pv[:, cols])
    return out


def _pool_mix(diffs, pw_ref):
    return jnp.concatenate([_dot(diffs[g], pw_ref[g]) for g in range(len(POOL_WINDOWS))], axis=1)


Q_BLOCK, K_BLOCK, V_BLOCK, AG_BLOCK = 8, 12, 16, 20


def _bias_tile(rel_bias):
    flat = jnp.concatenate(
        [jnp.broadcast_to(rel_bias[:, :1], (rel_bias.shape[0], BAND - CHUNK - 1)), rel_bias[:, :2 * MAX_REL]], axis=1)
    rows = [flat[:, CHUNK - 1 - i:CHUNK - 1 - i + BAND] for i in range(CHUNK)]
    bias = jnp.stack(rows, axis=1)
    first = jnp.pad(bias, ((0, 0), (0, 0), (0, CHUNK)), constant_values=MASK_VALUE)
    second = jnp.pad(bias, ((0, 0), (0, 0), (CHUNK, 0)), constant_values=MASK_VALUE)
    both = jnp.concatenate([first, second], axis=1)
    return both.reshape(N_PAIR, PAIR_LANES, WINDOW).transpose(0, 2, 1)


def _rel_index_tile():
    j = np.arange(WINDOW)[:, None]
    q = np.arange(PAIR_LANES)[None, :] % SUPER
    band_key = j - CHUNK * (q // CHUNK)
    idx = np.clip(band_key - LEFT_CHUNKS * CHUNK - q % CHUNK, -MAX_REL, MAX_REL) + MAX_REL
    return np.where((band_key >= 0) & (band_key < BAND), idx, -1).astype(np.int32)


def _by_head(block):
    low = lax.broadcasted_iota(jnp.int32, block.shape, 1) < HEAD_DIM
    zero = jnp.zeros_like(block)
    return jnp.concatenate([jnp.where(low, block, zero), jnp.where(low, zero, block)], axis=0)


def _own_head_rows(cross):
    head_of_row = lax.broadcasted_iota(jnp.int32, cross.shape, 0) >= HEAD_DIM
    head_of_lane = lax.broadcasted_iota(jnp.int32, cross.shape, 1) >= SUPER
    both = jnp.where(jnp.logical_xor(head_of_row, head_of_lane), 0.0, cross).T
    return both[:SUPER] + both[SUPER:]


def _with_mask_lane(q_rows):
    lane = lax.broadcasted_iota(jnp.int32, q_rows.shape, 1)
    return jnp.concatenate([q_rows, jnp.where(lane == 0, MASK_VALUE, 0.0).astype(q_rows.dtype)], axis=1)


def _band_exp(kb, q_rows, bias):
    s = _dot_t(kb, _with_mask_lane(q_rows)) + bias
    e = jnp.exp2(s - jnp.max(s, axis=0, keepdims=True))
    return e, jnp.sum(e, axis=0, keepdims=True)


def _window(sc):
    return slice(WIN_BASE + sc * SUPER, WIN_BASE + sc * SUPER + WINDOW)


def _shift_band(i, band_ref, new_ref):
    @pl.when(i == 0)
    def _():
        band_ref[:TA] = jnp.zeros((TA, GROUP), band_ref.dtype)

    @pl.when(i > 0)
    def _():
        band_ref[:TA] = band_ref[TA:]

    band_ref[TA:] = new_ref[...].astype(band_ref.dtype)


def _shift_key_band(i, band_ref, new_ref):
    @pl.when(i == 0)
    def _():
        lane = lax.broadcasted_iota(jnp.int32, (TA, 2 * GROUP), 1)
        band_ref[:TA] = jnp.where(lane == GROUP, 1.0, 0.0).astype(band_ref.dtype)
        band_ref[TA:, GROUP:] = jnp.zeros((TA, GROUP), band_ref.dtype)

    @pl.when(i > 0)
    def _():
        band_ref[:TA] = band_ref[TA:]

    band_ref[TA:, :GROUP] = new_ref[...].astype(band_ref.dtype)


def _shift_band_t(i, band_ref, new_ref):
    @pl.when(i == 0)
    def _():
        band_ref[:, :TA] = jnp.zeros((GROUP, TA), band_ref.dtype)

    @pl.when(i > 0)
    def _():
        band_ref[:, :TA] = band_ref[:, TA:]

    band_ref[:, TA:] = new_ref[...].astype(band_ref.dtype).T


def _scaled_queries(q_ref, rows, scale=SCALE):
    return _by_head((q_ref[rows] * scale).astype(BF16))


def _attn_fwd(proj_bf, bias_t):
    seq = proj_bf.shape[0]
    n_tiles = seq // TA

    def body(q_ref, k_ref, v_ref, ag_ref, bias_ref, a_ref, ya_ref, e_ref, inv_ref, kband, vband_t):
        i = pl.program_id(1)
        _shift_key_band(i, kband, k_ref)
        _shift_band_t(i, vband_t, v_ref)

        def weights(sc):
            rows = slice(sc * SUPER, (sc + 1) * SUPER)
            win = _window(sc)
            e, total = _band_exp(kband[win], _scaled_queries(q_ref, rows, SCALE * LOG2E), bias_ref[0])
            e_ref[0, sc] = e.astype(BF16)
            inv_total = 1.0 / total
            inv_ref[0, sc] = jnp.broadcast_to(inv_total, (8, PAIR_LANES))
            return inv_total

        nxt = weights(0)
        for sc in range(SUPERS_PER_TILE):
            rows = slice(sc * SUPER, (sc + 1) * SUPER)
            win = _window(sc)
            inv_total = nxt
            if sc + 1 < SUPERS_PER_TILE:
                nxt = weights(sc + 1)
            a = _own_head_rows(_dot(vband_t[:, win], e_ref[0, sc]) * inv_total)
            a_ref[rows] = a.astype(BF16)
            g = ag_ref[rows].astype(F32)
            ya_ref[rows] = (a * (g * _sigmoid(g))).astype(BF16)

    blk = pl.BlockSpec((TA, GROUP), lambda p, i: (i, p))

    def cols(first):
        return pl.BlockSpec((TA, GROUP), lambda p, i: (i, first + p))

    return pl.pallas_call(
        body,
        name="attn_fwd",
        grid=(N_PAIR, n_tiles),
        in_specs=[cols(Q_BLOCK), cols(K_BLOCK), cols(V_BLOCK), cols(AG_BLOCK),
                  pl.BlockSpec((1, WINDOW, PAIR_LANES), lambda p, i: (p, 0, 0))],
        out_specs=[blk, blk,
                   pl.BlockSpec((1, SUPERS_PER_TILE, WINDOW, PAIR_LANES), lambda p, i: (p, i, 0, 0)),
                   pl.BlockSpec((1, SUPERS_PER_TILE, 8, PAIR_LANES), lambda p, i: (p, i, 0, 0))],
        out_shape=[jax.ShapeDtypeStruct((seq, ATTN_WIDTH), BF16), jax.ShapeDtypeStruct((seq, ATTN_WIDTH), BF16),
                   jax.ShapeDtypeStruct((N_PAIR, seq // SUPER, WINDOW, PAIR_LANES), BF16),
                   jax.ShapeDtypeStruct((N_PAIR, seq // SUPER, 8, PAIR_LANES), F32)],
        scratch_shapes=[pltpu.VMEM((2 * TA, 2 * GROUP), BF16), pltpu.VMEM((GROUP, 2 * TA), BF16)],
        compiler_params=pltpu.CompilerParams(
            dimension_semantics=("arbitrary", "arbitrary"), vmem_limit_bytes=VMEM_LIMIT),
    )(proj_bf, proj_bf, proj_bf, proj_bf, bias_t)


BIN_ROWS = 136


def _bias_bin_sums(db_ref, idx_ref):
    lane = lax.broadcasted_iota(jnp.int32, (1, GROUP), 1)
    row = lax.broadcasted_iota(jnp.int32, (BIN_ROWS, GROUP), 0)
    out = jnp.zeros((BIN_ROWS, GROUP), F32)
    for r in range(N_REL - 1):
        lo = 0 if r == 0 else ((BAND - 2 * CHUNK + r) // 8) * 8
        hi = WINDOW if r == 0 else min(WINDOW, lo + SUPER + 8)
        hit = jnp.where(idx_ref[lo:hi] == r, db_ref[lo:hi], 0.0)
        col = jnp.sum(hit, axis=0, keepdims=True)
        s0 = jnp.sum(col[:, :SUPER], axis=1, keepdims=True)
        s1 = jnp.sum(col[:, SUPER:], axis=1, keepdims=True)
        val = jnp.where(lane == 0, s0, jnp.where(lane == 1, s1, 0.0))
        out = jnp.where(row == r, val, out)
    return out


def _attn_bwd(proj_bf, a, dy, e_all, inv_all):
    seq = proj_bf.shape[0]
    n_tiles = seq // TA

    def body(q_ref, k_ref, v_ref, a_ref, ag_ref, dy_ref, e_ref, inv_ref, idx_ref,
             dq_ref, dk_ref, dv_ref, dag_ref, bins_ref, vband, kband_t, dkacc, dvacc, db_ref):
        i = pl.program_id(1)

        @pl.when(i == 0)
        def _():
            dkacc[...] = jnp.zeros_like(dkacc)
            dvacc[...] = jnp.zeros_like(dvacc)
            db_ref[...] = jnp.zeros_like(db_ref)

        @pl.when(i < n_tiles)
        def _():
            _shift_band(i, vband, v_ref)
            _shift_band_t(i, kband_t, k_ref)

            def score_grads(sc):
                rows = slice(sc * SUPER, (sc + 1) * SUPER)
                win = _window(sc)
                q_rows = _scaled_queries(q_ref, rows)
                g = ag_ref[rows].astype(F32)
                sg = _sigmoid(g)
                dyc = dy_ref[rows].astype(F32)
                dag_ref[rows] = (dyc * a_ref[rows].astype(F32) * (sg * (1.0 + g * (1.0 - sg)))).astype(BF16)
                da_rows = _by_head((dyc * (g * sg)).astype(BF16))
                p = e_ref[0, sc].astype(F32) * inv_ref[0, sc, :1]
                dp = _dot_t(vband[win], da_rows)
                ds = p * (dp - jnp.sum(p * dp, axis=0, keepdims=True))
                db_ref[...] += ds
                return q_rows, da_rows, p.astype(BF16), ds.astype(BF16)

            nxt = score_grads(0)
            for sc in range(SUPERS_PER_TILE):
                rows = slice(sc * SUPER, (sc + 1) * SUPER)
                win = _window(sc)
                q_rows, da_rows, p_bf, ds_bf = nxt
                if sc + 1 < SUPERS_PER_TILE:
                    nxt = score_grads(sc + 1)
                dq_ref[rows] = (_own_head_rows(_dot(kband_t[:, win], ds_bf)) * SCALE).astype(BF16)
                dkacc[win] += _dot(ds_bf, q_rows)
                dvacc[win] += _dot(p_bf, da_rows)

        dk_ref[...] = dkacc[:TA].astype(BF16)
        dv_ref[...] = dvacc[:TA].astype(BF16)
        dkacc[:TA] = dkacc[TA:]
        dvacc[:TA] = dvacc[TA:]
        dkacc[TA:] = jnp.zeros((TA, GROUP), F32)
        dvacc[TA:] = jnp.zeros((TA, GROUP), F32)

        @pl.when(i == n_tiles)
        def _():
            bins_ref[0] = _bias_bin_sums(db_ref, idx_ref)

    last = n_tiles - 1
    cur = pl.BlockSpec((TA, GROUP), lambda p, i: (jnp.minimum(i, last), p))
    older = pl.BlockSpec((TA, GROUP), lambda p, i: (jnp.maximum(i - 1, 0), p))
    dy_blk = pl.BlockSpec((TA, GROUP), lambda p, i: (jnp.minimum(i, last), N_PAIR + p))
    per_pair = pl.BlockSpec((1, BIN_ROWS, GROUP), lambda p, i: (p, 0, 0))

    def cols(first):
        return pl.BlockSpec((TA, GROUP), lambda p, i: (jnp.minimum(i, last), first + p))

    def kept(rows):
        return pl.BlockSpec((1, SUPERS_PER_TILE, rows, PAIR_LANES), lambda p, i: (p, jnp.minimum(i, last), 0, 0))

    def out(dtype):
        return jax.ShapeDtypeStruct((seq, ATTN_WIDTH), dtype)

    return pl.pallas_call(
        body,
        name="attn_bwd",
        grid=(N_PAIR, n_tiles + 1),
        in_specs=[cols(Q_BLOCK), cols(K_BLOCK), cols(V_BLOCK), cur, cols(AG_BLOCK), dy_blk, kept(WINDOW), kept(8),
                  VMEM],
        out_specs=[cur, older, older, cur, per_pair],
        out_shape=[out(BF16), out(BF16), out(BF16), out(BF16),
                   jax.ShapeDtypeStruct((N_PAIR, BIN_ROWS, GROUP), F32)],
        scratch_shapes=[
            pltpu.VMEM((2 * TA, GROUP), BF16), pltpu.VMEM((GROUP, 2 * TA), BF16),
            pltpu.VMEM((2 * TA, GROUP), F32), pltpu.VMEM((2 * TA, GROUP), F32),
            pltpu.VMEM((WINDOW, PAIR_LANES), F32)],
        compiler_params=pltpu.CompilerParams(
            dimension_semantics=("arbitrary", "arbitrary"), vmem_limit_bytes=VMEM_LIMIT),
    )(proj_bf, proj_bf, proj_bf, a, proj_bf, dy, e_all, inv_all, jnp.asarray(_rel_index_tile()))


def _out_loss(x, tgt, proj, ya, wout_full, g2, pw_bf, ps):
    seq = x.shape[0]
    to = min(seq, 2 * TS)
    n_tiles = seq // to

    def body(x_ref, t_ref, pv_ref, pg_ref, ya_ref, w_ref, g2_ref, pw_ref, ps_ref,
             dx2_ref, dy_ref, gw_ref, gg_ref, loss_ref, sq_ref, halo_ref, wt_ref):
        i = pl.program_id(0)

        @pl.when(i == 0)
        def _():
            gw_ref[...] = jnp.zeros_like(gw_ref)
            gg_ref[...] = jnp.zeros_like(gg_ref)
            sq_ref[...] = jnp.zeros_like(sq_ref)
            halo_ref[...] = jnp.zeros_like(halo_ref)
            for b in range(N_CHIP):
                wt_ref[b] = w_ref[b].T

        pv = pv_ref[...].astype(F32)
        pg = pg_ref[...].astype(F32)
        diffs = [d.astype(BF16) for d in _pool_diffs(pv, halo_ref[...], i)]
        halo_ref[...] = pv[to - HALO:, :]
        yp = ((_pool_mix(diffs, pw_ref) * ps_ref[...]) * (pg * _sigmoid(pg))).astype(BF16)
        g2v = g2_ref[...]

        def parts(rows):
            return [yp[rows, :SHARD_OUT], yp[rows, SHARD_OUT:], ya_ref[rows, :SHARD_OUT], ya_ref[rows, SHARD_OUT:]]

        def project(rows, ys):
            x2 = x_ref[rows]
            for b in range(N_CHIP):
                x2 = x2 + _dot(ys[b], w_ref[b])
            return x2

        def norm_loss(rows, x2):
            r = lax.rsqrt(jnp.mean(x2 * x2, axis=-1, keepdims=True) + EPS)
            xh = x2 * r
            diff = xh * g2v - t_ref[rows]
            sq_ref[...] += jnp.sum(diff * diff, axis=0, keepdims=True)
            dfin = diff * (1.0 / D_MODEL)
            gg_ref[...] += jnp.sum(dfin * xh, axis=0, keepdims=True)
            dxh = dfin * g2v
            dx2 = r * (dxh - xh * jnp.mean(dxh * xh, axis=-1, keepdims=True))
            dx2_ref[rows] = dx2
            return dx2.astype(BF16)

        def back(rows, ys, dx2_bf):
            for b in range(N_CHIP):
                gw_ref[b] += _tdot(ys[b], dx2_bf)
                dy_ref[rows, b * SHARD_OUT:(b + 1) * SHARD_OUT] = _dot(dx2_bf, wt_ref[b]).astype(BF16)

        n_parts = 2
        part = to // n_parts
        spans = [slice(r * part, (r + 1) * part) for r in range(n_parts)]
        ys = [parts(rows) for rows in spans]
        x2_next = project(spans[0], ys[0])
        for r in range(n_parts):
            x2 = x2_next
            if r + 1 < n_parts:
                x2_next = project(spans[r + 1], ys[r + 1])
            back(spans[r], ys[r], norm_loss(spans[r], x2))

        @pl.when(i == n_tiles - 1)
        def _():
            total = jnp.sum(sq_ref[...], axis=1, keepdims=True) * (0.5 / D_MODEL)
            loss_ref[...] = jnp.broadcast_to(total, loss_ref.shape)

    def rows(width, col=0):
        return pl.BlockSpec((to, width), lambda i: (i, col))

    return pl.pallas_call(
        body,
        name="out_loss",
        grid=(n_tiles,),
        in_specs=[rows(D_MODEL), rows(D_MODEL), rows(POOL_WIDTH, 0), rows(POOL_WIDTH, 1), rows(ATTN_WIDTH),
                  VMEM, VMEM, VMEM, VMEM],
        out_specs=[rows(D_MODEL), rows(D_MODEL), VMEM, VMEM, VMEM],
        out_shape=[
            jax.ShapeDtypeStruct((seq, D_MODEL), F32), jax.ShapeDtypeStruct((seq, D_MODEL), BF16),
            jax.ShapeDtypeStruct((N_CHIP, SHARD_OUT, D_MODEL), F32), jax.ShapeDtypeStruct((1, D_MODEL), F32),
            jax.ShapeDtypeStruct((8, GROUP), F32)],
        scratch_shapes=[pltpu.VMEM((1, D_MODEL), F32), pltpu.VMEM((HALO, POOL_WIDTH), F32),
                        pltpu.VMEM((N_CHIP, D_MODEL, SHARD_OUT), BF16)],
        compiler_params=pltpu.CompilerParams(dimension_semantics=("arbitrary",), vmem_limit_bytes=VMEM_LIMIT),
    )(x, tgt, proj, proj, ya, wout_full, g2, pw_bf, ps)


def _inproj_bwd(x, dx2, dy, proj, dq, dk, dv, dag, win_full, pw_bf, g1, ps):
    seq = x.shape[0]
    n_tiles = seq // TS

    def body(x_ref, dx2_ref, dyp_ref, pv_ref, pvprev_ref, pg_ref, dq_ref, dk_ref, dv_ref, dag_ref,
             w_ref, pw_ref, g1_ref, ps_ref, gx_ref, dproj_ref, ht_ref, gg_ref, gps_ref, gpw_ref, halo_ref):
        i = pl.program_id(0)
        tile = n_tiles - 1 - i

        @pl.when(i == 0)
        def _():
            gg_ref[...] = jnp.zeros_like(gg_ref)
            gps_ref[...] = jnp.zeros_like(gps_ref)
            gpw_ref[...] = jnp.zeros_like(gpw_ref)
            halo_ref[...] = jnp.zeros_like(halo_ref)

        pv_t = pv_ref[...].astype(F32)
        pg_t = pg_ref[...].astype(F32)
        prev_rows = jnp.where(tile > 0, pvprev_ref[...].astype(F32), 0.0)
        diffs = [d.astype(BF16) for d in _pool_diffs(pv_t, prev_rows, tile)]
        mixed = _pool_mix(diffs, pw_ref)
        sg = _sigmoid(pg_t)
        silu = pg_t * sg
        dyp = dyp_ref[...].astype(F32)
        psv = ps_ref[...]
        gps_ref[...] += jnp.sum(dyp * mixed * silu, axis=0, keepdims=True)
        dmixed = (dyp * psv * silu).astype(BF16)
        dpg = dyp * (mixed * psv) * (sg * (1.0 + pg_t * (1.0 - sg)))
        counts = _row_counts(tile, TS)
        dds = []
        for g in range(len(POOL_WINDOWS)):
            dm_g = dmixed[:, g * GROUP:(g + 1) * GROUP]
            gpw_ref[g] += _tdot(diffs[g], dm_g)
            dds.append(_dot_t(dm_g, pw_ref[g]))
        dd = jnp.concatenate(dds, axis=1)
        spread = jnp.concatenate([dds[g] / counts[g] for g in range(len(POOL_WINDOWS))], axis=1)
        sums = _window_sums(jnp.concatenate([spread, halo_ref[...]], axis=0), forward=True)
        halo_ref[...] = spread[:HALO]
        dpv = jnp.concatenate(
            [sums[g][:TS, g * GROUP:(g + 1) * GROUP] for g in range(len(POOL_WINDOWS))], axis=1) - dd

        xt = x_ref[...]
        r = lax.rsqrt(jnp.mean(xt * xt, axis=-1, keepdims=True) + EPS)
        xh = xt * r
        g1v = g1_ref[...]
        ht_ref[...] = (xh * g1v).astype(BF16).T
        dproj = jnp.concatenate(
            [dpv.astype(BF16), dpg.astype(BF16), dq_ref[...], dk_ref[...], dv_ref[...], dag_ref[...]],
            axis=1)
        dproj_ref[...] = dproj
        dh = _dot_t(dproj[:, :SHARD_IN], w_ref[0])
        for chip in range(1, N_CHIP):
            dh = dh + _dot_t(dproj[:, chip * SHARD_IN:(chip + 1) * SHARD_IN], w_ref[chip])

        gg_ref[...] += jnp.sum(dh * xh, axis=0, keepdims=True)
        dxh = dh * g1v
        gx_ref[...] = dx2_ref[...] + r * (dxh - xh * jnp.mean(dxh * xh, axis=-1, keepdims=True))

    def rows(width, col=0):
        return pl.BlockSpec((TS, width), lambda i: (n_tiles - 1 - i, col))

    prev = pl.BlockSpec((HALO, POOL_WIDTH), lambda i: (jnp.maximum((n_tiles - 1 - i) * (TS // HALO) - 1, 0), 0))
    return pl.pallas_call(
        body,
        name="inproj_bwd",
        grid=(n_tiles,),
        in_specs=[rows(D_MODEL), rows(D_MODEL), rows(POOL_WIDTH), rows(POOL_WIDTH), prev, rows(POOL_WIDTH, 1),
                  rows(ATTN_WIDTH), rows(ATTN_WIDTH), rows(ATTN_WIDTH), rows(ATTN_WIDTH), VMEM, VMEM, VMEM, VMEM],
        out_specs=[rows(D_MODEL), rows(N_CHIP * SHARD_IN),
                   pl.BlockSpec((D_MODEL, TS), lambda i: (0, n_tiles - 1 - i)), VMEM, VMEM, VMEM],
        out_shape=[
            jax.ShapeDtypeStruct((seq, D_MODEL), F32),
            jax.ShapeDtypeStruct((seq, N_CHIP * SHARD_IN), BF16),
            jax.ShapeDtypeStruct((D_MODEL, seq), BF16),
            jax.ShapeDtypeStruct((1, D_MODEL), F32),
            jax.ShapeDtypeStruct((1, POOL_WIDTH), F32),
            jax.ShapeDtypeStruct((len(POOL_WINDOWS), GROUP, GROUP), F32)],
        scratch_shapes=[pltpu.VMEM((HALO, POOL_WIDTH), F32)],
        compiler_params=pltpu.CompilerParams(dimension_semantics=("arbitrary",), vmem_limit_bytes=VMEM_LIMIT),
    )(x, dx2, dy, proj, proj, proj, dq, dk, dv, dag, win_full, pw_bf, g1, ps)


PASS_PEER = (2, 0, 1)


def _gw_reduce(ht, dproj, gwout, small):
    seq = ht.shape[1]
    tm = min(2 * TS, seq // 4)
    n_tiles = seq // tm
    half_small = SMALL_ROWS // 2
    cx, cy = lax.axis_index("x"), lax.axis_index("y")
    outer = _other_chips(cx, cy)
    order = jnp.stack([outer[n][1] for n in PASS_PEER] + [2 * cx + cy]).astype(jnp.int32)

    def body(order_ref, ht_ref, dp_ref, gwout_ref, small_ref, gin_final, gout_final, small_final,
             hbuf, acc, pair_in, pair_out, pair_small, tx_in, tx_out, rx_in, rx_out, rx_small,
             gin_out, gout_out, small_out, send_sems, recv_sems, out_sems):
        j = pl.program_id(0)
        i = pl.program_id(1)
        x, y, c = _my_place()
        b = 2 * x + y
        sibling = (x, y, 1 - c)
        others = _other_chips(x, y)
        mine_in = pl.ds(pl.multiple_of(c * HALF_IN, HALF_IN), HALF_IN)
        mine_out = pl.ds(pl.multiple_of(c * HALF_OUT, HALF_OUT), HALF_OUT)
        mine_small = pl.ds(pl.multiple_of(c * half_small, 8), half_small)
        theirs_in = pl.ds(pl.multiple_of((1 - c) * HALF_IN, HALF_IN), HALF_IN)
        theirs_out = pl.ds(pl.multiple_of((1 - c) * HALF_OUT, HALF_OUT), HALF_OUT)
        theirs_small = pl.ds(pl.multiple_of((1 - c) * half_small, 8), half_small)

        def copy(k, src, dst, to):
            return pltpu.make_async_remote_copy(
                src_ref=src, dst_ref=dst, send_sem=send_sems.at[k], recv_sem=recv_sems.at[k],
                device_id=to, device_id_type=MESH)

        swap_out = copy(0, gwout_ref.at[:, theirs_out], pair_out, sibling)
        swap_small = copy(1, small_ref, pair_small, sibling)

        def swap_in(p):
            return copy(2 + p, acc.at[p % 2, theirs_in], pair_in.at[p % 2], sibling)

        def to_chip(n, t):
            to = (*others[n][0], c)
            if t == 0:
                return copy(6 + 3 * n, tx_in.at[n], rx_in.at[n], to)
            if t == 1:
                return copy(7 + 3 * n, tx_out.at[n], rx_out.at[n], to)
            return copy(8 + 3 * n, pair_small.at[mine_small], rx_small.at[b], to)

        share_in = copy(15, gin_out.at[mine_in], gin_out.at[mine_in], sibling)
        share_out = copy(16, gout_out.at[mine_out], gout_out.at[mine_out], sibling)
        share_small = copy(17, small_out.at[mine_small], small_out.at[mine_small], sibling)

        def at(jj, ii):
            return (j == jj) & (i == ii)

        par = j % 2

        @pl.when(i == 0)
        def _():
            acc[par] = jnp.zeros((D_MODEL, SHARD_IN), F32)

        cols = pl.ds(pl.multiple_of(i * tm, tm), tm)

        @pl.when(j == 0)
        def _():
            hbuf[:, cols] = ht_ref[...]

        acc[par] += _dot(hbuf[:, cols], dp_ref[...])

        @pl.when(at(0, 0))
        def _():
            swap_out.start()
            swap_small.start()

        @pl.when(at(0, 2))
        def _():
            swap_out.wait_recv()
            swap_small.wait_recv()
            for chip in range(N_CHIP):
                pair_out[chip] = gwout_ref[chip, mine_out] + pair_out[chip]
            pair_small[...] = small_ref[...] + pair_small[...]
            rx_small[b] = pair_small[mine_small]
            for n in range(N_CHIP - 1):
                tx_out[n] = pair_out[others[n][1]].astype(BF16)
                to_chip(n, 1).start()
                to_chip(n, 2).start()

        @pl.when(at(1, 3))
        def _():
            total_out = pair_out[b]
            for n in range(N_CHIP - 1):
                to_chip(n, 1).wait_recv()
                copy(8 + 3 * n, pair_small.at[mine_small], rx_small.at[others[n][1]], (*others[n][0], c)).wait_recv()
                total_out = total_out + rx_out[n].astype(F32)
            gout_out[mine_out] = total_out
            small_out[mine_small] = ((rx_small[0] + rx_small[1]) + rx_small[2]) + rx_small[3]
            share_out.start()
            share_small.start()

        for p in range(N_CHIP - 1):
            n = PASS_PEER[p]

            @pl.when(at(p + 1, 0))
            def _(p=p):
                swap_in(p).start()

            @pl.when(at(p + 1, 2))
            def _(p=p, n=n):
                swap_in(p).wait_recv()
                swap_in(p).wait_send()
                tx_in[n] = (acc[p % 2, mine_in] + pair_in[p % 2]).astype(BF16)
                to_chip(n, 0).start()

        @pl.when(at(N_CHIP - 1, n_tiles - 1))
        def _():
            last = N_CHIP - 1
            swap_in(last).start()
            swap_in(last).wait_recv()
            total_in = acc[last % 2, mine_in] + pair_in[last % 2]
            for n in range(N_CHIP - 1):
                to_chip(n, 0).wait_recv()
                total_in = total_in + rx_in[n].astype(F32)
            gin_out[mine_in] = total_in
            share_in.start()
            copy(15, gin_out.at[theirs_in], gin_out.at[theirs_in], sibling).wait_recv()
            copy(16, gout_out.at[theirs_out], gout_out.at[theirs_out], sibling).wait_recv()
            copy(17, small_out.at[theirs_small], small_out.at[theirs_small], sibling).wait_recv()
            swap_out.wait_send()
            swap_small.wait_send()
            swap_in(last).wait_send()
            for n in range(N_CHIP - 1):
                for t in range(3):
                    to_chip(n, t).wait_send()
            share_in.wait_send()
            share_out.wait_send()
            share_small.wait_send()
            outs = [pltpu.make_async_copy(src, dst, out_sems.at[k]) for k, (src, dst) in enumerate(
                [(gin_out, gin_final), (gout_out, gout_final), (small_out, small_final)])]
            for cp in outs:
                cp.start()
            for cp in outs:
                cp.wait()

    assert n_tiles >= 4, "the reduction's steps are spread over the first four token steps of a pass"
    grid_spec = pltpu.PrefetchScalarGridSpec(
        num_scalar_prefetch=1,
        grid=(N_CHIP, n_tiles),
        in_specs=[
            pl.BlockSpec((D_MODEL, tm), lambda j, i, o: (0, jnp.where(j == 0, i, n_tiles - 1))),
            pl.BlockSpec((tm, SHARD_IN), lambda j, i, o: (i, o[j])),
            VMEM, VMEM],
        out_specs=[ANY, ANY, ANY],
        scratch_shapes=[
            pltpu.VMEM((D_MODEL, seq), BF16),
            pltpu.VMEM((2, D_MODEL, SHARD_IN), F32),
            pltpu.VMEM((2, HALF_IN, SHARD_IN), F32),
            pltpu.VMEM((N_CHIP, HALF_OUT, D_MODEL), F32),
            pltpu.VMEM((SMALL_ROWS, GROUP), F32),
            pltpu.VMEM((N_CHIP - 1, HALF_IN, SHARD_IN), BF16),
            pltpu.VMEM((N_CHIP - 1, HALF_OUT, D_MODEL), BF16),
            pltpu.VMEM((N_CHIP - 1, HALF_IN, SHARD_IN), BF16),
            pltpu.VMEM((N_CHIP - 1, HALF_OUT, D_MODEL), BF16),
            pltpu.VMEM((N_CHIP, half_small, GROUP), F32),
            pltpu.VMEM((D_MODEL, SHARD_IN), F32),
            pltpu.VMEM((SHARD_OUT, D_MODEL), F32),
            pltpu.VMEM((SMALL_ROWS, GROUP), F32),
            pltpu.SemaphoreType.DMA((18,)),
            pltpu.SemaphoreType.DMA((18,)),
            pltpu.SemaphoreType.DMA((3,)),
        ],
    )
    return pl.pallas_call(
        body,
        name="gw_reduce",
        grid_spec=grid_spec,
        out_shape=(
            jax.ShapeDtypeStruct((D_MODEL, SHARD_IN), F32),
            jax.ShapeDtypeStruct((SHARD_OUT, D_MODEL), F32),
            jax.ShapeDtypeStruct((SMALL_ROWS, GROUP), F32),
        ),
        compiler_params=pltpu.CompilerParams(
            dimension_semantics=("arbitrary", "arbitrary"), vmem_limit_bytes=VMEM_LIMIT),
    )(order, ht, dproj, gwout, small)


def _adamw(name, w, g, m, v, block_rows):
    rows, cols = w.shape

    def body(w_ref, g_ref, m_ref, v_ref, d_ref, m_out, v_out):
        grad = g_ref[...]
        m_new = ADAM_B1 * m_ref[...] + (1.0 - ADAM_B1) * grad
        v_new = ADAM_B2 * v_ref[...] + (1.0 - ADAM_B2) * (grad * grad)
        m_hat = m_new / (1.0 - ADAM_B1 ** ADAM_STEP)
        v_hat = v_new / (1.0 - ADAM_B2 ** ADAM_STEP)
        d_ref[...] = -ADAM_LR * (m_hat / (jnp.sqrt(v_hat) + ADAM_EPS) + ADAM_WD * w_ref[...])
        m_out[...] = m_new
        v_out[...] = v_new

    blk = pl.BlockSpec((block_rows, cols), lambda i: (i, 0))
    shape = jax.ShapeDtypeStruct((rows, cols), F32)
    return pl.pallas_call(
        body,
        name=name,
        grid=(rows // block_rows,),
        in_specs=[blk] * 4,
        out_specs=[blk] * 3,
        out_shape=[shape] * 3,
        compiler_params=pltpu.CompilerParams(dimension_semantics=("arbitrary",)),
    )(w, g, m, v)


def _pack_small(norm_gain, pool_w, pool_scale, rel_bias, final_gain, loss_rows):
    parts = [
        norm_gain.reshape(8, GROUP),
        pool_w.reshape(len(POOL_WINDOWS) * GROUP, GROUP),
        jnp.pad(pool_scale.reshape(4, GROUP), ((0, 4), (0, 0))),
        jnp.pad(rel_bias.reshape(8, N_REL), ((0, 0), (0, 2 * GROUP - N_REL))).reshape(16, GROUP),
        final_gain.reshape(8, GROUP),
        loss_rows,
    ]
    return jnp.concatenate(parts, axis=0)


def _unpack_small(block):
    norm_gain = block[0:8].reshape(1, D_MODEL)
    pool_w = block[8:520].reshape(1, len(POOL_WINDOWS), GROUP, GROUP)
    pool_scale = block[520:524].reshape(1, POOL_WIDTH)
    rel_bias = block[528:544].reshape(8, 2 * GROUP)[:, :N_REL].reshape(1, 8, N_REL)
    final_gain = block[544:552].reshape(D_MODEL)
    return norm_gain, pool_w, pool_scale, rel_bias, final_gain


def kernel(x, norm_gain, w_in, pool_w, pool_scale, rel_bias, w_out, final_norm_gain, loss_target, m_norm_gain, m_w_in, m_pool_w, m_pool_scale, m_rel_bias, m_w_out, m_final_norm_gain, v_norm_gain, v_w_in, v_pool_w, v_pool_scale, v_rel_bias, v_w_out, v_final_norm_gain):
    assert x.shape[1] % TS == 0 and x.shape[2] == D_MODEL
    xs = x[0]
    tgt = loss_target[0]
    g1 = norm_gain.reshape(1, D_MODEL)
    g2 = final_norm_gain.reshape(1, D_MODEL)
    ps = pool_scale.reshape(1, POOL_WIDTH)
    pw_bf = pool_w[0].astype(BF16)

    proj, win_full, wout_full = _gather_inproj(xs, g1, w_in[0].astype(BF16), w_out[0].astype(BF16))
    bias_t = _bias_tile(rel_bias[0] * LOG2E)

    a, ya, e_all, inv_all = _attn_fwd(proj, bias_t)
    dx2, dy, gwout, gg2, loss_rows = _out_loss(xs, tgt, proj, ya, wout_full, g2, pw_bf, ps)
    dq, dk, dv, dag, bins = _attn_bwd(proj, a, dy, e_all, inv_all)
    gx, dproj, ht, gg1, gps, gpw = _inproj_bwd(xs, dx2, dy, proj, dq, dk, dv, dag, win_full, pw_bf, g1, ps)

    g_bias = bins[:, :N_REL, :2].transpose(0, 2, 1).reshape(8, N_REL)
    small = _pack_small(gg1, gpw, gps, g_bias, gg2, loss_rows)
    g_win, g_wout, g_small = _gw_reduce(ht, dproj, gwout, small)
    loss = g_small[LOSS_ROW, 0]

    zeros8 = jnp.zeros((8, GROUP), F32)
    w_small = _pack_small(norm_gain, pool_w, pool_scale, rel_bias, final_norm_gain, zeros8)
    m_small = _pack_small(m_norm_gain, m_pool_w, m_pool_scale, m_rel_bias, m_final_norm_gain, zeros8)
    v_small = _pack_small(v_norm_gain, v_pool_w, v_pool_scale, v_rel_bias, v_final_norm_gain, zeros8)

    d_win, m_win, v_win = _adamw("adamw_w_in", w_in[0], g_win, m_w_in[0], v_w_in[0], 256)
    d_wout, m_wout, v_wout = _adamw("adamw_w_out", w_out[0], g_wout, m_w_out[0], v_w_out[0], 128)
    d_small, m_new_small, v_new_small = _adamw("adamw_small", w_small, g_small, m_small, v_small, SMALL_ROWS // 2)

    def full(win_part, wout_part, block):
        ng, pw, psc, rb, fg = _unpack_small(block)
        return [ng, win_part[None], pw, psc, rb, wout_part[None], fg]

    grads = full(g_win, g_wout, g_small)
    deltas = full(d_win, d_wout, d_small)
    new_m = full(m_win, m_wout, m_new_small)
    new_v = full(v_win, v_wout, v_new_small)
    return (loss, gx[None], *grads, *deltas, *new_m, *new_v)
```

```python
import numpy as np
import jax
import jax.numpy as jnp
from jax import lax
from jax.experimental import pallas as pl
from jax.experimental.pallas import tpu as pltpu

F32 = jnp.float32
BF16 = jnp.bfloat16

D_MODEL = 1024
POOL_WIDTH = 512
ATTN_WIDTH = 512
POOL_WINDOWS = (2, 4, 8, 16)
GROUP = 128
CHUNK = 64
LEFT_CHUNKS = 8
BAND = (LEFT_CHUNKS + 1) * CHUNK
HEAD_DIM = 64
N_PAIR = 4
MAX_REL = 64
N_REL = 2 * MAX_REL + 1
EPS = 1e-6
MASK_VALUE = -1e30
SCALE = 0.125
LOG2E = 1.4426950408889634

ADAM_LR = 0.001
ADAM_B1 = 0.9
ADAM_B2 = 0.999
ADAM_EPS = 1e-08
ADAM_WD = 0.01
ADAM_STEP = 10

TS = LEFT_CHUNKS * CHUNK
SUPER = 2 * CHUNK
WINDOW = BAND + CHUNK
TA = 4 * TS
WIN_BASE = TA - LEFT_CHUNKS * CHUNK
SUPERS_PER_TILE = TA // SUPER
PAIR_LANES = 2 * SUPER
HALO = 16
N_CHIP = 4
SHARD_IN = 768
SHARD_OUT = 256
PIECE = 256
PIECES_PER_SHARD = SHARD_IN // PIECE
HALF_IN = D_MODEL // 2
HALF_OUT = SHARD_OUT // 2
SMALL_ROWS = 560
POOL_W_ROW = 8
POOL_SCALE_ROW = 520
REL_BIAS_ROW = 528
REL_BIAS_ROWS_PER_HEAD = 2
FINAL_GAIN_ROW = 544
LOSS_ROW = 552
VMEM_LIMIT = 60 * 1024 * 1024

MESH = pl.DeviceIdType.MESH
ANY = pl.BlockSpec(memory_space=pl.ANY)
VMEM = pl.BlockSpec(memory_space=pltpu.VMEM)


def _sigmoid(x):
    return 1.0 / (1.0 + jnp.exp(-x))


def _dot(a, b):
    return jnp.dot(a, b, preferred_element_type=F32)


def _dot_t(a, b):
    return lax.dot_general(a, b, (((1,), (1,)), ((), ())), preferred_element_type=F32)


def _tdot(a, b):
    return lax.dot_general(a, b, (((0,), (0,)), ((), ())), preferred_element_type=F32)


def _my_place():
    return lax.axis_index("x"), lax.axis_index("y"), lax.axis_index("c")


def _other_chips(x, y):
    places = [(1 - x, y), (x, 1 - y), (1 - x, 1 - y)]
    return [(p, 2 * p[0] + p[1]) for p in places]


def _gather_inproj(x, g1, win_sh, wout_sh):
    seq = x.shape[0]
    tm = min(seq, 4 * TS)
    n_tiles = seq // tm
    cx, cy = lax.axis_index("x"), lax.axis_index("y")
    order = jnp.stack([2 * cx + cy] + [chip for _, chip in _other_chips(cx, cy)]).astype(jnp.int32)

    def body(order_ref, x_ref, g1_ref, win_ref, wout_ref, proj_bf_ref, win_full, wout_full,
             hbuf, wbuf, send_sems, recv_sems, local_sems):
        j = pl.program_id(0)
        i = pl.program_id(1)
        x_, y_, c = _my_place()
        b = 2 * x_ + y_
        sibling = (x_, y_, 1 - c)
        others = _other_chips(x_, y_)

        def halves(chip, core):
            return (
                win_full.at[chip, pl.ds(core * HALF_IN, HALF_IN)],
                wout_full.at[chip, pl.ds(core * HALF_OUT, HALF_OUT)],
            )

        def copy(k, src, dst, to):
            return pltpu.make_async_remote_copy(
                src_ref=src, dst_ref=dst, send_sem=send_sems.at[k], recv_sem=recv_sems.at[k],
                device_id=to, device_id_type=MESH)

        own = [
            pltpu.make_async_copy(win_ref, win_full.at[b], local_sems.at[0]),
            pltpu.make_async_copy(wout_ref, wout_full.at[b], local_sems.at[1]),
        ]
        mine_src = (win_ref.at[pl.ds(c * HALF_IN, HALF_IN)], wout_ref.at[pl.ds(c * HALF_OUT, HALF_OUT)])

        def direct(n, t):
            return copy(2 * n + t, mine_src[t], halves(b, c)[t], (*others[n][0], c))

        def arrival(n, t):
            landed = halves(others[n][1], c)[t]
            return copy(2 * n + t, landed, landed, (*others[n][0], c))

        def passing(n, t):
            landed = halves(others[n][1], c)[t]
            return copy(6 + 2 * n + t, landed, landed, sibling)

        def from_sibling(n, t):
            landed = halves(others[n][1], 1 - c)[t]
            return copy(6 + 2 * n + t, landed, landed, sibling)

        @pl.when((j == 0) & (i == 0))
        def _():
            first = pltpu.make_async_copy(win_ref, wbuf.at[0], local_sems.at[2])
            first.start()
            for cp in own:
                cp.start()
            for t in range(2):
                for n in range(N_CHIP - 1):
                    direct(n, t).start()
            first.wait()

        def load(n):
            return pltpu.make_async_copy(win_full.at[others[n][1]], wbuf.at[(n + 1) % 2], local_sems.at[2])

        for n in range(N_CHIP - 1):
            @pl.when((j == n) & (i == n_tiles - 1))
            def _(n=n):
                arrival(n, 0).wait_recv()
                passing(n, 0).start()

            @pl.when((j == n + 1) & (i == 0))
            def _(n=n):
                load(n).wait()

        rows = pl.ds(pl.multiple_of(i * tm, tm), tm)

        @pl.when(j == 0)
        def _():
            xt = x_ref[...]
            r = lax.rsqrt(jnp.mean(xt * xt, axis=-1, keepdims=True) + EPS)
            hbuf[rows] = ((xt * r) * g1_ref[...]).astype(BF16)

        proj_bf_ref[...] = _dot(hbuf[rows], wbuf[j % 2]).astype(BF16)

        for n in range(N_CHIP - 1):
            @pl.when((j == n) & (i == n_tiles - 1))
            def _(n=n):
                from_sibling(n, 0).wait_recv()
                load(n).start()

        @pl.when((j == N_CHIP - 1) & (i == n_tiles - 1))
        def _():
            for n in range(N_CHIP - 1):
                arrival(n, 1).wait_recv()
                passing(n, 1).start()
            for n in range(N_CHIP - 1):
                from_sibling(n, 1).wait_recv()
            for n in range(N_CHIP - 1):
                for t in range(2):
                    direct(n, t).wait_send()
                    passing(n, t).wait_send()
            for cp in own:
                cp.wait()

    last = n_tiles - 1
    grid_spec = pltpu.PrefetchScalarGridSpec(
        num_scalar_prefetch=1,
        grid=(N_CHIP, n_tiles),
        in_specs=[
            pl.BlockSpec((tm, D_MODEL), lambda j, i, o: (jnp.where(j == 0, i, last), 0)),
            VMEM, ANY, ANY],
        out_specs=[pl.BlockSpec((tm, SHARD_IN), lambda j, i, o: (i, o[j])), ANY, ANY],
        scratch_shapes=[
            pltpu.VMEM((seq, D_MODEL), BF16), pltpu.VMEM((2, D_MODEL, SHARD_IN), BF16),
            pltpu.SemaphoreType.DMA((12,)), pltpu.SemaphoreType.DMA((12,)), pltpu.SemaphoreType.DMA((3,))],
    )
    return pl.pallas_call(
        body,
        name="gather_inproj",
        grid_spec=grid_spec,
        out_shape=(
            jax.ShapeDtypeStruct((seq, N_CHIP * SHARD_IN), BF16),
            jax.ShapeDtypeStruct((N_CHIP, D_MODEL, SHARD_IN), BF16),
            jax.ShapeDtypeStruct((N_CHIP, SHARD_OUT, D_MODEL), BF16),
        ),
        compiler_params=pltpu.CompilerParams(
            dimension_semantics=("arbitrary", "arbitrary"), vmem_limit_bytes=VMEM_LIMIT),
    )(order, x, g1, win_sh, wout_sh)


def _window_sums(ext, forward):
    n = ext.shape[0]
    sums = []
    acc = ext
    for step in (1, 2, 4, 8):
        acc = acc + pltpu.roll(acc, (n - step) if forward else step, 0)
        sums.append(acc)
    return sums


def _row_counts(tile, rows):
    t = tile * rows + lax.broadcasted_iota(jnp.int32, (rows, GROUP), 0)
    return [jnp.minimum(t + 1, w).astype(F32) for w in POOL_WINDOWS]


def _pool_diffs(pv, prev_rows, tile):
    ext = jnp.concatenate([prev_rows, pv], axis=0)
    sums = _window_sums(ext, forward=False)
    counts = _row_counts(tile, pv.shape[0])
    out = []
    for g in range(len(POOL_WINDOWS)):
        cols = slice(g * GROUP, (g + 1) * GROUP)
        out.append(sums[g][HALO:, cols] / counts[g] - pv[:, cols])
    return out


def _pool_mix(diffs, pw_ref):
    return jnp.concatenate([_dot(diffs[g], pw_ref[g]) for g in range(len(POOL_WINDOWS))], axis=1)


Q_BLOCK, K_BLOCK, V_BLOCK, AG_BLOCK = 8, 12, 16, 20


def _bias_tile(rel_bias):
    flat = jnp.concatenate(
        [jnp.broadcast_to(rel_bias[:, :1], (rel_bias.shape[0], BAND - CHUNK - 1)), rel_bias[:, :2 * MAX_REL]], axis=1)
    rows = [flat[:, CHUNK - 1 - i:CHUNK - 1 - i + BAND] for i in range(CHUNK)]
    bias = jnp.stack(rows, axis=1)
    first = jnp.pad(bias, ((0, 0), (0, 0), (0, CHUNK)), constant_values=MASK_VALUE)
    second = jnp.pad(bias, ((0, 0), (0, 0), (CHUNK, 0)), constant_values=MASK_VALUE)
    both = jnp.concatenate([first, second], axis=1)
    return both.reshape(N_PAIR, PAIR_LANES, WINDOW).transpose(0, 2, 1)


def _rel_index_tile():
    j = np.arange(WINDOW)[:, None]
    q = np.arange(PAIR_LANES)[None, :] % SUPER
    band_key = j - CHUNK * (q // CHUNK)
    idx = np.clip(band_key - LEFT_CHUNKS * CHUNK - q % CHUNK, -MAX_REL, MAX_REL) + MAX_REL
    return np.where((band_key >= 0) & (band_key < BAND), idx, -1).astype(np.int32)


def _by_head(block):
    low = lax.broadcasted_iota(jnp.int32, block.shape, 1) < HEAD_DIM
    zero = jnp.zeros_like(block)
    return jnp.concatenate([jnp.where(low, block, zero), jnp.where(low, zero, block)], axis=0)


def _own_head_rows(cross):
    head_of_row = lax.broadcasted_iota(jnp.int32, cross.shape, 0) >= HEAD_DIM
    head_of_lane = lax.broadcasted_iota(jnp.int32, cross.shape, 1) >= SUPER
    both = jnp.where(jnp.logical_xor(head_of_row, head_of_lane), 0.0, cross).T
    return both[:SUPER] + both[SUPER:]


def _with_mask_lane(q_rows):
    lane = lax.broadcasted_iota(jnp.int32, q_rows.shape, 1)
    return jnp.concatenate([q_rows, jnp.where(lane == 0, MASK_VALUE, 0.0).astype(q_rows.dtype)], axis=1)


def _band_exp(kb, q_rows, bias):
    s = _dot_t(kb, _with_mask_lane(q_rows)) + bias
    e = jnp.exp2(s - jnp.max(s, axis=0, keepdims=True))
    return e, jnp.sum(e, axis=0, keepdims=True)


def _window(sc):
    return slice(WIN_BASE + sc * SUPER, WIN_BASE + sc * SUPER + WINDOW)


def _shift_band(i, band_ref, new_ref):
    @pl.when(i == 0)
    def _():
        band_ref[:TA] = jnp.zeros((TA, GROUP), band_ref.dtype)

    @pl.when(i > 0)
    def _():
        band_ref[:TA] = band_ref[TA:]

    band_ref[TA:] = new_ref[...].astype(band_ref.dtype)


def _shift_key_band(i, band_ref, new_ref):
    @pl.when(i == 0)
    def _():
        lane = lax.broadcasted_iota(jnp.int32, (TA, 2 * GROUP), 1)
        band_ref[:TA] = jnp.where(lane == GROUP, 1.0, 0.0).astype(band_ref.dtype)
        band_ref[TA:, GROUP:] = jnp.zeros((TA, GROUP), band_ref.dtype)

    @pl.when(i > 0)
    def _():
        band_ref[:TA] = band_ref[TA:]

    band_ref[TA:, :GROUP] = new_ref[...].astype(band_ref.dtype)


def _shift_band_t(i, band_ref, new_ref):
    @pl.when(i == 0)
    def _():
        band_ref[:, :TA] = jnp.zeros((GROUP, TA), band_ref.dtype)

    @pl.when(i > 0)
    def _():
        band_ref[:, :TA] = band_ref[:, TA:]

    band_ref[:, TA:] = new_ref[...].astype(band_ref.dtype).T


def _scaled_queries(q_ref, rows, scale=SCALE):
    return _by_head((q_ref[rows] * scale).astype(BF16))


def _attn_fwd(proj_bf, bias_t):
    seq = proj_bf.shape[0]
    n_tiles = seq // TA

    def body(q_ref, k_ref, v_ref, ag_ref, bias_ref, a_ref, ya_ref, e_ref, inv_ref, kband, vband_t):
        i = pl.program_id(1)
        _shift_key_band(i, kband, k_ref)
        _shift_band_t(i, vband_t, v_ref)

        def weights(sc):
            rows = slice(sc * SUPER, (sc + 1) * SUPER)
            win = _window(sc)
            e, total = _band_exp(kband[win], _scaled_queries(q_ref, rows, SCALE * LOG2E), bias_ref[0])
            e_ref[0, sc] = e.astype(BF16)
            inv_total = 1.0 / total
            inv_ref[0, sc] = jnp.broadcast_to(inv_total, (8, PAIR_LANES))
            return inv_total

        nxt = weights(0)
        for sc in range(SUPERS_PER_TILE):
            rows = slice(sc * SUPER, (sc + 1) * SUPER)
            win = _window(sc)
            inv_total = nxt
            if sc + 1 < SUPERS_PER_TILE:
                nxt = weights(sc + 1)
            a = _own_head_rows(_dot(vband_t[:, win], e_ref[0, sc]) * inv_total)
            a_ref[rows] = a.astype(BF16)
            g = ag_ref[rows].astype(F32)
            ya_ref[rows] = (a * (g * _sigmoid(g))).astype(BF16)

    blk = pl.BlockSpec((TA, GROUP), lambda p, i: (i, p))

    def cols(first):
        return pl.BlockSpec((TA, GROUP), lambda p, i: (i, first + p))

    return pl.pallas_call(
        body,
        name="attn_fwd",
        grid=(N_PAIR, n_tiles),
        in_specs=[cols(Q_BLOCK), cols(K_BLOCK), cols(V_BLOCK), cols(AG_BLOCK),
                  pl.BlockSpec((1, WINDOW, PAIR_LANES), lambda p, i: (p, 0, 0))],
        out_specs=[blk, blk,
                   pl.BlockSpec((1, SUPERS_PER_TILE, WINDOW, PAIR_LANES), lambda p, i: (p, i, 0, 0)),
                   pl.BlockSpec((1, SUPERS_PER_TILE, 8, PAIR_LANES), lambda p, i: (p, i, 0, 0))],
        out_shape=[jax.ShapeDtypeStruct((seq, ATTN_WIDTH), BF16), jax.ShapeDtypeStruct((seq, ATTN_WIDTH), BF16),
                   jax.ShapeDtypeStruct((N_PAIR, seq // SUPER, WINDOW, PAIR_LANES), BF16),
                   jax.ShapeDtypeStruct((N_PAIR, seq // SUPER, 8, PAIR_LANES), F32)],
        scratch_shapes=[pltpu.VMEM((2 * TA, 2 * GROUP), BF16), pltpu.VMEM((GROUP, 2 * TA), BF16)],
        compiler_params=pltpu.CompilerParams(
            dimension_semantics=("arbitrary", "arbitrary"), vmem_limit_bytes=VMEM_LIMIT),
    )(proj_bf, proj_bf, proj_bf, proj_bf, bias_t)


BIN_ROWS = 136


def _bias_bin_sums(db_ref, idx_ref):
    lane = lax.broadcasted_iota(jnp.int32, (1, GROUP), 1)
    row = lax.broadcasted_iota(jnp.int32, (BIN_ROWS, GROUP), 0)
    out = jnp.zeros((BIN_ROWS, GROUP), F32)
    for r in range(N_REL - 1):
        lo = 0 if r == 0 else ((BAND - 2 * CHUNK + r) // 8) * 8
        hi = WINDOW if r == 0 else min(WINDOW, lo + SUPER + 8)
        hit = jnp.where(idx_ref[lo:hi] == r, db_ref[lo:hi], 0.0)
        col = jnp.sum(hit, axis=0, keepdims=True)
        s0 = jnp.sum(col[:, :SUPER], axis=1, keepdims=True)
        s1 = jnp.sum(col[:, SUPER:], axis=1, keepdims=True)
        val = jnp.where(lane == 0, s0, jnp.where(lane == 1, s1, 0.0))
        out = jnp.where(row == r, val, out)
    return out


def _attn_bwd(proj_bf, a, dy, e_all, inv_all):
    seq = proj_bf.shape[0]
    n_tiles = seq // TA

    def body(q_ref, k_ref, v_ref, a_ref, ag_ref, dy_ref, e_ref, inv_ref, idx_ref,
             dq_ref, dk_ref, dv_ref, dag_ref, bins_ref, vband, kband_t, dkacc, dvacc, db_ref):
        i = pl.program_id(1)

        @pl.when(i == 0)
        def _():
            dkacc[...] = jnp.zeros_like(dkacc)
            dvacc[...] = jnp.zeros_like(dvacc)
            db_ref[...] = jnp.zeros_like(db_ref)

        @pl.when(i < n_tiles)
        def _():
            _shift_band(i, vband, v_ref)
            _shift_band_t(i, kband_t, k_ref)

            def score_grads(sc):
                rows = slice(sc * SUPER, (sc + 1) * SUPER)
                win = _window(sc)
                q_rows = _scaled_queries(q_ref, rows)
                g = ag_ref[rows].astype(F32)
                sg = _sigmoid(g)
                dyc = dy_ref[rows].astype(F32)
                dag_ref[rows] = (dyc * a_ref[rows].astype(F32) * (sg * (1.0 + g * (1.0 - sg)))).astype(BF16)
                da_rows = _by_head((dyc * (g * sg)).astype(BF16))
                p = e_ref[0, sc].astype(F32) * inv_ref[0, sc, :1]
                dp = _dot_t(vband[win], da_rows)
                ds = p * (dp - jnp.sum(p * dp, axis=0, keepdims=True))
                db_ref[...] += ds
                return q_rows, da_rows, p.astype(BF16), ds.astype(BF16)

            nxt = score_grads(0)
            for sc in range(SUPERS_PER_TILE):
                rows = slice(sc * SUPER, (sc + 1) * SUPER)
                win = _window(sc)
                q_rows, da_rows, p_bf, ds_bf = nxt
                if sc + 1 < SUPERS_PER_TILE:
                    nxt = score_grads(sc + 1)
                dq_ref[rows] = (_own_head_rows(_dot(kband_t[:, win], ds_bf)) * SCALE).astype(BF16)
                dkacc[win] += _dot(ds_bf, q_rows)
                dvacc[win] += _dot(p_bf, da_rows)

        dk_ref[...] = dkacc[:TA].astype(BF16)
        dv_ref[...] = dvacc[:TA].astype(BF16)
        dkacc[:TA] = dkacc[TA:]
        dvacc[:TA] = dvacc[TA:]
        dkacc[TA:] = jnp.zeros((TA, GROUP), F32)
        dvacc[TA:] = jnp.zeros((TA, GROUP), F32)

        @pl.when(i == n_tiles)
        def _():
            bins_ref[0] = _bias_bin_sums(db_ref, idx_ref)

    last = n_tiles - 1
    cur = pl.BlockSpec((TA, GROUP), lambda p, i: (jnp.minimum(i, last), p))
    older = pl.BlockSpec((TA, GROUP), lambda p, i: (jnp.maximum(i - 1, 0), p))
    dy_blk = pl.BlockSpec((TA, GROUP), lambda p, i: (jnp.minimum(i, last), N_PAIR + p))
    per_pair = pl.BlockSpec((1, BIN_ROWS, GROUP), lambda p, i: (p, 0, 0))

    def cols(first):
        return pl.BlockSpec((TA, GROUP), lambda p, i: (jnp.minimum(i, last), first + p))

    def kept(rows):
        return pl.BlockSpec((1, SUPERS_PER_TILE, rows, PAIR_LANES), lambda p, i: (p, jnp.minimum(i, last), 0, 0))

    def out(dtype):
        return jax.ShapeDtypeStruct((seq, ATTN_WIDTH), dtype)

    return pl.pallas_call(
        body,
        name="attn_bwd",
        grid=(N_PAIR, n_tiles + 1),
        in_specs=[cols(Q_BLOCK), cols(K_BLOCK), cols(V_BLOCK), cur, cols(AG_BLOCK), dy_blk, kept(WINDOW), kept(8),
                  VMEM],
        out_specs=[cur, older, older, cur, per_pair],
        out_shape=[out(BF16), out(BF16), out(BF16), out(BF16),
                   jax.ShapeDtypeStruct((N_PAIR, BIN_ROWS, GROUP), F32)],
        scratch_shapes=[
            pltpu.VMEM((2 * TA, GROUP), BF16), pltpu.VMEM((GROUP, 2 * TA), BF16),
            pltpu.VMEM((2 * TA, GROUP), F32), pltpu.VMEM((2 * TA, GROUP), F32),
            pltpu.VMEM((WINDOW, PAIR_LANES), F32)],
        compiler_params=pltpu.CompilerParams(
            dimension_semantics=("arbitrary", "arbitrary"), vmem_limit_bytes=VMEM_LIMIT),
    )(proj_bf, proj_bf, proj_bf, a, proj_bf, dy, e_all, inv_all, jnp.asarray(_rel_index_tile()))


def _out_loss(x, tgt, proj, ya, wout_full, g2, pw_bf, ps):
    seq = x.shape[0]
    to = min(seq, 2 * TS)
    n_tiles = seq // to

    def body(x_ref, t_ref, pv_ref, pg_ref, ya_ref, w_ref, g2_ref, pw_ref, ps_ref,
             dx2_ref, dy_ref, gw_ref, gg_ref, loss_ref, sq_ref, halo_ref, wt_ref):
        i = pl.program_id(0)

        @pl.when(i == 0)
        def _():
            gw_ref[...] = jnp.zeros_like(gw_ref)
            gg_ref[...] = jnp.zeros_like(gg_ref)
            sq_ref[...] = jnp.zeros_like(sq_ref)
            halo_ref[...] = jnp.zeros_like(halo_ref)
            for b in range(N_CHIP):
                wt_ref[b] = w_ref[b].T

        pv = pv_ref[...].astype(F32)
        pg = pg_ref[...].astype(F32)
        diffs = [d.astype(BF16) for d in _pool_diffs(pv, halo_ref[...], i)]
        halo_ref[...] = pv[to - HALO:, :]
        yp = ((_pool_mix(diffs, pw_ref) * ps_ref[...]) * (pg * _sigmoid(pg))).astype(BF16)
        g2v = g2_ref[...]

        def parts(rows):
            return [yp[rows, :SHARD_OUT], yp[rows, SHARD_OUT:], ya_ref[rows, :SHARD_OUT], ya_ref[rows, SHARD_OUT:]]

        def project(rows, ys):
            x2 = x_ref[rows]
            for b in range(N_CHIP):
                x2 = x2 + _dot(ys[b], w_ref[b])
            return x2

        def norm_loss(rows, x2):
            r = lax.rsqrt(jnp.mean(x2 * x2, axis=-1, keepdims=True) + EPS)
            xh = x2 * r
            diff = xh * g2v - t_ref[rows]
            sq_ref[...] += jnp.sum(diff * diff, axis=0, keepdims=True)
            dfin = diff * (1.0 / D_MODEL)
            gg_ref[...] += jnp.sum(dfin * xh, axis=0, keepdims=True)
            dxh = dfin * g2v
            dx2 = r * (dxh - xh * jnp.mean(dxh * xh, axis=-1, keepdims=True))
            dx2_ref[rows] = dx2
            return dx2.astype(BF16)

        def back(rows, ys, dx2_bf):
            for b in range(N_CHIP):
                gw_ref[b] += _tdot(ys[b], dx2_bf)
                dy_ref[rows, b * SHARD_OUT:(b + 1) * SHARD_OUT] = _dot(dx2_bf, wt_ref[b]).astype(BF16)

        n_parts = 2
        part = to // n_parts
        spans = [slice(r * part, (r + 1) * part) for r in range(n_parts)]
        ys = [parts(rows) for rows in spans]
        x2_next = project(spans[0], ys[0])
        for r in range(n_parts):
            x2 = x2_next
            if r + 1 < n_parts:
                x2_next = project(spans[r + 1], ys[r + 1])
            back(spans[r], ys[r], norm_loss(spans[r], x2))

        @pl.when(i == n_tiles - 1)
        def _():
            total = jnp.sum(sq_ref[...], axis=1, keepdims=True) * (0.5 / D_MODEL)
            loss_ref[...] = jnp.broadcast_to(total, loss_ref.shape)

    def rows(width, col=0):
        return pl.BlockSpec((to, width), lambda i: (i, col))

    return pl.pallas_call(
        body,
        name="out_loss",
        grid=(n_tiles,),
        in_specs=[rows(D_MODEL), rows(D_MODEL), rows(POOL_WIDTH, 0), rows(POOL_WIDTH, 1), rows(ATTN_WIDTH),
                  VMEM, VMEM, VMEM, VMEM],
        out_specs=[rows(D_MODEL), rows(D_MODEL), VMEM, VMEM, VMEM],
        out_shape=[
            jax.ShapeDtypeStruct((seq, D_MODEL), F32), jax.ShapeDtypeStruct((seq, D_MODEL), BF16),
            jax.ShapeDtypeStruct((N_CHIP, SHARD_OUT, D_MODEL), F32), jax.ShapeDtypeStruct((1, D_MODEL), F32),
            jax.ShapeDtypeStruct((8, GROUP), F32)],
        scratch_shapes=[pltpu.VMEM((1, D_MODEL), F32), pltpu.VMEM((HALO, POOL_WIDTH), F32),
                        pltpu.VMEM((N_CHIP, D_MODEL, SHARD_OUT), BF16)],
        compiler_params=pltpu.CompilerParams(dimension_semantics=("arbitrary",), vmem_limit_bytes=VMEM_LIMIT),
    )(x, tgt, proj, proj, ya, wout_full, g2, pw_bf, ps)


def _inproj_bwd(x, dx2, dy, proj, dq, dk, dv, dag, win_full, pw_bf, g1, ps):
    seq = x.shape[0]
    n_tiles = seq // TS

    def body(x_ref, dx2_ref, dyp_ref, pv_ref, pvprev_ref, pg_ref, dq_ref, dk_ref, dv_ref, dag_ref,
             w_ref, pw_ref, g1_ref, ps_ref, gx_ref, dproj_ref, ht_ref, gg_ref, gps_ref, gpw_ref, halo_ref):
        i = pl.program_id(0)
        tile = n_tiles - 1 - i

        @pl.when(i == 0)
        def _():
            gg_ref[...] = jnp.zeros_like(gg_ref)
            gps_ref[...] = jnp.zeros_like(gps_ref)
            gpw_ref[...] = jnp.zeros_like(gpw_ref)
            halo_ref[...] = jnp.zeros_like(halo_ref)

        pv_t = pv_ref[...].astype(F32)
        pg_t = pg_ref[...].astype(F32)
        prev_rows = jnp.where(tile > 0, pvprev_ref[...].astype(F32), 0.0)
        diffs = [d.astype(BF16) for d in _pool_diffs(pv_t, prev_rows, tile)]
        mixed = _pool_mix(diffs, pw_ref)
        sg = _sigmoid(pg_t)
        silu = pg_t * sg
        dyp = dyp_ref[...].astype(F32)
        psv = ps_ref[...]
        gps_ref[...] += jnp.sum(dyp * mixed * silu, axis=0, keepdims=True)
        dmixed = (dyp * psv * silu).astype(BF16)
        dpg = dyp * (mixed * psv) * (sg * (1.0 + pg_t * (1.0 - sg)))
        counts = _row_counts(tile, TS)
        dds = []
        for g in range(len(POOL_WINDOWS)):
            dm_g = dmixed[:, g * GROUP:(g + 1) * GROUP]
            gpw_ref[g] += _tdot(diffs[g], dm_g)
            dds.append(_dot_t(dm_g, pw_ref[g]))
        dd = jnp.concatenate(dds, axis=1)
        spread = jnp.concatenate([dds[g] / counts[g] for g in range(len(POOL_WINDOWS))], axis=1)
        sums = _window_sums(jnp.concatenate([spread, halo_ref[...]], axis=0), forward=True)
        halo_ref[...] = spread[:HALO]
        dpv = jnp.concatenate(
            [sums[g][:TS, g * GROUP:(g + 1) * GROUP] for g in range(len(POOL_WINDOWS))], axis=1) - dd

        xt = x_ref[...]
        r = lax.rsqrt(jnp.mean(xt * xt, axis=-1, keepdims=True) + EPS)
        xh = xt * r
        g1v = g1_ref[...]
        ht_ref[...] = (xh * g1v).astype(BF16).T
        dproj = jnp.concatenate(
            [dpv.astype(BF16), dpg.astype(BF16), dq_ref[...], dk_ref[...], dv_ref[...], dag_ref[...]],
            axis=1)
        dproj_ref[...] = dproj
        dh = _dot_t(dproj[:, :SHARD_IN], w_ref[0])
        for chip in range(1, N_CHIP):
            dh = dh + _dot_t(dproj[:, chip * SHARD_IN:(chip + 1) * SHARD_IN], w_ref[chip])

        gg_ref[...] += jnp.sum(dh * xh, axis=0, keepdims=True)
        dxh = dh * g1v
        gx_ref[...] = dx2_ref[...] + r * (dxh - xh * jnp.mean(dxh * xh, axis=-1, keepdims=True))

    def rows(width, col=0):
        return pl.BlockSpec((TS, width), lambda i: (n_tiles - 1 - i, col))

    prev = pl.BlockSpec((HALO, POOL_WIDTH), lambda i: (jnp.maximum((n_tiles - 1 - i) * (TS // HALO) - 1, 0), 0))
    return pl.pallas_call(
        body,
        name="inproj_bwd",
        grid=(n_tiles,),
        in_specs=[rows(D_MODEL), rows(D_MODEL), rows(POOL_WIDTH), rows(POOL_WIDTH), prev, rows(POOL_WIDTH, 1),
                  rows(ATTN_WIDTH), rows(ATTN_WIDTH), rows(ATTN_WIDTH), rows(ATTN_WIDTH), VMEM, VMEM, VMEM, VMEM],
        out_specs=[rows(D_MODEL), rows(N_CHIP * SHARD_IN),
                   pl.BlockSpec((D_MODEL, TS), lambda i: (0, n_tiles - 1 - i)), VMEM, VMEM, VMEM],
        out_shape=[
            jax.ShapeDtypeStruct((seq, D_MODEL), F32),
            jax.ShapeDtypeStruct((seq, N_CHIP * SHARD_IN), BF16),
            jax.ShapeDtypeStruct((D_MODEL, seq), BF16),
            jax.ShapeDtypeStruct((1, D_MODEL), F32),
            jax.ShapeDtypeStruct((1, POOL_WIDTH), F32),
            jax.ShapeDtypeStruct((len(POOL_WINDOWS), GROUP, GROUP), F32)],
        scratch_shapes=[pltpu.VMEM((HALO, POOL_WIDTH), F32)],
        compiler_params=pltpu.CompilerParams(dimension_semantics=("arbitrary",), vmem_limit_bytes=VMEM_LIMIT),
    )(x, dx2, dy, proj, proj, proj, dq, dk, dv, dag, win_full, pw_bf, g1, ps)


PASS_PEER = (2, 0, 1)


def _gw_reduce(ht, dproj, gwout, small):
    seq = ht.shape[1]
    tm = min(2 * TS, seq // 4)
    n_tiles = seq // tm
    half_small = SMALL_ROWS // 2
    cx, cy = lax.axis_index("x"), lax.axis_index("y")
    outer = _other_chips(cx, cy)
    order = jnp.stack([outer[n][1] for n in PASS_PEER] + [2 * cx + cy]).astype(jnp.int32)

    def body(order_ref, ht_ref, dp_ref, gwout_ref, small_ref, gin_final, gout_final, small_final,
             hbuf, acc, pair_in, pair_out, pair_small, tx_in, tx_out, rx_in, rx_out, rx_small,
             gin_out, gout_out, small_out, send_sems, recv_sems, out_sems):
        j = pl.program_id(0)
        i = pl.program_id(1)
        x, y, c = _my_place()
        b = 2 * x + y
        sibling = (x, y, 1 - c)
        others = _other_chips(x, y)
        mine_in = pl.ds(pl.multiple_of(c * HALF_IN, HALF_IN), HALF_IN)
        mine_out = pl.ds(pl.multiple_of(c * HALF_OUT, HALF_OUT), HALF_OUT)
        mine_small = pl.ds(pl.multiple_of(c * half_small, 8), half_small)
        theirs_in = pl.ds(pl.multiple_of((1 - c) * HALF_IN, HALF_IN), HALF_IN)
        theirs_out = pl.ds(pl.multiple_of((1 - c) * HALF_OUT, HALF_OUT), HALF_OUT)
        theirs_small = pl.ds(pl.multiple_of((1 - c) * half_small, 8), half_small)

        def copy(k, src, dst, to):
            return pltpu.make_async_remote_copy(
                src_ref=src, dst_ref=dst, send_sem=send_sems.at[k], recv_sem=recv_sems.at[k],
                device_id=to, device_id_type=MESH)

        swap_out = copy(0, gwout_ref.at[:, theirs_out], pair_out, sibling)
        swap_small = copy(1, small_ref, pair_small, sibling)

        def swap_in(p):
            return copy(2 + p, acc.at[p % 2, theirs_in], pair_in.at[p % 2], sibling)

        def to_chip(n, t):
            to = (*others[n][0], c)
            if t == 0:
                return copy(6 + 3 * n, tx_in.at[n], rx_in.at[n], to)
            if t == 1:
                return copy(7 + 3 * n, tx_out.at[n], rx_out.at[n], to)
            return copy(8 + 3 * n, pair_small.at[mine_small], rx_small.at[b], to)

        share_in = copy(15, gin_out.at[mine_in], gin_out.at[mine_in], sibling)
        share_out = copy(16, gout_out.at[mine_out], gout_out.at[mine_out], sibling)
        share_small = copy(17, small_out.at[mine_small], small_out.at[mine_small], sibling)

        def at(jj, ii):
            return (j == jj) & (i == ii)

        par = j % 2

        @pl.when(i == 0)
        def _():
            acc[par] = jnp.zeros((D_MODEL, SHARD_IN), F32)

        cols = pl.ds(pl.multiple_of(i * tm, tm), tm)

        @pl.when(j == 0)
        def _():
            hbuf[:, cols] = ht_ref[...]

        acc[par] += _dot(hbuf[:, cols], dp_ref[...])

        @pl.when(at(0, 0))
        def _():
            swap_out.start()
            swap_small.start()

        @pl.when(at(0, 2))
        def _():
            swap_out.wait_recv()
            swap_small.wait_recv()
            for chip in range(N_CHIP):
                pair_out[chip] = gwout_ref[chip, mine_out] + pair_out[chip]
            pair_small[...] = small_ref[...] + pair_small[...]
            rx_small[b] = pair_small[mine_small]
            for n in range(N_CHIP - 1):
                tx_out[n] = pair_out[others[n][1]].astype(BF16)
                to_chip(n, 1).start()
                to_chip(n, 2).start()

        @pl.when(at(1, 3))
        def _():
            total_out = pair_out[b]
            for n in range(N_CHIP - 1):
                to_chip(n, 1).wait_recv()
                copy(8 + 3 * n, pair_small.at[mine_small], rx_small.at[others[n][1]], (*others[n][0], c)).wait_recv()
                total_out = total_out + rx_out[n].astype(F32)
            gout_out[mine_out] = total_out
            small_out[mine_small] = ((rx_small[0] + rx_small[1]) + rx_small[2]) + rx_small[3]
            share_out.start()
            share_small.start()

        for p in range(N_CHIP - 1):
            n = PASS_PEER[p]

            @pl.when(at(p + 1, 0))
            def _(p=p):
                swap_in(p).start()

            @pl.when(at(p + 1, 2))
            def _(p=p, n=n):
                swap_in(p).wait_recv()
                swap_in(p).wait_send()
                tx_in[n] = (acc[p % 2, mine_in] + pair_in[p % 2]).astype(BF16)
                to_chip(n, 0).start()

        @pl.when(at(N_CHIP - 1, n_tiles - 1))
        def _():
            last = N_CHIP - 1
            swap_in(last).start()
            swap_in(last).wait_recv()
            total_in = acc[last % 2, mine_in] + pair_in[last % 2]
            for n in range(N_CHIP - 1):
                to_chip(n, 0).wait_recv()
                total_in = total_in + rx_in[n].astype(F32)
            gin_out[mine_in] = total_in
            share_in.start()
            copy(15, gin_out.at[theirs_in], gin_out.at[theirs_in], sibling).wait_recv()
            copy(16, gout_out.at[theirs_out], gout_out.at[theirs_out], sibling).wait_recv()
            copy(17, small_out.at[theirs_small], small_out.at[theirs_small], sibling).wait_recv()
            swap_out.wait_send()
            swap_small.wait_send()
            swap_in(last).wait_send()
            for n in range(N_CHIP - 1):
                for t in range(3):
                    to_chip(n, t).wait_send()
            share_in.wait_send()
            share_out.wait_send()
            share_small.wait_send()
            outs = [pltpu.make_async_copy(src, dst, out_sems.at[k]) for k, (src, dst) in enumerate(
                [(gin_out, gin_final), (gout_out, gout_final), (small_out, small_final)])]
            for cp in outs:
                cp.start()
            for cp in outs:
                cp.wait()

    assert n_tiles >= 4, "the reduction's steps are spread over the first four token steps of a pass"
    grid_spec = pltpu.PrefetchScalarGridSpec(
        num_scalar_prefetch=1,
        grid=(N_CHIP, n_tiles),
        in_specs=[
            pl.BlockSpec((D_MODEL, tm), lambda j, i, o: (0, jnp.where(j == 0, i, n_tiles - 1))),
            pl.BlockSpec((tm, SHARD_IN), lambda j, i, o: (i, o[j])),
            VMEM, VMEM],
        out_specs=[ANY, ANY, ANY],
        scratch_shapes=[
            pltpu.VMEM((D_MODEL, seq), BF16),
            pltpu.VMEM((2, D_MODEL, SHARD_IN), F32),
            pltpu.VMEM((2, HALF_IN, SHARD_IN), F32),
            pltpu.VMEM((N_CHIP, HALF_OUT, D_MODEL), F32),
            pltpu.VMEM((SMALL_ROWS, GROUP), F32),
            pltpu.VMEM((N_CHIP - 1, HALF_IN, SHARD_IN), BF16),
            pltpu.VMEM((N_CHIP - 1, HALF_OUT, D_MODEL), BF16),
            pltpu.VMEM((N_CHIP - 1, HALF_IN, SHARD_IN), BF16),
            pltpu.VMEM((N_CHIP - 1, HALF_OUT, D_MODEL), BF16),
            pltpu.VMEM((N_CHIP, half_small, GROUP), F32),
            pltpu.VMEM((D_MODEL, SHARD_IN), F32),
            pltpu.VMEM((SHARD_OUT, D_MODEL), F32),
            pltpu.VMEM((SMALL_ROWS, GROUP), F32),
            pltpu.SemaphoreType.DMA((18,)),
            pltpu.SemaphoreType.DMA((18,)),
            pltpu.SemaphoreType.DMA((3,)),
        ],
    )
    return pl.pallas_call(
        body,
        name="gw_reduce",
        grid_spec=grid_spec,
        out_shape=(
            jax.ShapeDtypeStruct((D_MODEL, SHARD_IN), F32),
            jax.ShapeDtypeStruct((SHARD_OUT, D_MODEL), F32),
            jax.ShapeDtypeStruct((SMALL_ROWS, GROUP), F32),
        ),
        compiler_params=pltpu.CompilerParams(
            dimension_semantics=("arbitrary", "arbitrary"), vmem_limit_bytes=VMEM_LIMIT),
    )(order, ht, dproj, gwout, small)


def _adam_update(w, grad, m, v):
    m_new = ADAM_B1 * m + (1.0 - ADAM_B1) * grad
    v_new = ADAM_B2 * v + (1.0 - ADAM_B2) * (grad * grad)
    m_hat = m_new / (1.0 - ADAM_B1 ** ADAM_STEP)
    v_hat = v_new / (1.0 - ADAM_B2 ** ADAM_STEP)
    delta = -ADAM_LR * (m_hat / (jnp.sqrt(v_hat) + ADAM_EPS) + ADAM_WD * w)
    return delta, m_new, v_new


def _adamw(name, w, g, m, v, block_rows):
    rows, cols = w.shape

    def body(w_ref, g_ref, m_ref, v_ref, d_ref, m_out, v_out):
        d_ref[...], m_out[...], v_out[...] = _adam_update(w_ref[...], g_ref[...], m_ref[...], v_ref[...])

    blk = pl.BlockSpec((block_rows, cols), lambda i: (i, 0))
    shape = jax.ShapeDtypeStruct((rows, cols), F32)
    return pl.pallas_call(
        body,
        name=name,
        grid=(rows // block_rows,),
        in_specs=[blk] * 4,
        out_specs=[blk] * 3,
        out_shape=[shape] * 3,
        compiler_params=pltpu.CompilerParams(dimension_semantics=("arbitrary",)),
    )(w, g, m, v)


def _adamw_small(g_small, params, m_state, v_state):
    n_param = len(params)
    n_head = params[3].shape[0]

    def body(g_ref, *refs):
        w_refs, m_refs, v_refs = (refs[k * n_param:(k + 1) * n_param] for k in range(3))
        g_outs, d_outs, m_outs, v_outs = (refs[k * n_param:(k + 1) * n_param] for k in range(3, 7))

        def update(i, packed_rows, rows, cols, packed_cols=slice(None)):
            grad = g_ref[packed_rows, packed_cols]
            g_outs[i][rows, cols] = grad
            d_outs[i][rows, cols], m_outs[i][rows, cols], v_outs[i][rows, cols] = _adam_update(
                w_refs[i][rows, cols], grad, m_refs[i][rows, cols], v_refs[i][rows, cols])

        def lane_group(r):
            return slice(r * GROUP, (r + 1) * GROUP)

        everything = slice(None)
        for r in range(D_MODEL // GROUP):
            update(0, slice(r, r + 1), everything, lane_group(r))
            update(4, slice(FINAL_GAIN_ROW + r, FINAL_GAIN_ROW + r + 1), everything, lane_group(r))
        update(1, slice(POOL_W_ROW, POOL_W_ROW + len(POOL_WINDOWS) * GROUP), everything, everything)
        for r in range(POOL_WIDTH // GROUP):
            update(2, slice(POOL_SCALE_ROW + r, POOL_SCALE_ROW + r + 1), everything, lane_group(r))
        for h in range(n_head):
            first = REL_BIAS_ROW + REL_BIAS_ROWS_PER_HEAD * h
            update(3, slice(first, first + 1), slice(h, h + 1), slice(0, GROUP))
            update(3, slice(first + 1, first + 2), slice(h, h + 1), slice(GROUP, N_REL), slice(0, N_REL - GROUP))

    shapes = [jax.ShapeDtypeStruct(p.shape, F32) for p in params]
    outs = pl.pallas_call(
        body,
        name="adamw_small",
        out_shape=shapes * 4,
    )(g_small, *params, *m_state, *v_state)
    return [outs[k * n_param:(k + 1) * n_param] for k in range(4)]


def _pack_small(norm_gain, pool_w, pool_scale, rel_bias, final_gain, loss_rows):
    parts = [
        norm_gain.reshape(8, GROUP),
        pool_w.reshape(len(POOL_WINDOWS) * GROUP, GROUP),
        jnp.pad(pool_scale.reshape(4, GROUP), ((0, 4), (0, 0))),
        jnp.pad(rel_bias.reshape(8, N_REL), ((0, 0), (0, 2 * GROUP - N_REL))).reshape(16, GROUP),
        final_gain.reshape(8, GROUP),
        loss_rows,
    ]
    return jnp.concatenate(parts, axis=0)


def kernel(x, norm_gain, w_in, pool_w, pool_scale, rel_bias, w_out, final_norm_gain, loss_target, m_norm_gain, m_w_in, m_pool_w, m_pool_scale, m_rel_bias, m_w_out, m_final_norm_gain, v_norm_gain, v_w_in, v_pool_w, v_pool_scale, v_rel_bias, v_w_out, v_final_norm_gain):
    assert x.shape[1] % TS == 0 and x.shape[2] == D_MODEL
    xs = x[0]
    tgt = loss_target[0]
    g1 = norm_gain.reshape(1, D_MODEL)
    g2 = final_norm_gain.reshape(1, D_MODEL)
    ps = pool_scale.reshape(1, POOL_WIDTH)
    pw_bf = pool_w[0].astype(BF16)

    proj, win_full, wout_full = _gather_inproj(xs, g1, w_in[0].astype(BF16), w_out[0].astype(BF16))
    bias_t = _bias_tile(rel_bias[0] * LOG2E)

    a, ya, e_all, inv_all = _attn_fwd(proj, bias_t)
    dx2, dy, gwout, gg2, loss_rows = _out_loss(xs, tgt, proj, ya, wout_full, g2, pw_bf, ps)
    dq, dk, dv, dag, bins = _attn_bwd(proj, a, dy, e_all, inv_all)
    gx, dproj, ht, gg1, gps, gpw = _inproj_bwd(xs, dx2, dy, proj, dq, dk, dv, dag, win_full, pw_bf, g1, ps)

    g_bias = bins[:, :N_REL, :2].transpose(0, 2, 1).reshape(8, N_REL)
    small = _pack_small(gg1, gpw, gps, g_bias, gg2, loss_rows)
    g_win, g_wout, g_small = _gw_reduce(ht, dproj, gwout, small)
    loss = g_small[LOSS_ROW, 0]

    def small_views(norm, pool, scale, bias, final):
        return [norm.reshape(1, D_MODEL), pool.reshape(len(POOL_WINDOWS) * GROUP, GROUP), scale.reshape(1, POOL_WIDTH),
                bias.reshape(-1, N_REL), final.reshape(1, D_MODEL)]

    d_win, m_win, v_win = _adamw("adamw_w_in", w_in[0], g_win, m_w_in[0], v_w_in[0], 256)
    d_wout, m_wout, v_wout = _adamw("adamw_w_out", w_out[0], g_wout, m_w_out[0], v_w_out[0], 128)
    small_results = _adamw_small(
        g_small,
        small_views(norm_gain, pool_w, pool_scale, rel_bias, final_norm_gain),
        small_views(m_norm_gain, m_pool_w, m_pool_scale, m_rel_bias, m_final_norm_gain),
        small_views(v_norm_gain, v_pool_w, v_pool_scale, v_rel_bias, v_final_norm_gain))

    def full(win_part, wout_part, small_parts):
        norm, pool, scale, bias, final = small_parts
        return [norm.reshape(norm_gain.shape), win_part[None], pool.reshape(pool_w.shape), scale.reshape(pool_scale.shape),
                bias.reshape(rel_bias.shape), wout_part[None], final.reshape(final_norm_gain.shape)]

    grads = full(g_win, g_wout, small_results[0])
    deltas = full(d_win, d_wout, small_results[1])
    new_m = full(m_win, m_wout, small_results[2])
    new_v = full(v_win, v_wout, small_results[3])
    return (loss, gx[None], *grads, *deltas, *new_m, *new_v)
```

```python
import numpy as np
import jax
import jax.numpy as jnp
from jax import lax
from jax.experimental import pallas as pl
from jax.experimental.pallas import tpu as pltpu

F32 = jnp.float32
BF16 = jnp.bfloat16

D_MODEL = 1024
POOL_WIDTH = 512
ATTN_WIDTH = 512
POOL_WINDOWS = (2, 4, 8, 16)
GROUP = 128
CHUNK = 64
LEFT_CHUNKS = 8
BAND = (LEFT_CHUNKS + 1) * CHUNK
HEAD_DIM = 64
N_PAIR = 4
MAX_REL = 64
N_REL = 2 * MAX_REL + 1
EPS = 1e-6
MASK_VALUE = -1e30
SCALE = 0.125
LOG2E = 1.4426950408889634

ADAM_LR = 0.001
ADAM_B1 = 0.9
ADAM_B2 = 0.999
ADAM_EPS = 1e-08
ADAM_WD = 0.01
ADAM_STEP = 10

TS = LEFT_CHUNKS * CHUNK
SUPER = 2 * CHUNK
WINDOW = BAND + CHUNK
TA = 4 * TS
WIN_BASE = TA - LEFT_CHUNKS * CHUNK
SUPERS_PER_TILE = TA // SUPER
PAIR_LANES = 2 * SUPER
HALO = 16
N_CHIP = 4
SHARD_IN = 768
SHARD_OUT = 256
PIECE = 256
PIECES_PER_SHARD = SHARD_IN // PIECE
HALF_IN = D_MODEL // 2
HALF_OUT = SHARD_OUT // 2
SMALL_ROWS = 560
POOL_W_ROW = 8
POOL_SCALE_ROW = 520
REL_BIAS_ROW = 528
REL_BIAS_ROWS_PER_HEAD = 2
FINAL_GAIN_ROW = 544
LOSS_ROW = 552
VMEM_LIMIT = 60 * 1024 * 1024

MESH = pl.DeviceIdType.MESH
ANY = pl.BlockSpec(memory_space=pl.ANY)
VMEM = pl.BlockSpec(memory_space=pltpu.VMEM)


def _sigmoid(x):
    return 1.0 / (1.0 + jnp.exp(-x))


def _dot(a, b):
    return jnp.dot(a, b, preferred_element_type=F32)


def _dot_t(a, b):
    return lax.dot_general(a, b, (((1,), (1,)), ((), ())), preferred_element_type=F32)


def _tdot(a, b):
    return lax.dot_general(a, b, (((0,), (0,)), ((), ())), preferred_element_type=F32)


def _my_place():
    return lax.axis_index("x"), lax.axis_index("y"), lax.axis_index("c")


def _other_chips(x, y):
    places = [(1 - x, y), (x, 1 - y), (1 - x, 1 - y)]
    return [(p, 2 * p[0] + p[1]) for p in places]


def _gather_inproj(x, g1, win_sh, wout_sh):
    seq = x.shape[0]
    tm = min(seq, 4 * TS)
    n_tiles = seq // tm
    cx, cy = lax.axis_index("x"), lax.axis_index("y")
    order = jnp.stack([2 * cx + cy] + [chip for _, chip in _other_chips(cx, cy)]).astype(jnp.int32)

    def body(order_ref, x_ref, g1_ref, win_ref, wout_ref, proj_bf_ref, win_full, wout_full,
             hbuf, wbuf, send_sems, recv_sems, local_sems):
        j = pl.program_id(0)
        i = pl.program_id(1)
        x_, y_, c = _my_place()
        b = 2 * x_ + y_
        sibling = (x_, y_, 1 - c)
        others = _other_chips(x_, y_)

        def halves(chip, core):
            return (
                win_full.at[chip, pl.ds(core * HALF_IN, HALF_IN)],
                wout_full.at[chip, pl.ds(core * HALF_OUT, HALF_OUT)],
            )

        def copy(k, src, dst, to):
            return pltpu.make_async_remote_copy(
                src_ref=src, dst_ref=dst, send_sem=send_sems.at[k], recv_sem=recv_sems.at[k],
                device_id=to, device_id_type=MESH)

        own = [
            pltpu.make_async_copy(win_ref, win_full.at[b], local_sems.at[0]),
            pltpu.make_async_copy(wout_ref, wout_full.at[b], local_sems.at[1]),
        ]
        mine_src = (win_ref.at[pl.ds(c * HALF_IN, HALF_IN)], wout_ref.at[pl.ds(c * HALF_OUT, HALF_OUT)])

        def direct(n, t):
            return copy(2 * n + t, mine_src[t], halves(b, c)[t], (*others[n][0], c))

        def arrival(n, t):
            landed = halves(others[n][1], c)[t]
            return copy(2 * n + t, landed, landed, (*others[n][0], c))

        def passing(n, t):
            landed = halves(others[n][1], c)[t]
            return copy(6 + 2 * n + t, landed, landed, sibling)

        def from_sibling(n, t):
            landed = halves(others[n][1], 1 - c)[t]
            return copy(6 + 2 * n + t, landed, landed, sibling)

        @pl.when((j == 0) & (i == 0))
        def _():
            first = pltpu.make_async_copy(win_ref, wbuf.at[0], local_sems.at[2])
            first.start()
            for cp in own:
                cp.start()
            for t in range(2):
                for n in range(N_CHIP - 1):
                    direct(n, t).start()
            first.wait()

        def load(n):
            return pltpu.make_async_copy(win_full.at[others[n][1]], wbuf.at[(n + 1) % 2], local_sems.at[2])

        for n in range(N_CHIP - 1):
            @pl.when((j == n) & (i == n_tiles - 1))
            def _(n=n):
                arrival(n, 0).wait_recv()
                passing(n, 0).start()

            @pl.when((j == n + 1) & (i == 0))
            def _(n=n):
                load(n).wait()

        rows = pl.ds(pl.multiple_of(i * tm, tm), tm)

        @pl.when(j == 0)
        def _():
            xt = x_ref[...]
            r = lax.rsqrt(jnp.mean(xt * xt, axis=-1, keepdims=True) + EPS)
            hbuf[rows] = ((xt * r) * g1_ref[...]).astype(BF16)

        proj_bf_ref[...] = _dot(hbuf[rows], wbuf[j % 2]).astype(BF16)

        for n in range(N_CHIP - 1):
            @pl.when((j == n) & (i == n_tiles - 1))
            def _(n=n):
                from_sibling(n, 0).wait_recv()
                load(n).start()

        @pl.when((j == N_CHIP - 1) & (i == n_tiles - 1))
        def _():
            for n in range(N_CHIP - 1):
                arrival(n, 1).wait_recv()
                passing(n, 1).start()
            for n in range(N_CHIP - 1):
                from_sibling(n, 1).wait_recv()
            for n in range(N_CHIP - 1):
                for t in range(2):
                    direct(n, t).wait_send()
                    passing(n, t).wait_send()
            for cp in own:
                cp.wait()

    last = n_tiles - 1
    grid_spec = pltpu.PrefetchScalarGridSpec(
        num_scalar_prefetch=1,
        grid=(N_CHIP, n_tiles),
        in_specs=[
            pl.BlockSpec((tm, D_MODEL), lambda j, i, o: (jnp.where(j == 0, i, last), 0)),
            VMEM, ANY, ANY],
        out_specs=[pl.BlockSpec((tm, SHARD_IN), lambda j, i, o: (i, o[j])), ANY, ANY],
        scratch_shapes=[
            pltpu.VMEM((seq, D_MODEL), BF16), pltpu.VMEM((2, D_MODEL, SHARD_IN), BF16),
            pltpu.SemaphoreType.DMA((12,)), pltpu.SemaphoreType.DMA((12,)), pltpu.SemaphoreType.DMA((3,))],
    )
    return pl.pallas_call(
        body,
        name="gather_inproj",
        grid_spec=grid_spec,
        out_shape=(
            jax.ShapeDtypeStruct((seq, N_CHIP * SHARD_IN), BF16),
            jax.ShapeDtypeStruct((N_CHIP, D_MODEL, SHARD_IN), BF16),
            jax.ShapeDtypeStruct((N_CHIP, SHARD_OUT, D_MODEL), BF16),
        ),
        compiler_params=pltpu.CompilerParams(
            dimension_semantics=("arbitrary", "arbitrary"), vmem_limit_bytes=VMEM_LIMIT),
    )(order, x, g1, win_sh, wout_sh)


def _window_sums(ext, forward):
    n = ext.shape[0]
    sums = []
    acc = ext
    for step in (1, 2, 4, 8):
        acc = acc + pltpu.roll(acc, (n - step) if forward else step, 0)
        sums.append(acc)
    return sums


def _row_counts(tile, rows):
    t = tile * rows + lax.broadcasted_iota(jnp.int32, (rows, GROUP), 0)
    return [jnp.minimum(t + 1, w).astype(F32) for w in POOL_WINDOWS]


def _pool_diffs(pv, prev_rows, tile):
    ext = jnp.concatenate([prev_rows, pv], axis=0)
    sums = _window_sums(ext, forward=False)
    counts = _row_counts(tile, pv.shape[0])
    out = []
    for g in range(len(POOL_WINDOWS)):
        cols = slice(g * GROUP, (g + 1) * GROUP)
        out.append(sums[g][HALO:, cols] / counts[g] - pv[:, cols])
    return out


def _pool_mix(diffs, pw_ref):
    return jnp.concatenate([_dot(diffs[g], pw_ref[g]) for g in range(len(POOL_WINDOWS))], axis=1)


Q_BLOCK, K_BLOCK, V_BLOCK, AG_BLOCK = 8, 12, 16, 20


BIAS_TABLE = WINDOW + GROUP


def _bias_table(rel_bias):
    far = jnp.broadcast_to(rel_bias[:, :1], (rel_bias.shape[0], BAND - CHUNK - 1))
    by_distance = jnp.concatenate([far, rel_bias[:, :2 * MAX_REL]], axis=1)
    table = jnp.pad(by_distance, ((0, 0), (CHUNK, BIAS_TABLE - CHUNK - by_distance.shape[1])))
    return table.reshape(N_PAIR, 2, BIAS_TABLE)


def _bias_tile(table_ref):
    query = lax.broadcasted_iota(jnp.int32, (CHUNK, GROUP), 0)
    lane = lax.broadcasted_iota(jnp.int32, (CHUNK, GROUP), 1)
    key = lax.broadcasted_iota(jnp.int32, (CHUNK, WINDOW), 1)

    def skewed(table, offset):
        rotated = [pltpu.roll(jnp.broadcast_to(table[:, lane_group(m)], (CHUNK, GROUP)), (-offset) % GROUP, 1,
                              stride=1, stride_axis=0) for m in range(BIAS_TABLE // GROUP)]
        in_first = offset - query + lane < GROUP
        return jnp.concatenate(
            [jnp.where(in_first, rotated[m], rotated[m + 1]) for m in range(WINDOW // GROUP)], axis=1)

    def lane_group(m):
        return slice(m * GROUP, (m + 1) * GROUP)

    rows = []
    for h in range(2):
        table = table_ref[0, h:h + 1, :] * LOG2E
        rows.append(jnp.where(key < BAND, skewed(table, CHUNK + CHUNK - 1), MASK_VALUE))
        rows.append(jnp.where(key >= CHUNK, skewed(table, CHUNK - 1), MASK_VALUE))
    return jnp.concatenate(rows, axis=0).T


def _rel_index_tile():
    j = np.arange(WINDOW)[:, None]
    q = np.arange(PAIR_LANES)[None, :] % SUPER
    band_key = j - CHUNK * (q // CHUNK)
    idx = np.clip(band_key - LEFT_CHUNKS * CHUNK - q % CHUNK, -MAX_REL, MAX_REL) + MAX_REL
    return np.where((band_key >= 0) & (band_key < BAND), idx, -1).astype(np.int32)


def _by_head(block):
    low = lax.broadcasted_iota(jnp.int32, block.shape, 1) < HEAD_DIM
    zero = jnp.zeros_like(block)
    return jnp.concatenate([jnp.where(low, block, zero), jnp.where(low, zero, block)], axis=0)


def _own_head_rows(cross):
    head_of_row = lax.broadcasted_iota(jnp.int32, cross.shape, 0) >= HEAD_DIM
    head_of_lane = lax.broadcasted_iota(jnp.int32, cross.shape, 1) >= SUPER
    both = jnp.where(jnp.logical_xor(head_of_row, head_of_lane), 0.0, cross).T
    return both[:SUPER] + both[SUPER:]


def _with_mask_lane(q_rows):
    lane = lax.broadcasted_iota(jnp.int32, q_rows.shape, 1)
    return jnp.concatenate([q_rows, jnp.where(lane == 0, MASK_VALUE, 0.0).astype(q_rows.dtype)], axis=1)


def _band_exp(kb, q_rows, bias):
    s = _dot_t(kb, _with_mask_lane(q_rows)) + bias
    e = jnp.exp2(s - jnp.max(s, axis=0, keepdims=True))
    return e, jnp.sum(e, axis=0, keepdims=True)


def _window(sc):
    return slice(WIN_BASE + sc * SUPER, WIN_BASE + sc * SUPER + WINDOW)


def _shift_band(i, band_ref, new_ref):
    @pl.when(i == 0)
    def _():
        band_ref[:TA] = jnp.zeros((TA, GROUP), band_ref.dtype)

    @pl.when(i > 0)
    def _():
        band_ref[:TA] = band_ref[TA:]

    band_ref[TA:] = new_ref[...].astype(band_ref.dtype)


def _shift_key_band(i, band_ref, new_ref):
    @pl.when(i == 0)
    def _():
        lane = lax.broadcasted_iota(jnp.int32, (TA, 2 * GROUP), 1)
        band_ref[:TA] = jnp.where(lane == GROUP, 1.0, 0.0).astype(band_ref.dtype)
        band_ref[TA:, GROUP:] = jnp.zeros((TA, GROUP), band_ref.dtype)

    @pl.when(i > 0)
    def _():
        band_ref[:TA] = band_ref[TA:]

    band_ref[TA:, :GROUP] = new_ref[...].astype(band_ref.dtype)


def _shift_band_t(i, band_ref, new_ref):
    @pl.when(i == 0)
    def _():
        band_ref[:, :TA] = jnp.zeros((GROUP, TA), band_ref.dtype)

    @pl.when(i > 0)
    def _():
        band_ref[:, :TA] = band_ref[:, TA:]

    band_ref[:, TA:] = new_ref[...].astype(band_ref.dtype).T


def _scaled_queries(q_ref, rows, scale=SCALE):
    return _by_head((q_ref[rows] * scale).astype(BF16))


def _attn_fwd(proj_bf, bias_table):
    seq = proj_bf.shape[0]
    n_tiles = seq // TA

    def body(q_ref, k_ref, v_ref, ag_ref, table_ref, a_ref, ya_ref, e_ref, inv_ref, kband, vband_t, bias_ref):
        i = pl.program_id(1)
        _shift_key_band(i, kband, k_ref)
        _shift_band_t(i, vband_t, v_ref)

        @pl.when(i == 0)
        def _():
            bias_ref[...] = _bias_tile(table_ref)

        def weights(sc):
            rows = slice(sc * SUPER, (sc + 1) * SUPER)
            win = _window(sc)
            e, total = _band_exp(kband[win], _scaled_queries(q_ref, rows, SCALE * LOG2E), bias_ref[...])
            e_ref[0, sc] = e.astype(BF16)
            inv_total = 1.0 / total
            inv_ref[0, sc] = jnp.broadcast_to(inv_total, (8, PAIR_LANES))
            return inv_total

        nxt = weights(0)
        for sc in range(SUPERS_PER_TILE):
            rows = slice(sc * SUPER, (sc + 1) * SUPER)
            win = _window(sc)
            inv_total = nxt
            if sc + 1 < SUPERS_PER_TILE:
                nxt = weights(sc + 1)
            a = _own_head_rows(_dot(vband_t[:, win], e_ref[0, sc]) * inv_total)
            a_ref[rows] = a.astype(BF16)
            g = ag_ref[rows].astype(F32)
            ya_ref[rows] = (a * (g * _sigmoid(g))).astype(BF16)

    blk = pl.BlockSpec((TA, GROUP), lambda p, i: (i, p))

    def cols(first):
        return pl.BlockSpec((TA, GROUP), lambda p, i: (i, first + p))

    return pl.pallas_call(
        body,
        name="attn_fwd",
        grid=(N_PAIR, n_tiles),
        in_specs=[cols(Q_BLOCK), cols(K_BLOCK), cols(V_BLOCK), cols(AG_BLOCK),
                  pl.BlockSpec((1, 2, BIAS_TABLE), lambda p, i: (p, 0, 0))],
        out_specs=[blk, blk,
                   pl.BlockSpec((1, SUPERS_PER_TILE, WINDOW, PAIR_LANES), lambda p, i: (p, i, 0, 0)),
                   pl.BlockSpec((1, SUPERS_PER_TILE, 8, PAIR_LANES), lambda p, i: (p, i, 0, 0))],
        out_shape=[jax.ShapeDtypeStruct((seq, ATTN_WIDTH), BF16), jax.ShapeDtypeStruct((seq, ATTN_WIDTH), BF16),
                   jax.ShapeDtypeStruct((N_PAIR, seq // SUPER, WINDOW, PAIR_LANES), BF16),
                   jax.ShapeDtypeStruct((N_PAIR, seq // SUPER, 8, PAIR_LANES), F32)],
        scratch_shapes=[pltpu.VMEM((2 * TA, 2 * GROUP), BF16), pltpu.VMEM((GROUP, 2 * TA), BF16),
                        pltpu.VMEM((WINDOW, PAIR_LANES), F32)],
        compiler_params=pltpu.CompilerParams(
            dimension_semantics=("arbitrary", "arbitrary"), vmem_limit_bytes=VMEM_LIMIT),
    )(proj_bf, proj_bf, proj_bf, proj_bf, bias_table)


BIN_ROWS = 136


def _bias_bin_sums(db_ref, idx_ref):
    lane = lax.broadcasted_iota(jnp.int32, (1, GROUP), 1)
    row = lax.broadcasted_iota(jnp.int32, (BIN_ROWS, GROUP), 0)
    out = jnp.zeros((BIN_ROWS, GROUP), F32)
    for r in range(N_REL - 1):
        lo = 0 if r == 0 else ((BAND - 2 * CHUNK + r) // 8) * 8
        hi = WINDOW if r == 0 else min(WINDOW, lo + SUPER + 8)
        hit = jnp.where(idx_ref[lo:hi] == r, db_ref[lo:hi], 0.0)
        col = jnp.sum(hit, axis=0, keepdims=True)
        s0 = jnp.sum(col[:, :SUPER], axis=1, keepdims=True)
        s1 = jnp.sum(col[:, SUPER:], axis=1, keepdims=True)
        val = jnp.where(lane == 0, s0, jnp.where(lane == 1, s1, 0.0))
        out = jnp.where(row == r, val, out)
    return out


def _attn_bwd(proj_bf, a, dy, e_all, inv_all):
    seq = proj_bf.shape[0]
    n_tiles = seq // TA

    def body(q_ref, k_ref, v_ref, a_ref, ag_ref, dy_ref, e_ref, inv_ref, idx_ref,
             dq_ref, dk_ref, dv_ref, dag_ref, bins_ref, vband, kband_t, dkacc, dvacc, db_ref):
        i = pl.program_id(1)

        @pl.when(i == 0)
        def _():
            dkacc[...] = jnp.zeros_like(dkacc)
            dvacc[...] = jnp.zeros_like(dvacc)
            db_ref[...] = jnp.zeros_like(db_ref)

        @pl.when(i < n_tiles)
        def _():
            _shift_band(i, vband, v_ref)
            _shift_band_t(i, kband_t, k_ref)

            def score_grads(sc):
                rows = slice(sc * SUPER, (sc + 1) * SUPER)
                win = _window(sc)
                q_rows = _scaled_queries(q_ref, rows)
                g = ag_ref[rows].astype(F32)
                sg = _sigmoid(g)
                dyc = dy_ref[rows].astype(F32)
                dag_ref[rows] = (dyc * a_ref[rows].astype(F32) * (sg * (1.0 + g * (1.0 - sg)))).astype(BF16)
                da_rows = _by_head((dyc * (g * sg)).astype(BF16))
                p = e_ref[0, sc].astype(F32) * inv_ref[0, sc, :1]
                dp = _dot_t(vband[win], da_rows)
                ds = p * (dp - jnp.sum(p * dp, axis=0, keepdims=True))
                db_ref[...] += ds
                return q_rows, da_rows, p.astype(BF16), ds.astype(BF16)

            nxt = score_grads(0)
            for sc in range(SUPERS_PER_TILE):
                rows = slice(sc * SUPER, (sc + 1) * SUPER)
                win = _window(sc)
                q_rows, da_rows, p_bf, ds_bf = nxt
                if sc + 1 < SUPERS_PER_TILE:
                    nxt = score_grads(sc + 1)
                dq_ref[rows] = (_own_head_rows(_dot(kband_t[:, win], ds_bf)) * SCALE).astype(BF16)
                dkacc[win] += _dot(ds_bf, q_rows)
                dvacc[win] += _dot(p_bf, da_rows)

        dk_ref[...] = dkacc[:TA].astype(BF16)
        dv_ref[...] = dvacc[:TA].astype(BF16)
        dkacc[:TA] = dkacc[TA:]
        dvacc[:TA] = dvacc[TA:]
        dkacc[TA:] = jnp.zeros((TA, GROUP), F32)
        dvacc[TA:] = jnp.zeros((TA, GROUP), F32)

        @pl.when(i == n_tiles)
        def _():
            bins_ref[0] = _bias_bin_sums(db_ref, idx_ref)

    last = n_tiles - 1
    cur = pl.BlockSpec((TA, GROUP), lambda p, i: (jnp.minimum(i, last), p))
    older = pl.BlockSpec((TA, GROUP), lambda p, i: (jnp.maximum(i - 1, 0), p))
    dy_blk = pl.BlockSpec((TA, GROUP), lambda p, i: (jnp.minimum(i, last), N_PAIR + p))
    per_pair = pl.BlockSpec((1, BIN_ROWS, GROUP), lambda p, i: (p, 0, 0))

    def cols(first):
        return pl.BlockSpec((TA, GROUP), lambda p, i: (jnp.minimum(i, last), first + p))

    def kept(rows):
        return pl.BlockSpec((1, SUPERS_PER_TILE, rows, PAIR_LANES), lambda p, i: (p, jnp.minimum(i, last), 0, 0))

    def out(dtype):
        return jax.ShapeDtypeStruct((seq, ATTN_WIDTH), dtype)

    return pl.pallas_call(
        body,
        name="attn_bwd",
        grid=(N_PAIR, n_tiles + 1),
        in_specs=[cols(Q_BLOCK), cols(K_BLOCK), cols(V_BLOCK), cur, cols(AG_BLOCK), dy_blk, kept(WINDOW), kept(8),
                  VMEM],
        out_specs=[cur, older, older, cur, per_pair],
        out_shape=[out(BF16), out(BF16), out(BF16), out(BF16),
                   jax.ShapeDtypeStruct((N_PAIR, BIN_ROWS, GROUP), F32)],
        scratch_shapes=[
            pltpu.VMEM((2 * TA, GROUP), BF16), pltpu.VMEM((GROUP, 2 * TA), BF16),
            pltpu.VMEM((2 * TA, GROUP), F32), pltpu.VMEM((2 * TA, GROUP), F32),
            pltpu.VMEM((WINDOW, PAIR_LANES), F32)],
        compiler_params=pltpu.CompilerParams(
            dimension_semantics=("arbitrary", "arbitrary"), vmem_limit_bytes=VMEM_LIMIT),
    )(proj_bf, proj_bf, proj_bf, a, proj_bf, dy, e_all, inv_all, jnp.asarray(_rel_index_tile()))


def _out_loss(x, tgt, proj, ya, wout_full, g2, pw_bf, ps):
    seq = x.shape[0]
    to = min(seq, 2 * TS)
    n_tiles = seq // to

    def body(x_ref, t_ref, pv_ref, pg_ref, ya_ref, w_ref, g2_ref, pw_ref, ps_ref,
             dx2_ref, dy_ref, gw_ref, gg_ref, loss_ref, sq_ref, halo_ref, wt_ref):
        i = pl.program_id(0)

        @pl.when(i == 0)
        def _():
            gw_ref[...] = jnp.zeros_like(gw_ref)
            gg_ref[...] = jnp.zeros_like(gg_ref)
            sq_ref[...] = jnp.zeros_like(sq_ref)
            halo_ref[...] = jnp.zeros_like(halo_ref)
            for b in range(N_CHIP):
                wt_ref[b] = w_ref[b].T

        pv = pv_ref[...].astype(F32)
        pg = pg_ref[...].astype(F32)
        diffs = [d.astype(BF16) for d in _pool_diffs(pv, halo_ref[...], i)]
        halo_ref[...] = pv[to - HALO:, :]
        yp = ((_pool_mix(diffs, pw_ref) * ps_ref[...]) * (pg * _sigmoid(pg))).astype(BF16)
        g2v = g2_ref[...]

        def parts(rows):
            return [yp[rows, :SHARD_OUT], yp[rows, SHARD_OUT:], ya_ref[rows, :SHARD_OUT], ya_ref[rows, SHARD_OUT:]]

        def project(rows, ys):
            x2 = x_ref[rows]
            for b in range(N_CHIP):
                x2 = x2 + _dot(ys[b], w_ref[b])
            return x2

        def norm_loss(rows, x2):
            r = lax.rsqrt(jnp.mean(x2 * x2, axis=-1, keepdims=True) + EPS)
            xh = x2 * r
            diff = xh * g2v - t_ref[rows]
            sq_ref[...] += jnp.sum(diff * diff, axis=0, keepdims=True)
            dfin = diff * (1.0 / D_MODEL)
            gg_ref[...] += jnp.sum(dfin * xh, axis=0, keepdims=True)
            dxh = dfin * g2v
            dx2 = r * (dxh - xh * jnp.mean(dxh * xh, axis=-1, keepdims=True))
            dx2_ref[rows] = dx2
            return dx2.astype(BF16)

        def back(rows, ys, dx2_bf):
            for b in range(N_CHIP):
                gw_ref[b] += _tdot(ys[b], dx2_bf)
                dy_ref[rows, b * SHARD_OUT:(b + 1) * SHARD_OUT] = _dot(dx2_bf, wt_ref[b]).astype(BF16)

        n_parts = 2
        part = to // n_parts
        spans = [slice(r * part, (r + 1) * part) for r in range(n_parts)]
        ys = [parts(rows) for rows in spans]
        x2_next = project(spans[0], ys[0])
        for r in range(n_parts):
            x2 = x2_next
            if r + 1 < n_parts:
                x2_next = project(spans[r + 1], ys[r + 1])
            back(spans[r], ys[r], norm_loss(spans[r], x2))

        @pl.when(i == n_tiles - 1)
        def _():
            total = jnp.sum(sq_ref[...], axis=1, keepdims=True) * (0.5 / D_MODEL)
            loss_ref[...] = jnp.broadcast_to(total, loss_ref.shape)

    def rows(width, col=0):
        return pl.BlockSpec((to, width), lambda i: (i, col))

    return pl.pallas_call(
        body,
        name="out_loss",
        grid=(n_tiles,),
        in_specs=[rows(D_MODEL), rows(D_MODEL), rows(POOL_WIDTH, 0), rows(POOL_WIDTH, 1), rows(ATTN_WIDTH),
                  VMEM, VMEM, VMEM, VMEM],
        out_specs=[rows(D_MODEL), rows(D_MODEL), VMEM, VMEM, VMEM],
        out_shape=[
            jax.ShapeDtypeStruct((seq, D_MODEL), F32), jax.ShapeDtypeStruct((seq, D_MODEL), BF16),
            jax.ShapeDtypeStruct((N_CHIP, SHARD_OUT, D_MODEL), F32), jax.ShapeDtypeStruct((1, D_MODEL), F32),
            jax.ShapeDtypeStruct((8, GROUP), F32)],
        scratch_shapes=[pltpu.VMEM((1, D_MODEL), F32), pltpu.VMEM((HALO, POOL_WIDTH), F32),
                        pltpu.VMEM((N_CHIP, D_MODEL, SHARD_OUT), BF16)],
        compiler_params=pltpu.CompilerParams(dimension_semantics=("arbitrary",), vmem_limit_bytes=VMEM_LIMIT),
    )(x, tgt, proj, proj, ya, wout_full, g2, pw_bf, ps)


def _inproj_bwd(x, dx2, dy, proj, dq, dk, dv, dag, win_full, pw_bf, g1, ps):
    seq = x.shape[0]
    n_tiles = seq // TS

    def body(x_ref, dx2_ref, dyp_ref, pv_ref, pvprev_ref, pg_ref, dq_ref, dk_ref, dv_ref, dag_ref,
             w_ref, pw_ref, g1_ref, ps_ref, gx_ref, dproj_ref, ht_ref, gg_ref, gps_ref, gpw_ref, halo_ref):
        i = pl.program_id(0)
        tile = n_tiles - 1 - i

        @pl.when(i == 0)
        def _():
            gg_ref[...] = jnp.zeros_like(gg_ref)
            gps_ref[...] = jnp.zeros_like(gps_ref)
            gpw_ref[...] = jnp.zeros_like(gpw_ref)
            halo_ref[...] = jnp.zeros_like(halo_ref)

        pv_t = pv_ref[...].astype(F32)
        pg_t = pg_ref[...].astype(F32)
        prev_rows = jnp.where(tile > 0, pvprev_ref[...].astype(F32), 0.0)
        diffs = [d.astype(BF16) for d in _pool_diffs(pv_t, prev_rows, tile)]
        mixed = _pool_mix(diffs, pw_ref)
        sg = _sigmoid(pg_t)
        silu = pg_t * sg
        dyp = dyp_ref[...].astype(F32)
        psv = ps_ref[...]
        gps_ref[...] += jnp.sum(dyp * mixed * silu, axis=0, keepdims=True)
        dmixed = (dyp * psv * silu).astype(BF16)
        dpg = dyp * (mixed * psv) * (sg * (1.0 + pg_t * (1.0 - sg)))
        counts = _row_counts(tile, TS)
        dds = []
        for g in range(len(POOL_WINDOWS)):
            dm_g = dmixed[:, g * GROUP:(g + 1) * GROUP]
            gpw_ref[g] += _tdot(diffs[g], dm_g)
            dds.append(_dot_t(dm_g, pw_ref[g]))
        dd = jnp.concatenate(dds, axis=1)
        spread = jnp.concatenate([dds[g] / counts[g] for g in range(len(POOL_WINDOWS))], axis=1)
        sums = _window_sums(jnp.concatenate([spread, halo_ref[...]], axis=0), forward=True)
        halo_ref[...] = spread[:HALO]
        dpv = jnp.concatenate(
            [sums[g][:TS, g * GROUP:(g + 1) * GROUP] for g in range(len(POOL_WINDOWS))], axis=1) - dd

        xt = x_ref[...]
        r = lax.rsqrt(jnp.mean(xt * xt, axis=-1, keepdims=True) + EPS)
        xh = xt * r
        g1v = g1_ref[...]
        ht_ref[...] = (xh * g1v).astype(BF16).T
        dproj = jnp.concatenate(
            [dpv.astype(BF16), dpg.astype(BF16), dq_ref[...], dk_ref[...], dv_ref[...], dag_ref[...]],
            axis=1)
        dproj_ref[...] = dproj
        dh = _dot_t(dproj[:, :SHARD_IN], w_ref[0])
        for chip in range(1, N_CHIP):
            dh = dh + _dot_t(dproj[:, chip * SHARD_IN:(chip + 1) * SHARD_IN], w_ref[chip])

        gg_ref[...] += jnp.sum(dh * xh, axis=0, keepdims=True)
        dxh = dh * g1v
        gx_ref[...] = dx2_ref[...] + r * (dxh - xh * jnp.mean(dxh * xh, axis=-1, keepdims=True))

    def rows(width, col=0):
        return pl.BlockSpec((TS, width), lambda i: (n_tiles - 1 - i, col))

    prev = pl.BlockSpec((HALO, POOL_WIDTH), lambda i: (jnp.maximum((n_tiles - 1 - i) * (TS // HALO) - 1, 0), 0))
    return pl.pallas_call(
        body,
        name="inproj_bwd",
        grid=(n_tiles,),
        in_specs=[rows(D_MODEL), rows(D_MODEL), rows(POOL_WIDTH), rows(POOL_WIDTH), prev, rows(POOL_WIDTH, 1),
                  rows(ATTN_WIDTH), rows(ATTN_WIDTH), rows(ATTN_WIDTH), rows(ATTN_WIDTH), VMEM, VMEM, VMEM, VMEM],
        out_specs=[rows(D_MODEL), rows(N_CHIP * SHARD_IN),
                   pl.BlockSpec((D_MODEL, TS), lambda i: (0, n_tiles - 1 - i)), VMEM, VMEM, VMEM],
        out_shape=[
            jax.ShapeDtypeStruct((seq, D_MODEL), F32),
            jax.ShapeDtypeStruct((seq, N_CHIP * SHARD_IN), BF16),
            jax.ShapeDtypeStruct((D_MODEL, seq), BF16),
            jax.ShapeDtypeStruct((1, D_MODEL), F32),
            jax.ShapeDtypeStruct((1, POOL_WIDTH), F32),
            jax.ShapeDtypeStruct((len(POOL_WINDOWS), GROUP, GROUP), F32)],
        scratch_shapes=[pltpu.VMEM((HALO, POOL_WIDTH), F32)],
        compiler_params=pltpu.CompilerParams(dimension_semantics=("arbitrary",), vmem_limit_bytes=VMEM_LIMIT),
    )(x, dx2, dy, proj, proj, proj, dq, dk, dv, dag, win_full, pw_bf, g1, ps)


PASS_PEER = (2, 0, 1)


def _gw_reduce(ht, dproj, gwout, small):
    seq = ht.shape[1]
    tm = min(2 * TS, seq // 4)
    n_tiles = seq // tm
    half_small = SMALL_ROWS // 2
    cx, cy = lax.axis_index("x"), lax.axis_index("y")
    outer = _other_chips(cx, cy)
    order = jnp.stack([outer[n][1] for n in PASS_PEER] + [2 * cx + cy]).astype(jnp.int32)

    def body(order_ref, ht_ref, dp_ref, gwout_ref, small_ref, gin_final, gout_final, small_final,
             hbuf, acc, pair_in, pair_out, pair_small, tx_in, tx_out, rx_in, rx_out, rx_small,
             gin_out, gout_out, small_out, send_sems, recv_sems, out_sems):
        j = pl.program_id(0)
        i = pl.program_id(1)
        x, y, c = _my_place()
        b = 2 * x + y
        sibling = (x, y, 1 - c)
        others = _other_chips(x, y)
        mine_in = pl.ds(pl.multiple_of(c * HALF_IN, HALF_IN), HALF_IN)
        mine_out = pl.ds(pl.multiple_of(c * HALF_OUT, HALF_OUT), HALF_OUT)
        mine_small = pl.ds(pl.multiple_of(c * half_small, 8), half_small)
        theirs_in = pl.ds(pl.multiple_of((1 - c) * HALF_IN, HALF_IN), HALF_IN)
        theirs_out = pl.ds(pl.multiple_of((1 - c) * HALF_OUT, HALF_OUT), HALF_OUT)
        theirs_small = pl.ds(pl.multiple_of((1 - c) * half_small, 8), half_small)

        def copy(k, src, dst, to):
            return pltpu.make_async_remote_copy(
                src_ref=src, dst_ref=dst, send_sem=send_sems.at[k], recv_sem=recv_sems.at[k],
                device_id=to, device_id_type=MESH)

        swap_out = copy(0, gwout_ref.at[:, theirs_out], pair_out, sibling)
        swap_small = copy(1, small_ref, pair_small, sibling)

        def swap_in(p):
            return copy(2 + p, acc.at[p % 2, theirs_in], pair_in.at[p % 2], sibling)

        def to_chip(n, t):
            to = (*others[n][0], c)
            if t == 0:
                return copy(6 + 3 * n, tx_in.at[n], rx_in.at[n], to)
            if t == 1:
                return copy(7 + 3 * n, tx_out.at[n], rx_out.at[n], to)
            return copy(8 + 3 * n, pair_small.at[mine_small], rx_small.at[b], to)

        share_in = copy(15, gin_out.at[mine_in], gin_out.at[mine_in], sibling)
        share_out = copy(16, gout_out.at[mine_out], gout_out.at[mine_out], sibling)
        share_small = copy(17, small_out.at[mine_small], small_out.at[mine_small], sibling)

        def at(jj, ii):
            return (j == jj) & (i == ii)

        par = j % 2

        @pl.when(i == 0)
        def _():
            acc[par] = jnp.zeros((D_MODEL, SHARD_IN), F32)

        cols = pl.ds(pl.multiple_of(i * tm, tm), tm)

        @pl.when(j == 0)
        def _():
            hbuf[:, cols] = ht_ref[...]

        acc[par] += _dot(hbuf[:, cols], dp_ref[...])

        @pl.when(at(0, 0))
        def _():
            swap_out.start()
            swap_small.start()

        @pl.when(at(0, 2))
        def _():
            swap_out.wait_recv()
            swap_small.wait_recv()
            for chip in range(N_CHIP):
                pair_out[chip] = gwout_ref[chip, mine_out] + pair_out[chip]
            pair_small[...] = small_ref[...] + pair_small[...]
            rx_small[b] = pair_small[mine_small]
            for n in range(N_CHIP - 1):
                tx_out[n] = pair_out[others[n][1]].astype(BF16)
                to_chip(n, 1).start()
                to_chip(n, 2).start()

        @pl.when(at(1, 3))
        def _():
            total_out = pair_out[b]
            for n in range(N_CHIP - 1):
                to_chip(n, 1).wait_recv()
                copy(8 + 3 * n, pair_small.at[mine_small], rx_small.at[others[n][1]], (*others[n][0], c)).wait_recv()
                total_out = total_out + rx_out[n].astype(F32)
            gout_out[mine_out] = total_out
            small_out[mine_small] = ((rx_small[0] + rx_small[1]) + rx_small[2]) + rx_small[3]
            share_out.start()
            share_small.start()

        for p in range(N_CHIP - 1):
            n = PASS_PEER[p]

            @pl.when(at(p + 1, 0))
            def _(p=p):
                swap_in(p).start()

            @pl.when(at(p + 1, 2))
            def _(p=p, n=n):
                swap_in(p).wait_recv()
                swap_in(p).wait_send()
                tx_in[n] = (acc[p % 2, mine_in] + pair_in[p % 2]).astype(BF16)
                to_chip(n, 0).start()

        @pl.when(at(N_CHIP - 1, n_tiles - 1))
        def _():
            last = N_CHIP - 1
            swap_in(last).start()
            swap_in(last).wait_recv()
            total_in = acc[last % 2, mine_in] + pair_in[last % 2]
            for n in range(N_CHIP - 1):
                to_chip(n, 0).wait_recv()
                total_in = total_in + rx_in[n].astype(F32)
            gin_out[mine_in] = total_in
            share_in.start()
            copy(15, gin_out.at[theirs_in], gin_out.at[theirs_in], sibling).wait_recv()
            copy(16, gout_out.at[theirs_out], gout_out.at[theirs_out], sibling).wait_recv()
            copy(17, small_out.at[theirs_small], small_out.at[theirs_small], sibling).wait_recv()
            swap_out.wait_send()
            swap_small.wait_send()
            swap_in(last).wait_send()
            for n in range(N_CHIP - 1):
                for t in range(3):
                    to_chip(n, t).wait_send()
            share_in.wait_send()
            share_out.wait_send()
            share_small.wait_send()
            outs = [pltpu.make_async_copy(src, dst, out_sems.at[k]) for k, (src, dst) in enumerate(
                [(gin_out, gin_final), (gout_out, gout_final), (small_out, small_final)])]
            for cp in outs:
                cp.start()
            for cp in outs:
                cp.wait()

    assert n_tiles >= 4, "the reduction's steps are spread over the first four token steps of a pass"
    grid_spec = pltpu.PrefetchScalarGridSpec(
        num_scalar_prefetch=1,
        grid=(N_CHIP, n_tiles),
        in_specs=[
            pl.BlockSpec((D_MODEL, tm), lambda j, i, o: (0, jnp.where(j == 0, i, n_tiles - 1))),
            pl.BlockSpec((tm, SHARD_IN), lambda j, i, o: (i, o[j])),
            VMEM, VMEM],
        out_specs=[ANY, ANY, ANY],
        scratch_shapes=[
            pltpu.VMEM((D_MODEL, seq), BF16),
            pltpu.VMEM((2, D_MODEL, SHARD_IN), F32),
            pltpu.VMEM((2, HALF_IN, SHARD_IN), F32),
            pltpu.VMEM((N_CHIP, HALF_OUT, D_MODEL), F32),
            pltpu.VMEM((SMALL_ROWS, GROUP), F32),
            pltpu.VMEM((N_CHIP - 1, HALF_IN, SHARD_IN), BF16),
            pltpu.VMEM((N_CHIP - 1, HALF_OUT, D_MODEL), BF16),
            pltpu.VMEM((N_CHIP - 1, HALF_IN, SHARD_IN), BF16),
            pltpu.VMEM((N_CHIP - 1, HALF_OUT, D_MODEL), BF16),
            pltpu.VMEM((N_CHIP, half_small, GROUP), F32),
            pltpu.VMEM((D_MODEL, SHARD_IN), F32),
            pltpu.VMEM((SHARD_OUT, D_MODEL), F32),
            pltpu.VMEM((SMALL_ROWS, GROUP), F32),
            pltpu.SemaphoreType.DMA((18,)),
            pltpu.SemaphoreType.DMA((18,)),
            pltpu.SemaphoreType.DMA((3,)),
        ],
    )
    return pl.pallas_call(
        body,
        name="gw_reduce",
        grid_spec=grid_spec,
        out_shape=(
            jax.ShapeDtypeStruct((D_MODEL, SHARD_IN), F32),
            jax.ShapeDtypeStruct((SHARD_OUT, D_MODEL), F32),
            jax.ShapeDtypeStruct((SMALL_ROWS, GROUP), F32),
        ),
        compiler_params=pltpu.CompilerParams(
            dimension_semantics=("arbitrary", "arbitrary"), vmem_limit_bytes=VMEM_LIMIT),
    )(order, ht, dproj, gwout, small)


def _adam_update(w, grad, m, v):
    m_new = ADAM_B1 * m + (1.0 - ADAM_B1) * grad
    v_new = ADAM_B2 * v + (1.0 - ADAM_B2) * (grad * grad)
    m_hat = m_new / (1.0 - ADAM_B1 ** ADAM_STEP)
    v_hat = v_new / (1.0 - ADAM_B2 ** ADAM_STEP)
    delta = -ADAM_LR * (m_hat / (jnp.sqrt(v_hat) + ADAM_EPS) + ADAM_WD * w)
    return delta, m_new, v_new


def _adamw(name, w, g, m, v, block_rows):
    rows, cols = w.shape

    def body(w_ref, g_ref, m_ref, v_ref, d_ref, m_out, v_out):
        d_ref[...], m_out[...], v_out[...] = _adam_update(w_ref[...], g_ref[...], m_ref[...], v_ref[...])

    blk = pl.BlockSpec((block_rows, cols), lambda i: (i, 0))
    shape = jax.ShapeDtypeStruct((rows, cols), F32)
    return pl.pallas_call(
        body,
        name=name,
        grid=(rows // block_rows,),
        in_specs=[blk] * 4,
        out_specs=[blk] * 3,
        out_shape=[shape] * 3,
        compiler_params=pltpu.CompilerParams(dimension_semantics=("arbitrary",)),
    )(w, g, m, v)


def _adamw_small(g_small, params, m_state, v_state):
    n_param = len(params)
    n_head = params[3].shape[0]

    def body(g_ref, *refs):
        w_refs, m_refs, v_refs = (refs[k * n_param:(k + 1) * n_param] for k in range(3))
        g_outs, d_outs, m_outs, v_outs = (refs[k * n_param:(k + 1) * n_param] for k in range(3, 7))

        def update(i, packed_rows, rows, cols, packed_cols=slice(None)):
            grad = g_ref[packed_rows, packed_cols]
            g_outs[i][rows, cols] = grad
            d_outs[i][rows, cols], m_outs[i][rows, cols], v_outs[i][rows, cols] = _adam_update(
                w_refs[i][rows, cols], grad, m_refs[i][rows, cols], v_refs[i][rows, cols])

        def lane_group(r):
            return slice(r * GROUP, (r + 1) * GROUP)

        everything = slice(None)
        for r in range(D_MODEL // GROUP):
            update(0, slice(r, r + 1), everything, lane_group(r))
            update(4, slice(FINAL_GAIN_ROW + r, FINAL_GAIN_ROW + r + 1), everything, lane_group(r))
        update(1, slice(POOL_W_ROW, POOL_W_ROW + len(POOL_WINDOWS) * GROUP), everything, everything)
        for r in range(POOL_WIDTH // GROUP):
            update(2, slice(POOL_SCALE_ROW + r, POOL_SCALE_ROW + r + 1), everything, lane_group(r))
        for h in range(n_head):
            first = REL_BIAS_ROW + REL_BIAS_ROWS_PER_HEAD * h
            update(3, slice(first, first + 1), slice(h, h + 1), slice(0, GROUP))
            update(3, slice(first + 1, first + 2), slice(h, h + 1), slice(GROUP, N_REL), slice(0, N_REL - GROUP))

    shapes = [jax.ShapeDtypeStruct(p.shape, F32) for p in params]
    outs = pl.pallas_call(
        body,
        name="adamw_small",
        out_shape=shapes * 4,
    )(g_small, *params, *m_state, *v_state)
    return [outs[k * n_param:(k + 1) * n_param] for k in range(4)]


def _pack_small(norm_gain, pool_w, pool_scale, rel_bias, final_gain, loss_rows):
    parts = [
        norm_gain.reshape(8, GROUP),
        pool_w.reshape(len(POOL_WINDOWS) * GROUP, GROUP),
        jnp.pad(pool_scale.reshape(4, GROUP), ((0, 4), (0, 0))),
        jnp.pad(rel_bias.reshape(8, N_REL), ((0, 0), (0, 2 * GROUP - N_REL))).reshape(16, GROUP),
        final_gain.reshape(8, GROUP),
        loss_rows,
    ]
    return jnp.concatenate(parts, axis=0)


def kernel(x, norm_gain, w_in, pool_w, pool_scale, rel_bias, w_out, final_norm_gain, loss_target, m_norm_gain, m_w_in, m_pool_w, m_pool_scale, m_rel_bias, m_w_out, m_final_norm_gain, v_norm_gain, v_w_in, v_pool_w, v_pool_scale, v_rel_bias, v_w_out, v_final_norm_gain):
    assert x.shape[1] % TS == 0 and x.shape[2] == D_MODEL
    xs = x[0]
    tgt = loss_target[0]
    g1 = norm_gain.reshape(1, D_MODEL)
    g2 = final_norm_gain.reshape(1, D_MODEL)
    ps = pool_scale.reshape(1, POOL_WIDTH)
    pw_bf = pool_w[0].astype(BF16)

    proj, win_full, wout_full = _gather_inproj(xs, g1, w_in[0].astype(BF16), w_out[0].astype(BF16))
    bias_table = _bias_table(rel_bias[0])

    a, ya, e_all, inv_all = _attn_fwd(proj, bias_table)
    dx2, dy, gwout, gg2, loss_rows = _out_loss(xs, tgt, proj, ya, wout_full, g2, pw_bf, ps)
    dq, dk, dv, dag, bins = _attn_bwd(proj, a, dy, e_all, inv_all)
    gx, dproj, ht, gg1, gps, gpw = _inproj_bwd(xs, dx2, dy, proj, dq, dk, dv, dag, win_full, pw_bf, g1, ps)

    g_bias = bins[:, :N_REL, :2].transpose(0, 2, 1).reshape(8, N_REL)
    small = _pack_small(gg1, gpw, gps, g_bias, gg2, loss_rows)
    g_win, g_wout, g_small = _gw_reduce(ht, dproj, gwout, small)
    loss = g_small[LOSS_ROW, 0]

    def small_views(norm, pool, scale, bias, final):
        return [norm.reshape(1, D_MODEL), pool.reshape(len(POOL_WINDOWS) * GROUP, GROUP), scale.reshape(1, POOL_WIDTH),
                bias.reshape(-1, N_REL), final.reshape(1, D_MODEL)]

    d_win, m_win, v_win = _adamw("adamw_w_in", w_in[0], g_win, m_w_in[0], v_w_in[0], 256)
    d_wout, m_wout, v_wout = _adamw("adamw_w_out", w_out[0], g_wout, m_w_out[0], v_w_out[0], 128)
    small_results = _adamw_small(
        g_small,
        small_views(norm_gain, pool_w, pool_scale, rel_bias, final_norm_gain),
        small_views(m_norm_gain, m_pool_w, m_pool_scale, m_rel_bias, m_final_norm_gain),
        small_views(v_norm_gain, v_pool_w, v_pool_scale, v_rel_bias, v_final_norm_gain))

    def full(win_part, wout_part, small_parts):
        norm, pool, scale, bias, final = small_parts
        return [norm.reshape(norm_gain.shape), win_part[None], pool.reshape(pool_w.shape), scale.reshape(pool_scale.shape),
                bias.reshape(rel_bias.shape), wout_part[None], final.reshape(final_norm_gain.shape)]

    grads = full(g_win, g_wout, small_results[0])
    deltas = full(d_win, d_wout, small_results[1])
    new_m = full(m_win, m_wout, small_results[2])
    new_v = full(v_win, v_wout, small_results[3])
    return (loss, gx[None], *grads, *deltas, *new_m, *new_v)
```

```python
import numpy as np
import jax
import jax.numpy as jnp
from jax import lax
from jax.experimental import pallas as pl
from jax.experimental.pallas import tpu as pltpu

F32 = jnp.float32
BF16 = jnp.bfloat16

D_MODEL = 1024
POOL_WIDTH = 512
ATTN_WIDTH = 512
POOL_WINDOWS = (2, 4, 8, 16)
GROUP = 128
CHUNK = 64
LEFT_CHUNKS = 8
BAND = (LEFT_CHUNKS + 1) * CHUNK
HEAD_DIM = 64
N_PAIR = 4
MAX_REL = 64
N_REL = 2 * MAX_REL + 1
EPS = 1e-6
MASK_VALUE = -1e30
SCALE = 0.125
LOG2E = 1.4426950408889634

ADAM_LR = 0.001
ADAM_B1 = 0.9
ADAM_B2 = 0.999
ADAM_EPS = 1e-08
ADAM_WD = 0.01
ADAM_STEP = 10

TS = LEFT_CHUNKS * CHUNK
SUPER = 2 * CHUNK
WINDOW = BAND + CHUNK
TA = 4 * TS
WIN_BASE = TA - LEFT_CHUNKS * CHUNK
SUPERS_PER_TILE = TA // SUPER
PAIR_LANES = 2 * SUPER
HALO = 16
N_CHIP = 4
SHARD_IN = 768
SHARD_OUT = 256
PIECE = 256
PIECES_PER_SHARD = SHARD_IN // PIECE
HALF_IN = D_MODEL // 2
HALF_OUT = SHARD_OUT // 2
SMALL_ROWS = 560
POOL_W_ROW = 8
POOL_SCALE_ROW = 520
REL_BIAS_ROW = 528
REL_BIAS_ROWS_PER_HEAD = 2
FINAL_GAIN_ROW = 544
LOSS_ROW = 552
VMEM_LIMIT = 60 * 1024 * 1024

MESH = pl.DeviceIdType.MESH
ANY = pl.BlockSpec(memory_space=pl.ANY)
VMEM = pl.BlockSpec(memory_space=pltpu.VMEM)


def _sigmoid(x):
    return 1.0 / (1.0 + jnp.exp(-x))


def _dot(a, b):
    return jnp.dot(a, b, preferred_element_type=F32)


def _dot_t(a, b):
    return lax.dot_general(a, b, (((1,), (1,)), ((), ())), preferred_element_type=F32)


def _tdot(a, b):
    return lax.dot_general(a, b, (((0,), (0,)), ((), ())), preferred_element_type=F32)


def _my_place():
    return lax.axis_index("x"), lax.axis_index("y"), lax.axis_index("c")


def _other_chips(x, y):
    places = [(1 - x, y), (x, 1 - y), (1 - x, 1 - y)]
    return [(p, 2 * p[0] + p[1]) for p in places]


def _gather_inproj(x, g1, win_sh, wout_sh):
    seq = x.shape[0]
    tm = min(seq, 4 * TS)
    n_tiles = seq // tm
    cx, cy = lax.axis_index("x"), lax.axis_index("y")
    order = jnp.stack([2 * cx + cy] + [chip for _, chip in _other_chips(cx, cy)]).astype(jnp.int32)

    def body(order_ref, x_ref, g1_ref, win_ref, wout_ref, proj_bf_ref, win_full, wout_full,
             hbuf, wbuf, send_sems, recv_sems, local_sems):
        j = pl.program_id(0)
        i = pl.program_id(1)
        x_, y_, c = _my_place()
        b = 2 * x_ + y_
        sibling = (x_, y_, 1 - c)
        others = _other_chips(x_, y_)

        def halves(chip, core):
            return (
                win_full.at[chip, pl.ds(core * HALF_IN, HALF_IN)],
                wout_full.at[chip, pl.ds(core * HALF_OUT, HALF_OUT)],
            )

        def copy(k, src, dst, to):
            return pltpu.make_async_remote_copy(
                src_ref=src, dst_ref=dst, send_sem=send_sems.at[k], recv_sem=recv_sems.at[k],
                device_id=to, device_id_type=MESH)

        own = [
            pltpu.make_async_copy(win_ref, win_full.at[b], local_sems.at[0]),
            pltpu.make_async_copy(wout_ref, wout_full.at[b], local_sems.at[1]),
        ]
        mine_src = (win_ref.at[pl.ds(c * HALF_IN, HALF_IN)], wout_ref.at[pl.ds(c * HALF_OUT, HALF_OUT)])

        def direct(n, t):
            return copy(2 * n + t, mine_src[t], halves(b, c)[t], (*others[n][0], c))

        def arrival(n, t):
            landed = halves(others[n][1], c)[t]
            return copy(2 * n + t, landed, landed, (*others[n][0], c))

        def passing(n, t):
            landed = halves(others[n][1], c)[t]
            return copy(6 + 2 * n + t, landed, landed, sibling)

        def from_sibling(n, t):
            landed = halves(others[n][1], 1 - c)[t]
            return copy(6 + 2 * n + t, landed, landed, sibling)

        @pl.when((j == 0) & (i == 0))
        def _():
            first = pltpu.make_async_copy(win_ref, wbuf.at[0], local_sems.at[2])
            first.start()
            for cp in own:
                cp.start()
            for t in range(2):
                for n in range(N_CHIP - 1):
                    direct(n, t).start()
            first.wait()

        def load(n):
            return pltpu.make_async_copy(win_full.at[others[n][1]], wbuf.at[(n + 1) % 2], local_sems.at[2])

        for n in range(N_CHIP - 1):
            @pl.when((j == n) & (i == n_tiles - 1))
            def _(n=n):
                arrival(n, 0).wait_recv()
                passing(n, 0).start()

            @pl.when((j == n + 1) & (i == 0))
            def _(n=n):
                load(n).wait()

        rows = pl.ds(pl.multiple_of(i * tm, tm), tm)

        @pl.when(j == 0)
        def _():
            xt = x_ref[...]
            r = lax.rsqrt(jnp.mean(xt * xt, axis=-1, keepdims=True) + EPS)
            hbuf[rows] = ((xt * r) * g1_ref[...]).astype(BF16)

        proj_bf_ref[...] = _dot(hbuf[rows], wbuf[j % 2]).astype(BF16)

        for n in range(N_CHIP - 1):
            @pl.when((j == n) & (i == n_tiles - 1))
            def _(n=n):
                from_sibling(n, 0).wait_recv()
                load(n).start()

        @pl.when((j == N_CHIP - 1) & (i == n_tiles - 1))
        def _():
            for n in range(N_CHIP - 1):
                arrival(n, 1).wait_recv()
                passing(n, 1).start()
            for n in range(N_CHIP - 1):
                from_sibling(n, 1).wait_recv()
            for n in range(N_CHIP - 1):
                for t in range(2):
                    direct(n, t).wait_send()
                    passing(n, t).wait_send()
            for cp in own:
                cp.wait()

    last = n_tiles - 1
    grid_spec = pltpu.PrefetchScalarGridSpec(
        num_scalar_prefetch=1,
        grid=(N_CHIP, n_tiles),
        in_specs=[
            pl.BlockSpec((tm, D_MODEL), lambda j, i, o: (jnp.where(j == 0, i, last), 0)),
            VMEM, ANY, ANY],
        out_specs=[pl.BlockSpec((tm, SHARD_IN), lambda j, i, o: (i, o[j])), ANY, ANY],
        scratch_shapes=[
            pltpu.VMEM((seq, D_MODEL), BF16), pltpu.VMEM((2, D_MODEL, SHARD_IN), BF16),
            pltpu.SemaphoreType.DMA((12,)), pltpu.SemaphoreType.DMA((12,)), pltpu.SemaphoreType.DMA((3,))],
    )
    return pl.pallas_call(
        body,
        name="gather_inproj",
        grid_spec=grid_spec,
        out_shape=(
            jax.ShapeDtypeStruct((seq, N_CHIP * SHARD_IN), BF16),
            jax.ShapeDtypeStruct((N_CHIP, D_MODEL, SHARD_IN), BF16),
            jax.ShapeDtypeStruct((N_CHIP, SHARD_OUT, D_MODEL), BF16),
        ),
        compiler_params=pltpu.CompilerParams(
            dimension_semantics=("arbitrary", "arbitrary"), vmem_limit_bytes=VMEM_LIMIT),
    )(order, x, g1, win_sh, wout_sh)


def _window_sums(ext, forward):
    n = ext.shape[0]
    sums = []
    acc = ext
    for step in (1, 2, 4, 8):
        acc = acc + pltpu.roll(acc, (n - step) if forward else step, 0)
        sums.append(acc)
    return sums


def _row_counts(tile, rows):
    t = tile * rows + lax.broadcasted_iota(jnp.int32, (rows, GROUP), 0)
    return [jnp.minimum(t + 1, w).astype(F32) for w in POOL_WINDOWS]


def _pool_diffs(pv, prev_rows, tile):
    ext = jnp.concatenate([prev_rows, pv], axis=0)
    sums = _window_sums(ext, forward=False)
    counts = _row_counts(tile, pv.shape[0])
    out = []
    for g in range(len(POOL_WINDOWS)):
        cols = slice(g * GROUP, (g + 1) * GROUP)
        out.append(sums[g][HALO:, cols] / counts[g] - pv[:, cols])
    return out


def _pool_mix(diffs, pw_ref):
    return jnp.concatenate([_dot(diffs[g], pw_ref[g]) for g in range(len(POOL_WINDOWS))], axis=1)


Q_BLOCK, K_BLOCK, V_BLOCK, AG_BLOCK = 8, 12, 16, 20


BIAS_TABLE = WINDOW + GROUP


def _bias_table(rel_bias):
    far = jnp.broadcast_to(rel_bias[:, :1], (rel_bias.shape[0], BAND - CHUNK - 1))
    by_distance = jnp.concatenate([far, rel_bias[:, :2 * MAX_REL]], axis=1)
    table = jnp.pad(by_distance, ((0, 0), (CHUNK, BIAS_TABLE - CHUNK - by_distance.shape[1])))
    return table.reshape(N_PAIR, 2, BIAS_TABLE)


def _bias_tile(table_ref):
    query = lax.broadcasted_iota(jnp.int32, (CHUNK, GROUP), 0)
    lane = lax.broadcasted_iota(jnp.int32, (CHUNK, GROUP), 1)
    key = lax.broadcasted_iota(jnp.int32, (CHUNK, WINDOW), 1)

    def skewed(table, offset):
        rotated = [pltpu.roll(jnp.broadcast_to(table[:, lane_group(m)], (CHUNK, GROUP)), (-offset) % GROUP, 1,
                              stride=1, stride_axis=0) for m in range(BIAS_TABLE // GROUP)]
        in_first = offset - query + lane < GROUP
        return jnp.concatenate(
            [jnp.where(in_first, rotated[m], rotated[m + 1]) for m in range(WINDOW // GROUP)], axis=1)

    def lane_group(m):
        return slice(m * GROUP, (m + 1) * GROUP)

    rows = []
    for h in range(2):
        table = table_ref[0, h:h + 1, :] * LOG2E
        rows.append(jnp.where(key < BAND, skewed(table, CHUNK + CHUNK - 1), MASK_VALUE))
        rows.append(jnp.where(key >= CHUNK, skewed(table, CHUNK - 1), MASK_VALUE))
    return jnp.concatenate(rows, axis=0).T


def _rel_index_tile():
    j = np.arange(WINDOW)[:, None]
    q = np.arange(PAIR_LANES)[None, :] % SUPER
    band_key = j - CHUNK * (q // CHUNK)
    idx = np.clip(band_key - LEFT_CHUNKS * CHUNK - q % CHUNK, -MAX_REL, MAX_REL) + MAX_REL
    return np.where((band_key >= 0) & (band_key < BAND), idx, -1).astype(np.int32)


def _by_head(block):
    low = lax.broadcasted_iota(jnp.int32, block.shape, 1) < HEAD_DIM
    zero = jnp.zeros_like(block)
    return jnp.concatenate([jnp.where(low, block, zero), jnp.where(low, zero, block)], axis=0)


def _own_head_rows(cross):
    head_of_row = lax.broadcasted_iota(jnp.int32, cross.shape, 0) >= HEAD_DIM
    head_of_lane = lax.broadcasted_iota(jnp.int32, cross.shape, 1) >= SUPER
    both = jnp.where(jnp.logical_xor(head_of_row, head_of_lane), 0.0, cross).T
    return both[:SUPER] + both[SUPER:]


def _with_mask_lane(q_rows):
    lane = lax.broadcasted_iota(jnp.int32, q_rows.shape, 1)
    return jnp.concatenate([q_rows, jnp.where(lane == 0, MASK_VALUE, 0.0).astype(q_rows.dtype)], axis=1)


def _band_exp(kb, q_rows, bias):
    s = _dot_t(kb, _with_mask_lane(q_rows)) + bias
    e = jnp.exp2(s - jnp.max(s, axis=0, keepdims=True))
    return e, jnp.sum(e, axis=0, keepdims=True)


def _window(sc):
    return slice(WIN_BASE + sc * SUPER, WIN_BASE + sc * SUPER + WINDOW)


def _shift_band(i, band_ref, new_ref):
    @pl.when(i == 0)
    def _():
        band_ref[:TA] = jnp.zeros((TA, GROUP), band_ref.dtype)

    @pl.when(i > 0)
    def _():
        band_ref[:TA] = band_ref[TA:]

    band_ref[TA:] = new_ref[...].astype(band_ref.dtype)


def _shift_key_band(i, band_ref, new_ref):
    @pl.when(i == 0)
    def _():
        lane = lax.broadcasted_iota(jnp.int32, (TA, 2 * GROUP), 1)
        band_ref[:TA] = jnp.where(lane == GROUP, 1.0, 0.0).astype(band_ref.dtype)
        band_ref[TA:, GROUP:] = jnp.zeros((TA, GROUP), band_ref.dtype)

    @pl.when(i > 0)
    def _():
        band_ref[:TA] = band_ref[TA:]

    band_ref[TA:, :GROUP] = new_ref[...].astype(band_ref.dtype)


def _shift_band_t(i, band_ref, new_ref):
    @pl.when(i == 0)
    def _():
        band_ref[:, :TA] = jnp.zeros((GROUP, TA), band_ref.dtype)

    @pl.when(i > 0)
    def _():
        band_ref[:, :TA] = band_ref[:, TA:]

    band_ref[:, TA:] = new_ref[...].astype(band_ref.dtype).T


def _scaled_queries(q_ref, rows, scale=SCALE):
    return _by_head((q_ref[rows] * scale).astype(BF16))


def _attn_fwd(proj_bf, bias_table):
    seq = proj_bf.shape[0]
    n_tiles = seq // TA

    def body(q_ref, k_ref, v_ref, ag_ref, table_ref, a_ref, ya_ref, e_ref, inv_ref, kband, vband_t, bias_ref):
        i = pl.program_id(1)
        _shift_key_band(i, kband, k_ref)
        _shift_band_t(i, vband_t, v_ref)

        @pl.when(i == 0)
        def _():
            bias_ref[...] = _bias_tile(table_ref)

        def weights(sc):
            rows = slice(sc * SUPER, (sc + 1) * SUPER)
            win = _window(sc)
            e, total = _band_exp(kband[win], _scaled_queries(q_ref, rows, SCALE * LOG2E), bias_ref[...])
            e_ref[0, sc] = e.astype(BF16)
            inv_total = 1.0 / total
            inv_ref[0, sc] = jnp.broadcast_to(inv_total, (8, PAIR_LANES))
            return inv_total

        nxt = weights(0)
        for sc in range(SUPERS_PER_TILE):
            rows = slice(sc * SUPER, (sc + 1) * SUPER)
            win = _window(sc)
            inv_total = nxt
            if sc + 1 < SUPERS_PER_TILE:
                nxt = weights(sc + 1)
            a = _own_head_rows(_dot(vband_t[:, win], e_ref[0, sc]) * inv_total)
            a_ref[rows] = a.astype(BF16)
            g = ag_ref[rows].astype(F32)
            ya_ref[rows] = (a * (g * _sigmoid(g))).astype(BF16)

    blk = pl.BlockSpec((TA, GROUP), lambda p, i: (i, p))

    def cols(first):
        return pl.BlockSpec((TA, GROUP), lambda p, i: (i, first + p))

    return pl.pallas_call(
        body,
        name="attn_fwd",
        grid=(N_PAIR, n_tiles),
        in_specs=[cols(Q_BLOCK), cols(K_BLOCK), cols(V_BLOCK), cols(AG_BLOCK),
                  pl.BlockSpec((1, 2, BIAS_TABLE), lambda p, i: (p, 0, 0))],
        out_specs=[blk, blk,
                   pl.BlockSpec((1, SUPERS_PER_TILE, WINDOW, PAIR_LANES), lambda p, i: (p, i, 0, 0)),
                   pl.BlockSpec((1, SUPERS_PER_TILE, 8, PAIR_LANES), lambda p, i: (p, i, 0, 0))],
        out_shape=[jax.ShapeDtypeStruct((seq, ATTN_WIDTH), BF16), jax.ShapeDtypeStruct((seq, ATTN_WIDTH), BF16),
                   jax.ShapeDtypeStruct((N_PAIR, seq // SUPER, WINDOW, PAIR_LANES), BF16),
                   jax.ShapeDtypeStruct((N_PAIR, seq // SUPER, 8, PAIR_LANES), F32)],
        scratch_shapes=[pltpu.VMEM((2 * TA, 2 * GROUP), BF16), pltpu.VMEM((GROUP, 2 * TA), BF16),
                        pltpu.VMEM((WINDOW, PAIR_LANES), F32)],
        compiler_params=pltpu.CompilerParams(
            dimension_semantics=("arbitrary", "arbitrary"), vmem_limit_bytes=VMEM_LIMIT),
    )(proj_bf, proj_bf, proj_bf, proj_bf, bias_table)


BIN_ROWS = 136


def _bias_bin_sums(db_ref, idx_ref):
    lane = lax.broadcasted_iota(jnp.int32, (1, GROUP), 1)
    row = lax.broadcasted_iota(jnp.int32, (BIN_ROWS, GROUP), 0)
    out = jnp.zeros((BIN_ROWS, GROUP), F32)
    for r in range(N_REL - 1):
        lo = 0 if r == 0 else ((BAND - 2 * CHUNK + r) // 8) * 8
        hi = WINDOW if r == 0 else min(WINDOW, lo + SUPER + 8)
        hit = jnp.where(idx_ref[lo:hi] == r, db_ref[lo:hi], 0.0)
        col = jnp.sum(hit, axis=0, keepdims=True)
        s0 = jnp.sum(col[:, :SUPER], axis=1, keepdims=True)
        s1 = jnp.sum(col[:, SUPER:], axis=1, keepdims=True)
        val = jnp.where(lane == 0, s0, jnp.where(lane == 1, s1, 0.0))
        out = jnp.where(row == r, val, out)
    return out


def _attn_bwd(proj_bf, a, dy, e_all, inv_all):
    seq = proj_bf.shape[0]
    n_tiles = seq // TA

    def body(q_ref, k_ref, v_ref, a_ref, ag_ref, dy_ref, e_ref, inv_ref, idx_ref,
             dq_ref, dk_ref, dv_ref, dag_ref, bins_ref, vband, kband_t, dkacc, dvacc, db_ref):
        i = pl.program_id(1)

        @pl.when(i == 0)
        def _():
            dkacc[...] = jnp.zeros_like(dkacc)
            dvacc[...] = jnp.zeros_like(dvacc)
            db_ref[...] = jnp.zeros_like(db_ref)

        @pl.when(i < n_tiles)
        def _():
            _shift_band(i, vband, v_ref)
            _shift_band_t(i, kband_t, k_ref)

            def score_grads(sc):
                rows = slice(sc * SUPER, (sc + 1) * SUPER)
                win = _window(sc)
                q_rows = _scaled_queries(q_ref, rows)
                g = ag_ref[rows].astype(F32)
                sg = _sigmoid(g)
                dyc = dy_ref[rows].astype(F32)
                dag_ref[rows] = (dyc * a_ref[rows].astype(F32) * (sg * (1.0 + g * (1.0 - sg)))).astype(BF16)
                da_rows = _by_head((dyc * (g * sg)).astype(BF16))
                p = e_ref[0, sc].astype(F32) * inv_ref[0, sc, :1]
                dp = _dot_t(vband[win], da_rows)
                ds = p * (dp - jnp.sum(p * dp, axis=0, keepdims=True))
                db_ref[...] += ds
                return q_rows, da_rows, p.astype(BF16), ds.astype(BF16)

            nxt = score_grads(0)
            for sc in range(SUPERS_PER_TILE):
                rows = slice(sc * SUPER, (sc + 1) * SUPER)
                win = _window(sc)
                q_rows, da_rows, p_bf, ds_bf = nxt
                if sc + 1 < SUPERS_PER_TILE:
                    nxt = score_grads(sc + 1)
                dq_ref[rows] = (_own_head_rows(_dot(kband_t[:, win], ds_bf)) * SCALE).astype(BF16)
                dkacc[win] += _dot(ds_bf, q_rows)
                dvacc[win] += _dot(p_bf, da_rows)

        dk_ref[...] = dkacc[:TA].astype(BF16)
        dv_ref[...] = dvacc[:TA].astype(BF16)
        dkacc[:TA] = dkacc[TA:]
        dvacc[:TA] = dvacc[TA:]
        dkacc[TA:] = jnp.zeros((TA, GROUP), F32)
        dvacc[TA:] = jnp.zeros((TA, GROUP), F32)

        @pl.when(i == n_tiles)
        def _():
            bins_ref[0] = _bias_bin_sums(db_ref, idx_ref)

    last = n_tiles - 1
    cur = pl.BlockSpec((TA, GROUP), lambda p, i: (jnp.minimum(i, last), p))
    older = pl.BlockSpec((TA, GROUP), lambda p, i: (jnp.maximum(i - 1, 0), p))
    dy_blk = pl.BlockSpec((TA, GROUP), lambda p, i: (jnp.minimum(i, last), N_PAIR + p))
    per_pair = pl.BlockSpec((1, BIN_ROWS, GROUP), lambda p, i: (p, 0, 0))

    def cols(first):
        return pl.BlockSpec((TA, GROUP), lambda p, i: (jnp.minimum(i, last), first + p))

    def kept(rows):
        return pl.BlockSpec((1, SUPERS_PER_TILE, rows, PAIR_LANES), lambda p, i: (p, jnp.minimum(i, last), 0, 0))

    def out(dtype):
        return jax.ShapeDtypeStruct((seq, ATTN_WIDTH), dtype)

    return pl.pallas_call(
        body,
        name="attn_bwd",
        grid=(N_PAIR, n_tiles + 1),
        in_specs=[cols(Q_BLOCK), cols(K_BLOCK), cols(V_BLOCK), cur, cols(AG_BLOCK), dy_blk, kept(WINDOW), kept(8),
                  VMEM],
        out_specs=[cur, older, older, cur, per_pair],
        out_shape=[out(BF16), out(BF16), out(BF16), out(BF16),
                   jax.ShapeDtypeStruct((N_PAIR, BIN_ROWS, GROUP), F32)],
        scratch_shapes=[
            pltpu.VMEM((2 * TA, GROUP), BF16), pltpu.VMEM((GROUP, 2 * TA), BF16),
            pltpu.VMEM((2 * TA, GROUP), F32), pltpu.VMEM((2 * TA, GROUP), F32),
            pltpu.VMEM((WINDOW, PAIR_LANES), F32)],
        compiler_params=pltpu.CompilerParams(
            dimension_semantics=("arbitrary", "arbitrary"), vmem_limit_bytes=VMEM_LIMIT),
    )(proj_bf, proj_bf, proj_bf, a, proj_bf, dy, e_all, inv_all, jnp.asarray(_rel_index_tile()))


def _out_loss(x, tgt, proj, ya, wout_full, g2, pw_bf, ps):
    seq = x.shape[0]
    to = min(seq, 2 * TS)
    n_tiles = seq // to

    def body(x_ref, t_ref, pv_ref, pg_ref, ya_ref, w_ref, g2_ref, pw_ref, ps_ref,
             dx2_ref, dy_ref, gw_ref, gg_ref, loss_ref, sq_ref, halo_ref, wt_ref):
        i = pl.program_id(0)

        @pl.when(i == 0)
        def _():
            gw_ref[...] = jnp.zeros_like(gw_ref)
            gg_ref[...] = jnp.zeros_like(gg_ref)
            sq_ref[...] = jnp.zeros_like(sq_ref)
            halo_ref[...] = jnp.zeros_like(halo_ref)
            for b in range(N_CHIP):
                wt_ref[b] = w_ref[b].T

        pv = pv_ref[...].astype(F32)
        pg = pg_ref[...].astype(F32)
        diffs = [d.astype(BF16) for d in _pool_diffs(pv, halo_ref[...], i)]
        halo_ref[...] = pv[to - HALO:, :]
        yp = ((_pool_mix(diffs, pw_ref) * ps_ref[...]) * (pg * _sigmoid(pg))).astype(BF16)
        g2v = g2_ref[...]

        def parts(rows):
            return [yp[rows, :SHARD_OUT], yp[rows, SHARD_OUT:], ya_ref[rows, :SHARD_OUT], ya_ref[rows, SHARD_OUT:]]

        def project(rows, ys):
            x2 = x_ref[rows]
            for b in range(N_CHIP):
                x2 = x2 + _dot(ys[b], w_ref[b])
            return x2

        def norm_loss(rows, x2):
            r = lax.rsqrt(jnp.mean(x2 * x2, axis=-1, keepdims=True) + EPS)
            xh = x2 * r
            diff = xh * g2v - t_ref[rows]
            sq_ref[...] += jnp.sum(diff * diff, axis=0, keepdims=True)
            dfin = diff * (1.0 / D_MODEL)
            gg_ref[...] += jnp.sum(dfin * xh, axis=0, keepdims=True)
            dxh = dfin * g2v
            dx2 = r * (dxh - xh * jnp.mean(dxh * xh, axis=-1, keepdims=True))
            dx2_ref[rows] = dx2
            return dx2.astype(BF16)

        def back(rows, ys, dx2_bf):
            for b in range(N_CHIP):
                gw_ref[b] += _tdot(ys[b], dx2_bf)
                dy_ref[rows, b * SHARD_OUT:(b + 1) * SHARD_OUT] = _dot(dx2_bf, wt_ref[b]).astype(BF16)

        n_parts = 2
        part = to // n_parts
        spans = [slice(r * part, (r + 1) * part) for r in range(n_parts)]
        ys = [parts(rows) for rows in spans]
        x2_next = project(spans[0], ys[0])
        for r in range(n_parts):
            x2 = x2_next
            if r + 1 < n_parts:
                x2_next = project(spans[r + 1], ys[r + 1])
            back(spans[r], ys[r], norm_loss(spans[r], x2))

        @pl.when(i == n_tiles - 1)
        def _():
            total = jnp.sum(sq_ref[...], axis=1, keepdims=True) * (0.5 / D_MODEL)
            loss_ref[...] = jnp.broadcast_to(total, loss_ref.shape)

    def rows(width, col=0):
        return pl.BlockSpec((to, width), lambda i: (i, col))

    return pl.pallas_call(
        body,
        name="out_loss",
        grid=(n_tiles,),
        in_specs=[rows(D_MODEL), rows(D_MODEL), rows(POOL_WIDTH, 0), rows(POOL_WIDTH, 1), rows(ATTN_WIDTH),
                  VMEM, VMEM, VMEM, VMEM],
        out_specs=[rows(D_MODEL), rows(D_MODEL), VMEM, VMEM, VMEM],
        out_shape=[
            jax.ShapeDtypeStruct((seq, D_MODEL), F32), jax.ShapeDtypeStruct((seq, D_MODEL), BF16),
            jax.ShapeDtypeStruct((N_CHIP, SHARD_OUT, D_MODEL), F32), jax.ShapeDtypeStruct((1, D_MODEL), F32),
            jax.ShapeDtypeStruct((8, GROUP), F32)],
        scratch_shapes=[pltpu.VMEM((1, D_MODEL), F32), pltpu.VMEM((HALO, POOL_WIDTH), F32),
                        pltpu.VMEM((N_CHIP, D_MODEL, SHARD_OUT), BF16)],
        compiler_params=pltpu.CompilerParams(dimension_semantics=("arbitrary",), vmem_limit_bytes=VMEM_LIMIT),
    )(x, tgt, proj, proj, ya, wout_full, g2, pw_bf, ps)


def _inproj_bwd(x, dx2, dy, proj, dq, dk, dv, dag, win_full, pw_bf, g1, ps):
    seq = x.shape[0]
    n_tiles = seq // TS

    def body(x_ref, dx2_ref, dyp_ref, pv_ref, pvprev_ref, pg_ref, dq_ref, dk_ref, dv_ref, dag_ref,
             w_ref, pw_ref, g1_ref, ps_ref, gx_ref, dproj_ref, ht_ref, gg_ref, gps_ref, gpw_ref, halo_ref):
        i = pl.program_id(0)
        tile = n_tiles - 1 - i

        @pl.when(i == 0)
        def _():
            gg_ref[...] = jnp.zeros_like(gg_ref)
            gps_ref[...] = jnp.zeros_like(gps_ref)
            gpw_ref[...] = jnp.zeros_like(gpw_ref)
            halo_ref[...] = jnp.zeros_like(halo_ref)

        pv_t = pv_ref[...].astype(F32)
        pg_t = pg_ref[...].astype(F32)
        prev_rows = jnp.where(tile > 0, pvprev_ref[...].astype(F32), 0.0)
        diffs = [d.astype(BF16) for d in _pool_diffs(pv_t, prev_rows, tile)]
        mixed = _pool_mix(diffs, pw_ref)
        sg = _sigmoid(pg_t)
        silu = pg_t * sg
        dyp = dyp_ref[...].astype(F32)
        psv = ps_ref[...]
        gps_ref[...] += jnp.sum(dyp * mixed * silu, axis=0, keepdims=True)
        dmixed = (dyp * psv * silu).astype(BF16)
        dpg = dyp * (mixed * psv) * (sg * (1.0 + pg_t * (1.0 - sg)))
        counts = _row_counts(tile, TS)
        dds = []
        for g in range(len(POOL_WINDOWS)):
            dm_g = dmixed[:, g * GROUP:(g + 1) * GROUP]
            gpw_ref[g] += _tdot(diffs[g], dm_g)
            dds.append(_dot_t(dm_g, pw_ref[g]))
        dd = jnp.concatenate(dds, axis=1)
        spread = jnp.concatenate([dds[g] / counts[g] for g in range(len(POOL_WINDOWS))], axis=1)
        sums = _window_sums(jnp.concatenate([spread, halo_ref[...]], axis=0), forward=True)
        halo_ref[...] = spread[:HALO]
        dpv = jnp.concatenate(
            [sums[g][:TS, g * GROUP:(g + 1) * GROUP] for g in range(len(POOL_WINDOWS))], axis=1) - dd

        xt = x_ref[...]
        r = lax.rsqrt(jnp.mean(xt * xt, axis=-1, keepdims=True) + EPS)
        xh = xt * r
        g1v = g1_ref[...]
        ht_ref[...] = (xh * g1v).astype(BF16).T
        dproj = jnp.concatenate(
            [dpv.astype(BF16), dpg.astype(BF16), dq_ref[...], dk_ref[...], dv_ref[...], dag_ref[...]],
            axis=1)
        dproj_ref[...] = dproj
        dh = _dot_t(dproj[:, :SHARD_IN], w_ref[0])
        for chip in range(1, N_CHIP):
            dh = dh + _dot_t(dproj[:, chip * SHARD_IN:(chip + 1) * SHARD_IN], w_ref[chip])

        gg_ref[...] += jnp.sum(dh * xh, axis=0, keepdims=True)
        dxh = dh * g1v
        gx_ref[...] = dx2_ref[...] + r * (dxh - xh * jnp.mean(dxh * xh, axis=-1, keepdims=True))

    def rows(width, col=0):
        return pl.BlockSpec((TS, width), lambda i: (n_tiles - 1 - i, col))

    prev = pl.BlockSpec((HALO, POOL_WIDTH), lambda i: (jnp.maximum((n_tiles - 1 - i) * (TS // HALO) - 1, 0), 0))
    return pl.pallas_call(
        body,
        name="inproj_bwd",
        grid=(n_tiles,),
        in_specs=[rows(D_MODEL), rows(D_MODEL), rows(POOL_WIDTH), rows(POOL_WIDTH), prev, rows(POOL_WIDTH, 1),
                  rows(ATTN_WIDTH), rows(ATTN_WIDTH), rows(ATTN_WIDTH), rows(ATTN_WIDTH), VMEM, VMEM, VMEM, VMEM],
        out_specs=[rows(D_MODEL), rows(N_CHIP * SHARD_IN),
                   pl.BlockSpec((D_MODEL, TS), lambda i: (0, n_tiles - 1 - i)), VMEM, VMEM, VMEM],
        out_shape=[
            jax.ShapeDtypeStruct((seq, D_MODEL), F32),
            jax.ShapeDtypeStruct((seq, N_CHIP * SHARD_IN), BF16),
            jax.ShapeDtypeStruct((D_MODEL, seq), BF16),
            jax.ShapeDtypeStruct((1, D_MODEL), F32),
            jax.ShapeDtypeStruct((1, POOL_WIDTH), F32),
            jax.ShapeDtypeStruct((len(POOL_WINDOWS), GROUP, GROUP), F32)],
        scratch_shapes=[pltpu.VMEM((HALO, POOL_WIDTH), F32)],
        compiler_params=pltpu.CompilerParams(dimension_semantics=("arbitrary",), vmem_limit_bytes=VMEM_LIMIT),
    )(x, dx2, dy, proj, proj, proj, dq, dk, dv, dag, win_full, pw_bf, g1, ps)


PASS_PEER = (2, 0, 1)


def _pack_small(dst, norm_ref, pool_w_ref, pool_scale_ref, bins_ref, final_ref, loss_ref):
    def lane_group(r):
        return slice(r * GROUP, (r + 1) * GROUP)

    dst[...] = jnp.zeros(dst.shape, F32)
    for r in range(D_MODEL // GROUP):
        dst[r:r + 1] = norm_ref[:, lane_group(r)]
        dst[FINAL_GAIN_ROW + r:FINAL_GAIN_ROW + r + 1] = final_ref[:, lane_group(r)]
    for g in range(len(POOL_WINDOWS)):
        dst[POOL_W_ROW + g * GROUP:POOL_W_ROW + (g + 1) * GROUP] = pool_w_ref[g]
    for r in range(POOL_WIDTH // GROUP):
        dst[POOL_SCALE_ROW + r:POOL_SCALE_ROW + r + 1] = pool_scale_ref[:, lane_group(r)]
    first_lane = lax.broadcasted_iota(jnp.int32, (BIN_ROWS - GROUP, GROUP), 1) == 0
    for p in range(N_PAIR):
        by_head = bins_ref[p, :GROUP].T
        last_column = bins_ref[p, GROUP:]
        for h in range(2):
            row = REL_BIAS_ROW + REL_BIAS_ROWS_PER_HEAD * (2 * p + h)
            dst[row:row + 1] = by_head[h:h + 1]
            to_first_lane = last_column if h == 0 else pltpu.roll(last_column, GROUP - h, 1)
            dst[row + 1:row + 2] = jnp.where(first_lane, to_first_lane, 0.0)[:1]
    dst[LOSS_ROW:LOSS_ROW + loss_ref.shape[0]] = loss_ref[...]


def _gw_reduce(ht, dproj, gwout, small_parts):
    seq = ht.shape[1]
    tm = min(2 * TS, seq // 4)
    n_tiles = seq // tm
    half_small = SMALL_ROWS // 2
    cx, cy = lax.axis_index("x"), lax.axis_index("y")
    outer = _other_chips(cx, cy)
    order = jnp.stack([outer[n][1] for n in PASS_PEER] + [2 * cx + cy]).astype(jnp.int32)

    def body(order_ref, ht_ref, dp_ref, gwout_ref, *refs):
        part_refs = refs[:len(small_parts)]
        (gin_final, gout_final, small_final, hbuf, acc, pair_in, pair_out, small_ref, pair_small, tx_in, tx_out, rx_in, rx_out,
         rx_small, gin_out, gout_out, small_out, send_sems, recv_sems, out_sems) = refs[len(small_parts):]
        j = pl.program_id(0)
        i = pl.program_id(1)
        x, y, c = _my_place()
        b = 2 * x + y
        sibling = (x, y, 1 - c)
        others = _other_chips(x, y)
        mine_in = pl.ds(pl.multiple_of(c * HALF_IN, HALF_IN), HALF_IN)
        mine_out = pl.ds(pl.multiple_of(c * HALF_OUT, HALF_OUT), HALF_OUT)
        mine_small = pl.ds(pl.multiple_of(c * half_small, 8), half_small)
        theirs_in = pl.ds(pl.multiple_of((1 - c) * HALF_IN, HALF_IN), HALF_IN)
        theirs_out = pl.ds(pl.multiple_of((1 - c) * HALF_OUT, HALF_OUT), HALF_OUT)
        theirs_small = pl.ds(pl.multiple_of((1 - c) * half_small, 8), half_small)

        def copy(k, src, dst, to):
            return pltpu.make_async_remote_copy(
                src_ref=src, dst_ref=dst, send_sem=send_sems.at[k], recv_sem=recv_sems.at[k],
                device_id=to, device_id_type=MESH)

        swap_out = copy(0, gwout_ref.at[:, theirs_out], pair_out, sibling)
        swap_small = copy(1, small_ref, pair_small, sibling)

        def swap_in(p):
            return copy(2 + p, acc.at[p % 2, theirs_in], pair_in.at[p % 2], sibling)

        def to_chip(n, t):
            to = (*others[n][0], c)
            if t == 0:
                return copy(6 + 3 * n, tx_in.at[n], rx_in.at[n], to)
            if t == 1:
                return copy(7 + 3 * n, tx_out.at[n], rx_out.at[n], to)
            return copy(8 + 3 * n, pair_small.at[mine_small], rx_small.at[b], to)

        share_in = copy(15, gin_out.at[mine_in], gin_out.at[mine_in], sibling)
        share_out = copy(16, gout_out.at[mine_out], gout_out.at[mine_out], sibling)
        share_small = copy(17, small_out.at[mine_small], small_out.at[mine_small], sibling)

        def at(jj, ii):
            return (j == jj) & (i == ii)

        par = j % 2

        @pl.when(i == 0)
        def _():
            acc[par] = jnp.zeros((D_MODEL, SHARD_IN), F32)

        cols = pl.ds(pl.multiple_of(i * tm, tm), tm)

        @pl.when(j == 0)
        def _():
            hbuf[:, cols] = ht_ref[...]

        acc[par] += _dot(hbuf[:, cols], dp_ref[...])

        @pl.when(at(0, 0))
        def _():
            swap_out.start()
            _pack_small(small_ref, *part_refs)
            swap_small.start()

        @pl.when(at(0, 2))
        def _():
            swap_out.wait_recv()
            swap_small.wait_recv()
            for chip in range(N_CHIP):
                pair_out[chip] = gwout_ref[chip, mine_out] + pair_out[chip]
            pair_small[...] = small_ref[...] + pair_small[...]
            rx_small[b] = pair_small[mine_small]
            for n in range(N_CHIP - 1):
                tx_out[n] = pair_out[others[n][1]].astype(BF16)
                to_chip(n, 1).start()
                to_chip(n, 2).start()

        @pl.when(at(1, 3))
        def _():
            total_out = pair_out[b]
            for n in range(N_CHIP - 1):
                to_chip(n, 1).wait_recv()
                copy(8 + 3 * n, pair_small.at[mine_small], rx_small.at[others[n][1]], (*others[n][0], c)).wait_recv()
                total_out = total_out + rx_out[n].astype(F32)
            gout_out[mine_out] = total_out
            small_out[mine_small] = ((rx_small[0] + rx_small[1]) + rx_small[2]) + rx_small[3]
            share_out.start()
            share_small.start()

        for p in range(N_CHIP - 1):
            n = PASS_PEER[p]

            @pl.when(at(p + 1, 0))
            def _(p=p):
                swap_in(p).start()

            @pl.when(at(p + 1, 2))
            def _(p=p, n=n):
                swap_in(p).wait_recv()
                swap_in(p).wait_send()
                tx_in[n] = (acc[p % 2, mine_in] + pair_in[p % 2]).astype(BF16)
                to_chip(n, 0).start()

        @pl.when(at(N_CHIP - 1, n_tiles - 1))
        def _():
            last = N_CHIP - 1
            swap_in(last).start()
            swap_in(last).wait_recv()
            total_in = acc[last % 2, mine_in] + pair_in[last % 2]
            for n in range(N_CHIP - 1):
                to_chip(n, 0).wait_recv()
                total_in = total_in + rx_in[n].astype(F32)
            gin_out[mine_in] = total_in
            share_in.start()
            copy(15, gin_out.at[theirs_in], gin_out.at[theirs_in], sibling).wait_recv()
            copy(16, gout_out.at[theirs_out], gout_out.at[theirs_out], sibling).wait_recv()
            copy(17, small_out.at[theirs_small], small_out.at[theirs_small], sibling).wait_recv()
            swap_out.wait_send()
            swap_small.wait_send()
            swap_in(last).wait_send()
            for n in range(N_CHIP - 1):
                for t in range(3):
                    to_chip(n, t).wait_send()
            share_in.wait_send()
            share_out.wait_send()
            share_small.wait_send()
            outs = [pltpu.make_async_copy(src, dst, out_sems.at[k]) for k, (src, dst) in enumerate(
                [(gin_out, gin_final), (gout_out, gout_final), (small_out, small_final)])]
            for cp in outs:
                cp.start()
            for cp in outs:
                cp.wait()

    assert n_tiles >= 4, "the reduction's steps are spread over the first four token steps of a pass"
    grid_spec = pltpu.PrefetchScalarGridSpec(
        num_scalar_prefetch=1,
        grid=(N_CHIP, n_tiles),
        in_specs=[
            pl.BlockSpec((D_MODEL, tm), lambda j, i, o: (0, jnp.where(j == 0, i, n_tiles - 1))),
            pl.BlockSpec((tm, SHARD_IN), lambda j, i, o: (i, o[j])),
            VMEM] + [VMEM] * len(small_parts),
        out_specs=[ANY, ANY, ANY],
        scratch_shapes=[
            pltpu.VMEM((D_MODEL, seq), BF16),
            pltpu.VMEM((2, D_MODEL, SHARD_IN), F32),
            pltpu.VMEM((2, HALF_IN, SHARD_IN), F32),
            pltpu.VMEM((N_CHIP, HALF_OUT, D_MODEL), F32),
            pltpu.VMEM((SMALL_ROWS, GROUP), F32),
            pltpu.VMEM((SMALL_ROWS, GROUP), F32),
            pltpu.VMEM((N_CHIP - 1, HALF_IN, SHARD_IN), BF16),
            pltpu.VMEM((N_CHIP - 1, HALF_OUT, D_MODEL), BF16),
            pltpu.VMEM((N_CHIP - 1, HALF_IN, SHARD_IN), BF16),
            pltpu.VMEM((N_CHIP - 1, HALF_OUT, D_MODEL), BF16),
            pltpu.VMEM((N_CHIP, half_small, GROUP), F32),
            pltpu.VMEM((D_MODEL, SHARD_IN), F32),
            pltpu.VMEM((SHARD_OUT, D_MODEL), F32),
            pltpu.VMEM((SMALL_ROWS, GROUP), F32),
            pltpu.SemaphoreType.DMA((18,)),
            pltpu.SemaphoreType.DMA((18,)),
            pltpu.SemaphoreType.DMA((3,)),
        ],
    )
    return pl.pallas_call(
        body,
        name="gw_reduce",
        grid_spec=grid_spec,
        out_shape=(
            jax.ShapeDtypeStruct((D_MODEL, SHARD_IN), F32),
            jax.ShapeDtypeStruct((SHARD_OUT, D_MODEL), F32),
            jax.ShapeDtypeStruct((SMALL_ROWS, GROUP), F32),
        ),
        compiler_params=pltpu.CompilerParams(
            dimension_semantics=("arbitrary", "arbitrary"), vmem_limit_bytes=VMEM_LIMIT),
    )(order, ht, dproj, gwout, *small_parts)


def _adam_update(w, grad, m, v):
    m_new = ADAM_B1 * m + (1.0 - ADAM_B1) * grad
    v_new = ADAM_B2 * v + (1.0 - ADAM_B2) * (grad * grad)
    m_hat = m_new / (1.0 - ADAM_B1 ** ADAM_STEP)
    v_hat = v_new / (1.0 - ADAM_B2 ** ADAM_STEP)
    delta = -ADAM_LR * (m_hat / (jnp.sqrt(v_hat) + ADAM_EPS) + ADAM_WD * w)
    return delta, m_new, v_new


def _adamw(name, w, g, m, v, block_rows):
    rows, cols = w.shape

    def body(w_ref, g_ref, m_ref, v_ref, d_ref, m_out, v_out):
        d_ref[...], m_out[...], v_out[...] = _adam_update(w_ref[...], g_ref[...], m_ref[...], v_ref[...])

    blk = pl.BlockSpec((block_rows, cols), lambda i: (i, 0))
    shape = jax.ShapeDtypeStruct((rows, cols), F32)
    return pl.pallas_call(
        body,
        name=name,
        grid=(rows // block_rows,),
        in_specs=[blk] * 4,
        out_specs=[blk] * 3,
        out_shape=[shape] * 3,
        compiler_params=pltpu.CompilerParams(dimension_semantics=("arbitrary",)),
    )(w, g, m, v)


def _adamw_small(g_small, params, m_state, v_state):
    n_param = len(params)
    n_head = params[3].shape[0]

    def body(g_ref, *refs):
        w_refs, m_refs, v_refs = (refs[k * n_param:(k + 1) * n_param] for k in range(3))
        g_outs, d_outs, m_outs, v_outs = (refs[k * n_param:(k + 1) * n_param] for k in range(3, 7))

        def update(i, packed_rows, rows, cols, packed_cols=slice(None)):
            grad = g_ref[packed_rows, packed_cols]
            g_outs[i][rows, cols] = grad
            d_outs[i][rows, cols], m_outs[i][rows, cols], v_outs[i][rows, cols] = _adam_update(
                w_refs[i][rows, cols], grad, m_refs[i][rows, cols], v_refs[i][rows, cols])

        def lane_group(r):
            return slice(r * GROUP, (r + 1) * GROUP)

        everything = slice(None)
        for r in range(D_MODEL // GROUP):
            update(0, slice(r, r + 1), everything, lane_group(r))
            update(4, slice(FINAL_GAIN_ROW + r, FINAL_GAIN_ROW + r + 1), everything, lane_group(r))
        update(1, slice(POOL_W_ROW, POOL_W_ROW + len(POOL_WINDOWS) * GROUP), everything, everything)
        for r in range(POOL_WIDTH // GROUP):
            update(2, slice(POOL_SCALE_ROW + r, POOL_SCALE_ROW + r + 1), everything, lane_group(r))
        for h in range(n_head):
            first = REL_BIAS_ROW + REL_BIAS_ROWS_PER_HEAD * h
            update(3, slice(first, first + 1), slice(h, h + 1), slice(0, GROUP))
            update(3, slice(first + 1, first + 2), slice(h, h + 1), slice(GROUP, N_REL), slice(0, N_REL - GROUP))

    shapes = [jax.ShapeDtypeStruct(p.shape, F32) for p in params]
    outs = pl.pallas_call(
        body,
        name="adamw_small",
        out_shape=shapes * 4,
    )(g_small, *params, *m_state, *v_state)
    return [outs[k * n_param:(k + 1) * n_param] for k in range(4)]


def kernel(x, norm_gain, w_in, pool_w, pool_scale, rel_bias, w_out, final_norm_gain, loss_target, m_norm_gain, m_w_in, m_pool_w, m_pool_scale, m_rel_bias, m_w_out, m_final_norm_gain, v_norm_gain, v_w_in, v_pool_w, v_pool_scale, v_rel_bias, v_w_out, v_final_norm_gain):
    assert x.shape[1] % TS == 0 and x.shape[2] == D_MODEL
    xs = x[0]
    tgt = loss_target[0]
    g1 = norm_gain.reshape(1, D_MODEL)
    g2 = final_norm_gain.reshape(1, D_MODEL)
    ps = pool_scale.reshape(1, POOL_WIDTH)
    pw_bf = pool_w[0].astype(BF16)

    proj, win_full, wout_full = _gather_inproj(xs, g1, w_in[0].astype(BF16), w_out[0].astype(BF16))
    bias_table = _bias_table(rel_bias[0])

    a, ya, e_all, inv_all = _attn_fwd(proj, bias_table)
    dx2, dy, gwout, gg2, loss_rows = _out_loss(xs, tgt, proj, ya, wout_full, g2, pw_bf, ps)
    dq, dk, dv, dag, bins = _attn_bwd(proj, a, dy, e_all, inv_all)
    gx, dproj, ht, gg1, gps, gpw = _inproj_bwd(xs, dx2, dy, proj, dq, dk, dv, dag, win_full, pw_bf, g1, ps)

    g_win, g_wout, g_small = _gw_reduce(ht, dproj, gwout, (gg1, gpw, gps, bins, gg2, loss_rows))
    loss = g_small[LOSS_ROW, 0]

    def small_views(norm, pool, scale, bias, final):
        return [norm.reshape(1, D_MODEL), pool.reshape(len(POOL_WINDOWS) * GROUP, GROUP), scale.reshape(1, POOL_WIDTH),
                bias.reshape(-1, N_REL), final.reshape(1, D_MODEL)]

    d_win, m_win, v_win = _adamw("adamw_w_in", w_in[0], g_win, m_w_in[0], v_w_in[0], 256)
    d_wout, m_wout, v_wout = _adamw("adamw_w_out", w_out[0], g_wout, m_w_out[0], v_w_out[0], 128)
    small_results = _adamw_small(
        g_small,
        small_views(norm_gain, pool_w, pool_scale, rel_bias, final_norm_gain),
        small_views(m_norm_gain, m_pool_w, m_pool_scale, m_rel_bias, m_final_norm_gain),
        small_views(v_norm_gain, v_pool_w, v_pool_scale, v_rel_bias, v_final_norm_gain))

    def full(win_part, wout_part, small_parts):
        norm, pool, scale, bias, final = small_parts
        return [norm.reshape(norm_gain.shape), win_part[None], pool.reshape(pool_w.shape), scale.reshape(pool_scale.shape),
                bias.reshape(rel_bias.shape), wout_part[None], final.reshape(final_norm_gain.shape)]

    grads = full(g_win, g_wout, small_results[0])
    deltas = full(d_win, d_wout, small_results[1])
    new_m = full(m_win, m_wout, small_results[2])
    new_v = full(v_win, v_wout, small_results[3])
    return (loss, gx[None], *grads, *deltas, *new_m, *new_v)
```

```python
import numpy as np
import jax
import jax.numpy as jnp
from jax import lax
from jax.experimental import pallas as pl
from jax.experimental.pallas import tpu as pltpu

F32 = jnp.float32
BF16 = jnp.bfloat16

D_MODEL = 1024
POOL_WIDTH = 512
ATTN_WIDTH = 512
POOL_WINDOWS = (2, 4, 8, 16)
GROUP = 128
CHUNK = 64
LEFT_CHUNKS = 8
BAND = (LEFT_CHUNKS + 1) * CHUNK
HEAD_DIM = 64
N_PAIR = 4
MAX_REL = 64
N_REL = 2 * MAX_REL + 1
EPS = 1e-6
MASK_VALUE = -1e30
SCALE = 0.125
LOG2E = 1.4426950408889634

ADAM_LR = 0.001
ADAM_B1 = 0.9
ADAM_B2 = 0.999
ADAM_EPS = 1e-08
ADAM_WD = 0.01
ADAM_STEP = 10

TS = LEFT_CHUNKS * CHUNK
SUPER = 2 * CHUNK
WINDOW = BAND + CHUNK
TA = 4 * TS
WIN_BASE = TA - LEFT_CHUNKS * CHUNK
SUPERS_PER_TILE = TA // SUPER
PAIR_LANES = 2 * SUPER
HALO = 16
N_CHIP = 4
SHARD_IN = 768
SHARD_OUT = 256
PIECE = 256
PIECES_PER_SHARD = SHARD_IN // PIECE
HALF_IN = D_MODEL // 2
HALF_OUT = SHARD_OUT // 2
SMALL_ROWS = 560
POOL_W_ROW = 8
POOL_SCALE_ROW = 520
REL_BIAS_ROW = 528
REL_BIAS_ROWS_PER_HEAD = 2
FINAL_GAIN_ROW = 544
LOSS_ROW = 552
VMEM_LIMIT = 60 * 1024 * 1024

MESH = pl.DeviceIdType.MESH
ANY = pl.BlockSpec(memory_space=pl.ANY)
VMEM = pl.BlockSpec(memory_space=pltpu.VMEM)


def _sigmoid(x):
    return 1.0 / (1.0 + jnp.exp(-x))


def _dot(a, b):
    return jnp.dot(a, b, preferred_element_type=F32)


def _dot_t(a, b):
    return lax.dot_general(a, b, (((1,), (1,)), ((), ())), preferred_element_type=F32)


def _tdot(a, b):
    return lax.dot_general(a, b, (((0,), (0,)), ((), ())), preferred_element_type=F32)


def _my_place():
    return lax.axis_index("x"), lax.axis_index("y"), lax.axis_index("c")


def _other_chips(x, y):
    places = [(1 - x, y), (x, 1 - y), (1 - x, 1 - y)]
    return [(p, 2 * p[0] + p[1]) for p in places]


OWN_SLOT = 2


def _gather_inproj(x, g1, win_sh, wout_sh):
    seq = x.shape[0]
    tm = min(seq, 4 * TS)
    n_tiles = seq // tm
    cx, cy = lax.axis_index("x"), lax.axis_index("y")
    order = jnp.stack([2 * cx + cy] + [chip for _, chip in _other_chips(cx, cy)]).astype(jnp.int32)

    def body(order_ref, x_ref, g1_ref, win_ref, wout_ref, proj_bf_ref, win_full, wout_full,
             hbuf, wbuf, wout_own, win_f32, wout_f32, send_sems, recv_sems, local_sems):
        j = pl.program_id(0)
        i = pl.program_id(1)
        x_, y_, c = _my_place()
        b = 2 * x_ + y_
        sibling = (x_, y_, 1 - c)
        others = _other_chips(x_, y_)

        def halves(chip, core):
            return (
                win_full.at[chip, pl.ds(core * HALF_IN, HALF_IN)],
                wout_full.at[chip, pl.ds(core * HALF_OUT, HALF_OUT)],
            )

        def copy(k, src, dst, to):
            return pltpu.make_async_remote_copy(
                src_ref=src, dst_ref=dst, send_sem=send_sems.at[k], recv_sem=recv_sems.at[k],
                device_id=to, device_id_type=MESH)

        own = [
            pltpu.make_async_copy(wbuf.at[OWN_SLOT], win_full.at[b], local_sems.at[0]),
            pltpu.make_async_copy(wout_own, wout_full.at[b], local_sems.at[1]),
        ]
        mine_src = (wbuf.at[OWN_SLOT, pl.ds(c * HALF_IN, HALF_IN)], wout_own.at[pl.ds(c * HALF_OUT, HALF_OUT)])

        def direct(n, t):
            return copy(2 * n + t, mine_src[t], halves(b, c)[t], (*others[n][0], c))

        def arrival(n, t):
            landed = halves(others[n][1], c)[t]
            return copy(2 * n + t, landed, landed, (*others[n][0], c))

        def passing(n, t):
            landed = halves(others[n][1], c)[t]
            return copy(6 + 2 * n + t, landed, landed, sibling)

        def from_sibling(n, t):
            landed = halves(others[n][1], 1 - c)[t]
            return copy(6 + 2 * n + t, landed, landed, sibling)

        @pl.when((j == 0) & (i == 0))
        def _():
            staged = [pltpu.make_async_copy(win_ref, win_f32, local_sems.at[2]),
                      pltpu.make_async_copy(wout_ref, wout_f32, local_sems.at[3])]
            for cp in staged:
                cp.start()
            for t in range(2):
                staged[t].wait()
                if t == 0:
                    wbuf[OWN_SLOT] = win_f32[...].astype(BF16)
                else:
                    wout_own[...] = wout_f32[...].astype(BF16)
                own[t].start()
                for n in range(N_CHIP - 1):
                    direct(n, t).start()

        def load(n):
            return pltpu.make_async_copy(win_full.at[others[n][1]], wbuf.at[(n + 1) % 2], local_sems.at[2])

        for n in range(N_CHIP - 1):
            @pl.when((j == n) & (i == n_tiles - 1))
            def _(n=n):
                arrival(n, 0).wait_recv()
                passing(n, 0).start()

            @pl.when((j == n + 1) & (i == 0))
            def _(n=n):
                load(n).wait()

        rows = pl.ds(pl.multiple_of(i * tm, tm), tm)

        @pl.when(j == 0)
        def _():
            xt = x_ref[...]
            r = lax.rsqrt(jnp.mean(xt * xt, axis=-1, keepdims=True) + EPS)
            hbuf[rows] = ((xt * r) * g1_ref[...]).astype(BF16)

        proj_bf_ref[...] = _dot(hbuf[rows], wbuf[jnp.where(j == 0, OWN_SLOT, j % 2)]).astype(BF16)

        for n in range(N_CHIP - 1):
            @pl.when((j == n) & (i == n_tiles - 1))
            def _(n=n):
                from_sibling(n, 0).wait_recv()
                load(n).start()

        @pl.when((j == N_CHIP - 1) & (i == n_tiles - 1))
        def _():
            for n in range(N_CHIP - 1):
                arrival(n, 1).wait_recv()
                passing(n, 1).start()
            for n in range(N_CHIP - 1):
                from_sibling(n, 1).wait_recv()
            for n in range(N_CHIP - 1):
                for t in range(2):
                    direct(n, t).wait_send()
                    passing(n, t).wait_send()
            for cp in own:
                cp.wait()

    last = n_tiles - 1
    grid_spec = pltpu.PrefetchScalarGridSpec(
        num_scalar_prefetch=1,
        grid=(N_CHIP, n_tiles),
        in_specs=[
            pl.BlockSpec((tm, D_MODEL), lambda j, i, o: (jnp.where(j == 0, i, last), 0)),
            VMEM, ANY, ANY],
        out_specs=[pl.BlockSpec((tm, SHARD_IN), lambda j, i, o: (i, o[j])), ANY, ANY],
        scratch_shapes=[
            pltpu.VMEM((seq, D_MODEL), BF16), pltpu.VMEM((OWN_SLOT + 1, D_MODEL, SHARD_IN), BF16),
            pltpu.VMEM((SHARD_OUT, D_MODEL), BF16), pltpu.VMEM((D_MODEL, SHARD_IN), F32), pltpu.VMEM((SHARD_OUT, D_MODEL), F32),
            pltpu.SemaphoreType.DMA((12,)), pltpu.SemaphoreType.DMA((12,)), pltpu.SemaphoreType.DMA((4,))],
    )
    return pl.pallas_call(
        body,
        name="gather_inproj",
        grid_spec=grid_spec,
        out_shape=(
            jax.ShapeDtypeStruct((seq, N_CHIP * SHARD_IN), BF16),
            jax.ShapeDtypeStruct((N_CHIP, D_MODEL, SHARD_IN), BF16),
            jax.ShapeDtypeStruct((N_CHIP, SHARD_OUT, D_MODEL), BF16),
        ),
        compiler_params=pltpu.CompilerParams(
            dimension_semantics=("arbitrary", "arbitrary"), vmem_limit_bytes=VMEM_LIMIT),
    )(order, x, g1, win_sh, wout_sh)


def _window_sums(ext, forward):
    n = ext.shape[0]
    sums = []
    acc = ext
    for step in (1, 2, 4, 8):
        acc = acc + pltpu.roll(acc, (n - step) if forward else step, 0)
        sums.append(acc)
    return sums


def _row_counts(tile, rows):
    t = tile * rows + lax.broadcasted_iota(jnp.int32, (rows, GROUP), 0)
    return [jnp.minimum(t + 1, w).astype(F32) for w in POOL_WINDOWS]


def _pool_diffs(pv, prev_rows, tile):
    ext = jnp.concatenate([prev_rows, pv], axis=0)
    sums = _window_sums(ext, forward=False)
    counts = _row_counts(tile, pv.shape[0])
    out = []
    for g in range(len(POOL_WINDOWS)):
        cols = slice(g * GROUP, (g + 1) * GROUP)
        out.append(sums[g][HALO:, cols] / counts[g] - pv[:, cols])
    return out


def _pool_mix(diffs, pw_ref):
    return jnp.concatenate([_dot(diffs[g], pw_ref[g]) for g in range(len(POOL_WINDOWS))], axis=1)


Q_BLOCK, K_BLOCK, V_BLOCK, AG_BLOCK = 8, 12, 16, 20


BIAS_TABLE = WINDOW + GROUP


def _bias_table(rel_bias):
    far = jnp.broadcast_to(rel_bias[:, :1], (rel_bias.shape[0], BAND - CHUNK - 1))
    by_distance = jnp.concatenate([far, rel_bias[:, :2 * MAX_REL]], axis=1)
    table = jnp.pad(by_distance, ((0, 0), (CHUNK, BIAS_TABLE - CHUNK - by_distance.shape[1])))
    return table.reshape(N_PAIR, 2, BIAS_TABLE)


def _bias_tile(table_ref):
    query = lax.broadcasted_iota(jnp.int32, (CHUNK, GROUP), 0)
    lane = lax.broadcasted_iota(jnp.int32, (CHUNK, GROUP), 1)
    key = lax.broadcasted_iota(jnp.int32, (CHUNK, WINDOW), 1)

    def skewed(table, offset):
        rotated = [pltpu.roll(jnp.broadcast_to(table[:, lane_group(m)], (CHUNK, GROUP)), (-offset) % GROUP, 1,
                              stride=1, stride_axis=0) for m in range(BIAS_TABLE // GROUP)]
        in_first = offset - query + lane < GROUP
        return jnp.concatenate(
            [jnp.where(in_first, rotated[m], rotated[m + 1]) for m in range(WINDOW // GROUP)], axis=1)

    def lane_group(m):
        return slice(m * GROUP, (m + 1) * GROUP)

    rows = []
    for h in range(2):
        table = table_ref[0, h:h + 1, :] * LOG2E
        rows.append(jnp.where(key < BAND, skewed(table, CHUNK + CHUNK - 1), MASK_VALUE))
        rows.append(jnp.where(key >= CHUNK, skewed(table, CHUNK - 1), MASK_VALUE))
    return jnp.concatenate(rows, axis=0).T


def _rel_index_tile():
    j = np.arange(WINDOW)[:, None]
    q = np.arange(PAIR_LANES)[None, :] % SUPER
    band_key = j - CHUNK * (q // CHUNK)
    idx = np.clip(band_key - LEFT_CHUNKS * CHUNK - q % CHUNK, -MAX_REL, MAX_REL) + MAX_REL
    return np.where((band_key >= 0) & (band_key < BAND), idx, -1).astype(np.int32)


def _by_head(block):
    low = lax.broadcasted_iota(jnp.int32, block.shape, 1) < HEAD_DIM
    zero = jnp.zeros_like(block)
    return jnp.concatenate([jnp.where(low, block, zero), jnp.where(low, zero, block)], axis=0)


def _own_head_rows(cross):
    head_of_row = lax.broadcasted_iota(jnp.int32, cross.shape, 0) >= HEAD_DIM
    head_of_lane = lax.broadcasted_iota(jnp.int32, cross.shape, 1) >= SUPER
    both = jnp.where(jnp.logical_xor(head_of_row, head_of_lane), 0.0, cross).T
    return both[:SUPER] + both[SUPER:]


def _with_mask_lane(q_rows):
    lane = lax.broadcasted_iota(jnp.int32, q_rows.shape, 1)
    return jnp.concatenate([q_rows, jnp.where(lane == 0, MASK_VALUE, 0.0).astype(q_rows.dtype)], axis=1)


def _band_exp(kb, q_rows, bias):
    s = _dot_t(kb, _with_mask_lane(q_rows)) + bias
    e = jnp.exp2(s - jnp.max(s, axis=0, keepdims=True))
    return e, jnp.sum(e, axis=0, keepdims=True)


def _window(sc):
    return slice(WIN_BASE + sc * SUPER, WIN_BASE + sc * SUPER + WINDOW)


def _shift_band(i, band_ref, new_ref):
    @pl.when(i == 0)
    def _():
        band_ref[:TA] = jnp.zeros((TA, GROUP), band_ref.dtype)

    @pl.when(i > 0)
    def _():
        band_ref[:TA] = band_ref[TA:]

    band_ref[TA:] = new_ref[...].astype(band_ref.dtype)


def _shift_key_band(i, band_ref, new_ref):
    @pl.when(i == 0)
    def _():
        lane = lax.broadcasted_iota(jnp.int32, (TA, 2 * GROUP), 1)
        band_ref[:TA] = jnp.where(lane == GROUP, 1.0, 0.0).astype(band_ref.dtype)
        band_ref[TA:, GROUP:] = jnp.zeros((TA, GROUP), band_ref.dtype)

    @pl.when(i > 0)
    def _():
        band_ref[:TA] = band_ref[TA:]

    band_ref[TA:, :GROUP] = new_ref[...].astype(band_ref.dtype)


def _shift_band_t(i, band_ref, new_ref):
    @pl.when(i == 0)
    def _():
        band_ref[:, :TA] = jnp.zeros((GROUP, TA), band_ref.dtype)

    @pl.when(i > 0)
    def _():
        band_ref[:, :TA] = band_ref[:, TA:]

    band_ref[:, TA:] = new_ref[...].astype(band_ref.dtype).T


def _scaled_queries(q_ref, rows, scale=SCALE):
    return _by_head((q_ref[rows] * scale).astype(BF16))


def _attn_fwd(proj_bf, bias_table):
    seq = proj_bf.shape[0]
    n_tiles = seq // TA

    def body(q_ref, k_ref, v_ref, ag_ref, table_ref, a_ref, ya_ref, e_ref, inv_ref, kband, vband_t, bias_ref):
        i = pl.program_id(1)
        _shift_key_band(i, kband, k_ref)
        _shift_band_t(i, vband_t, v_ref)

        @pl.when(i == 0)
        def _():
            bias_ref[...] = _bias_tile(table_ref)

        def weights(sc):
            rows = slice(sc * SUPER, (sc + 1) * SUPER)
            win = _window(sc)
            e, total = _band_exp(kband[win], _scaled_queries(q_ref, rows, SCALE * LOG2E), bias_ref[...])
            e_ref[0, sc] = e.astype(BF16)
            inv_total = 1.0 / total
            inv_ref[0, sc] = jnp.broadcast_to(inv_total, (8, PAIR_LANES))
            return inv_total

        nxt = weights(0)
        for sc in range(SUPERS_PER_TILE):
            rows = slice(sc * SUPER, (sc + 1) * SUPER)
            win = _window(sc)
            inv_total = nxt
            if sc + 1 < SUPERS_PER_TILE:
                nxt = weights(sc + 1)
            a = _own_head_rows(_dot(vband_t[:, win], e_ref[0, sc]) * inv_total)
            a_ref[rows] = a.astype(BF16)
            g = ag_ref[rows].astype(F32)
            ya_ref[rows] = (a * (g * _sigmoid(g))).astype(BF16)

    blk = pl.BlockSpec((TA, GROUP), lambda p, i: (i, p))

    def cols(first):
        return pl.BlockSpec((TA, GROUP), lambda p, i: (i, first + p))

    return pl.pallas_call(
        body,
        name="attn_fwd",
        grid=(N_PAIR, n_tiles),
        in_specs=[cols(Q_BLOCK), cols(K_BLOCK), cols(V_BLOCK), cols(AG_BLOCK),
                  pl.BlockSpec((1, 2, BIAS_TABLE), lambda p, i: (p, 0, 0))],
        out_specs=[blk, blk,
                   pl.BlockSpec((1, SUPERS_PER_TILE, WINDOW, PAIR_LANES), lambda p, i: (p, i, 0, 0)),
                   pl.BlockSpec((1, SUPERS_PER_TILE, 8, PAIR_LANES), lambda p, i: (p, i, 0, 0))],
        out_shape=[jax.ShapeDtypeStruct((seq, ATTN_WIDTH), BF16), jax.ShapeDtypeStruct((seq, ATTN_WIDTH), BF16),
                   jax.ShapeDtypeStruct((N_PAIR, seq // SUPER, WINDOW, PAIR_LANES), BF16),
                   jax.ShapeDtypeStruct((N_PAIR, seq // SUPER, 8, PAIR_LANES), F32)],
        scratch_shapes=[pltpu.VMEM((2 * TA, 2 * GROUP), BF16), pltpu.VMEM((GROUP, 2 * TA), BF16),
                        pltpu.VMEM((WINDOW, PAIR_LANES), F32)],
        compiler_params=pltpu.CompilerParams(
            dimension_semantics=("arbitrary", "arbitrary"), vmem_limit_bytes=VMEM_LIMIT),
    )(proj_bf, proj_bf, proj_bf, proj_bf, bias_table)


BIN_ROWS = 136


def _bias_bin_sums(db_ref, idx_ref):
    lane = lax.broadcasted_iota(jnp.int32, (1, GROUP), 1)
    row = lax.broadcasted_iota(jnp.int32, (BIN_ROWS, GROUP), 0)
    out = jnp.zeros((BIN_ROWS, GROUP), F32)
    for r in range(N_REL - 1):
        lo = 0 if r == 0 else ((BAND - 2 * CHUNK + r) // 8) * 8
        hi = WINDOW if r == 0 else min(WINDOW, lo + SUPER + 8)
        hit = jnp.where(idx_ref[lo:hi] == r, db_ref[lo:hi], 0.0)
        col = jnp.sum(hit, axis=0, keepdims=True)
        s0 = jnp.sum(col[:, :SUPER], axis=1, keepdims=True)
        s1 = jnp.sum(col[:, SUPER:], axis=1, keepdims=True)
        val = jnp.where(lane == 0, s0, jnp.where(lane == 1, s1, 0.0))
        out = jnp.where(row == r, val, out)
    return out


def _attn_bwd(proj_bf, a, dy, e_all, inv_all):
    seq = proj_bf.shape[0]
    n_tiles = seq // TA

    def body(q_ref, k_ref, v_ref, a_ref, ag_ref, dy_ref, e_ref, inv_ref, idx_ref,
             dq_ref, dk_ref, dv_ref, dag_ref, bins_ref, vband, kband_t, dkacc, dvacc, db_ref):
        i = pl.program_id(1)

        @pl.when(i == 0)
        def _():
            dkacc[...] = jnp.zeros_like(dkacc)
            dvacc[...] = jnp.zeros_like(dvacc)
            db_ref[...] = jnp.zeros_like(db_ref)

        @pl.when(i < n_tiles)
        def _():
            _shift_band(i, vband, v_ref)
            _shift_band_t(i, kband_t, k_ref)

            def score_grads(sc):
                rows = slice(sc * SUPER, (sc + 1) * SUPER)
                win = _window(sc)
                q_rows = _scaled_queries(q_ref, rows)
                g = ag_ref[rows].astype(F32)
                sg = _sigmoid(g)
                dyc = dy_ref[rows].astype(F32)
                dag_ref[rows] = (dyc * a_ref[rows].astype(F32) * (sg * (1.0 + g * (1.0 - sg)))).astype(BF16)
                da_rows = _by_head((dyc * (g * sg)).astype(BF16))
                p = e_ref[0, sc].astype(F32) * inv_ref[0, sc, :1]
                dp = _dot_t(vband[win], da_rows)
                ds = p * (dp - jnp.sum(p * dp, axis=0, keepdims=True))
                db_ref[...] += ds
                return q_rows, da_rows, p.astype(BF16), ds.astype(BF16)

            nxt = score_grads(0)
            for sc in range(SUPERS_PER_TILE):
                rows = slice(sc * SUPER, (sc + 1) * SUPER)
                win = _window(sc)
                q_rows, da_rows, p_bf, ds_bf = nxt
                if sc + 1 < SUPERS_PER_TILE:
                    nxt = score_grads(sc + 1)
                dq_ref[rows] = (_own_head_rows(_dot(kband_t[:, win], ds_bf)) * SCALE).astype(BF16)
                dkacc[win] += _dot(ds_bf, q_rows)
                dvacc[win] += _dot(p_bf, da_rows)

        dk_ref[...] = dkacc[:TA].astype(BF16)
        dv_ref[...] = dvacc[:TA].astype(BF16)
        dkacc[:TA] = dkacc[TA:]
        dvacc[:TA] = dvacc[TA:]
        dkacc[TA:] = jnp.zeros((TA, GROUP), F32)
        dvacc[TA:] = jnp.zeros((TA, GROUP), F32)

        @pl.when(i == n_tiles)
        def _():
            bins_ref[0] = _bias_bin_sums(db_ref, idx_ref)

    last = n_tiles - 1
    cur = pl.BlockSpec((TA, GROUP), lambda p, i: (jnp.minimum(i, last), p))
    older = pl.BlockSpec((TA, GROUP), lambda p, i: (jnp.maximum(i - 1, 0), p))
    dy_blk = pl.BlockSpec((TA, GROUP), lambda p, i: (jnp.minimum(i, last), N_PAIR + p))
    per_pair = pl.BlockSpec((1, BIN_ROWS, GROUP), lambda p, i: (p, 0, 0))

    def cols(first):
        return pl.BlockSpec((TA, GROUP), lambda p, i: (jnp.minimum(i, last), first + p))

    def kept(rows):
        return pl.BlockSpec((1, SUPERS_PER_TILE, rows, PAIR_LANES), lambda p, i: (p, jnp.minimum(i, last), 0, 0))

    def out(dtype):
        return jax.ShapeDtypeStruct((seq, ATTN_WIDTH), dtype)

    return pl.pallas_call(
        body,
        name="attn_bwd",
        grid=(N_PAIR, n_tiles + 1),
        in_specs=[cols(Q_BLOCK), cols(K_BLOCK), cols(V_BLOCK), cur, cols(AG_BLOCK), dy_blk, kept(WINDOW), kept(8),
                  VMEM],
        out_specs=[cur, older, older, cur, per_pair],
        out_shape=[out(BF16), out(BF16), out(BF16), out(BF16),
                   jax.ShapeDtypeStruct((N_PAIR, BIN_ROWS, GROUP), F32)],
        scratch_shapes=[
            pltpu.VMEM((2 * TA, GROUP), BF16), pltpu.VMEM((GROUP, 2 * TA), BF16),
            pltpu.VMEM((2 * TA, GROUP), F32), pltpu.VMEM((2 * TA, GROUP), F32),
            pltpu.VMEM((WINDOW, PAIR_LANES), F32)],
        compiler_params=pltpu.CompilerParams(
            dimension_semantics=("arbitrary", "arbitrary"), vmem_limit_bytes=VMEM_LIMIT),
    )(proj_bf, proj_bf, proj_bf, a, proj_bf, dy, e_all, inv_all, jnp.asarray(_rel_index_tile()))


def _out_loss(x, tgt, proj, ya, wout_full, g2, pw_bf, ps):
    seq = x.shape[0]
    to = min(seq, 2 * TS)
    n_tiles = seq // to

    def body(x_ref, t_ref, pv_ref, pg_ref, ya_ref, w_ref, g2_ref, pw_ref, ps_ref,
             dx2_ref, dy_ref, gw_ref, gg_ref, loss_ref, sq_ref, halo_ref, wt_ref):
        i = pl.program_id(0)

        @pl.when(i == 0)
        def _():
            gw_ref[...] = jnp.zeros_like(gw_ref)
            gg_ref[...] = jnp.zeros_like(gg_ref)
            sq_ref[...] = jnp.zeros_like(sq_ref)
            halo_ref[...] = jnp.zeros_like(halo_ref)
            for b in range(N_CHIP):
                wt_ref[b] = w_ref[b].T

        pv = pv_ref[...].astype(F32)
        pg = pg_ref[...].astype(F32)
        diffs = [d.astype(BF16) for d in _pool_diffs(pv, halo_ref[...], i)]
        halo_ref[...] = pv[to - HALO:, :]
        yp = ((_pool_mix(diffs, pw_ref) * ps_ref[...]) * (pg * _sigmoid(pg))).astype(BF16)
        g2v = g2_ref[...]

        def parts(rows):
            return [yp[rows, :SHARD_OUT], yp[rows, SHARD_OUT:], ya_ref[rows, :SHARD_OUT], ya_ref[rows, SHARD_OUT:]]

        def project(rows, ys):
            x2 = x_ref[rows]
            for b in range(N_CHIP):
                x2 = x2 + _dot(ys[b], w_ref[b])
            return x2

        def norm_loss(rows, x2):
            r = lax.rsqrt(jnp.mean(x2 * x2, axis=-1, keepdims=True) + EPS)
            xh = x2 * r
            diff = xh * g2v - t_ref[rows]
            sq_ref[...] += jnp.sum(diff * diff, axis=0, keepdims=True)
            dfin = diff * (1.0 / D_MODEL)
            gg_ref[...] += jnp.sum(dfin * xh, axis=0, keepdims=True)
            dxh = dfin * g2v
            dx2 = r * (dxh - xh * jnp.mean(dxh * xh, axis=-1, keepdims=True))
            dx2_ref[rows] = dx2
            return dx2.astype(BF16)

        def back(rows, ys, dx2_bf):
            for b in range(N_CHIP):
                gw_ref[b] += _tdot(ys[b], dx2_bf)
                dy_ref[rows, b * SHARD_OUT:(b + 1) * SHARD_OUT] = _dot(dx2_bf, wt_ref[b]).astype(BF16)

        n_parts = 2
        part = to // n_parts
        spans = [slice(r * part, (r + 1) * part) for r in range(n_parts)]
        ys = [parts(rows) for rows in spans]
        x2_next = project(spans[0], ys[0])
        for r in range(n_parts):
            x2 = x2_next
            if r + 1 < n_parts:
                x2_next = project(spans[r + 1], ys[r + 1])
            back(spans[r], ys[r], norm_loss(spans[r], x2))

        @pl.when(i == n_tiles - 1)
        def _():
            total = jnp.sum(sq_ref[...], axis=1, keepdims=True) * (0.5 / D_MODEL)
            loss_ref[...] = jnp.broadcast_to(total, loss_ref.shape)

    def rows(width, col=0):
        return pl.BlockSpec((to, width), lambda i: (i, col))

    return pl.pallas_call(
        body,
        name="out_loss",
        grid=(n_tiles,),
        in_specs=[rows(D_MODEL), rows(D_MODEL), rows(POOL_WIDTH, 0), rows(POOL_WIDTH, 1), rows(ATTN_WIDTH),
                  VMEM, VMEM, VMEM, VMEM],
        out_specs=[rows(D_MODEL), rows(D_MODEL), VMEM, VMEM, VMEM],
        out_shape=[
            jax.ShapeDtypeStruct((seq, D_MODEL), F32), jax.ShapeDtypeStruct((seq, D_MODEL), BF16),
            jax.ShapeDtypeStruct((N_CHIP, SHARD_OUT, D_MODEL), F32), jax.ShapeDtypeStruct((1, D_MODEL), F32),
            jax.ShapeDtypeStruct((8, GROUP), F32)],
        scratch_shapes=[pltpu.VMEM((1, D_MODEL), F32), pltpu.VMEM((HALO, POOL_WIDTH), F32),
                        pltpu.VMEM((N_CHIP, D_MODEL, SHARD_OUT), BF16)],
        compiler_params=pltpu.CompilerParams(dimension_semantics=("arbitrary",), vmem_limit_bytes=VMEM_LIMIT),
    )(x, tgt, proj, proj, ya, wout_full, g2, pw_bf, ps)


def _inproj_bwd(x, dx2, dy, proj, dq, dk, dv, dag, win_full, pw_bf, g1, ps):
    seq = x.shape[0]
    n_tiles = seq // TS

    def body(x_ref, dx2_ref, dyp_ref, pv_ref, pvprev_ref, pg_ref, dq_ref, dk_ref, dv_ref, dag_ref,
             w_ref, pw_ref, g1_ref, ps_ref, gx_ref, dproj_ref, ht_ref, gg_ref, gps_ref, gpw_ref, halo_ref):
        i = pl.program_id(0)
        tile = n_tiles - 1 - i

        @pl.when(i == 0)
        def _():
            gg_ref[...] = jnp.zeros_like(gg_ref)
            gps_ref[...] = jnp.zeros_like(gps_ref)
            gpw_ref[...] = jnp.zeros_like(gpw_ref)
            halo_ref[...] = jnp.zeros_like(halo_ref)

        pv_t = pv_ref[...].astype(F32)
        pg_t = pg_ref[...].astype(F32)
        prev_rows = jnp.where(tile > 0, pvprev_ref[...].astype(F32), 0.0)
        diffs = [d.astype(BF16) for d in _pool_diffs(pv_t, prev_rows, tile)]
        mixed = _pool_mix(diffs, pw_ref)
        sg = _sigmoid(pg_t)
        silu = pg_t * sg
        dyp = dyp_ref[...].astype(F32)
        psv = ps_ref[...]
        gps_ref[...] += jnp.sum(dyp * mixed * silu, axis=0, keepdims=True)
        dmixed = (dyp * psv * silu).astype(BF16)
        dpg = dyp * (mixed * psv) * (sg * (1.0 + pg_t * (1.0 - sg)))
        counts = _row_counts(tile, TS)
        dds = []
        for g in range(len(POOL_WINDOWS)):
            dm_g = dmixed[:, g * GROUP:(g + 1) * GROUP]
            gpw_ref[g] += _tdot(diffs[g], dm_g)
            dds.append(_dot_t(dm_g, pw_ref[g]))
        dd = jnp.concatenate(dds, axis=1)
        spread = jnp.concatenate([dds[g] / counts[g] for g in range(len(POOL_WINDOWS))], axis=1)
        sums = _window_sums(jnp.concatenate([spread, halo_ref[...]], axis=0), forward=True)
        halo_ref[...] = spread[:HALO]
        dpv = jnp.concatenate(
            [sums[g][:TS, g * GROUP:(g + 1) * GROUP] for g in range(len(POOL_WINDOWS))], axis=1) - dd

        xt = x_ref[...]
        r = lax.rsqrt(jnp.mean(xt * xt, axis=-1, keepdims=True) + EPS)
        xh = xt * r
        g1v = g1_ref[...]
        ht_ref[...] = (xh * g1v).astype(BF16).T
        dproj = jnp.concatenate(
            [dpv.astype(BF16), dpg.astype(BF16), dq_ref[...], dk_ref[...], dv_ref[...], dag_ref[...]],
            axis=1)
        dproj_ref[...] = dproj
        dh = _dot_t(dproj[:, :SHARD_IN], w_ref[0])
        for chip in range(1, N_CHIP):
            dh = dh + _dot_t(dproj[:, chip * SHARD_IN:(chip + 1) * SHARD_IN], w_ref[chip])

        gg_ref[...] += jnp.sum(dh * xh, axis=0, keepdims=True)
        dxh = dh * g1v
        gx_ref[...] = dx2_ref[...] + r * (dxh - xh * jnp.mean(dxh * xh, axis=-1, keepdims=True))

    def rows(width, col=0):
        return pl.BlockSpec((TS, width), lambda i: (n_tiles - 1 - i, col))

    prev = pl.BlockSpec((HALO, POOL_WIDTH), lambda i: (jnp.maximum((n_tiles - 1 - i) * (TS // HALO) - 1, 0), 0))
    return pl.pallas_call(
        body,
        name="inproj_bwd",
        grid=(n_tiles,),
        in_specs=[rows(D_MODEL), rows(D_MODEL), rows(POOL_WIDTH), rows(POOL_WIDTH), prev, rows(POOL_WIDTH, 1),
                  rows(ATTN_WIDTH), rows(ATTN_WIDTH), rows(ATTN_WIDTH), rows(ATTN_WIDTH), VMEM, VMEM, VMEM, VMEM],
        out_specs=[rows(D_MODEL), rows(N_CHIP * SHARD_IN),
                   pl.BlockSpec((D_MODEL, TS), lambda i: (0, n_tiles - 1 - i)), VMEM, VMEM, VMEM],
        out_shape=[
            jax.ShapeDtypeStruct((seq, D_MODEL), F32),
            jax.ShapeDtypeStruct((seq, N_CHIP * SHARD_IN), BF16),
            jax.ShapeDtypeStruct((D_MODEL, seq), BF16),
            jax.ShapeDtypeStruct((1, D_MODEL), F32),
            jax.ShapeDtypeStruct((1, POOL_WIDTH), F32),
            jax.ShapeDtypeStruct((len(POOL_WINDOWS), GROUP, GROUP), F32)],
        scratch_shapes=[pltpu.VMEM((HALO, POOL_WIDTH), F32)],
        compiler_params=pltpu.CompilerParams(dimension_semantics=("arbitrary",), vmem_limit_bytes=VMEM_LIMIT),
    )(x, dx2, dy, proj, proj, proj, dq, dk, dv, dag, win_full, pw_bf, g1, ps)


PASS_PEER = (2, 0, 1)


def _pack_small(dst, norm_ref, pool_w_ref, pool_scale_ref, bins_ref, final_ref, loss_ref):
    def lane_group(r):
        return slice(r * GROUP, (r + 1) * GROUP)

    dst[...] = jnp.zeros(dst.shape, F32)
    for r in range(D_MODEL // GROUP):
        dst[r:r + 1] = norm_ref[:, lane_group(r)]
        dst[FINAL_GAIN_ROW + r:FINAL_GAIN_ROW + r + 1] = final_ref[:, lane_group(r)]
    for g in range(len(POOL_WINDOWS)):
        dst[POOL_W_ROW + g * GROUP:POOL_W_ROW + (g + 1) * GROUP] = pool_w_ref[g]
    for r in range(POOL_WIDTH // GROUP):
        dst[POOL_SCALE_ROW + r:POOL_SCALE_ROW + r + 1] = pool_scale_ref[:, lane_group(r)]
    first_lane = lax.broadcasted_iota(jnp.int32, (BIN_ROWS - GROUP, GROUP), 1) == 0
    for p in range(N_PAIR):
        by_head = bins_ref[p, :GROUP].T
        last_column = bins_ref[p, GROUP:]
        for h in range(2):
            row = REL_BIAS_ROW + REL_BIAS_ROWS_PER_HEAD * (2 * p + h)
            dst[row:row + 1] = by_head[h:h + 1]
            to_first_lane = last_column if h == 0 else pltpu.roll(last_column, GROUP - h, 1)
            dst[row + 1:row + 2] = jnp.where(first_lane, to_first_lane, 0.0)[:1]
    dst[LOSS_ROW:LOSS_ROW + loss_ref.shape[0]] = loss_ref[...]


def _gw_reduce(ht, dproj, gwout, small_parts):
    seq = ht.shape[1]
    tm = min(2 * TS, seq // 4)
    n_tiles = seq // tm
    half_small = SMALL_ROWS // 2
    cx, cy = lax.axis_index("x"), lax.axis_index("y")
    outer = _other_chips(cx, cy)
    order = jnp.stack([outer[n][1] for n in PASS_PEER] + [2 * cx + cy]).astype(jnp.int32)

    def body(order_ref, ht_ref, dp_ref, gwout_ref, *refs):
        part_refs = refs[:len(small_parts)]
        (gin_final, gout_final, small_final, hbuf, acc, pair_in, pair_out, small_ref, pair_small, tx_in, tx_out, rx_in, rx_out,
         rx_small, gin_out, gout_out, small_out, send_sems, recv_sems, out_sems) = refs[len(small_parts):]
        j = pl.program_id(0)
        i = pl.program_id(1)
        x, y, c = _my_place()
        b = 2 * x + y
        sibling = (x, y, 1 - c)
        others = _other_chips(x, y)
        mine_in = pl.ds(pl.multiple_of(c * HALF_IN, HALF_IN), HALF_IN)
        mine_out = pl.ds(pl.multiple_of(c * HALF_OUT, HALF_OUT), HALF_OUT)
        mine_small = pl.ds(pl.multiple_of(c * half_small, 8), half_small)
        theirs_in = pl.ds(pl.multiple_of((1 - c) * HALF_IN, HALF_IN), HALF_IN)
        theirs_out = pl.ds(pl.multiple_of((1 - c) * HALF_OUT, HALF_OUT), HALF_OUT)
        theirs_small = pl.ds(pl.multiple_of((1 - c) * half_small, 8), half_small)

        def copy(k, src, dst, to):
            return pltpu.make_async_remote_copy(
                src_ref=src, dst_ref=dst, send_sem=send_sems.at[k], recv_sem=recv_sems.at[k],
                device_id=to, device_id_type=MESH)

        swap_out = copy(0, gwout_ref.at[:, theirs_out], pair_out, sibling)
        swap_small = copy(1, small_ref, pair_small, sibling)

        def swap_in(p):
            return copy(2 + p, acc.at[p % 2, theirs_in], pair_in.at[p % 2], sibling)

        def to_chip(n, t):
            to = (*others[n][0], c)
            if t == 0:
                return copy(6 + 3 * n, tx_in.at[n], rx_in.at[n], to)
            if t == 1:
                return copy(7 + 3 * n, tx_out.at[n], rx_out.at[n], to)
            return copy(8 + 3 * n, pair_small.at[mine_small], rx_small.at[b], to)

        share_in = copy(15, gin_out.at[mine_in], gin_out.at[mine_in], sibling)
        share_out = copy(16, gout_out.at[mine_out], gout_out.at[mine_out], sibling)
        share_small = copy(17, small_out.at[mine_small], small_out.at[mine_small], sibling)

        def at(jj, ii):
            return (j == jj) & (i == ii)

        par = j % 2

        @pl.when(i == 0)
        def _():
            acc[par] = jnp.zeros((D_MODEL, SHARD_IN), F32)

        cols = pl.ds(pl.multiple_of(i * tm, tm), tm)

        @pl.when(j == 0)
        def _():
            hbuf[:, cols] = ht_ref[...]

        acc[par] += _dot(hbuf[:, cols], dp_ref[...])

        @pl.when(at(0, 0))
        def _():
            swap_out.start()
            _pack_small(small_ref, *part_refs)
            swap_small.start()

        @pl.when(at(0, 2))
        def _():
            swap_out.wait_recv()
            swap_small.wait_recv()
            for chip in range(N_CHIP):
                pair_out[chip] = gwout_ref[chip, mine_out] + pair_out[chip]
            pair_small[...] = small_ref[...] + pair_small[...]
            rx_small[b] = pair_small[mine_small]
            for n in range(N_CHIP - 1):
                tx_out[n] = pair_out[others[n][1]].astype(BF16)
                to_chip(n, 1).start()
                to_chip(n, 2).start()

        @pl.when(at(1, 3))
        def _():
            total_out = pair_out[b]
            for n in range(N_CHIP - 1):
                to_chip(n, 1).wait_recv()
                copy(8 + 3 * n, pair_small.at[mine_small], rx_small.at[others[n][1]], (*others[n][0], c)).wait_recv()
                total_out = total_out + rx_out[n].astype(F32)
            gout_out[mine_out] = total_out
            small_out[mine_small] = ((rx_small[0] + rx_small[1]) + rx_small[2]) + rx_small[3]
            share_out.start()
            share_small.start()

        for p in range(N_CHIP - 1):
            n = PASS_PEER[p]

            @pl.when(at(p + 1, 0))
            def _(p=p):
                swap_in(p).start()

            @pl.when(at(p + 1, 2))
            def _(p=p, n=n):
                swap_in(p).wait_recv()
                swap_in(p).wait_send()
                tx_in[n] = (acc[p % 2, mine_in] + pair_in[p % 2]).astype(BF16)
                to_chip(n, 0).start()

        @pl.when(at(N_CHIP - 1, n_tiles - 1))
        def _():
            last = N_CHIP - 1
            swap_in(last).start()
            swap_in(last).wait_recv()
            total_in = acc[last % 2, mine_in] + pair_in[last % 2]
            for n in range(N_CHIP - 1):
                to_chip(n, 0).wait_recv()
                total_in = total_in + rx_in[n].astype(F32)
            gin_out[mine_in] = total_in
            share_in.start()
            copy(15, gin_out.at[theirs_in], gin_out.at[theirs_in], sibling).wait_recv()
            copy(16, gout_out.at[theirs_out], gout_out.at[theirs_out], sibling).wait_recv()
            copy(17, small_out.at[theirs_small], small_out.at[theirs_small], sibling).wait_recv()
            swap_out.wait_send()
            swap_small.wait_send()
            swap_in(last).wait_send()
            for n in range(N_CHIP - 1):
                for t in range(3):
                    to_chip(n, t).wait_send()
            share_in.wait_send()
            share_out.wait_send()
            share_small.wait_send()
            outs = [pltpu.make_async_copy(src, dst, out_sems.at[k]) for k, (src, dst) in enumerate(
                [(gin_out, gin_final), (gout_out, gout_final), (small_out, small_final)])]
            for cp in outs:
                cp.start()
            for cp in outs:
                cp.wait()

    assert n_tiles >= 4, "the reduction's steps are spread over the first four token steps of a pass"
    grid_spec = pltpu.PrefetchScalarGridSpec(
        num_scalar_prefetch=1,
        grid=(N_CHIP, n_tiles),
        in_specs=[
            pl.BlockSpec((D_MODEL, tm), lambda j, i, o: (0, jnp.where(j == 0, i, n_tiles - 1))),
            pl.BlockSpec((tm, SHARD_IN), lambda j, i, o: (i, o[j])),
            VMEM] + [VMEM] * len(small_parts),
        out_specs=[ANY, ANY, ANY],
        scratch_shapes=[
            pltpu.VMEM((D_MODEL, seq), BF16),
            pltpu.VMEM((2, D_MODEL, SHARD_IN), F32),
            pltpu.VMEM((2, HALF_IN, SHARD_IN), F32),
            pltpu.VMEM((N_CHIP, HALF_OUT, D_MODEL), F32),
            pltpu.VMEM((SMALL_ROWS, GROUP), F32),
            pltpu.VMEM((SMALL_ROWS, GROUP), F32),
            pltpu.VMEM((N_CHIP - 1, HALF_IN, SHARD_IN), BF16),
            pltpu.VMEM((N_CHIP - 1, HALF_OUT, D_MODEL), BF16),
            pltpu.VMEM((N_CHIP - 1, HALF_IN, SHARD_IN), BF16),
            pltpu.VMEM((N_CHIP - 1, HALF_OUT, D_MODEL), BF16),
            pltpu.VMEM((N_CHIP, half_small, GROUP), F32),
            pltpu.VMEM((D_MODEL, SHARD_IN), F32),
            pltpu.VMEM((SHARD_OUT, D_MODEL), F32),
            pltpu.VMEM((SMALL_ROWS, GROUP), F32),
            pltpu.SemaphoreType.DMA((18,)),
            pltpu.SemaphoreType.DMA((18,)),
            pltpu.SemaphoreType.DMA((3,)),
        ],
    )
    return pl.pallas_call(
        body,
        name="gw_reduce",
        grid_spec=grid_spec,
        out_shape=(
            jax.ShapeDtypeStruct((D_MODEL, SHARD_IN), F32),
            jax.ShapeDtypeStruct((SHARD_OUT, D_MODEL), F32),
            jax.ShapeDtypeStruct((SMALL_ROWS, GROUP), F32),
        ),
        compiler_params=pltpu.CompilerParams(
            dimension_semantics=("arbitrary", "arbitrary"), vmem_limit_bytes=VMEM_LIMIT),
    )(order, ht, dproj, gwout, *small_parts)


def _adam_update(w, grad, m, v):
    m_new = ADAM_B1 * m + (1.0 - ADAM_B1) * grad
    v_new = ADAM_B2 * v + (1.0 - ADAM_B2) * (grad * grad)
    m_hat = m_new / (1.0 - ADAM_B1 ** ADAM_STEP)
    v_hat = v_new / (1.0 - ADAM_B2 ** ADAM_STEP)
    delta = -ADAM_LR * (m_hat / (jnp.sqrt(v_hat) + ADAM_EPS) + ADAM_WD * w)
    return delta, m_new, v_new


def _adamw(name, w, g, m, v, block_rows):
    rows, cols = w.shape

    def body(w_ref, g_ref, m_ref, v_ref, d_ref, m_out, v_out):
        d_ref[...], m_out[...], v_out[...] = _adam_update(w_ref[...], g_ref[...], m_ref[...], v_ref[...])

    blk = pl.BlockSpec((block_rows, cols), lambda i: (i, 0))
    shape = jax.ShapeDtypeStruct((rows, cols), F32)
    return pl.pallas_call(
        body,
        name=name,
        grid=(rows // block_rows,),
        in_specs=[blk] * 4,
        out_specs=[blk] * 3,
        out_shape=[shape] * 3,
        compiler_params=pltpu.CompilerParams(dimension_semantics=("arbitrary",)),
    )(w, g, m, v)


def _adamw_small(g_small, params, m_state, v_state):
    n_param = len(params)
    n_head = params[3].shape[0]

    def body(g_ref, *refs):
        w_refs, m_refs, v_refs = (refs[k * n_param:(k + 1) * n_param] for k in range(3))
        g_outs, d_outs, m_outs, v_outs = (refs[k * n_param:(k + 1) * n_param] for k in range(3, 7))

        def update(i, packed_rows, rows, cols, packed_cols=slice(None)):
            grad = g_ref[packed_rows, packed_cols]
            g_outs[i][rows, cols] = grad
            d_outs[i][rows, cols], m_outs[i][rows, cols], v_outs[i][rows, cols] = _adam_update(
                w_refs[i][rows, cols], grad, m_refs[i][rows, cols], v_refs[i][rows, cols])

        def lane_group(r):
            return slice(r * GROUP, (r + 1) * GROUP)

        everything = slice(None)
        for r in range(D_MODEL // GROUP):
            update(0, slice(r, r + 1), everything, lane_group(r))
            update(4, slice(FINAL_GAIN_ROW + r, FINAL_GAIN_ROW + r + 1), everything, lane_group(r))
        update(1, slice(POOL_W_ROW, POOL_W_ROW + len(POOL_WINDOWS) * GROUP), everything, everything)
        for r in range(POOL_WIDTH // GROUP):
            update(2, slice(POOL_SCALE_ROW + r, POOL_SCALE_ROW + r + 1), everything, lane_group(r))
        for h in range(n_head):
            first = REL_BIAS_ROW + REL_BIAS_ROWS_PER_HEAD * h
            update(3, slice(first, first + 1), slice(h, h + 1), slice(0, GROUP))
            update(3, slice(first + 1, first + 2), slice(h, h + 1), slice(GROUP, N_REL), slice(0, N_REL - GROUP))

    shapes = [jax.ShapeDtypeStruct(p.shape, F32) for p in params]
    outs = pl.pallas_call(
        body,
        name="adamw_small",
        out_shape=shapes * 4,
    )(g_small, *params, *m_state, *v_state)
    return [outs[k * n_param:(k + 1) * n_param] for k in range(4)]


def kernel(x, norm_gain, w_in, pool_w, pool_scale, rel_bias, w_out, final_norm_gain, loss_target, m_norm_gain, m_w_in, m_pool_w, m_pool_scale, m_rel_bias, m_w_out, m_final_norm_gain, v_norm_gain, v_w_in, v_pool_w, v_pool_scale, v_rel_bias, v_w_out, v_final_norm_gain):
    assert x.shape[1] % TS == 0 and x.shape[2] == D_MODEL
    xs = x[0]
    tgt = loss_target[0]
    g1 = norm_gain.reshape(1, D_MODEL)
    g2 = final_norm_gain.reshape(1, D_MODEL)
    ps = pool_scale.reshape(1, POOL_WIDTH)
    pw_bf = pool_w[0].astype(BF16)

    proj, win_full, wout_full = _gather_inproj(xs, g1, w_in[0], w_out[0])
    bias_table = _bias_table(rel_bias[0])

    a, ya, e_all, inv_all = _attn_fwd(proj, bias_table)
    dx2, dy, gwout, gg2, loss_rows = _out_loss(xs, tgt, proj, ya, wout_full, g2, pw_bf, ps)
    dq, dk, dv, dag, bins = _attn_bwd(proj, a, dy, e_all, inv_all)
    gx, dproj, ht, gg1, gps, gpw = _inproj_bwd(xs, dx2, dy, proj, dq, dk, dv, dag, win_full, pw_bf, g1, ps)

    g_win, g_wout, g_small = _gw_reduce(ht, dproj, gwout, (gg1, gpw, gps, bins, gg2, loss_rows))
    loss = g_small[LOSS_ROW, 0]

    def small_views(norm, pool, scale, bias, final):
        return [norm.reshape(1, D_MODEL), pool.reshape(len(POOL_WINDOWS) * GROUP, GROUP), scale.reshape(1, POOL_WIDTH),
                bias.reshape(-1, N_REL), final.reshape(1, D_MODEL)]

    d_win, m_win, v_win = _adamw("adamw_w_in", w_in[0], g_win, m_w_in[0], v_w_in[0], 256)
    d_wout, m_wout, v_wout = _adamw("adamw_w_out", w_out[0], g_wout, m_w_out[0], v_w_out[0], 128)
    small_results = _adamw_small(
        g_small,
        small_views(norm_gain, pool_w, pool_scale, rel_bias, final_norm_gain),
        small_views(m_norm_gain, m_pool_w, m_pool_scale, m_rel_bias, m_final_norm_gain),
        small_views(v_norm_gain, v_pool_w, v_pool_scale, v_rel_bias, v_final_norm_gain))

    def full(win_part, wout_part, small_parts):
        norm, pool, scale, bias, final = small_parts
        return [norm.reshape(norm_gain.shape), win_part[None], pool.reshape(pool_w.shape), scale.reshape(pool_scale.shape),
                bias.reshape(rel_bias.shape), wout_part[None], final.reshape(final_norm_gain.shape)]

    grads = full(g_win, g_wout, small_results[0])
    deltas = full(d_win, d_wout, small_results[1])
    new_m = full(m_win, m_wout, small_results[2])
    new_v = full(v_win, v_wout, small_results[3])
    return (loss, gx[None], *grads, *deltas, *new_m, *new_v)
```

```python
import numpy as np
import jax
import jax.numpy as jnp
from jax import lax
from jax.experimental import pallas as pl
from jax.experimental.pallas import tpu as pltpu

F32 = jnp.float32
BF16 = jnp.bfloat16

D_MODEL = 1024
POOL_WIDTH = 512
ATTN_WIDTH = 512
POOL_WINDOWS = (2, 4, 8, 16)
GROUP = 128
CHUNK = 64
LEFT_CHUNKS = 8
BAND = (LEFT_CHUNKS + 1) * CHUNK
HEAD_DIM = 64
N_PAIR = 4
MAX_REL = 64
N_REL = 2 * MAX_REL + 1
EPS = 1e-6
MASK_VALUE = -1e30
SCALE = 0.125
LOG2E = 1.4426950408889634

ADAM_LR = 0.001
ADAM_B1 = 0.9
ADAM_B2 = 0.999
ADAM_EPS = 1e-08
ADAM_WD = 0.01
ADAM_STEP = 10

TS = LEFT_CHUNKS * CHUNK
SUPER = 2 * CHUNK
WINDOW = BAND + CHUNK
TA = 4 * TS
WIN_BASE = TA - LEFT_CHUNKS * CHUNK
SUPERS_PER_TILE = TA // SUPER
PAIR_LANES = 2 * SUPER
HALO = 16
N_CHIP = 4
SHARD_IN = 768
SHARD_OUT = 256
PIECE = 256
PIECES_PER_SHARD = SHARD_IN // PIECE
HALF_IN = D_MODEL // 2
HALF_OUT = SHARD_OUT // 2
SMALL_ROWS = 560
POOL_W_ROW = 8
POOL_SCALE_ROW = 520
REL_BIAS_ROW = 528
REL_BIAS_ROWS_PER_HEAD = 2
FINAL_GAIN_ROW = 544
LOSS_ROW = 552
VMEM_LIMIT = 60 * 1024 * 1024

MESH = pl.DeviceIdType.MESH
ANY = pl.BlockSpec(memory_space=pl.ANY)
VMEM = pl.BlockSpec(memory_space=pltpu.VMEM)


def _sigmoid(x):
    return 1.0 / (1.0 + jnp.exp(-x))


def _dot(a, b):
    return jnp.dot(a, b, preferred_element_type=F32)


def _dot_t(a, b):
    return lax.dot_general(a, b, (((1,), (1,)), ((), ())), preferred_element_type=F32)


def _tdot(a, b):
    return lax.dot_general(a, b, (((0,), (0,)), ((), ())), preferred_element_type=F32)


def _my_place():
    return lax.axis_index("x"), lax.axis_index("y"), lax.axis_index("c")


def _other_chips(x, y):
    places = [(1 - x, y), (x, 1 - y), (1 - x, 1 - y)]
    return [(p, 2 * p[0] + p[1]) for p in places]


OWN_SLOT = 2


def _gather_inproj(x, g1, win_sh, wout_sh):
    seq = x.shape[0]
    tm = min(seq, 4 * TS)
    n_tiles = seq // tm
    cx, cy = lax.axis_index("x"), lax.axis_index("y")
    order = jnp.stack([2 * cx + cy] + [chip for _, chip in _other_chips(cx, cy)]).astype(jnp.int32)

    def body(order_ref, x_ref, g1_ref, win_ref, wout_ref, proj_bf_ref, win_full, wout_full,
             hbuf, wbuf, wout_own, win_f32, wout_f32, send_sems, recv_sems, local_sems):
        j = pl.program_id(0)
        i = pl.program_id(1)
        x_, y_, c = _my_place()
        b = 2 * x_ + y_
        sibling = (x_, y_, 1 - c)
        others = _other_chips(x_, y_)

        def halves(chip, core):
            return (
                win_full.at[chip, pl.ds(core * HALF_IN, HALF_IN)],
                wout_full.at[chip, pl.ds(core * HALF_OUT, HALF_OUT)],
            )

        def copy(k, src, dst, to):
            return pltpu.make_async_remote_copy(
                src_ref=src, dst_ref=dst, send_sem=send_sems.at[k], recv_sem=recv_sems.at[k],
                device_id=to, device_id_type=MESH)

        own = [
            pltpu.make_async_copy(wbuf.at[OWN_SLOT], win_full.at[b], local_sems.at[0]),
            pltpu.make_async_copy(wout_own, wout_full.at[b], local_sems.at[1]),
        ]
        mine_src = (wbuf.at[OWN_SLOT, pl.ds(c * HALF_IN, HALF_IN)], wout_own.at[pl.ds(c * HALF_OUT, HALF_OUT)])

        def direct(n, t):
            return copy(2 * n + t, mine_src[t], halves(b, c)[t], (*others[n][0], c))

        def arrival(n, t):
            landed = halves(others[n][1], c)[t]
            return copy(2 * n + t, landed, landed, (*others[n][0], c))

        def passing(n, t):
            landed = halves(others[n][1], c)[t]
            return copy(6 + 2 * n + t, landed, landed, sibling)

        def from_sibling(n, t):
            landed = halves(others[n][1], 1 - c)[t]
            return copy(6 + 2 * n + t, landed, landed, sibling)

        @pl.when((j == 0) & (i == 0))
        def _():
            staged = [pltpu.make_async_copy(win_ref, win_f32, local_sems.at[2]),
                      pltpu.make_async_copy(wout_ref, wout_f32, local_sems.at[3])]
            for cp in staged:
                cp.start()
            for t in range(2):
                staged[t].wait()
                if t == 0:
                    wbuf[OWN_SLOT] = win_f32[...].astype(BF16)
                else:
                    wout_own[...] = wout_f32[...].astype(BF16)
                own[t].start()
                for n in range(N_CHIP - 1):
                    direct(n, t).start()

        def load(n):
            return pltpu.make_async_copy(win_full.at[others[n][1]], wbuf.at[(n + 1) % 2], local_sems.at[2])

        for n in range(N_CHIP - 1):
            @pl.when((j == n) & (i == n_tiles - 1))
            def _(n=n):
                arrival(n, 0).wait_recv()
                passing(n, 0).start()

            @pl.when((j == n + 1) & (i == 0))
            def _(n=n):
                load(n).wait()

        rows = pl.ds(pl.multiple_of(i * tm, tm), tm)

        @pl.when(j == 0)
        def _():
            xt = x_ref[...]
            r = lax.rsqrt(jnp.mean(xt * xt, axis=-1, keepdims=True) + EPS)
            hbuf[rows] = ((xt * r) * g1_ref[...]).astype(BF16)

        proj_bf_ref[...] = _dot(hbuf[rows], wbuf[jnp.where(j == 0, OWN_SLOT, j % 2)]).astype(BF16)

        for n in range(N_CHIP - 1):
            @pl.when((j == n) & (i == n_tiles - 1))
            def _(n=n):
                from_sibling(n, 0).wait_recv()
                load(n).start()

        @pl.when((j == N_CHIP - 1) & (i == n_tiles - 1))
        def _():
            for n in range(N_CHIP - 1):
                arrival(n, 1).wait_recv()
                passing(n, 1).start()
            for n in range(N_CHIP - 1):
                from_sibling(n, 1).wait_recv()
            for n in range(N_CHIP - 1):
                for t in range(2):
                    direct(n, t).wait_send()
                    passing(n, t).wait_send()
            for cp in own:
                cp.wait()

    last = n_tiles - 1
    grid_spec = pltpu.PrefetchScalarGridSpec(
        num_scalar_prefetch=1,
        grid=(N_CHIP, n_tiles),
        in_specs=[
            pl.BlockSpec((tm, D_MODEL), lambda j, i, o: (jnp.where(j == 0, i, last), 0)),
            VMEM, ANY, ANY],
        out_specs=[pl.BlockSpec((tm, SHARD_IN), lambda j, i, o: (i, o[j])), ANY, ANY],
        scratch_shapes=[
            pltpu.VMEM((seq, D_MODEL), BF16), pltpu.VMEM((OWN_SLOT + 1, D_MODEL, SHARD_IN), BF16),
            pltpu.VMEM((SHARD_OUT, D_MODEL), BF16), pltpu.VMEM((D_MODEL, SHARD_IN), F32), pltpu.VMEM((SHARD_OUT, D_MODEL), F32),
            pltpu.SemaphoreType.DMA((12,)), pltpu.SemaphoreType.DMA((12,)), pltpu.SemaphoreType.DMA((4,))],
    )
    return pl.pallas_call(
        body,
        name="gather_inproj",
        grid_spec=grid_spec,
        out_shape=(
            jax.ShapeDtypeStruct((seq, N_CHIP * SHARD_IN), BF16),
            jax.ShapeDtypeStruct((N_CHIP, D_MODEL, SHARD_IN), BF16),
            jax.ShapeDtypeStruct((N_CHIP, SHARD_OUT, D_MODEL), BF16),
        ),
        compiler_params=pltpu.CompilerParams(
            dimension_semantics=("arbitrary", "arbitrary"), vmem_limit_bytes=VMEM_LIMIT),
    )(order, x, g1, win_sh, wout_sh)


def _window_sums(ext, forward):
    n = ext.shape[0]
    sums = []
    acc = ext
    for step in (1, 2, 4, 8):
        acc = acc + pltpu.roll(acc, (n - step) if forward else step, 0)
        sums.append(acc)
    return sums


def _row_counts(tile, rows):
    t = tile * rows + lax.broadcasted_iota(jnp.int32, (rows, GROUP), 0)
    return [jnp.minimum(t + 1, w).astype(F32) for w in POOL_WINDOWS]


def _pool_diffs(pv, prev_rows, tile):
    ext = jnp.concatenate([prev_rows, pv], axis=0)
    sums = _window_sums(ext, forward=False)
    counts = _row_counts(tile, pv.shape[0])
    out = []
    for g in range(len(POOL_WINDOWS)):
        cols = slice(g * GROUP, (g + 1) * GROUP)
        out.append(sums[g][HALO:, cols] / counts[g] - pv[:, cols])
    return out


def _pool_mix(diffs, pw_ref):
    return jnp.concatenate([_dot(diffs[g], pw_ref[g].astype(BF16)) for g in range(len(POOL_WINDOWS))], axis=1)


Q_BLOCK, K_BLOCK, V_BLOCK, AG_BLOCK = 8, 12, 16, 20


BIAS_TABLE = WINDOW + GROUP


def _bias_table(rel_bias):
    far = jnp.broadcast_to(rel_bias[:, :1], (rel_bias.shape[0], BAND - CHUNK - 1))
    by_distance = jnp.concatenate([far, rel_bias[:, :2 * MAX_REL]], axis=1)
    table = jnp.pad(by_distance, ((0, 0), (CHUNK, BIAS_TABLE - CHUNK - by_distance.shape[1])))
    return table.reshape(N_PAIR, 2, BIAS_TABLE)


def _bias_tile(table_ref):
    query = lax.broadcasted_iota(jnp.int32, (CHUNK, GROUP), 0)
    lane = lax.broadcasted_iota(jnp.int32, (CHUNK, GROUP), 1)
    key = lax.broadcasted_iota(jnp.int32, (CHUNK, WINDOW), 1)

    def skewed(table, offset):
        rotated = [pltpu.roll(jnp.broadcast_to(table[:, lane_group(m)], (CHUNK, GROUP)), (-offset) % GROUP, 1,
                              stride=1, stride_axis=0) for m in range(BIAS_TABLE // GROUP)]
        in_first = offset - query + lane < GROUP
        return jnp.concatenate(
            [jnp.where(in_first, rotated[m], rotated[m + 1]) for m in range(WINDOW // GROUP)], axis=1)

    def lane_group(m):
        return slice(m * GROUP, (m + 1) * GROUP)

    rows = []
    for h in range(2):
        table = table_ref[0, h:h + 1, :] * LOG2E
        rows.append(jnp.where(key < BAND, skewed(table, CHUNK + CHUNK - 1), MASK_VALUE))
        rows.append(jnp.where(key >= CHUNK, skewed(table, CHUNK - 1), MASK_VALUE))
    return jnp.concatenate(rows, axis=0).T


def _rel_index_tile():
    j = np.arange(WINDOW)[:, None]
    q = np.arange(PAIR_LANES)[None, :] % SUPER
    band_key = j - CHUNK * (q // CHUNK)
    idx = np.clip(band_key - LEFT_CHUNKS * CHUNK - q % CHUNK, -MAX_REL, MAX_REL) + MAX_REL
    return np.where((band_key >= 0) & (band_key < BAND), idx, -1).astype(np.int32)


def _by_head(block):
    low = lax.broadcasted_iota(jnp.int32, block.shape, 1) < HEAD_DIM
    zero = jnp.zeros_like(block)
    return jnp.concatenate([jnp.where(low, block, zero), jnp.where(low, zero, block)], axis=0)


def _own_head_rows(cross):
    head_of_row = lax.broadcasted_iota(jnp.int32, cross.shape, 0) >= HEAD_DIM
    head_of_lane = lax.broadcasted_iota(jnp.int32, cross.shape, 1) >= SUPER
    both = jnp.where(jnp.logical_xor(head_of_row, head_of_lane), 0.0, cross).T
    return both[:SUPER] + both[SUPER:]


def _with_mask_lane(q_rows):
    lane = lax.broadcasted_iota(jnp.int32, q_rows.shape, 1)
    return jnp.concatenate([q_rows, jnp.where(lane == 0, MASK_VALUE, 0.0).astype(q_rows.dtype)], axis=1)


def _band_exp(kb, q_rows, bias):
    s = _dot_t(kb, _with_mask_lane(q_rows)) + bias
    e = jnp.exp2(s - jnp.max(s, axis=0, keepdims=True))
    return e, jnp.sum(e, axis=0, keepdims=True)


def _window(sc):
    return slice(WIN_BASE + sc * SUPER, WIN_BASE + sc * SUPER + WINDOW)


def _shift_band(i, band_ref, new_ref):
    @pl.when(i == 0)
    def _():
        band_ref[:TA] = jnp.zeros((TA, GROUP), band_ref.dtype)

    @pl.when(i > 0)
    def _():
        band_ref[:TA] = band_ref[TA:]

    band_ref[TA:] = new_ref[...].astype(band_ref.dtype)


def _shift_key_band(i, band_ref, new_ref):
    @pl.when(i == 0)
    def _():
        lane = lax.broadcasted_iota(jnp.int32, (TA, 2 * GROUP), 1)
        band_ref[:TA] = jnp.where(lane == GROUP, 1.0, 0.0).astype(band_ref.dtype)
        band_ref[TA:, GROUP:] = jnp.zeros((TA, GROUP), band_ref.dtype)

    @pl.when(i > 0)
    def _():
        band_ref[:TA] = band_ref[TA:]

    band_ref[TA:, :GROUP] = new_ref[...].astype(band_ref.dtype)


def _shift_band_t(i, band_ref, new_ref):
    @pl.when(i == 0)
    def _():
        band_ref[:, :TA] = jnp.zeros((GROUP, TA), band_ref.dtype)

    @pl.when(i > 0)
    def _():
        band_ref[:, :TA] = band_ref[:, TA:]

    band_ref[:, TA:] = new_ref[...].astype(band_ref.dtype).T


def _scaled_queries(q_ref, rows, scale=SCALE):
    return _by_head((q_ref[rows] * scale).astype(BF16))


def _attn_fwd(proj_bf, bias_table):
    seq = proj_bf.shape[0]
    n_tiles = seq // TA

    def body(q_ref, k_ref, v_ref, ag_ref, table_ref, a_ref, ya_ref, e_ref, inv_ref, kband, vband_t, bias_ref):
        i = pl.program_id(1)
        _shift_key_band(i, kband, k_ref)
        _shift_band_t(i, vband_t, v_ref)

        @pl.when(i == 0)
        def _():
            bias_ref[...] = _bias_tile(table_ref)

        def weights(sc):
            rows = slice(sc * SUPER, (sc + 1) * SUPER)
            win = _window(sc)
            e, total = _band_exp(kband[win], _scaled_queries(q_ref, rows, SCALE * LOG2E), bias_ref[...])
            e_ref[0, sc] = e.astype(BF16)
            inv_total = 1.0 / total
            inv_ref[0, sc] = jnp.broadcast_to(inv_total, (8, PAIR_LANES))
            return inv_total

        nxt = weights(0)
        for sc in range(SUPERS_PER_TILE):
            rows = slice(sc * SUPER, (sc + 1) * SUPER)
            win = _window(sc)
            inv_total = nxt
            if sc + 1 < SUPERS_PER_TILE:
                nxt = weights(sc + 1)
            a = _own_head_rows(_dot(vband_t[:, win], e_ref[0, sc]) * inv_total)
            a_ref[rows] = a.astype(BF16)
            g = ag_ref[rows].astype(F32)
            ya_ref[rows] = (a * (g * _sigmoid(g))).astype(BF16)

    blk = pl.BlockSpec((TA, GROUP), lambda p, i: (i, p))

    def cols(first):
        return pl.BlockSpec((TA, GROUP), lambda p, i: (i, first + p))

    return pl.pallas_call(
        body,
        name="attn_fwd",
        grid=(N_PAIR, n_tiles),
        in_specs=[cols(Q_BLOCK), cols(K_BLOCK), cols(V_BLOCK), cols(AG_BLOCK),
                  pl.BlockSpec((1, 2, BIAS_TABLE), lambda p, i: (p, 0, 0))],
        out_specs=[blk, blk,
                   pl.BlockSpec((1, SUPERS_PER_TILE, WINDOW, PAIR_LANES), lambda p, i: (p, i, 0, 0)),
                   pl.BlockSpec((1, SUPERS_PER_TILE, 8, PAIR_LANES), lambda p, i: (p, i, 0, 0))],
        out_shape=[jax.ShapeDtypeStruct((seq, ATTN_WIDTH), BF16), jax.ShapeDtypeStruct((seq, ATTN_WIDTH), BF16),
                   jax.ShapeDtypeStruct((N_PAIR, seq // SUPER, WINDOW, PAIR_LANES), BF16),
                   jax.ShapeDtypeStruct((N_PAIR, seq // SUPER, 8, PAIR_LANES), F32)],
        scratch_shapes=[pltpu.VMEM((2 * TA, 2 * GROUP), BF16), pltpu.VMEM((GROUP, 2 * TA), BF16),
                        pltpu.VMEM((WINDOW, PAIR_LANES), F32)],
        compiler_params=pltpu.CompilerParams(
            dimension_semantics=("arbitrary", "arbitrary"), vmem_limit_bytes=VMEM_LIMIT),
    )(proj_bf, proj_bf, proj_bf, proj_bf, bias_table)


BIN_ROWS = 136


def _bias_bin_sums(db_ref, idx_ref):
    lane = lax.broadcasted_iota(jnp.int32, (1, GROUP), 1)
    row = lax.broadcasted_iota(jnp.int32, (BIN_ROWS, GROUP), 0)
    out = jnp.zeros((BIN_ROWS, GROUP), F32)
    for r in range(N_REL - 1):
        lo = 0 if r == 0 else ((BAND - 2 * CHUNK + r) // 8) * 8
        hi = WINDOW if r == 0 else min(WINDOW, lo + SUPER + 8)
        hit = jnp.where(idx_ref[lo:hi] == r, db_ref[lo:hi], 0.0)
        col = jnp.sum(hit, axis=0, keepdims=True)
        s0 = jnp.sum(col[:, :SUPER], axis=1, keepdims=True)
        s1 = jnp.sum(col[:, SUPER:], axis=1, keepdims=True)
        val = jnp.where(lane == 0, s0, jnp.where(lane == 1, s1, 0.0))
        out = jnp.where(row == r, val, out)
    return out


def _attn_bwd(proj_bf, a, dy, e_all, inv_all):
    seq = proj_bf.shape[0]
    n_tiles = seq // TA

    def body(q_ref, k_ref, v_ref, a_ref, ag_ref, dy_ref, e_ref, inv_ref, idx_ref,
             dq_ref, dk_ref, dv_ref, dag_ref, bins_ref, vband, kband_t, dkacc, dvacc, db_ref):
        i = pl.program_id(1)

        @pl.when(i == 0)
        def _():
            dkacc[...] = jnp.zeros_like(dkacc)
            dvacc[...] = jnp.zeros_like(dvacc)
            db_ref[...] = jnp.zeros_like(db_ref)

        @pl.when(i < n_tiles)
        def _():
            _shift_band(i, vband, v_ref)
            _shift_band_t(i, kband_t, k_ref)

            def score_grads(sc):
                rows = slice(sc * SUPER, (sc + 1) * SUPER)
                win = _window(sc)
                q_rows = _scaled_queries(q_ref, rows)
                g = ag_ref[rows].astype(F32)
                sg = _sigmoid(g)
                dyc = dy_ref[rows].astype(F32)
                dag_ref[rows] = (dyc * a_ref[rows].astype(F32) * (sg * (1.0 + g * (1.0 - sg)))).astype(BF16)
                da_rows = _by_head((dyc * (g * sg)).astype(BF16))
                p = e_ref[0, sc].astype(F32) * inv_ref[0, sc, :1]
                dp = _dot_t(vband[win], da_rows)
                ds = p * (dp - jnp.sum(p * dp, axis=0, keepdims=True))
                db_ref[...] += ds
                return q_rows, da_rows, p.astype(BF16), ds.astype(BF16)

            nxt = score_grads(0)
            for sc in range(SUPERS_PER_TILE):
                rows = slice(sc * SUPER, (sc + 1) * SUPER)
                win = _window(sc)
                q_rows, da_rows, p_bf, ds_bf = nxt
                if sc + 1 < SUPERS_PER_TILE:
                    nxt = score_grads(sc + 1)
                dq_ref[rows] = (_own_head_rows(_dot(kband_t[:, win], ds_bf)) * SCALE).astype(BF16)
                dkacc[win] += _dot(ds_bf, q_rows)
                dvacc[win] += _dot(p_bf, da_rows)

        dk_ref[...] = dkacc[:TA].astype(BF16)
        dv_ref[...] = dvacc[:TA].astype(BF16)
        dkacc[:TA] = dkacc[TA:]
        dvacc[:TA] = dvacc[TA:]
        dkacc[TA:] = jnp.zeros((TA, GROUP), F32)
        dvacc[TA:] = jnp.zeros((TA, GROUP), F32)

        @pl.when(i == n_tiles)
        def _():
            bins_ref[0] = _bias_bin_sums(db_ref, idx_ref)

    last = n_tiles - 1
    cur = pl.BlockSpec((TA, GROUP), lambda p, i: (jnp.minimum(i, last), p))
    older = pl.BlockSpec((TA, GROUP), lambda p, i: (jnp.maximum(i - 1, 0), p))
    dy_blk = pl.BlockSpec((TA, GROUP), lambda p, i: (jnp.minimum(i, last), N_PAIR + p))
    per_pair = pl.BlockSpec((1, BIN_ROWS, GROUP), lambda p, i: (p, 0, 0))

    def cols(first):
        return pl.BlockSpec((TA, GROUP), lambda p, i: (jnp.minimum(i, last), first + p))

    def kept(rows):
        return pl.BlockSpec((1, SUPERS_PER_TILE, rows, PAIR_LANES), lambda p, i: (p, jnp.minimum(i, last), 0, 0))

    def out(dtype):
        return jax.ShapeDtypeStruct((seq, ATTN_WIDTH), dtype)

    return pl.pallas_call(
        body,
        name="attn_bwd",
        grid=(N_PAIR, n_tiles + 1),
        in_specs=[cols(Q_BLOCK), cols(K_BLOCK), cols(V_BLOCK), cur, cols(AG_BLOCK), dy_blk, kept(WINDOW), kept(8),
                  VMEM],
        out_specs=[cur, older, older, cur, per_pair],
        out_shape=[out(BF16), out(BF16), out(BF16), out(BF16),
                   jax.ShapeDtypeStruct((N_PAIR, BIN_ROWS, GROUP), F32)],
        scratch_shapes=[
            pltpu.VMEM((2 * TA, GROUP), BF16), pltpu.VMEM((GROUP, 2 * TA), BF16),
            pltpu.VMEM((2 * TA, GROUP), F32), pltpu.VMEM((2 * TA, GROUP), F32),
            pltpu.VMEM((WINDOW, PAIR_LANES), F32)],
        compiler_params=pltpu.CompilerParams(
            dimension_semantics=("arbitrary", "arbitrary"), vmem_limit_bytes=VMEM_LIMIT),
    )(proj_bf, proj_bf, proj_bf, a, proj_bf, dy, e_all, inv_all, jnp.asarray(_rel_index_tile()))


def _out_loss(x, tgt, proj, ya, wout_full, g2, pw, ps):
    seq = x.shape[0]
    to = min(seq, 2 * TS)
    n_tiles = seq // to

    def body(x_ref, t_ref, pv_ref, pg_ref, ya_ref, w_ref, g2_ref, pw_ref, ps_ref,
             dx2_ref, dy_ref, gw_ref, gg_ref, loss_ref, sq_ref, halo_ref, wt_ref):
        i = pl.program_id(0)

        @pl.when(i == 0)
        def _():
            gw_ref[...] = jnp.zeros_like(gw_ref)
            gg_ref[...] = jnp.zeros_like(gg_ref)
            sq_ref[...] = jnp.zeros_like(sq_ref)
            halo_ref[...] = jnp.zeros_like(halo_ref)
            for b in range(N_CHIP):
                wt_ref[b] = w_ref[b].T

        pv = pv_ref[...].astype(F32)
        pg = pg_ref[...].astype(F32)
        diffs = [d.astype(BF16) for d in _pool_diffs(pv, halo_ref[...], i)]
        halo_ref[...] = pv[to - HALO:, :]
        yp = ((_pool_mix(diffs, pw_ref) * ps_ref[...]) * (pg * _sigmoid(pg))).astype(BF16)
        g2v = g2_ref[...]

        def parts(rows):
            return [yp[rows, :SHARD_OUT], yp[rows, SHARD_OUT:], ya_ref[rows, :SHARD_OUT], ya_ref[rows, SHARD_OUT:]]

        def project(rows, ys):
            x2 = x_ref[rows]
            for b in range(N_CHIP):
                x2 = x2 + _dot(ys[b], w_ref[b])
            return x2

        def norm_loss(rows, x2):
            r = lax.rsqrt(jnp.mean(x2 * x2, axis=-1, keepdims=True) + EPS)
            xh = x2 * r
            diff = xh * g2v - t_ref[rows]
            sq_ref[...] += jnp.sum(diff * diff, axis=0, keepdims=True)
            dfin = diff * (1.0 / D_MODEL)
            gg_ref[...] += jnp.sum(dfin * xh, axis=0, keepdims=True)
            dxh = dfin * g2v
            dx2 = r * (dxh - xh * jnp.mean(dxh * xh, axis=-1, keepdims=True))
            dx2_ref[rows] = dx2
            return dx2.astype(BF16)

        def back(rows, ys, dx2_bf):
            for b in range(N_CHIP):
                gw_ref[b] += _tdot(ys[b], dx2_bf)
                dy_ref[rows, b * SHARD_OUT:(b + 1) * SHARD_OUT] = _dot(dx2_bf, wt_ref[b]).astype(BF16)

        n_parts = 2
        part = to // n_parts
        spans = [slice(r * part, (r + 1) * part) for r in range(n_parts)]
        ys = [parts(rows) for rows in spans]
        x2_next = project(spans[0], ys[0])
        for r in range(n_parts):
            x2 = x2_next
            if r + 1 < n_parts:
                x2_next = project(spans[r + 1], ys[r + 1])
            back(spans[r], ys[r], norm_loss(spans[r], x2))

        @pl.when(i == n_tiles - 1)
        def _():
            total = jnp.sum(sq_ref[...], axis=1, keepdims=True) * (0.5 / D_MODEL)
            loss_ref[...] = jnp.broadcast_to(total, loss_ref.shape)

    def rows(width, col=0):
        return pl.BlockSpec((to, width), lambda i: (i, col))

    return pl.pallas_call(
        body,
        name="out_loss",
        grid=(n_tiles,),
        in_specs=[rows(D_MODEL), rows(D_MODEL), rows(POOL_WIDTH, 0), rows(POOL_WIDTH, 1), rows(ATTN_WIDTH),
                  VMEM, VMEM, VMEM, VMEM],
        out_specs=[rows(D_MODEL), rows(D_MODEL), VMEM, VMEM, VMEM],
        out_shape=[
            jax.ShapeDtypeStruct((seq, D_MODEL), F32), jax.ShapeDtypeStruct((seq, D_MODEL), BF16),
            jax.ShapeDtypeStruct((N_CHIP, SHARD_OUT, D_MODEL), F32), jax.ShapeDtypeStruct((1, D_MODEL), F32),
            jax.ShapeDtypeStruct((8, GROUP), F32)],
        scratch_shapes=[pltpu.VMEM((1, D_MODEL), F32), pltpu.VMEM((HALO, POOL_WIDTH), F32),
                        pltpu.VMEM((N_CHIP, D_MODEL, SHARD_OUT), BF16)],
        compiler_params=pltpu.CompilerParams(dimension_semantics=("arbitrary",), vmem_limit_bytes=VMEM_LIMIT),
    )(x, tgt, proj, proj, ya, wout_full, g2, pw, ps)


def _inproj_bwd(x, dx2, dy, proj, dq, dk, dv, dag, win_full, pw, g1, ps):
    seq = x.shape[0]
    n_tiles = seq // TS

    def body(x_ref, dx2_ref, dyp_ref, pv_ref, pvprev_ref, pg_ref, dq_ref, dk_ref, dv_ref, dag_ref,
             w_ref, pw_ref, g1_ref, ps_ref, gx_ref, dproj_ref, ht_ref, gg_ref, gps_ref, gpw_ref, halo_ref):
        i = pl.program_id(0)
        tile = n_tiles - 1 - i

        @pl.when(i == 0)
        def _():
            gg_ref[...] = jnp.zeros_like(gg_ref)
            gps_ref[...] = jnp.zeros_like(gps_ref)
            gpw_ref[...] = jnp.zeros_like(gpw_ref)
            halo_ref[...] = jnp.zeros_like(halo_ref)

        pv_t = pv_ref[...].astype(F32)
        pg_t = pg_ref[...].astype(F32)
        prev_rows = jnp.where(tile > 0, pvprev_ref[...].astype(F32), 0.0)
        diffs = [d.astype(BF16) for d in _pool_diffs(pv_t, prev_rows, tile)]
        mixed = _pool_mix(diffs, pw_ref)
        sg = _sigmoid(pg_t)
        silu = pg_t * sg
        dyp = dyp_ref[...].astype(F32)
        psv = ps_ref[...]
        gps_ref[...] += jnp.sum(dyp * mixed * silu, axis=0, keepdims=True)
        dmixed = (dyp * psv * silu).astype(BF16)
        dpg = dyp * (mixed * psv) * (sg * (1.0 + pg_t * (1.0 - sg)))
        counts = _row_counts(tile, TS)
        dds = []
        for g in range(len(POOL_WINDOWS)):
            dm_g = dmixed[:, g * GROUP:(g + 1) * GROUP]
            gpw_ref[g] += _tdot(diffs[g], dm_g)
            dds.append(_dot_t(dm_g, pw_ref[g].astype(BF16)))
        dd = jnp.concatenate(dds, axis=1)
        spread = jnp.concatenate([dds[g] / counts[g] for g in range(len(POOL_WINDOWS))], axis=1)
        sums = _window_sums(jnp.concatenate([spread, halo_ref[...]], axis=0), forward=True)
        halo_ref[...] = spread[:HALO]
        dpv = jnp.concatenate(
            [sums[g][:TS, g * GROUP:(g + 1) * GROUP] for g in range(len(POOL_WINDOWS))], axis=1) - dd

        xt = x_ref[...]
        r = lax.rsqrt(jnp.mean(xt * xt, axis=-1, keepdims=True) + EPS)
        xh = xt * r
        g1v = g1_ref[...]
        ht_ref[...] = (xh * g1v).astype(BF16).T
        dproj = jnp.concatenate(
            [dpv.astype(BF16), dpg.astype(BF16), dq_ref[...], dk_ref[...], dv_ref[...], dag_ref[...]],
            axis=1)
        dproj_ref[...] = dproj
        dh = _dot_t(dproj[:, :SHARD_IN], w_ref[0])
        for chip in range(1, N_CHIP):
            dh = dh + _dot_t(dproj[:, chip * SHARD_IN:(chip + 1) * SHARD_IN], w_ref[chip])

        gg_ref[...] += jnp.sum(dh * xh, axis=0, keepdims=True)
        dxh = dh * g1v
        gx_ref[...] = dx2_ref[...] + r * (dxh - xh * jnp.mean(dxh * xh, axis=-1, keepdims=True))

    def rows(width, col=0):
        return pl.BlockSpec((TS, width), lambda i: (n_tiles - 1 - i, col))

    prev = pl.BlockSpec((HALO, POOL_WIDTH), lambda i: (jnp.maximum((n_tiles - 1 - i) * (TS // HALO) - 1, 0), 0))
    return pl.pallas_call(
        body,
        name="inproj_bwd",
        grid=(n_tiles,),
        in_specs=[rows(D_MODEL), rows(D_MODEL), rows(POOL_WIDTH), rows(POOL_WIDTH), prev, rows(POOL_WIDTH, 1),
                  rows(ATTN_WIDTH), rows(ATTN_WIDTH), rows(ATTN_WIDTH), rows(ATTN_WIDTH), VMEM, VMEM, VMEM, VMEM],
        out_specs=[rows(D_MODEL), rows(N_CHIP * SHARD_IN),
                   pl.BlockSpec((D_MODEL, TS), lambda i: (0, n_tiles - 1 - i)), VMEM, VMEM, VMEM],
        out_shape=[
            jax.ShapeDtypeStruct((seq, D_MODEL), F32),
            jax.ShapeDtypeStruct((seq, N_CHIP * SHARD_IN), BF16),
            jax.ShapeDtypeStruct((D_MODEL, seq), BF16),
            jax.ShapeDtypeStruct((1, D_MODEL), F32),
            jax.ShapeDtypeStruct((1, POOL_WIDTH), F32),
            jax.ShapeDtypeStruct((len(POOL_WINDOWS), GROUP, GROUP), F32)],
        scratch_shapes=[pltpu.VMEM((HALO, POOL_WIDTH), F32)],
        compiler_params=pltpu.CompilerParams(dimension_semantics=("arbitrary",), vmem_limit_bytes=VMEM_LIMIT),
    )(x, dx2, dy, proj, proj, proj, dq, dk, dv, dag, win_full, pw, g1, ps)


PASS_PEER = (2, 0, 1)


def _pack_small(dst, norm_ref, pool_w_ref, pool_scale_ref, bins_ref, final_ref, loss_ref):
    def lane_group(r):
        return slice(r * GROUP, (r + 1) * GROUP)

    dst[...] = jnp.zeros(dst.shape, F32)
    for r in range(D_MODEL // GROUP):
        dst[r:r + 1] = norm_ref[:, lane_group(r)]
        dst[FINAL_GAIN_ROW + r:FINAL_GAIN_ROW + r + 1] = final_ref[:, lane_group(r)]
    for g in range(len(POOL_WINDOWS)):
        dst[POOL_W_ROW + g * GROUP:POOL_W_ROW + (g + 1) * GROUP] = pool_w_ref[g]
    for r in range(POOL_WIDTH // GROUP):
        dst[POOL_SCALE_ROW + r:POOL_SCALE_ROW + r + 1] = pool_scale_ref[:, lane_group(r)]
    first_lane = lax.broadcasted_iota(jnp.int32, (BIN_ROWS - GROUP, GROUP), 1) == 0
    for p in range(N_PAIR):
        by_head = bins_ref[p, :GROUP].T
        last_column = bins_ref[p, GROUP:]
        for h in range(2):
            row = REL_BIAS_ROW + REL_BIAS_ROWS_PER_HEAD * (2 * p + h)
            dst[row:row + 1] = by_head[h:h + 1]
            to_first_lane = last_column if h == 0 else pltpu.roll(last_column, GROUP - h, 1)
            dst[row + 1:row + 2] = jnp.where(first_lane, to_first_lane, 0.0)[:1]
    dst[LOSS_ROW:LOSS_ROW + loss_ref.shape[0]] = loss_ref[...]


def _gw_reduce(ht, dproj, gwout, small_parts):
    seq = ht.shape[1]
    tm = min(2 * TS, seq // 4)
    n_tiles = seq // tm
    half_small = SMALL_ROWS // 2
    cx, cy = lax.axis_index("x"), lax.axis_index("y")
    outer = _other_chips(cx, cy)
    order = jnp.stack([outer[n][1] for n in PASS_PEER] + [2 * cx + cy]).astype(jnp.int32)

    def body(order_ref, ht_ref, dp_ref, gwout_ref, *refs):
        part_refs = refs[:len(small_parts)]
        (gin_final, gout_final, small_final, hbuf, acc, pair_in, pair_out, small_ref, pair_small, tx_in, tx_out, rx_in, rx_out,
         rx_small, gin_out, gout_out, small_out, send_sems, recv_sems, out_sems) = refs[len(small_parts):]
        j = pl.program_id(0)
        i = pl.program_id(1)
        x, y, c = _my_place()
        b = 2 * x + y
        sibling = (x, y, 1 - c)
        others = _other_chips(x, y)
        mine_in = pl.ds(pl.multiple_of(c * HALF_IN, HALF_IN), HALF_IN)
        mine_out = pl.ds(pl.multiple_of(c * HALF_OUT, HALF_OUT), HALF_OUT)
        mine_small = pl.ds(pl.multiple_of(c * half_small, 8), half_small)
        theirs_in = pl.ds(pl.multiple_of((1 - c) * HALF_IN, HALF_IN), HALF_IN)
        theirs_out = pl.ds(pl.multiple_of((1 - c) * HALF_OUT, HALF_OUT), HALF_OUT)
        theirs_small = pl.ds(pl.multiple_of((1 - c) * half_small, 8), half_small)

        def copy(k, src, dst, to):
            return pltpu.make_async_remote_copy(
                src_ref=src, dst_ref=dst, send_sem=send_sems.at[k], recv_sem=recv_sems.at[k],
                device_id=to, device_id_type=MESH)

        swap_out = copy(0, gwout_ref.at[:, theirs_out], pair_out, sibling)
        swap_small = copy(1, small_ref, pair_small, sibling)

        def swap_in(p):
            return copy(2 + p, acc.at[p % 2, theirs_in], pair_in.at[p % 2], sibling)

        def to_chip(n, t):
            to = (*others[n][0], c)
            if t == 0:
                return copy(6 + 3 * n, tx_in.at[n], rx_in.at[n], to)
            if t == 1:
                return copy(7 + 3 * n, tx_out.at[n], rx_out.at[n], to)
            return copy(8 + 3 * n, pair_small.at[mine_small], rx_small.at[b], to)

        share_in = copy(15, gin_out.at[mine_in], gin_out.at[mine_in], sibling)
        share_out = copy(16, gout_out.at[mine_out], gout_out.at[mine_out], sibling)
        share_small = copy(17, small_out.at[mine_small], small_out.at[mine_small], sibling)

        def at(jj, ii):
            return (j == jj) & (i == ii)

        par = j % 2

        @pl.when(i == 0)
        def _():
            acc[par] = jnp.zeros((D_MODEL, SHARD_IN), F32)

        cols = pl.ds(pl.multiple_of(i * tm, tm), tm)

        @pl.when(j == 0)
        def _():
            hbuf[:, cols] = ht_ref[...]

        acc[par] += _dot(hbuf[:, cols], dp_ref[...])

        @pl.when(at(0, 0))
        def _():
            swap_out.start()
            _pack_small(small_ref, *part_refs)
            swap_small.start()

        @pl.when(at(0, 2))
        def _():
            swap_out.wait_recv()
            swap_small.wait_recv()
            for chip in range(N_CHIP):
                pair_out[chip] = gwout_ref[chip, mine_out] + pair_out[chip]
            pair_small[...] = small_ref[...] + pair_small[...]
            rx_small[b] = pair_small[mine_small]
            for n in range(N_CHIP - 1):
                tx_out[n] = pair_out[others[n][1]].astype(BF16)
                to_chip(n, 1).start()
                to_chip(n, 2).start()

        @pl.when(at(1, 3))
        def _():
            total_out = pair_out[b]
            for n in range(N_CHIP - 1):
                to_chip(n, 1).wait_recv()
                copy(8 + 3 * n, pair_small.at[mine_small], rx_small.at[others[n][1]], (*others[n][0], c)).wait_recv()
                total_out = total_out + rx_out[n].astype(F32)
            gout_out[mine_out] = total_out
            small_out[mine_small] = ((rx_small[0] + rx_small[1]) + rx_small[2]) + rx_small[3]
            share_out.start()
            share_small.start()

        for p in range(N_CHIP - 1):
            n = PASS_PEER[p]

            @pl.when(at(p + 1, 0))
            def _(p=p):
                swap_in(p).start()

            @pl.when(at(p + 1, 2))
            def _(p=p, n=n):
                swap_in(p).wait_recv()
                swap_in(p).wait_send()
                tx_in[n] = (acc[p % 2, mine_in] + pair_in[p % 2]).astype(BF16)
                to_chip(n, 0).start()

        @pl.when(at(N_CHIP - 1, n_tiles - 1))
        def _():
            last = N_CHIP - 1
            swap_in(last).start()
            swap_in(last).wait_recv()
            total_in = acc[last % 2, mine_in] + pair_in[last % 2]
            for n in range(N_CHIP - 1):
                to_chip(n, 0).wait_recv()
                total_in = total_in + rx_in[n].astype(F32)
            gin_out[mine_in] = total_in
            share_in.start()
            copy(15, gin_out.at[theirs_in], gin_out.at[theirs_in], sibling).wait_recv()
            copy(16, gout_out.at[theirs_out], gout_out.at[theirs_out], sibling).wait_recv()
            copy(17, small_out.at[theirs_small], small_out.at[theirs_small], sibling).wait_recv()
            swap_out.wait_send()
            swap_small.wait_send()
            swap_in(last).wait_send()
            for n in range(N_CHIP - 1):
                for t in range(3):
                    to_chip(n, t).wait_send()
            share_in.wait_send()
            share_out.wait_send()
            share_small.wait_send()
            outs = [pltpu.make_async_copy(src, dst, out_sems.at[k]) for k, (src, dst) in enumerate(
                [(gin_out, gin_final), (gout_out, gout_final), (small_out, small_final)])]
            for cp in outs:
                cp.start()
            for cp in outs:
                cp.wait()

    assert n_tiles >= 4, "the reduction's steps are spread over the first four token steps of a pass"
    grid_spec = pltpu.PrefetchScalarGridSpec(
        num_scalar_prefetch=1,
        grid=(N_CHIP, n_tiles),
        in_specs=[
            pl.BlockSpec((D_MODEL, tm), lambda j, i, o: (0, jnp.where(j == 0, i, n_tiles - 1))),
            pl.BlockSpec((tm, SHARD_IN), lambda j, i, o: (i, o[j])),
            VMEM] + [VMEM] * len(small_parts),
        out_specs=[ANY, ANY, ANY],
        scratch_shapes=[
            pltpu.VMEM((D_MODEL, seq), BF16),
            pltpu.VMEM((2, D_MODEL, SHARD_IN), F32),
            pltpu.VMEM((2, HALF_IN, SHARD_IN), F32),
            pltpu.VMEM((N_CHIP, HALF_OUT, D_MODEL), F32),
            pltpu.VMEM((SMALL_ROWS, GROUP), F32),
            pltpu.VMEM((SMALL_ROWS, GROUP), F32),
            pltpu.VMEM((N_CHIP - 1, HALF_IN, SHARD_IN), BF16),
            pltpu.VMEM((N_CHIP - 1, HALF_OUT, D_MODEL), BF16),
            pltpu.VMEM((N_CHIP - 1, HALF_IN, SHARD_IN), BF16),
            pltpu.VMEM((N_CHIP - 1, HALF_OUT, D_MODEL), BF16),
            pltpu.VMEM((N_CHIP, half_small, GROUP), F32),
            pltpu.VMEM((D_MODEL, SHARD_IN), F32),
            pltpu.VMEM((SHARD_OUT, D_MODEL), F32),
            pltpu.VMEM((SMALL_ROWS, GROUP), F32),
            pltpu.SemaphoreType.DMA((18,)),
            pltpu.SemaphoreType.DMA((18,)),
            pltpu.SemaphoreType.DMA((3,)),
        ],
    )
    return pl.pallas_call(
        body,
        name="gw_reduce",
        grid_spec=grid_spec,
        out_shape=(
            jax.ShapeDtypeStruct((D_MODEL, SHARD_IN), F32),
            jax.ShapeDtypeStruct((SHARD_OUT, D_MODEL), F32),
            jax.ShapeDtypeStruct((SMALL_ROWS, GROUP), F32),
        ),
        compiler_params=pltpu.CompilerParams(
            dimension_semantics=("arbitrary", "arbitrary"), vmem_limit_bytes=VMEM_LIMIT),
    )(order, ht, dproj, gwout, *small_parts)


def _adam_update(w, grad, m, v):
    m_new = ADAM_B1 * m + (1.0 - ADAM_B1) * grad
    v_new = ADAM_B2 * v + (1.0 - ADAM_B2) * (grad * grad)
    m_hat = m_new / (1.0 - ADAM_B1 ** ADAM_STEP)
    v_hat = v_new / (1.0 - ADAM_B2 ** ADAM_STEP)
    delta = -ADAM_LR * (m_hat / (jnp.sqrt(v_hat) + ADAM_EPS) + ADAM_WD * w)
    return delta, m_new, v_new


def _adamw(name, w, g, m, v, block_rows):
    rows, cols = w.shape

    def body(w_ref, g_ref, m_ref, v_ref, d_ref, m_out, v_out):
        d_ref[...], m_out[...], v_out[...] = _adam_update(w_ref[...], g_ref[...], m_ref[...], v_ref[...])

    blk = pl.BlockSpec((block_rows, cols), lambda i: (i, 0))
    shape = jax.ShapeDtypeStruct((rows, cols), F32)
    return pl.pallas_call(
        body,
        name=name,
        grid=(rows // block_rows,),
        in_specs=[blk] * 4,
        out_specs=[blk] * 3,
        out_shape=[shape] * 3,
        compiler_params=pltpu.CompilerParams(dimension_semantics=("arbitrary",)),
    )(w, g, m, v)


def _adamw_small(g_small, params, m_state, v_state):
    n_param = len(params)
    n_head = params[3].shape[0]

    def body(g_ref, *refs):
        w_refs, m_refs, v_refs = (refs[k * n_param:(k + 1) * n_param] for k in range(3))
        g_outs, d_outs, m_outs, v_outs = (refs[k * n_param:(k + 1) * n_param] for k in range(3, 7))

        def update(i, packed_rows, rows, cols, packed_cols=slice(None)):
            grad = g_ref[packed_rows, packed_cols]
            g_outs[i][rows, cols] = grad
            d_outs[i][rows, cols], m_outs[i][rows, cols], v_outs[i][rows, cols] = _adam_update(
                w_refs[i][rows, cols], grad, m_refs[i][rows, cols], v_refs[i][rows, cols])

        def lane_group(r):
            return slice(r * GROUP, (r + 1) * GROUP)

        everything = slice(None)
        for r in range(D_MODEL // GROUP):
            update(0, slice(r, r + 1), everything, lane_group(r))
            update(4, slice(FINAL_GAIN_ROW + r, FINAL_GAIN_ROW + r + 1), everything, lane_group(r))
        update(1, slice(POOL_W_ROW, POOL_W_ROW + len(POOL_WINDOWS) * GROUP), everything, everything)
        for r in range(POOL_WIDTH // GROUP):
            update(2, slice(POOL_SCALE_ROW + r, POOL_SCALE_ROW + r + 1), everything, lane_group(r))
        for h in range(n_head):
            first = REL_BIAS_ROW + REL_BIAS_ROWS_PER_HEAD * h
            update(3, slice(first, first + 1), slice(h, h + 1), slice(0, GROUP))
            update(3, slice(first + 1, first + 2), slice(h, h + 1), slice(GROUP, N_REL), slice(0, N_REL - GROUP))

    shapes = [jax.ShapeDtypeStruct(p.shape, F32) for p in params]
    outs = pl.pallas_call(
        body,
        name="adamw_small",
        out_shape=shapes * 4,
    )(g_small, *params, *m_state, *v_state)
    return [outs[k * n_param:(k + 1) * n_param] for k in range(4)]


def kernel(x, norm_gain, w_in, pool_w, pool_scale, rel_bias, w_out, final_norm_gain, loss_target, m_norm_gain, m_w_in, m_pool_w, m_pool_scale, m_rel_bias, m_w_out, m_final_norm_gain, v_norm_gain, v_w_in, v_pool_w, v_pool_scale, v_rel_bias, v_w_out, v_final_norm_gain):
    assert x.shape[1] % TS == 0 and x.shape[2] == D_MODEL
    xs = x[0]
    tgt = loss_target[0]
    g1 = norm_gain.reshape(1, D_MODEL)
    g2 = final_norm_gain.reshape(1, D_MODEL)
    ps = pool_scale.reshape(1, POOL_WIDTH)
    pw = pool_w[0]

    proj, win_full, wout_full = _gather_inproj(xs, g1, w_in[0], w_out[0])
    bias_table = _bias_table(rel_bias[0])

    a, ya, e_all, inv_all = _attn_fwd(proj, bias_table)
    dx2, dy, gwout, gg2, loss_rows = _out_loss(xs, tgt, proj, ya, wout_full, g2, pw, ps)
    dq, dk, dv, dag, bins = _attn_bwd(proj, a, dy, e_all, inv_all)
    gx, dproj, ht, gg1, gps, gpw = _inproj_bwd(xs, dx2, dy, proj, dq, dk, dv, dag, win_full, pw, g1, ps)

    g_win, g_wout, g_small = _gw_reduce(ht, dproj, gwout, (gg1, gpw, gps, bins, gg2, loss_rows))
    loss = g_small[LOSS_ROW, 0]

    def small_views(norm, pool, scale, bias, final):
        return [norm.reshape(1, D_MODEL), pool.reshape(len(POOL_WINDOWS) * GROUP, GROUP), scale.reshape(1, POOL_WIDTH),
                bias.reshape(-1, N_REL), final.reshape(1, D_MODEL)]

    d_win, m_win, v_win = _adamw("adamw_w_in", w_in[0], g_win, m_w_in[0], v_w_in[0], 256)
    d_wout, m_wout, v_wout = _adamw("adamw_w_out", w_out[0], g_wout, m_w_out[0], v_w_out[0], 128)
    small_results = _adamw_small(
        g_small,
        small_views(norm_gain, pool_w, pool_scale, rel_bias, final_norm_gain),
        small_views(m_norm_gain, m_pool_w, m_pool_scale, m_rel_bias, m_final_norm_gain),
        small_views(v_norm_gain, v_pool_w, v_pool_scale, v_rel_bias, v_final_norm_gain))

    def full(win_part, wout_part, small_parts):
        norm, pool, scale, bias, final = small_parts
        return [norm.reshape(norm_gain.shape), win_part[None], pool.reshape(pool_w.shape), scale.reshape(pool_scale.shape),
                bias.reshape(rel_bias.shape), wout_part[None], final.reshape(final_norm_gain.shape)]

    grads = full(g_win, g_wout, small_results[0])
    deltas = full(d_win, d_wout, small_results[1])
    new_m = full(m_win, m_wout, small_results[2])
    new_v = full(v_win, v_wout, small_results[3])
    return (loss, gx[None], *grads, *deltas, *new_m, *new_v)
```

```python
import numpy as np
import jax
import jax.numpy as jnp
from jax import lax
from jax.experimental import pallas as pl
from jax.experimental.pallas import tpu as pltpu

F32 = jnp.float32
BF16 = jnp.bfloat16

D_MODEL = 1024
POOL_WIDTH = 512
ATTN_WIDTH = 512
POOL_WINDOWS = (2, 4, 8, 16)
GROUP = 128
CHUNK = 64
LEFT_CHUNKS = 8
BAND = (LEFT_CHUNKS + 1) * CHUNK
HEAD_DIM = 64
N_PAIR = 4
MAX_REL = 64
N_REL = 2 * MAX_REL + 1
EPS = 1e-6
MASK_VALUE = -1e30
SCALE = 0.125
LOG2E = 1.4426950408889634

ADAM_LR = 0.001
ADAM_B1 = 0.9
ADAM_B2 = 0.999
ADAM_EPS = 1e-08
ADAM_WD = 0.01
ADAM_STEP = 10

TS = LEFT_CHUNKS * CHUNK
SUPER = 2 * CHUNK
WINDOW = BAND + CHUNK
TA = 4 * TS
WIN_BASE = TA - LEFT_CHUNKS * CHUNK
SUPERS_PER_TILE = TA // SUPER
PAIR_LANES = 2 * SUPER
HALO = 16
N_CHIP = 4
SHARD_IN = 768
SHARD_OUT = 256
PIECE = 256
PIECES_PER_SHARD = SHARD_IN // PIECE
HALF_IN = D_MODEL // 2
HALF_OUT = SHARD_OUT // 2
SMALL_ROWS = 560
POOL_W_ROW = 8
POOL_SCALE_ROW = 520
REL_BIAS_ROW = 528
REL_BIAS_ROWS_PER_HEAD = 2
FINAL_GAIN_ROW = 544
LOSS_ROW = 552
VMEM_LIMIT = 60 * 1024 * 1024

MESH = pl.DeviceIdType.MESH
ANY = pl.BlockSpec(memory_space=pl.ANY)
VMEM = pl.BlockSpec(memory_space=pltpu.VMEM)


def _sigmoid(x):
    return 1.0 / (1.0 + jnp.exp(-x))


def _dot(a, b):
    return jnp.dot(a, b, preferred_element_type=F32)


def _dot_t(a, b):
    return lax.dot_general(a, b, (((1,), (1,)), ((), ())), preferred_element_type=F32)


def _tdot(a, b):
    return lax.dot_general(a, b, (((0,), (0,)), ((), ())), preferred_element_type=F32)


def _my_place():
    return lax.axis_index("x"), lax.axis_index("y"), lax.axis_index("c")


def _other_chips(x, y):
    places = [(1 - x, y), (x, 1 - y), (1 - x, 1 - y)]
    return [(p, 2 * p[0] + p[1]) for p in places]


OWN_SLOT = 2


def _gather_inproj(x, g1, win_sh, wout_sh):
    seq = x.shape[0]
    tm = min(seq, 4 * TS)
    n_tiles = seq // tm
    cx, cy = lax.axis_index("x"), lax.axis_index("y")
    order = jnp.stack([2 * cx + cy] + [chip for _, chip in _other_chips(cx, cy)]).astype(jnp.int32)

    def body(order_ref, x_ref, g1_ref, win_ref, wout_ref, proj_bf_ref, win_full, wout_full,
             hbuf, wbuf, wout_own, win_f32, wout_f32, send_sems, recv_sems, local_sems):
        j = pl.program_id(0)
        i = pl.program_id(1)
        x_, y_, c = _my_place()
        b = 2 * x_ + y_
        sibling = (x_, y_, 1 - c)
        others = _other_chips(x_, y_)

        def halves(chip, core):
            return (
                win_full.at[chip, pl.ds(core * HALF_IN, HALF_IN)],
                wout_full.at[chip, pl.ds(core * HALF_OUT, HALF_OUT)],
            )

        def copy(k, src, dst, to):
            return pltpu.make_async_remote_copy(
                src_ref=src, dst_ref=dst, send_sem=send_sems.at[k], recv_sem=recv_sems.at[k],
                device_id=to, device_id_type=MESH)

        own = [
            pltpu.make_async_copy(wbuf.at[OWN_SLOT], win_full.at[b], local_sems.at[0]),
            pltpu.make_async_copy(wout_own, wout_full.at[b], local_sems.at[1]),
        ]
        mine_src = (wbuf.at[OWN_SLOT, pl.ds(c * HALF_IN, HALF_IN)], wout_own.at[pl.ds(c * HALF_OUT, HALF_OUT)])

        def direct(n, t):
            return copy(2 * n + t, mine_src[t], halves(b, c)[t], (*others[n][0], c))

        def arrival(n, t):
            landed = halves(others[n][1], c)[t]
            return copy(2 * n + t, landed, landed, (*others[n][0], c))

        def passing(n, t):
            landed = halves(others[n][1], c)[t]
            return copy(6 + 2 * n + t, landed, landed, sibling)

        def from_sibling(n, t):
            landed = halves(others[n][1], 1 - c)[t]
            return copy(6 + 2 * n + t, landed, landed, sibling)

        @pl.when((j == 0) & (i == 0))
        def _():
            staged = [pltpu.make_async_copy(win_ref, win_f32, local_sems.at[2]),
                      pltpu.make_async_copy(wout_ref, wout_f32, local_sems.at[3])]
            for cp in staged:
                cp.start()
            for t in range(2):
                staged[t].wait()
                if t == 0:
                    wbuf[OWN_SLOT] = win_f32[...].astype(BF16)
                else:
                    wout_own[...] = wout_f32[...].astype(BF16)
                own[t].start()
                for n in range(N_CHIP - 1):
                    direct(n, t).start()

        def load(n):
            return pltpu.make_async_copy(win_full.at[others[n][1]], wbuf.at[(n + 1) % 2], local_sems.at[2])

        for n in range(N_CHIP - 1):
            @pl.when((j == n) & (i == n_tiles - 1))
            def _(n=n):
                arrival(n, 0).wait_recv()
                passing(n, 0).start()

            @pl.when((j == n + 1) & (i == 0))
            def _(n=n):
                load(n).wait()

        rows = pl.ds(pl.multiple_of(i * tm, tm), tm)

        @pl.when(j == 0)
        def _():
            xt = x_ref[...]
            r = lax.rsqrt(jnp.mean(xt * xt, axis=-1, keepdims=True) + EPS)
            hbuf[rows] = ((xt * r) * g1_ref[...]).astype(BF16)

        proj_bf_ref[...] = _dot(hbuf[rows], wbuf[jnp.where(j == 0, OWN_SLOT, j % 2)]).astype(BF16)

        for n in range(N_CHIP - 1):
            @pl.when((j == n) & (i == n_tiles - 1))
            def _(n=n):
                from_sibling(n, 0).wait_recv()
                load(n).start()

        @pl.when((j == N_CHIP - 1) & (i == n_tiles - 1))
        def _():
            for n in range(N_CHIP - 1):
                arrival(n, 1).wait_recv()
                passing(n, 1).start()
            for n in range(N_CHIP - 1):
                from_sibling(n, 1).wait_recv()
            for n in range(N_CHIP - 1):
                for t in range(2):
                    direct(n, t).wait_send()
                    passing(n, t).wait_send()
            for cp in own:
                cp.wait()

    last = n_tiles - 1
    grid_spec = pltpu.PrefetchScalarGridSpec(
        num_scalar_prefetch=1,
        grid=(N_CHIP, n_tiles),
        in_specs=[
            pl.BlockSpec((tm, D_MODEL), lambda j, i, o: (jnp.where(j == 0, i, last), 0)),
            VMEM, ANY, ANY],
        out_specs=[pl.BlockSpec((tm, SHARD_IN), lambda j, i, o: (i, o[j])), ANY, ANY],
        scratch_shapes=[
            pltpu.VMEM((seq, D_MODEL), BF16), pltpu.VMEM((OWN_SLOT + 1, D_MODEL, SHARD_IN), BF16),
            pltpu.VMEM((SHARD_OUT, D_MODEL), BF16), pltpu.VMEM((D_MODEL, SHARD_IN), F32), pltpu.VMEM((SHARD_OUT, D_MODEL), F32),
            pltpu.SemaphoreType.DMA((12,)), pltpu.SemaphoreType.DMA((12,)), pltpu.SemaphoreType.DMA((4,))],
    )
    return pl.pallas_call(
        body,
        name="gather_inproj",
        grid_spec=grid_spec,
        out_shape=(
            jax.ShapeDtypeStruct((seq, N_CHIP * SHARD_IN), BF16),
            jax.ShapeDtypeStruct((N_CHIP, D_MODEL, SHARD_IN), BF16),
            jax.ShapeDtypeStruct((N_CHIP, SHARD_OUT, D_MODEL), BF16),
        ),
        compiler_params=pltpu.CompilerParams(
            dimension_semantics=("arbitrary", "arbitrary"), vmem_limit_bytes=VMEM_LIMIT),
    )(order, x, g1, win_sh, wout_sh)


def _window_sums(ext, forward):
    n = ext.shape[0]
    sums = []
    acc = ext
    for step in (1, 2, 4, 8):
        acc = acc + pltpu.roll(acc, (n - step) if forward else step, 0)
        sums.append(acc)
    return sums


def _row_counts(tile, rows):
    t = tile * rows + lax.broadcasted_iota(jnp.int32, (rows, GROUP), 0)
    return [jnp.minimum(t + 1, w).astype(F32) for w in POOL_WINDOWS]


def _pool_diffs(pv, prev_rows, tile):
    ext = jnp.concatenate([prev_rows, pv], axis=0)
    sums = _window_sums(ext, forward=False)
    counts = _row_counts(tile, pv.shape[0])
    out = []
    for g in range(len(POOL_WINDOWS)):
        cols = slice(g * GROUP, (g + 1) * GROUP)
        out.append(sums[g][HALO:, cols] / counts[g] - pv[:, cols])
    return out


def _pool_mix(diffs, pw_ref):
    return jnp.concatenate([_dot(diffs[g], pw_ref[g].astype(BF16)) for g in range(len(POOL_WINDOWS))], axis=1)


Q_BLOCK, K_BLOCK, V_BLOCK, AG_BLOCK = 8, 12, 16, 20


BIAS_TABLE = WINDOW + GROUP


def _bias_table(rel_bias):
    far = jnp.broadcast_to(rel_bias[:, :1], (rel_bias.shape[0], BAND - CHUNK - 1))
    by_distance = jnp.concatenate([far, rel_bias[:, :2 * MAX_REL]], axis=1)
    table = jnp.pad(by_distance, ((0, 0), (CHUNK, BIAS_TABLE - CHUNK - by_distance.shape[1])))
    return table.reshape(N_PAIR, 2, BIAS_TABLE)


def _bias_tile(table_ref):
    query = lax.broadcasted_iota(jnp.int32, (CHUNK, GROUP), 0)
    lane = lax.broadcasted_iota(jnp.int32, (CHUNK, GROUP), 1)
    key = lax.broadcasted_iota(jnp.int32, (CHUNK, WINDOW), 1)

    def skewed(table, offset):
        rotated = [pltpu.roll(jnp.broadcast_to(table[:, lane_group(m)], (CHUNK, GROUP)), (-offset) % GROUP, 1,
                              stride=1, stride_axis=0) for m in range(BIAS_TABLE // GROUP)]
        in_first = offset - query + lane < GROUP
        return jnp.concatenate(
            [jnp.where(in_first, rotated[m], rotated[m + 1]) for m in range(WINDOW // GROUP)], axis=1)

    def lane_group(m):
        return slice(m * GROUP, (m + 1) * GROUP)

    rows = []
    for h in range(2):
        table = table_ref[0, h:h + 1, :] * LOG2E
        rows.append(jnp.where(key < BAND, skewed(table, CHUNK + CHUNK - 1), MASK_VALUE))
        rows.append(jnp.where(key >= CHUNK, skewed(table, CHUNK - 1), MASK_VALUE))
    return jnp.concatenate(rows, axis=0).T


def _rel_index_tile():
    j = np.arange(WINDOW)[:, None]
    q = np.arange(PAIR_LANES)[None, :] % SUPER
    band_key = j - CHUNK * (q // CHUNK)
    idx = np.clip(band_key - LEFT_CHUNKS * CHUNK - q % CHUNK, -MAX_REL, MAX_REL) + MAX_REL
    return np.where((band_key >= 0) & (band_key < BAND), idx, -1).astype(np.int32)


def _by_head(block):
    low = lax.broadcasted_iota(jnp.int32, block.shape, 1) < HEAD_DIM
    zero = jnp.zeros_like(block)
    return jnp.concatenate([jnp.where(low, block, zero), jnp.where(low, zero, block)], axis=0)


def _own_head_rows(cross):
    head_of_row = lax.broadcasted_iota(jnp.int32, cross.shape, 0) >= HEAD_DIM
    head_of_lane = lax.broadcasted_iota(jnp.int32, cross.shape, 1) >= SUPER
    both = jnp.where(jnp.logical_xor(head_of_row, head_of_lane), 0.0, cross).T
    return both[:SUPER] + both[SUPER:]


def _with_mask_lane(q_rows):
    lane = lax.broadcasted_iota(jnp.int32, q_rows.shape, 1)
    return jnp.concatenate([q_rows, jnp.where(lane == 0, MASK_VALUE, 0.0).astype(q_rows.dtype)], axis=1)


def _band_exp(kb, q_rows, bias):
    s = _dot_t(kb, _with_mask_lane(q_rows)) + bias
    e = jnp.exp2(s - jnp.max(s, axis=0, keepdims=True))
    return e, jnp.sum(e, axis=0, keepdims=True)


def _window(sc):
    return slice(WIN_BASE + sc * SUPER, WIN_BASE + sc * SUPER + WINDOW)


def _shift_band(i, band_ref, new_ref):
    @pl.when(i == 0)
    def _():
        band_ref[:TA] = jnp.zeros((TA, GROUP), band_ref.dtype)

    @pl.when(i > 0)
    def _():
        band_ref[:TA] = band_ref[TA:]

    band_ref[TA:] = new_ref[...].astype(band_ref.dtype)


def _shift_key_band(i, band_ref, new_ref):
    @pl.when(i == 0)
    def _():
        lane = lax.broadcasted_iota(jnp.int32, (TA, 2 * GROUP), 1)
        band_ref[:TA] = jnp.where(lane == GROUP, 1.0, 0.0).astype(band_ref.dtype)
        band_ref[TA:, GROUP:] = jnp.zeros((TA, GROUP), band_ref.dtype)

    @pl.when(i > 0)
    def _():
        band_ref[:TA] = band_ref[TA:]

    band_ref[TA:, :GROUP] = new_ref[...].astype(band_ref.dtype)


def _shift_band_t(i, band_ref, new_ref):
    @pl.when(i == 0)
    def _():
        band_ref[:, :TA] = jnp.zeros((GROUP, TA), band_ref.dtype)

    @pl.when(i > 0)
    def _():
        band_ref[:, :TA] = band_ref[:, TA:]

    band_ref[:, TA:] = new_ref[...].astype(band_ref.dtype).T


def _scaled_queries(q_ref, rows, scale=SCALE):
    return _by_head((q_ref[rows] * scale).astype(BF16))


def _attn_fwd(proj_bf, bias_table):
    seq = proj_bf.shape[0]
    n_tiles = seq // TA

    def body(q_ref, k_ref, v_ref, ag_ref, table_ref, a_ref, ya_ref, e_ref, inv_ref, kband, vband_t, bias_ref):
        i = pl.program_id(1)
        _shift_key_band(i, kband, k_ref)
        _shift_band_t(i, vband_t, v_ref)

        @pl.when(i == 0)
        def _():
            bias_ref[...] = _bias_tile(table_ref)

        def weights(sc):
            rows = slice(sc * SUPER, (sc + 1) * SUPER)
            win = _window(sc)
            e, total = _band_exp(kband[win], _scaled_queries(q_ref, rows, SCALE * LOG2E), bias_ref[...])
            e_ref[0, sc] = e.astype(BF16)
            inv_total = 1.0 / total
            inv_ref[0, sc] = jnp.broadcast_to(inv_total, (8, PAIR_LANES))
            return inv_total

        nxt = weights(0)
        for sc in range(SUPERS_PER_TILE):
            rows = slice(sc * SUPER, (sc + 1) * SUPER)
            win = _window(sc)
            inv_total = nxt
            if sc + 1 < SUPERS_PER_TILE:
                nxt = weights(sc + 1)
            a = _own_head_rows(_dot(vband_t[:, win], e_ref[0, sc]) * inv_total)
            a_ref[rows] = a.astype(BF16)
            g = ag_ref[rows].astype(F32)
            ya_ref[rows] = (a * (g * _sigmoid(g))).astype(BF16)

    blk = pl.BlockSpec((TA, GROUP), lambda p, i: (i, p))

    def cols(first):
        return pl.BlockSpec((TA, GROUP), lambda p, i: (i, first + p))

    return pl.pallas_call(
        body,
        name="attn_fwd",
        grid=(N_PAIR, n_tiles),
        in_specs=[cols(Q_BLOCK), cols(K_BLOCK), cols(V_BLOCK), cols(AG_BLOCK),
                  pl.BlockSpec((1, 2, BIAS_TABLE), lambda p, i: (p, 0, 0))],
        out_specs=[blk, blk,
                   pl.BlockSpec((1, SUPERS_PER_TILE, WINDOW, PAIR_LANES), lambda p, i: (p, i, 0, 0)),
                   pl.BlockSpec((1, SUPERS_PER_TILE, 8, PAIR_LANES), lambda p, i: (p, i, 0, 0))],
        out_shape=[jax.ShapeDtypeStruct((seq, ATTN_WIDTH), BF16), jax.ShapeDtypeStruct((seq, ATTN_WIDTH), BF16),
                   jax.ShapeDtypeStruct((N_PAIR, seq // SUPER, WINDOW, PAIR_LANES), BF16),
                   jax.ShapeDtypeStruct((N_PAIR, seq // SUPER, 8, PAIR_LANES), F32)],
        scratch_shapes=[pltpu.VMEM((2 * TA, 2 * GROUP), BF16), pltpu.VMEM((GROUP, 2 * TA), BF16),
                        pltpu.VMEM((WINDOW, PAIR_LANES), F32)],
        compiler_params=pltpu.CompilerParams(
            dimension_semantics=("arbitrary", "arbitrary"), vmem_limit_bytes=VMEM_LIMIT),
    )(proj_bf, proj_bf, proj_bf, proj_bf, bias_table)


BIN_ROWS = 136


def _bias_bin_sums(db_ref, idx_ref):
    lane = lax.broadcasted_iota(jnp.int32, (1, GROUP), 1)
    row = lax.broadcasted_iota(jnp.int32, (BIN_ROWS, GROUP), 0)
    out = jnp.zeros((BIN_ROWS, GROUP), F32)
    for r in range(N_REL - 1):
        lo = 0 if r == 0 else ((BAND - 2 * CHUNK + r) // 8) * 8
        hi = WINDOW if r == 0 else min(WINDOW, lo + SUPER + 8)
        hit = jnp.where(idx_ref[lo:hi] == r, db_ref[lo:hi], 0.0)
        col = jnp.sum(hit, axis=0, keepdims=True)
        s0 = jnp.sum(col[:, :SUPER], axis=1, keepdims=True)
        s1 = jnp.sum(col[:, SUPER:], axis=1, keepdims=True)
        val = jnp.where(lane == 0, s0, jnp.where(lane == 1, s1, 0.0))
        out = jnp.where(row == r, val, out)
    return out


def _attn_bwd(proj_bf, a, dy, e_all, inv_all):
    seq = proj_bf.shape[0]
    n_tiles = seq // TA

    def body(q_ref, k_ref, v_ref, a_ref, ag_ref, dy_ref, e_ref, inv_ref, idx_ref,
             dq_ref, dk_ref, dv_ref, dag_ref, bins_ref, vband, kband_t, dkacc, dvacc, db_ref):
        i = pl.program_id(1)

        @pl.when(i == 0)
        def _():
            dkacc[...] = jnp.zeros_like(dkacc)
            dvacc[...] = jnp.zeros_like(dvacc)
            db_ref[...] = jnp.zeros_like(db_ref)

        @pl.when(i < n_tiles)
        def _():
            _shift_band(i, vband, v_ref)
            _shift_band_t(i, kband_t, k_ref)

            def score_grads(sc):
                rows = slice(sc * SUPER, (sc + 1) * SUPER)
                win = _window(sc)
                q_rows = _scaled_queries(q_ref, rows)
                g = ag_ref[rows].astype(F32)
                sg = _sigmoid(g)
                dyc = dy_ref[rows].astype(F32)
                dag_ref[rows] = (dyc * a_ref[rows].astype(F32) * (sg * (1.0 + g * (1.0 - sg)))).astype(BF16)
                da_rows = _by_head((dyc * (g * sg)).astype(BF16))
                p = e_ref[0, sc].astype(F32) * inv_ref[0, sc, :1]
                dp = _dot_t(vband[win], da_rows)
                ds = p * (dp - jnp.sum(p * dp, axis=0, keepdims=True))
                db_ref[...] += ds
                return q_rows, da_rows, p.astype(BF16), ds.astype(BF16)

            nxt = score_grads(0)
            for sc in range(SUPERS_PER_TILE):
                rows = slice(sc * SUPER, (sc + 1) * SUPER)
                win = _window(sc)
                q_rows, da_rows, p_bf, ds_bf = nxt
                if sc + 1 < SUPERS_PER_TILE:
                    nxt = score_grads(sc + 1)
                dq_ref[rows] = (_own_head_rows(_dot(kband_t[:, win], ds_bf)) * SCALE).astype(BF16)
                dkacc[win] += _dot(ds_bf, q_rows)
                dvacc[win] += _dot(p_bf, da_rows)

        dk_ref[...] = dkacc[:TA].astype(BF16)
        dv_ref[...] = dvacc[:TA].astype(BF16)
        dkacc[:TA] = dkacc[TA:]
        dvacc[:TA] = dvacc[TA:]
        dkacc[TA:] = jnp.zeros((TA, GROUP), F32)
        dvacc[TA:] = jnp.zeros((TA, GROUP), F32)

        @pl.when(i == n_tiles)
        def _():
            bins_ref[0] = _bias_bin_sums(db_ref, idx_ref)

    last = n_tiles - 1
    cur = pl.BlockSpec((TA, GROUP), lambda p, i: (jnp.minimum(i, last), p))
    older = pl.BlockSpec((TA, GROUP), lambda p, i: (jnp.maximum(i - 1, 0), p))
    dy_blk = pl.BlockSpec((TA, GROUP), lambda p, i: (jnp.minimum(i, last), N_PAIR + p))
    per_pair = pl.BlockSpec((1, BIN_ROWS, GROUP), lambda p, i: (p, 0, 0))

    def cols(first):
        return pl.BlockSpec((TA, GROUP), lambda p, i: (jnp.minimum(i, last), first + p))

    def kept(rows):
        return pl.BlockSpec((1, SUPERS_PER_TILE, rows, PAIR_LANES), lambda p, i: (p, jnp.minimum(i, last), 0, 0))

    def out(dtype):
        return jax.ShapeDtypeStruct((seq, ATTN_WIDTH), dtype)

    return pl.pallas_call(
        body,
        name="attn_bwd",
        grid=(N_PAIR, n_tiles + 1),
        in_specs=[cols(Q_BLOCK), cols(K_BLOCK), cols(V_BLOCK), cur, cols(AG_BLOCK), dy_blk, kept(WINDOW), kept(8),
                  VMEM],
        out_specs=[cur, older, older, cur, per_pair],
        out_shape=[out(BF16), out(BF16), out(BF16), out(BF16),
                   jax.ShapeDtypeStruct((N_PAIR, BIN_ROWS, GROUP), F32)],
        scratch_shapes=[
            pltpu.VMEM((2 * TA, GROUP), BF16), pltpu.VMEM((GROUP, 2 * TA), BF16),
            pltpu.VMEM((2 * TA, GROUP), F32), pltpu.VMEM((2 * TA, GROUP), F32),
            pltpu.VMEM((WINDOW, PAIR_LANES), F32)],
        compiler_params=pltpu.CompilerParams(
            dimension_semantics=("arbitrary", "arbitrary"), vmem_limit_bytes=VMEM_LIMIT),
    )(proj_bf, proj_bf, proj_bf, a, proj_bf, dy, e_all, inv_all, jnp.asarray(_rel_index_tile()))


def _out_loss(x, tgt, proj, ya, wout_full, g2, pw, ps):
    seq = x.shape[0]
    to = min(seq, 2 * TS)
    n_tiles = seq // to

    def body(x_ref, t_ref, pv_ref, pg_ref, ya_ref, w_ref, g2_ref, pw_ref, ps_ref,
             dx2_ref, dy_ref, gw_ref, gg_ref, loss_ref, sq_ref, halo_ref, wt_ref):
        i = pl.program_id(0)

        @pl.when(i == 0)
        def _():
            gw_ref[...] = jnp.zeros_like(gw_ref)
            gg_ref[...] = jnp.zeros_like(gg_ref)
            sq_ref[...] = jnp.zeros_like(sq_ref)
            halo_ref[...] = jnp.zeros_like(halo_ref)
            for b in range(N_CHIP):
                wt_ref[b] = w_ref[b].T

        pv = pv_ref[...].astype(F32)
        pg = pg_ref[...].astype(F32)
        diffs = [d.astype(BF16) for d in _pool_diffs(pv, halo_ref[...], i)]
        halo_ref[...] = pv[to - HALO:, :]
        yp = ((_pool_mix(diffs, pw_ref) * ps_ref[...]) * (pg * _sigmoid(pg))).astype(BF16)
        g2v = g2_ref[...]

        def parts(rows):
            return [yp[rows, :SHARD_OUT], yp[rows, SHARD_OUT:], ya_ref[rows, :SHARD_OUT], ya_ref[rows, SHARD_OUT:]]

        def project(rows, ys):
            x2 = x_ref[rows]
            for b in range(N_CHIP):
                x2 = x2 + _dot(ys[b], w_ref[b])
            return x2

        def norm_loss(rows, x2):
            r = lax.rsqrt(jnp.mean(x2 * x2, axis=-1, keepdims=True) + EPS)
            xh = x2 * r
            diff = xh * g2v - t_ref[rows]
            sq_ref[...] += jnp.sum(diff * diff, axis=0, keepdims=True)
            dfin = diff * (1.0 / D_MODEL)
            gg_ref[...] += jnp.sum(dfin * xh, axis=0, keepdims=True)
            dxh = dfin * g2v
            dx2 = r * (dxh - xh * jnp.mean(dxh * xh, axis=-1, keepdims=True))
            dx2_ref[rows] = dx2
            return dx2.astype(BF16)

        def back(rows, ys, dx2_bf):
            for b in range(N_CHIP):
                gw_ref[b] += _tdot(ys[b], dx2_bf)
                dy_ref[rows, b * SHARD_OUT:(b + 1) * SHARD_OUT] = _dot(dx2_bf, wt_ref[b]).astype(BF16)

        n_parts = 2
        part = to // n_parts
        spans = [slice(r * part, (r + 1) * part) for r in range(n_parts)]
        ys = [parts(rows) for rows in spans]
        x2_next = project(spans[0], ys[0])
        for r in range(n_parts):
            x2 = x2_next
            if r + 1 < n_parts:
                x2_next = project(spans[r + 1], ys[r + 1])
            back(spans[r], ys[r], norm_loss(spans[r], x2))

        @pl.when(i == n_tiles - 1)
        def _():
            total = jnp.sum(sq_ref[...], axis=1, keepdims=True) * (0.5 / D_MODEL)
            loss_ref[...] = jnp.broadcast_to(total, loss_ref.shape)

    def rows(width, col=0):
        return pl.BlockSpec((to, width), lambda i: (i, col))

    return pl.pallas_call(
        body,
        name="out_loss",
        grid=(n_tiles,),
        in_specs=[rows(D_MODEL), rows(D_MODEL), rows(POOL_WIDTH, 0), rows(POOL_WIDTH, 1), rows(ATTN_WIDTH),
                  VMEM, VMEM, VMEM, VMEM],
        out_specs=[rows(D_MODEL), rows(D_MODEL), VMEM, VMEM, VMEM],
        out_shape=[
            jax.ShapeDtypeStruct((seq, D_MODEL), F32), jax.ShapeDtypeStruct((seq, D_MODEL), BF16),
            jax.ShapeDtypeStruct((N_CHIP, SHARD_OUT, D_MODEL), F32), jax.ShapeDtypeStruct((1, D_MODEL), F32),
            jax.ShapeDtypeStruct((8, GROUP), F32)],
        scratch_shapes=[pltpu.VMEM((1, D_MODEL), F32), pltpu.VMEM((HALO, POOL_WIDTH), F32),
                        pltpu.VMEM((N_CHIP, D_MODEL, SHARD_OUT), BF16)],
        compiler_params=pltpu.CompilerParams(dimension_semantics=("arbitrary",), vmem_limit_bytes=VMEM_LIMIT),
    )(x, tgt, proj, proj, ya, wout_full, g2, pw, ps)


def _inproj_bwd(x, dx2, dy, proj, dq, dk, dv, dag, win_full, pw, g1, ps):
    seq = x.shape[0]
    n_tiles = seq // TS

    def body(x_ref, dx2_ref, dyp_ref, pv_ref, pvprev_ref, pg_ref, dq_ref, dk_ref, dv_ref, dag_ref,
             w_ref, pw_ref, g1_ref, ps_ref, gx_ref, dproj_ref, ht_ref, gg_ref, gps_ref, gpw_ref, halo_ref):
        i = pl.program_id(0)
        tile = n_tiles - 1 - i

        @pl.when(i == 0)
        def _():
            gg_ref[...] = jnp.zeros_like(gg_ref)
            gps_ref[...] = jnp.zeros_like(gps_ref)
            gpw_ref[...] = jnp.zeros_like(gpw_ref)
            halo_ref[...] = jnp.zeros_like(halo_ref)

        pv_t = pv_ref[...].astype(F32)
        pg_t = pg_ref[...].astype(F32)
        prev_rows = jnp.where(tile > 0, pvprev_ref[...].astype(F32), 0.0)
        diffs = [d.astype(BF16) for d in _pool_diffs(pv_t, prev_rows, tile)]
        mixed = _pool_mix(diffs, pw_ref)
        sg = _sigmoid(pg_t)
        silu = pg_t * sg
        dyp = dyp_ref[...].astype(F32)
        psv = ps_ref[...]
        gps_ref[...] += jnp.sum(dyp * mixed * silu, axis=0, keepdims=True)
        dmixed = (dyp * psv * silu).astype(BF16)
        dpg = dyp * (mixed * psv) * (sg * (1.0 + pg_t * (1.0 - sg)))
        counts = _row_counts(tile, TS)
        dds = []
        for g in range(len(POOL_WINDOWS)):
            dm_g = dmixed[:, g * GROUP:(g + 1) * GROUP]
            gpw_ref[g] += _tdot(diffs[g], dm_g)
            dds.append(_dot_t(dm_g, pw_ref[g].astype(BF16)))
        dd = jnp.concatenate(dds, axis=1)
        spread = jnp.concatenate([dds[g] / counts[g] for g in range(len(POOL_WINDOWS))], axis=1)
        sums = _window_sums(jnp.concatenate([spread, halo_ref[...]], axis=0), forward=True)
        halo_ref[...] = spread[:HALO]
        dpv = jnp.concatenate(
            [sums[g][:TS, g * GROUP:(g + 1) * GROUP] for g in range(len(POOL_WINDOWS))], axis=1) - dd

        xt = x_ref[...]
        r = lax.rsqrt(jnp.mean(xt * xt, axis=-1, keepdims=True) + EPS)
        xh = xt * r
        g1v = g1_ref[...]
        ht_ref[...] = (xh * g1v).astype(BF16).T
        dproj = jnp.concatenate(
            [dpv.astype(BF16), dpg.astype(BF16), dq_ref[...], dk_ref[...], dv_ref[...], dag_ref[...]],
            axis=1)
        dproj_ref[...] = dproj
        dh = _dot_t(dproj[:, :SHARD_IN], w_ref[0])
        for chip in range(1, N_CHIP):
            dh = dh + _dot_t(dproj[:, chip * SHARD_IN:(chip + 1) * SHARD_IN], w_ref[chip])

        gg_ref[...] += jnp.sum(dh * xh, axis=0, keepdims=True)
        dxh = dh * g1v
        gx_ref[...] = dx2_ref[...] + r * (dxh - xh * jnp.mean(dxh * xh, axis=-1, keepdims=True))

    def rows(width, col=0):
        return pl.BlockSpec((TS, width), lambda i: (n_tiles - 1 - i, col))

    prev = pl.BlockSpec((HALO, POOL_WIDTH), lambda i: (jnp.maximum((n_tiles - 1 - i) * (TS // HALO) - 1, 0), 0))
    return pl.pallas_call(
        body,
        name="inproj_bwd",
        grid=(n_tiles,),
        in_specs=[rows(D_MODEL), rows(D_MODEL), rows(POOL_WIDTH), rows(POOL_WIDTH), prev, rows(POOL_WIDTH, 1),
                  rows(ATTN_WIDTH), rows(ATTN_WIDTH), rows(ATTN_WIDTH), rows(ATTN_WIDTH), VMEM, VMEM, VMEM, VMEM],
        out_specs=[rows(D_MODEL), rows(N_CHIP * SHARD_IN),
                   pl.BlockSpec((D_MODEL, TS), lambda i: (0, n_tiles - 1 - i)), VMEM, VMEM, VMEM],
        out_shape=[
            jax.ShapeDtypeStruct((seq, D_MODEL), F32),
            jax.ShapeDtypeStruct((seq, N_CHIP * SHARD_IN), BF16),
            jax.ShapeDtypeStruct((D_MODEL, seq), BF16),
            jax.ShapeDtypeStruct((1, D_MODEL), F32),
            jax.ShapeDtypeStruct((1, POOL_WIDTH), F32),
            jax.ShapeDtypeStruct((len(POOL_WINDOWS), GROUP, GROUP), F32)],
        scratch_shapes=[pltpu.VMEM((HALO, POOL_WIDTH), F32)],
        compiler_params=pltpu.CompilerParams(dimension_semantics=("arbitrary",), vmem_limit_bytes=VMEM_LIMIT),
    )(x, dx2, dy, proj, proj, proj, dq, dk, dv, dag, win_full, pw, g1, ps)


PASS_PEER = (2, 0, 1)


def _pack_small(dst, norm_ref, pool_w_ref, pool_scale_ref, bins_ref, final_ref, loss_ref):
    def lane_group(r):
        return slice(r * GROUP, (r + 1) * GROUP)

    dst[...] = jnp.zeros(dst.shape, F32)
    for r in range(D_MODEL // GROUP):
        dst[r:r + 1] = norm_ref[:, lane_group(r)]
        dst[FINAL_GAIN_ROW + r:FINAL_GAIN_ROW + r + 1] = final_ref[:, lane_group(r)]
    for g in range(len(POOL_WINDOWS)):
        dst[POOL_W_ROW + g * GROUP:POOL_W_ROW + (g + 1) * GROUP] = pool_w_ref[g]
    for r in range(POOL_WIDTH // GROUP):
        dst[POOL_SCALE_ROW + r:POOL_SCALE_ROW + r + 1] = pool_scale_ref[:, lane_group(r)]
    first_lane = lax.broadcasted_iota(jnp.int32, (BIN_ROWS - GROUP, GROUP), 1) == 0
    for p in range(N_PAIR):
        by_head = bins_ref[p, :GROUP].T
        last_column = bins_ref[p, GROUP:]
        for h in range(2):
            row = REL_BIAS_ROW + REL_BIAS_ROWS_PER_HEAD * (2 * p + h)
            dst[row:row + 1] = by_head[h:h + 1]
            to_first_lane = last_column if h == 0 else pltpu.roll(last_column, GROUP - h, 1)
            dst[row + 1:row + 2] = jnp.where(first_lane, to_first_lane, 0.0)[:1]
    dst[LOSS_ROW:LOSS_ROW + loss_ref.shape[0]] = loss_ref[...]


def _gw_reduce(ht, dproj, gwout, small_parts):
    seq = ht.shape[1]
    tm = min(2 * TS, seq // 4)
    n_tiles = seq // tm
    half_small = SMALL_ROWS // 2
    cx, cy = lax.axis_index("x"), lax.axis_index("y")
    outer = _other_chips(cx, cy)
    order = jnp.stack([outer[n][1] for n in PASS_PEER] + [2 * cx + cy]).astype(jnp.int32)

    def body(order_ref, ht_ref, dp_ref, gwout_ref, *refs):
        part_refs = refs[:len(small_parts)]
        (gin_final, gout_final, small_final, hbuf, acc, pair_in, pair_out, small_ref, pair_small, tx_in, tx_out, rx_in, rx_out,
         rx_small, gin_out, gout_out, small_out, send_sems, recv_sems, out_sems) = refs[len(small_parts):]
        j = pl.program_id(0)
        i = pl.program_id(1)
        x, y, c = _my_place()
        b = 2 * x + y
        sibling = (x, y, 1 - c)
        others = _other_chips(x, y)
        mine_in = pl.ds(pl.multiple_of(c * HALF_IN, HALF_IN), HALF_IN)
        mine_out = pl.ds(pl.multiple_of(c * HALF_OUT, HALF_OUT), HALF_OUT)
        mine_small = pl.ds(pl.multiple_of(c * half_small, 8), half_small)
        theirs_in = pl.ds(pl.multiple_of((1 - c) * HALF_IN, HALF_IN), HALF_IN)
        theirs_out = pl.ds(pl.multiple_of((1 - c) * HALF_OUT, HALF_OUT), HALF_OUT)
        theirs_small = pl.ds(pl.multiple_of((1 - c) * half_small, 8), half_small)

        def copy(k, src, dst, to):
            return pltpu.make_async_remote_copy(
                src_ref=src, dst_ref=dst, send_sem=send_sems.at[k], recv_sem=recv_sems.at[k],
                device_id=to, device_id_type=MESH)

        swap_out = copy(0, gwout_ref.at[:, theirs_out], pair_out, sibling)
        swap_small = copy(1, small_ref, pair_small, sibling)

        def swap_in(p):
            return copy(2 + p, acc.at[p % 2, theirs_in], pair_in.at[p % 2], sibling)

        def to_chip(n, t):
            to = (*others[n][0], c)
            if t == 0:
                return copy(6 + 3 * n, tx_in.at[n], rx_in.at[n], to)
            if t == 1:
                return copy(7 + 3 * n, tx_out.at[n], rx_out.at[n], to)
            return copy(8 + 3 * n, pair_small.at[mine_small], rx_small.at[b], to)

        share_in = copy(15, gin_out.at[mine_in], gin_out.at[mine_in], sibling)
        share_out = copy(16, gout_out.at[mine_out], gout_out.at[mine_out], sibling)
        share_small = copy(17, small_out.at[mine_small], small_out.at[mine_small], sibling)

        def at(jj, ii):
            return (j == jj) & (i == ii)

        par = j % 2

        @pl.when(i == 0)
        def _():
            acc[par] = jnp.zeros((D_MODEL, SHARD_IN), F32)

        cols = pl.ds(pl.multiple_of(i * tm, tm), tm)

        @pl.when(j == 0)
        def _():
            hbuf[:, cols] = ht_ref[...]

        acc[par] += _dot(hbuf[:, cols], dp_ref[...])

        @pl.when(at(0, 0))
        def _():
            swap_out.start()
            _pack_small(small_ref, *part_refs)
            swap_small.start()

        @pl.when(at(0, 2))
        def _():
            swap_out.wait_recv()
            swap_small.wait_recv()
            for chip in range(N_CHIP):
                pair_out[chip] = gwout_ref[chip, mine_out] + pair_out[chip]
            pair_small[...] = small_ref[...] + pair_small[...]
            rx_small[b] = pair_small[mine_small]
            for n in range(N_CHIP - 1):
                tx_out[n] = pair_out[others[n][1]].astype(BF16)
                to_chip(n, 1).start()
                to_chip(n, 2).start()

        @pl.when(at(1, 3))
        def _():
            total_out = pair_out[b]
            for n in range(N_CHIP - 1):
                to_chip(n, 1).wait_recv()
                copy(8 + 3 * n, pair_small.at[mine_small], rx_small.at[others[n][1]], (*others[n][0], c)).wait_recv()
                total_out = total_out + rx_out[n].astype(F32)
            gout_out[mine_out] = total_out
            small_out[mine_small] = ((rx_small[0] + rx_small[1]) + rx_small[2]) + rx_small[3]
            share_out.start()
            share_small.start()

        for p in range(N_CHIP - 1):
            n = PASS_PEER[p]

            @pl.when(at(p + 1, 0))
            def _(p=p):
                swap_in(p).start()

            @pl.when(at(p + 1, 2))
            def _(p=p, n=n):
                swap_in(p).wait_recv()
                swap_in(p).wait_send()
                tx_in[n] = (acc[p % 2, mine_in] + pair_in[p % 2]).astype(BF16)
                to_chip(n, 0).start()

        @pl.when(at(N_CHIP - 1, n_tiles - 1))
        def _():
            last = N_CHIP - 1
            swap_in(last).start()
            swap_in(last).wait_recv()
            total_in = acc[last % 2, mine_in] + pair_in[last % 2]
            for n in range(N_CHIP - 1):
                to_chip(n, 0).wait_recv()
                total_in = total_in + rx_in[n].astype(F32)
            gin_out[mine_in] = total_in
            share_in.start()
            copy(15, gin_out.at[theirs_in], gin_out.at[theirs_in], sibling).wait_recv()
            copy(16, gout_out.at[theirs_out], gout_out.at[theirs_out], sibling).wait_recv()
            copy(17, small_out.at[theirs_small], small_out.at[theirs_small], sibling).wait_recv()
            swap_out.wait_send()
            swap_small.wait_send()
            swap_in(last).wait_send()
            for n in range(N_CHIP - 1):
                for t in range(3):
                    to_chip(n, t).wait_send()
            share_in.wait_send()
            share_out.wait_send()
            share_small.wait_send()
            outs = [pltpu.make_async_copy(src, dst, out_sems.at[k]) for k, (src, dst) in enumerate(
                [(gin_out, gin_final), (gout_out, gout_final), (small_out, small_final)])]
            for cp in outs:
                cp.start()
            for cp in outs:
                cp.wait()

    assert n_tiles >= 4, "the reduction's steps are spread over the first four token steps of a pass"
    grid_spec = pltpu.PrefetchScalarGridSpec(
        num_scalar_prefetch=1,
        grid=(N_CHIP, n_tiles),
        in_specs=[
            pl.BlockSpec((D_MODEL, tm), lambda j, i, o: (0, jnp.where(j == 0, i, n_tiles - 1))),
            pl.BlockSpec((tm, SHARD_IN), lambda j, i, o: (i, o[j])),
            VMEM] + [VMEM] * len(small_parts),
        out_specs=[ANY, ANY, ANY],
        scratch_shapes=[
            pltpu.VMEM((D_MODEL, seq), BF16),
            pltpu.VMEM((2, D_MODEL, SHARD_IN), F32),
            pltpu.VMEM((2, HALF_IN, SHARD_IN), F32),
            pltpu.VMEM((N_CHIP, HALF_OUT, D_MODEL), F32),
            pltpu.VMEM((SMALL_ROWS, GROUP), F32),
            pltpu.VMEM((SMALL_ROWS, GROUP), F32),
            pltpu.VMEM((N_CHIP - 1, HALF_IN, SHARD_IN), BF16),
            pltpu.VMEM((N_CHIP - 1, HALF_OUT, D_MODEL), BF16),
            pltpu.VMEM((N_CHIP - 1, HALF_IN, SHARD_IN), BF16),
            pltpu.VMEM((N_CHIP - 1, HALF_OUT, D_MODEL), BF16),
            pltpu.VMEM((N_CHIP, half_small, GROUP), F32),
            pltpu.VMEM((D_MODEL, SHARD_IN), F32),
            pltpu.VMEM((SHARD_OUT, D_MODEL), F32),
            pltpu.VMEM((SMALL_ROWS, GROUP), F32),
            pltpu.SemaphoreType.DMA((18,)),
            pltpu.SemaphoreType.DMA((18,)),
            pltpu.SemaphoreType.DMA((3,)),
        ],
    )
    return pl.pallas_call(
        body,
        name="gw_reduce",
        grid_spec=grid_spec,
        out_shape=(
            jax.ShapeDtypeStruct((D_MODEL, SHARD_IN), F32),
            jax.ShapeDtypeStruct((SHARD_OUT, D_MODEL), F32),
            jax.ShapeDtypeStruct((SMALL_ROWS, GROUP), F32),
        ),
        compiler_params=pltpu.CompilerParams(
            dimension_semantics=("arbitrary", "arbitrary"), vmem_limit_bytes=VMEM_LIMIT),
    )(order, ht, dproj, gwout, *small_parts)


def _adam_update(w, grad, m, v):
    m_new = ADAM_B1 * m + (1.0 - ADAM_B1) * grad
    v_new = ADAM_B2 * v + (1.0 - ADAM_B2) * (grad * grad)
    m_hat = m_new / (1.0 - ADAM_B1 ** ADAM_STEP)
    v_hat = v_new / (1.0 - ADAM_B2 ** ADAM_STEP)
    delta = -ADAM_LR * (m_hat / (jnp.sqrt(v_hat) + ADAM_EPS) + ADAM_WD * w)
    return delta, m_new, v_new


ADAMW_STEPS = 4


def _adamw_weights(win_state, wout_state):
    def body(*refs):
        for w_ref, g_ref, m_ref, v_ref, g_out, d_out, m_out, v_out in (refs[0:4] + refs[8:12], refs[4:8] + refs[12:16]):
            grad = g_ref[...]
            g_out[...] = grad
            d_out[...], m_out[...], v_out[...] = _adam_update(w_ref[...], grad, m_ref[...], v_ref[...])

    specs, shapes = [], []
    for w in (win_state[0], wout_state[0]):
        rows, cols = w.shape
        specs += [pl.BlockSpec((rows // ADAMW_STEPS, cols), lambda i: (i, 0))] * 4
        shapes += [jax.ShapeDtypeStruct((rows, cols), F32)] * 4
    outs = pl.pallas_call(
        body,
        name="adamw_weights",
        grid=(ADAMW_STEPS,),
        in_specs=specs,
        out_specs=specs,
        out_shape=shapes,
        compiler_params=pltpu.CompilerParams(dimension_semantics=("arbitrary",)),
    )(*win_state, *wout_state)
    return outs[0:4], outs[4:8]


def _adamw_small(g_small, params, m_state, v_state):
    n_param = len(params)
    n_head = params[3].shape[0]

    def body(g_ref, *refs):
        w_refs, m_refs, v_refs = (refs[k * n_param:(k + 1) * n_param] for k in range(3))
        g_outs, d_outs, m_outs, v_outs = (refs[k * n_param:(k + 1) * n_param] for k in range(3, 7))

        def update(i, packed_rows, rows, cols, packed_cols=slice(None)):
            grad = g_ref[packed_rows, packed_cols]
            g_outs[i][rows, cols] = grad
            d_outs[i][rows, cols], m_outs[i][rows, cols], v_outs[i][rows, cols] = _adam_update(
                w_refs[i][rows, cols], grad, m_refs[i][rows, cols], v_refs[i][rows, cols])

        def lane_group(r):
            return slice(r * GROUP, (r + 1) * GROUP)

        everything = slice(None)
        for r in range(D_MODEL // GROUP):
            update(0, slice(r, r + 1), everything, lane_group(r))
            update(4, slice(FINAL_GAIN_ROW + r, FINAL_GAIN_ROW + r + 1), everything, lane_group(r))
        update(1, slice(POOL_W_ROW, POOL_W_ROW + len(POOL_WINDOWS) * GROUP), everything, everything)
        for r in range(POOL_WIDTH // GROUP):
            update(2, slice(POOL_SCALE_ROW + r, POOL_SCALE_ROW + r + 1), everything, lane_group(r))
        for h in range(n_head):
            first = REL_BIAS_ROW + REL_BIAS_ROWS_PER_HEAD * h
            update(3, slice(first, first + 1), slice(h, h + 1), slice(0, GROUP))
            update(3, slice(first + 1, first + 2), slice(h, h + 1), slice(GROUP, N_REL), slice(0, N_REL - GROUP))

    shapes = [jax.ShapeDtypeStruct(p.shape, F32) for p in params]
    outs = pl.pallas_call(
        body,
        name="adamw_small",
        out_shape=shapes * 4,
    )(g_small, *params, *m_state, *v_state)
    return [outs[k * n_param:(k + 1) * n_param] for k in range(4)]


def kernel(x, norm_gain, w_in, pool_w, pool_scale, rel_bias, w_out, final_norm_gain, loss_target, m_norm_gain, m_w_in, m_pool_w, m_pool_scale, m_rel_bias, m_w_out, m_final_norm_gain, v_norm_gain, v_w_in, v_pool_w, v_pool_scale, v_rel_bias, v_w_out, v_final_norm_gain):
    assert x.shape[1] % TS == 0 and x.shape[2] == D_MODEL
    xs = x[0]
    tgt = loss_target[0]
    g1 = norm_gain.reshape(1, D_MODEL)
    g2 = final_norm_gain.reshape(1, D_MODEL)
    ps = pool_scale.reshape(1, POOL_WIDTH)
    pw = pool_w[0]

    proj, win_full, wout_full = _gather_inproj(xs, g1, w_in[0], w_out[0])
    bias_table = _bias_table(rel_bias[0])

    a, ya, e_all, inv_all = _attn_fwd(proj, bias_table)
    dx2, dy, gwout, gg2, loss_rows = _out_loss(xs, tgt, proj, ya, wout_full, g2, pw, ps)
    dq, dk, dv, dag, bins = _attn_bwd(proj, a, dy, e_all, inv_all)
    gx, dproj, ht, gg1, gps, gpw = _inproj_bwd(xs, dx2, dy, proj, dq, dk, dv, dag, win_full, pw, g1, ps)

    g_win, g_wout, g_small = _gw_reduce(ht, dproj, gwout, (gg1, gpw, gps, bins, gg2, loss_rows))
    loss = g_small[LOSS_ROW, 0]

    def small_views(norm, pool, scale, bias, final):
        return [norm.reshape(1, D_MODEL), pool.reshape(len(POOL_WINDOWS) * GROUP, GROUP), scale.reshape(1, POOL_WIDTH),
                bias.reshape(-1, N_REL), final.reshape(1, D_MODEL)]

    (g_win, d_win, m_win, v_win), (g_wout, d_wout, m_wout, v_wout) = _adamw_weights(
        (w_in[0], g_win, m_w_in[0], v_w_in[0]), (w_out[0], g_wout, m_w_out[0], v_w_out[0]))
    small_results = _adamw_small(
        g_small,
        small_views(norm_gain, pool_w, pool_scale, rel_bias, final_norm_gain),
        small_views(m_norm_gain, m_pool_w, m_pool_scale, m_rel_bias, m_final_norm_gain),
        small_views(v_norm_gain, v_pool_w, v_pool_scale, v_rel_bias, v_final_norm_gain))

    def full(win_part, wout_part, small_parts):
        norm, pool, scale, bias, final = small_parts
        return [norm.reshape(norm_gain.shape), win_part[None], pool.reshape(pool_w.shape), scale.reshape(pool_scale.shape),
                bias.reshape(rel_bias.shape), wout_part[None], final.reshape(final_norm_gain.shape)]

    grads = full(g_win, g_wout, small_results[0])
    deltas = full(d_win, d_wout, small_results[1])
    new_m = full(m_win, m_wout, small_results[2])
    new_v = full(v_win, v_wout, small_results[3])
    return (loss, gx[None], *grads, *deltas, *new_m, *new_v)
```
